```python
import jax, jax.numpy as jnp
from jax import lax
import numpy as np

D_MODEL = 2048
BATCH = 8
SEQ = 2048
DEPTH = 2

D_MIX = D_MODEL
D_LRU = D_MIX // 2
D_RWKV = D_MIX - D_LRU
LRU_HEADS = 4
LRU_BLOCK = D_LRU // LRU_HEADS
LRU_CONV = 4
LRU_C = 8.0
RWKV_HEAD = 64
RWKV_HEADS = D_RWKV // RWKV_HEAD
LORA_W = 64
LORA_A = 64
LORA_V = 32
LORA_G = 160
N_SHIFT = 3 * D_RWKV + LORA_W + LORA_A + LORA_G
D_IN = 2 * D_LRU + N_SHIFT
D_FF = 3 * D_MODEL
FFN_CONV = 3
D_PLE = 256
RMS_EPS = 1e-6
LNX_EPS = 64e-5

kernel_name = 'hybrid_rglru_rwkv7_parallel_heads'


def rmsnorm(x, g):
    xf = x.astype(jnp.float32)
    y = xf * lax.rsqrt(jnp.mean(xf * xf, axis=-1, keepdims=True) + RMS_EPS)
    return y.astype(x.dtype) * g


def causal_dwconv(x, w, b):
    K = w.shape[0]
    S = x.shape[1]
    xp = jnp.pad(x, ((0, 0), (K - 1, 0), (0, 0)))
    out = xp[:, K - 1:K - 1 + S] * w[K - 1] + b
    for j in range(K - 1):
        out = out + xp[:, j:j + S] * w[j]
    return out


def token_shift(z):
    return jnp.pad(z, ((0, 0), (1, 0), (0, 0)))[:, :-1]


def _linear_combine(left, right):
    a_l, b_l = left
    a_r, b_r = right
    return a_l * a_r, a_r * b_l + b_r


def rg_lru(xc, wx, bx, wa, ba, lam):
    B, S, _ = xc.shape
    xh = xc.reshape(B, S, LRU_HEADS, LRU_BLOCK)
    gate_x = jax.nn.sigmoid(jnp.einsum('bshi,hij->bshj', xh, wx).reshape(B, S, D_LRU) + bx)
    gate_a = jax.nn.sigmoid(jnp.einsum('bshi,hij->bshj', xh, wa).reshape(B, S, D_LRU) + ba)
    log_a = (-LRU_C * gate_a * jax.nn.softplus(-lam)).astype(jnp.float32)
    a = jnp.exp(log_a)
    mult = jnp.sqrt(1.0 - jnp.exp(2.0 * log_a))
    mult = jnp.where((jnp.arange(S) == 0)[None, :, None], 1.0, mult)
    b_in = (xc * gate_x).astype(jnp.float32) * mult
    _, h = lax.associative_scan(_linear_combine, (a, b_in), axis=1)
    return h.astype(xc.dtype)


def wkv7_scan(r, decay, k, v, kk, kka):
    B, S, H, N = r.shape

    def step(state, inp):
        r_t, w_t, k_t, v_t, kk_t, b_t = inp
        sa = jnp.einsum('bhvk,bhk->bhv', state, -kk_t)
        state = (state * w_t[:, :, None, :] + sa[..., None] * b_t[:, :, None, :]
                 + v_t[..., None] * k_t[:, :, None, :])
        y = jnp.einsum('bhvk,bhk->bhv', state, r_t)
        return state, y

    s0 = jnp.zeros((B, H, N, N), jnp.float32)
    xs = tuple(jnp.swapaxes(t, 0, 1) for t in (r, decay, k, v, kk, kka))
    _, ys = lax.scan(step, s0, xs)
    return jnp.swapaxes(ys, 0, 1)


def _fwd_setup_inputs(seed: int = 0) -> dict:
    key = jax.random.key(seed)
    ks = iter(jax.random.split(key, 64))
    f32 = jnp.float32
    L = DEPTH
    Lv = DEPTH - 1

    def nrm(shape, scale):
        return scale * jax.random.normal(next(ks), shape, f32)

    def gain(shape, base=1.0):
        return base + 0.02 * jax.random.normal(next(ks), shape, f32)

    x = jax.random.normal(next(ks), (BATCH, SEQ, D_MODEL), f32)
    p = jax.random.normal(next(ks), (DEPTH, BATCH, SEQ, D_PLE), f32)

    a_target = jax.random.uniform(next(ks), (L, D_LRU), f32, 0.9, 0.999)
    s = a_target ** (1.0 / LRU_C)
    lru_lambda = jnp.log(s) - jnp.log1p(-s)

    ratio = jnp.arange(D_RWKV, dtype=f32) / (D_RWKV - 1)
    rwkv_w0 = (-5.5 + 5.0 * ratio ** 0.9)[None, :] + nrm((L, D_RWKV), 0.1)

    return {
        'x': x,
        'p': p,
        'ln_mix': gain((L, D_MODEL)),
        'w_in': nrm((L, D_MODEL, D_IN), D_MODEL ** -0.5),
        'w_in_vres': nrm((Lv, D_MODEL, LORA_V), D_MODEL ** -0.5),
        'mu_shift': jax.random.uniform(next(ks), (L, N_SHIFT), f32),
        'mu_shift_vres': jax.random.uniform(next(ks), (Lv, LORA_V), f32),
        'conv_a_w': nrm((L, LRU_CONV, D_LRU), LRU_CONV ** -0.5),
        'conv_a_b': nrm((L, D_LRU), 0.02),
        'lru_wx': nrm((L, LRU_HEADS, LRU_BLOCK, LRU_BLOCK), LRU_BLOCK ** -0.5),
        'lru_bx': nrm((L, D_LRU), 0.02),
        'lru_wa': nrm((L, LRU_HEADS, LRU_BLOCK, LRU_BLOCK), LRU_BLOCK ** -0.5),
        'lru_ba': nrm((L, D_LRU), 0.02),
        'lru_lambda': lru_lambda,
        'lru_norm': gain((L, D_LRU)),
        'rwkv_w0': rwkv_w0,
        'rwkv_w2': nrm((L, LORA_W, D_RWKV), 0.5 * LORA_W ** -0.5),
        'rwkv_a0': nrm((L, D_RWKV), 0.1),
        'rwkv_a2': nrm((L, LORA_A, D_RWKV), LORA_A ** -0.5),
        'rwkv_v0': gain((Lv, D_RWKV)),
        'rwkv_v2': nrm((Lv, LORA_V, D_RWKV), LORA_V ** -0.5),
        'rwkv_g2': nrm((L, LORA_G, D_RWKV), LORA_G ** -0.5),
        'rwkv_kk': gain((L, D_RWKV), 0.85),
        'rwkv_ka': gain((L, D_RWKV)),
        'rwkv_rk': nrm((L, RWKV_HEADS, RWKV_HEAD), 0.1),
        'rwkv_lnx_w': gain((L, D_RWKV)),
        'rwkv_lnx_b': nrm((L, D_RWKV), 0.02),
        'w_o': nrm((L, D_MIX, D_MODEL), D_MIX ** -0.5),
        'ln_ffn': gain((L, D_MODEL)),
        'w_gate': nrm((L, D_MODEL, D_FF), D_MODEL ** -0.5),
        'w_up': nrm((L, D_MODEL, D_FF), D_MODEL ** -0.5),
        'conv_f_w': nrm((L, FFN_CONV, D_FF), FFN_CONV ** -0.5),
        'conv_f_b': nrm((L, D_FF), 0.02),
        'w_down': nrm((L, D_FF, D_MODEL), D_FF ** -0.5),
        'ln_ple': gain((L, D_MODEL)),
        'w_ple_gate': nrm((L, D_MODEL, D_MODEL), D_MODEL ** -0.5),
        'w_ple_proj': nrm((L, D_PLE, D_MODEL), D_PLE ** -0.5),
        'ln_ple_post': gain((L, D_MODEL)),
        'ln_final': gain((D_MODEL,)),
    }


def _fwd_reference(x, p, ln_mix, w_in, w_in_vres, mu_shift, mu_shift_vres, conv_a_w, conv_a_b,
              lru_wx, lru_bx, lru_wa, lru_ba, lru_lambda, lru_norm,
              rwkv_w0, rwkv_w2, rwkv_a0, rwkv_a2, rwkv_v0, rwkv_v2, rwkv_g2,
              rwkv_kk, rwkv_ka, rwkv_rk, rwkv_lnx_w, rwkv_lnx_b, w_o,
              ln_ffn, w_gate, w_up, conv_f_w, conv_f_b, w_down,
              ln_ple, w_ple_gate, w_ple_proj, ln_ple_post, ln_final):
    B, S, _ = x.shape
    H, N = RWKV_HEADS, RWKV_HEAD

    def heads(t):
        return t.reshape(B, S, H, N)

    h = x
    v_first = None
    for i in range(DEPTH):
        u = rmsnorm(h, ln_mix[i])
        if i == 0:
            w_cat, mu = w_in[0], mu_shift[0]
        else:
            w_cat = jnp.concatenate([w_in[i], w_in_vres[i - 1]], axis=1)
            mu = jnp.concatenate([mu_shift[i], mu_shift_vres[i - 1]], axis=0)
        z = u @ w_cat

        xb = causal_dwconv(z[..., :D_LRU], conv_a_w[i], conv_a_b[i])
        yb = jax.nn.gelu(z[..., D_LRU:2 * D_LRU])
        hl = rg_lru(xb, lru_wx[i], lru_bx[i], lru_wa[i], lru_ba[i], lru_lambda[i])
        out_a = rmsnorm(hl * yb, lru_norm[i])

        zr = z[..., 2 * D_LRU:]
        zr = zr + (token_shift(zr) - zr) * mu
        r = zr[..., :D_RWKV]
        k = zr[..., D_RWKV:2 * D_RWKV]
        v = zr[..., 2 * D_RWKV:3 * D_RWKV]
        o = 3 * D_RWKV
        wl = zr[..., o:o + LORA_W]
        o += LORA_W
        al = zr[..., o:o + LORA_A]
        o += LORA_A
        gl = zr[..., o:o + LORA_G]

        w_log = -jax.nn.softplus(-(rwkv_w0[i] + jnp.tanh(wl) @ rwkv_w2[i])) - 0.5
        decay = jnp.exp(-jnp.exp(w_log.astype(jnp.float32)))
        a = jax.nn.sigmoid(rwkv_a0[i] + al @ rwkv_a2[i])
        g = jax.nn.sigmoid(gl) @ rwkv_g2[i]
        if i == 0:
            v_first = v
        else:
            vl = zr[..., N_SHIFT:]
            v = v + (v_first - v) * jax.nn.sigmoid(rwkv_v0[i - 1] + vl @ rwkv_v2[i - 1])

        kk = heads((k * rwkv_kk[i]).astype(jnp.float32))
        kk = kk / jnp.maximum(jnp.sqrt(jnp.sum(kk * kk, axis=-1, keepdims=True)), 1e-12)
        k = k * (1.0 + (a - 1.0) * rwkv_ka[i])
        rh, kh, vh = heads(r), heads(k), heads(v)
        ah = heads(a.astype(jnp.float32))
        y = wkv7_scan(rh.astype(jnp.float32), heads(decay), kh.astype(jnp.float32),
                      vh.astype(jnp.float32), kk, kk * ah)
        mean = jnp.mean(y, axis=-1, keepdims=True)
        var = jnp.mean(jnp.square(y - mean), axis=-1, keepdims=True)
        yn = ((y - mean) * lax.rsqrt(var + LNX_EPS)).reshape(B, S, D_RWKV).astype(x.dtype)
        yn = yn * rwkv_lnx_w[i] + rwkv_lnx_b[i]
        bonus = (jnp.sum(rh * kh * rwkv_rk[i], axis=-1, keepdims=True) * vh).reshape(B, S, D_RWKV)
        out_b = (yn + bonus) * g

        h = h + jnp.concatenate([out_a, out_b], axis=-1) @ w_o[i]

        u = rmsnorm(h, ln_ffn[i])
        gate = causal_dwconv(u @ w_gate[i], conv_f_w[i], conv_f_b[i])
        h = h + (jax.nn.gelu(gate) * (u @ w_up[i])) @ w_down[i]

        u = rmsnorm(h, ln_ple[i])
        e = jax.nn.sigmoid(u @ w_ple_gate[i]) * (p[i] @ w_ple_proj[i])
        h = h + rmsnorm(e, ln_ple_post[i])

    return rmsnorm(h, ln_final)


import jax as _jax
import jax.numpy as _jnp

TWIN_FORMAT = 'train_step'
FWD_PARAMS = ['x', 'p', 'ln_mix', 'w_in', 'w_in_vres', 'mu_shift', 'mu_shift_vres', 'conv_a_w', 'conv_a_b', 'lru_wx', 'lru_bx', 'lru_wa', 'lru_ba', 'lru_lambda', 'lru_norm', 'rwkv_w0', 'rwkv_w2', 'rwkv_a0', 'rwkv_a2', 'rwkv_v0', 'rwkv_v2', 'rwkv_g2', 'rwkv_kk', 'rwkv_ka', 'rwkv_rk', 'rwkv_lnx_w', 'rwkv_lnx_b', 'w_o', 'ln_ffn', 'w_gate', 'w_up', 'conv_f_w', 'conv_f_b', 'w_down', 'ln_ple', 'w_ple_gate', 'w_ple_proj', 'ln_ple_post', 'ln_final']
TWIN_WEIGHTS = ['ln_mix', 'w_in', 'w_in_vres', 'mu_shift', 'mu_shift_vres', 'conv_a_w', 'conv_a_b', 'lru_wx', 'lru_bx', 'lru_wa', 'lru_ba', 'lru_lambda', 'lru_norm', 'rwkv_w0', 'rwkv_w2', 'rwkv_a0', 'rwkv_a2', 'rwkv_v0', 'rwkv_v2', 'rwkv_g2', 'rwkv_kk', 'rwkv_ka', 'rwkv_rk', 'rwkv_lnx_w', 'rwkv_lnx_b', 'w_o', 'ln_ffn', 'w_gate', 'w_up', 'conv_f_w', 'conv_f_b', 'w_down', 'ln_ple', 'w_ple_gate', 'w_ple_proj', 'ln_ple_post', 'ln_final']
TWIN_DIFF_INPUT = 'x'
TWIN_INPUTS = ['x', 'p', 'ln_mix', 'w_in', 'w_in_vres', 'mu_shift', 'mu_shift_vres', 'conv_a_w', 'conv_a_b', 'lru_wx', 'lru_bx', 'lru_wa', 'lru_ba', 'lru_lambda', 'lru_norm', 'rwkv_w0', 'rwkv_w2', 'rwkv_a0', 'rwkv_a2', 'rwkv_v0', 'rwkv_v2', 'rwkv_g2', 'rwkv_kk', 'rwkv_ka', 'rwkv_rk', 'rwkv_lnx_w', 'rwkv_lnx_b', 'w_o', 'ln_ffn', 'w_gate', 'w_up', 'conv_f_w', 'conv_f_b', 'w_down', 'ln_ple', 'w_ple_gate', 'w_ple_proj', 'ln_ple_post', 'ln_final', 'loss_target', 'm_ln_mix', 'm_w_in', 'm_w_in_vres', 'm_mu_shift', 'm_mu_shift_vres', 'm_conv_a_w', 'm_conv_a_b', 'm_lru_wx', 'm_lru_bx', 'm_lru_wa', 'm_lru_ba', 'm_lru_lambda', 'm_lru_norm', 'm_rwkv_w0', 'm_rwkv_w2', 'm_rwkv_a0', 'm_rwkv_a2', 'm_rwkv_v0', 'm_rwkv_v2', 'm_rwkv_g2', 'm_rwkv_kk', 'm_rwkv_ka', 'm_rwkv_rk', 'm_rwkv_lnx_w', 'm_rwkv_lnx_b', 'm_w_o', 'm_ln_ffn', 'm_w_gate', 'm_w_up', 'm_conv_f_w', 'm_conv_f_b', 'm_w_down', 'm_ln_ple', 'm_w_ple_gate', 'm_w_ple_proj', 'm_ln_ple_post', 'm_ln_final', 'v_ln_mix', 'v_w_in', 'v_w_in_vres', 'v_mu_shift', 'v_mu_shift_vres', 'v_conv_a_w', 'v_conv_a_b', 'v_lru_wx', 'v_lru_bx', 'v_lru_wa', 'v_lru_ba', 'v_lru_lambda', 'v_lru_norm', 'v_rwkv_w0', 'v_rwkv_w2', 'v_rwkv_a0', 'v_rwkv_a2', 'v_rwkv_v0', 'v_rwkv_v2', 'v_rwkv_g2', 'v_rwkv_kk', 'v_rwkv_ka', 'v_rwkv_rk', 'v_rwkv_lnx_w', 'v_rwkv_lnx_b', 'v_w_o', 'v_ln_ffn', 'v_w_gate', 'v_w_up', 'v_conv_f_w', 'v_conv_f_b', 'v_w_down', 'v_ln_ple', 'v_w_ple_gate', 'v_w_ple_proj', 'v_ln_ple_post', 'v_ln_final']
TWIN_OUTPUTS = ['loss', 'grad_x', 'grad_ln_mix', 'grad_w_in', 'grad_w_in_vres', 'grad_mu_shift', 'grad_mu_shift_vres', 'grad_conv_a_w', 'grad_conv_a_b', 'grad_lru_wx', 'grad_lru_bx', 'grad_lru_wa', 'grad_lru_ba', 'grad_lru_lambda', 'grad_lru_norm', 'grad_rwkv_w0', 'grad_rwkv_w2', 'grad_rwkv_a0', 'grad_rwkv_a2', 'grad_rwkv_v0', 'grad_rwkv_v2', 'grad_rwkv_g2', 'grad_rwkv_kk', 'grad_rwkv_ka', 'grad_rwkv_rk', 'grad_rwkv_lnx_w', 'grad_rwkv_lnx_b', 'grad_w_o', 'grad_ln_ffn', 'grad_w_gate', 'grad_w_up', 'grad_conv_f_w', 'grad_conv_f_b', 'grad_w_down', 'grad_ln_ple', 'grad_w_ple_gate', 'grad_w_ple_proj', 'grad_ln_ple_post', 'grad_ln_final', 'delta_ln_mix', 'delta_w_in', 'delta_w_in_vres', 'delta_mu_shift', 'delta_mu_shift_vres', 'delta_conv_a_w', 'delta_conv_a_b', 'delta_lru_wx', 'delta_lru_bx', 'delta_lru_wa', 'delta_lru_ba', 'delta_lru_lambda', 'delta_lru_norm', 'delta_rwkv_w0', 'delta_rwkv_w2', 'delta_rwkv_a0', 'delta_rwkv_a2', 'delta_rwkv_v0', 'delta_rwkv_v2', 'delta_rwkv_g2', 'delta_rwkv_kk', 'delta_rwkv_ka', 'delta_rwkv_rk', 'delta_rwkv_lnx_w', 'delta_rwkv_lnx_b', 'delta_w_o', 'delta_ln_ffn', 'delta_w_gate', 'delta_w_up', 'delta_conv_f_w', 'delta_conv_f_b', 'delta_w_down', 'delta_ln_ple', 'delta_w_ple_gate', 'delta_w_ple_proj', 'delta_ln_ple_post', 'delta_ln_final', 'new_m_ln_mix', 'new_m_w_in', 'new_m_w_in_vres', 'new_m_mu_shift', 'new_m_mu_shift_vres', 'new_m_conv_a_w', 'new_m_conv_a_b', 'new_m_lru_wx', 'new_m_lru_bx', 'new_m_lru_wa', 'new_m_lru_ba', 'new_m_lru_lambda', 'new_m_lru_norm', 'new_m_rwkv_w0', 'new_m_rwkv_w2', 'new_m_rwkv_a0', 'new_m_rwkv_a2', 'new_m_rwkv_v0', 'new_m_rwkv_v2', 'new_m_rwkv_g2', 'new_m_rwkv_kk', 'new_m_rwkv_ka', 'new_m_rwkv_rk', 'new_m_rwkv_lnx_w', 'new_m_rwkv_lnx_b', 'new_m_w_o', 'new_m_ln_ffn', 'new_m_w_gate', 'new_m_w_up', 'new_m_conv_f_w', 'new_m_conv_f_b', 'new_m_w_down', 'new_m_ln_ple', 'new_m_w_ple_gate', 'new_m_w_ple_proj', 'new_m_ln_ple_post', 'new_m_ln_final', 'new_v_ln_mix', 'new_v_w_in', 'new_v_w_in_vres', 'new_v_mu_shift', 'new_v_mu_shift_vres', 'new_v_conv_a_w', 'new_v_conv_a_b', 'new_v_lru_wx', 'new_v_lru_bx', 'new_v_lru_wa', 'new_v_lru_ba', 'new_v_lru_lambda', 'new_v_lru_norm', 'new_v_rwkv_w0', 'new_v_rwkv_w2', 'new_v_rwkv_a0', 'new_v_rwkv_a2', 'new_v_rwkv_v0', 'new_v_rwkv_v2', 'new_v_rwkv_g2', 'new_v_rwkv_kk', 'new_v_rwkv_ka', 'new_v_rwkv_rk', 'new_v_rwkv_lnx_w', 'new_v_rwkv_lnx_b', 'new_v_w_o', 'new_v_ln_ffn', 'new_v_w_gate', 'new_v_w_up', 'new_v_conv_f_w', 'new_v_conv_f_b', 'new_v_w_down', 'new_v_ln_ple', 'new_v_w_ple_gate', 'new_v_w_ple_proj', 'new_v_ln_ple_post', 'new_v_ln_final']
TWIN_LEAF_KINDS = {'loss': 'loss', 'grad_x': 'grad_x', 'grad_ln_mix': 'grad_w', 'grad_w_in': 'grad_w', 'grad_w_in_vres': 'grad_w', 'grad_mu_shift': 'grad_w', 'grad_mu_shift_vres': 'grad_w', 'grad_conv_a_w': 'grad_w', 'grad_conv_a_b': 'grad_w', 'grad_lru_wx': 'grad_w', 'grad_lru_bx': 'grad_w', 'grad_lru_wa': 'grad_w', 'grad_lru_ba': 'grad_w', 'grad_lru_lambda': 'grad_w', 'grad_lru_norm': 'grad_w', 'grad_rwkv_w0': 'grad_w', 'grad_rwkv_w2': 'grad_w', 'grad_rwkv_a0': 'grad_w', 'grad_rwkv_a2': 'grad_w', 'grad_rwkv_v0': 'grad_w', 'grad_rwkv_v2': 'grad_w', 'grad_rwkv_g2': 'grad_w', 'grad_rwkv_kk': 'grad_w', 'grad_rwkv_ka': 'grad_w', 'grad_rwkv_rk': 'grad_w', 'grad_rwkv_lnx_w': 'grad_w', 'grad_rwkv_lnx_b': 'grad_w', 'grad_w_o': 'grad_w', 'grad_ln_ffn': 'grad_w', 'grad_w_gate': 'grad_w', 'grad_w_up': 'grad_w', 'grad_conv_f_w': 'grad_w', 'grad_conv_f_b': 'grad_w', 'grad_w_down': 'grad_w', 'grad_ln_ple': 'grad_w', 'grad_w_ple_gate': 'grad_w', 'grad_w_ple_proj': 'grad_w', 'grad_ln_ple_post': 'grad_w', 'grad_ln_final': 'grad_w', 'delta_ln_mix': 'delta_w', 'delta_w_in': 'delta_w', 'delta_w_in_vres': 'delta_w', 'delta_mu_shift': 'delta_w', 'delta_mu_shift_vres': 'delta_w', 'delta_conv_a_w': 'delta_w', 'delta_conv_a_b': 'delta_w', 'delta_lru_wx': 'delta_w', 'delta_lru_bx': 'delta_w', 'delta_lru_wa': 'delta_w', 'delta_lru_ba': 'delta_w', 'delta_lru_lambda': 'delta_w', 'delta_lru_norm': 'delta_w', 'delta_rwkv_w0': 'delta_w', 'delta_rwkv_w2': 'delta_w', 'delta_rwkv_a0': 'delta_w', 'delta_rwkv_a2': 'delta_w', 'delta_rwkv_v0': 'delta_w', 'delta_rwkv_v2': 'delta_w', 'delta_rwkv_g2': 'delta_w', 'delta_rwkv_kk': 'delta_w', 'delta_rwkv_ka': 'delta_w', 'delta_rwkv_rk': 'delta_w', 'delta_rwkv_lnx_w': 'delta_w', 'delta_rwkv_lnx_b': 'delta_w', 'delta_w_o': 'delta_w', 'delta_ln_ffn': 'delta_w', 'delta_w_gate': 'delta_w', 'delta_w_up': 'delta_w', 'delta_conv_f_w': 'delta_w', 'delta_conv_f_b': 'delta_w', 'delta_w_down': 'delta_w', 'delta_ln_ple': 'delta_w', 'delta_w_ple_gate': 'delta_w', 'delta_w_ple_proj': 'delta_w', 'delta_ln_ple_post': 'delta_w', 'delta_ln_final': 'delta_w', 'new_m_ln_mix': 'new_m', 'new_m_w_in': 'new_m', 'new_m_w_in_vres': 'new_m', 'new_m_mu_shift': 'new_m', 'new_m_mu_shift_vres': 'new_m', 'new_m_conv_a_w': 'new_m', 'new_m_conv_a_b': 'new_m', 'new_m_lru_wx': 'new_m', 'new_m_lru_bx': 'new_m', 'new_m_lru_wa': 'new_m', 'new_m_lru_ba': 'new_m', 'new_m_lru_lambda': 'new_m', 'new_m_lru_norm': 'new_m', 'new_m_rwkv_w0': 'new_m', 'new_m_rwkv_w2': 'new_m', 'new_m_rwkv_a0': 'new_m', 'new_m_rwkv_a2': 'new_m', 'new_m_rwkv_v0': 'new_m', 'new_m_rwkv_v2': 'new_m', 'new_m_rwkv_g2': 'new_m', 'new_m_rwkv_kk': 'new_m', 'new_m_rwkv_ka': 'new_m', 'new_m_rwkv_rk': 'new_m', 'new_m_rwkv_lnx_w': 'new_m', 'new_m_rwkv_lnx_b': 'new_m', 'new_m_w_o': 'new_m', 'new_m_ln_ffn': 'new_m', 'new_m_w_gate': 'new_m', 'new_m_w_up': 'new_m', 'new_m_conv_f_w': 'new_m', 'new_m_conv_f_b': 'new_m', 'new_m_w_down': 'new_m', 'new_m_ln_ple': 'new_m', 'new_m_w_ple_gate': 'new_m', 'new_m_w_ple_proj': 'new_m', 'new_m_ln_ple_post': 'new_m', 'new_m_ln_final': 'new_m', 'new_v_ln_mix': 'new_v', 'new_v_w_in': 'new_v', 'new_v_w_in_vres': 'new_v', 'new_v_mu_shift': 'new_v', 'new_v_mu_shift_vres': 'new_v', 'new_v_conv_a_w': 'new_v', 'new_v_conv_a_b': 'new_v', 'new_v_lru_wx': 'new_v', 'new_v_lru_bx': 'new_v', 'new_v_lru_wa': 'new_v', 'new_v_lru_ba': 'new_v', 'new_v_lru_lambda': 'new_v', 'new_v_lru_norm': 'new_v', 'new_v_rwkv_w0': 'new_v', 'new_v_rwkv_w2': 'new_v', 'new_v_rwkv_a0': 'new_v', 'new_v_rwkv_a2': 'new_v', 'new_v_rwkv_v0': 'new_v', 'new_v_rwkv_v2': 'new_v', 'new_v_rwkv_g2': 'new_v', 'new_v_rwkv_kk': 'new_v', 'new_v_rwkv_ka': 'new_v', 'new_v_rwkv_rk': 'new_v', 'new_v_rwkv_lnx_w': 'new_v', 'new_v_rwkv_lnx_b': 'new_v', 'new_v_w_o': 'new_v', 'new_v_ln_ffn': 'new_v', 'new_v_w_gate': 'new_v', 'new_v_w_up': 'new_v', 'new_v_conv_f_w': 'new_v', 'new_v_conv_f_b': 'new_v', 'new_v_w_down': 'new_v', 'new_v_ln_ple': 'new_v', 'new_v_w_ple_gate': 'new_v', 'new_v_w_ple_proj': 'new_v', 'new_v_ln_ple_post': 'new_v', 'new_v_ln_final': 'new_v'}


def _forward(args):
    return _fwd_reference(*[args[k] for k in FWD_PARAMS])


def _output_shape():
    out = _jax.eval_shape(lambda: _forward(_fwd_setup_inputs(0)))
    return out.shape, out.dtype

N_MICROBATCH = 1
ADAM_LR = 0.001
ADAM_B1 = 0.9
ADAM_B2 = 0.999
ADAM_EPS = 1e-08
ADAM_WD = 0.01
ADAM_STEP = 10
PER_EXAMPLE_BATCH_AXIS = {'x': 0, 'p': 1, 'loss_target': 0}
SHARED_INPUTS = []
_WEIGHT_DTYPES = {'ln_mix': _jnp.float32, 'w_in': _jnp.float32, 'w_in_vres': _jnp.float32, 'mu_shift': _jnp.float32, 'mu_shift_vres': _jnp.float32, 'conv_a_w': _jnp.float32, 'conv_a_b': _jnp.float32, 'lru_wx': _jnp.float32, 'lru_bx': _jnp.float32, 'lru_wa': _jnp.float32, 'lru_ba': _jnp.float32, 'lru_lambda': _jnp.float32, 'lru_norm': _jnp.float32, 'rwkv_w0': _jnp.float32, 'rwkv_w2': _jnp.float32, 'rwkv_a0': _jnp.float32, 'rwkv_a2': _jnp.float32, 'rwkv_v0': _jnp.float32, 'rwkv_v2': _jnp.float32, 'rwkv_g2': _jnp.float32, 'rwkv_kk': _jnp.float32, 'rwkv_ka': _jnp.float32, 'rwkv_rk': _jnp.float32, 'rwkv_lnx_w': _jnp.float32, 'rwkv_lnx_b': _jnp.float32, 'w_o': _jnp.float32, 'ln_ffn': _jnp.float32, 'w_gate': _jnp.float32, 'w_up': _jnp.float32, 'conv_f_w': _jnp.float32, 'conv_f_b': _jnp.float32, 'w_down': _jnp.float32, 'ln_ple': _jnp.float32, 'w_ple_gate': _jnp.float32, 'w_ple_proj': _jnp.float32, 'ln_ple_post': _jnp.float32, 'ln_final': _jnp.float32}
MOMENT_SCALE = {'ln_mix': 5.368053e-02, 'w_in': 3.322456e-02, 'w_in_vres': 2.591027e-02, 'mu_shift': 4.012658e-02, 'mu_shift_vres': 3.858756e-02, 'conv_a_w': 4.595786e-02, 'conv_a_b': 5.542973e-01, 'lru_wx': 2.205313e-02, 'lru_bx': 1.625513e-02, 'lru_wa': 1.239282e-02, 'lru_ba': 1.100727e-02, 'lru_lambda': 2.198889e-02, 'lru_norm': 4.396059e-02, 'rwkv_w0': 1.064969e-02, 'rwkv_w2': 2.048840e-03, 'rwkv_a0': 1.037724e-02, 'rwkv_a2': 8.365965e-03, 'rwkv_v0': 5.867270e-03, 'rwkv_v2': 4.606492e-03, 'rwkv_g2': 2.209656e-02, 'rwkv_kk': 1.702272e-02, 'rwkv_ka': 2.661016e-02, 'rwkv_rk': 4.765796e-02, 'rwkv_lnx_w': 2.165758e-02, 'rwkv_lnx_b': 3.917118e-02, 'w_o': 3.452912e-02, 'ln_ffn': 3.201849e-02, 'w_gate': 1.325671e-02, 'w_up': 1.292734e-02, 'conv_f_w': 1.333277e-02, 'conv_f_b': 1.294079e-02, 'w_down': 2.240065e-02, 'ln_ple': 1.317568e-02, 'w_ple_gate': 1.276497e-02, 'w_ple_proj': 3.271467e-02, 'ln_ple_post': 5.556234e-02, 'ln_final': 8.005870e+00}


def _to_microbatches(a, axis):
    t = _jnp.moveaxis(a, axis, 0)
    t = t.reshape((N_MICROBATCH, t.shape[0] // N_MICROBATCH) + t.shape[1:])
    return _jnp.moveaxis(t, 1, axis + 1)


def setup_inputs(seed: int = 0) -> dict:
    inp = _fwd_setup_inputs(seed)
    key = _jax.random.fold_in(_jax.random.key(seed), 7919)
    shape, _ = _output_shape()
    out = dict(inp)
    out["loss_target"] = _jax.random.normal(_jax.random.fold_in(key, 0), shape, _jnp.float32)
    for i, name in enumerate(TWIN_WEIGHTS):
        w = inp[name].astype(_jnp.float32)
        if MOMENT_SCALE is None:
            s = _jnp.sqrt(_jnp.mean(_jnp.square(w)) + 1e-30)
        else:
            s = MOMENT_SCALE[name]
        km, kv = _jax.random.split(_jax.random.fold_in(key, i + 1))
        out[name] = w
        out["m_" + name] = s * _jax.random.normal(km, w.shape, _jnp.float32)
        out["v_" + name] = (s * s) * _jax.random.uniform(kv, w.shape, _jnp.float32, 0.5, 1.5)
    if N_MICROBATCH > 1:
        for name, axis in PER_EXAMPLE_BATCH_AXIS.items():
            out[name] = _to_microbatches(out[name], axis)
    return {'x': out['x'], 'p': out['p'], 'ln_mix': out['ln_mix'], 'w_in': out['w_in'], 'w_in_vres': out['w_in_vres'], 'mu_shift': out['mu_shift'], 'mu_shift_vres': out['mu_shift_vres'], 'conv_a_w': out['conv_a_w'], 'conv_a_b': out['conv_a_b'], 'lru_wx': out['lru_wx'], 'lru_bx': out['lru_bx'], 'lru_wa': out['lru_wa'], 'lru_ba': out['lru_ba'], 'lru_lambda': out['lru_lambda'], 'lru_norm': out['lru_norm'], 'rwkv_w0': out['rwkv_w0'], 'rwkv_w2': out['rwkv_w2'], 'rwkv_a0': out['rwkv_a0'], 'rwkv_a2': out['rwkv_a2'], 'rwkv_v0': out['rwkv_v0'], 'rwkv_v2': out['rwkv_v2'], 'rwkv_g2': out['rwkv_g2'], 'rwkv_kk': out['rwkv_kk'], 'rwkv_ka': out['rwkv_ka'], 'rwkv_rk': out['rwkv_rk'], 'rwkv_lnx_w': out['rwkv_lnx_w'], 'rwkv_lnx_b': out['rwkv_lnx_b'], 'w_o': out['w_o'], 'ln_ffn': out['ln_ffn'], 'w_gate': out['w_gate'], 'w_up': out['w_up'], 'conv_f_w': out['conv_f_w'], 'conv_f_b': out['conv_f_b'], 'w_down': out['w_down'], 'ln_ple': out['ln_ple'], 'w_ple_gate': out['w_ple_gate'], 'w_ple_proj': out['w_ple_proj'], 'ln_ple_post': out['ln_ple_post'], 'ln_final': out['ln_final'], 'loss_target': out['loss_target'], 'm_ln_mix': out['m_ln_mix'], 'm_w_in': out['m_w_in'], 'm_w_in_vres': out['m_w_in_vres'], 'm_mu_shift': out['m_mu_shift'], 'm_mu_shift_vres': out['m_mu_shift_vres'], 'm_conv_a_w': out['m_conv_a_w'], 'm_conv_a_b': out['m_conv_a_b'], 'm_lru_wx': out['m_lru_wx'], 'm_lru_bx': out['m_lru_bx'], 'm_lru_wa': out['m_lru_wa'], 'm_lru_ba': out['m_lru_ba'], 'm_lru_lambda': out['m_lru_lambda'], 'm_lru_norm': out['m_lru_norm'], 'm_rwkv_w0': out['m_rwkv_w0'], 'm_rwkv_w2': out['m_rwkv_w2'], 'm_rwkv_a0': out['m_rwkv_a0'], 'm_rwkv_a2': out['m_rwkv_a2'], 'm_rwkv_v0': out['m_rwkv_v0'], 'm_rwkv_v2': out['m_rwkv_v2'], 'm_rwkv_g2': out['m_rwkv_g2'], 'm_rwkv_kk': out['m_rwkv_kk'], 'm_rwkv_ka': out['m_rwkv_ka'], 'm_rwkv_rk': out['m_rwkv_rk'], 'm_rwkv_lnx_w': out['m_rwkv_lnx_w'], 'm_rwkv_lnx_b': out['m_rwkv_lnx_b'], 'm_w_o': out['m_w_o'], 'm_ln_ffn': out['m_ln_ffn'], 'm_w_gate': out['m_w_gate'], 'm_w_up': out['m_w_up'], 'm_conv_f_w': out['m_conv_f_w'], 'm_conv_f_b': out['m_conv_f_b'], 'm_w_down': out['m_w_down'], 'm_ln_ple': out['m_ln_ple'], 'm_w_ple_gate': out['m_w_ple_gate'], 'm_w_ple_proj': out['m_w_ple_proj'], 'm_ln_ple_post': out['m_ln_ple_post'], 'm_ln_final': out['m_ln_final'], 'v_ln_mix': out['v_ln_mix'], 'v_w_in': out['v_w_in'], 'v_w_in_vres': out['v_w_in_vres'], 'v_mu_shift': out['v_mu_shift'], 'v_mu_shift_vres': out['v_mu_shift_vres'], 'v_conv_a_w': out['v_conv_a_w'], 'v_conv_a_b': out['v_conv_a_b'], 'v_lru_wx': out['v_lru_wx'], 'v_lru_bx': out['v_lru_bx'], 'v_lru_wa': out['v_lru_wa'], 'v_lru_ba': out['v_lru_ba'], 'v_lru_lambda': out['v_lru_lambda'], 'v_lru_norm': out['v_lru_norm'], 'v_rwkv_w0': out['v_rwkv_w0'], 'v_rwkv_w2': out['v_rwkv_w2'], 'v_rwkv_a0': out['v_rwkv_a0'], 'v_rwkv_a2': out['v_rwkv_a2'], 'v_rwkv_v0': out['v_rwkv_v0'], 'v_rwkv_v2': out['v_rwkv_v2'], 'v_rwkv_g2': out['v_rwkv_g2'], 'v_rwkv_kk': out['v_rwkv_kk'], 'v_rwkv_ka': out['v_rwkv_ka'], 'v_rwkv_rk': out['v_rwkv_rk'], 'v_rwkv_lnx_w': out['v_rwkv_lnx_w'], 'v_rwkv_lnx_b': out['v_rwkv_lnx_b'], 'v_w_o': out['v_w_o'], 'v_ln_ffn': out['v_ln_ffn'], 'v_w_gate': out['v_w_gate'], 'v_w_up': out['v_w_up'], 'v_conv_f_w': out['v_conv_f_w'], 'v_conv_f_b': out['v_conv_f_b'], 'v_w_down': out['v_w_down'], 'v_ln_ple': out['v_ln_ple'], 'v_w_ple_gate': out['v_w_ple_gate'], 'v_w_ple_proj': out['v_w_ple_proj'], 'v_ln_ple_post': out['v_ln_ple_post'], 'v_ln_final': out['v_ln_final']}


def _loss(weights, diff, rest, loss_target):
    with _jax.named_scope("forward"):
        args = {**rest, TWIN_DIFF_INPUT: diff, **{k: w.astype(_WEIGHT_DTYPES[k]) for k, w in weights.items()}}
        y = _forward(args)
    with _jax.named_scope("loss_head"):
        err = _jnp.square(y.astype(_jnp.float32) - loss_target)
        return 0.5 * _jnp.sum(_jnp.mean(err, axis=-1)) if err.ndim else 0.5 * err


def _adamw(w, g, m, v):
    m = ADAM_B1 * m + (1.0 - ADAM_B1) * g
    v = ADAM_B2 * v + (1.0 - ADAM_B2) * _jnp.square(g)
    m_hat = m / (1.0 - ADAM_B1 ** ADAM_STEP)
    v_hat = v / (1.0 - ADAM_B2 ** ADAM_STEP)
    delta = -ADAM_LR * (m_hat / (_jnp.sqrt(v_hat) + ADAM_EPS) + ADAM_WD * w)
    return delta, m, v


def reference(x, p, ln_mix, w_in, w_in_vres, mu_shift, mu_shift_vres, conv_a_w, conv_a_b, lru_wx, lru_bx, lru_wa, lru_ba, lru_lambda, lru_norm, rwkv_w0, rwkv_w2, rwkv_a0, rwkv_a2, rwkv_v0, rwkv_v2, rwkv_g2, rwkv_kk, rwkv_ka, rwkv_rk, rwkv_lnx_w, rwkv_lnx_b, w_o, ln_ffn, w_gate, w_up, conv_f_w, conv_f_b, w_down, ln_ple, w_ple_gate, w_ple_proj, ln_ple_post, ln_final, loss_target, m_ln_mix, m_w_in, m_w_in_vres, m_mu_shift, m_mu_shift_vres, m_conv_a_w, m_conv_a_b, m_lru_wx, m_lru_bx, m_lru_wa, m_lru_ba, m_lru_lambda, m_lru_norm, m_rwkv_w0, m_rwkv_w2, m_rwkv_a0, m_rwkv_a2, m_rwkv_v0, m_rwkv_v2, m_rwkv_g2, m_rwkv_kk, m_rwkv_ka, m_rwkv_rk, m_rwkv_lnx_w, m_rwkv_lnx_b, m_w_o, m_ln_ffn, m_w_gate, m_w_up, m_conv_f_w, m_conv_f_b, m_w_down, m_ln_ple, m_w_ple_gate, m_w_ple_proj, m_ln_ple_post, m_ln_final, v_ln_mix, v_w_in, v_w_in_vres, v_mu_shift, v_mu_shift_vres, v_conv_a_w, v_conv_a_b, v_lru_wx, v_lru_bx, v_lru_wa, v_lru_ba, v_lru_lambda, v_lru_norm, v_rwkv_w0, v_rwkv_w2, v_rwkv_a0, v_rwkv_a2, v_rwkv_v0, v_rwkv_v2, v_rwkv_g2, v_rwkv_kk, v_rwkv_ka, v_rwkv_rk, v_rwkv_lnx_w, v_rwkv_lnx_b, v_w_o, v_ln_ffn, v_w_gate, v_w_up, v_conv_f_w, v_conv_f_b, v_w_down, v_ln_ple, v_w_ple_gate, v_w_ple_proj, v_ln_ple_post, v_ln_final):
    given = dict(x=x, p=p, ln_mix=ln_mix, w_in=w_in, w_in_vres=w_in_vres, mu_shift=mu_shift, mu_shift_vres=mu_shift_vres, conv_a_w=conv_a_w, conv_a_b=conv_a_b, lru_wx=lru_wx, lru_bx=lru_bx, lru_wa=lru_wa, lru_ba=lru_ba, lru_lambda=lru_lambda, lru_norm=lru_norm, rwkv_w0=rwkv_w0, rwkv_w2=rwkv_w2, rwkv_a0=rwkv_a0, rwkv_a2=rwkv_a2, rwkv_v0=rwkv_v0, rwkv_v2=rwkv_v2, rwkv_g2=rwkv_g2, rwkv_kk=rwkv_kk, rwkv_ka=rwkv_ka, rwkv_rk=rwkv_rk, rwkv_lnx_w=rwkv_lnx_w, rwkv_lnx_b=rwkv_lnx_b, w_o=w_o, ln_ffn=ln_ffn, w_gate=w_gate, w_up=w_up, conv_f_w=conv_f_w, conv_f_b=conv_f_b, w_down=w_down, ln_ple=ln_ple, w_ple_gate=w_ple_gate, w_ple_proj=w_ple_proj, ln_ple_post=ln_ple_post, ln_final=ln_final, loss_target=loss_target, m_ln_mix=m_ln_mix, m_w_in=m_w_in, m_w_in_vres=m_w_in_vres, m_mu_shift=m_mu_shift, m_mu_shift_vres=m_mu_shift_vres, m_conv_a_w=m_conv_a_w, m_conv_a_b=m_conv_a_b, m_lru_wx=m_lru_wx, m_lru_bx=m_lru_bx, m_lru_wa=m_lru_wa, m_lru_ba=m_lru_ba, m_lru_lambda=m_lru_lambda, m_lru_norm=m_lru_norm, m_rwkv_w0=m_rwkv_w0, m_rwkv_w2=m_rwkv_w2, m_rwkv_a0=m_rwkv_a0, m_rwkv_a2=m_rwkv_a2, m_rwkv_v0=m_rwkv_v0, m_rwkv_v2=m_rwkv_v2, m_rwkv_g2=m_rwkv_g2, m_rwkv_kk=m_rwkv_kk, m_rwkv_ka=m_rwkv_ka, m_rwkv_rk=m_rwkv_rk, m_rwkv_lnx_w=m_rwkv_lnx_w, m_rwkv_lnx_b=m_rwkv_lnx_b, m_w_o=m_w_o, m_ln_ffn=m_ln_ffn, m_w_gate=m_w_gate, m_w_up=m_w_up, m_conv_f_w=m_conv_f_w, m_conv_f_b=m_conv_f_b, m_w_down=m_w_down, m_ln_ple=m_ln_ple, m_w_ple_gate=m_w_ple_gate, m_w_ple_proj=m_w_ple_proj, m_ln_ple_post=m_ln_ple_post, m_ln_final=m_ln_final, v_ln_mix=v_ln_mix, v_w_in=v_w_in, v_w_in_vres=v_w_in_vres, v_mu_shift=v_mu_shift, v_mu_shift_vres=v_mu_shift_vres, v_conv_a_w=v_conv_a_w, v_conv_a_b=v_conv_a_b, v_lru_wx=v_lru_wx, v_lru_bx=v_lru_bx, v_lru_wa=v_lru_wa, v_lru_ba=v_lru_ba, v_lru_lambda=v_lru_lambda, v_lru_norm=v_lru_norm, v_rwkv_w0=v_rwkv_w0, v_rwkv_w2=v_rwkv_w2, v_rwkv_a0=v_rwkv_a0, v_rwkv_a2=v_rwkv_a2, v_rwkv_v0=v_rwkv_v0, v_rwkv_v2=v_rwkv_v2, v_rwkv_g2=v_rwkv_g2, v_rwkv_kk=v_rwkv_kk, v_rwkv_ka=v_rwkv_ka, v_rwkv_rk=v_rwkv_rk, v_rwkv_lnx_w=v_rwkv_lnx_w, v_rwkv_lnx_b=v_rwkv_lnx_b, v_w_o=v_w_o, v_ln_ffn=v_ln_ffn, v_w_gate=v_w_gate, v_w_up=v_w_up, v_conv_f_w=v_conv_f_w, v_conv_f_b=v_conv_f_b, v_w_down=v_w_down, v_ln_ple=v_ln_ple, v_w_ple_gate=v_w_ple_gate, v_w_ple_proj=v_w_ple_proj, v_ln_ple_post=v_ln_ple_post, v_ln_final=v_ln_final)
    weights = {n: given[n] for n in TWIN_WEIGHTS}
    shared = {n: given[n] for n in SHARED_INPUTS}
    per_example = {n: given[n] for n in ['x', 'p']}
    grad_fn = _jax.value_and_grad(_loss, argnums=(0, 1))

    def one_microbatch(ex, loss_target):
        ex = dict(ex)
        diff = ex.pop(TWIN_DIFF_INPUT)
        return grad_fn(weights, diff, {**shared, **ex}, loss_target)

    if N_MICROBATCH == 1:
        loss, (grad_w, grad_x) = one_microbatch(per_example, given["loss_target"])
    else:
        def body(carry, xs):
            loss_sum, grad_sum = carry
            l_k, (gw_k, gx_k) = one_microbatch(xs[0], xs[1])
            with _jax.named_scope("update"):
                return (loss_sum + l_k, _jax.tree.map(_jnp.add, grad_sum, gw_k)), gx_k

        init = (_jnp.zeros((), _jnp.float32), _jax.tree.map(_jnp.zeros_like, weights))
        (loss, grad_w), grad_x = _jax.lax.scan(body, init, (per_example, given["loss_target"]))
    with _jax.named_scope("update"):
        delta_w, new_m, new_v = {}, {}, {}
        for n in TWIN_WEIGHTS:
            delta_w[n], new_m[n], new_v[n] = _adamw(weights[n], grad_w[n], given["m_" + n], given["v_" + n])
    return (loss, grad_x, *[grad_w[n] for n in TWIN_WEIGHTS], *[delta_w[n] for n in TWIN_WEIGHTS],
            *[new_m[n] for n in TWIN_WEIGHTS], *[new_v[n] for n in TWIN_WEIGHTS])
```

```python
import functools

import jax
import jax.numpy as jnp
from jax import lax
from jax.experimental import pallas as pl
from jax.experimental.pallas import tpu as pltpu

F32 = jnp.float32
BF16 = jnp.bfloat16
HIGHEST = lax.Precision.HIGHEST
MESH = pl.DeviceIdType.MESH

RMS_EPS = 1e-6
LNX_EPS = 64e-5
LRU_C = 8.0
ADAM_LR = 0.001
ADAM_B1 = 0.9
ADAM_B2 = 0.999
ADAM_EPS = 1e-08
ADAM_WD = 0.01
ADAM_STEP = 10

LANES_V7X = 128
VMEM_LIMIT_V7X = 60 * 1024 * 1024
WKV_CHUNK = 16
N_XY = 4
N_DEV = 8


def _cparams(sem=None, **kw):
    if sem is not None:
        kw["dimension_semantics"] = sem
    return pltpu.CompilerParams(vmem_limit_bytes=VMEM_LIMIT_V7X, **kw)


def _tile(dim, prefs):
    for t in prefs:
        if dim % t == 0:
            return t
    return dim


def _round_up(n, m):
    return (n + m - 1) // m * m


def _mm(a, b, *, ta=False, tb=False, res=None, out_dtype=F32, name):
    if ta:
        kdim, m = a.shape
    else:
        m, kdim = a.shape
    n = b.shape[0] if tb else b.shape[1]
    assert (b.shape[1] if tb else b.shape[0]) == kdim
    tm = _tile(m, (2048, 1024, 512, 256, 128))
    tn = _tile(n, (512, 256, 128))
    tk = _tile(kdim, (1024, 512, 256, 128))
    nk = kdim // tk
    a_spec = pl.BlockSpec((tk, tm), lambda i, j, k: (k, i)) if ta else pl.BlockSpec((tm, tk), lambda i, j, k: (i, k))
    b_spec = pl.BlockSpec((tn, tk), lambda i, j, k: (j, k)) if tb else pl.BlockSpec((tk, tn), lambda i, j, k: (k, j))
    o_spec = pl.BlockSpec((tm, tn), lambda i, j, k: (i, j))
    dn = (((0 if ta else 1,), (1 if tb else 0,)), ((), ()))
    has_res = res is not None

    def body(*refs):
        if has_res:
            a_ref, b_ref, r_ref, o_ref, acc_ref = refs
        else:
            a_ref, b_ref, o_ref, acc_ref = refs
        k = pl.program_id(2)

        @pl.when(k == 0)
        def _():
            acc_ref[...] = jnp.zeros_like(acc_ref)

        acc_ref[...] += lax.dot_general(a_ref[...], b_ref[...], dn, preferred_element_type=F32)

        @pl.when(k == nk - 1)
        def _():
            acc = acc_ref[...]
            if has_res:
                acc = acc + r_ref[...].astype(F32)
            o_ref[...] = acc.astype(out_dtype)

    ins = [a, b] + ([res] if has_res else [])
    in_specs = [a_spec, b_spec] + ([o_spec] if has_res else [])
    return pl.pallas_call(
        body, name=name, grid=(m // tm, n // tn, nk), in_specs=in_specs, out_specs=o_spec,
        out_shape=jax.ShapeDtypeStruct((m, n), out_dtype), scratch_shapes=[pltpu.VMEM((tm, tn), F32)],
        compiler_params=_cparams(("parallel", "parallel", "arbitrary")),
    )(*ins)


def _stage_specs(axis, tile, tiled, params, consts, rows):
    specs = []
    for arr, width, cblk in tiled:
        if axis == 0:
            specs.append(pl.BlockSpec((tile, width), functools.partial(lambda i, c: (i, c), c=cblk)))
        else:
            specs.append(pl.BlockSpec((rows, tile), functools.partial(lambda i, c: (0, i + c), c=cblk)))
    for arr, cblk in params:
        if axis == 0:
            specs.append(pl.BlockSpec(arr.shape, functools.partial(lambda i, nd: (0,) * nd, nd=arr.ndim)))
        else:
            specs.append(pl.BlockSpec((arr.shape[0], tile), functools.partial(lambda i, c: (0, i + c), c=cblk)))
    for arr in consts:
        specs.append(pl.BlockSpec(arr.shape, functools.partial(lambda i, nd: (0,) * nd, nd=arr.ndim)))
    return specs


def _stage_fwd(fn, tiled, params, consts, outs, *, axis, tile, rows, name):
    nt, npar, nc = len(tiled), len(params), len(consts)
    ntiles = (rows // tile) if axis == 0 else (outs[0][0] // tile)

    def body(*refs):
        ins = refs[: nt + npar + nc]
        orefs = refs[nt + npar + nc:]
        vals = [r[...].astype(F32) for r in ins[: nt + npar]] + [r[...] for r in ins[nt + npar:]]
        ctx = pl.program_id(0) * tile
        res = fn(ctx, *vals)
        for o_ref, o in zip(orefs, res):
            o_ref[...] = o.astype(o_ref.dtype)

    if axis == 0:
        out_specs = [pl.BlockSpec((tile, w), lambda i: (i, 0)) for w, _ in outs]
    else:
        out_specs = [pl.BlockSpec((rows, tile), lambda i: (0, i)) for w, _ in outs]
    res = pl.pallas_call(
        body, name=name, grid=(ntiles,),
        in_specs=_stage_specs(axis, tile, tiled, params, consts, rows), out_specs=out_specs,
        out_shape=[jax.ShapeDtypeStruct((rows, w), dt) for w, dt in outs],
        compiler_params=_cparams(("arbitrary",)),
    )(*[t[0] for t in tiled], *[p[0] for p in params], *consts)
    return res


def _stage_bwd(fn, tiled, params, consts, cots, dtiled, *, axis, tile, rows, name, ncols=None):
    nt, npar, nc, nco = len(tiled), len(params), len(consts), len(cots)
    ntiles = (rows // tile) if axis == 0 else (ncols // tile)
    didx = [d[0] for d in dtiled]

    def body(*refs):
        ins = refs[: nt + npar + nc]
        crefs = refs[nt + npar + nc: nt + npar + nc + nco]
        orefs = refs[nt + npar + nc + nco:]
        vals = [r[...].astype(F32) for r in ins[: nt + npar]] + [r[...] for r in ins[nt + npar:]]
        ctx = pl.program_id(0) * tile

        def g(*dv):
            full = list(vals)
            for j, ix in enumerate(didx):
                full[ix] = dv[j]
            for j in range(npar):
                full[nt + j] = dv[len(didx) + j]
            return tuple(fn(ctx, *full))

        prim = [vals[ix] for ix in didx] + [vals[nt + j] for j in range(npar)]
        _, vjp = jax.vjp(g, *prim)
        grads = vjp(tuple(c[...].astype(F32) for c in crefs))
        for j in range(len(didx)):
            orefs[j][...] = grads[j].astype(orefs[j].dtype)
        for j in range(npar):
            o_ref = orefs[len(didx) + j]
            gp = grads[len(didx) + j]
            if axis == 0:
                @pl.when(pl.program_id(0) == 0)
                def _(o_ref=o_ref):
                    o_ref[...] = jnp.zeros_like(o_ref)

                o_ref[...] += gp
            else:
                o_ref[...] = gp

    if axis == 0:
        cot_specs = [pl.BlockSpec((tile, w), functools.partial(lambda i, c: (i, c), c=cb)) for _, w, cb in cots]
        out_specs = [pl.BlockSpec((tile, w), lambda i: (i, 0)) for _, w, _ in dtiled]
        out_specs += [pl.BlockSpec(p.shape, functools.partial(lambda i, nd: (0,) * nd, nd=p.ndim)) for p, _ in params]
        out_shape = [jax.ShapeDtypeStruct((rows, w), dt) for _, w, dt in dtiled]
        out_shape += [jax.ShapeDtypeStruct(p.shape, F32) for p, _ in params]
    else:
        cot_specs = [pl.BlockSpec((rows, tile), functools.partial(lambda i, c: (0, i + c), c=cb)) for _, cb in cots]
        out_specs = [pl.BlockSpec((rows, tile), lambda i: (0, i)) for _ in dtiled]
        out_specs += [pl.BlockSpec((p.shape[0], tile), lambda i: (0, i)) for p, _ in params]
        out_shape = [jax.ShapeDtypeStruct((rows, w), dt) for _, w, dt in dtiled]
        out_shape += [jax.ShapeDtypeStruct((p.shape[0], ncols), F32) for p, _ in params]
    return pl.pallas_call(
        body, name=name, grid=(ntiles,),
        in_specs=_stage_specs(axis, tile, tiled, params, consts, rows) + cot_specs, out_specs=out_specs,
        out_shape=out_shape, compiler_params=_cparams(("arbitrary",)),
    )(*[t[0] for t in tiled], *[p[0] for p in params], *consts, *[c[0] for c in cots])


def _rms(x, g):
    return x * lax.rsqrt(jnp.mean(x * x, axis=-1, keepdims=True) + RMS_EPS) * g


def _row_mask(x, k, first):
    t = lax.broadcasted_iota(jnp.int32, x.shape, 0)
    keep = (t >= k) if first else (t < x.shape[0] - k)
    return jnp.where(keep, x, 0.0)


@functools.partial(jax.custom_vjp, nondiff_argnums=(1,))
def _shift_down(x, k):
    return _row_mask(pltpu.roll(x, k, 0), k, True)


def _shift_down_fwd(x, k):
    return _shift_down(x, k), None


def _shift_down_bwd(k, _, g):
    return (_row_mask(pltpu.roll(g, g.shape[0] - k, 0), k, False),)


_shift_down.defvjp(_shift_down_fwd, _shift_down_bwd)


def _dwconv(x, w, b):
    kw = w.shape[0]
    out = x * w[kw - 1:kw] + b
    for j in range(kw - 1):
        out = out + _shift_down(x, kw - 1 - j) * w[j:j + 1]
    return out


def _f_norm(ctx, x, g):
    return (_rms(x, g),)


def _f_norm_res(ctx, x, g):
    return (_rms(x, g), x)


def _f_shiftmix(ctx, z, mu):
    return (z + (_shift_down(z, 1) - z) * mu,)


def _f_conv(ctx, x, w, b):
    return (_dwconv(x, w, b),)


def _f_ffn_act(ctx, gpre, up, w, b):
    return (jax.nn.gelu(_dwconv(gpre, w, b)) * up,)


def _make_f_lru_gates(heads):
    def fn(ctx, xb, wx, wa, bx, ba, lam):
        blk = xb.shape[1] // heads
        px, pa = [], []
        for h in range(heads):
            xh = xb[:, h * blk:(h + 1) * blk]
            px.append(jnp.dot(xh, wx[h], preferred_element_type=F32))
            pa.append(jnp.dot(xh, wa[h], preferred_element_type=F32))
        px = px[0] if heads == 1 else jnp.concatenate(px, axis=1)
        pa = pa[0] if heads == 1 else jnp.concatenate(pa, axis=1)
        gate_x = jax.nn.sigmoid(px + bx)
        gate_a = jax.nn.sigmoid(pa + ba)
        log_a = -LRU_C * gate_a * jax.nn.softplus(-lam)
        a = jnp.exp(log_a)
        mult = jnp.sqrt(1.0 - jnp.exp(2.0 * log_a))
        t = ctx + lax.broadcasted_iota(jnp.int32, xb.shape, 0)
        mult = jnp.where(t == 0, 1.0, mult)
        return a, xb * gate_x * mult

    return fn


def _f_lru_out(ctx, hl, ya, g):
    return (_rms(hl * jax.nn.gelu(ya), g),)


def _headsum_3pass(x, bb):
    hi = x.astype(BF16)
    r1 = x - hi.astype(F32)
    mid = r1.astype(BF16)
    lo = (r1 - mid.astype(F32)).astype(BF16)
    return (jnp.dot(hi, bb, preferred_element_type=F32) + jnp.dot(mid, bb, preferred_element_type=F32)
            + jnp.dot(lo, bb, preferred_element_type=F32))


@jax.custom_vjp
def _headsum(x, bb):
    return _headsum_3pass(x, bb)


def _headsum_fwd(x, bb):
    return _headsum_3pass(x, bb), bb


def _headsum_bwd(bb, g):
    return _headsum_3pass(g, bb), None


_headsum.defvjp(_headsum_fwd, _headsum_bwd)


def _make_f_rwkv_pre(has_vres, v_uses=0):
    def fn(ctx, *args):
        if v_uses:
            r, args = args[0], args[1:]
        if has_vres:
            k, v, lz, vf, w0, w2, a0, a2, g2, kkw, ka, v0, v2, bb = args
        else:
            k, v, lz, w0, w2, a0, a2, g2, kkw, ka, bb = args
        w_log = -jax.nn.softplus(-(w0 + jnp.dot(jnp.tanh(lz), w2, preferred_element_type=F32))) - 0.5
        logw = -jnp.exp(w_log)
        a = jax.nn.sigmoid(a0 + jnp.dot(lz, a2, preferred_element_type=F32))
        g = jnp.dot(jax.nn.sigmoid(lz), g2, preferred_element_type=F32)
        if has_vres:
            v = v + (vf - v) * jax.nn.sigmoid(v0 + jnp.dot(lz, v2, preferred_element_type=F32))
        xk = k * kkw
        kk = xk / jnp.maximum(jnp.sqrt(_headsum(xk * xk, bb)), 1e-12)
        k2 = k * (1.0 + (a - 1.0) * ka)
        if v_uses:
            return (r, r, logw, k2, k2) + (v,) * v_uses + (kk, kk * a, g)
        return logw, k2, v, kk, kk * a, g

    return fn


def _make_f_rwkv_post(head_size):
    def fn(ctx, y, r, k2, v2, g, lnw, lnb, rk, bb):
        mean = _headsum(y, bb) / head_size
        d = y - mean
        var = _headsum(d * d, bb) / head_size
        yn = d * lax.rsqrt(var + LNX_EPS) * lnw + lnb
        bonus = _headsum(r * k2 * rk, bb) * v2
        return ((yn + bonus) * g,)

    return fn


def _f_ple(ctx, h, eg, ep, g):
    return (h + _rms(jax.nn.sigmoid(eg) * ep, g),)


def _lru_scan(a, b, *, name):
    rows, cols = a.shape
    tc = _tile(cols, (512, 256, 128))

    def body(a_ref, b_ref, h_ref):
        def step(t, carry):
            h = a_ref[pl.ds(t, 1), :] * carry + b_ref[pl.ds(t, 1), :]
            h_ref[pl.ds(t, 1), :] = h
            return h

        lax.fori_loop(0, rows, step, jnp.zeros((1, tc), F32), unroll=8)

    spec = pl.BlockSpec((rows, tc), lambda j: (0, j))
    return pl.pallas_call(body, name=name, grid=(cols // tc,), in_specs=[spec, spec], out_specs=spec,
                          out_shape=jax.ShapeDtypeStruct((rows, cols), F32), compiler_params=_cparams(("arbitrary",)))(a, b)


def _lru_scan_bwd(a, h, dh, *, name):
    rows, cols = a.shape
    tc = _tile(cols, (512, 256, 128))

    def body(a_ref, h_ref, dh_ref, da_ref, db_ref):
        def step(i, carry):
            t = rows - 1 - i
            g = dh_ref[pl.ds(t, 1), :] + carry
            db_ref[pl.ds(t, 1), :] = g
            hp = h_ref[pl.ds(jnp.maximum(t - 1, 0), 1), :]
            da_ref[pl.ds(t, 1), :] = jnp.where(t > 0, g * hp, 0.0)
            return a_ref[pl.ds(t, 1), :] * g

        lax.fori_loop(0, rows, step, jnp.zeros((1, tc), F32), unroll=8)

    spec = pl.BlockSpec((rows, tc), lambda j: (0, j))
    return pl.pallas_call(body, name=name, grid=(cols // tc,), in_specs=[spec] * 3, out_specs=[spec] * 2,
                          out_shape=[jax.ShapeDtypeStruct((rows, cols), F32)] * 2,
                          compiler_params=_cparams(("arbitrary",)))(a, h, dh)


def _dot(a, b, dn):
    return lax.dot_general(a, b, (dn, ((), ())), preferred_element_type=F32, precision=HIGHEST)


def _wkv_chunk(s0, r, lw, k, v, kk, b):
    c = r.shape[0]
    row = lax.broadcasted_iota(jnp.int32, (c, c), 0)
    col = lax.broadcasted_iota(jnp.int32, (c, c), 1)
    incl = (row >= col).astype(F32)
    strict = (row > col).astype(F32)
    cl = _dot(incl, lw, ((1,), (0,)))
    w_t = jnp.exp(cl)
    w_prev = jnp.exp(cl - lw)
    inv_w = jnp.exp(-cl)
    kk_s = kk * w_prev
    b_s = b * inv_w
    k_s = k * inv_w
    r_s = r * w_t
    m = -strict * _dot(kk_s, b_s, ((1,), (1,)))
    lk = strict * _dot(kk_s, k_s, ((1,), (1,)))
    sa = -_dot(kk_s, s0, ((1,), (1,))) - _dot(lk, v, ((1,), (0,)))
    steps = max(1, (c - 1).bit_length())
    for i in range(steps):
        sa = sa + _dot(m, sa, ((1,), (0,)))
        if i + 1 < steps:
            m = _dot(m, m, ((1,), (0,)))
    y = (_dot(r_s, s0, ((1,), (1,))) + _dot(incl * _dot(r_s, b_s, ((1,), (1,))), sa, ((1,), (0,)))
         + _dot(incl * _dot(r_s, k_s, ((1,), (1,))), v, ((1,), (0,))))
    w_end = w_t[c - 1:c, :]
    s1 = s0 * w_end + _dot(sa, b_s * w_end, ((0,), (0,))) + _dot(v, k_s * w_end, ((0,), (0,)))
    return y, s1


def _wkv_heads_per_step(h):
    return _tile(h, (4, 2, 1))


def _wkv_fwd(r, lw, k, v, kk, b, *, name):
    h, t, n = r.shape
    c = WKV_CHUNK
    hb = _wkv_heads_per_step(h)
    nchunk = t // c

    def body(r_ref, lw_ref, k_ref, v_ref, kk_ref, b_ref, y_ref, st_ref, s_ref):
        @pl.when(pl.program_id(1) == 0)
        def _():
            s_ref[...] = jnp.zeros_like(s_ref)

        for i in range(hb):
            s0 = s_ref[i]
            st_ref[i, 0] = s0
            y, s1 = _wkv_chunk(s0, r_ref[i], lw_ref[i], k_ref[i], v_ref[i], kk_ref[i], b_ref[i])
            y_ref[i] = y
            s_ref[i] = s1

    spec = pl.BlockSpec((hb, c, n), lambda i, j: (i, j, 0))
    st_spec = pl.BlockSpec((hb, 1, n, n), lambda i, j: (i, j, 0, 0))
    return pl.pallas_call(
        body, name=name, grid=(h // hb, nchunk), in_specs=[spec] * 6, out_specs=[spec, st_spec],
        out_shape=[jax.ShapeDtypeStruct((h, t, n), F32), jax.ShapeDtypeStruct((h, nchunk, n, n), F32)],
        scratch_shapes=[pltpu.VMEM((hb, n, n), F32)], compiler_params=_cparams(("parallel", "arbitrary")),
    )(r, lw, k, v, kk, b)


def _wkv_bwd(r, lw, k, v, kk, b, states, dy, *, name):
    h, t, n = r.shape
    c = WKV_CHUNK
    hb = _wkv_heads_per_step(h)
    nchunk = t // c

    def body(r_ref, lw_ref, k_ref, v_ref, kk_ref, b_ref, st_ref, dy_ref, dr_ref, dlw_ref, dk_ref, dv_ref, dkk_ref,
             db_ref, ds_ref):
        @pl.when(pl.program_id(1) == 0)
        def _():
            ds_ref[...] = jnp.zeros_like(ds_ref)

        for i in range(hb):
            prim = (st_ref[i, 0], r_ref[i], lw_ref[i], k_ref[i], v_ref[i], kk_ref[i], b_ref[i])
            _, vjp = jax.vjp(_wkv_chunk, *prim)
            ds0, dr, dlw, dk, dv, dkk, db = vjp((dy_ref[i], ds_ref[i]))
            ds_ref[i] = ds0
            dr_ref[i] = dr
            dlw_ref[i] = dlw
            dk_ref[i] = dk
            dv_ref[i] = dv
            dkk_ref[i] = dkk
            db_ref[i] = db

    spec = pl.BlockSpec((hb, c, n), lambda i, j: (i, nchunk - 1 - j, 0))
    st_spec = pl.BlockSpec((hb, 1, n, n), lambda i, j: (i, nchunk - 1 - j, 0, 0))
    return pl.pallas_call(
        body, name=name, grid=(h // hb, nchunk), in_specs=[spec] * 6 + [st_spec, spec], out_specs=[spec] * 6,
        out_shape=[jax.ShapeDtypeStruct((h, t, n), F32)] * 6,
        scratch_shapes=[pltpu.VMEM((hb, n, n), F32)], compiler_params=_cparams(("parallel", "arbitrary")),
    )(r, lw, k, v, kk, b, states, dy)


class _Dims:
    pass


def _make_dims(x, p, w):
    m = _Dims()
    m.t, m.d = x.shape[-2], x.shape[-1]
    m.nl = w["ln_mix"].shape[0]
    m.dl = w["conv_a_b"].shape[1]
    m.hl = w["lru_wx"].shape[1]
    m.dr = w["rwkv_w0"].shape[1]
    m.h, m.n = w["rwkv_rk"].shape[1], w["rwkv_rk"].shape[2]
    m.lw, m.la, m.lg, m.lv = (w[k].shape[1] for k in ("rwkv_w2", "rwkv_a2", "rwkv_g2", "rwkv_v2"))
    m.nsh = w["mu_shift"].shape[1]
    m.ff = w["conv_f_b"].shape[1]
    m.ple = p.shape[-1]
    m.din = 2 * m.dl + m.nsh
    m.lz = _round_up(m.lw + m.la + m.lg + m.lv, LANES_V7X)
    m.zw = _round_up(2 * m.dl + 3 * m.dr + m.lz, 512)
    m.zs = m.zw - 2 * m.dl
    m.tr = _tile(m.t, (256, 128, 64, 32, 16, 8))
    m.trb = _tile(m.t, (128, 64, 32, 16, 8))
    m.tcs = _tile(m.zs, (512, 256, 128))
    assert (3 * m.dr) % m.lz == 0 and (2 * m.dl) % m.tcs == 0 and m.t % WKV_CHUNK == 0
    assert m.nsh == 3 * m.dr + m.lw + m.la + m.lg
    return m


def _to_heads(m, a):
    return jnp.transpose(a.reshape(m.t, m.h, m.n), (1, 0, 2))


def _from_heads(m, a):
    return jnp.transpose(a, (1, 0, 2)).reshape(m.t, m.dr)


def _norm_fwd(m, h, g, name):
    return _stage_fwd(_f_norm, [(h, m.d, 0)], [(g, 0)], [], [(m.d, BF16)], axis=0, tile=m.tr, rows=m.t, name=name)[0]


def _norm_bwd(m, h, g, du, dres, name):
    return _stage_bwd(_f_norm_res, [(h, m.d, 0)], [(g, 0)], [], [(du, m.d, 0), (dres, m.d, 0)], [(0, m.d, F32)],
                      axis=0, tile=m.tr, rows=m.t, name=name)


def _rwkv_pre_operands(m, w, i, sv, v_first_zs, with_r):
    zs = sv["zs"]
    tiled = ([(zs, m.dr, 0)] if with_r else []) + [(zs, m.dr, 1), (zs, m.dr, 2), (zs, m.lz, 3 * m.dr // m.lz)]
    params = [(w["rwkv_w0"][i:i + 1], 0), (w["w2p"][i], 0), (w["rwkv_a0"][i:i + 1], 0), (w["a2p"][i], 0),
              (w["g2p"][i], 0), (w["rwkv_kk"][i:i + 1], 0), (w["rwkv_ka"][i:i + 1], 0)]
    if i > 0:
        tiled.append((v_first_zs, m.dr, 2))
        params += [(w["rwkv_v0"][i - 1:i], 0), (w["v2p"][i - 1], 0)]
    return tiled, params


def _rwkv_post_operands(m, w, i, sv):
    tiled = [(sv["y"], m.dr, 0), (sv["zs"], m.dr, 0), (sv["k2"], m.dr, 0), (sv["v2"], m.dr, 0), (sv["g"], m.dr, 0)]
    params = [(w["rwkv_lnx_w"][i:i + 1], 0), (w["rwkv_lnx_b"][i:i + 1], 0), (w["rk"][i], 0)]
    return tiled, params


def _lru_gate_params(w, i):
    return [(w["lru_wx"][i], 0), (w["lru_wa"][i], 0), (w["lru_bx"][i:i + 1], 0), (w["lru_ba"][i:i + 1], 0),
            (w["lru_lambda"][i:i + 1], 0)]


def _layer_fwd(m, w, i, h, p_bf, v_first_zs):
    sv = {"h": h}
    t, dl, dr = m.t, m.dl, m.dr
    sv["u1"] = _norm_fwd(m, h, w["ln_mix"][i:i + 1], "norm_mix")
    z = sv["z"] = _mm(sv["u1"], w["wcat"][i], name="mm_in")
    off = 2 * dl // m.tcs
    sv["zs"] = _stage_fwd(_f_shiftmix, [(z, None, off)], [(w["mu_pad"][i], off)], [], [(m.zs, F32)],
                          axis=1, tile=m.tcs, rows=t, name="shiftmix")[0]
    tca = _tile(dl, (512, 256, 128))
    sv["xb"] = _stage_fwd(_f_conv, [(z, None, 0)], [(w["conv_a_w"][i], 0), (w["conv_a_b"][i:i + 1], 0)], [],
                          [(dl, F32)], axis=1, tile=tca, rows=t, name="conv_a")[0]
    sv["a"], b_in = _stage_fwd(_make_f_lru_gates(m.hl), [(sv["xb"], dl, 0)], _lru_gate_params(w, i), [],
                               [(dl, F32), (dl, F32)], axis=0, tile=m.tr, rows=t, name="lru_gates")
    sv["hl"] = _lru_scan(sv["a"], b_in, name="lru_scan")
    out_a = _stage_fwd(_f_lru_out, [(sv["hl"], dl, 0), (z, dl, 1)], [(w["lru_norm"][i:i + 1], 0)], [],
                       [(dl, BF16)], axis=0, tile=m.tr, rows=t, name="lru_out")[0]
    tiled, params = _rwkv_pre_operands(m, w, i, sv, v_first_zs, False)
    pre = _stage_fwd(_make_f_rwkv_pre(i > 0), tiled, params, [w["bb"]], [(dr, F32)] * 6,
                     axis=0, tile=m.tr, rows=t, name="rwkv_pre")
    sv["logw"], sv["k2"], sv["v2"], sv["kk"], sv["b"], sv["g"] = pre
    heads = [_to_heads(m, a) for a in (sv["zs"][:, :dr], sv["logw"], sv["k2"], sv["v2"], sv["kk"], sv["b"])]
    y_h, sv["states"] = _wkv_fwd(*heads, name="wkv_fwd")
    sv["y"] = _from_heads(m, y_h)
    tiled, params = _rwkv_post_operands(m, w, i, sv)
    out_b = _stage_fwd(_make_f_rwkv_post(m.n), tiled, params, [w["bb"]], [(dr, BF16)],
                       axis=0, tile=m.tr, rows=t, name="rwkv_post")[0]
    sv["cat"] = jnp.concatenate([out_a, out_b], axis=1)
    h2 = sv["h2"] = _mm(sv["cat"], w["w_o"][i], res=h, name="mm_o")
    sv["u2"] = _norm_fwd(m, h2, w["ln_ffn"][i:i + 1], "norm_ffn")
    sv["gpre"] = _mm(sv["u2"], w["w_gate"][i], name="mm_gate")
    sv["up"] = _mm(sv["u2"], w["w_up"][i], name="mm_up")
    tcf = _tile(m.ff, (512, 256, 128))
    sv["act"] = _stage_fwd(_f_ffn_act, [(sv["gpre"], None, 0), (sv["up"], None, 0)],
                           [(w["conv_f_w"][i], 0), (w["conv_f_b"][i:i + 1], 0)], [], [(m.ff, BF16)],
                           axis=1, tile=tcf, rows=t, name="ffn_act")[0]
    h3 = sv["h3"] = _mm(sv["act"], w["w_down"][i], res=h2, name="mm_down")
    sv["u3"] = _norm_fwd(m, h3, w["ln_ple"][i:i + 1], "norm_ple")
    sv["eg"] = _mm(sv["u3"], w["w_ple_gate"][i], name="mm_pgate")
    sv["ep"] = _mm(p_bf, w["w_ple_proj"][i], name="mm_pproj")
    h4 = _stage_fwd(_f_ple, [(h3, m.d, 0), (sv["eg"], m.d, 0), (sv["ep"], m.d, 0)], [(w["ln_ple_post"][i:i + 1], 0)],
                    [], [(m.d, F32)], axis=0, tile=m.tr, rows=t, name="ple")[0]
    return h4, sv


def _layer_bwd(m, w, i, dh4, sv, p_bf, v_first_zs, dvf_in):
    t, d, dl, dr = m.t, m.d, m.dl, m.dr
    g = {}
    deg, dep, g["ln_ple_post"] = _stage_bwd(
        _f_ple, [(sv["h3"], d, 0), (sv["eg"], d, 0), (sv["ep"], d, 0)], [(w["ln_ple_post"][i:i + 1], 0)], [],
        [(dh4, d, 0)], [(1, d, BF16), (2, d, BF16)], axis=0, tile=m.tr, rows=t, name="ple_bwd")
    du3 = _mm(deg, w["w_ple_gate"][i], tb=True, name="mm_pgate_dx")
    g["w_ple_gate"] = _mm(sv["u3"], deg, ta=True, name="mm_pgate_dw")
    g["w_ple_proj"] = _mm(p_bf, dep, ta=True, name="mm_pproj_dw")
    dh3, g["ln_ple"] = _norm_bwd(m, sv["h3"], w["ln_ple"][i:i + 1], du3, dh4, "norm_ple_bwd")
    dh3_bf = dh3.astype(BF16)
    dact = _mm(dh3_bf, w["w_down"][i], tb=True, name="mm_down_dx")
    g["w_down"] = _mm(sv["act"], dh3_bf, ta=True, name="mm_down_dw")
    tcf = _tile(m.ff, (512, 256, 128))
    dgpre, dup, g["conv_f_w"], g["conv_f_b"] = _stage_bwd(
        _f_ffn_act, [(sv["gpre"], None, 0), (sv["up"], None, 0)], [(w["conv_f_w"][i], 0), (w["conv_f_b"][i:i + 1], 0)],
        [], [(dact, 0)], [(0, m.ff, BF16), (1, m.ff, BF16)], axis=1, tile=tcf, rows=t, ncols=m.ff, name="ffn_act_bwd")
    du2 = _mm(dgpre, w["w_gate"][i], tb=True, name="mm_gate_dx")
    du2 = _mm(dup, w["w_up"][i], tb=True, res=du2, name="mm_up_dx")
    g["w_gate"] = _mm(sv["u2"], dgpre, ta=True, name="mm_gate_dw")
    g["w_up"] = _mm(sv["u2"], dup, ta=True, name="mm_up_dw")
    dh2, g["ln_ffn"] = _norm_bwd(m, sv["h2"], w["ln_ffn"][i:i + 1], du2, dh3, "norm_ffn_bwd")
    dh2_bf = dh2.astype(BF16)
    dcat = _mm(dh2_bf, w["w_o"][i], tb=True, name="mm_o_dx")
    g["w_o"] = _mm(sv["cat"], dh2_bf, ta=True, name="mm_o_dw")
    tiled, params = _rwkv_post_operands(m, w, i, sv)
    dy, dr_a, dk2_a, dv2_a, dg, g["rwkv_lnx_w"], g["rwkv_lnx_b"], g["rk"] = _stage_bwd(
        _make_f_rwkv_post(m.n), tiled, params, [w["bb"]], [(dcat, dr, dl // dr)], [(j, dr, F32) for j in range(5)],
        axis=0, tile=m.trb, rows=t, name="rwkv_post_bwd")
    heads = [_to_heads(m, a) for a in (sv["zs"][:, :dr], sv["logw"], sv["k2"], sv["v2"], sv["kk"], sv["b"])]
    dwkv = _wkv_bwd(*heads, sv["states"], _to_heads(m, dy), name="wkv_bwd")
    dr_b, dlw, dk2_b, dv2_b, dkk, db = [_from_heads(m, a) for a in dwkv]
    tiled, params = _rwkv_pre_operands(m, w, i, sv, v_first_zs, True)
    v_cots = [dv2_a, dv2_b] + ([dvf_in] if dvf_in is not None else [])
    cots = [(c, dr, 0) for c in [dr_a, dr_b, dlw, dk2_a, dk2_b] + v_cots + [dkk, db, dg]]
    ntil = len(tiled)
    dtiled = [(0, dr, F32), (1, dr, F32), (2, dr, F32), (3, m.lz, F32)] + ([(4, dr, F32)] if i > 0 else [])
    res = _stage_bwd(_make_f_rwkv_pre(i > 0, len(v_cots)), tiled, params, [w["bb"]], cots, dtiled,
                     axis=0, tile=m.trb, rows=t, name="rwkv_pre_bwd")
    d_r, d_k, d_v, d_lz = res[:4]
    dvf_out = res[4] if i > 0 else None
    pg = res[ntil:]
    g["rwkv_w0"], g["w2p"], g["rwkv_a0"], g["a2p"], g["g2p"], g["rwkv_kk"], g["rwkv_ka"] = pg[:7]
    if i > 0:
        g["rwkv_v0"], g["v2p"] = pg[7:9]
    dzs = jnp.concatenate([d_r, d_k, d_v, d_lz, jnp.zeros((t, m.zs - 3 * dr - m.lz), F32)], axis=1)
    off = 2 * dl // m.tcs
    dzr, g["mu_pad"] = _stage_bwd(_f_shiftmix, [(sv["z"], None, off)], [(w["mu_pad"][i], off)], [], [(dzs, 0)],
                                  [(0, m.zs, BF16)], axis=1, tile=m.tcs, rows=t, ncols=m.zs, name="shiftmix_bwd")
    dhl, dya, g["lru_norm"] = _stage_bwd(
        _f_lru_out, [(sv["hl"], dl, 0), (sv["z"], dl, 1)], [(w["lru_norm"][i:i + 1], 0)], [], [(dcat, dl, 0)],
        [(0, dl, F32), (1, dl, BF16)], axis=0, tile=m.tr, rows=t, name="lru_out_bwd")
    da, db_in = _lru_scan_bwd(sv["a"], sv["hl"], dhl, name="lru_scan_bwd")
    dxb, g["lru_wx"], g["lru_wa"], g["lru_bx"], g["lru_ba"], g["lru_lambda"] = _stage_bwd(
        _make_f_lru_gates(m.hl), [(sv["xb"], dl, 0)], _lru_gate_params(w, i), [], [(da, dl, 0), (db_in, dl, 0)],
        [(0, dl, F32)], axis=0, tile=m.tr, rows=t, name="lru_gates_bwd")
    tca = _tile(dl, (512, 256, 128))
    dxa, g["conv_a_w"], g["conv_a_b"] = _stage_bwd(
        _f_conv, [(sv["z"], None, 0)], [(w["conv_a_w"][i], 0), (w["conv_a_b"][i:i + 1], 0)], [], [(dxb, 0)],
        [(0, dl, BF16)], axis=1, tile=tca, rows=t, ncols=dl, name="conv_a_bwd")
    dz = jnp.concatenate([dxa, dya, dzr], axis=1)
    du1 = _mm(dz, w["wcat"][i], tb=True, name="mm_in_dx")
    g["wcat"] = _mm(sv["u1"], dz, ta=True, name="mm_in_dw")
    dh, g["ln_mix"] = _norm_bwd(m, sv["h"], w["ln_mix"][i:i + 1], du1, dh2, "norm_mix_bwd")
    return dh, g, dvf_out


def _loss_head(m, h, g, tgt):
    tile, d = m.tr, m.d

    def body(h_ref, g_ref, t_ref, loss_ref, dh_ref, dg_ref):
        def f(hv, gv):
            err = _rms(hv, gv) - t_ref[...]
            return 0.5 * jnp.sum(jnp.mean(err * err, axis=-1))

        val, vjp = jax.vjp(f, h_ref[...], g_ref[...])
        dh, dg = vjp(jnp.ones((), F32))
        dh_ref[...] = dh

        @pl.when(pl.program_id(0) == 0)
        def _():
            dg_ref[...] = jnp.zeros_like(dg_ref)
            loss_ref[...] = jnp.zeros_like(loss_ref)

        dg_ref[...] += dg
        loss_ref[...] += jnp.full(loss_ref.shape, val, F32)

    row = pl.BlockSpec((tile, d), lambda i: (i, 0))
    return pl.pallas_call(
        body, name="loss_head", grid=(m.t // tile,),
        in_specs=[row, pl.BlockSpec((1, d), lambda i: (0, 0)), row],
        out_specs=[pl.BlockSpec((1, LANES_V7X), lambda i: (0, 0)), row, pl.BlockSpec((1, d), lambda i: (0, 0))],
        out_shape=[jax.ShapeDtypeStruct((1, LANES_V7X), F32), jax.ShapeDtypeStruct((m.t, d), F32),
                   jax.ShapeDtypeStruct((1, d), F32)],
        compiler_params=_cparams(("arbitrary",)),
    )(h, g, tgt)


def _local_step(m, w, x, p, tgt):
    h = x
    saved = []
    p_bf = p.astype(BF16)
    for i in range(m.nl):
        h, sv = _layer_fwd(m, w, i, h, p_bf[i], saved[0]["zs"] if i > 0 else None)
        saved.append(sv)
    loss_row, dh, d_ln_final = _loss_head(m, h, w["ln_final"], tgt)
    grads = [None] * m.nl
    dvf = None
    for i in reversed(range(m.nl)):
        dh, grads[i], dvf_i = _layer_bwd(m, w, i, dh, saved[i], p_bf[i], saved[0]["zs"] if i > 0 else None,
                                         dvf if i == 0 else None)
        if i > 0:
            dvf = dvf_i if dvf is None else dvf + dvf_i
    return loss_row, dh, grads, d_ln_final


_BIG = ("w_o", "w_gate", "w_up", "w_down", "w_ple_gate", "w_ple_proj")


def _lora_rows(m):
    o1 = m.lw
    o2 = o1 + m.la
    o3 = o2 + m.lg
    return {"w2p": (0, o1), "a2p": (o1, o2), "g2p": (o2, o3), "v2p": (o3, o3 + m.lv)}


def _prepare_weights(m, wf):
    w = {k: v for k, v in wf.items() if k not in _BIG and k not in ("w_in", "w_in_vres")}
    for k in _BIG:
        w[k] = wf[k].astype(BF16)
    nl = m.nl
    vres = jnp.concatenate([jnp.zeros((1, m.d, m.lv), BF16), wf["w_in_vres"].astype(BF16)], axis=0)
    pad = jnp.zeros((nl, m.d, m.zw - m.din - m.lv), BF16)
    w["wcat"] = jnp.concatenate([wf["w_in"].astype(BF16), vres, pad], axis=2)
    mu_v = jnp.concatenate([jnp.zeros((1, m.lv), F32), wf["mu_shift_vres"]], axis=0)
    w["mu_pad"] = jnp.concatenate([jnp.zeros((nl, 2 * m.dl), F32), wf["mu_shift"], mu_v,
                                   jnp.zeros((nl, m.zw - m.din - m.lv), F32)], axis=1)[:, None, :]
    rows = _lora_rows(m)
    for name, src in (("w2p", "rwkv_w2"), ("a2p", "rwkv_a2"), ("g2p", "rwkv_g2"), ("v2p", "rwkv_v2")):
        lo, hi = rows[name]
        a = wf[src]
        w[name] = jnp.concatenate([jnp.zeros((a.shape[0], lo, m.dr), F32), a, jnp.zeros((a.shape[0], m.lz - hi, m.dr), F32)],
                                  axis=1)
    w["rk"] = wf["rwkv_rk"].reshape(nl, 1, m.dr)
    w["ln_final"] = wf["ln_final"].reshape(1, m.d)
    head = jnp.arange(m.dr, dtype=jnp.int32) // m.n
    w["bb"] = (head[:, None] == head[None, :]).astype(BF16)
    return w


def _unpack_grads(m, grads, d_ln_final):
    nl = m.nl
    out = {}

    def stack(key):
        return jnp.stack([grads[i][key] for i in range(nl)], axis=0)

    for k in _BIG + ("conv_a_w", "conv_f_w", "lru_wx", "lru_wa"):
        out[k] = stack(k)
    for k in ("ln_mix", "conv_a_b", "lru_bx", "lru_ba", "lru_lambda", "lru_norm", "rwkv_w0", "rwkv_a0", "rwkv_kk",
              "rwkv_ka", "rwkv_lnx_w", "rwkv_lnx_b", "ln_ffn", "conv_f_b", "ln_ple", "ln_ple_post"):
        out[k] = stack(k)[:, 0, :]
    wcat = stack("wcat")
    out["w_in"] = wcat[:, :, :m.din]
    out["w_in_vres"] = wcat[1:, :, m.din:m.din + m.lv]
    mu = stack("mu_pad")[:, 0, :]
    out["mu_shift"] = mu[:, :m.nsh]
    out["mu_shift_vres"] = mu[1:, m.nsh:m.nsh + m.lv]
    rows = _lora_rows(m)
    for name, dst in (("w2p", "rwkv_w2"), ("a2p", "rwkv_a2"), ("g2p", "rwkv_g2")):
        lo, hi = rows[name]
        out[dst] = stack(name)[:, lo:hi, :]
    lo, hi = rows["v2p"]
    out["rwkv_v2"] = jnp.stack([grads[i]["v2p"] for i in range(1, nl)], axis=0)[:, lo:hi, :]
    out["rwkv_v0"] = jnp.stack([grads[i]["rwkv_v0"] for i in range(1, nl)], axis=0)[:, 0, :]
    out["rwkv_rk"] = stack("rk").reshape(nl, m.h, m.n)
    out["ln_final"] = d_ln_final.reshape(m.d)
    return out


_ANY = pl.BlockSpec(memory_space=pl.ANY)


def _position():
    return lax.axis_index("x"), lax.axis_index("y"), lax.axis_index("c")


def _other_chips(x, y):
    return [(1 - x, y), (x, 1 - y), (1 - x, 1 - y)]


def _all_gather_xy(blob, *, name):
    def body(in_ref, out_ref, send_sems, recv_sems, local_sem):
        x, y, c = _position()
        mine = pltpu.make_async_copy(in_ref, out_ref.at[2 * x + y], local_sem)
        mine.start()
        sends = []
        for k, (px, py) in enumerate(_other_chips(x, y)):
            cp = pltpu.make_async_remote_copy(src_ref=in_ref, dst_ref=out_ref.at[2 * x + y], send_sem=send_sems.at[k],
                                              recv_sem=recv_sems.at[k], device_id=(px, py, c), device_id_type=MESH)
            cp.start()
            sends.append(cp)
        for k, (px, py) in enumerate(_other_chips(x, y)):
            pltpu.make_async_remote_copy(src_ref=in_ref, dst_ref=out_ref.at[2 * px + py], send_sem=send_sems.at[k],
                                         recv_sem=recv_sems.at[k], device_id=(px, py, c), device_id_type=MESH).wait_recv()
        for cp in sends:
            cp.wait_send()
        mine.wait()

    return pl.pallas_call(
        body, name=name, in_specs=[_ANY], out_specs=_ANY,
        out_shape=jax.ShapeDtypeStruct((N_XY,) + blob.shape, blob.dtype),
        scratch_shapes=[pltpu.SemaphoreType.DMA((3,)), pltpu.SemaphoreType.DMA((3,)), pltpu.SemaphoreType.DMA(())],
    )(blob)


def _pair_send_half(g, *, name):
    nq, r, wd = g.shape
    half = r // 2

    def body(g_ref, out_ref, send_sem, recv_sem):
        x, y, c = _position()
        cp = pltpu.make_async_remote_copy(src_ref=g_ref.at[:, pl.ds((1 - c) * half, half), :], dst_ref=out_ref,
                                          send_sem=send_sem, recv_sem=recv_sem, device_id=(x, y, 1 - c), device_id_type=MESH)
        cp.start()
        cp.wait()

    return pl.pallas_call(
        body, name=name, in_specs=[_ANY], out_specs=_ANY, out_shape=jax.ShapeDtypeStruct((nq, half, wd), g.dtype),
        scratch_shapes=[pltpu.SemaphoreType.DMA(()), pltpu.SemaphoreType.DMA(())],
    )(g)


def _pair_sum(g, got, c_idx, *, name):
    nq, r, wd = g.shape
    half = r // 2
    tr = _tile(half, (256, 128, 64, 32, 16, 8))
    nb = half // tr

    def body(c_ref, g_ref, got_ref, o_ref):
        o_ref[...] = (g_ref[...] + got_ref[...]).astype(o_ref.dtype)

    grid_spec = pltpu.PrefetchScalarGridSpec(
        num_scalar_prefetch=1, grid=(nq, nb),
        in_specs=[pl.BlockSpec((1, tr, wd), lambda q, j, c_ref: (q, c_ref[0] * nb + j, 0)),
                  pl.BlockSpec((1, tr, wd), lambda q, j, c_ref: (q, j, 0))],
        out_specs=pl.BlockSpec((1, tr, wd), lambda q, j, c_ref: (q, j, 0)))
    return pl.pallas_call(body, name=name, grid_spec=grid_spec, out_shape=jax.ShapeDtypeStruct((nq, half, wd), BF16),
                          compiler_params=_cparams(("arbitrary", "arbitrary")))(c_idx, g, got)


def _exchange_xy(pb, *, name):
    def body(in_ref, out_ref, send_sems, recv_sems, local_sem):
        x, y, c = _position()
        me = 2 * x + y
        mine = pltpu.make_async_copy(in_ref.at[me], out_ref.at[me], local_sem)
        mine.start()
        sends = []
        for k, (px, py) in enumerate(_other_chips(x, y)):
            cp = pltpu.make_async_remote_copy(src_ref=in_ref.at[2 * px + py], dst_ref=out_ref.at[me], send_sem=send_sems.at[k],
                                              recv_sem=recv_sems.at[k], device_id=(px, py, c), device_id_type=MESH)
            cp.start()
            sends.append(cp)
        for k, (px, py) in enumerate(_other_chips(x, y)):
            pltpu.make_async_remote_copy(src_ref=in_ref.at[me], dst_ref=out_ref.at[2 * px + py], send_sem=send_sems.at[k],
                                         recv_sem=recv_sems.at[k], device_id=(px, py, c), device_id_type=MESH).wait_recv()
        for cp in sends:
            cp.wait_send()
        mine.wait()

    return pl.pallas_call(
        body, name=name, in_specs=[_ANY], out_specs=_ANY, out_shape=jax.ShapeDtypeStruct(pb.shape, pb.dtype),
        scratch_shapes=[pltpu.SemaphoreType.DMA((3,)), pltpu.SemaphoreType.DMA((3,)), pltpu.SemaphoreType.DMA(())],
    )(pb)


def _chip_sum(parts, *, name):
    nq, r, wd = parts.shape
    tr = _tile(r, (256, 128, 64, 32, 16, 8))

    def body(p_ref, o_ref):
        acc = p_ref[0].astype(F32)
        for q in range(1, nq):
            acc = acc + p_ref[q].astype(F32)
        o_ref[...] = acc

    return pl.pallas_call(body, name=name, grid=(r // tr,), in_specs=[pl.BlockSpec((nq, tr, wd), lambda j: (0, j, 0))],
                          out_specs=pl.BlockSpec((tr, wd), lambda j: (j, 0)), out_shape=jax.ShapeDtypeStruct((r, wd), F32),
                          compiler_params=_cparams(("arbitrary",)))(parts)


def _pair_gather(rhalf, *, name):
    half, wd = rhalf.shape

    def body(in_ref, out_ref, send_sem, recv_sem, local_sem):
        x, y, c = _position()
        mine = pltpu.make_async_copy(in_ref, out_ref.at[pl.ds(c * half, half), :], local_sem)
        mine.start()
        cp = pltpu.make_async_remote_copy(src_ref=in_ref, dst_ref=out_ref.at[pl.ds(c * half, half), :], send_sem=send_sem,
                                          recv_sem=recv_sem, device_id=(x, y, 1 - c), device_id_type=MESH)
        cp.start()
        pltpu.make_async_remote_copy(src_ref=in_ref, dst_ref=out_ref.at[pl.ds((1 - c) * half, half), :], send_sem=send_sem,
                                     recv_sem=recv_sem, device_id=(x, y, 1 - c), device_id_type=MESH).wait_recv()
        cp.wait_send()
        mine.wait()

    return pl.pallas_call(
        body, name=name, in_specs=[_ANY], out_specs=_ANY, out_shape=jax.ShapeDtypeStruct((2 * half, wd), rhalf.dtype),
        scratch_shapes=[pltpu.SemaphoreType.DMA(()), pltpu.SemaphoreType.DMA(()), pltpu.SemaphoreType.DMA(())],
    )(rhalf)


def _reduce_to_shard(g, c_idx, tag):
    got = _pair_send_half(g, name="rs_pair_send_" + tag)
    pb = _pair_sum(g, got, c_idx, name="rs_pair_sum_" + tag)
    parts = _exchange_xy(pb, name="rs_exchange_" + tag)
    rhalf = _chip_sum(parts, name="rs_chip_sum_" + tag)
    return _pair_gather(rhalf, name="rs_pair_gather_" + tag)


def _all_reduce_small(vec, *, name):
    r, wd = vec.shape

    def body(in_ref, out_ref, slots, send_sems, recv_sems):
        x, y, c = _position()
        me = 4 * x + 2 * y + c
        flips = [(fx, fy, fc) for fx in (0, 1) for fy in (0, 1) for fc in (0, 1) if fx + fy + fc]
        peers = [(1 - x if fx else x, 1 - y if fy else y, 1 - c if fc else c) for fx, fy, fc in flips]
        sends = []
        for k, peer in enumerate(peers):
            cp = pltpu.make_async_remote_copy(src_ref=in_ref, dst_ref=slots.at[me], send_sem=send_sems.at[k],
                                              recv_sem=recv_sems.at[k], device_id=peer, device_id_type=MESH)
            cp.start()
            sends.append(cp)
        slots[me] = in_ref[...]
        for k, (px, py, pc) in enumerate(peers):
            pltpu.make_async_remote_copy(src_ref=in_ref, dst_ref=slots.at[4 * px + 2 * py + pc], send_sem=send_sems.at[k],
                                         recv_sem=recv_sems.at[k], device_id=(px, py, pc), device_id_type=MESH).wait_recv()
        for cp in sends:
            cp.wait_send()
        acc = slots[0]
        for q in range(1, N_DEV):
            acc = acc + slots[q]
        out_ref[...] = acc

    vm = pl.BlockSpec(memory_space=pltpu.VMEM)
    return pl.pallas_call(
        body, name=name, in_specs=[vm], out_specs=vm, out_shape=jax.ShapeDtypeStruct((r, wd), F32),
        scratch_shapes=[pltpu.VMEM((N_DEV, r, wd), F32), pltpu.SemaphoreType.DMA((N_DEV - 1,)),
                        pltpu.SemaphoreType.DMA((N_DEV - 1,))],
        compiler_params=_cparams(),
    )(vec)


def _adamw(w, g, m, v, *, name):
    r, wd = w.shape
    tr = _tile(r, (256, 128, 64, 32, 16, 8))

    def body(w_ref, g_ref, m_ref, v_ref, d_ref, m_out, v_out):
        gv = g_ref[...]
        m_new = ADAM_B1 * m_ref[...] + (1.0 - ADAM_B1) * gv
        v_new = ADAM_B2 * v_ref[...] + (1.0 - ADAM_B2) * (gv * gv)
        m_hat = m_new / (1.0 - ADAM_B1 ** ADAM_STEP)
        v_hat = v_new / (1.0 - ADAM_B2 ** ADAM_STEP)
        d_ref[...] = -ADAM_LR * (m_hat / (jnp.sqrt(v_hat) + ADAM_EPS) + ADAM_WD * w_ref[...])
        m_out[...] = m_new
        v_out[...] = v_new

    spec = pl.BlockSpec((tr, wd), lambda j: (j, 0))
    return pl.pallas_call(body, name=name, grid=(r // tr,), in_specs=[spec] * 4, out_specs=[spec] * 3,
                          out_shape=[jax.ShapeDtypeStruct((r, wd), F32)] * 3, compiler_params=_cparams(("arbitrary",)))(w, g, m, v)


_WEIGHTS = ("ln_mix", "w_in", "w_in_vres", "mu_shift", "mu_shift_vres", "conv_a_w", "conv_a_b", "lru_wx", "lru_bx", "lru_wa",
            "lru_ba", "lru_lambda", "lru_norm", "rwkv_w0", "rwkv_w2", "rwkv_a0", "rwkv_a2", "rwkv_v0", "rwkv_v2", "rwkv_g2",
            "rwkv_kk", "rwkv_ka", "rwkv_rk", "rwkv_lnx_w", "rwkv_lnx_b", "w_o", "ln_ffn", "w_gate", "w_up", "conv_f_w",
            "conv_f_b", "w_down", "ln_ple", "w_ple_gate", "w_ple_proj", "ln_ple_post", "ln_final")
_SHARD_AXIS = {"w_in": 2, "w_in_vres": 1, "conv_a_w": 2, "lru_wx": 2, "lru_wa": 2, "rwkv_w2": 2, "rwkv_a2": 2, "rwkv_v2": 2,
               "rwkv_g2": 2, "w_o": 1, "w_gate": 2, "w_up": 2, "conv_f_w": 2, "w_down": 1, "w_ple_gate": 1, "w_ple_proj": 2}
_BIG_SHARDED = ("w_in",) + _BIG
_SMALL_SHARDED = tuple(k for k in _WEIGHTS if k in _SHARD_AXIS and k not in _BIG_SHARDED)
_REPLICATED = tuple(k for k in _WEIGHTS if k not in _SHARD_AXIS)
PACK_WIDTH = 512


def _to_shards(g, axis):
    n = g.shape[axis] // N_XY
    return jnp.moveaxis(g.reshape(g.shape[:axis] + (N_XY, n) + g.shape[axis + 1:]), axis, 0)


def _from_shards(s, axis):
    s = jnp.moveaxis(s, 0, axis)
    return s.reshape(s.shape[:axis] + (N_XY * s.shape[axis + 1],) + s.shape[axis + 2:])


def _pack(arrs, lead, width, row_mult):
    lead_shape = arrs[0].shape[:lead]
    flat = jnp.concatenate([a.reshape(lead_shape + (-1,)) for a in arrs], axis=-1)
    n = flat.shape[-1]
    total = _round_up(n, width * row_mult)
    flat = jnp.pad(flat, [(0, 0)] * lead + [(0, total - n)])
    return flat.reshape(lead_shape + (total // width, width))


def _unpack(packed, shapes):
    flat = packed.reshape(-1)
    out, o = [], 0
    for s in shapes:
        n = 1
        for dim in s:
            n *= dim
        out.append(flat[o:o + n].reshape(s))
        o += n
    return out


def _as2d(a):
    return a.reshape(-1, a.shape[-1])


def kernel(x, p, ln_mix, w_in, w_in_vres, mu_shift, mu_shift_vres, conv_a_w, conv_a_b, lru_wx, lru_bx, lru_wa, lru_ba, lru_lambda, lru_norm, rwkv_w0, rwkv_w2, rwkv_a0, rwkv_a2, rwkv_v0, rwkv_v2, rwkv_g2, rwkv_kk, rwkv_ka, rwkv_rk, rwkv_lnx_w, rwkv_lnx_b, w_o, ln_ffn, w_gate, w_up, conv_f_w, conv_f_b, w_down, ln_ple, w_ple_gate, w_ple_proj, ln_ple_post, ln_final, loss_target, m_ln_mix, m_w_in, m_w_in_vres, m_mu_shift, m_mu_shift_vres, m_conv_a_w, m_conv_a_b, m_lru_wx, m_lru_bx, m_lru_wa, m_lru_ba, m_lru_lambda, m_lru_norm, m_rwkv_w0, m_rwkv_w2, m_rwkv_a0, m_rwkv_a2, m_rwkv_v0, m_rwkv_v2, m_rwkv_g2, m_rwkv_kk, m_rwkv_ka, m_rwkv_rk, m_rwkv_lnx_w, m_rwkv_lnx_b, m_w_o, m_ln_ffn, m_w_gate, m_w_up, m_conv_f_w, m_conv_f_b, m_w_down, m_ln_ple, m_w_ple_gate, m_w_ple_proj, m_ln_ple_post, m_ln_final, v_ln_mix, v_w_in, v_w_in_vres, v_mu_shift, v_mu_shift_vres, v_conv_a_w, v_conv_a_b, v_lru_wx, v_lru_bx, v_lru_wa, v_lru_ba, v_lru_lambda, v_lru_norm, v_rwkv_w0, v_rwkv_w2, v_rwkv_a0, v_rwkv_a2, v_rwkv_v0, v_rwkv_v2, v_rwkv_g2, v_rwkv_kk, v_rwkv_ka, v_rwkv_rk, v_rwkv_lnx_w, v_rwkv_lnx_b, v_w_o, v_ln_ffn, v_w_gate, v_w_up, v_conv_f_w, v_conv_f_b, v_w_down, v_ln_ple, v_w_ple_gate, v_w_ple_proj, v_ln_ple_post, v_ln_final):
    a = dict(locals())
    x2, p, tgt = a["x"][0], a["p"][:, 0], a["loss_target"][0]
    c_idx = lax.axis_index("c").astype(jnp.int32).reshape(1)

    wf = {k: a[k] for k in _REPLICATED}
    for k in _BIG_SHARDED:
        wf[k] = _from_shards(_all_gather_xy(a[k].astype(BF16), name="ag_" + k), _SHARD_AXIS[k])
    small_shapes = [a[k].shape for k in _SMALL_SHARDED]
    got = _all_gather_xy(_pack([a[k] for k in _SMALL_SHARDED], 0, PACK_WIDTH, 8), name="ag_small")
    pieces = [_unpack(got[q], small_shapes) for q in range(N_XY)]
    for j, k in enumerate(_SMALL_SHARDED):
        wf[k] = _from_shards(jnp.stack([pieces[q][j] for q in range(N_XY)], axis=0), _SHARD_AXIS[k])

    m = _make_dims(x2, p, wf)
    loss_row, dx, grads, d_ln_final = _local_step(m, _prepare_weights(m, wf), x2, p, tgt)
    gfull = _unpack_grads(m, grads, d_ln_final)
    loss = lax.psum(loss_row[0, 0], ("x", "y", "c"))

    gred = {}
    for k in _BIG_SHARDED:
        gs = _to_shards(gfull[k], _SHARD_AXIS[k])
        gred[k] = _reduce_to_shard(gs.reshape(N_XY, -1, gs.shape[-1]), c_idx, k).reshape(gs.shape[1:])
    gs = _pack([_to_shards(gfull[k], _SHARD_AXIS[k]) for k in _SMALL_SHARDED], 1, PACK_WIDTH, 32)
    g_small = _reduce_to_shard(gs, c_idx, "small")
    rep_shapes = [a[k].shape for k in _REPLICATED]
    g_rep = _all_reduce_small(_pack([gfull[k] for k in _REPLICATED], 0, LANES_V7X, 8), name="ar_replicated")

    delta, new_m, new_v = {}, {}, {}
    for k in _BIG_SHARDED:
        res = _adamw(_as2d(a[k]), _as2d(gred[k]), _as2d(a["m_" + k]), _as2d(a["v_" + k]), name="adamw_" + k)
        delta[k], new_m[k], new_v[k] = (r.reshape(a[k].shape) for r in res)
    for names, shapes, g_packed, width, mult, tag in ((_SMALL_SHARDED, small_shapes, g_small, PACK_WIDTH, 32, "small"),
                                                      (_REPLICATED, rep_shapes, g_rep, LANES_V7X, 8, "replicated")):
        packs = [_pack([a[pre + k] for k in names], 0, width, mult) for pre in ("", "m_", "v_")]
        res = _adamw(packs[0], g_packed, packs[1], packs[2], name="adamw_" + tag)
        for dst, r in zip((gred, delta, new_m, new_v), [g_packed] + list(res)):
            dst.update(zip(names, _unpack(r, shapes)))
    return (loss, dx[None], *[gred[k] for k in _WEIGHTS], *[delta[k] for k in _WEIGHTS],
            *[new_m[k] for k in _WEIGHTS], *[new_v[k] for k in _WEIGHTS])
```

```python
import functools

import jax
import jax.numpy as jnp
from jax import lax
from jax.experimental import pallas as pl
from jax.experimental.pallas import tpu as pltpu

F32 = jnp.float32
BF16 = jnp.bfloat16
HIGHEST = lax.Precision.HIGHEST
MESH = pl.DeviceIdType.MESH

RMS_EPS = 1e-6
LNX_EPS = 64e-5
LRU_C = 8.0
ADAM_LR = 0.001
ADAM_B1 = 0.9
ADAM_B2 = 0.999
ADAM_EPS = 1e-08
ADAM_WD = 0.01
ADAM_STEP = 10

LANES_V7X = 128
VMEM_LIMIT_V7X = 60 * 1024 * 1024
WKV_CHUNK = 16
N_XY = 4
N_DEV = 8


def _cparams(sem=None, **kw):
    if sem is not None:
        kw["dimension_semantics"] = sem
    return pltpu.CompilerParams(vmem_limit_bytes=VMEM_LIMIT_V7X, **kw)


def _tile(dim, prefs):
    for t in prefs:
        if dim % t == 0:
            return t
    return dim


def _round_up(n, m):
    return (n + m - 1) // m * m


def _mm(a, b, *, ta=False, tb=False, res=None, out_dtype=F32, name):
    if ta:
        kdim, m = a.shape
    else:
        m, kdim = a.shape
    n = b.shape[0] if tb else b.shape[1]
    assert (b.shape[1] if tb else b.shape[0]) == kdim
    tm = _tile(m, (2048, 1024, 512, 256, 128))
    tn = _tile(n, (512, 256, 128))
    tk = _tile(kdim, (1024, 512, 256, 128))
    nk = kdim // tk
    a_spec = pl.BlockSpec((tk, tm), lambda i, j, k: (k, i)) if ta else pl.BlockSpec((tm, tk), lambda i, j, k: (i, k))
    b_spec = pl.BlockSpec((tn, tk), lambda i, j, k: (j, k)) if tb else pl.BlockSpec((tk, tn), lambda i, j, k: (k, j))
    o_spec = pl.BlockSpec((tm, tn), lambda i, j, k: (i, j))
    dn = (((0 if ta else 1,), (1 if tb else 0,)), ((), ()))
    has_res = res is not None

    def body(*refs):
        if has_res:
            a_ref, b_ref, r_ref, o_ref, acc_ref = refs
        else:
            a_ref, b_ref, o_ref, acc_ref = refs
        k = pl.program_id(2)

        @pl.when(k == 0)
        def _():
            acc_ref[...] = jnp.zeros_like(acc_ref)

        acc_ref[...] += lax.dot_general(a_ref[...], b_ref[...], dn, preferred_element_type=F32)

        @pl.when(k == nk - 1)
        def _():
            acc = acc_ref[...]
            if has_res:
                acc = acc + r_ref[...].astype(F32)
            o_ref[...] = acc.astype(out_dtype)

    ins = [a, b] + ([res] if has_res else [])
    in_specs = [a_spec, b_spec] + ([o_spec] if has_res else [])
    return pl.pallas_call(
        body, name=name, grid=(m // tm, n // tn, nk), in_specs=in_specs, out_specs=o_spec,
        out_shape=jax.ShapeDtypeStruct((m, n), out_dtype), scratch_shapes=[pltpu.VMEM((tm, tn), F32)],
        compiler_params=_cparams(("parallel", "parallel", "arbitrary")),
    )(*ins)


def _stage_specs(axis, tile, tiled, params, consts, rows):
    specs = []
    for arr, width, cblk in tiled:
        if axis == 0:
            specs.append(pl.BlockSpec((tile, width), functools.partial(lambda i, c: (i, c), c=cblk)))
        else:
            specs.append(pl.BlockSpec((rows, tile), functools.partial(lambda i, c: (0, i + c), c=cblk)))
    for arr, cblk in params:
        if axis == 0:
            specs.append(pl.BlockSpec(arr.shape, functools.partial(lambda i, nd: (0,) * nd, nd=arr.ndim)))
        else:
            specs.append(pl.BlockSpec((arr.shape[0], tile), functools.partial(lambda i, c: (0, i + c), c=cblk)))
    for arr in consts:
        specs.append(pl.BlockSpec(arr.shape, functools.partial(lambda i, nd: (0,) * nd, nd=arr.ndim)))
    return specs


def _stage_fwd(fn, tiled, params, consts, outs, *, axis, tile, rows, name):
    nt, npar, nc = len(tiled), len(params), len(consts)
    ntiles = (rows // tile) if axis == 0 else (outs[0][0] // tile)

    def body(*refs):
        ins = refs[: nt + npar + nc]
        orefs = refs[nt + npar + nc:]
        vals = [r[...].astype(F32) for r in ins[: nt + npar]] + [r[...] for r in ins[nt + npar:]]
        ctx = pl.program_id(0) * tile
        res = fn(ctx, *vals)
        for o_ref, o in zip(orefs, res):
            o_ref[...] = o.astype(o_ref.dtype)

    if axis == 0:
        out_specs = [pl.BlockSpec((tile, w), lambda i: (i, 0)) for w, _ in outs]
    else:
        out_specs = [pl.BlockSpec((rows, tile), lambda i: (0, i)) for w, _ in outs]
    res = pl.pallas_call(
        body, name=name, grid=(ntiles,),
        in_specs=_stage_specs(axis, tile, tiled, params, consts, rows), out_specs=out_specs,
        out_shape=[jax.ShapeDtypeStruct((rows, w), dt) for w, dt in outs],
        compiler_params=_cparams(("arbitrary",)),
    )(*[t[0] for t in tiled], *[p[0] for p in params], *consts)
    return res


def _stage_bwd(fn, tiled, params, consts, cots, dtiled, *, axis, tile, rows, name, ncols=None):
    nt, npar, nc, nco = len(tiled), len(params), len(consts), len(cots)
    ntiles = (rows // tile) if axis == 0 else (ncols // tile)
    didx = [d[0] for d in dtiled]

    def body(*refs):
        ins = refs[: nt + npar + nc]
        crefs = refs[nt + npar + nc: nt + npar + nc + nco]
        orefs = refs[nt + npar + nc + nco:]
        vals = [r[...].astype(F32) for r in ins[: nt + npar]] + [r[...] for r in ins[nt + npar:]]
        ctx = pl.program_id(0) * tile

        def g(*dv):
            full = list(vals)
            for j, ix in enumerate(didx):
                full[ix] = dv[j]
            for j in range(npar):
                full[nt + j] = dv[len(didx) + j]
            return tuple(fn(ctx, *full))

        prim = [vals[ix] for ix in didx] + [vals[nt + j] for j in range(npar)]
        _, vjp = jax.vjp(g, *prim)
        grads = vjp(tuple(c[...].astype(F32) for c in crefs))
        for j in range(len(didx)):
            orefs[j][...] = grads[j].astype(orefs[j].dtype)
        for j in range(npar):
            o_ref = orefs[len(didx) + j]
            gp = grads[len(didx) + j]
            if axis == 0:
                @pl.when(pl.program_id(0) == 0)
                def _(o_ref=o_ref):
                    o_ref[...] = jnp.zeros_like(o_ref)

                o_ref[...] += gp
            else:
                o_ref[...] = gp

    if axis == 0:
        cot_specs = [pl.BlockSpec((tile, w), functools.partial(lambda i, c: (i, c), c=cb)) for _, w, cb in cots]
        out_specs = [pl.BlockSpec((tile, w), lambda i: (i, 0)) for _, w, _ in dtiled]
        out_specs += [pl.BlockSpec(p.shape, functools.partial(lambda i, nd: (0,) * nd, nd=p.ndim)) for p, _ in params]
        out_shape = [jax.ShapeDtypeStruct((rows, w), dt) for _, w, dt in dtiled]
        out_shape += [jax.ShapeDtypeStruct(p.shape, F32) for p, _ in params]
    else:
        cot_specs = [pl.BlockSpec((rows, tile), functools.partial(lambda i, c: (0, i + c), c=cb)) for _, cb in cots]
        out_specs = [pl.BlockSpec((rows, tile), lambda i: (0, i)) for _ in dtiled]
        out_specs += [pl.BlockSpec((p.shape[0], tile), lambda i: (0, i)) for p, _ in params]
        out_shape = [jax.ShapeDtypeStruct((rows, w), dt) for _, w, dt in dtiled]
        out_shape += [jax.ShapeDtypeStruct((p.shape[0], ncols), F32) for p, _ in params]
    return pl.pallas_call(
        body, name=name, grid=(ntiles,),
        in_specs=_stage_specs(axis, tile, tiled, params, consts, rows) + cot_specs, out_specs=out_specs,
        out_shape=out_shape, compiler_params=_cparams(("arbitrary",)),
    )(*[t[0] for t in tiled], *[p[0] for p in params], *consts, *[c[0] for c in cots])


def _rms(x, g):
    return x * lax.rsqrt(jnp.mean(x * x, axis=-1, keepdims=True) + RMS_EPS) * g


def _row_mask(x, k, first):
    t = lax.broadcasted_iota(jnp.int32, x.shape, 0)
    keep = (t >= k) if first else (t < x.shape[0] - k)
    return jnp.where(keep, x, 0.0)


@functools.partial(jax.custom_vjp, nondiff_argnums=(1,))
def _shift_down(x, k):
    return _row_mask(pltpu.roll(x, k, 0), k, True)


def _shift_down_fwd(x, k):
    return _shift_down(x, k), None


def _shift_down_bwd(k, _, g):
    return (_row_mask(pltpu.roll(g, g.shape[0] - k, 0), k, False),)


_shift_down.defvjp(_shift_down_fwd, _shift_down_bwd)


def _dwconv(x, w, b):
    kw = w.shape[0]
    out = x * w[kw - 1:kw] + b
    for j in range(kw - 1):
        out = out + _shift_down(x, kw - 1 - j) * w[j:j + 1]
    return out


def _f_norm(ctx, x, g):
    return (_rms(x, g),)


def _f_norm_res(ctx, x, g):
    return (_rms(x, g), x)


def _f_shiftmix(ctx, z, mu):
    return (z + (_shift_down(z, 1) - z) * mu,)


def _f_conv(ctx, x, w, b):
    return (_dwconv(x, w, b),)


def _f_ffn_act(ctx, gpre, up, w, b):
    return (jax.nn.gelu(_dwconv(gpre, w, b)) * up,)


def _make_f_lru_gates(heads):
    def fn(ctx, xb, wx, wa, bx, ba, lam):
        blk = xb.shape[1] // heads
        px, pa = [], []
        for h in range(heads):
            xh = xb[:, h * blk:(h + 1) * blk]
            px.append(jnp.dot(xh, wx[h], preferred_element_type=F32))
            pa.append(jnp.dot(xh, wa[h], preferred_element_type=F32))
        px = px[0] if heads == 1 else jnp.concatenate(px, axis=1)
        pa = pa[0] if heads == 1 else jnp.concatenate(pa, axis=1)
        gate_x = jax.nn.sigmoid(px + bx)
        gate_a = jax.nn.sigmoid(pa + ba)
        log_a = -LRU_C * gate_a * jax.nn.softplus(-lam)
        a = jnp.exp(log_a)
        mult = jnp.sqrt(1.0 - jnp.exp(2.0 * log_a))
        t = ctx + lax.broadcasted_iota(jnp.int32, xb.shape, 0)
        mult = jnp.where(t == 0, 1.0, mult)
        return a, xb * gate_x * mult

    return fn


def _f_lru_out(ctx, hl, ya, g):
    return (_rms(hl * jax.nn.gelu(ya), g),)


def _headsum_3pass(x, bb):
    hi = x.astype(BF16)
    r1 = x - hi.astype(F32)
    mid = r1.astype(BF16)
    lo = (r1 - mid.astype(F32)).astype(BF16)
    return (jnp.dot(hi, bb, preferred_element_type=F32) + jnp.dot(mid, bb, preferred_element_type=F32)
            + jnp.dot(lo, bb, preferred_element_type=F32))


@jax.custom_vjp
def _headsum(x, bb):
    return _headsum_3pass(x, bb)


def _headsum_fwd(x, bb):
    return _headsum_3pass(x, bb), bb


def _headsum_bwd(bb, g):
    return _headsum_3pass(g, bb), None


_headsum.defvjp(_headsum_fwd, _headsum_bwd)


def _make_f_rwkv_pre(has_vres, v_uses=0):
    def fn(ctx, *args):
        if v_uses:
            r, args = args[0], args[1:]
        if has_vres:
            k, v, lz, vf, w0, w2, a0, a2, g2, kkw, ka, v0, v2, bb = args
        else:
            k, v, lz, w0, w2, a0, a2, g2, kkw, ka, bb = args
        w_log = -jax.nn.softplus(-(w0 + jnp.dot(jnp.tanh(lz), w2, preferred_element_type=F32))) - 0.5
        logw = -jnp.exp(w_log)
        a = jax.nn.sigmoid(a0 + jnp.dot(lz, a2, preferred_element_type=F32))
        g = jnp.dot(jax.nn.sigmoid(lz), g2, preferred_element_type=F32)
        if has_vres:
            v = v + (vf - v) * jax.nn.sigmoid(v0 + jnp.dot(lz, v2, preferred_element_type=F32))
        xk = k * kkw
        kk = xk / jnp.maximum(jnp.sqrt(_headsum(xk * xk, bb)), 1e-12)
        k2 = k * (1.0 + (a - 1.0) * ka)
        if v_uses:
            return (r, r, logw, k2, k2) + (v,) * v_uses + (kk, kk * a, g)
        return logw, k2, v, kk, kk * a, g

    return fn


def _make_f_rwkv_post(head_size):
    def fn(ctx, y, r, k2, v2, g, lnw, lnb, rk, bb):
        mean = _headsum(y, bb) / head_size
        d = y - mean
        var = _headsum(d * d, bb) / head_size
        yn = d * lax.rsqrt(var + LNX_EPS) * lnw + lnb
        bonus = _headsum(r * k2 * rk, bb) * v2
        return ((yn + bonus) * g,)

    return fn


def _f_ple(ctx, h, eg, ep, g):
    return (h + _rms(jax.nn.sigmoid(eg) * ep, g),)


def _lru_scan(a, b, *, name):
    rows, cols = a.shape
    tc = _tile(cols, (512, 256, 128))

    def body(a_ref, b_ref, h_ref):
        def step(t, carry):
            h = a_ref[pl.ds(t, 1), :] * carry + b_ref[pl.ds(t, 1), :]
            h_ref[pl.ds(t, 1), :] = h
            return h

        lax.fori_loop(0, rows, step, jnp.zeros((1, tc), F32), unroll=8)

    spec = pl.BlockSpec((rows, tc), lambda j: (0, j))
    return pl.pallas_call(body, name=name, grid=(cols // tc,), in_specs=[spec, spec], out_specs=spec,
                          out_shape=jax.ShapeDtypeStruct((rows, cols), F32), compiler_params=_cparams(("arbitrary",)))(a, b)


def _lru_scan_bwd(a, h, dh, *, name):
    rows, cols = a.shape
    tc = _tile(cols, (512, 256, 128))

    def body(a_ref, h_ref, dh_ref, da_ref, db_ref):
        def step(i, carry):
            t = rows - 1 - i
            g = dh_ref[pl.ds(t, 1), :] + carry
            db_ref[pl.ds(t, 1), :] = g
            hp = h_ref[pl.ds(jnp.maximum(t - 1, 0), 1), :]
            da_ref[pl.ds(t, 1), :] = jnp.where(t > 0, g * hp, 0.0)
            return a_ref[pl.ds(t, 1), :] * g

        lax.fori_loop(0, rows, step, jnp.zeros((1, tc), F32), unroll=8)

    spec = pl.BlockSpec((rows, tc), lambda j: (0, j))
    return pl.pallas_call(body, name=name, grid=(cols // tc,), in_specs=[spec] * 3, out_specs=[spec] * 2,
                          out_shape=[jax.ShapeDtypeStruct((rows, cols), F32)] * 2,
                          compiler_params=_cparams(("arbitrary",)))(a, h, dh)


def _split_bf16(x):
    hi = x.astype(BF16)
    return hi, (x - hi.astype(F32)).astype(BF16)


def _dot3_passes(a, b, ca, cb):
    dn = (((ca,), (cb,)), ((), ()))
    ah, al = _split_bf16(a)
    bh, bl = _split_bf16(b)
    return (lax.dot_general(ah, bh, dn, preferred_element_type=F32) + lax.dot_general(al, bh, dn, preferred_element_type=F32)
            + lax.dot_general(ah, bl, dn, preferred_element_type=F32))


@functools.partial(jax.custom_vjp, nondiff_argnums=(2, 3))
def _dot3(a, b, ca, cb):
    return _dot3_passes(a, b, ca, cb)


def _dot3_fwd(a, b, ca, cb):
    return _dot3_passes(a, b, ca, cb), (a, b)


def _dot3_bwd(ca, cb, res, g):
    a, b = res
    fa, fb = 1 - ca, 1 - cb
    da = _dot3_passes(g, b, 1, fb) if ca == 1 else _dot3_passes(b, g, fb, 1)
    db = _dot3_passes(a, g, fa, 0) if cb == 0 else _dot3_passes(g, a, 0, fa)
    return da, db


_dot3.defvjp(_dot3_fwd, _dot3_bwd)


def _each(f, *lists):
    return [f(*t) for t in zip(*lists)]


def _wkv_local(r, lw, k, v, kk, b):
    c, n = r[0].shape
    row = lax.broadcasted_iota(jnp.int32, (c, c), 0)
    col = lax.broadcasted_iota(jnp.int32, (c, c), 1)
    incl = (row >= col).astype(F32)
    strict = (row > col).astype(F32)
    eye = lax.broadcasted_iota(jnp.int32, (n, n), 0) == lax.broadcasted_iota(jnp.int32, (n, n), 1)
    cl = _each(lambda x: _dot3(incl, x, 1, 0), lw)
    w_t = _each(jnp.exp, cl)
    inv_w = _each(lambda x: jnp.exp(-x), cl)
    kk_s = _each(lambda x, y, z: x * jnp.exp(y - z), kk, cl, lw)
    b_s = _each(jnp.multiply, b, inv_w)
    k_s = _each(jnp.multiply, k, inv_w)
    r_s = _each(jnp.multiply, r, w_t)
    q = _each(lambda x, y: jnp.concatenate([x, y], axis=0), kk_s, r_s)
    qb = _each(lambda x, y: _dot3(x, y, 1, 1), q, b_s)
    qk = _each(lambda x, y: _dot3(x, y, 1, 1), q, k_s)
    m = _each(lambda x: -strict * x[:c], qb)
    pb = _each(lambda x: incl * x[c:], qb)
    lkv = _each(lambda x, y: _dot3(strict * x[:c], y, 1, 0), qk, v)
    pkv = _each(lambda x, y: _dot3(incl * x[c:], y, 1, 0), qk, v)
    a = _each(lambda x, y: jnp.concatenate([x, y], axis=1), kk_s, lkv)
    steps = max(1, (c - 1).bit_length())
    for i in range(steps):
        a = _each(lambda x, y: y + _dot3(x, y, 1, 0), m, a)
        if i + 1 < steps:
            m = _each(lambda x: _dot3(x, x, 1, 0), m)
    ry = _each(lambda x, y, z, w: jnp.concatenate([x, y], axis=1) - _dot3(z, w, 1, 0), r_s, pkv, pb, a)
    w_end = _each(lambda x: x[c - 1:c, :], w_t)
    gu_low = _each(lambda x, y, z: _dot3(x, y * z, 0, 0), a, b_s, w_end)
    g = _each(lambda x, y: jnp.where(eye, jnp.broadcast_to(x, (n, n)), 0.0) - y[:n], w_end, gu_low)
    u = _each(lambda x, y, z, w: _dot3(x, y * z, 0, 0) - w[n:], v, k_s, w_end, gu_low)
    return g, u, _each(lambda x: x[:, :n], ry), _each(lambda x: x[:, n:], ry)


def _wkv_blocks(h, nchunk):
    return (_tile(h, (4, 2, 1)), _tile(nchunk, (4, 2, 1))), (h, _tile(nchunk, (4, 2, 1)))


def _wkv_fwd(r, lw, k, v, kk, b, *, name):
    h, t, n = r.shape
    c = WKV_CHUNK
    nchunk = t // c
    (hb, cb), (hs, cs) = _wkv_blocks(h, nchunk)

    pairs = [(i, j) for i in range(hb) for j in range(cb)]

    def local_body(*refs):
        ins, (g_ref, u_ref, r2_ref, y0_ref) = refs[:6], refs[6:]
        g, u, r2, y0 = _wkv_local(*[[ref[i, pl.ds(j * c, c)] for i, j in pairs] for ref in ins])
        for idx, (i, j) in enumerate(pairs):
            g_ref[i, j] = g[idx]
            u_ref[i, j] = u[idx]
            r2_ref[i, pl.ds(j * c, c)] = r2[idx]
            y0_ref[i, pl.ds(j * c, c)] = y0[idx]

    seq = pl.BlockSpec((hb, cb * c, n), lambda i, j: (i, j, 0))
    mat = pl.BlockSpec((hb, cb, n, n), lambda i, j: (i, j, 0, 0))
    gm, um, r2, y0 = pl.pallas_call(
        local_body, name=name + "_local", grid=(h // hb, nchunk // cb), in_specs=[seq] * 6, out_specs=[mat, mat, seq, seq],
        out_shape=[jax.ShapeDtypeStruct((h, nchunk, n, n), F32)] * 2 + [jax.ShapeDtypeStruct((h, t, n), F32)] * 2,
        compiler_params=_cparams(("parallel", "parallel")),
    )(r, lw, k, v, kk, b)

    def state_body(g_ref, u_ref, r2_ref, y0_ref, y_ref, st_ref, s_ref):
        @pl.when(pl.program_id(0) == 0)
        def _():
            s_ref[...] = jnp.zeros_like(s_ref)

        s = [s_ref[i] for i in range(hs)]
        for j in range(cs):
            rows = pl.ds(j * c, c)
            for i in range(hs):
                st_ref[i, j] = s[i]
                y_ref[i, rows] = _dot3(r2_ref[i, rows], s[i], 1, 1) + y0_ref[i, rows]
            s = [_dot3(s[i], g_ref[i, j], 1, 0) + u_ref[i, j] for i in range(hs)]
        for i in range(hs):
            s_ref[i] = s[i]

    seq = pl.BlockSpec((hs, cs * c, n), lambda j: (0, j, 0))
    mat = pl.BlockSpec((hs, cs, n, n), lambda j: (0, j, 0, 0))
    y, states = pl.pallas_call(
        state_body, name=name + "_state", grid=(nchunk // cs,), in_specs=[mat, mat, seq, seq], out_specs=[seq, mat],
        out_shape=[jax.ShapeDtypeStruct((h, t, n), F32), jax.ShapeDtypeStruct((h, nchunk, n, n), F32)],
        scratch_shapes=[pltpu.VMEM((hs, n, n), F32)], compiler_params=_cparams(("arbitrary",)),
    )(gm, um, r2, y0)
    return y, (states, gm, r2)


def _wkv_bwd(r, lw, k, v, kk, b, saved, dy, *, name):
    states, gm, r2 = saved
    h, t, n = r.shape
    c = WKV_CHUNK
    nchunk = t // c
    (hb, cb), (hs, cs) = _wkv_blocks(h, nchunk)
    nsteps = nchunk // cs

    def state_body(g_ref, r2_ref, st_ref, dy_ref, dg_ref, du_ref, dr2_ref, ds_ref):
        @pl.when(pl.program_id(0) == 0)
        def _():
            ds_ref[...] = jnp.zeros_like(ds_ref)

        ds = [ds_ref[i] for i in range(hs)]
        for j in reversed(range(cs)):
            rows = pl.ds(j * c, c)
            for i in range(hs):
                s0 = st_ref[i, j]
                du_ref[i, j] = ds[i]
                dg_ref[i, j] = _dot3(s0, ds[i], 0, 0)
                dr2_ref[i, rows] = _dot3(dy_ref[i, rows], s0, 1, 0)
            ds = [_dot3(dy_ref[i, rows], r2_ref[i, rows], 0, 0) + _dot3(ds[i], g_ref[i, j], 1, 1) for i in range(hs)]
        for i in range(hs):
            ds_ref[i] = ds[i]

    seq = pl.BlockSpec((hs, cs * c, n), lambda j: (0, nsteps - 1 - j, 0))
    mat = pl.BlockSpec((hs, cs, n, n), lambda j: (0, nsteps - 1 - j, 0, 0))
    dg, du, dr2 = pl.pallas_call(
        state_body, name=name + "_state", grid=(nsteps,), in_specs=[mat, seq, mat, seq], out_specs=[mat, mat, seq],
        out_shape=[jax.ShapeDtypeStruct((h, nchunk, n, n), F32)] * 2 + [jax.ShapeDtypeStruct((h, t, n), F32)],
        scratch_shapes=[pltpu.VMEM((hs, n, n), F32)], compiler_params=_cparams(("arbitrary",)),
    )(gm, r2, states, dy)

    pairs = [(i, j) for i in range(hb) for j in range(cb)]

    def local_body(*refs):
        ins, (dg_ref, du_ref, dr2_ref, dy_ref), out_refs = refs[:6], refs[6:10], refs[10:]
        _, vjp = jax.vjp(_wkv_local, *[[ref[i, pl.ds(j * c, c)] for i, j in pairs] for ref in ins])
        grads = vjp(([dg_ref[i, j] for i, j in pairs], [du_ref[i, j] for i, j in pairs],
                     [dr2_ref[i, pl.ds(j * c, c)] for i, j in pairs], [dy_ref[i, pl.ds(j * c, c)] for i, j in pairs]))
        for o_ref, gr in zip(out_refs, grads):
            for idx, (i, j) in enumerate(pairs):
                o_ref[i, pl.ds(j * c, c)] = gr[idx]

    seq = pl.BlockSpec((hb, cb * c, n), lambda i, j: (i, j, 0))
    mat = pl.BlockSpec((hb, cb, n, n), lambda i, j: (i, j, 0, 0))
    return pl.pallas_call(
        local_body, name=name + "_local", grid=(h // hb, nchunk // cb), in_specs=[seq] * 6 + [mat, mat, seq, seq],
        out_specs=[seq] * 6, out_shape=[jax.ShapeDtypeStruct((h, t, n), F32)] * 6,
        compiler_params=_cparams(("parallel", "parallel")),
    )(r, lw, k, v, kk, b, dg, du, dr2, dy)


class _Dims:
    pass


def _make_dims(x, p, w):
    m = _Dims()
    m.t, m.d = x.shape[-2], x.shape[-1]
    m.nl = w["ln_mix"].shape[0]
    m.dl = w["conv_a_b"].shape[1]
    m.hl = w["lru_wx"].shape[1]
    m.dr = w["rwkv_w0"].shape[1]
    m.h, m.n = w["rwkv_rk"].shape[1], w["rwkv_rk"].shape[2]
    m.lw, m.la, m.lg, m.lv = (w[k].shape[1] for k in ("rwkv_w2", "rwkv_a2", "rwkv_g2", "rwkv_v2"))
    m.nsh = w["mu_shift"].shape[1]
    m.ff = w["conv_f_b"].shape[1]
    m.ple = p.shape[-1]
    m.din = 2 * m.dl + m.nsh
    m.lz = _round_up(m.lw + m.la + m.lg + m.lv, LANES_V7X)
    m.zw = _round_up(2 * m.dl + 3 * m.dr + m.lz, 512)
    m.zs = m.zw - 2 * m.dl
    m.tr = _tile(m.t, (256, 128, 64, 32, 16, 8))
    m.trb = _tile(m.t, (128, 64, 32, 16, 8))
    m.tcs = _tile(m.zs, (512, 256, 128))
    assert (3 * m.dr) % m.lz == 0 and (2 * m.dl) % m.tcs == 0 and m.t % WKV_CHUNK == 0
    assert m.nsh == 3 * m.dr + m.lw + m.la + m.lg
    return m


def _to_heads(m, a):
    return jnp.transpose(a.reshape(m.t, m.h, m.n), (1, 0, 2))


def _from_heads(m, a):
    return jnp.transpose(a, (1, 0, 2)).reshape(m.t, m.dr)


def _norm_fwd(m, h, g, name):
    return _stage_fwd(_f_norm, [(h, m.d, 0)], [(g, 0)], [], [(m.d, BF16)], axis=0, tile=m.tr, rows=m.t, name=name)[0]


def _norm_bwd(m, h, g, du, dres, name):
    return _stage_bwd(_f_norm_res, [(h, m.d, 0)], [(g, 0)], [], [(du, m.d, 0), (dres, m.d, 0)], [(0, m.d, F32)],
                      axis=0, tile=m.tr, rows=m.t, name=name)


def _rwkv_pre_operands(m, w, i, sv, v_first_zs, with_r):
    zs = sv["zs"]
    tiled = ([(zs, m.dr, 0)] if with_r else []) + [(zs, m.dr, 1), (zs, m.dr, 2), (zs, m.lz, 3 * m.dr // m.lz)]
    params = [(w["rwkv_w0"][i:i + 1], 0), (w["w2p"][i], 0), (w["rwkv_a0"][i:i + 1], 0), (w["a2p"][i], 0),
              (w["g2p"][i], 0), (w["rwkv_kk"][i:i + 1], 0), (w["rwkv_ka"][i:i + 1], 0)]
    if i > 0:
        tiled.append((v_first_zs, m.dr, 2))
        params += [(w["rwkv_v0"][i - 1:i], 0), (w["v2p"][i - 1], 0)]
    return tiled, params


def _rwkv_post_operands(m, w, i, sv):
    tiled = [(sv["y"], m.dr, 0), (sv["zs"], m.dr, 0), (sv["k2"], m.dr, 0), (sv["v2"], m.dr, 0), (sv["g"], m.dr, 0)]
    params = [(w["rwkv_lnx_w"][i:i + 1], 0), (w["rwkv_lnx_b"][i:i + 1], 0), (w["rk"][i], 0)]
    return tiled, params


def _lru_gate_params(w, i):
    return [(w["lru_wx"][i], 0), (w["lru_wa"][i], 0), (w["lru_bx"][i:i + 1], 0), (w["lru_ba"][i:i + 1], 0),
            (w["lru_lambda"][i:i + 1], 0)]


def _layer_fwd(m, w, i, h, p_bf, v_first_zs):
    sv = {"h": h}
    t, dl, dr = m.t, m.dl, m.dr
    sv["u1"] = _norm_fwd(m, h, w["ln_mix"][i:i + 1], "norm_mix")
    z = sv["z"] = _mm(sv["u1"], w["wcat"][i], name="mm_in")
    off = 2 * dl // m.tcs
    sv["zs"] = _stage_fwd(_f_shiftmix, [(z, None, off)], [(w["mu_pad"][i], off)], [], [(m.zs, F32)],
                          axis=1, tile=m.tcs, rows=t, name="shiftmix")[0]
    tca = _tile(dl, (512, 256, 128))
    sv["xb"] = _stage_fwd(_f_conv, [(z, None, 0)], [(w["conv_a_w"][i], 0), (w["conv_a_b"][i:i + 1], 0)], [],
                          [(dl, F32)], axis=1, tile=tca, rows=t, name="conv_a")[0]
    sv["a"], b_in = _stage_fwd(_make_f_lru_gates(m.hl), [(sv["xb"], dl, 0)], _lru_gate_params(w, i), [],
                               [(dl, F32), (dl, F32)], axis=0, tile=m.tr, rows=t, name="lru_gates")
    sv["hl"] = _lru_scan(sv["a"], b_in, name="lru_scan")
    out_a = _stage_fwd(_f_lru_out, [(sv["hl"], dl, 0), (z, dl, 1)], [(w["lru_norm"][i:i + 1], 0)], [],
                       [(dl, BF16)], axis=0, tile=m.tr, rows=t, name="lru_out")[0]
    tiled, params = _rwkv_pre_operands(m, w, i, sv, v_first_zs, False)
    pre = _stage_fwd(_make_f_rwkv_pre(i > 0), tiled, params, [w["bb"]], [(dr, F32)] * 6,
                     axis=0, tile=m.tr, rows=t, name="rwkv_pre")
    sv["logw"], sv["k2"], sv["v2"], sv["kk"], sv["b"], sv["g"] = pre
    heads = [_to_heads(m, a) for a in (sv["zs"][:, :dr], sv["logw"], sv["k2"], sv["v2"], sv["kk"], sv["b"])]
    y_h, sv["states"] = _wkv_fwd(*heads, name="wkv_fwd")
    sv["y"] = _from_heads(m, y_h)
    tiled, params = _rwkv_post_operands(m, w, i, sv)
    out_b = _stage_fwd(_make_f_rwkv_post(m.n), tiled, params, [w["bb"]], [(dr, BF16)],
                       axis=0, tile=m.tr, rows=t, name="rwkv_post")[0]
    sv["cat"] = jnp.concatenate([out_a, out_b], axis=1)
    h2 = sv["h2"] = _mm(sv["cat"], w["w_o"][i], res=h, name="mm_o")
    sv["u2"] = _norm_fwd(m, h2, w["ln_ffn"][i:i + 1], "norm_ffn")
    sv["gpre"] = _mm(sv["u2"], w["w_gate"][i], name="mm_gate")
    sv["up"] = _mm(sv["u2"], w["w_up"][i], name="mm_up")
    tcf = _tile(m.ff, (512, 256, 128))
    sv["act"] = _stage_fwd(_f_ffn_act, [(sv["gpre"], None, 0), (sv["up"], None, 0)],
                           [(w["conv_f_w"][i], 0), (w["conv_f_b"][i:i + 1], 0)], [], [(m.ff, BF16)],
                           axis=1, tile=tcf, rows=t, name="ffn_act")[0]
    h3 = sv["h3"] = _mm(sv["act"], w["w_down"][i], res=h2, name="mm_down")
    sv["u3"] = _norm_fwd(m, h3, w["ln_ple"][i:i + 1], "norm_ple")
    sv["eg"] = _mm(sv["u3"], w["w_ple_gate"][i], name="mm_pgate")
    sv["ep"] = _mm(p_bf, w["w_ple_proj"][i], name="mm_pproj")
    h4 = _stage_fwd(_f_ple, [(h3, m.d, 0), (sv["eg"], m.d, 0), (sv["ep"], m.d, 0)], [(w["ln_ple_post"][i:i + 1], 0)],
                    [], [(m.d, F32)], axis=0, tile=m.tr, rows=t, name="ple")[0]
    return h4, sv


def _layer_bwd(m, w, i, dh4, sv, p_bf, v_first_zs, dvf_in):
    t, d, dl, dr = m.t, m.d, m.dl, m.dr
    g = {}
    deg, dep, g["ln_ple_post"] = _stage_bwd(
        _f_ple, [(sv["h3"], d, 0), (sv["eg"], d, 0), (sv["ep"], d, 0)], [(w["ln_ple_post"][i:i + 1], 0)], [],
        [(dh4, d, 0)], [(1, d, BF16), (2, d, BF16)], axis=0, tile=m.tr, rows=t, name="ple_bwd")
    du3 = _mm(deg, w["w_ple_gate"][i], tb=True, name="mm_pgate_dx")
    g["w_ple_gate"] = _mm(sv["u3"], deg, ta=True, name="mm_pgate_dw")
    g["w_ple_proj"] = _mm(p_bf, dep, ta=True, name="mm_pproj_dw")
    dh3, g["ln_ple"] = _norm_bwd(m, sv["h3"], w["ln_ple"][i:i + 1], du3, dh4, "norm_ple_bwd")
    dh3_bf = dh3.astype(BF16)
    dact = _mm(dh3_bf, w["w_down"][i], tb=True, name="mm_down_dx")
    g["w_down"] = _mm(sv["act"], dh3_bf, ta=True, name="mm_down_dw")
    tcf = _tile(m.ff, (512, 256, 128))
    dgpre, dup, g["conv_f_w"], g["conv_f_b"] = _stage_bwd(
        _f_ffn_act, [(sv["gpre"], None, 0), (sv["up"], None, 0)], [(w["conv_f_w"][i], 0), (w["conv_f_b"][i:i + 1], 0)],
        [], [(dact, 0)], [(0, m.ff, BF16), (1, m.ff, BF16)], axis=1, tile=tcf, rows=t, ncols=m.ff, name="ffn_act_bwd")
    du2 = _mm(dgpre, w["w_gate"][i], tb=True, name="mm_gate_dx")
    du2 = _mm(dup, w["w_up"][i], tb=True, res=du2, name="mm_up_dx")
    g["w_gate"] = _mm(sv["u2"], dgpre, ta=True, name="mm_gate_dw")
    g["w_up"] = _mm(sv["u2"], dup, ta=True, name="mm_up_dw")
    dh2, g["ln_ffn"] = _norm_bwd(m, sv["h2"], w["ln_ffn"][i:i + 1], du2, dh3, "norm_ffn_bwd")
    dh2_bf = dh2.astype(BF16)
    dcat = _mm(dh2_bf, w["w_o"][i], tb=True, name="mm_o_dx")
    g["w_o"] = _mm(sv["cat"], dh2_bf, ta=True, name="mm_o_dw")
    tiled, params = _rwkv_post_operands(m, w, i, sv)
    dy, dr_a, dk2_a, dv2_a, dg, g["rwkv_lnx_w"], g["rwkv_lnx_b"], g["rk"] = _stage_bwd(
        _make_f_rwkv_post(m.n), tiled, params, [w["bb"]], [(dcat, dr, dl // dr)], [(j, dr, F32) for j in range(5)],
        axis=0, tile=m.trb, rows=t, name="rwkv_post_bwd")
    heads = [_to_heads(m, a) for a in (sv["zs"][:, :dr], sv["logw"], sv["k2"], sv["v2"], sv["kk"], sv["b"])]
    dwkv = _wkv_bwd(*heads, sv["states"], _to_heads(m, dy), name="wkv_bwd")
    dr_b, dlw, dk2_b, dv2_b, dkk, db = [_from_heads(m, a) for a in dwkv]
    tiled, params = _rwkv_pre_operands(m, w, i, sv, v_first_zs, True)
    v_cots = [dv2_a, dv2_b] + ([dvf_in] if dvf_in is not None else [])
    cots = [(c, dr, 0) for c in [dr_a, dr_b, dlw, dk2_a, dk2_b] + v_cots + [dkk, db, dg]]
    ntil = len(tiled)
    dtiled = [(0, dr, F32), (1, dr, F32), (2, dr, F32), (3, m.lz, F32)] + ([(4, dr, F32)] if i > 0 else [])
    res = _stage_bwd(_make_f_rwkv_pre(i > 0, len(v_cots)), tiled, params, [w["bb"]], cots, dtiled,
                     axis=0, tile=m.trb, rows=t, name="rwkv_pre_bwd")
    d_r, d_k, d_v, d_lz = res[:4]
    dvf_out = res[4] if i > 0 else None
    pg = res[ntil:]
    g["rwkv_w0"], g["w2p"], g["rwkv_a0"], g["a2p"], g["g2p"], g["rwkv_kk"], g["rwkv_ka"] = pg[:7]
    if i > 0:
        g["rwkv_v0"], g["v2p"] = pg[7:9]
    dzs = jnp.concatenate([d_r, d_k, d_v, d_lz, jnp.zeros((t, m.zs - 3 * dr - m.lz), F32)], axis=1)
    off = 2 * dl // m.tcs
    dzr, g["mu_pad"] = _stage_bwd(_f_shiftmix, [(sv["z"], None, off)], [(w["mu_pad"][i], off)], [], [(dzs, 0)],
                                  [(0, m.zs, BF16)], axis=1, tile=m.tcs, rows=t, ncols=m.zs, name="shiftmix_bwd")
    dhl, dya, g["lru_norm"] = _stage_bwd(
        _f_lru_out, [(sv["hl"], dl, 0), (sv["z"], dl, 1)], [(w["lru_norm"][i:i + 1], 0)], [], [(dcat, dl, 0)],
        [(0, dl, F32), (1, dl, BF16)], axis=0, tile=m.tr, rows=t, name="lru_out_bwd")
    da, db_in = _lru_scan_bwd(sv["a"], sv["hl"], dhl, name="lru_scan_bwd")
    dxb, g["lru_wx"], g["lru_wa"], g["lru_bx"], g["lru_ba"], g["lru_lambda"] = _stage_bwd(
        _make_f_lru_gates(m.hl), [(sv["xb"], dl, 0)], _lru_gate_params(w, i), [], [(da, dl, 0), (db_in, dl, 0)],
        [(0, dl, F32)], axis=0, tile=m.tr, rows=t, name="lru_gates_bwd")
    tca = _tile(dl, (512, 256, 128))
    dxa, g["conv_a_w"], g["conv_a_b"] = _stage_bwd(
        _f_conv, [(sv["z"], None, 0)], [(w["conv_a_w"][i], 0), (w["conv_a_b"][i:i + 1], 0)], [], [(dxb, 0)],
        [(0, dl, BF16)], axis=1, tile=tca, rows=t, ncols=dl, name="conv_a_bwd")
    dz = jnp.concatenate([dxa, dya, dzr], axis=1)
    du1 = _mm(dz, w["wcat"][i], tb=True, name="mm_in_dx")
    g["wcat"] = _mm(sv["u1"], dz, ta=True, name="mm_in_dw")
    dh, g["ln_mix"] = _norm_bwd(m, sv["h"], w["ln_mix"][i:i + 1], du1, dh2, "norm_mix_bwd")
    return dh, g, dvf_out


def _loss_head(m, h, g, tgt):
    tile, d = m.tr, m.d

    def body(h_ref, g_ref, t_ref, loss_ref, dh_ref, dg_ref):
        def f(hv, gv):
            err = _rms(hv, gv) - t_ref[...]
            return 0.5 * jnp.sum(jnp.mean(err * err, axis=-1))

        val, vjp = jax.vjp(f, h_ref[...], g_ref[...])
        dh, dg = vjp(jnp.ones((), F32))
        dh_ref[...] = dh

        @pl.when(pl.program_id(0) == 0)
        def _():
            dg_ref[...] = jnp.zeros_like(dg_ref)
            loss_ref[...] = jnp.zeros_like(loss_ref)

        dg_ref[...] += dg
        loss_ref[...] += jnp.full(loss_ref.shape, val, F32)

    row = pl.BlockSpec((tile, d), lambda i: (i, 0))
    return pl.pallas_call(
        body, name="loss_head", grid=(m.t // tile,),
        in_specs=[row, pl.BlockSpec((1, d), lambda i: (0, 0)), row],
        out_specs=[pl.BlockSpec((1, LANES_V7X), lambda i: (0, 0)), row, pl.BlockSpec((1, d), lambda i: (0, 0))],
        out_shape=[jax.ShapeDtypeStruct((1, LANES_V7X), F32), jax.ShapeDtypeStruct((m.t, d), F32),
                   jax.ShapeDtypeStruct((1, d), F32)],
        compiler_params=_cparams(("arbitrary",)),
    )(h, g, tgt)


def _local_step(m, w, x, p, tgt):
    h = x
    saved = []
    p_bf = p.astype(BF16)
    for i in range(m.nl):
        h, sv = _layer_fwd(m, w, i, h, p_bf[i], saved[0]["zs"] if i > 0 else None)
        saved.append(sv)
    loss_row, dh, d_ln_final = _loss_head(m, h, w["ln_final"], tgt)
    grads = [None] * m.nl
    dvf = None
    for i in reversed(range(m.nl)):
        dh, grads[i], dvf_i = _layer_bwd(m, w, i, dh, saved[i], p_bf[i], saved[0]["zs"] if i > 0 else None,
                                         dvf if i == 0 else None)
        if i > 0:
            dvf = dvf_i if dvf is None else dvf + dvf_i
    return loss_row, dh, grads, d_ln_final


_BIG = ("w_o", "w_gate", "w_up", "w_down", "w_ple_gate", "w_ple_proj")


def _lora_rows(m):
    o1 = m.lw
    o2 = o1 + m.la
    o3 = o2 + m.lg
    return {"w2p": (0, o1), "a2p": (o1, o2), "g2p": (o2, o3), "v2p": (o3, o3 + m.lv)}


def _prepare_weights(m, wf):
    w = {k: v for k, v in wf.items() if k not in _BIG and k not in ("w_in", "w_in_vres")}
    for k in _BIG:
        w[k] = wf[k].astype(BF16)
    nl = m.nl
    vres = jnp.concatenate([jnp.zeros((1, m.d, m.lv), BF16), wf["w_in_vres"].astype(BF16)], axis=0)
    pad = jnp.zeros((nl, m.d, m.zw - m.din - m.lv), BF16)
    w["wcat"] = jnp.concatenate([wf["w_in"].astype(BF16), vres, pad], axis=2)
    mu_v = jnp.concatenate([jnp.zeros((1, m.lv), F32), wf["mu_shift_vres"]], axis=0)
    w["mu_pad"] = jnp.concatenate([jnp.zeros((nl, 2 * m.dl), F32), wf["mu_shift"], mu_v,
                                   jnp.zeros((nl, m.zw - m.din - m.lv), F32)], axis=1)[:, None, :]
    rows = _lora_rows(m)
    for name, src in (("w2p", "rwkv_w2"), ("a2p", "rwkv_a2"), ("g2p", "rwkv_g2"), ("v2p", "rwkv_v2")):
        lo, hi = rows[name]
        a = wf[src]
        w[name] = jnp.concatenate([jnp.zeros((a.shape[0], lo, m.dr), F32), a, jnp.zeros((a.shape[0], m.lz - hi, m.dr), F32)],
                                  axis=1)
    w["rk"] = wf["rwkv_rk"].reshape(nl, 1, m.dr)
    w["ln_final"] = wf["ln_final"].reshape(1, m.d)
    head = jnp.arange(m.dr, dtype=jnp.int32) // m.n
    w["bb"] = (head[:, None] == head[None, :]).astype(BF16)
    return w


def _unpack_grads(m, grads, d_ln_final):
    nl = m.nl
    out = {}

    def stack(key):
        return jnp.stack([grads[i][key] for i in range(nl)], axis=0)

    for k in _BIG + ("conv_a_w", "conv_f_w", "lru_wx", "lru_wa"):
        out[k] = stack(k)
    for k in ("ln_mix", "conv_a_b", "lru_bx", "lru_ba", "lru_lambda", "lru_norm", "rwkv_w0", "rwkv_a0", "rwkv_kk",
              "rwkv_ka", "rwkv_lnx_w", "rwkv_lnx_b", "ln_ffn", "conv_f_b", "ln_ple", "ln_ple_post"):
        out[k] = stack(k)[:, 0, :]
    wcat = stack("wcat")
    out["w_in"] = wcat[:, :, :m.din]
    out["w_in_vres"] = wcat[1:, :, m.din:m.din + m.lv]
    mu = stack("mu_pad")[:, 0, :]
    out["mu_shift"] = mu[:, :m.nsh]
    out["mu_shift_vres"] = mu[1:, m.nsh:m.nsh + m.lv]
    rows = _lora_rows(m)
    for name, dst in (("w2p", "rwkv_w2"), ("a2p", "rwkv_a2"), ("g2p", "rwkv_g2")):
        lo, hi = rows[name]
        out[dst] = stack(name)[:, lo:hi, :]
    lo, hi = rows["v2p"]
    out["rwkv_v2"] = jnp.stack([grads[i]["v2p"] for i in range(1, nl)], axis=0)[:, lo:hi, :]
    out["rwkv_v0"] = jnp.stack([grads[i]["rwkv_v0"] for i in range(1, nl)], axis=0)[:, 0, :]
    out["rwkv_rk"] = stack("rk").reshape(nl, m.h, m.n)
    out["ln_final"] = d_ln_final.reshape(m.d)
    return out


_ANY = pl.BlockSpec(memory_space=pl.ANY)


def _position():
    return lax.axis_index("x"), lax.axis_index("y"), lax.axis_index("c")


def _other_chips(x, y):
    return [(1 - x, y), (x, 1 - y), (1 - x, 1 - y)]


def _all_gather_xy(blob, *, name):
    def body(in_ref, out_ref, send_sems, recv_sems, local_sem):
        x, y, c = _position()
        mine = pltpu.make_async_copy(in_ref, out_ref.at[2 * x + y], local_sem)
        mine.start()
        sends = []
        for k, (px, py) in enumerate(_other_chips(x, y)):
            cp = pltpu.make_async_remote_copy(src_ref=in_ref, dst_ref=out_ref.at[2 * x + y], send_sem=send_sems.at[k],
                                              recv_sem=recv_sems.at[k], device_id=(px, py, c), device_id_type=MESH)
            cp.start()
            sends.append(cp)
        for k, (px, py) in enumerate(_other_chips(x, y)):
            pltpu.make_async_remote_copy(src_ref=in_ref, dst_ref=out_ref.at[2 * px + py], send_sem=send_sems.at[k],
                                         recv_sem=recv_sems.at[k], device_id=(px, py, c), device_id_type=MESH).wait_recv()
        for cp in sends:
            cp.wait_send()
        mine.wait()

    return pl.pallas_call(
        body, name=name, in_specs=[_ANY], out_specs=_ANY,
        out_shape=jax.ShapeDtypeStruct((N_XY,) + blob.shape, blob.dtype),
        scratch_shapes=[pltpu.SemaphoreType.DMA((3,)), pltpu.SemaphoreType.DMA((3,)), pltpu.SemaphoreType.DMA(())],
    )(blob)


def _pair_send_half(g, *, name):
    nq, r, wd = g.shape
    half = r // 2

    def body(g_ref, out_ref, send_sem, recv_sem):
        x, y, c = _position()
        cp = pltpu.make_async_remote_copy(src_ref=g_ref.at[:, pl.ds((1 - c) * half, half), :], dst_ref=out_ref,
                                          send_sem=send_sem, recv_sem=recv_sem, device_id=(x, y, 1 - c), device_id_type=MESH)
        cp.start()
        cp.wait()

    return pl.pallas_call(
        body, name=name, in_specs=[_ANY], out_specs=_ANY, out_shape=jax.ShapeDtypeStruct((nq, half, wd), g.dtype),
        scratch_shapes=[pltpu.SemaphoreType.DMA(()), pltpu.SemaphoreType.DMA(())],
    )(g)


def _pair_sum(g, got, c_idx, *, name):
    nq, r, wd = g.shape
    half = r // 2
    tr = _tile(half, (256, 128, 64, 32, 16, 8))
    nb = half // tr

    def body(c_ref, g_ref, got_ref, o_ref):
        o_ref[...] = (g_ref[...] + got_ref[...]).astype(o_ref.dtype)

    grid_spec = pltpu.PrefetchScalarGridSpec(
        num_scalar_prefetch=1, grid=(nq, nb),
        in_specs=[pl.BlockSpec((1, tr, wd), lambda q, j, c_ref: (q, c_ref[0] * nb + j, 0)),
                  pl.BlockSpec((1, tr, wd), lambda q, j, c_ref: (q, j, 0))],
        out_specs=pl.BlockSpec((1, tr, wd), lambda q, j, c_ref: (q, j, 0)))
    return pl.pallas_call(body, name=name, grid_spec=grid_spec, out_shape=jax.ShapeDtypeStruct((nq, half, wd), BF16),
                          compiler_params=_cparams(("arbitrary", "arbitrary")))(c_idx, g, got)


def _exchange_xy(pb, *, name):
    def body(in_ref, out_ref, send_sems, recv_sems, local_sem):
        x, y, c = _position()
        me = 2 * x + y
        mine = pltpu.make_async_copy(in_ref.at[me], out_ref.at[me], local_sem)
        mine.start()
        sends = []
        for k, (px, py) in enumerate(_other_chips(x, y)):
            cp = pltpu.make_async_remote_copy(src_ref=in_ref.at[2 * px + py], dst_ref=out_ref.at[me], send_sem=send_sems.at[k],
                                              recv_sem=recv_sems.at[k], device_id=(px, py, c), device_id_type=MESH)
            cp.start()
            sends.append(cp)
        for k, (px, py) in enumerate(_other_chips(x, y)):
            pltpu.make_async_remote_copy(src_ref=in_ref.at[me], dst_ref=out_ref.at[2 * px + py], send_sem=send_sems.at[k],
                                         recv_sem=recv_sems.at[k], device_id=(px, py, c), device_id_type=MESH).wait_recv()
        for cp in sends:
            cp.wait_send()
        mine.wait()

    return pl.pallas_call(
        body, name=name, in_specs=[_ANY], out_specs=_ANY, out_shape=jax.ShapeDtypeStruct(pb.shape, pb.dtype),
        scratch_shapes=[pltpu.SemaphoreType.DMA((3,)), pltpu.SemaphoreType.DMA((3,)), pltpu.SemaphoreType.DMA(())],
    )(pb)


def _chip_sum(parts, *, name):
    nq, r, wd = parts.shape
    tr = _tile(r, (256, 128, 64, 32, 16, 8))

    def body(p_ref, o_ref):
        acc = p_ref[0].astype(F32)
        for q in range(1, nq):
            acc = acc + p_ref[q].astype(F32)
        o_ref[...] = acc

    return pl.pallas_call(body, name=name, grid=(r // tr,), in_specs=[pl.BlockSpec((nq, tr, wd), lambda j: (0, j, 0))],
                          out_specs=pl.BlockSpec((tr, wd), lambda j: (j, 0)), out_shape=jax.ShapeDtypeStruct((r, wd), F32),
                          compiler_params=_cparams(("arbitrary",)))(parts)


def _pair_gather(rhalf, *, name):
    half, wd = rhalf.shape

    def body(in_ref, out_ref, send_sem, recv_sem, local_sem):
        x, y, c = _position()
        mine = pltpu.make_async_copy(in_ref, out_ref.at[pl.ds(c * half, half), :], local_sem)
        mine.start()
        cp = pltpu.make_async_remote_copy(src_ref=in_ref, dst_ref=out_ref.at[pl.ds(c * half, half), :], send_sem=send_sem,
                                          recv_sem=recv_sem, device_id=(x, y, 1 - c), device_id_type=MESH)
        cp.start()
        pltpu.make_async_remote_copy(src_ref=in_ref, dst_ref=out_ref.at[pl.ds((1 - c) * half, half), :], send_sem=send_sem,
                                     recv_sem=recv_sem, device_id=(x, y, 1 - c), device_id_type=MESH).wait_recv()
        cp.wait_send()
        mine.wait()

    return pl.pallas_call(
        body, name=name, in_specs=[_ANY], out_specs=_ANY, out_shape=jax.ShapeDtypeStruct((2 * half, wd), rhalf.dtype),
        scratch_shapes=[pltpu.SemaphoreType.DMA(()), pltpu.SemaphoreType.DMA(()), pltpu.SemaphoreType.DMA(())],
    )(rhalf)


def _reduce_to_shard(g, c_idx, tag):
    got = _pair_send_half(g, name="rs_pair_send_" + tag)
    pb = _pair_sum(g, got, c_idx, name="rs_pair_sum_" + tag)
    parts = _exchange_xy(pb, name="rs_exchange_" + tag)
    rhalf = _chip_sum(parts, name="rs_chip_sum_" + tag)
    return _pair_gather(rhalf, name="rs_pair_gather_" + tag)


def _all_reduce_small(vec, *, name):
    r, wd = vec.shape

    def body(in_ref, out_ref, slots, send_sems, recv_sems):
        x, y, c = _position()
        me = 4 * x + 2 * y + c
        flips = [(fx, fy, fc) for fx in (0, 1) for fy in (0, 1) for fc in (0, 1) if fx + fy + fc]
        peers = [(1 - x if fx else x, 1 - y if fy else y, 1 - c if fc else c) for fx, fy, fc in flips]
        sends = []
        for k, peer in enumerate(peers):
            cp = pltpu.make_async_remote_copy(src_ref=in_ref, dst_ref=slots.at[me], send_sem=send_sems.at[k],
                                              recv_sem=recv_sems.at[k], device_id=peer, device_id_type=MESH)
            cp.start()
            sends.append(cp)
        slots[me] = in_ref[...]
        for k, (px, py, pc) in enumerate(peers):
            pltpu.make_async_remote_copy(src_ref=in_ref, dst_ref=slots.at[4 * px + 2 * py + pc], send_sem=send_sems.at[k],
                                         recv_sem=recv_sems.at[k], device_id=(px, py, pc), device_id_type=MESH).wait_recv()
        for cp in sends:
            cp.wait_send()
        acc = slots[0]
        for q in range(1, N_DEV):
            acc = acc + slots[q]
        out_ref[...] = acc

    vm = pl.BlockSpec(memory_space=pltpu.VMEM)
    return pl.pallas_call(
        body, name=name, in_specs=[vm], out_specs=vm, out_shape=jax.ShapeDtypeStruct((r, wd), F32),
        scratch_shapes=[pltpu.VMEM((N_DEV, r, wd), F32), pltpu.SemaphoreType.DMA((N_DEV - 1,)),
                        pltpu.SemaphoreType.DMA((N_DEV - 1,))],
        compiler_params=_cparams(),
    )(vec)


def _adamw(w, g, m, v, *, name):
    r, wd = w.shape
    tr = _tile(r, (256, 128, 64, 32, 16, 8))

    def body(w_ref, g_ref, m_ref, v_ref, d_ref, m_out, v_out):
        gv = g_ref[...]
        m_new = ADAM_B1 * m_ref[...] + (1.0 - ADAM_B1) * gv
        v_new = ADAM_B2 * v_ref[...] + (1.0 - ADAM_B2) * (gv * gv)
        m_hat = m_new / (1.0 - ADAM_B1 ** ADAM_STEP)
        v_hat = v_new / (1.0 - ADAM_B2 ** ADAM_STEP)
        d_ref[...] = -ADAM_LR * (m_hat / (jnp.sqrt(v_hat) + ADAM_EPS) + ADAM_WD * w_ref[...])
        m_out[...] = m_new
        v_out[...] = v_new

    spec = pl.BlockSpec((tr, wd), lambda j: (j, 0))
    return pl.pallas_call(body, name=name, grid=(r // tr,), in_specs=[spec] * 4, out_specs=[spec] * 3,
                          out_shape=[jax.ShapeDtypeStruct((r, wd), F32)] * 3, compiler_params=_cparams(("arbitrary",)))(w, g, m, v)


_WEIGHTS = ("ln_mix", "w_in", "w_in_vres", "mu_shift", "mu_shift_vres", "conv_a_w", "conv_a_b", "lru_wx", "lru_bx", "lru_wa",
            "lru_ba", "lru_lambda", "lru_norm", "rwkv_w0", "rwkv_w2", "rwkv_a0", "rwkv_a2", "rwkv_v0", "rwkv_v2", "rwkv_g2",
            "rwkv_kk", "rwkv_ka", "rwkv_rk", "rwkv_lnx_w", "rwkv_lnx_b", "w_o", "ln_ffn", "w_gate", "w_up", "conv_f_w",
            "conv_f_b", "w_down", "ln_ple", "w_ple_gate", "w_ple_proj", "ln_ple_post", "ln_final")
_SHARD_AXIS = {"w_in": 2, "w_in_vres": 1, "conv_a_w": 2, "lru_wx": 2, "lru_wa": 2, "rwkv_w2": 2, "rwkv_a2": 2, "rwkv_v2": 2,
               "rwkv_g2": 2, "w_o": 1, "w_gate": 2, "w_up": 2, "conv_f_w": 2, "w_down": 1, "w_ple_gate": 1, "w_ple_proj": 2}
_BIG_SHARDED = ("w_in",) + _BIG
_SMALL_SHARDED = tuple(k for k in _WEIGHTS if k in _SHARD_AXIS and k not in _BIG_SHARDED)
_REPLICATED = tuple(k for k in _WEIGHTS if k not in _SHARD_AXIS)
PACK_WIDTH = 512


def _to_shards(g, axis):
    n = g.shape[axis] // N_XY
    return jnp.moveaxis(g.reshape(g.shape[:axis] + (N_XY, n) + g.shape[axis + 1:]), axis, 0)


def _from_shards(s, axis):
    s = jnp.moveaxis(s, 0, axis)
    return s.reshape(s.shape[:axis] + (N_XY * s.shape[axis + 1],) + s.shape[axis + 2:])


def _pack(arrs, lead, width, row_mult):
    lead_shape = arrs[0].shape[:lead]
    flat = jnp.concatenate([a.reshape(lead_shape + (-1,)) for a in arrs], axis=-1)
    n = flat.shape[-1]
    total = _round_up(n, width * row_mult)
    flat = jnp.pad(flat, [(0, 0)] * lead + [(0, total - n)])
    return flat.reshape(lead_shape + (total // width, width))


def _unpack(packed, shapes):
    flat = packed.reshape(-1)
    out, o = [], 0
    for s in shapes:
        n = 1
        for dim in s:
            n *= dim
        out.append(flat[o:o + n].reshape(s))
        o += n
    return out


def _as2d(a):
    return a.reshape(-1, a.shape[-1])


def kernel(x, p, ln_mix, w_in, w_in_vres, mu_shift, mu_shift_vres, conv_a_w, conv_a_b, lru_wx, lru_bx, lru_wa, lru_ba, lru_lambda, lru_norm, rwkv_w0, rwkv_w2, rwkv_a0, rwkv_a2, rwkv_v0, rwkv_v2, rwkv_g2, rwkv_kk, rwkv_ka, rwkv_rk, rwkv_lnx_w, rwkv_lnx_b, w_o, ln_ffn, w_gate, w_up, conv_f_w, conv_f_b, w_down, ln_ple, w_ple_gate, w_ple_proj, ln_ple_post, ln_final, loss_target, m_ln_mix, m_w_in, m_w_in_vres, m_mu_shift, m_mu_shift_vres, m_conv_a_w, m_conv_a_b, m_lru_wx, m_lru_bx, m_lru_wa, m_lru_ba, m_lru_lambda, m_lru_norm, m_rwkv_w0, m_rwkv_w2, m_rwkv_a0, m_rwkv_a2, m_rwkv_v0, m_rwkv_v2, m_rwkv_g2, m_rwkv_kk, m_rwkv_ka, m_rwkv_rk, m_rwkv_lnx_w, m_rwkv_lnx_b, m_w_o, m_ln_ffn, m_w_gate, m_w_up, m_conv_f_w, m_conv_f_b, m_w_down, m_ln_ple, m_w_ple_gate, m_w_ple_proj, m_ln_ple_post, m_ln_final, v_ln_mix, v_w_in, v_w_in_vres, v_mu_shift, v_mu_shift_vres, v_conv_a_w, v_conv_a_b, v_lru_wx, v_lru_bx, v_lru_wa, v_lru_ba, v_lru_lambda, v_lru_norm, v_rwkv_w0, v_rwkv_w2, v_rwkv_a0, v_rwkv_a2, v_rwkv_v0, v_rwkv_v2, v_rwkv_g2, v_rwkv_kk, v_rwkv_ka, v_rwkv_rk, v_rwkv_lnx_w, v_rwkv_lnx_b, v_w_o, v_ln_ffn, v_w_gate, v_w_up, v_conv_f_w, v_conv_f_b, v_w_down, v_ln_ple, v_w_ple_gate, v_w_ple_proj, v_ln_ple_post, v_ln_final):
    a = dict(locals())
    x2, p, tgt = a["x"][0], a["p"][:, 0], a["loss_target"][0]
    c_idx = lax.axis_index("c").astype(jnp.int32).reshape(1)

    wf = {k: a[k] for k in _REPLICATED}
    for k in _BIG_SHARDED:
        wf[k] = _from_shards(_all_gather_xy(a[k].astype(BF16), name="ag_" + k), _SHARD_AXIS[k])
    small_shapes = [a[k].shape for k in _SMALL_SHARDED]
    got = _all_gather_xy(_pack([a[k] for k in _SMALL_SHARDED], 0, PACK_WIDTH, 8), name="ag_small")
    pieces = [_unpack(got[q], small_shapes) for q in range(N_XY)]
    for j, k in enumerate(_SMALL_SHARDED):
        wf[k] = _from_shards(jnp.stack([pieces[q][j] for q in range(N_XY)], axis=0), _SHARD_AXIS[k])

    m = _make_dims(x2, p, wf)
    loss_row, dx, grads, d_ln_final = _local_step(m, _prepare_weights(m, wf), x2, p, tgt)
    gfull = _unpack_grads(m, grads, d_ln_final)
    loss = lax.psum(loss_row[0, 0], ("x", "y", "c"))

    gred = {}
    for k in _BIG_SHARDED:
        gs = _to_shards(gfull[k], _SHARD_AXIS[k])
        gred[k] = _reduce_to_shard(gs.reshape(N_XY, -1, gs.shape[-1]), c_idx, k).reshape(gs.shape[1:])
    gs = _pack([_to_shards(gfull[k], _SHARD_AXIS[k]) for k in _SMALL_SHARDED], 1, PACK_WIDTH, 32)
    g_small = _reduce_to_shard(gs, c_idx, "small")
    rep_shapes = [a[k].shape for k in _REPLICATED]
    g_rep = _all_reduce_small(_pack([gfull[k] for k in _REPLICATED], 0, LANES_V7X, 8), name="ar_replicated")

    delta, new_m, new_v = {}, {}, {}
    for k in _BIG_SHARDED:
        res = _adamw(_as2d(a[k]), _as2d(gred[k]), _as2d(a["m_" + k]), _as2d(a["v_" + k]), name="adamw_" + k)
        delta[k], new_m[k], new_v[k] = (r.reshape(a[k].shape) for r in res)
    for names, shapes, g_packed, width, mult, tag in ((_SMALL_SHARDED, small_shapes, g_small, PACK_WIDTH, 32, "small"),
                                                      (_REPLICATED, rep_shapes, g_rep, LANES_V7X, 8, "replicated")):
        packs = [_pack([a[pre + k] for k in names], 0, width, mult) for pre in ("", "m_", "v_")]
        res = _adamw(packs[0], g_packed, packs[1], packs[2], name="adamw_" + tag)
        for dst, r in zip((gred, delta, new_m, new_v), [g_packed] + list(res)):
            dst.update(zip(names, _unpack(r, shapes)))
    return (loss, dx[None], *[gred[k] for k in _WEIGHTS], *[delta[k] for k in _WEIGHTS],
            *[new_m[k] for k in _WEIGHTS], *[new_v[k] for k in _WEIGHTS])
```

```python
import functools

import jax
import jax.numpy as jnp
from jax import lax
from jax.experimental import pallas as pl
from jax.experimental.pallas import tpu as pltpu

F32 = jnp.float32
BF16 = jnp.bfloat16
HIGHEST = lax.Precision.HIGHEST
MESH = pl.DeviceIdType.MESH

RMS_EPS = 1e-6
LNX_EPS = 64e-5
LRU_C = 8.0
ADAM_LR = 0.001
ADAM_B1 = 0.9
ADAM_B2 = 0.999
ADAM_EPS = 1e-08
ADAM_WD = 0.01
ADAM_STEP = 10

LANES_V7X = 128
VMEM_LIMIT_V7X = 60 * 1024 * 1024
WKV_CHUNK = 16
N_XY = 4
N_DEV = 8


def _cparams(sem=None, **kw):
    if sem is not None:
        kw["dimension_semantics"] = sem
    return pltpu.CompilerParams(vmem_limit_bytes=VMEM_LIMIT_V7X, **kw)


def _tile(dim, prefs):
    for t in prefs:
        if dim % t == 0:
            return t
    return dim


def _round_up(n, m):
    return (n + m - 1) // m * m


def _mm(a, b, *, ta=False, tb=False, res=None, out_dtype=F32, name):
    if ta:
        kdim, m = a.shape
    else:
        m, kdim = a.shape
    n = b.shape[0] if tb else b.shape[1]
    assert (b.shape[1] if tb else b.shape[0]) == kdim
    tm = _tile(m, (2048, 1024, 512, 256, 128))
    tn = _tile(n, (512, 256, 128))
    tk = _tile(kdim, (1024, 512, 256, 128))
    nk = kdim // tk
    a_spec = pl.BlockSpec((tk, tm), lambda i, j, k: (k, i)) if ta else pl.BlockSpec((tm, tk), lambda i, j, k: (i, k))
    b_spec = pl.BlockSpec((tn, tk), lambda i, j, k: (j, k)) if tb else pl.BlockSpec((tk, tn), lambda i, j, k: (k, j))
    o_spec = pl.BlockSpec((tm, tn), lambda i, j, k: (i, j))
    dn = (((0 if ta else 1,), (1 if tb else 0,)), ((), ()))
    has_res = res is not None

    def body(*refs):
        if has_res:
            a_ref, b_ref, r_ref, o_ref, acc_ref = refs
        else:
            a_ref, b_ref, o_ref, acc_ref = refs
        k = pl.program_id(2)

        @pl.when(k == 0)
        def _():
            acc_ref[...] = jnp.zeros_like(acc_ref)

        acc_ref[...] += lax.dot_general(a_ref[...], b_ref[...], dn, preferred_element_type=F32)

        @pl.when(k == nk - 1)
        def _():
            acc = acc_ref[...]
            if has_res:
                acc = acc + r_ref[...].astype(F32)
            o_ref[...] = acc.astype(out_dtype)

    ins = [a, b] + ([res] if has_res else [])
    in_specs = [a_spec, b_spec] + ([o_spec] if has_res else [])
    return pl.pallas_call(
        body, name=name, grid=(m // tm, n // tn, nk), in_specs=in_specs, out_specs=o_spec,
        out_shape=jax.ShapeDtypeStruct((m, n), out_dtype), scratch_shapes=[pltpu.VMEM((tm, tn), F32)],
        compiler_params=_cparams(("parallel", "parallel", "arbitrary")),
    )(*ins)


def _stage_specs(axis, tile, tiled, params, consts, rows):
    specs = []
    for arr, width, cblk in tiled:
        if axis == 0:
            specs.append(pl.BlockSpec((tile, width), functools.partial(lambda i, c: (i, c), c=cblk)))
        else:
            specs.append(pl.BlockSpec((rows, tile), functools.partial(lambda i, c: (0, i + c), c=cblk)))
    for arr, cblk in params:
        if axis == 0:
            specs.append(pl.BlockSpec(arr.shape, functools.partial(lambda i, nd: (0,) * nd, nd=arr.ndim)))
        else:
            specs.append(pl.BlockSpec((arr.shape[0], tile), functools.partial(lambda i, c: (0, i + c), c=cblk)))
    for arr in consts:
        specs.append(pl.BlockSpec(arr.shape, functools.partial(lambda i, nd: (0,) * nd, nd=arr.ndim)))
    return specs


def _stage_fwd(fn, tiled, params, consts, outs, *, axis, tile, rows, name):
    nt, npar, nc = len(tiled), len(params), len(consts)
    ntiles = (rows // tile) if axis == 0 else (outs[0][0] // tile)

    def body(*refs):
        ins = refs[: nt + npar + nc]
        orefs = refs[nt + npar + nc:]
        vals = [r[...].astype(F32) for r in ins[: nt + npar]] + [r[...] for r in ins[nt + npar:]]
        ctx = pl.program_id(0) * tile
        res = fn(ctx, *vals)
        for o_ref, o in zip(orefs, res):
            o_ref[...] = o.astype(o_ref.dtype)

    if axis == 0:
        out_specs = [pl.BlockSpec((tile, w), lambda i: (i, 0)) for w, _ in outs]
    else:
        out_specs = [pl.BlockSpec((rows, tile), lambda i: (0, i)) for w, _ in outs]
    res = pl.pallas_call(
        body, name=name, grid=(ntiles,),
        in_specs=_stage_specs(axis, tile, tiled, params, consts, rows), out_specs=out_specs,
        out_shape=[jax.ShapeDtypeStruct((rows, w), dt) for w, dt in outs],
        compiler_params=_cparams(("arbitrary",)),
    )(*[t[0] for t in tiled], *[p[0] for p in params], *consts)
    return res


def _stage_bwd(fn, tiled, params, consts, cots, dtiled, *, axis, tile, rows, name, ncols=None):
    nt, npar, nc, nco = len(tiled), len(params), len(consts), len(cots)
    ntiles = (rows // tile) if axis == 0 else (ncols // tile)
    didx = [d[0] for d in dtiled]

    def body(*refs):
        ins = refs[: nt + npar + nc]
        crefs = refs[nt + npar + nc: nt + npar + nc + nco]
        orefs = refs[nt + npar + nc + nco:]
        vals = [r[...].astype(F32) for r in ins[: nt + npar]] + [r[...] for r in ins[nt + npar:]]
        ctx = pl.program_id(0) * tile

        def g(*dv):
            full = list(vals)
            for j, ix in enumerate(didx):
                full[ix] = dv[j]
            for j in range(npar):
                full[nt + j] = dv[len(didx) + j]
            return tuple(fn(ctx, *full))

        prim = [vals[ix] for ix in didx] + [vals[nt + j] for j in range(npar)]
        _, vjp = jax.vjp(g, *prim)
        grads = vjp(tuple(c[...].astype(F32) for c in crefs))
        for j in range(len(didx)):
            orefs[j][...] = grads[j].astype(orefs[j].dtype)
        for j in range(npar):
            o_ref = orefs[len(didx) + j]
            gp = grads[len(didx) + j]
            if axis == 0:
                @pl.when(pl.program_id(0) == 0)
                def _(o_ref=o_ref):
                    o_ref[...] = jnp.zeros_like(o_ref)

                o_ref[...] += gp
            else:
                o_ref[...] = gp

    if axis == 0:
        cot_specs = [pl.BlockSpec((tile, w), functools.partial(lambda i, c: (i, c), c=cb)) for _, w, cb in cots]
        out_specs = [pl.BlockSpec((tile, w), lambda i: (i, 0)) for _, w, _ in dtiled]
        out_specs += [pl.BlockSpec(p.shape, functools.partial(lambda i, nd: (0,) * nd, nd=p.ndim)) for p, _ in params]
        out_shape = [jax.ShapeDtypeStruct((rows, w), dt) for _, w, dt in dtiled]
        out_shape += [jax.ShapeDtypeStruct(p.shape, F32) for p, _ in params]
    else:
        cot_specs = [pl.BlockSpec((rows, tile), functools.partial(lambda i, c: (0, i + c), c=cb)) for _, cb in cots]
        out_specs = [pl.BlockSpec((rows, tile), lambda i: (0, i)) for _ in dtiled]
        out_specs += [pl.BlockSpec((p.shape[0], tile), lambda i: (0, i)) for p, _ in params]
        out_shape = [jax.ShapeDtypeStruct((rows, w), dt) for _, w, dt in dtiled]
        out_shape += [jax.ShapeDtypeStruct((p.shape[0], ncols), F32) for p, _ in params]
    return pl.pallas_call(
        body, name=name, grid=(ntiles,),
        in_specs=_stage_specs(axis, tile, tiled, params, consts, rows) + cot_specs, out_specs=out_specs,
        out_shape=out_shape, compiler_params=_cparams(("arbitrary",)),
    )(*[t[0] for t in tiled], *[p[0] for p in params], *consts, *[c[0] for c in cots])


def _rms(x, g):
    return x * lax.rsqrt(jnp.mean(x * x, axis=-1, keepdims=True) + RMS_EPS) * g


def _row_mask(x, k, first):
    t = lax.broadcasted_iota(jnp.int32, x.shape, 0)
    keep = (t >= k) if first else (t < x.shape[0] - k)
    return jnp.where(keep, x, 0.0)


@functools.partial(jax.custom_vjp, nondiff_argnums=(1,))
def _shift_down(x, k):
    return _row_mask(pltpu.roll(x, k, 0), k, True)


def _shift_down_fwd(x, k):
    return _shift_down(x, k), None


def _shift_down_bwd(k, _, g):
    return (_row_mask(pltpu.roll(g, g.shape[0] - k, 0), k, False),)


_shift_down.defvjp(_shift_down_fwd, _shift_down_bwd)


def _dwconv(x, w, b):
    kw = w.shape[0]
    out = x * w[kw - 1:kw] + b
    for j in range(kw - 1):
        out = out + _shift_down(x, kw - 1 - j) * w[j:j + 1]
    return out


def _f_norm(ctx, x, g):
    return (_rms(x, g),)


def _f_norm_res(ctx, x, g):
    return (_rms(x, g), x)


def _f_shiftmix(ctx, z, mu):
    return (z + (_shift_down(z, 1) - z) * mu,)


def _f_conv(ctx, x, w, b):
    return (_dwconv(x, w, b),)


def _f_ffn_act(ctx, gpre, up, w, b):
    return (jax.nn.gelu(_dwconv(gpre, w, b)) * up,)


def _make_f_lru_gates(heads):
    def fn(ctx, xb, wx, wa, bx, ba, lam):
        blk = xb.shape[1] // heads
        px, pa = [], []
        for h in range(heads):
            xh = xb[:, h * blk:(h + 1) * blk]
            px.append(jnp.dot(xh, wx[h], preferred_element_type=F32))
            pa.append(jnp.dot(xh, wa[h], preferred_element_type=F32))
        px = px[0] if heads == 1 else jnp.concatenate(px, axis=1)
        pa = pa[0] if heads == 1 else jnp.concatenate(pa, axis=1)
        gate_x = jax.nn.sigmoid(px + bx)
        gate_a = jax.nn.sigmoid(pa + ba)
        log_a = -LRU_C * gate_a * jax.nn.softplus(-lam)
        a = jnp.exp(log_a)
        mult = jnp.sqrt(1.0 - jnp.exp(2.0 * log_a))
        t = ctx + lax.broadcasted_iota(jnp.int32, xb.shape, 0)
        mult = jnp.where(t == 0, 1.0, mult)
        return a, xb * gate_x * mult

    return fn


def _f_lru_out(ctx, hl, ya, g):
    return (_rms(hl * jax.nn.gelu(ya), g),)


def _headsum_3pass(x, bb):
    hi = x.astype(BF16)
    r1 = x - hi.astype(F32)
    mid = r1.astype(BF16)
    lo = (r1 - mid.astype(F32)).astype(BF16)
    return (jnp.dot(hi, bb, preferred_element_type=F32) + jnp.dot(mid, bb, preferred_element_type=F32)
            + jnp.dot(lo, bb, preferred_element_type=F32))


@jax.custom_vjp
def _headsum(x, bb):
    return _headsum_3pass(x, bb)


def _headsum_fwd(x, bb):
    return _headsum_3pass(x, bb), bb


def _headsum_bwd(bb, g):
    return _headsum_3pass(g, bb), None


_headsum.defvjp(_headsum_fwd, _headsum_bwd)


def _make_f_rwkv_pre(has_vres, v_uses=0):
    def fn(ctx, *args):
        if v_uses:
            r, args = args[0], args[1:]
        if has_vres:
            k, v, lz, vf, w0, w2, a0, a2, g2, kkw, ka, v0, v2, bb = args
        else:
            k, v, lz, w0, w2, a0, a2, g2, kkw, ka, bb = args
        w_log = -jax.nn.softplus(-(w0 + jnp.dot(jnp.tanh(lz), w2, preferred_element_type=F32))) - 0.5
        logw = -jnp.exp(w_log)
        a = jax.nn.sigmoid(a0 + jnp.dot(lz, a2, preferred_element_type=F32))
        g = jnp.dot(jax.nn.sigmoid(lz), g2, preferred_element_type=F32)
        if has_vres:
            v = v + (vf - v) * jax.nn.sigmoid(v0 + jnp.dot(lz, v2, preferred_element_type=F32))
        xk = k * kkw
        kk = xk / jnp.maximum(jnp.sqrt(_headsum(xk * xk, bb)), 1e-12)
        k2 = k * (1.0 + (a - 1.0) * ka)
        if v_uses:
            return (r, r, logw, k2, k2) + (v,) * v_uses + (kk, kk * a, g)
        return logw, k2, v, kk, kk * a, g

    return fn


def _make_f_rwkv_post(head_size):
    def fn(ctx, y, r, k2, v2, g, lnw, lnb, rk, bb):
        mean = _headsum(y, bb) / head_size
        d = y - mean
        var = _headsum(d * d, bb) / head_size
        yn = d * lax.rsqrt(var + LNX_EPS) * lnw + lnb
        bonus = _headsum(r * k2 * rk, bb) * v2
        return ((yn + bonus) * g,)

    return fn


def _f_ple(ctx, h, eg, ep, g):
    return (h + _rms(jax.nn.sigmoid(eg) * ep, g),)


def _lru_scan(a, b, *, name):
    rows, cols = a.shape
    tc = _tile(cols, (512, 256, 128))

    def body(a_ref, b_ref, h_ref):
        def step(t, carry):
            h = a_ref[pl.ds(t, 1), :] * carry + b_ref[pl.ds(t, 1), :]
            h_ref[pl.ds(t, 1), :] = h
            return h

        lax.fori_loop(0, rows, step, jnp.zeros((1, tc), F32), unroll=8)

    spec = pl.BlockSpec((rows, tc), lambda j: (0, j))
    return pl.pallas_call(body, name=name, grid=(cols // tc,), in_specs=[spec, spec], out_specs=spec,
                          out_shape=jax.ShapeDtypeStruct((rows, cols), F32), compiler_params=_cparams(("arbitrary",)))(a, b)


def _lru_scan_bwd(a, h, dh, *, name):
    rows, cols = a.shape
    tc = _tile(cols, (512, 256, 128))

    def body(a_ref, h_ref, dh_ref, da_ref, db_ref):
        def step(i, carry):
            t = rows - 1 - i
            g = dh_ref[pl.ds(t, 1), :] + carry
            db_ref[pl.ds(t, 1), :] = g
            hp = h_ref[pl.ds(jnp.maximum(t - 1, 0), 1), :]
            da_ref[pl.ds(t, 1), :] = jnp.where(t > 0, g * hp, 0.0)
            return a_ref[pl.ds(t, 1), :] * g

        lax.fori_loop(0, rows, step, jnp.zeros((1, tc), F32), unroll=8)

    spec = pl.BlockSpec((rows, tc), lambda j: (0, j))
    return pl.pallas_call(body, name=name, grid=(cols // tc,), in_specs=[spec] * 3, out_specs=[spec] * 2,
                          out_shape=[jax.ShapeDtypeStruct((rows, cols), F32)] * 2,
                          compiler_params=_cparams(("arbitrary",)))(a, h, dh)


def _split_bf16(x):
    hi = x.astype(BF16)
    return hi, (x - hi.astype(F32)).astype(BF16)


def _dot3_passes(a, b, ca, cb):
    dn = (((ca,), (cb,)), ((), ()))
    ah, al = _split_bf16(a)
    bh, bl = _split_bf16(b)
    return (lax.dot_general(ah, bh, dn, preferred_element_type=F32) + lax.dot_general(al, bh, dn, preferred_element_type=F32)
            + lax.dot_general(ah, bl, dn, preferred_element_type=F32))


@functools.partial(jax.custom_vjp, nondiff_argnums=(2, 3))
def _dot3(a, b, ca, cb):
    return _dot3_passes(a, b, ca, cb)


def _dot3_fwd(a, b, ca, cb):
    return _dot3_passes(a, b, ca, cb), (a, b)


def _dot3_bwd(ca, cb, res, g):
    a, b = res
    fa, fb = 1 - ca, 1 - cb
    da = _dot3_passes(g, b, 1, fb) if ca == 1 else _dot3_passes(b, g, fb, 1)
    db = _dot3_passes(a, g, fa, 0) if cb == 0 else _dot3_passes(g, a, 0, fa)
    return da, db


_dot3.defvjp(_dot3_fwd, _dot3_bwd)


def _each(f, *lists):
    return [f(*t) for t in zip(*lists)]


def _wkv_local(r, lw, k, v, kk, b):
    c, n = r[0].shape
    row = lax.broadcasted_iota(jnp.int32, (c, c), 0)
    col = lax.broadcasted_iota(jnp.int32, (c, c), 1)
    incl = (row >= col).astype(F32)
    strict = (row > col).astype(F32)
    eye = lax.broadcasted_iota(jnp.int32, (n, n), 0) == lax.broadcasted_iota(jnp.int32, (n, n), 1)
    cl = _each(lambda x: _dot3(incl, x, 1, 0), lw)
    w_t = _each(jnp.exp, cl)
    inv_w = _each(lambda x: jnp.exp(-x), cl)
    kk_s = _each(lambda x, y, z: x * jnp.exp(y - z), kk, cl, lw)
    b_s = _each(jnp.multiply, b, inv_w)
    k_s = _each(jnp.multiply, k, inv_w)
    r_s = _each(jnp.multiply, r, w_t)
    q = _each(lambda x, y: jnp.concatenate([x, y], axis=0), kk_s, r_s)
    qb = _each(lambda x, y: _dot3(x, y, 1, 1), q, b_s)
    qk = _each(lambda x, y: _dot3(x, y, 1, 1), q, k_s)
    m = _each(lambda x: -strict * x[:c], qb)
    pb = _each(lambda x: incl * x[c:], qb)
    lkv = _each(lambda x, y: _dot3(strict * x[:c], y, 1, 0), qk, v)
    pkv = _each(lambda x, y: _dot3(incl * x[c:], y, 1, 0), qk, v)
    a = _each(lambda x, y: jnp.concatenate([x, y], axis=1), kk_s, lkv)
    steps = max(1, (c - 1).bit_length())
    for i in range(steps):
        a = _each(lambda x, y: y + _dot3(x, y, 1, 0), m, a)
        if i + 1 < steps:
            m = _each(lambda x: _dot3(x, x, 1, 0), m)
    ry = _each(lambda x, y, z, w: jnp.concatenate([x, y], axis=1) - _dot3(z, w, 1, 0), r_s, pkv, pb, a)
    w_end = _each(lambda x: x[c - 1:c, :], w_t)
    gu_low = _each(lambda x, y, z: _dot3(x, y * z, 0, 0), a, b_s, w_end)
    g = _each(lambda x, y: jnp.where(eye, jnp.broadcast_to(x, (n, n)), 0.0) - y[:n], w_end, gu_low)
    u = _each(lambda x, y, z, w: _dot3(x, y * z, 0, 0) - w[n:], v, k_s, w_end, gu_low)
    return g, u, _each(lambda x: x[:, :n], ry), _each(lambda x: x[:, n:], ry)


def _wkv_blocks(h, nchunk):
    return (_tile(h, (4, 2, 1)), _tile(nchunk, (4, 2, 1))), (h, _tile(nchunk, (4, 2, 1)))


def _wkv_fwd(r, lw, k, v, kk, b, *, name):
    h, t, n = r.shape
    c = WKV_CHUNK
    nchunk = t // c
    (hb, cb), (hs, cs) = _wkv_blocks(h, nchunk)

    pairs = [(i, j) for i in range(hb) for j in range(cb)]

    def local_body(*refs):
        ins, (g_ref, u_ref, r2_ref, y0_ref) = refs[:6], refs[6:]
        g, u, r2, y0 = _wkv_local(*[[ref[i, pl.ds(j * c, c)] for i, j in pairs] for ref in ins])
        for idx, (i, j) in enumerate(pairs):
            g_ref[i, j] = g[idx]
            u_ref[i, j] = u[idx]
            r2_ref[i, pl.ds(j * c, c)] = r2[idx]
            y0_ref[i, pl.ds(j * c, c)] = y0[idx]

    seq = pl.BlockSpec((hb, cb * c, n), lambda i, j: (i, j, 0))
    mat = pl.BlockSpec((hb, cb, n, n), lambda i, j: (i, j, 0, 0))
    gm, um, r2, y0 = pl.pallas_call(
        local_body, name=name + "_local", grid=(h // hb, nchunk // cb), in_specs=[seq] * 6, out_specs=[mat, mat, seq, seq],
        out_shape=[jax.ShapeDtypeStruct((h, nchunk, n, n), F32)] * 2 + [jax.ShapeDtypeStruct((h, t, n), F32)] * 2,
        compiler_params=_cparams(("parallel", "parallel")),
    )(r, lw, k, v, kk, b)

    def state_body(g_ref, u_ref, r2_ref, y0_ref, y_ref, st_ref, s_ref):
        @pl.when(pl.program_id(0) == 0)
        def _():
            s_ref[...] = jnp.zeros_like(s_ref)

        s = [s_ref[i] for i in range(hs)]
        for j in range(cs):
            rows = pl.ds(j * c, c)
            for i in range(hs):
                st_ref[i, j] = s[i]
                y_ref[i, rows] = _dot3(r2_ref[i, rows], s[i], 1, 1) + y0_ref[i, rows]
            s = [_dot3(s[i], g_ref[i, j], 1, 0) + u_ref[i, j] for i in range(hs)]
        for i in range(hs):
            s_ref[i] = s[i]

    seq = pl.BlockSpec((hs, cs * c, n), lambda j: (0, j, 0))
    mat = pl.BlockSpec((hs, cs, n, n), lambda j: (0, j, 0, 0))
    y, states = pl.pallas_call(
        state_body, name=name + "_state", grid=(nchunk // cs,), in_specs=[mat, mat, seq, seq], out_specs=[seq, mat],
        out_shape=[jax.ShapeDtypeStruct((h, t, n), F32), jax.ShapeDtypeStruct((h, nchunk, n, n), F32)],
        scratch_shapes=[pltpu.VMEM((hs, n, n), F32)], compiler_params=_cparams(("arbitrary",)),
    )(gm, um, r2, y0)
    return y, (states, gm, r2)


def _wkv_bwd(r, lw, k, v, kk, b, saved, dy, *, name):
    states, gm, r2 = saved
    h, t, n = r.shape
    c = WKV_CHUNK
    nchunk = t // c
    (hb, cb), (hs, cs) = _wkv_blocks(h, nchunk)
    nsteps = nchunk // cs

    def state_body(g_ref, r2_ref, st_ref, dy_ref, dg_ref, du_ref, dr2_ref, ds_ref):
        @pl.when(pl.program_id(0) == 0)
        def _():
            ds_ref[...] = jnp.zeros_like(ds_ref)

        ds = [ds_ref[i] for i in range(hs)]
        for j in reversed(range(cs)):
            rows = pl.ds(j * c, c)
            for i in range(hs):
                s0 = st_ref[i, j]
                du_ref[i, j] = ds[i]
                dg_ref[i, j] = _dot3(s0, ds[i], 0, 0)
                dr2_ref[i, rows] = _dot3(dy_ref[i, rows], s0, 1, 0)
            ds = [_dot3(dy_ref[i, rows], r2_ref[i, rows], 0, 0) + _dot3(ds[i], g_ref[i, j], 1, 1) for i in range(hs)]
        for i in range(hs):
            ds_ref[i] = ds[i]

    seq = pl.BlockSpec((hs, cs * c, n), lambda j: (0, nsteps - 1 - j, 0))
    mat = pl.BlockSpec((hs, cs, n, n), lambda j: (0, nsteps - 1 - j, 0, 0))
    dg, du, dr2 = pl.pallas_call(
        state_body, name=name + "_state", grid=(nsteps,), in_specs=[mat, seq, mat, seq], out_specs=[mat, mat, seq],
        out_shape=[jax.ShapeDtypeStruct((h, nchunk, n, n), F32)] * 2 + [jax.ShapeDtypeStruct((h, t, n), F32)],
        scratch_shapes=[pltpu.VMEM((hs, n, n), F32)], compiler_params=_cparams(("arbitrary",)),
    )(gm, r2, states, dy)

    pairs = [(i, j) for i in range(hb) for j in range(cb)]

    def local_body(*refs):
        ins, (dg_ref, du_ref, dr2_ref, dy_ref), out_refs = refs[:6], refs[6:10], refs[10:]
        _, vjp = jax.vjp(_wkv_local, *[[ref[i, pl.ds(j * c, c)] for i, j in pairs] for ref in ins])
        grads = vjp(([dg_ref[i, j] for i, j in pairs], [du_ref[i, j] for i, j in pairs],
                     [dr2_ref[i, pl.ds(j * c, c)] for i, j in pairs], [dy_ref[i, pl.ds(j * c, c)] for i, j in pairs]))
        for o_ref, gr in zip(out_refs, grads):
            for idx, (i, j) in enumerate(pairs):
                o_ref[i, pl.ds(j * c, c)] = gr[idx]

    seq = pl.BlockSpec((hb, cb * c, n), lambda i, j: (i, j, 0))
    mat = pl.BlockSpec((hb, cb, n, n), lambda i, j: (i, j, 0, 0))
    return pl.pallas_call(
        local_body, name=name + "_local", grid=(h // hb, nchunk // cb), in_specs=[seq] * 6 + [mat, mat, seq, seq],
        out_specs=[seq] * 6, out_shape=[jax.ShapeDtypeStruct((h, t, n), F32)] * 6,
        compiler_params=_cparams(("parallel", "parallel")),
    )(r, lw, k, v, kk, b, dg, du, dr2, dy)


class _Dims:
    pass


def _make_dims(x, p, w):
    m = _Dims()
    m.t, m.d = x.shape[-2], x.shape[-1]
    m.nl = w["ln_mix"].shape[0]
    m.dl = w["conv_a_b"].shape[1]
    m.hl = w["lru_wx"].shape[1]
    m.dr = w["rwkv_w0"].shape[1]
    m.h, m.n = w["rwkv_rk"].shape[1], w["rwkv_rk"].shape[2]
    m.lw, m.la, m.lg, m.lv = (w[k].shape[1] for k in ("rwkv_w2", "rwkv_a2", "rwkv_g2", "rwkv_v2"))
    m.nsh = w["mu_shift"].shape[1]
    m.ff = w["conv_f_b"].shape[1]
    m.ple = p.shape[-1]
    m.din = 2 * m.dl + m.nsh
    m.lz = _round_up(m.lw + m.la + m.lg + m.lv, LANES_V7X)
    m.zw = _round_up(2 * m.dl + 3 * m.dr + m.lz, 512)
    m.zs = m.zw - 2 * m.dl
    m.tr = _tile(m.t, (256, 128, 64, 32, 16, 8))
    m.trb = _tile(m.t, (128, 64, 32, 16, 8))
    m.tcs = _tile(m.zs, (512, 256, 128))
    assert (3 * m.dr) % m.lz == 0 and (2 * m.dl) % m.tcs == 0 and m.t % WKV_CHUNK == 0
    assert m.nsh == 3 * m.dr + m.lw + m.la + m.lg
    return m


def _to_heads(m, a):
    return jnp.transpose(a.reshape(m.t, m.h, m.n), (1, 0, 2))


def _from_heads(m, a):
    return jnp.transpose(a, (1, 0, 2)).reshape(m.t, m.dr)


def _norm_fwd(m, h, g, name):
    return _stage_fwd(_f_norm, [(h, m.d, 0)], [(g, 0)], [], [(m.d, BF16)], axis=0, tile=m.tr, rows=m.t, name=name)[0]


def _norm_bwd(m, h, g, du, dres, name):
    return _stage_bwd(_f_norm_res, [(h, m.d, 0)], [(g, 0)], [], [(du, m.d, 0), (dres, m.d, 0)], [(0, m.d, F32)],
                      axis=0, tile=m.tr, rows=m.t, name=name)


def _rwkv_pre_operands(m, w, i, sv, v_first_zs, with_r):
    zs = sv["zs"]
    tiled = ([(zs, m.dr, 0)] if with_r else []) + [(zs, m.dr, 1), (zs, m.dr, 2), (zs, m.lz, 3 * m.dr // m.lz)]
    params = [(w["rwkv_w0"][i:i + 1], 0), (w["w2p"][i], 0), (w["rwkv_a0"][i:i + 1], 0), (w["a2p"][i], 0),
              (w["g2p"][i], 0), (w["rwkv_kk"][i:i + 1], 0), (w["rwkv_ka"][i:i + 1], 0)]
    if i > 0:
        tiled.append((v_first_zs, m.dr, 2))
        params += [(w["rwkv_v0"][i - 1:i], 0), (w["v2p"][i - 1], 0)]
    return tiled, params


def _rwkv_post_operands(m, w, i, sv):
    tiled = [(sv["y"], m.dr, 0), (sv["zs"], m.dr, 0), (sv["k2"], m.dr, 0), (sv["v2"], m.dr, 0), (sv["g"], m.dr, 0)]
    params = [(w["rwkv_lnx_w"][i:i + 1], 0), (w["rwkv_lnx_b"][i:i + 1], 0), (w["rk"][i], 0)]
    return tiled, params


def _lru_gate_params(w, i):
    return [(w["lru_wx"][i], 0), (w["lru_wa"][i], 0), (w["lru_bx"][i:i + 1], 0), (w["lru_ba"][i:i + 1], 0),
            (w["lru_lambda"][i:i + 1], 0)]


def _layer_fwd(m, w, i, h, p_bf, v_first_zs):
    sv = {"h": h}
    t, dl, dr = m.t, m.dl, m.dr
    sv["u1"] = _norm_fwd(m, h, w["ln_mix"][i:i + 1], "norm_mix")
    z = sv["z"] = _mm(sv["u1"], w["wcat"][i], name="mm_in")
    off = 2 * dl // m.tcs
    sv["zs"] = _stage_fwd(_f_shiftmix, [(z, None, off)], [(w["mu_pad"][i], off)], [], [(m.zs, F32)],
                          axis=1, tile=m.tcs, rows=t, name="shiftmix")[0]
    tca = _tile(dl, (512, 256, 128))
    sv["xb"] = _stage_fwd(_f_conv, [(z, None, 0)], [(w["conv_a_w"][i], 0), (w["conv_a_b"][i:i + 1], 0)], [],
                          [(dl, F32)], axis=1, tile=tca, rows=t, name="conv_a")[0]
    sv["a"], b_in = _stage_fwd(_make_f_lru_gates(m.hl), [(sv["xb"], dl, 0)], _lru_gate_params(w, i), [],
                               [(dl, F32), (dl, F32)], axis=0, tile=m.tr, rows=t, name="lru_gates")
    sv["hl"] = _lru_scan(sv["a"], b_in, name="lru_scan")
    out_a = _stage_fwd(_f_lru_out, [(sv["hl"], dl, 0), (z, dl, 1)], [(w["lru_norm"][i:i + 1], 0)], [],
                       [(dl, BF16)], axis=0, tile=m.tr, rows=t, name="lru_out")[0]
    tiled, params = _rwkv_pre_operands(m, w, i, sv, v_first_zs, False)
    pre = _stage_fwd(_make_f_rwkv_pre(i > 0), tiled, params, [w["bb"]], [(dr, F32)] * 6,
                     axis=0, tile=m.tr, rows=t, name="rwkv_pre")
    sv["logw"], sv["k2"], sv["v2"], sv["kk"], sv["b"], sv["g"] = pre
    heads = [_to_heads(m, a) for a in (sv["zs"][:, :dr], sv["logw"], sv["k2"], sv["v2"], sv["kk"], sv["b"])]
    y_h, sv["states"] = _wkv_fwd(*heads, name="wkv_fwd")
    sv["y"] = _from_heads(m, y_h)
    tiled, params = _rwkv_post_operands(m, w, i, sv)
    out_b = _stage_fwd(_make_f_rwkv_post(m.n), tiled, params, [w["bb"]], [(dr, BF16)],
                       axis=0, tile=m.tr, rows=t, name="rwkv_post")[0]
    sv["cat"] = jnp.concatenate([out_a, out_b], axis=1)
    h2 = sv["h2"] = _mm(sv["cat"], w["w_o"][i], res=h, name="mm_o")
    sv["u2"] = _norm_fwd(m, h2, w["ln_ffn"][i:i + 1], "norm_ffn")
    sv["gpre"] = _mm(sv["u2"], w["w_gate"][i], name="mm_gate")
    sv["up"] = _mm(sv["u2"], w["w_up"][i], name="mm_up")
    tcf = _tile(m.ff, (512, 256, 128))
    sv["act"] = _stage_fwd(_f_ffn_act, [(sv["gpre"], None, 0), (sv["up"], None, 0)],
                           [(w["conv_f_w"][i], 0), (w["conv_f_b"][i:i + 1], 0)], [], [(m.ff, BF16)],
                           axis=1, tile=tcf, rows=t, name="ffn_act")[0]
    h3 = sv["h3"] = _mm(sv["act"], w["w_down"][i], res=h2, name="mm_down")
    sv["u3"] = _norm_fwd(m, h3, w["ln_ple"][i:i + 1], "norm_ple")
    sv["eg"] = _mm(sv["u3"], w["w_ple_gate"][i], name="mm_pgate")
    sv["ep"] = _mm(p_bf, w["w_ple_proj"][i], name="mm_pproj")
    h4 = _stage_fwd(_f_ple, [(h3, m.d, 0), (sv["eg"], m.d, 0), (sv["ep"], m.d, 0)], [(w["ln_ple_post"][i:i + 1], 0)],
                    [], [(m.d, F32)], axis=0, tile=m.tr, rows=t, name="ple")[0]
    return h4, sv


def _layer_bwd(m, w, i, dh4, sv, p_bf, v_first_zs, dvf_in):
    t, d, dl, dr = m.t, m.d, m.dl, m.dr
    g = {}
    deg, dep, g["ln_ple_post"] = _stage_bwd(
        _f_ple, [(sv["h3"], d, 0), (sv["eg"], d, 0), (sv["ep"], d, 0)], [(w["ln_ple_post"][i:i + 1], 0)], [],
        [(dh4, d, 0)], [(1, d, BF16), (2, d, BF16)], axis=0, tile=m.tr, rows=t, name="ple_bwd")
    du3 = _mm(deg, w["w_ple_gate"][i], tb=True, name="mm_pgate_dx")
    g["w_ple_gate"] = _mm(sv["u3"], deg, ta=True, name="mm_pgate_dw")
    g["w_ple_proj"] = _mm(p_bf, dep, ta=True, name="mm_pproj_dw")
    dh3, g["ln_ple"] = _norm_bwd(m, sv["h3"], w["ln_ple"][i:i + 1], du3, dh4, "norm_ple_bwd")
    dh3_bf = dh3.astype(BF16)
    dact = _mm(dh3_bf, w["w_down"][i], tb=True, name="mm_down_dx")
    g["w_down"] = _mm(sv["act"], dh3_bf, ta=True, name="mm_down_dw")
    tcf = _tile(m.ff, (512, 256, 128))
    dgpre, dup, g["conv_f_w"], g["conv_f_b"] = _stage_bwd(
        _f_ffn_act, [(sv["gpre"], None, 0), (sv["up"], None, 0)], [(w["conv_f_w"][i], 0), (w["conv_f_b"][i:i + 1], 0)],
        [], [(dact, 0)], [(0, m.ff, BF16), (1, m.ff, BF16)], axis=1, tile=tcf, rows=t, ncols=m.ff, name="ffn_act_bwd")
    du2 = _mm(dgpre, w["w_gate"][i], tb=True, name="mm_gate_dx")
    du2 = _mm(dup, w["w_up"][i], tb=True, res=du2, name="mm_up_dx")
    g["w_gate"] = _mm(sv["u2"], dgpre, ta=True, name="mm_gate_dw")
    g["w_up"] = _mm(sv["u2"], dup, ta=True, name="mm_up_dw")
    dh2, g["ln_ffn"] = _norm_bwd(m, sv["h2"], w["ln_ffn"][i:i + 1], du2, dh3, "norm_ffn_bwd")
    dh2_bf = dh2.astype(BF16)
    dcat = _mm(dh2_bf, w["w_o"][i], tb=True, name="mm_o_dx")
    g["w_o"] = _mm(sv["cat"], dh2_bf, ta=True, name="mm_o_dw")
    tiled, params = _rwkv_post_operands(m, w, i, sv)
    dy, dr_a, dk2_a, dv2_a, dg, g["rwkv_lnx_w"], g["rwkv_lnx_b"], g["rk"] = _stage_bwd(
        _make_f_rwkv_post(m.n), tiled, params, [w["bb"]], [(dcat, dr, dl // dr)], [(j, dr, F32) for j in range(5)],
        axis=0, tile=m.trb, rows=t, name="rwkv_post_bwd")
    heads = [_to_heads(m, a) for a in (sv["zs"][:, :dr], sv["logw"], sv["k2"], sv["v2"], sv["kk"], sv["b"])]
    dwkv = _wkv_bwd(*heads, sv["states"], _to_heads(m, dy), name="wkv_bwd")
    dr_b, dlw, dk2_b, dv2_b, dkk, db = [_from_heads(m, a) for a in dwkv]
    tiled, params = _rwkv_pre_operands(m, w, i, sv, v_first_zs, True)
    v_cots = [dv2_a, dv2_b] + ([dvf_in] if dvf_in is not None else [])
    cots = [(c, dr, 0) for c in [dr_a, dr_b, dlw, dk2_a, dk2_b] + v_cots + [dkk, db, dg]]
    ntil = len(tiled)
    dtiled = [(0, dr, F32), (1, dr, F32), (2, dr, F32), (3, m.lz, F32)] + ([(4, dr, F32)] if i > 0 else [])
    res = _stage_bwd(_make_f_rwkv_pre(i > 0, len(v_cots)), tiled, params, [w["bb"]], cots, dtiled,
                     axis=0, tile=m.trb, rows=t, name="rwkv_pre_bwd")
    d_r, d_k, d_v, d_lz = res[:4]
    dvf_out = res[4] if i > 0 else None
    pg = res[ntil:]
    g["rwkv_w0"], g["w2p"], g["rwkv_a0"], g["a2p"], g["g2p"], g["rwkv_kk"], g["rwkv_ka"] = pg[:7]
    if i > 0:
        g["rwkv_v0"], g["v2p"] = pg[7:9]
    dzs = jnp.concatenate([d_r, d_k, d_v, d_lz, jnp.zeros((t, m.zs - 3 * dr - m.lz), F32)], axis=1)
    off = 2 * dl // m.tcs
    dzr, g["mu_pad"] = _stage_bwd(_f_shiftmix, [(sv["z"], None, off)], [(w["mu_pad"][i], off)], [], [(dzs, 0)],
                                  [(0, m.zs, BF16)], axis=1, tile=m.tcs, rows=t, ncols=m.zs, name="shiftmix_bwd")
    dhl, dya, g["lru_norm"] = _stage_bwd(
        _f_lru_out, [(sv["hl"], dl, 0), (sv["z"], dl, 1)], [(w["lru_norm"][i:i + 1], 0)], [], [(dcat, dl, 0)],
        [(0, dl, F32), (1, dl, BF16)], axis=0, tile=m.tr, rows=t, name="lru_out_bwd")
    da, db_in = _lru_scan_bwd(sv["a"], sv["hl"], dhl, name="lru_scan_bwd")
    dxb, g["lru_wx"], g["lru_wa"], g["lru_bx"], g["lru_ba"], g["lru_lambda"] = _stage_bwd(
        _make_f_lru_gates(m.hl), [(sv["xb"], dl, 0)], _lru_gate_params(w, i), [], [(da, dl, 0), (db_in, dl, 0)],
        [(0, dl, F32)], axis=0, tile=m.tr, rows=t, name="lru_gates_bwd")
    tca = _tile(dl, (512, 256, 128))
    dxa, g["conv_a_w"], g["conv_a_b"] = _stage_bwd(
        _f_conv, [(sv["z"], None, 0)], [(w["conv_a_w"][i], 0), (w["conv_a_b"][i:i + 1], 0)], [], [(dxb, 0)],
        [(0, dl, BF16)], axis=1, tile=tca, rows=t, ncols=dl, name="conv_a_bwd")
    dz = jnp.concatenate([dxa, dya, dzr], axis=1)
    du1 = _mm(dz, w["wcat"][i], tb=True, name="mm_in_dx")
    g["wcat"] = _mm(sv["u1"], dz, ta=True, name="mm_in_dw")
    dh, g["ln_mix"] = _norm_bwd(m, sv["h"], w["ln_mix"][i:i + 1], du1, dh2, "norm_mix_bwd")
    return dh, g, dvf_out


def _loss_head(m, h, g, tgt):
    tile, d = m.tr, m.d

    def body(h_ref, g_ref, t_ref, loss_ref, dh_ref, dg_ref):
        def f(hv, gv):
            err = _rms(hv, gv) - t_ref[...]
            return 0.5 * jnp.sum(jnp.mean(err * err, axis=-1))

        val, vjp = jax.vjp(f, h_ref[...], g_ref[...])
        dh, dg = vjp(jnp.ones((), F32))
        dh_ref[...] = dh

        @pl.when(pl.program_id(0) == 0)
        def _():
            dg_ref[...] = jnp.zeros_like(dg_ref)
            loss_ref[...] = jnp.zeros_like(loss_ref)

        dg_ref[...] += dg
        loss_ref[...] += jnp.full(loss_ref.shape, val, F32)

    row = pl.BlockSpec((tile, d), lambda i: (i, 0))
    return pl.pallas_call(
        body, name="loss_head", grid=(m.t // tile,),
        in_specs=[row, pl.BlockSpec((1, d), lambda i: (0, 0)), row],
        out_specs=[pl.BlockSpec((1, LANES_V7X), lambda i: (0, 0)), row, pl.BlockSpec((1, d), lambda i: (0, 0))],
        out_shape=[jax.ShapeDtypeStruct((1, LANES_V7X), F32), jax.ShapeDtypeStruct((m.t, d), F32),
                   jax.ShapeDtypeStruct((1, d), F32)],
        compiler_params=_cparams(("arbitrary",)),
    )(h, g, tgt)


def _local_step(m, w, x, p, tgt):
    h = x
    saved = []
    p_bf = p.astype(BF16)
    for i in range(m.nl):
        h, sv = _layer_fwd(m, w, i, h, p_bf[i], saved[0]["zs"] if i > 0 else None)
        saved.append(sv)
    loss_row, dh, d_ln_final = _loss_head(m, h, w["ln_final"], tgt)
    grads = [None] * m.nl
    dvf = None
    for i in reversed(range(m.nl)):
        dh, grads[i], dvf_i = _layer_bwd(m, w, i, dh, saved[i], p_bf[i], saved[0]["zs"] if i > 0 else None,
                                         dvf if i == 0 else None)
        if i > 0:
            dvf = dvf_i if dvf is None else dvf + dvf_i
    return loss_row, dh, grads, d_ln_final


_BIG = ("w_o", "w_gate", "w_up", "w_down", "w_ple_gate", "w_ple_proj")


def _lora_rows(m):
    o1 = m.lw
    o2 = o1 + m.la
    o3 = o2 + m.lg
    return {"w2p": (0, o1), "a2p": (o1, o2), "g2p": (o2, o3), "v2p": (o3, o3 + m.lv)}


def _prepare_weights(m, wf):
    w = {k: v for k, v in wf.items() if k not in _BIG and k not in ("w_in", "w_in_vres")}
    for k in _BIG:
        w[k] = wf[k].astype(BF16)
    nl = m.nl
    vres = jnp.concatenate([jnp.zeros((1, m.d, m.lv), BF16), wf["w_in_vres"].astype(BF16)], axis=0)
    pad = jnp.zeros((nl, m.d, m.zw - m.din - m.lv), BF16)
    w["wcat"] = jnp.concatenate([wf["w_in"].astype(BF16), vres, pad], axis=2)
    mu_v = jnp.concatenate([jnp.zeros((1, m.lv), F32), wf["mu_shift_vres"]], axis=0)
    w["mu_pad"] = jnp.concatenate([jnp.zeros((nl, 2 * m.dl), F32), wf["mu_shift"], mu_v,
                                   jnp.zeros((nl, m.zw - m.din - m.lv), F32)], axis=1)[:, None, :]
    rows = _lora_rows(m)
    for name, src in (("w2p", "rwkv_w2"), ("a2p", "rwkv_a2"), ("g2p", "rwkv_g2"), ("v2p", "rwkv_v2")):
        lo, hi = rows[name]
        a = wf[src]
        w[name] = jnp.concatenate([jnp.zeros((a.shape[0], lo, m.dr), F32), a, jnp.zeros((a.shape[0], m.lz - hi, m.dr), F32)],
                                  axis=1)
    w["rk"] = wf["rwkv_rk"].reshape(nl, 1, m.dr)
    w["ln_final"] = wf["ln_final"].reshape(1, m.d)
    head = jnp.arange(m.dr, dtype=jnp.int32) // m.n
    w["bb"] = (head[:, None] == head[None, :]).astype(BF16)
    return w


def _unpack_grads(m, grads, d_ln_final):
    nl = m.nl
    out = {}

    def stack(key):
        return jnp.stack([grads[i][key] for i in range(nl)], axis=0)

    for k in _BIG + ("conv_a_w", "conv_f_w", "lru_wx", "lru_wa"):
        out[k] = stack(k)
    for k in ("ln_mix", "conv_a_b", "lru_bx", "lru_ba", "lru_lambda", "lru_norm", "rwkv_w0", "rwkv_a0", "rwkv_kk",
              "rwkv_ka", "rwkv_lnx_w", "rwkv_lnx_b", "ln_ffn", "conv_f_b", "ln_ple", "ln_ple_post"):
        out[k] = stack(k)[:, 0, :]
    wcat = stack("wcat")
    out["w_in"] = wcat[:, :, :m.din]
    out["w_in_vres"] = wcat[1:, :, m.din:m.din + m.lv]
    mu = stack("mu_pad")[:, 0, :]
    out["mu_shift"] = mu[:, :m.nsh]
    out["mu_shift_vres"] = mu[1:, m.nsh:m.nsh + m.lv]
    rows = _lora_rows(m)
    for name, dst in (("w2p", "rwkv_w2"), ("a2p", "rwkv_a2"), ("g2p", "rwkv_g2")):
        lo, hi = rows[name]
        out[dst] = stack(name)[:, lo:hi, :]
    lo, hi = rows["v2p"]
    out["rwkv_v2"] = jnp.stack([grads[i]["v2p"] for i in range(1, nl)], axis=0)[:, lo:hi, :]
    out["rwkv_v0"] = jnp.stack([grads[i]["rwkv_v0"] for i in range(1, nl)], axis=0)[:, 0, :]
    out["rwkv_rk"] = stack("rk").reshape(nl, m.h, m.n)
    out["ln_final"] = d_ln_final.reshape(m.d)
    return out


_ANY = pl.BlockSpec(memory_space=pl.ANY)


def _position():
    return lax.axis_index("x"), lax.axis_index("y"), lax.axis_index("c")


def _other_chips(x, y):
    return [(1 - x, y), (x, 1 - y), (1 - x, 1 - y)]


def _all_gather_xy(blob, *, name):
    def body(in_ref, out_ref, send_sems, recv_sems):
        x, y, c = _position()
        me = 2 * x + y
        chips = _other_chips(x, y)
        sends = []
        for k, (px, py) in enumerate(chips):
            cp = pltpu.make_async_remote_copy(src_ref=in_ref.at[c], dst_ref=out_ref.at[me, c], send_sem=send_sems.at[k],
                                              recv_sem=recv_sems.at[k], device_id=(px, py, c), device_id_type=MESH)
            cp.start()
            sends.append(cp)
        for k, (px, py) in enumerate(chips):
            landed = out_ref.at[2 * px + py, c]
            pltpu.make_async_remote_copy(src_ref=in_ref.at[c], dst_ref=landed, send_sem=send_sems.at[k],
                                         recv_sem=recv_sems.at[k], device_id=(px, py, c), device_id_type=MESH).wait_recv()
            cp = pltpu.make_async_remote_copy(src_ref=landed, dst_ref=landed, send_sem=send_sems.at[3 + k],
                                              recv_sem=recv_sems.at[3 + k], device_id=(x, y, 1 - c), device_id_type=MESH)
            cp.start()
            sends.append(cp)
        for k, (px, py) in enumerate(chips):
            pltpu.make_async_remote_copy(src_ref=in_ref.at[c], dst_ref=out_ref.at[2 * px + py, 1 - c],
                                         send_sem=send_sems.at[3 + k], recv_sem=recv_sems.at[3 + k],
                                         device_id=(x, y, 1 - c), device_id_type=MESH).wait_recv()
        for cp in sends:
            cp.wait_send()

    return pl.pallas_call(
        body, name=name, in_specs=[_ANY], out_specs=_ANY,
        out_shape=jax.ShapeDtypeStruct((N_XY,) + blob.shape, blob.dtype),
        scratch_shapes=[pltpu.SemaphoreType.DMA((6,)), pltpu.SemaphoreType.DMA((6,))],
    )(blob)


def _gather_shards(shard, chip, *, name):
    two = shard.reshape((2, shard.shape[0] // 2) + shard.shape[1:])
    got = lax.dynamic_update_index_in_dim(_all_gather_xy(two, name=name), two, chip, 0)
    return got.reshape((N_XY,) + shard.shape)


def _pair_send_half(g, *, name):
    nq, r, wd = g.shape
    half = r // 2

    def body(g_ref, out_ref, send_sem, recv_sem):
        x, y, c = _position()
        cp = pltpu.make_async_remote_copy(src_ref=g_ref.at[:, pl.ds((1 - c) * half, half), :], dst_ref=out_ref,
                                          send_sem=send_sem, recv_sem=recv_sem, device_id=(x, y, 1 - c), device_id_type=MESH)
        cp.start()
        cp.wait()

    return pl.pallas_call(
        body, name=name, in_specs=[_ANY], out_specs=_ANY, out_shape=jax.ShapeDtypeStruct((nq, half, wd), g.dtype),
        scratch_shapes=[pltpu.SemaphoreType.DMA(()), pltpu.SemaphoreType.DMA(())],
    )(g)


def _pair_sum(g, got, pos, *, name):
    nq, r, wd = g.shape
    half = r // 2
    tr = _tile(half, (256, 128, 64, 32, 16, 8))
    nb = half // tr

    def body(c_ref, g_ref, got_ref, o_ref):
        o_ref[...] = (g_ref[...] + got_ref[...]).astype(o_ref.dtype)

    grid_spec = pltpu.PrefetchScalarGridSpec(
        num_scalar_prefetch=1, grid=(nq, nb),
        in_specs=[pl.BlockSpec((1, tr, wd), lambda q, j, c_ref: (q, c_ref[0] * nb + j, 0)),
                  pl.BlockSpec((1, tr, wd), lambda q, j, c_ref: (q, j, 0))],
        out_specs=pl.BlockSpec((1, tr, wd), lambda q, j, c_ref: (q, j, 0)))
    return pl.pallas_call(body, name=name, grid_spec=grid_spec, out_shape=jax.ShapeDtypeStruct((nq, half, wd), BF16),
                          compiler_params=_cparams(("arbitrary", "arbitrary")))(pos[0], g, got)


def _exchange_xy(pb, *, name):
    def body(in_ref, out_ref, send_sems, recv_sems):
        x, y, c = _position()
        me = 2 * x + y
        sends = []
        for k, (px, py) in enumerate(_other_chips(x, y)):
            cp = pltpu.make_async_remote_copy(src_ref=in_ref.at[2 * px + py], dst_ref=out_ref.at[me], send_sem=send_sems.at[k],
                                              recv_sem=recv_sems.at[k], device_id=(px, py, c), device_id_type=MESH)
            cp.start()
            sends.append(cp)
        for k, (px, py) in enumerate(_other_chips(x, y)):
            pltpu.make_async_remote_copy(src_ref=in_ref.at[me], dst_ref=out_ref.at[2 * px + py], send_sem=send_sems.at[k],
                                         recv_sem=recv_sems.at[k], device_id=(px, py, c), device_id_type=MESH).wait_recv()
        for cp in sends:
            cp.wait_send()

    return pl.pallas_call(
        body, name=name, in_specs=[_ANY], out_specs=_ANY, out_shape=jax.ShapeDtypeStruct(pb.shape, pb.dtype),
        scratch_shapes=[pltpu.SemaphoreType.DMA((3,)), pltpu.SemaphoreType.DMA((3,))],
    )(pb)


def _chip_sum(parts, pb, pos, *, name):
    nq, half, wd = parts.shape
    tr = _tile(half, (256, 128, 64, 32, 16, 8))
    nb = half // tr

    def body(c_ref, x_ref, y_ref, p_ref, own_ref, o_ref):
        chip = 2 * x_ref[0] + y_ref[0]
        own = own_ref[0].astype(F32)
        acc = None
        for q in range(nq):
            term = jnp.where(chip == q, own, p_ref[q].astype(F32))
            acc = term if acc is None else acc + term
        o_ref[...] = acc

    grid_spec = pltpu.PrefetchScalarGridSpec(
        num_scalar_prefetch=3, grid=(nb,),
        in_specs=[pl.BlockSpec((nq, tr, wd), lambda j, c_ref, x_ref, y_ref: (0, j, 0)),
                  pl.BlockSpec((1, tr, wd), lambda j, c_ref, x_ref, y_ref: (2 * x_ref[0] + y_ref[0], j, 0))],
        out_specs=pl.BlockSpec((tr, wd), lambda j, c_ref, x_ref, y_ref: (c_ref[0] * nb + j, 0)))
    return pl.pallas_call(body, name=name, grid_spec=grid_spec, out_shape=jax.ShapeDtypeStruct((2 * half, wd), F32),
                          compiler_params=_cparams(("arbitrary",)))(*pos, parts, pb)


def _pair_gather(full, *, name):
    r, wd = full.shape
    half = r // 2

    def body(in_ref, out_ref, send_sem, recv_sem):
        x, y, c = _position()
        mine = out_ref.at[pl.ds(c * half, half), :]
        cp = pltpu.make_async_remote_copy(src_ref=mine, dst_ref=mine, send_sem=send_sem, recv_sem=recv_sem,
                                          device_id=(x, y, 1 - c), device_id_type=MESH)
        cp.start()
        pltpu.make_async_remote_copy(src_ref=mine, dst_ref=out_ref.at[pl.ds((1 - c) * half, half), :], send_sem=send_sem,
                                     recv_sem=recv_sem, device_id=(x, y, 1 - c), device_id_type=MESH).wait_recv()
        cp.wait_send()

    return pl.pallas_call(
        body, name=name, in_specs=[_ANY], out_specs=_ANY, out_shape=jax.ShapeDtypeStruct(full.shape, full.dtype),
        input_output_aliases={0: 0}, scratch_shapes=[pltpu.SemaphoreType.DMA(()), pltpu.SemaphoreType.DMA(())],
    )(full)


def _reduce_to_shard(g, pos, tag):
    got = _pair_send_half(g, name="rs_pair_send_" + tag)
    pb = _pair_sum(g, got, pos, name="rs_pair_sum_" + tag)
    parts = _exchange_xy(pb, name="rs_exchange_" + tag)
    full = _chip_sum(parts, pb, pos, name="rs_chip_sum_" + tag)
    return _pair_gather(full, name="rs_pair_gather_" + tag)


def _all_reduce_small(vec, *, name):
    r, wd = vec.shape

    def body(in_ref, out_ref, slots, send_sems, recv_sems):
        x, y, c = _position()
        me = 4 * x + 2 * y + c
        flips = [(fx, fy, fc) for fx in (0, 1) for fy in (0, 1) for fc in (0, 1) if fx + fy + fc]
        peers = [(1 - x if fx else x, 1 - y if fy else y, 1 - c if fc else c) for fx, fy, fc in flips]
        sends = []
        for k, peer in enumerate(peers):
            cp = pltpu.make_async_remote_copy(src_ref=in_ref, dst_ref=slots.at[me], send_sem=send_sems.at[k],
                                              recv_sem=recv_sems.at[k], device_id=peer, device_id_type=MESH)
            cp.start()
            sends.append(cp)
        slots[me] = in_ref[...]
        for k, (px, py, pc) in enumerate(peers):
            pltpu.make_async_remote_copy(src_ref=in_ref, dst_ref=slots.at[4 * px + 2 * py + pc], send_sem=send_sems.at[k],
                                         recv_sem=recv_sems.at[k], device_id=(px, py, pc), device_id_type=MESH).wait_recv()
        for cp in sends:
            cp.wait_send()
        acc = slots[0]
        for q in range(1, N_DEV):
            acc = acc + slots[q]
        out_ref[...] = acc

    vm = pl.BlockSpec(memory_space=pltpu.VMEM)
    return pl.pallas_call(
        body, name=name, in_specs=[vm], out_specs=vm, out_shape=jax.ShapeDtypeStruct((r, wd), F32),
        scratch_shapes=[pltpu.VMEM((N_DEV, r, wd), F32), pltpu.SemaphoreType.DMA((N_DEV - 1,)),
                        pltpu.SemaphoreType.DMA((N_DEV - 1,))],
        compiler_params=_cparams(),
    )(vec)


def _adamw(w, g, m, v, *, name):
    r, wd = w.shape
    tr = _tile(r, (256, 128, 64, 32, 16, 8))

    def body(w_ref, g_ref, m_ref, v_ref, d_ref, m_out, v_out):
        gv = g_ref[...]
        m_new = ADAM_B1 * m_ref[...] + (1.0 - ADAM_B1) * gv
        v_new = ADAM_B2 * v_ref[...] + (1.0 - ADAM_B2) * (gv * gv)
        m_hat = m_new / (1.0 - ADAM_B1 ** ADAM_STEP)
        v_hat = v_new / (1.0 - ADAM_B2 ** ADAM_STEP)
        d_ref[...] = -ADAM_LR * (m_hat / (jnp.sqrt(v_hat) + ADAM_EPS) + ADAM_WD * w_ref[...])
        m_out[...] = m_new
        v_out[...] = v_new

    spec = pl.BlockSpec((tr, wd), lambda j: (j, 0))
    return pl.pallas_call(body, name=name, grid=(r // tr,), in_specs=[spec] * 4, out_specs=[spec] * 3,
                          out_shape=[jax.ShapeDtypeStruct((r, wd), F32)] * 3, compiler_params=_cparams(("arbitrary",)))(w, g, m, v)


_WEIGHTS = ("ln_mix", "w_in", "w_in_vres", "mu_shift", "mu_shift_vres", "conv_a_w", "conv_a_b", "lru_wx", "lru_bx", "lru_wa",
            "lru_ba", "lru_lambda", "lru_norm", "rwkv_w0", "rwkv_w2", "rwkv_a0", "rwkv_a2", "rwkv_v0", "rwkv_v2", "rwkv_g2",
            "rwkv_kk", "rwkv_ka", "rwkv_rk", "rwkv_lnx_w", "rwkv_lnx_b", "w_o", "ln_ffn", "w_gate", "w_up", "conv_f_w",
            "conv_f_b", "w_down", "ln_ple", "w_ple_gate", "w_ple_proj", "ln_ple_post", "ln_final")
_SHARD_AXIS = {"w_in": 2, "w_in_vres": 1, "conv_a_w": 2, "lru_wx": 2, "lru_wa": 2, "rwkv_w2": 2, "rwkv_a2": 2, "rwkv_v2": 2,
               "rwkv_g2": 2, "w_o": 1, "w_gate": 2, "w_up": 2, "conv_f_w": 2, "w_down": 1, "w_ple_gate": 1, "w_ple_proj": 2}
_BIG_SHARDED = ("w_in",) + _BIG
_SMALL_SHARDED = tuple(k for k in _WEIGHTS if k in _SHARD_AXIS and k not in _BIG_SHARDED)
_REPLICATED = tuple(k for k in _WEIGHTS if k not in _SHARD_AXIS)
PACK_WIDTH = 512


def _to_shards(g, axis):
    n = g.shape[axis] // N_XY
    return jnp.moveaxis(g.reshape(g.shape[:axis] + (N_XY, n) + g.shape[axis + 1:]), axis, 0)


def _from_shards(s, axis):
    s = jnp.moveaxis(s, 0, axis)
    return s.reshape(s.shape[:axis] + (N_XY * s.shape[axis + 1],) + s.shape[axis + 2:])


def _pack(arrs, lead, width, row_mult):
    lead_shape = arrs[0].shape[:lead]
    flat = jnp.concatenate([a.reshape(lead_shape + (-1,)) for a in arrs], axis=-1)
    n = flat.shape[-1]
    total = _round_up(n, width * row_mult)
    flat = jnp.pad(flat, [(0, 0)] * lead + [(0, total - n)])
    return flat.reshape(lead_shape + (total // width, width))


def _unpack(packed, shapes):
    flat = packed.reshape(-1)
    out, o = [], 0
    for s in shapes:
        n = 1
        for dim in s:
            n *= dim
        out.append(flat[o:o + n].reshape(s))
        o += n
    return out


def _as2d(a):
    return a.reshape(-1, a.shape[-1])


def kernel(x, p, ln_mix, w_in, w_in_vres, mu_shift, mu_shift_vres, conv_a_w, conv_a_b, lru_wx, lru_bx, lru_wa, lru_ba, lru_lambda, lru_norm, rwkv_w0, rwkv_w2, rwkv_a0, rwkv_a2, rwkv_v0, rwkv_v2, rwkv_g2, rwkv_kk, rwkv_ka, rwkv_rk, rwkv_lnx_w, rwkv_lnx_b, w_o, ln_ffn, w_gate, w_up, conv_f_w, conv_f_b, w_down, ln_ple, w_ple_gate, w_ple_proj, ln_ple_post, ln_final, loss_target, m_ln_mix, m_w_in, m_w_in_vres, m_mu_shift, m_mu_shift_vres, m_conv_a_w, m_conv_a_b, m_lru_wx, m_lru_bx, m_lru_wa, m_lru_ba, m_lru_lambda, m_lru_norm, m_rwkv_w0, m_rwkv_w2, m_rwkv_a0, m_rwkv_a2, m_rwkv_v0, m_rwkv_v2, m_rwkv_g2, m_rwkv_kk, m_rwkv_ka, m_rwkv_rk, m_rwkv_lnx_w, m_rwkv_lnx_b, m_w_o, m_ln_ffn, m_w_gate, m_w_up, m_conv_f_w, m_conv_f_b, m_w_down, m_ln_ple, m_w_ple_gate, m_w_ple_proj, m_ln_ple_post, m_ln_final, v_ln_mix, v_w_in, v_w_in_vres, v_mu_shift, v_mu_shift_vres, v_conv_a_w, v_conv_a_b, v_lru_wx, v_lru_bx, v_lru_wa, v_lru_ba, v_lru_lambda, v_lru_norm, v_rwkv_w0, v_rwkv_w2, v_rwkv_a0, v_rwkv_a2, v_rwkv_v0, v_rwkv_v2, v_rwkv_g2, v_rwkv_kk, v_rwkv_ka, v_rwkv_rk, v_rwkv_lnx_w, v_rwkv_lnx_b, v_w_o, v_ln_ffn, v_w_gate, v_w_up, v_conv_f_w, v_conv_f_b, v_w_down, v_ln_ple, v_w_ple_gate, v_w_ple_proj, v_ln_ple_post, v_ln_final):
    a = dict(locals())
    x2, p, tgt = a["x"][0], a["p"][:, 0], a["loss_target"][0]
    chip = 2 * lax.axis_index("x") + lax.axis_index("y")
    pos = tuple(lax.axis_index(ax).astype(jnp.int32).reshape(1) for ax in ("c", "x", "y"))

    wf = {k: a[k] for k in _REPLICATED}
    for k in _BIG_SHARDED:
        wf[k] = _from_shards(_gather_shards(a[k].astype(BF16), chip, name="ag_" + k), _SHARD_AXIS[k])
    small_shapes = [a[k].shape for k in _SMALL_SHARDED]
    got = _gather_shards(_pack([a[k] for k in _SMALL_SHARDED], 0, PACK_WIDTH, 16), chip, name="ag_small")
    pieces = [_unpack(got[q], small_shapes) for q in range(N_XY)]
    for j, k in enumerate(_SMALL_SHARDED):
        wf[k] = _from_shards(jnp.stack([pieces[q][j] for q in range(N_XY)], axis=0), _SHARD_AXIS[k])

    m = _make_dims(x2, p, wf)
    loss_row, dx, grads, d_ln_final = _local_step(m, _prepare_weights(m, wf), x2, p, tgt)
    gfull = _unpack_grads(m, grads, d_ln_final)
    loss = lax.psum(loss_row[0, 0], ("x", "y", "c"))

    gred = {}
    for k in _BIG_SHARDED:
        gs = _to_shards(gfull[k], _SHARD_AXIS[k])
        gred[k] = _reduce_to_shard(gs.reshape(N_XY, -1, gs.shape[-1]), pos, k).reshape(gs.shape[1:])
    gs = _pack([_to_shards(gfull[k], _SHARD_AXIS[k]) for k in _SMALL_SHARDED], 1, PACK_WIDTH, 32)
    g_small = _reduce_to_shard(gs, pos, "small")
    rep_shapes = [a[k].shape for k in _REPLICATED]
    g_rep = _all_reduce_small(_pack([gfull[k] for k in _REPLICATED], 0, LANES_V7X, 8), name="ar_replicated")

    delta, new_m, new_v = {}, {}, {}
    for k in _BIG_SHARDED:
        res = _adamw(_as2d(a[k]), _as2d(gred[k]), _as2d(a["m_" + k]), _as2d(a["v_" + k]), name="adamw_" + k)
        delta[k], new_m[k], new_v[k] = (r.reshape(a[k].shape) for r in res)
    for names, shapes, g_packed, width, mult, tag in ((_SMALL_SHARDED, small_shapes, g_small, PACK_WIDTH, 32, "small"),
                                                      (_REPLICATED, rep_shapes, g_rep, LANES_V7X, 8, "replicated")):
        packs = [_pack([a[pre + k] for k in names], 0, width, mult) for pre in ("", "m_", "v_")]
        res = _adamw(packs[0], g_packed, packs[1], packs[2], name="adamw_" + tag)
        for dst, r in zip((gred, delta, new_m, new_v), [g_packed] + list(res)):
            dst.update(zip(names, _unpack(r, shapes)))
    return (loss, dx[None], *[gred[k] for k in _WEIGHTS], *[delta[k] for k in _WEIGHTS],
            *[new_m[k] for k in _WEIGHTS], *[new_v[k] for k in _WEIGHTS])
```

```python
import functools

import jax
import jax.numpy as jnp
from jax import lax
from jax.experimental import pallas as pl
from jax.experimental.pallas import tpu as pltpu

F32 = jnp.float32
BF16 = jnp.bfloat16
HIGHEST = lax.Precision.HIGHEST
MESH = pl.DeviceIdType.MESH

RMS_EPS = 1e-6
LNX_EPS = 64e-5
LRU_C = 8.0
ADAM_LR = 0.001
ADAM_B1 = 0.9
ADAM_B2 = 0.999
ADAM_EPS = 1e-08
ADAM_WD = 0.01
ADAM_STEP = 10

LANES_V7X = 128
VMEM_LIMIT_V7X = 60 * 1024 * 1024
WKV_CHUNK = 16
N_XY = 4
N_DEV = 8


def _cparams(sem=None, **kw):
    if sem is not None:
        kw["dimension_semantics"] = sem
    return pltpu.CompilerParams(vmem_limit_bytes=VMEM_LIMIT_V7X, **kw)


def _tile(dim, prefs):
    for t in prefs:
        if dim % t == 0:
            return t
    return dim


def _round_up(n, m):
    return (n + m - 1) // m * m


MM_MAX_TK = 2816


def _tile_k(kdim):
    best = None
    for t in range(LANES_V7X, min(kdim, MM_MAX_TK) + 1, LANES_V7X):
        if kdim % t == 0:
            best = t
    return best or kdim


def _mm(a, b, *, ta=False, tb=False, res=None, out_dtype=F32, name, gather=()):
    if ta:
        kdim, m = a.shape
    else:
        m, kdim = a.shape
    n = b.shape[0] if tb else b.shape[1]
    assert (b.shape[1] if tb else b.shape[0]) == kdim
    tk = _tile_k(kdim)
    tm = _tile(m, (2048, 1024, 512, 256, 128) if tk <= 2048 else (1024, 512, 256, 128))
    tn = _tile(n, (512, 256, 128))
    nk = kdim // tk
    ni, nj = m // tm, n // tn
    a_spec = pl.BlockSpec((tk, tm), lambda i, j, k: (k, i)) if ta else pl.BlockSpec((tm, tk), lambda i, j, k: (i, k))
    b_spec = pl.BlockSpec((tn, tk), lambda i, j, k: (j, k)) if tb else pl.BlockSpec((tk, tn), lambda i, j, k: (k, j))
    o_spec = pl.BlockSpec((tm, tn), lambda i, j, k: (i, j))
    dn = (((0 if ta else 1,), (1 if tb else 0,)), ((), ()))
    has_res = res is not None
    ng = len(gather)
    nin = 2 + has_res

    def body(*refs):
        a_ref, b_ref = refs[:2]
        r_ref = refs[2] if has_res else None
        g_in, o_ref, g_out = refs[nin:nin + ng], refs[nin + ng], refs[nin + ng + 1:nin + 2 * ng + 1]
        scratch = refs[nin + 2 * ng + 1:]
        acc_ref = scratch[0] if nk > 1 else None
        g_sems = scratch[1 if nk > 1 else 0:]
        i, j, k = pl.program_id(0), pl.program_id(1), pl.program_id(2)

        if ng:
            @pl.when((i == 0) & (j == 0) & (k == 0))
            def _():
                for q in range(ng):
                    _gather_start(g_in[q], g_out[q], *g_sems[3 * q:3 * q + 3])

        def finish(acc):
            if has_res:
                acc = acc + r_ref[...].astype(F32)
            o_ref[...] = acc.astype(out_dtype)

        prod = lax.dot_general(a_ref[...], b_ref[...], dn, preferred_element_type=F32)
        if nk == 1:
            finish(prod)
        else:
            @pl.when(k == 0)
            def _():
                acc_ref[...] = prod

            @pl.when(k > 0)
            def _():
                acc_ref[...] += prod

            @pl.when(k == nk - 1)
            def _():
                finish(acc_ref[...])

        if ng:
            @pl.when((i == ni - 1) & (j == nj - 1) & (k == nk - 1))
            def _():
                for q in range(ng):
                    _gather_finish(g_in[q], g_out[q], *g_sems[3 * q:3 * q + 3])

    ins = [a, b] + ([res] if has_res else []) + list(gather)
    in_specs = [a_spec, b_spec] + ([o_spec] if has_res else []) + [_ANY] * ng
    scratch = ([pltpu.VMEM((tm, tn), F32)] if nk > 1 else []) + _gather_scratch(ng)
    sem = ("arbitrary",) * 3 if ng else ("parallel", "parallel", "arbitrary")
    out = pl.pallas_call(
        body, name=name, grid=(ni, nj, nk), in_specs=in_specs, out_specs=[o_spec] + [_ANY] * ng,
        out_shape=[jax.ShapeDtypeStruct((m, n), out_dtype)] + [_gathered_shape(g) for g in gather],
        scratch_shapes=scratch, compiler_params=_cparams(sem),
    )(*ins)
    return (out[0], list(out[1:])) if ng else out[0]


def _stage_specs(axis, tile, tiled, params, consts, rows):
    specs = []
    for arr, width, cblk in tiled:
        if axis == 0:
            specs.append(pl.BlockSpec((tile, width), functools.partial(lambda i, c: (i, c), c=cblk)))
        else:
            specs.append(pl.BlockSpec((rows, tile), functools.partial(lambda i, c: (0, i + c), c=cblk)))
    for arr, cblk in params:
        if axis == 0:
            specs.append(pl.BlockSpec(arr.shape, functools.partial(lambda i, nd: (0,) * nd, nd=arr.ndim)))
        else:
            specs.append(pl.BlockSpec((arr.shape[0], tile), functools.partial(lambda i, c: (0, i + c), c=cblk)))
    for arr in consts:
        specs.append(pl.BlockSpec(arr.shape, functools.partial(lambda i, nd: (0,) * nd, nd=arr.ndim)))
    return specs


def _stage_fwd(fn, tiled, params, consts, outs, *, axis, tile, rows, name):
    nt, npar, nc = len(tiled), len(params), len(consts)
    ntiles = (rows // tile) if axis == 0 else (outs[0][0] // tile)

    def body(*refs):
        ins = refs[: nt + npar + nc]
        orefs = refs[nt + npar + nc:]
        vals = [r[...].astype(F32) for r in ins[: nt + npar]] + [r[...] for r in ins[nt + npar:]]
        ctx = pl.program_id(0) * tile
        res = fn(ctx, *vals)
        for o_ref, o in zip(orefs, res):
            o_ref[...] = o.astype(o_ref.dtype)

    if axis == 0:
        out_specs = [pl.BlockSpec((tile, w), lambda i: (i, 0)) for w, _ in outs]
    else:
        out_specs = [pl.BlockSpec((rows, tile), lambda i: (0, i)) for w, _ in outs]
    res = pl.pallas_call(
        body, name=name, grid=(ntiles,),
        in_specs=_stage_specs(axis, tile, tiled, params, consts, rows), out_specs=out_specs,
        out_shape=[jax.ShapeDtypeStruct((rows, w), dt) for w, dt in outs],
        compiler_params=_cparams(("arbitrary",)),
    )(*[t[0] for t in tiled], *[p[0] for p in params], *consts)
    return res


def _stage_bwd(fn, tiled, params, consts, cots, dtiled, *, axis, tile, rows, name, ncols=None):
    nt, npar, nc, nco = len(tiled), len(params), len(consts), len(cots)
    ntiles = (rows // tile) if axis == 0 else (ncols // tile)
    didx = [d[0] for d in dtiled]

    def body(*refs):
        ins = refs[: nt + npar + nc]
        crefs = refs[nt + npar + nc: nt + npar + nc + nco]
        orefs = refs[nt + npar + nc + nco:]
        vals = [r[...].astype(F32) for r in ins[: nt + npar]] + [r[...] for r in ins[nt + npar:]]
        ctx = pl.program_id(0) * tile

        def g(*dv):
            full = list(vals)
            for j, ix in enumerate(didx):
                full[ix] = dv[j]
            for j in range(npar):
                full[nt + j] = dv[len(didx) + j]
            return tuple(fn(ctx, *full))

        prim = [vals[ix] for ix in didx] + [vals[nt + j] for j in range(npar)]
        _, vjp = jax.vjp(g, *prim)
        grads = vjp(tuple(c[...].astype(F32) for c in crefs))
        for j in range(len(didx)):
            orefs[j][...] = grads[j].astype(orefs[j].dtype)
        for j in range(npar):
            o_ref = orefs[len(didx) + j]
            gp = grads[len(didx) + j]
            if axis == 0:
                @pl.when(pl.program_id(0) == 0)
                def _(o_ref=o_ref):
                    o_ref[...] = jnp.zeros_like(o_ref)

                o_ref[...] += gp
            else:
                o_ref[...] = gp

    if axis == 0:
        cot_specs = [pl.BlockSpec((tile, w), functools.partial(lambda i, c: (i, c), c=cb)) for _, w, cb in cots]
        out_specs = [pl.BlockSpec((tile, w), lambda i: (i, 0)) for _, w, _ in dtiled]
        out_specs += [pl.BlockSpec(p.shape, functools.partial(lambda i, nd: (0,) * nd, nd=p.ndim)) for p, _ in params]
        out_shape = [jax.ShapeDtypeStruct((rows, w), dt) for _, w, dt in dtiled]
        out_shape += [jax.ShapeDtypeStruct(p.shape, F32) for p, _ in params]
    else:
        cot_specs = [pl.BlockSpec((rows, tile), functools.partial(lambda i, c: (0, i + c), c=cb)) for _, cb in cots]
        out_specs = [pl.BlockSpec((rows, tile), lambda i: (0, i)) for _ in dtiled]
        out_specs += [pl.BlockSpec((p.shape[0], tile), lambda i: (0, i)) for p, _ in params]
        out_shape = [jax.ShapeDtypeStruct((rows, w), dt) for _, w, dt in dtiled]
        out_shape += [jax.ShapeDtypeStruct((p.shape[0], ncols), F32) for p, _ in params]
    return pl.pallas_call(
        body, name=name, grid=(ntiles,),
        in_specs=_stage_specs(axis, tile, tiled, params, consts, rows) + cot_specs, out_specs=out_specs,
        out_shape=out_shape, compiler_params=_cparams(("arbitrary",)),
    )(*[t[0] for t in tiled], *[p[0] for p in params], *consts, *[c[0] for c in cots])


def _rms(x, g):
    return x * lax.rsqrt(jnp.mean(x * x, axis=-1, keepdims=True) + RMS_EPS) * g


def _row_mask(x, k, first):
    t = lax.broadcasted_iota(jnp.int32, x.shape, 0)
    keep = (t >= k) if first else (t < x.shape[0] - k)
    return jnp.where(keep, x, 0.0)


@functools.partial(jax.custom_vjp, nondiff_argnums=(1,))
def _shift_down(x, k):
    return _row_mask(pltpu.roll(x, k, 0), k, True)


def _shift_down_fwd(x, k):
    return _shift_down(x, k), None


def _shift_down_bwd(k, _, g):
    return (_row_mask(pltpu.roll(g, g.shape[0] - k, 0), k, False),)


_shift_down.defvjp(_shift_down_fwd, _shift_down_bwd)


def _dwconv(x, w, b):
    kw = w.shape[0]
    out = x * w[kw - 1:kw] + b
    for j in range(kw - 1):
        out = out + _shift_down(x, kw - 1 - j) * w[j:j + 1]
    return out


def _f_norm(ctx, x, g):
    return (_rms(x, g),)


def _f_norm_res(ctx, x, g):
    return (_rms(x, g), x)


def _f_shiftmix(ctx, z, mu):
    return (z + (_shift_down(z, 1) - z) * mu,)


def _f_conv(ctx, x, w, b):
    return (_dwconv(x, w, b),)


def _f_ffn_act(ctx, gpre, up, w, b):
    return (jax.nn.gelu(_dwconv(gpre, w, b)) * up,)


def _make_f_lru_gates(heads):
    def fn(ctx, xb, wx, wa, bx, ba, lam):
        blk = xb.shape[1] // heads
        px, pa = [], []
        for h in range(heads):
            xh = xb[:, h * blk:(h + 1) * blk]
            px.append(jnp.dot(xh, wx[h], preferred_element_type=F32))
            pa.append(jnp.dot(xh, wa[h], preferred_element_type=F32))
        px = px[0] if heads == 1 else jnp.concatenate(px, axis=1)
        pa = pa[0] if heads == 1 else jnp.concatenate(pa, axis=1)
        gate_x = jax.nn.sigmoid(px + bx)
        gate_a = jax.nn.sigmoid(pa + ba)
        log_a = -LRU_C * gate_a * jax.nn.softplus(-lam)
        a = jnp.exp(log_a)
        mult = jnp.sqrt(1.0 - jnp.exp(2.0 * log_a))
        t = ctx + lax.broadcasted_iota(jnp.int32, xb.shape, 0)
        mult = jnp.where(t == 0, 1.0, mult)
        return a, xb * gate_x * mult

    return fn


def _f_lru_out(ctx, hl, ya, g):
    return (_rms(hl * jax.nn.gelu(ya), g),)


def _headsum_3pass(x, bb):
    hi = x.astype(BF16)
    r1 = x - hi.astype(F32)
    mid = r1.astype(BF16)
    lo = (r1 - mid.astype(F32)).astype(BF16)
    return (jnp.dot(hi, bb, preferred_element_type=F32) + jnp.dot(mid, bb, preferred_element_type=F32)
            + jnp.dot(lo, bb, preferred_element_type=F32))


@jax.custom_vjp
def _headsum(x, bb):
    return _headsum_3pass(x, bb)


def _headsum_fwd(x, bb):
    return _headsum_3pass(x, bb), bb


def _headsum_bwd(bb, g):
    return _headsum_3pass(g, bb), None


_headsum.defvjp(_headsum_fwd, _headsum_bwd)


def _make_f_rwkv_pre(has_vres, v_uses=0):
    def fn(ctx, *args):
        if v_uses:
            r, args = args[0], args[1:]
        if has_vres:
            k, v, lz, vf, w0, w2, a0, a2, g2, kkw, ka, v0, v2, bb = args
        else:
            k, v, lz, w0, w2, a0, a2, g2, kkw, ka, bb = args
        w_log = -jax.nn.softplus(-(w0 + jnp.dot(jnp.tanh(lz), w2, preferred_element_type=F32))) - 0.5
        logw = -jnp.exp(w_log)
        a = jax.nn.sigmoid(a0 + jnp.dot(lz, a2, preferred_element_type=F32))
        g = jnp.dot(jax.nn.sigmoid(lz), g2, preferred_element_type=F32)
        if has_vres:
            v = v + (vf - v) * jax.nn.sigmoid(v0 + jnp.dot(lz, v2, preferred_element_type=F32))
        xk = k * kkw
        kk = xk / jnp.maximum(jnp.sqrt(_headsum(xk * xk, bb)), 1e-12)
        k2 = k * (1.0 + (a - 1.0) * ka)
        if v_uses:
            return (r, r, logw, k2, k2) + (v,) * v_uses + (kk, kk * a, g)
        return logw, k2, v, kk, kk * a, g

    return fn


def _make_f_rwkv_post(head_size):
    def fn(ctx, y, r, k2, v2, g, lnw, lnb, rk, bb):
        mean = _headsum(y, bb) / head_size
        d = y - mean
        var = _headsum(d * d, bb) / head_size
        yn = d * lax.rsqrt(var + LNX_EPS) * lnw + lnb
        bonus = _headsum(r * k2 * rk, bb) * v2
        return ((yn + bonus) * g,)

    return fn


def _f_ple(ctx, h, eg, ep, g):
    return (h + _rms(jax.nn.sigmoid(eg) * ep, g),)


def _lru_scan(a, b, *, name):
    rows, cols = a.shape
    tc = _tile(cols, (512, 256, 128))

    def body(a_ref, b_ref, h_ref):
        def step(t, carry):
            h = a_ref[pl.ds(t, 1), :] * carry + b_ref[pl.ds(t, 1), :]
            h_ref[pl.ds(t, 1), :] = h
            return h

        lax.fori_loop(0, rows, step, jnp.zeros((1, tc), F32), unroll=8)

    spec = pl.BlockSpec((rows, tc), lambda j: (0, j))
    return pl.pallas_call(body, name=name, grid=(cols // tc,), in_specs=[spec, spec], out_specs=spec,
                          out_shape=jax.ShapeDtypeStruct((rows, cols), F32), compiler_params=_cparams(("arbitrary",)))(a, b)


def _lru_scan_bwd(a, h, dh, *, name):
    rows, cols = a.shape
    tc = _tile(cols, (512, 256, 128))

    def body(a_ref, h_ref, dh_ref, da_ref, db_ref):
        def step(i, carry):
            t = rows - 1 - i
            g = dh_ref[pl.ds(t, 1), :] + carry
            db_ref[pl.ds(t, 1), :] = g
            hp = h_ref[pl.ds(jnp.maximum(t - 1, 0), 1), :]
            da_ref[pl.ds(t, 1), :] = jnp.where(t > 0, g * hp, 0.0)
            return a_ref[pl.ds(t, 1), :] * g

        lax.fori_loop(0, rows, step, jnp.zeros((1, tc), F32), unroll=8)

    spec = pl.BlockSpec((rows, tc), lambda j: (0, j))
    return pl.pallas_call(body, name=name, grid=(cols // tc,), in_specs=[spec] * 3, out_specs=[spec] * 2,
                          out_shape=[jax.ShapeDtypeStruct((rows, cols), F32)] * 2,
                          compiler_params=_cparams(("arbitrary",)))(a, h, dh)


def _split_bf16(x):
    hi = x.astype(BF16)
    return hi, (x - hi.astype(F32)).astype(BF16)


def _dot3_passes(a, b, ca, cb):
    dn = (((ca,), (cb,)), ((), ()))
    ah, al = _split_bf16(a)
    bh, bl = _split_bf16(b)
    return (lax.dot_general(ah, bh, dn, preferred_element_type=F32) + lax.dot_general(al, bh, dn, preferred_element_type=F32)
            + lax.dot_general(ah, bl, dn, preferred_element_type=F32))


@functools.partial(jax.custom_vjp, nondiff_argnums=(2, 3))
def _dot3(a, b, ca, cb):
    return _dot3_passes(a, b, ca, cb)


def _dot3_fwd(a, b, ca, cb):
    return _dot3_passes(a, b, ca, cb), (a, b)


def _dot3_bwd(ca, cb, res, g):
    a, b = res
    fa, fb = 1 - ca, 1 - cb
    da = _dot3_passes(g, b, 1, fb) if ca == 1 else _dot3_passes(b, g, fb, 1)
    db = _dot3_passes(a, g, fa, 0) if cb == 0 else _dot3_passes(g, a, 0, fa)
    return da, db


_dot3.defvjp(_dot3_fwd, _dot3_bwd)


def _each(f, *lists):
    return [f(*t) for t in zip(*lists)]


def _wkv_local(r, lw, k, v, kk, b):
    c, n = r[0].shape
    row = lax.broadcasted_iota(jnp.int32, (c, c), 0)
    col = lax.broadcasted_iota(jnp.int32, (c, c), 1)
    incl = (row >= col).astype(F32)
    strict = (row > col).astype(F32)
    eye = lax.broadcasted_iota(jnp.int32, (n, n), 0) == lax.broadcasted_iota(jnp.int32, (n, n), 1)
    cl = _each(lambda x: _dot3(incl, x, 1, 0), lw)
    w_t = _each(jnp.exp, cl)
    inv_w = _each(lambda x: jnp.exp(-x), cl)
    kk_s = _each(lambda x, y, z: x * jnp.exp(y - z), kk, cl, lw)
    b_s = _each(jnp.multiply, b, inv_w)
    k_s = _each(jnp.multiply, k, inv_w)
    r_s = _each(jnp.multiply, r, w_t)
    q = _each(lambda x, y: jnp.concatenate([x, y], axis=0), kk_s, r_s)
    qb = _each(lambda x, y: _dot3(x, y, 1, 1), q, b_s)
    qk = _each(lambda x, y: _dot3(x, y, 1, 1), q, k_s)
    m = _each(lambda x: -strict * x[:c], qb)
    pb = _each(lambda x: incl * x[c:], qb)
    lkv = _each(lambda x, y: _dot3(strict * x[:c], y, 1, 0), qk, v)
    pkv = _each(lambda x, y: _dot3(incl * x[c:], y, 1, 0), qk, v)
    a = _each(lambda x, y: jnp.concatenate([x, y], axis=1), kk_s, lkv)
    steps = max(1, (c - 1).bit_length())
    for i in range(steps):
        a = _each(lambda x, y: y + _dot3(x, y, 1, 0), m, a)
        if i + 1 < steps:
            m = _each(lambda x: _dot3(x, x, 1, 0), m)
    ry = _each(lambda x, y, z, w: jnp.concatenate([x, y], axis=1) - _dot3(z, w, 1, 0), r_s, pkv, pb, a)
    w_end = _each(lambda x: x[c - 1:c, :], w_t)
    gu_low = _each(lambda x, y, z: _dot3(x, y * z, 0, 0), a, b_s, w_end)
    g = _each(lambda x, y: jnp.where(eye, jnp.broadcast_to(x, (n, n)), 0.0) - y[:n], w_end, gu_low)
    u = _each(lambda x, y, z, w: _dot3(x, y * z, 0, 0) - w[n:], v, k_s, w_end, gu_low)
    return g, u, _each(lambda x: x[:, :n], ry), _each(lambda x: x[:, n:], ry)


def _wkv_blocks(h, nchunk):
    return (_tile(h, (4, 2, 1)), _tile(nchunk, (4, 2, 1))), (h, _tile(nchunk, (4, 2, 1)))


def _wkv_fwd(r, lw, k, v, kk, b, *, name, gather=()):
    h, t, n = r.shape
    c = WKV_CHUNK
    nchunk = t // c
    (hb, cb), (hs, cs) = _wkv_blocks(h, nchunk)
    ng = len(gather)
    ni, nj = h // hb, nchunk // cb

    pairs = [(i, j) for i in range(hb) for j in range(cb)]

    def local_body(*refs):
        ins, g_in = refs[:6], refs[6:6 + ng]
        g_ref, u_ref, r2_ref, y0_ref = refs[6 + ng:10 + ng]
        g_out, g_sems = refs[10 + ng:10 + 2 * ng], refs[10 + 2 * ng:]
        if ng:
            @pl.when((pl.program_id(0) == 0) & (pl.program_id(1) == 0))
            def _():
                for q in range(ng):
                    _gather_start(g_in[q], g_out[q], *g_sems[3 * q:3 * q + 3])

        g, u, r2, y0 = _wkv_local(*[[ref[i, pl.ds(j * c, c)] for i, j in pairs] for ref in ins])
        for idx, (i, j) in enumerate(pairs):
            g_ref[i, j] = g[idx]
            u_ref[i, j] = u[idx]
            r2_ref[i, pl.ds(j * c, c)] = r2[idx]
            y0_ref[i, pl.ds(j * c, c)] = y0[idx]
        if ng:
            @pl.when((pl.program_id(0) == ni - 1) & (pl.program_id(1) == nj - 1))
            def _():
                for q in range(ng):
                    _gather_finish(g_in[q], g_out[q], *g_sems[3 * q:3 * q + 3])

    seq = pl.BlockSpec((hb, cb * c, n), lambda i, j: (i, j, 0))
    mat = pl.BlockSpec((hb, cb, n, n), lambda i, j: (i, j, 0, 0))
    res = pl.pallas_call(
        local_body, name=name + "_local", grid=(ni, nj), in_specs=[seq] * 6 + [_ANY] * ng,
        out_specs=[mat, mat, seq, seq] + [_ANY] * ng,
        out_shape=[jax.ShapeDtypeStruct((h, nchunk, n, n), F32)] * 2 + [jax.ShapeDtypeStruct((h, t, n), F32)] * 2
        + [_gathered_shape(g) for g in gather],
        scratch_shapes=_gather_scratch(ng),
        compiler_params=_cparams(("arbitrary", "arbitrary") if ng else ("parallel", "parallel")),
    )(r, lw, k, v, kk, b, *gather)
    gm, um, r2, y0 = res[:4]
    gathered = list(res[4:])

    def state_body(g_ref, u_ref, r2_ref, y0_ref, y_ref, st_ref, s_ref):
        @pl.when(pl.program_id(0) == 0)
        def _():
            s_ref[...] = jnp.zeros_like(s_ref)

        s = [s_ref[i] for i in range(hs)]
        for j in range(cs):
            rows = pl.ds(j * c, c)
            for i in range(hs):
                st_ref[i, j] = s[i]
                y_ref[i, rows] = _dot3(r2_ref[i, rows], s[i], 1, 1) + y0_ref[i, rows]
            s = [_dot3(s[i], g_ref[i, j], 1, 0) + u_ref[i, j] for i in range(hs)]
        for i in range(hs):
            s_ref[i] = s[i]

    seq = pl.BlockSpec((hs, cs * c, n), lambda j: (0, j, 0))
    mat = pl.BlockSpec((hs, cs, n, n), lambda j: (0, j, 0, 0))
    y, states = pl.pallas_call(
        state_body, name=name + "_state", grid=(nchunk // cs,), in_specs=[mat, mat, seq, seq], out_specs=[seq, mat],
        out_shape=[jax.ShapeDtypeStruct((h, t, n), F32), jax.ShapeDtypeStruct((h, nchunk, n, n), F32)],
        scratch_shapes=[pltpu.VMEM((hs, n, n), F32)], compiler_params=_cparams(("arbitrary",)),
    )(gm, um, r2, y0)
    return y, (states, gm, r2), gathered


def _wkv_bwd(r, lw, k, v, kk, b, saved, dy, *, name):
    states, gm, r2 = saved
    h, t, n = r.shape
    c = WKV_CHUNK
    nchunk = t // c
    (hb, cb), (hs, cs) = _wkv_blocks(h, nchunk)
    nsteps = nchunk // cs

    def state_body(g_ref, r2_ref, st_ref, dy_ref, dg_ref, du_ref, dr2_ref, ds_ref):
        @pl.when(pl.program_id(0) == 0)
        def _():
            ds_ref[...] = jnp.zeros_like(ds_ref)

        ds = [ds_ref[i] for i in range(hs)]
        for j in reversed(range(cs)):
            rows = pl.ds(j * c, c)
            for i in range(hs):
                s0 = st_ref[i, j]
                du_ref[i, j] = ds[i]
                dg_ref[i, j] = _dot3(s0, ds[i], 0, 0)
                dr2_ref[i, rows] = _dot3(dy_ref[i, rows], s0, 1, 0)
            ds = [_dot3(dy_ref[i, rows], r2_ref[i, rows], 0, 0) + _dot3(ds[i], g_ref[i, j], 1, 1) for i in range(hs)]
        for i in range(hs):
            ds_ref[i] = ds[i]

    seq = pl.BlockSpec((hs, cs * c, n), lambda j: (0, nsteps - 1 - j, 0))
    mat = pl.BlockSpec((hs, cs, n, n), lambda j: (0, nsteps - 1 - j, 0, 0))
    dg, du, dr2 = pl.pallas_call(
        state_body, name=name + "_state", grid=(nsteps,), in_specs=[mat, seq, mat, seq], out_specs=[mat, mat, seq],
        out_shape=[jax.ShapeDtypeStruct((h, nchunk, n, n), F32)] * 2 + [jax.ShapeDtypeStruct((h, t, n), F32)],
        scratch_shapes=[pltpu.VMEM((hs, n, n), F32)], compiler_params=_cparams(("arbitrary",)),
    )(gm, r2, states, dy)

    pairs = [(i, j) for i in range(hb) for j in range(cb)]

    def local_body(*refs):
        ins, (dg_ref, du_ref, dr2_ref, dy_ref), out_refs = refs[:6], refs[6:10], refs[10:]
        _, vjp = jax.vjp(_wkv_local, *[[ref[i, pl.ds(j * c, c)] for i, j in pairs] for ref in ins])
        grads = vjp(([dg_ref[i, j] for i, j in pairs], [du_ref[i, j] for i, j in pairs],
                     [dr2_ref[i, pl.ds(j * c, c)] for i, j in pairs], [dy_ref[i, pl.ds(j * c, c)] for i, j in pairs]))
        for o_ref, gr in zip(out_refs, grads):
            for idx, (i, j) in enumerate(pairs):
                o_ref[i, pl.ds(j * c, c)] = gr[idx]

    seq = pl.BlockSpec((hb, cb * c, n), lambda i, j: (i, j, 0))
    mat = pl.BlockSpec((hb, cb, n, n), lambda i, j: (i, j, 0, 0))
    return pl.pallas_call(
        local_body, name=name + "_local", grid=(h // hb, nchunk // cb), in_specs=[seq] * 6 + [mat, mat, seq, seq],
        out_specs=[seq] * 6, out_shape=[jax.ShapeDtypeStruct((h, t, n), F32)] * 6,
        compiler_params=_cparams(("parallel", "parallel")),
    )(r, lw, k, v, kk, b, dg, du, dr2, dy)


class _Dims:
    pass


def _make_dims(x, p, w):
    m = _Dims()
    m.t, m.d = x.shape[-2], x.shape[-1]
    m.nl = w["ln_mix"].shape[0]
    m.dl = w["conv_a_b"].shape[1]
    m.hl = w["lru_wx"].shape[1]
    m.dr = w["rwkv_w0"].shape[1]
    m.h, m.n = w["rwkv_rk"].shape[1], w["rwkv_rk"].shape[2]
    m.lw, m.la, m.lg, m.lv = (w[k].shape[1] for k in ("rwkv_w2", "rwkv_a2", "rwkv_g2", "rwkv_v2"))
    m.nsh = w["mu_shift"].shape[1]
    m.ff = w["conv_f_b"].shape[1]
    m.ple = p.shape[-1]
    m.din = 2 * m.dl + m.nsh
    m.lz = _round_up(m.lw + m.la + m.lg + m.lv, LANES_V7X)
    m.zw = _round_up(2 * m.dl + 3 * m.dr + m.lz, 512)
    m.zs = m.zw - 2 * m.dl
    m.tr = _tile(m.t, (256, 128, 64, 32, 16, 8))
    m.trb = _tile(m.t, (128, 64, 32, 16, 8))
    m.tcs = _tile(m.zs, (512, 256, 128))
    assert (3 * m.dr) % m.lz == 0 and (2 * m.dl) % m.tcs == 0 and m.t % WKV_CHUNK == 0
    assert m.nsh == 3 * m.dr + m.lw + m.la + m.lg
    return m


def _to_heads(m, a):
    return jnp.transpose(a.reshape(m.t, m.h, m.n), (1, 0, 2))


def _from_heads(m, a):
    return jnp.transpose(a, (1, 0, 2)).reshape(m.t, m.dr)


def _norm_fwd(m, h, g, name):
    return _stage_fwd(_f_norm, [(h, m.d, 0)], [(g, 0)], [], [(m.d, BF16)], axis=0, tile=m.tr, rows=m.t, name=name)[0]


def _norm_bwd(m, h, g, du, dres, name):
    return _stage_bwd(_f_norm_res, [(h, m.d, 0)], [(g, 0)], [], [(du, m.d, 0), (dres, m.d, 0)], [(0, m.d, F32)],
                      axis=0, tile=m.tr, rows=m.t, name=name)


def _rwkv_pre_operands(m, w, i, sv, v_first_zs, with_r):
    zs = sv["zs"]
    tiled = ([(zs, m.dr, 0)] if with_r else []) + [(zs, m.dr, 1), (zs, m.dr, 2), (zs, m.lz, 3 * m.dr // m.lz)]
    params = [(w["rwkv_w0"][i:i + 1], 0), (w["w2p"][i], 0), (w["rwkv_a0"][i:i + 1], 0), (w["a2p"][i], 0),
              (w["g2p"][i], 0), (w["rwkv_kk"][i:i + 1], 0), (w["rwkv_ka"][i:i + 1], 0)]
    if i > 0:
        tiled.append((v_first_zs, m.dr, 2))
        params += [(w["rwkv_v0"][i - 1:i], 0), (w["v2p"][i - 1], 0)]
    return tiled, params


def _rwkv_post_operands(m, w, i, sv):
    tiled = [(sv["y"], m.dr, 0), (sv["zs"], m.dr, 0), (sv["k2"], m.dr, 0), (sv["v2"], m.dr, 0), (sv["g"], m.dr, 0)]
    params = [(w["rwkv_lnx_w"][i:i + 1], 0), (w["rwkv_lnx_b"][i:i + 1], 0), (w["rk"][i], 0)]
    return tiled, params


def _lru_gate_params(w, i):
    return [(w["lru_wx"][i], 0), (w["lru_wa"][i], 0), (w["lru_bx"][i:i + 1], 0), (w["lru_ba"][i:i + 1], 0),
            (w["lru_lambda"][i:i + 1], 0)]


class _WeightFeed:
    def __init__(self, m, w, shards, vres):
        self.m, self.w, self.shards, self.vres = m, w, shards, vres

    def keys(self, carrier, i):
        plan = {"mm_in": [("w_o", i)] if i == 0 else [],
                "wkv": [("w_gate", i), ("w_up", i), ("w_down", i)],
                "mm_gate": [("w_ple_gate", i)], "mm_up": [("w_ple_proj", i)],
                "mm_down": [("w_in", i + 1)], "mm_pgate": [("w_o", i + 1)]}
        return [key for key in plan[carrier] if key[1] < self.m.nl]

    def blobs(self, keys):
        return [_gather_blob(self.shards[name][layer]) for name, layer in keys]

    def arrive(self, keys, gathered):
        m = self.m
        for (name, layer), got in zip(keys, gathered):
            full = got.reshape((N_XY, got.shape[1] * got.shape[2], got.shape[3]))
            full = _from_shards(full, _SHARD_AXIS[name] - 1)
            if name == "w_in":
                vres = self.vres[layer - 1] if layer > 0 else jnp.zeros((m.d, m.lv), BF16)
                self.w["wcat"][layer] = jnp.concatenate([full, vres, jnp.zeros((m.d, m.zw - m.din - m.lv), BF16)], axis=1)
            else:
                self.w[name][layer] = full


def _mm_fed(feed, carrier, i, a, b, **kw):
    keys = feed.keys(carrier, i) if feed is not None else []
    if not keys:
        return _mm(a, b, **kw)
    out, got = _mm(a, b, gather=feed.blobs(keys), **kw)
    feed.arrive(keys, got)
    return out


def _layer_fwd(m, w, i, h, p_bf, v_first_zs, feed=None):
    sv = {"h": h}
    t, dl, dr = m.t, m.dl, m.dr
    sv["u1"] = _norm_fwd(m, h, w["ln_mix"][i:i + 1], "norm_mix")
    z = sv["z"] = _mm_fed(feed, "mm_in", i, sv["u1"], w["wcat"][i], name="mm_in")
    off = 2 * dl // m.tcs
    sv["zs"] = _stage_fwd(_f_shiftmix, [(z, None, off)], [(w["mu_pad"][i], off)], [], [(m.zs, F32)],
                          axis=1, tile=m.tcs, rows=t, name="shiftmix")[0]
    tca = _tile(dl, (512, 256, 128))
    sv["xb"] = _stage_fwd(_f_conv, [(z, None, 0)], [(w["conv_a_w"][i], 0), (w["conv_a_b"][i:i + 1], 0)], [],
                          [(dl, F32)], axis=1, tile=tca, rows=t, name="conv_a")[0]
    sv["a"], b_in = _stage_fwd(_make_f_lru_gates(m.hl), [(sv["xb"], dl, 0)], _lru_gate_params(w, i), [],
                               [(dl, F32), (dl, F32)], axis=0, tile=m.tr, rows=t, name="lru_gates")
    sv["hl"] = _lru_scan(sv["a"], b_in, name="lru_scan")
    out_a = _stage_fwd(_f_lru_out, [(sv["hl"], dl, 0), (z, dl, 1)], [(w["lru_norm"][i:i + 1], 0)], [],
                       [(dl, BF16)], axis=0, tile=m.tr, rows=t, name="lru_out")[0]
    tiled, params = _rwkv_pre_operands(m, w, i, sv, v_first_zs, False)
    pre = _stage_fwd(_make_f_rwkv_pre(i > 0), tiled, params, [w["bb"]], [(dr, F32)] * 6,
                     axis=0, tile=m.tr, rows=t, name="rwkv_pre")
    sv["logw"], sv["k2"], sv["v2"], sv["kk"], sv["b"], sv["g"] = pre
    heads = [_to_heads(m, a) for a in (sv["zs"][:, :dr], sv["logw"], sv["k2"], sv["v2"], sv["kk"], sv["b"])]
    keys = feed.keys("wkv", i) if feed is not None else []
    y_h, sv["states"], got = _wkv_fwd(*heads, name="wkv_fwd", gather=feed.blobs(keys) if keys else ())
    if keys:
        feed.arrive(keys, got)
    sv["y"] = _from_heads(m, y_h)
    tiled, params = _rwkv_post_operands(m, w, i, sv)
    out_b = _stage_fwd(_make_f_rwkv_post(m.n), tiled, params, [w["bb"]], [(dr, BF16)],
                       axis=0, tile=m.tr, rows=t, name="rwkv_post")[0]
    sv["cat"] = jnp.concatenate([out_a, out_b], axis=1)
    h2 = sv["h2"] = _mm(sv["cat"], w["w_o"][i], res=h, name="mm_o")
    sv["u2"] = _norm_fwd(m, h2, w["ln_ffn"][i:i + 1], "norm_ffn")
    sv["gpre"] = _mm_fed(feed, "mm_gate", i, sv["u2"], w["w_gate"][i], name="mm_gate")
    sv["up"] = _mm_fed(feed, "mm_up", i, sv["u2"], w["w_up"][i], name="mm_up")
    tcf = _tile(m.ff, (512, 256, 128))
    sv["act"] = _stage_fwd(_f_ffn_act, [(sv["gpre"], None, 0), (sv["up"], None, 0)],
                           [(w["conv_f_w"][i], 0), (w["conv_f_b"][i:i + 1], 0)], [], [(m.ff, BF16)],
                           axis=1, tile=tcf, rows=t, name="ffn_act")[0]
    h3 = sv["h3"] = _mm_fed(feed, "mm_down", i, sv["act"], w["w_down"][i], res=h2, name="mm_down")
    sv["u3"] = _norm_fwd(m, h3, w["ln_ple"][i:i + 1], "norm_ple")
    sv["eg"] = _mm_fed(feed, "mm_pgate", i, sv["u3"], w["w_ple_gate"][i], name="mm_pgate")
    sv["ep"] = _mm(p_bf, w["w_ple_proj"][i], name="mm_pproj")
    h4 = _stage_fwd(_f_ple, [(h3, m.d, 0), (sv["eg"], m.d, 0), (sv["ep"], m.d, 0)], [(w["ln_ple_post"][i:i + 1], 0)],
                    [], [(m.d, F32)], axis=0, tile=m.tr, rows=t, name="ple")[0]
    return h4, sv


def _layer_bwd(m, w, i, dh4, sv, p_bf, v_first_zs, dvf_in):
    t, d, dl, dr = m.t, m.d, m.dl, m.dr
    g = {}
    deg, dep, g["ln_ple_post"] = _stage_bwd(
        _f_ple, [(sv["h3"], d, 0), (sv["eg"], d, 0), (sv["ep"], d, 0)], [(w["ln_ple_post"][i:i + 1], 0)], [],
        [(dh4, d, 0)], [(1, d, BF16), (2, d, BF16)], axis=0, tile=m.tr, rows=t, name="ple_bwd")
    du3 = _mm(deg, w["w_ple_gate"][i], tb=True, name="mm_pgate_dx")
    g["w_ple_gate"] = _mm(sv["u3"], deg, ta=True, name="mm_pgate_dw")
    g["w_ple_proj"] = _mm(p_bf, dep, ta=True, name="mm_pproj_dw")
    dh3, g["ln_ple"] = _norm_bwd(m, sv["h3"], w["ln_ple"][i:i + 1], du3, dh4, "norm_ple_bwd")
    dh3_bf = dh3.astype(BF16)
    dact = _mm(dh3_bf, w["w_down"][i], tb=True, name="mm_down_dx")
    g["w_down"] = _mm(sv["act"], dh3_bf, ta=True, name="mm_down_dw")
    tcf = _tile(m.ff, (512, 256, 128))
    dgpre, dup, g["conv_f_w"], g["conv_f_b"] = _stage_bwd(
        _f_ffn_act, [(sv["gpre"], None, 0), (sv["up"], None, 0)], [(w["conv_f_w"][i], 0), (w["conv_f_b"][i:i + 1], 0)],
        [], [(dact, 0)], [(0, m.ff, BF16), (1, m.ff, BF16)], axis=1, tile=tcf, rows=t, ncols=m.ff, name="ffn_act_bwd")
    du2 = _mm(dgpre, w["w_gate"][i], tb=True, name="mm_gate_dx")
    du2 = _mm(dup, w["w_up"][i], tb=True, res=du2, name="mm_up_dx")
    g["w_gate"] = _mm(sv["u2"], dgpre, ta=True, name="mm_gate_dw")
    g["w_up"] = _mm(sv["u2"], dup, ta=True, name="mm_up_dw")
    dh2, g["ln_ffn"] = _norm_bwd(m, sv["h2"], w["ln_ffn"][i:i + 1], du2, dh3, "norm_ffn_bwd")
    dh2_bf = dh2.astype(BF16)
    dcat = _mm(dh2_bf, w["w_o"][i], tb=True, name="mm_o_dx")
    g["w_o"] = _mm(sv["cat"], dh2_bf, ta=True, name="mm_o_dw")
    tiled, params = _rwkv_post_operands(m, w, i, sv)
    dy, dr_a, dk2_a, dv2_a, dg, g["rwkv_lnx_w"], g["rwkv_lnx_b"], g["rk"] = _stage_bwd(
        _make_f_rwkv_post(m.n), tiled, params, [w["bb"]], [(dcat, dr, dl // dr)], [(j, dr, F32) for j in range(5)],
        axis=0, tile=m.trb, rows=t, name="rwkv_post_bwd")
    heads = [_to_heads(m, a) for a in (sv["zs"][:, :dr], sv["logw"], sv["k2"], sv["v2"], sv["kk"], sv["b"])]
    dwkv = _wkv_bwd(*heads, sv["states"], _to_heads(m, dy), name="wkv_bwd")
    dr_b, dlw, dk2_b, dv2_b, dkk, db = [_from_heads(m, a) for a in dwkv]
    tiled, params = _rwkv_pre_operands(m, w, i, sv, v_first_zs, True)
    v_cots = [dv2_a, dv2_b] + ([dvf_in] if dvf_in is not None else [])
    cots = [(c, dr, 0) for c in [dr_a, dr_b, dlw, dk2_a, dk2_b] + v_cots + [dkk, db, dg]]
    ntil = len(tiled)
    dtiled = [(0, dr, F32), (1, dr, F32), (2, dr, F32), (3, m.lz, F32)] + ([(4, dr, F32)] if i > 0 else [])
    res = _stage_bwd(_make_f_rwkv_pre(i > 0, len(v_cots)), tiled, params, [w["bb"]], cots, dtiled,
                     axis=0, tile=m.trb, rows=t, name="rwkv_pre_bwd")
    d_r, d_k, d_v, d_lz = res[:4]
    dvf_out = res[4] if i > 0 else None
    pg = res[ntil:]
    g["rwkv_w0"], g["w2p"], g["rwkv_a0"], g["a2p"], g["g2p"], g["rwkv_kk"], g["rwkv_ka"] = pg[:7]
    if i > 0:
        g["rwkv_v0"], g["v2p"] = pg[7:9]
    dzs = jnp.concatenate([d_r, d_k, d_v, d_lz, jnp.zeros((t, m.zs - 3 * dr - m.lz), F32)], axis=1)
    off = 2 * dl // m.tcs
    dzr, g["mu_pad"] = _stage_bwd(_f_shiftmix, [(sv["z"], None, off)], [(w["mu_pad"][i], off)], [], [(dzs, 0)],
                                  [(0, m.zs, BF16)], axis=1, tile=m.tcs, rows=t, ncols=m.zs, name="shiftmix_bwd")
    dhl, dya, g["lru_norm"] = _stage_bwd(
        _f_lru_out, [(sv["hl"], dl, 0), (sv["z"], dl, 1)], [(w["lru_norm"][i:i + 1], 0)], [], [(dcat, dl, 0)],
        [(0, dl, F32), (1, dl, BF16)], axis=0, tile=m.tr, rows=t, name="lru_out_bwd")
    da, db_in = _lru_scan_bwd(sv["a"], sv["hl"], dhl, name="lru_scan_bwd")
    dxb, g["lru_wx"], g["lru_wa"], g["lru_bx"], g["lru_ba"], g["lru_lambda"] = _stage_bwd(
        _make_f_lru_gates(m.hl), [(sv["xb"], dl, 0)], _lru_gate_params(w, i), [], [(da, dl, 0), (db_in, dl, 0)],
        [(0, dl, F32)], axis=0, tile=m.tr, rows=t, name="lru_gates_bwd")
    tca = _tile(dl, (512, 256, 128))
    dxa, g["conv_a_w"], g["conv_a_b"] = _stage_bwd(
        _f_conv, [(sv["z"], None, 0)], [(w["conv_a_w"][i], 0), (w["conv_a_b"][i:i + 1], 0)], [], [(dxb, 0)],
        [(0, dl, BF16)], axis=1, tile=tca, rows=t, ncols=dl, name="conv_a_bwd")
    dz = jnp.concatenate([dxa, dya, dzr], axis=1)
    du1 = _mm(dz, w["wcat"][i], tb=True, name="mm_in_dx")
    g["wcat"] = _mm(sv["u1"], dz, ta=True, name="mm_in_dw")
    dh, g["ln_mix"] = _norm_bwd(m, sv["h"], w["ln_mix"][i:i + 1], du1, dh2, "norm_mix_bwd")
    return dh, g, dvf_out


def _loss_head(m, h, g, tgt):
    tile, d = m.tr, m.d

    def body(h_ref, g_ref, t_ref, loss_ref, dh_ref, dg_ref):
        def f(hv, gv):
            err = _rms(hv, gv) - t_ref[...]
            return 0.5 * jnp.sum(jnp.mean(err * err, axis=-1))

        val, vjp = jax.vjp(f, h_ref[...], g_ref[...])
        dh, dg = vjp(jnp.ones((), F32))
        dh_ref[...] = dh

        @pl.when(pl.program_id(0) == 0)
        def _():
            dg_ref[...] = jnp.zeros_like(dg_ref)
            loss_ref[...] = jnp.zeros_like(loss_ref)

        dg_ref[...] += dg
        loss_ref[...] += jnp.full(loss_ref.shape, val, F32)

    row = pl.BlockSpec((tile, d), lambda i: (i, 0))
    return pl.pallas_call(
        body, name="loss_head", grid=(m.t // tile,),
        in_specs=[row, pl.BlockSpec((1, d), lambda i: (0, 0)), row],
        out_specs=[pl.BlockSpec((1, LANES_V7X), lambda i: (0, 0)), row, pl.BlockSpec((1, d), lambda i: (0, 0))],
        out_shape=[jax.ShapeDtypeStruct((1, LANES_V7X), F32), jax.ShapeDtypeStruct((m.t, d), F32),
                   jax.ShapeDtypeStruct((1, d), F32)],
        compiler_params=_cparams(("arbitrary",)),
    )(h, g, tgt)


def _local_step(m, w, x, p, tgt, feed=None):
    h = x
    saved = []
    p_bf = p.astype(BF16)
    for i in range(m.nl):
        h, sv = _layer_fwd(m, w, i, h, p_bf[i], saved[0]["zs"] if i > 0 else None, feed)
        saved.append(sv)
    loss_row, dh, d_ln_final = _loss_head(m, h, w["ln_final"], tgt)
    grads = [None] * m.nl
    dvf = None
    for i in reversed(range(m.nl)):
        dh, grads[i], dvf_i = _layer_bwd(m, w, i, dh, saved[i], p_bf[i], saved[0]["zs"] if i > 0 else None,
                                         dvf if i == 0 else None)
        if i > 0:
            dvf = dvf_i if dvf is None else dvf + dvf_i
    return loss_row, dh, grads, d_ln_final


_BIG = ("w_o", "w_gate", "w_up", "w_down", "w_ple_gate", "w_ple_proj")


def _lora_rows(m):
    o1 = m.lw
    o2 = o1 + m.la
    o3 = o2 + m.lg
    return {"w2p": (0, o1), "a2p": (o1, o2), "g2p": (o2, o3), "v2p": (o3, o3 + m.lv)}


def _prepare_weights(m, wf):
    w = {k: v for k, v in wf.items() if k not in _BIG and k not in ("w_in", "w_in_vres")}
    nl = m.nl
    for k in _BIG:
        w[k] = [wf[k][i].astype(BF16) for i in range(nl)] if k in wf else [None] * nl
    w["wcat"] = [None] * nl
    if "w_in" in wf:
        vres = jnp.concatenate([jnp.zeros((1, m.d, m.lv), BF16), wf["w_in_vres"].astype(BF16)], axis=0)
        pad = jnp.zeros((m.d, m.zw - m.din - m.lv), BF16)
        w["wcat"] = [jnp.concatenate([wf["w_in"][i].astype(BF16), vres[i], pad], axis=1) for i in range(nl)]
    mu_v = jnp.concatenate([jnp.zeros((1, m.lv), F32), wf["mu_shift_vres"]], axis=0)
    w["mu_pad"] = jnp.concatenate([jnp.zeros((nl, 2 * m.dl), F32), wf["mu_shift"], mu_v,
                                   jnp.zeros((nl, m.zw - m.din - m.lv), F32)], axis=1)[:, None, :]
    rows = _lora_rows(m)
    for name, src in (("w2p", "rwkv_w2"), ("a2p", "rwkv_a2"), ("g2p", "rwkv_g2"), ("v2p", "rwkv_v2")):
        lo, hi = rows[name]
        a = wf[src]
        w[name] = jnp.concatenate([jnp.zeros((a.shape[0], lo, m.dr), F32), a, jnp.zeros((a.shape[0], m.lz - hi, m.dr), F32)],
                                  axis=1)
    w["rk"] = wf["rwkv_rk"].reshape(nl, 1, m.dr)
    w["ln_final"] = wf["ln_final"].reshape(1, m.d)
    head = jnp.arange(m.dr, dtype=jnp.int32) // m.n
    w["bb"] = (head[:, None] == head[None, :]).astype(BF16)
    return w


def _unpack_grads(m, grads, d_ln_final):
    nl = m.nl
    out = {}

    def stack(key):
        return jnp.stack([grads[i][key] for i in range(nl)], axis=0)

    for k in _BIG + ("conv_a_w", "conv_f_w", "lru_wx", "lru_wa"):
        out[k] = stack(k)
    for k in ("ln_mix", "conv_a_b", "lru_bx", "lru_ba", "lru_lambda", "lru_norm", "rwkv_w0", "rwkv_a0", "rwkv_kk",
              "rwkv_ka", "rwkv_lnx_w", "rwkv_lnx_b", "ln_ffn", "conv_f_b", "ln_ple", "ln_ple_post"):
        out[k] = stack(k)[:, 0, :]
    wcat = stack("wcat")
    out["w_in"] = wcat[:, :, :m.din]
    out["w_in_vres"] = wcat[1:, :, m.din:m.din + m.lv]
    mu = stack("mu_pad")[:, 0, :]
    out["mu_shift"] = mu[:, :m.nsh]
    out["mu_shift_vres"] = mu[1:, m.nsh:m.nsh + m.lv]
    rows = _lora_rows(m)
    for name, dst in (("w2p", "rwkv_w2"), ("a2p", "rwkv_a2"), ("g2p", "rwkv_g2")):
        lo, hi = rows[name]
        out[dst] = stack(name)[:, lo:hi, :]
    lo, hi = rows["v2p"]
    out["rwkv_v2"] = jnp.stack([grads[i]["v2p"] for i in range(1, nl)], axis=0)[:, lo:hi, :]
    out["rwkv_v0"] = jnp.stack([grads[i]["rwkv_v0"] for i in range(1, nl)], axis=0)[:, 0, :]
    out["rwkv_rk"] = stack("rk").reshape(nl, m.h, m.n)
    out["ln_final"] = d_ln_final.reshape(m.d)
    return out


_ANY = pl.BlockSpec(memory_space=pl.ANY)


def _position():
    return lax.axis_index("x"), lax.axis_index("y"), lax.axis_index("c")


def _other_chips(x, y):
    return [(1 - x, y), (x, 1 - y), (1 - x, 1 - y)]


LOCAL_COPY_PIECES = 4


def _gather_blob(shard):
    rows, wd = shard.shape
    assert rows % (2 * LOCAL_COPY_PIECES) == 0
    return shard.reshape(2, rows // 2, wd)


def _gathered_shape(blob):
    return jax.ShapeDtypeStruct((N_XY,) + blob.shape, blob.dtype)


def _gather_scratch(njobs):
    return [pltpu.SemaphoreType.DMA((6,)), pltpu.SemaphoreType.DMA((6,)),
            pltpu.SemaphoreType.DMA((2 * LOCAL_COPY_PIECES,))] * njobs


def _gather_copies(in_ref, out_ref, send_sems, recv_sems, local_sems):
    x, y, c = _position()
    me = 2 * x + y
    sends, hands, ici_in, d2d_in = [], [], [], []
    for k, (px, py) in enumerate(_other_chips(x, y)):
        landed = out_ref.at[2 * px + py, c]
        sends.append(pltpu.make_async_remote_copy(
            src_ref=in_ref.at[c], dst_ref=out_ref.at[me, c], send_sem=send_sems.at[k], recv_sem=recv_sems.at[k],
            device_id=(px, py, c), device_id_type=MESH))
        ici_in.append(pltpu.make_async_remote_copy(
            src_ref=in_ref.at[c], dst_ref=landed, send_sem=send_sems.at[k], recv_sem=recv_sems.at[k],
            device_id=(px, py, c), device_id_type=MESH))
        hands.append(pltpu.make_async_remote_copy(
            src_ref=landed, dst_ref=landed, send_sem=send_sems.at[3 + k], recv_sem=recv_sems.at[3 + k],
            device_id=(x, y, 1 - c), device_id_type=MESH))
        d2d_in.append(pltpu.make_async_remote_copy(
            src_ref=in_ref.at[c], dst_ref=out_ref.at[2 * px + py, 1 - c], send_sem=send_sems.at[3 + k],
            recv_sem=recv_sems.at[3 + k], device_id=(x, y, 1 - c), device_id_type=MESH))
    piece = in_ref.shape[1] // LOCAL_COPY_PIECES
    local = [pltpu.make_async_copy(in_ref.at[h, pl.ds(i * piece, piece)], out_ref.at[me, h, pl.ds(i * piece, piece)],
                                   local_sems.at[h * LOCAL_COPY_PIECES + i])
             for h in range(2) for i in range(LOCAL_COPY_PIECES)]
    return sends, hands, ici_in, d2d_in, local


def _gather_start(in_ref, out_ref, send_sems, recv_sems, local_sems):
    sends, _, _, _, local = _gather_copies(in_ref, out_ref, send_sems, recv_sems, local_sems)
    for cp in sends + local:
        cp.start()


def _gather_finish(in_ref, out_ref, send_sems, recv_sems, local_sems):
    sends, hands, ici_in, d2d_in, local = _gather_copies(in_ref, out_ref, send_sems, recv_sems, local_sems)
    for arrived, hand in zip(ici_in, hands):
        arrived.wait_recv()
        hand.start()
    for arrived in d2d_in:
        arrived.wait_recv()
    for cp in sends + hands:
        cp.wait_send()
    for cp in local:
        cp.wait()


def _all_gather_xy(blobs, *, name):
    ng = len(blobs)

    def body(*refs):
        for q in range(ng):
            _gather_start(refs[q], refs[ng + q], *refs[2 * ng + 3 * q:2 * ng + 3 * q + 3])
        for q in range(ng):
            _gather_finish(refs[q], refs[ng + q], *refs[2 * ng + 3 * q:2 * ng + 3 * q + 3])

    return pl.pallas_call(body, name=name, in_specs=[_ANY] * ng, out_specs=[_ANY] * ng,
                          out_shape=[_gathered_shape(b) for b in blobs], scratch_shapes=_gather_scratch(ng))(*blobs)


def _pair_send_half(g, *, name):
    nq, r, wd = g.shape
    half = r // 2

    def body(g_ref, out_ref, send_sem, recv_sem):
        x, y, c = _position()
        cp = pltpu.make_async_remote_copy(src_ref=g_ref.at[:, pl.ds((1 - c) * half, half), :], dst_ref=out_ref,
                                          send_sem=send_sem, recv_sem=recv_sem, device_id=(x, y, 1 - c), device_id_type=MESH)
        cp.start()
        cp.wait()

    return pl.pallas_call(
        body, name=name, in_specs=[_ANY], out_specs=_ANY, out_shape=jax.ShapeDtypeStruct((nq, half, wd), g.dtype),
        scratch_shapes=[pltpu.SemaphoreType.DMA(()), pltpu.SemaphoreType.DMA(())],
    )(g)


def _pair_sum(g, got, pos, *, name):
    nq, r, wd = g.shape
    half = r // 2
    tr = _tile(half, (256, 128, 64, 32, 16, 8))
    nb = half // tr

    def body(c_ref, g_ref, got_ref, o_ref):
        o_ref[...] = (g_ref[...] + got_ref[...]).astype(o_ref.dtype)

    grid_spec = pltpu.PrefetchScalarGridSpec(
        num_scalar_prefetch=1, grid=(nq, nb),
        in_specs=[pl.BlockSpec((1, tr, wd), lambda q, j, c_ref: (q, c_ref[0] * nb + j, 0)),
                  pl.BlockSpec((1, tr, wd), lambda q, j, c_ref: (q, j, 0))],
        out_specs=pl.BlockSpec((1, tr, wd), lambda q, j, c_ref: (q, j, 0)))
    return pl.pallas_call(body, name=name, grid_spec=grid_spec, out_shape=jax.ShapeDtypeStruct((nq, half, wd), BF16),
                          compiler_params=_cparams(("arbitrary", "arbitrary")))(pos[0], g, got)


def _exchange_xy(pb, *, name):
    def body(in_ref, out_ref, send_sems, recv_sems):
        x, y, c = _position()
        me = 2 * x + y
        sends = []
        for k, (px, py) in enumerate(_other_chips(x, y)):
            cp = pltpu.make_async_remote_copy(src_ref=in_ref.at[2 * px + py], dst_ref=out_ref.at[me], send_sem=send_sems.at[k],
                                              recv_sem=recv_sems.at[k], device_id=(px, py, c), device_id_type=MESH)
            cp.start()
            sends.append(cp)
        for k, (px, py) in enumerate(_other_chips(x, y)):
            pltpu.make_async_remote_copy(src_ref=in_ref.at[me], dst_ref=out_ref.at[2 * px + py], send_sem=send_sems.at[k],
                                         recv_sem=recv_sems.at[k], device_id=(px, py, c), device_id_type=MESH).wait_recv()
        for cp in sends:
            cp.wait_send()

    return pl.pallas_call(
        body, name=name, in_specs=[_ANY], out_specs=_ANY, out_shape=jax.ShapeDtypeStruct(pb.shape, pb.dtype),
        scratch_shapes=[pltpu.SemaphoreType.DMA((3,)), pltpu.SemaphoreType.DMA((3,))],
    )(pb)


def _chip_sum(parts, pb, pos, *, name):
    nq, half, wd = parts.shape
    tr = _tile(half, (256, 128, 64, 32, 16, 8))
    nb = half // tr

    def body(c_ref, x_ref, y_ref, p_ref, own_ref, o_ref):
        chip = 2 * x_ref[0] + y_ref[0]
        own = own_ref[0].astype(F32)
        acc = None
        for q in range(nq):
            term = jnp.where(chip == q, own, p_ref[q].astype(F32))
            acc = term if acc is None else acc + term
        o_ref[...] = acc

    grid_spec = pltpu.PrefetchScalarGridSpec(
        num_scalar_prefetch=3, grid=(nb,),
        in_specs=[pl.BlockSpec((nq, tr, wd), lambda j, c_ref, x_ref, y_ref: (0, j, 0)),
                  pl.BlockSpec((1, tr, wd), lambda j, c_ref, x_ref, y_ref: (2 * x_ref[0] + y_ref[0], j, 0))],
        out_specs=pl.BlockSpec((tr, wd), lambda j, c_ref, x_ref, y_ref: (c_ref[0] * nb + j, 0)))
    return pl.pallas_call(body, name=name, grid_spec=grid_spec, out_shape=jax.ShapeDtypeStruct((2 * half, wd), F32),
                          compiler_params=_cparams(("arbitrary",)))(*pos, parts, pb)


def _pair_gather(full, *, name):
    r, wd = full.shape
    half = r // 2

    def body(in_ref, out_ref, send_sem, recv_sem):
        x, y, c = _position()
        mine = out_ref.at[pl.ds(c * half, half), :]
        cp = pltpu.make_async_remote_copy(src_ref=mine, dst_ref=mine, send_sem=send_sem, recv_sem=recv_sem,
                                          device_id=(x, y, 1 - c), device_id_type=MESH)
        cp.start()
        pltpu.make_async_remote_copy(src_ref=mine, dst_ref=out_ref.at[pl.ds((1 - c) * half, half), :], send_sem=send_sem,
                                     recv_sem=recv_sem, device_id=(x, y, 1 - c), device_id_type=MESH).wait_recv()
        cp.wait_send()

    return pl.pallas_call(
        body, name=name, in_specs=[_ANY], out_specs=_ANY, out_shape=jax.ShapeDtypeStruct(full.shape, full.dtype),
        input_output_aliases={0: 0}, scratch_shapes=[pltpu.SemaphoreType.DMA(()), pltpu.SemaphoreType.DMA(())],
    )(full)


def _reduce_to_shard(g, pos, tag):
    got = _pair_send_half(g, name="rs_pair_send_" + tag)
    pb = _pair_sum(g, got, pos, name="rs_pair_sum_" + tag)
    parts = _exchange_xy(pb, name="rs_exchange_" + tag)
    full = _chip_sum(parts, pb, pos, name="rs_chip_sum_" + tag)
    return _pair_gather(full, name="rs_pair_gather_" + tag)


def _all_reduce_small(vec, *, name):
    r, wd = vec.shape

    def body(in_ref, out_ref, slots, send_sems, recv_sems):
        x, y, c = _position()
        me = 4 * x + 2 * y + c
        flips = [(fx, fy, fc) for fx in (0, 1) for fy in (0, 1) for fc in (0, 1) if fx + fy + fc]
        peers = [(1 - x if fx else x, 1 - y if fy else y, 1 - c if fc else c) for fx, fy, fc in flips]
        sends = []
        for k, peer in enumerate(peers):
            cp = pltpu.make_async_remote_copy(src_ref=in_ref, dst_ref=slots.at[me], send_sem=send_sems.at[k],
                                              recv_sem=recv_sems.at[k], device_id=peer, device_id_type=MESH)
            cp.start()
            sends.append(cp)
        slots[me] = in_ref[...]
        for k, (px, py, pc) in enumerate(peers):
            pltpu.make_async_remote_copy(src_ref=in_ref, dst_ref=slots.at[4 * px + 2 * py + pc], send_sem=send_sems.at[k],
                                         recv_sem=recv_sems.at[k], device_id=(px, py, pc), device_id_type=MESH).wait_recv()
        for cp in sends:
            cp.wait_send()
        acc = slots[0]
        for q in range(1, N_DEV):
            acc = acc + slots[q]
        out_ref[...] = acc

    vm = pl.BlockSpec(memory_space=pltpu.VMEM)
    return pl.pallas_call(
        body, name=name, in_specs=[vm], out_specs=vm, out_shape=jax.ShapeDtypeStruct((r, wd), F32),
        scratch_shapes=[pltpu.VMEM((N_DEV, r, wd), F32), pltpu.SemaphoreType.DMA((N_DEV - 1,)),
                        pltpu.SemaphoreType.DMA((N_DEV - 1,))],
        compiler_params=_cparams(),
    )(vec)


def _adamw(w, g, m, v, *, name):
    r, wd = w.shape
    tr = _tile(r, (256, 128, 64, 32, 16, 8))

    def body(w_ref, g_ref, m_ref, v_ref, d_ref, m_out, v_out):
        gv = g_ref[...]
        m_new = ADAM_B1 * m_ref[...] + (1.0 - ADAM_B1) * gv
        v_new = ADAM_B2 * v_ref[...] + (1.0 - ADAM_B2) * (gv * gv)
        m_hat = m_new / (1.0 - ADAM_B1 ** ADAM_STEP)
        v_hat = v_new / (1.0 - ADAM_B2 ** ADAM_STEP)
        d_ref[...] = -ADAM_LR * (m_hat / (jnp.sqrt(v_hat) + ADAM_EPS) + ADAM_WD * w_ref[...])
        m_out[...] = m_new
        v_out[...] = v_new

    spec = pl.BlockSpec((tr, wd), lambda j: (j, 0))
    return pl.pallas_call(body, name=name, grid=(r // tr,), in_specs=[spec] * 4, out_specs=[spec] * 3,
                          out_shape=[jax.ShapeDtypeStruct((r, wd), F32)] * 3, compiler_params=_cparams(("arbitrary",)))(w, g, m, v)


_WEIGHTS = ("ln_mix", "w_in", "w_in_vres", "mu_shift", "mu_shift_vres", "conv_a_w", "conv_a_b", "lru_wx", "lru_bx", "lru_wa",
            "lru_ba", "lru_lambda", "lru_norm", "rwkv_w0", "rwkv_w2", "rwkv_a0", "rwkv_a2", "rwkv_v0", "rwkv_v2", "rwkv_g2",
            "rwkv_kk", "rwkv_ka", "rwkv_rk", "rwkv_lnx_w", "rwkv_lnx_b", "w_o", "ln_ffn", "w_gate", "w_up", "conv_f_w",
            "conv_f_b", "w_down", "ln_ple", "w_ple_gate", "w_ple_proj", "ln_ple_post", "ln_final")
_SHARD_AXIS = {"w_in": 2, "w_in_vres": 1, "conv_a_w": 2, "lru_wx": 2, "lru_wa": 2, "rwkv_w2": 2, "rwkv_a2": 2, "rwkv_v2": 2,
               "rwkv_g2": 2, "w_o": 1, "w_gate": 2, "w_up": 2, "conv_f_w": 2, "w_down": 1, "w_ple_gate": 1, "w_ple_proj": 2}
_BIG_SHARDED = ("w_in",) + _BIG
_SMALL_SHARDED = tuple(k for k in _WEIGHTS if k in _SHARD_AXIS and k not in _BIG_SHARDED)
_REPLICATED = tuple(k for k in _WEIGHTS if k not in _SHARD_AXIS)
PACK_WIDTH = 512


def _to_shards(g, axis):
    n = g.shape[axis] // N_XY
    return jnp.moveaxis(g.reshape(g.shape[:axis] + (N_XY, n) + g.shape[axis + 1:]), axis, 0)


def _from_shards(s, axis):
    s = jnp.moveaxis(s, 0, axis)
    return s.reshape(s.shape[:axis] + (N_XY * s.shape[axis + 1],) + s.shape[axis + 2:])


def _pack(arrs, lead, width, row_mult):
    lead_shape = arrs[0].shape[:lead]
    flat = jnp.concatenate([a.reshape(lead_shape + (-1,)) for a in arrs], axis=-1)
    n = flat.shape[-1]
    total = _round_up(n, width * row_mult)
    flat = jnp.pad(flat, [(0, 0)] * lead + [(0, total - n)])
    return flat.reshape(lead_shape + (total // width, width))


def _unpack(packed, shapes):
    flat = packed.reshape(-1)
    out, o = [], 0
    for s in shapes:
        n = 1
        for dim in s:
            n *= dim
        out.append(flat[o:o + n].reshape(s))
        o += n
    return out


def _as2d(a):
    return a.reshape(-1, a.shape[-1])


def kernel(x, p, ln_mix, w_in, w_in_vres, mu_shift, mu_shift_vres, conv_a_w, conv_a_b, lru_wx, lru_bx, lru_wa, lru_ba, lru_lambda, lru_norm, rwkv_w0, rwkv_w2, rwkv_a0, rwkv_a2, rwkv_v0, rwkv_v2, rwkv_g2, rwkv_kk, rwkv_ka, rwkv_rk, rwkv_lnx_w, rwkv_lnx_b, w_o, ln_ffn, w_gate, w_up, conv_f_w, conv_f_b, w_down, ln_ple, w_ple_gate, w_ple_proj, ln_ple_post, ln_final, loss_target, m_ln_mix, m_w_in, m_w_in_vres, m_mu_shift, m_mu_shift_vres, m_conv_a_w, m_conv_a_b, m_lru_wx, m_lru_bx, m_lru_wa, m_lru_ba, m_lru_lambda, m_lru_norm, m_rwkv_w0, m_rwkv_w2, m_rwkv_a0, m_rwkv_a2, m_rwkv_v0, m_rwkv_v2, m_rwkv_g2, m_rwkv_kk, m_rwkv_ka, m_rwkv_rk, m_rwkv_lnx_w, m_rwkv_lnx_b, m_w_o, m_ln_ffn, m_w_gate, m_w_up, m_conv_f_w, m_conv_f_b, m_w_down, m_ln_ple, m_w_ple_gate, m_w_ple_proj, m_ln_ple_post, m_ln_final, v_ln_mix, v_w_in, v_w_in_vres, v_mu_shift, v_mu_shift_vres, v_conv_a_w, v_conv_a_b, v_lru_wx, v_lru_bx, v_lru_wa, v_lru_ba, v_lru_lambda, v_lru_norm, v_rwkv_w0, v_rwkv_w2, v_rwkv_a0, v_rwkv_a2, v_rwkv_v0, v_rwkv_v2, v_rwkv_g2, v_rwkv_kk, v_rwkv_ka, v_rwkv_rk, v_rwkv_lnx_w, v_rwkv_lnx_b, v_w_o, v_ln_ffn, v_w_gate, v_w_up, v_conv_f_w, v_conv_f_b, v_w_down, v_ln_ple, v_w_ple_gate, v_w_ple_proj, v_ln_ple_post, v_ln_final):
    a = dict(locals())
    x2, p, tgt = a["x"][0], a["p"][:, 0], a["loss_target"][0]
    pos = tuple(lax.axis_index(ax).astype(jnp.int32).reshape(1) for ax in ("c", "x", "y"))

    wf = {k: a[k] for k in _REPLICATED}
    small_shapes = [a[k].shape for k in _SMALL_SHARDED]
    shards = {k: [a[k][i].astype(BF16) for i in range(a[k].shape[0])] for k in _BIG_SHARDED}
    packed = _pack([a[k] for k in _SMALL_SHARDED], 0, PACK_WIDTH, 2 * LOCAL_COPY_PIECES * 8)
    got_small, got_w_in = _all_gather_xy([_gather_blob(packed), _gather_blob(shards["w_in"][0])], name="ag_first")
    got_small = got_small.reshape((N_XY,) + packed.shape)
    pieces = [_unpack(got_small[q], small_shapes) for q in range(N_XY)]
    for j, k in enumerate(_SMALL_SHARDED):
        wf[k] = _from_shards(jnp.stack([pieces[q][j] for q in range(N_XY)], axis=0), _SHARD_AXIS[k])

    m = _make_dims(x2, p, wf)
    w = _prepare_weights(m, wf)
    feed = _WeightFeed(m, w, shards, wf["w_in_vres"].astype(BF16))
    feed.arrive([("w_in", 0)], [got_w_in])
    loss_row, dx, grads, d_ln_final = _local_step(m, w, x2, p, tgt, feed)
    gfull = _unpack_grads(m, grads, d_ln_final)
    loss = lax.psum(loss_row[0, 0], ("x", "y", "c"))

    gred = {}
    for k in _BIG_SHARDED:
        gs = _to_shards(gfull[k], _SHARD_AXIS[k])
        gred[k] = _reduce_to_shard(gs.reshape(N_XY, -1, gs.shape[-1]), pos, k).reshape(gs.shape[1:])
    gs = _pack([_to_shards(gfull[k], _SHARD_AXIS[k]) for k in _SMALL_SHARDED], 1, PACK_WIDTH, 32)
    g_small = _reduce_to_shard(gs, pos, "small")
    rep_shapes = [a[k].shape for k in _REPLICATED]
    g_rep = _all_reduce_small(_pack([gfull[k] for k in _REPLICATED], 0, LANES_V7X, 8), name="ar_replicated")

    delta, new_m, new_v = {}, {}, {}
    for k in _BIG_SHARDED:
        res = _adamw(_as2d(a[k]), _as2d(gred[k]), _as2d(a["m_" + k]), _as2d(a["v_" + k]), name="adamw_" + k)
        delta[k], new_m[k], new_v[k] = (r.reshape(a[k].shape) for r in res)
    for names, shapes, g_packed, width, mult, tag in ((_SMALL_SHARDED, small_shapes, g_small, PACK_WIDTH, 32, "small"),
                                                      (_REPLICATED, rep_shapes, g_rep, LANES_V7X, 8, "replicated")):
        packs = [_pack([a[pre + k] for k in names], 0, width, mult) for pre in ("", "m_", "v_")]
        res = _adamw(packs[0], g_packed, packs[1], packs[2], name="adamw_" + tag)
        for dst, r in zip((gred, delta, new_m, new_v), [g_packed] + list(res)):
            dst.update(zip(names, _unpack(r, shapes)))
    return (loss, dx[None], *[gred[k] for k in _WEIGHTS], *[delta[k] for k in _WEIGHTS],
            *[new_m[k] for k in _WEIGHTS], *[new_v[k] for k in _WEIGHTS])
```

```python
import functools

import jax
import jax.numpy as jnp
from jax import lax
from jax.experimental import pallas as pl
from jax.experimental.pallas import tpu as pltpu

F32 = jnp.float32
BF16 = jnp.bfloat16
HIGHEST = lax.Precision.HIGHEST
MESH = pl.DeviceIdType.MESH

RMS_EPS = 1e-6
LNX_EPS = 64e-5
LRU_C = 8.0
ADAM_LR = 0.001
ADAM_B1 = 0.9
ADAM_B2 = 0.999
ADAM_EPS = 1e-08
ADAM_WD = 0.01
ADAM_STEP = 10

LANES_V7X = 128
VMEM_LIMIT_V7X = 60 * 1024 * 1024
WKV_CHUNK = 16
N_XY = 4
N_DEV = 8


def _cparams(sem=None, **kw):
    if sem is not None:
        kw["dimension_semantics"] = sem
    return pltpu.CompilerParams(vmem_limit_bytes=VMEM_LIMIT_V7X, **kw)


def _tile(dim, prefs):
    for t in prefs:
        if dim % t == 0:
            return t
    return dim


def _round_up(n, m):
    return (n + m - 1) // m * m


MM_MAX_TK = 2816


def _tile_k(kdim):
    best = None
    for t in range(LANES_V7X, min(kdim, MM_MAX_TK) + 1, LANES_V7X):
        if kdim % t == 0:
            best = t
    return best or kdim


def _mm(a, b, *, ta=False, tb=False, res=None, out_dtype=F32, name, gather=()):
    if ta:
        kdim, m = a.shape
    else:
        m, kdim = a.shape
    n = b.shape[0] if tb else b.shape[1]
    assert (b.shape[1] if tb else b.shape[0]) == kdim
    tk = _tile_k(kdim)
    tm = _tile(m, (2048, 1024, 512, 256, 128) if tk <= 2048 else (1024, 512, 256, 128))
    tn = _tile(n, (512, 256, 128))
    nk = kdim // tk
    ni, nj = m // tm, n // tn
    a_spec = pl.BlockSpec((tk, tm), lambda i, j, k: (k, i)) if ta else pl.BlockSpec((tm, tk), lambda i, j, k: (i, k))
    b_spec = pl.BlockSpec((tn, tk), lambda i, j, k: (j, k)) if tb else pl.BlockSpec((tk, tn), lambda i, j, k: (k, j))
    o_spec = pl.BlockSpec((tm, tn), lambda i, j, k: (i, j))
    dn = (((0 if ta else 1,), (1 if tb else 0,)), ((), ()))
    has_res = res is not None
    ng = len(gather)
    nin = 2 + has_res

    def body(*refs):
        a_ref, b_ref = refs[:2]
        r_ref = refs[2] if has_res else None
        g_in, o_ref, g_out = refs[nin:nin + ng], refs[nin + ng], refs[nin + ng + 1:nin + 2 * ng + 1]
        scratch = refs[nin + 2 * ng + 1:]
        acc_ref = scratch[0] if nk > 1 else None
        g_sems = scratch[1 if nk > 1 else 0:]
        i, j, k = pl.program_id(0), pl.program_id(1), pl.program_id(2)

        if ng:
            @pl.when((i == 0) & (j == 0) & (k == 0))
            def _():
                for q in range(ng):
                    _gather_start(g_in[q], g_out[q], *g_sems[2 * q:2 * q + 2])

        def finish(acc):
            if has_res:
                acc = acc + r_ref[...].astype(F32)
            o_ref[...] = acc.astype(out_dtype)

        prod = lax.dot_general(a_ref[...], b_ref[...], dn, preferred_element_type=F32)
        if nk == 1:
            finish(prod)
        else:
            @pl.when(k == 0)
            def _():
                acc_ref[...] = prod

            @pl.when(k > 0)
            def _():
                acc_ref[...] += prod

            @pl.when(k == nk - 1)
            def _():
                finish(acc_ref[...])

        if ng:
            @pl.when((i == ni - 1) & (j == nj - 1) & (k == nk - 1))
            def _():
                for q in range(ng):
                    _gather_finish(g_in[q], g_out[q], *g_sems[2 * q:2 * q + 2])

    ins = [a, b] + ([res] if has_res else []) + list(gather)
    in_specs = [a_spec, b_spec] + ([o_spec] if has_res else []) + [_ANY] * ng
    scratch = ([pltpu.VMEM((tm, tn), F32)] if nk > 1 else []) + _gather_scratch(ng)
    sem = ("arbitrary",) * 3 if ng else ("parallel", "parallel", "arbitrary")
    out = pl.pallas_call(
        body, name=name, grid=(ni, nj, nk), in_specs=in_specs, out_specs=[o_spec] + [_ANY] * ng,
        out_shape=[jax.ShapeDtypeStruct((m, n), out_dtype)] + [_gathered_shape(g) for g in gather],
        scratch_shapes=scratch, compiler_params=_cparams(sem),
    )(*ins)
    return (out[0], list(out[1:])) if ng else out[0]


def _stage_specs(axis, tile, tiled, params, consts, rows):
    specs = []
    for arr, width, cblk in tiled:
        if axis == 0:
            specs.append(pl.BlockSpec((tile, width), functools.partial(lambda i, c: (i, c), c=cblk)))
        else:
            specs.append(pl.BlockSpec((rows, tile), functools.partial(lambda i, c: (0, i + c), c=cblk)))
    for arr, cblk in params:
        if axis == 0:
            specs.append(pl.BlockSpec(arr.shape, functools.partial(lambda i, nd: (0,) * nd, nd=arr.ndim)))
        else:
            specs.append(pl.BlockSpec((arr.shape[0], tile), functools.partial(lambda i, c: (0, i + c), c=cblk)))
    for arr in consts:
        specs.append(pl.BlockSpec(arr.shape, functools.partial(lambda i, nd: (0,) * nd, nd=arr.ndim)))
    return specs


def _stage_fwd(fn, tiled, params, consts, outs, *, axis, tile, rows, name):
    nt, npar, nc = len(tiled), len(params), len(consts)
    ntiles = (rows // tile) if axis == 0 else (outs[0][0] // tile)

    def body(*refs):
        ins = refs[: nt + npar + nc]
        orefs = refs[nt + npar + nc:]
        vals = [r[...].astype(F32) for r in ins[: nt + npar]] + [r[...] for r in ins[nt + npar:]]
        ctx = pl.program_id(0) * tile
        res = fn(ctx, *vals)
        for o_ref, o in zip(orefs, res):
            o_ref[...] = o.astype(o_ref.dtype)

    if axis == 0:
        out_specs = [pl.BlockSpec((tile, w), lambda i: (i, 0)) for w, _ in outs]
    else:
        out_specs = [pl.BlockSpec((rows, tile), lambda i: (0, i)) for w, _ in outs]
    res = pl.pallas_call(
        body, name=name, grid=(ntiles,),
        in_specs=_stage_specs(axis, tile, tiled, params, consts, rows), out_specs=out_specs,
        out_shape=[jax.ShapeDtypeStruct((rows, w), dt) for w, dt in outs],
        compiler_params=_cparams(("arbitrary",)),
    )(*[t[0] for t in tiled], *[p[0] for p in params], *consts)
    return res


def _stage_bwd(fn, tiled, params, consts, cots, dtiled, *, axis, tile, rows, name, ncols=None):
    nt, npar, nc, nco = len(tiled), len(params), len(consts), len(cots)
    ntiles = (rows // tile) if axis == 0 else (ncols // tile)
    didx = [d[0] for d in dtiled]

    def body(*refs):
        ins = refs[: nt + npar + nc]
        crefs = refs[nt + npar + nc: nt + npar + nc + nco]
        orefs = refs[nt + npar + nc + nco:]
        vals = [r[...].astype(F32) for r in ins[: nt + npar]] + [r[...] for r in ins[nt + npar:]]
        ctx = pl.program_id(0) * tile

        def g(*dv):
            full = list(vals)
            for j, ix in enumerate(didx):
                full[ix] = dv[j]
            for j in range(npar):
                full[nt + j] = dv[len(didx) + j]
            return tuple(fn(ctx, *full))

        prim = [vals[ix] for ix in didx] + [vals[nt + j] for j in range(npar)]
        _, vjp = jax.vjp(g, *prim)
        grads = vjp(tuple(c[...].astype(F32) for c in crefs))
        for j in range(len(didx)):
            orefs[j][...] = grads[j].astype(orefs[j].dtype)
        for j in range(npar):
            o_ref = orefs[len(didx) + j]
            gp = grads[len(didx) + j]
            if axis == 0:
                @pl.when(pl.program_id(0) == 0)
                def _(o_ref=o_ref):
                    o_ref[...] = jnp.zeros_like(o_ref)

                o_ref[...] += gp
            else:
                o_ref[...] = gp

    if axis == 0:
        cot_specs = [pl.BlockSpec((tile, w), functools.partial(lambda i, c: (i, c), c=cb)) for _, w, cb in cots]
        out_specs = [pl.BlockSpec((tile, w), lambda i: (i, 0)) for _, w, _ in dtiled]
        out_specs += [pl.BlockSpec(p.shape, functools.partial(lambda i, nd: (0,) * nd, nd=p.ndim)) for p, _ in params]
        out_shape = [jax.ShapeDtypeStruct((rows, w), dt) for _, w, dt in dtiled]
        out_shape += [jax.ShapeDtypeStruct(p.shape, F32) for p, _ in params]
    else:
        cot_specs = [pl.BlockSpec((rows, tile), functools.partial(lambda i, c: (0, i + c), c=cb)) for _, cb in cots]
        out_specs = [pl.BlockSpec((rows, tile), lambda i: (0, i)) for _ in dtiled]
        out_specs += [pl.BlockSpec((p.shape[0], tile), lambda i: (0, i)) for p, _ in params]
        out_shape = [jax.ShapeDtypeStruct((rows, w), dt) for _, w, dt in dtiled]
        out_shape += [jax.ShapeDtypeStruct((p.shape[0], ncols), F32) for p, _ in params]
    return pl.pallas_call(
        body, name=name, grid=(ntiles,),
        in_specs=_stage_specs(axis, tile, tiled, params, consts, rows) + cot_specs, out_specs=out_specs,
        out_shape=out_shape, compiler_params=_cparams(("arbitrary",)),
    )(*[t[0] for t in tiled], *[p[0] for p in params], *consts, *[c[0] for c in cots])


def _rms(x, g):
    return x * lax.rsqrt(jnp.mean(x * x, axis=-1, keepdims=True) + RMS_EPS) * g


def _row_mask(x, k, first):
    t = lax.broadcasted_iota(jnp.int32, x.shape, 0)
    keep = (t >= k) if first else (t < x.shape[0] - k)
    return jnp.where(keep, x, 0.0)


@functools.partial(jax.custom_vjp, nondiff_argnums=(1,))
def _shift_down(x, k):
    return _row_mask(pltpu.roll(x, k, 0), k, True)


def _shift_down_fwd(x, k):
    return _shift_down(x, k), None


def _shift_down_bwd(k, _, g):
    return (_row_mask(pltpu.roll(g, g.shape[0] - k, 0), k, False),)


_shift_down.defvjp(_shift_down_fwd, _shift_down_bwd)


def _dwconv(x, w, b):
    kw = w.shape[0]
    out = x * w[kw - 1:kw] + b
    for j in range(kw - 1):
        out = out + _shift_down(x, kw - 1 - j) * w[j:j + 1]
    return out


def _f_norm(ctx, x, g):
    return (_rms(x, g),)


def _f_norm_res(ctx, x, g):
    return (_rms(x, g), x)


def _f_shiftmix(ctx, z, mu):
    return (z + (_shift_down(z, 1) - z) * mu,)


def _f_conv(ctx, x, w, b):
    return (_dwconv(x, w, b),)


def _f_ffn_act(ctx, gpre, up, w, b):
    return (jax.nn.gelu(_dwconv(gpre, w, b)) * up,)


def _make_f_lru_gates(heads):
    def fn(ctx, xb, wx, wa, bx, ba, lam):
        blk = xb.shape[1] // heads
        px, pa = [], []
        for h in range(heads):
            xh = xb[:, h * blk:(h + 1) * blk]
            px.append(jnp.dot(xh, wx[h], preferred_element_type=F32))
            pa.append(jnp.dot(xh, wa[h], preferred_element_type=F32))
        px = px[0] if heads == 1 else jnp.concatenate(px, axis=1)
        pa = pa[0] if heads == 1 else jnp.concatenate(pa, axis=1)
        gate_x = jax.nn.sigmoid(px + bx)
        gate_a = jax.nn.sigmoid(pa + ba)
        log_a = -LRU_C * gate_a * jax.nn.softplus(-lam)
        a = jnp.exp(log_a)
        mult = jnp.sqrt(1.0 - jnp.exp(2.0 * log_a))
        t = ctx + lax.broadcasted_iota(jnp.int32, xb.shape, 0)
        mult = jnp.where(t == 0, 1.0, mult)
        return a, xb * gate_x * mult

    return fn


def _f_lru_out(ctx, hl, ya, g):
    return (_rms(hl * jax.nn.gelu(ya), g),)


def _headsum_3pass(x, bb):
    hi = x.astype(BF16)
    r1 = x - hi.astype(F32)
    mid = r1.astype(BF16)
    lo = (r1 - mid.astype(F32)).astype(BF16)
    return (jnp.dot(hi, bb, preferred_element_type=F32) + jnp.dot(mid, bb, preferred_element_type=F32)
            + jnp.dot(lo, bb, preferred_element_type=F32))


@jax.custom_vjp
def _headsum(x, bb):
    return _headsum_3pass(x, bb)


def _headsum_fwd(x, bb):
    return _headsum_3pass(x, bb), bb


def _headsum_bwd(bb, g):
    return _headsum_3pass(g, bb), None


_headsum.defvjp(_headsum_fwd, _headsum_bwd)


def _make_f_rwkv_pre(has_vres, v_uses=0):
    def fn(ctx, *args):
        if v_uses:
            r, args = args[0], args[1:]
        if has_vres:
            k, v, lz, vf, w0, w2, a0, a2, g2, kkw, ka, v0, v2, bb = args
        else:
            k, v, lz, w0, w2, a0, a2, g2, kkw, ka, bb = args
        w_log = -jax.nn.softplus(-(w0 + jnp.dot(jnp.tanh(lz), w2, preferred_element_type=F32))) - 0.5
        logw = -jnp.exp(w_log)
        a = jax.nn.sigmoid(a0 + jnp.dot(lz, a2, preferred_element_type=F32))
        g = jnp.dot(jax.nn.sigmoid(lz), g2, preferred_element_type=F32)
        if has_vres:
            v = v + (vf - v) * jax.nn.sigmoid(v0 + jnp.dot(lz, v2, preferred_element_type=F32))
        xk = k * kkw
        kk = xk / jnp.maximum(jnp.sqrt(_headsum(xk * xk, bb)), 1e-12)
        k2 = k * (1.0 + (a - 1.0) * ka)
        if v_uses:
            return (r, r, logw, k2, k2) + (v,) * v_uses + (kk, kk * a, g)
        return logw, k2, v, kk, kk * a, g

    return fn


def _make_f_rwkv_post(head_size):
    def fn(ctx, y, r, k2, v2, g, lnw, lnb, rk, bb):
        mean = _headsum(y, bb) / head_size
        d = y - mean
        var = _headsum(d * d, bb) / head_size
        yn = d * lax.rsqrt(var + LNX_EPS) * lnw + lnb
        bonus = _headsum(r * k2 * rk, bb) * v2
        return ((yn + bonus) * g,)

    return fn


def _f_ple(ctx, h, eg, ep, g):
    return (h + _rms(jax.nn.sigmoid(eg) * ep, g),)


def _lru_scan(a, b, *, name):
    rows, cols = a.shape
    tc = _tile(cols, (512, 256, 128))

    def body(a_ref, b_ref, h_ref):
        def step(t, carry):
            h = a_ref[pl.ds(t, 1), :] * carry + b_ref[pl.ds(t, 1), :]
            h_ref[pl.ds(t, 1), :] = h
            return h

        lax.fori_loop(0, rows, step, jnp.zeros((1, tc), F32), unroll=8)

    spec = pl.BlockSpec((rows, tc), lambda j: (0, j))
    return pl.pallas_call(body, name=name, grid=(cols // tc,), in_specs=[spec, spec], out_specs=spec,
                          out_shape=jax.ShapeDtypeStruct((rows, cols), F32), compiler_params=_cparams(("arbitrary",)))(a, b)


def _lru_scan_bwd(a, h, dh, *, name):
    rows, cols = a.shape
    tc = _tile(cols, (512, 256, 128))

    def body(a_ref, h_ref, dh_ref, da_ref, db_ref):
        def step(i, carry):
            t = rows - 1 - i
            g = dh_ref[pl.ds(t, 1), :] + carry
            db_ref[pl.ds(t, 1), :] = g
            hp = h_ref[pl.ds(jnp.maximum(t - 1, 0), 1), :]
            da_ref[pl.ds(t, 1), :] = jnp.where(t > 0, g * hp, 0.0)
            return a_ref[pl.ds(t, 1), :] * g

        lax.fori_loop(0, rows, step, jnp.zeros((1, tc), F32), unroll=8)

    spec = pl.BlockSpec((rows, tc), lambda j: (0, j))
    return pl.pallas_call(body, name=name, grid=(cols // tc,), in_specs=[spec] * 3, out_specs=[spec] * 2,
                          out_shape=[jax.ShapeDtypeStruct((rows, cols), F32)] * 2,
                          compiler_params=_cparams(("arbitrary",)))(a, h, dh)


def _split_bf16(x):
    hi = x.astype(BF16)
    return hi, (x - hi.astype(F32)).astype(BF16)


def _dot3_passes(a, b, ca, cb):
    dn = (((ca,), (cb,)), ((), ()))
    ah, al = _split_bf16(a)
    bh, bl = _split_bf16(b)
    return (lax.dot_general(ah, bh, dn, preferred_element_type=F32) + lax.dot_general(al, bh, dn, preferred_element_type=F32)
            + lax.dot_general(ah, bl, dn, preferred_element_type=F32))


@functools.partial(jax.custom_vjp, nondiff_argnums=(2, 3))
def _dot3(a, b, ca, cb):
    return _dot3_passes(a, b, ca, cb)


def _dot3_fwd(a, b, ca, cb):
    return _dot3_passes(a, b, ca, cb), (a, b)


def _dot3_bwd(ca, cb, res, g):
    a, b = res
    fa, fb = 1 - ca, 1 - cb
    da = _dot3_passes(g, b, 1, fb) if ca == 1 else _dot3_passes(b, g, fb, 1)
    db = _dot3_passes(a, g, fa, 0) if cb == 0 else _dot3_passes(g, a, 0, fa)
    return da, db


_dot3.defvjp(_dot3_fwd, _dot3_bwd)


def _each(f, *lists):
    return [f(*t) for t in zip(*lists)]


def _wkv_local(r, lw, k, v, kk, b):
    c, n = r[0].shape
    row = lax.broadcasted_iota(jnp.int32, (c, c), 0)
    col = lax.broadcasted_iota(jnp.int32, (c, c), 1)
    incl = (row >= col).astype(F32)
    strict = (row > col).astype(F32)
    eye = lax.broadcasted_iota(jnp.int32, (n, n), 0) == lax.broadcasted_iota(jnp.int32, (n, n), 1)
    cl = _each(lambda x: _dot3(incl, x, 1, 0), lw)
    w_t = _each(jnp.exp, cl)
    inv_w = _each(lambda x: jnp.exp(-x), cl)
    kk_s = _each(lambda x, y, z: x * jnp.exp(y - z), kk, cl, lw)
    b_s = _each(jnp.multiply, b, inv_w)
    k_s = _each(jnp.multiply, k, inv_w)
    r_s = _each(jnp.multiply, r, w_t)
    q = _each(lambda x, y: jnp.concatenate([x, y], axis=0), kk_s, r_s)
    qb = _each(lambda x, y: _dot3(x, y, 1, 1), q, b_s)
    qk = _each(lambda x, y: _dot3(x, y, 1, 1), q, k_s)
    m = _each(lambda x: -strict * x[:c], qb)
    pb = _each(lambda x: incl * x[c:], qb)
    lkv = _each(lambda x, y: _dot3(strict * x[:c], y, 1, 0), qk, v)
    pkv = _each(lambda x, y: _dot3(incl * x[c:], y, 1, 0), qk, v)
    a = _each(lambda x, y: jnp.concatenate([x, y], axis=1), kk_s, lkv)
    steps = max(1, (c - 1).bit_length())
    for i in range(steps):
        a = _each(lambda x, y: y + _dot3(x, y, 1, 0), m, a)
        if i + 1 < steps:
            m = _each(lambda x: _dot3(x, x, 1, 0), m)
    ry = _each(lambda x, y, z, w: jnp.concatenate([x, y], axis=1) - _dot3(z, w, 1, 0), r_s, pkv, pb, a)
    w_end = _each(lambda x: x[c - 1:c, :], w_t)
    gu_low = _each(lambda x, y, z: _dot3(x, y * z, 0, 0), a, b_s, w_end)
    g = _each(lambda x, y: jnp.where(eye, jnp.broadcast_to(x, (n, n)), 0.0) - y[:n], w_end, gu_low)
    u = _each(lambda x, y, z, w: _dot3(x, y * z, 0, 0) - w[n:], v, k_s, w_end, gu_low)
    return g, u, _each(lambda x: x[:, :n], ry), _each(lambda x: x[:, n:], ry)


def _wkv_blocks(h, nchunk):
    return (_tile(h, (4, 2, 1)), _tile(nchunk, (4, 2, 1))), (h, _tile(nchunk, (4, 2, 1)))


def _wkv_fwd(r, lw, k, v, kk, b, *, name, gather=(), gather_state=()):
    h, t, n = r.shape
    c = WKV_CHUNK
    nchunk = t // c
    (hb, cb), (hs, cs) = _wkv_blocks(h, nchunk)
    ng = len(gather)
    ni, nj = h // hb, nchunk // cb

    pairs = [(i, j) for i in range(hb) for j in range(cb)]

    def local_body(*refs):
        ins, g_in = refs[:6], refs[6:6 + ng]
        g_ref, u_ref, r2_ref, y0_ref = refs[6 + ng:10 + ng]
        g_out, g_sems = refs[10 + ng:10 + 2 * ng], refs[10 + 2 * ng:]
        if ng:
            @pl.when((pl.program_id(0) == 0) & (pl.program_id(1) == 0))
            def _():
                for q in range(ng):
                    _gather_start(g_in[q], g_out[q], *g_sems[2 * q:2 * q + 2])

        g, u, r2, y0 = _wkv_local(*[[ref[i, pl.ds(j * c, c)] for i, j in pairs] for ref in ins])
        for idx, (i, j) in enumerate(pairs):
            g_ref[i, j] = g[idx]
            u_ref[i, j] = u[idx]
            r2_ref[i, pl.ds(j * c, c)] = r2[idx]
            y0_ref[i, pl.ds(j * c, c)] = y0[idx]
        if ng:
            @pl.when((pl.program_id(0) == ni - 1) & (pl.program_id(1) == nj - 1))
            def _():
                for q in range(ng):
                    _gather_finish(g_in[q], g_out[q], *g_sems[2 * q:2 * q + 2])

    seq = pl.BlockSpec((hb, cb * c, n), lambda i, j: (i, j, 0))
    mat = pl.BlockSpec((hb, cb, n, n), lambda i, j: (i, j, 0, 0))
    res = pl.pallas_call(
        local_body, name=name + "_local", grid=(ni, nj), in_specs=[seq] * 6 + [_ANY] * ng,
        out_specs=[mat, mat, seq, seq] + [_ANY] * ng,
        out_shape=[jax.ShapeDtypeStruct((h, nchunk, n, n), F32)] * 2 + [jax.ShapeDtypeStruct((h, t, n), F32)] * 2
        + [_gathered_shape(g) for g in gather],
        scratch_shapes=_gather_scratch(ng),
        compiler_params=_cparams(("arbitrary", "arbitrary") if ng else ("parallel", "parallel")),
    )(r, lw, k, v, kk, b, *gather)
    gm, um, r2, y0 = res[:4]
    gathered = list(res[4:])

    ng2 = len(gather_state)
    nsteps = nchunk // cs

    def state_body(*refs):
        g_ref, u_ref, r2_ref, y0_ref = refs[:4]
        g_in, (y_ref, st_ref) = refs[4:4 + ng2], refs[4 + ng2:6 + ng2]
        g_out, s_ref, g_sems = refs[6 + ng2:6 + 2 * ng2], refs[6 + 2 * ng2], refs[7 + 2 * ng2:]

        @pl.when(pl.program_id(0) == 0)
        def _():
            s_ref[...] = jnp.zeros_like(s_ref)
            for q in range(ng2):
                _gather_start(g_in[q], g_out[q], *g_sems[2 * q:2 * q + 2])

        s = [s_ref[i] for i in range(hs)]
        for j in range(cs):
            rows = pl.ds(j * c, c)
            for i in range(hs):
                st_ref[i, j] = s[i]
                y_ref[i, rows] = _dot3(r2_ref[i, rows], s[i], 1, 1) + y0_ref[i, rows]
            s = [_dot3(s[i], g_ref[i, j], 1, 0) + u_ref[i, j] for i in range(hs)]
        for i in range(hs):
            s_ref[i] = s[i]
        if ng2:
            @pl.when(pl.program_id(0) == nsteps - 1)
            def _():
                for q in range(ng2):
                    _gather_finish(g_in[q], g_out[q], *g_sems[2 * q:2 * q + 2])

    seq = pl.BlockSpec((hs, cs * c, n), lambda j: (0, j, 0))
    mat = pl.BlockSpec((hs, cs, n, n), lambda j: (0, j, 0, 0))
    res = pl.pallas_call(
        state_body, name=name + "_state", grid=(nsteps,), in_specs=[mat, mat, seq, seq] + [_ANY] * ng2,
        out_specs=[seq, mat] + [_ANY] * ng2,
        out_shape=[jax.ShapeDtypeStruct((h, t, n), F32), jax.ShapeDtypeStruct((h, nchunk, n, n), F32)]
        + [_gathered_shape(g) for g in gather_state],
        scratch_shapes=[pltpu.VMEM((hs, n, n), F32)] + _gather_scratch(ng2), compiler_params=_cparams(("arbitrary",)),
    )(gm, um, r2, y0, *gather_state)
    return res[0], (res[1], gm, r2), gathered + list(res[2:])


def _wkv_bwd(r, lw, k, v, kk, b, saved, dy, *, name):
    states, gm, r2 = saved
    h, t, n = r.shape
    c = WKV_CHUNK
    nchunk = t // c
    (hb, cb), (hs, cs) = _wkv_blocks(h, nchunk)
    nsteps = nchunk // cs

    def state_body(g_ref, r2_ref, st_ref, dy_ref, dg_ref, du_ref, dr2_ref, ds_ref):
        @pl.when(pl.program_id(0) == 0)
        def _():
            ds_ref[...] = jnp.zeros_like(ds_ref)

        ds = [ds_ref[i] for i in range(hs)]
        for j in reversed(range(cs)):
            rows = pl.ds(j * c, c)
            for i in range(hs):
                s0 = st_ref[i, j]
                du_ref[i, j] = ds[i]
                dg_ref[i, j] = _dot3(s0, ds[i], 0, 0)
                dr2_ref[i, rows] = _dot3(dy_ref[i, rows], s0, 1, 0)
            ds = [_dot3(dy_ref[i, rows], r2_ref[i, rows], 0, 0) + _dot3(ds[i], g_ref[i, j], 1, 1) for i in range(hs)]
        for i in range(hs):
            ds_ref[i] = ds[i]

    seq = pl.BlockSpec((hs, cs * c, n), lambda j: (0, nsteps - 1 - j, 0))
    mat = pl.BlockSpec((hs, cs, n, n), lambda j: (0, nsteps - 1 - j, 0, 0))
    dg, du, dr2 = pl.pallas_call(
        state_body, name=name + "_state", grid=(nsteps,), in_specs=[mat, seq, mat, seq], out_specs=[mat, mat, seq],
        out_shape=[jax.ShapeDtypeStruct((h, nchunk, n, n), F32)] * 2 + [jax.ShapeDtypeStruct((h, t, n), F32)],
        scratch_shapes=[pltpu.VMEM((hs, n, n), F32)], compiler_params=_cparams(("arbitrary",)),
    )(gm, r2, states, dy)

    pairs = [(i, j) for i in range(hb) for j in range(cb)]

    def local_body(*refs):
        ins, (dg_ref, du_ref, dr2_ref, dy_ref), out_refs = refs[:6], refs[6:10], refs[10:]
        _, vjp = jax.vjp(_wkv_local, *[[ref[i, pl.ds(j * c, c)] for i, j in pairs] for ref in ins])
        grads = vjp(([dg_ref[i, j] for i, j in pairs], [du_ref[i, j] for i, j in pairs],
                     [dr2_ref[i, pl.ds(j * c, c)] for i, j in pairs], [dy_ref[i, pl.ds(j * c, c)] for i, j in pairs]))
        for o_ref, gr in zip(out_refs, grads):
            for idx, (i, j) in enumerate(pairs):
                o_ref[i, pl.ds(j * c, c)] = gr[idx]

    seq = pl.BlockSpec((hb, cb * c, n), lambda i, j: (i, j, 0))
    mat = pl.BlockSpec((hb, cb, n, n), lambda i, j: (i, j, 0, 0))
    return pl.pallas_call(
        local_body, name=name + "_local", grid=(h // hb, nchunk // cb), in_specs=[seq] * 6 + [mat, mat, seq, seq],
        out_specs=[seq] * 6, out_shape=[jax.ShapeDtypeStruct((h, t, n), F32)] * 6,
        compiler_params=_cparams(("parallel", "parallel")),
    )(r, lw, k, v, kk, b, dg, du, dr2, dy)


class _Dims:
    pass


def _make_dims(x, p, w):
    m = _Dims()
    m.t, m.d = x.shape[-2], x.shape[-1]
    m.nl = w["ln_mix"].shape[0]
    m.dl = w["conv_a_b"].shape[1]
    m.hl = w["lru_wx"].shape[1]
    m.dr = w["rwkv_w0"].shape[1]
    m.h, m.n = w["rwkv_rk"].shape[1], w["rwkv_rk"].shape[2]
    m.lw, m.la, m.lg, m.lv = (w[k].shape[1] for k in ("rwkv_w2", "rwkv_a2", "rwkv_g2", "rwkv_v2"))
    m.nsh = w["mu_shift"].shape[1]
    m.ff = w["conv_f_b"].shape[1]
    m.ple = p.shape[-1]
    m.din = 2 * m.dl + m.nsh
    m.lz = _round_up(m.lw + m.la + m.lg + m.lv, LANES_V7X)
    m.zw = _round_up(2 * m.dl + 3 * m.dr + m.lz, 512)
    m.zs = m.zw - 2 * m.dl
    m.tr = _tile(m.t, (256, 128, 64, 32, 16, 8))
    m.trb = _tile(m.t, (128, 64, 32, 16, 8))
    m.tcs = _tile(m.zs, (512, 256, 128))
    assert (3 * m.dr) % m.lz == 0 and (2 * m.dl) % m.tcs == 0 and m.t % WKV_CHUNK == 0
    assert m.nsh == 3 * m.dr + m.lw + m.la + m.lg
    return m


def _to_heads(m, a):
    return jnp.transpose(a.reshape(m.t, m.h, m.n), (1, 0, 2))


def _from_heads(m, a):
    return jnp.transpose(a, (1, 0, 2)).reshape(m.t, m.dr)


def _norm_fwd(m, h, g, name):
    return _stage_fwd(_f_norm, [(h, m.d, 0)], [(g, 0)], [], [(m.d, BF16)], axis=0, tile=m.tr, rows=m.t, name=name)[0]


def _norm_bwd(m, h, g, du, dres, name):
    return _stage_bwd(_f_norm_res, [(h, m.d, 0)], [(g, 0)], [], [(du, m.d, 0), (dres, m.d, 0)], [(0, m.d, F32)],
                      axis=0, tile=m.tr, rows=m.t, name=name)


def _rwkv_pre_operands(m, w, i, sv, v_first_zs, with_r):
    zs = sv["zs"]
    tiled = ([(zs, m.dr, 0)] if with_r else []) + [(zs, m.dr, 1), (zs, m.dr, 2), (zs, m.lz, 3 * m.dr // m.lz)]
    params = [(w["rwkv_w0"][i:i + 1], 0), (w["w2p"][i], 0), (w["rwkv_a0"][i:i + 1], 0), (w["a2p"][i], 0),
              (w["g2p"][i], 0), (w["rwkv_kk"][i:i + 1], 0), (w["rwkv_ka"][i:i + 1], 0)]
    if i > 0:
        tiled.append((v_first_zs, m.dr, 2))
        params += [(w["rwkv_v0"][i - 1:i], 0), (w["v2p"][i - 1], 0)]
    return tiled, params


def _rwkv_post_operands(m, w, i, sv):
    tiled = [(sv["y"], m.dr, 0), (sv["zs"], m.dr, 0), (sv["k2"], m.dr, 0), (sv["v2"], m.dr, 0), (sv["g"], m.dr, 0)]
    params = [(w["rwkv_lnx_w"][i:i + 1], 0), (w["rwkv_lnx_b"][i:i + 1], 0), (w["rk"][i], 0)]
    return tiled, params


def _lru_gate_params(w, i):
    return [(w["lru_wx"][i], 0), (w["lru_wa"][i], 0), (w["lru_bx"][i:i + 1], 0), (w["lru_ba"][i:i + 1], 0),
            (w["lru_lambda"][i:i + 1], 0)]


class _WeightFeed:
    def __init__(self, m, w, shards, vres, chip):
        self.m, self.w, self.shards, self.vres, self.chip = m, w, shards, vres, chip

    def keys(self, carrier, i):
        plan = {"mm_in": [("w_o", i)] if i == 0 else [],
                "wkv_local": [("w_gate", i), ("w_up", i)], "wkv_state": [("w_down", i)],
                "mm_gate": [("w_ple_gate", i)], "mm_up": [("w_ple_proj", i), ("w_o", i + 1)],
                "mm_down": [("w_in", i + 1)], "mm_pgate": []}
        return [key for key in plan[carrier] if key[1] < self.m.nl]

    def blobs(self, keys):
        return [_gather_blob(self.shards[name][layer]) for name, layer in keys]

    def arrive(self, keys, gathered):
        m = self.m
        for (name, layer), got in zip(keys, gathered):
            shard = self.shards[name][layer]
            full = _own_slot(got.reshape((N_XY,) + shard.shape), shard, self.chip)
            full = _from_shards(full, _SHARD_AXIS[name] - 1)
            if name == "w_in":
                vres = self.vres[layer - 1] if layer > 0 else jnp.zeros((m.d, m.lv), BF16)
                self.w["wcat"][layer] = jnp.concatenate([full, vres, jnp.zeros((m.d, m.zw - m.din - m.lv), BF16)], axis=1)
            else:
                self.w[name][layer] = full


def _mm_fed(feed, carrier, i, a, b, **kw):
    keys = feed.keys(carrier, i) if feed is not None else []
    if not keys:
        return _mm(a, b, **kw)
    out, got = _mm(a, b, gather=feed.blobs(keys), **kw)
    feed.arrive(keys, got)
    return out


def _layer_fwd(m, w, i, h, p_bf, v_first_zs, feed=None):
    sv = {"h": h}
    t, dl, dr = m.t, m.dl, m.dr
    sv["u1"] = _norm_fwd(m, h, w["ln_mix"][i:i + 1], "norm_mix")
    z = sv["z"] = _mm_fed(feed, "mm_in", i, sv["u1"], w["wcat"][i], name="mm_in")
    off = 2 * dl // m.tcs
    sv["zs"] = _stage_fwd(_f_shiftmix, [(z, None, off)], [(w["mu_pad"][i], off)], [], [(m.zs, F32)],
                          axis=1, tile=m.tcs, rows=t, name="shiftmix")[0]
    tca = _tile(dl, (512, 256, 128))
    sv["xb"] = _stage_fwd(_f_conv, [(z, None, 0)], [(w["conv_a_w"][i], 0), (w["conv_a_b"][i:i + 1], 0)], [],
                          [(dl, F32)], axis=1, tile=tca, rows=t, name="conv_a")[0]
    sv["a"], b_in = _stage_fwd(_make_f_lru_gates(m.hl), [(sv["xb"], dl, 0)], _lru_gate_params(w, i), [],
                               [(dl, F32), (dl, F32)], axis=0, tile=m.tr, rows=t, name="lru_gates")
    sv["hl"] = _lru_scan(sv["a"], b_in, name="lru_scan")
    out_a = _stage_fwd(_f_lru_out, [(sv["hl"], dl, 0), (z, dl, 1)], [(w["lru_norm"][i:i + 1], 0)], [],
                       [(dl, BF16)], axis=0, tile=m.tr, rows=t, name="lru_out")[0]
    tiled, params = _rwkv_pre_operands(m, w, i, sv, v_first_zs, False)
    pre = _stage_fwd(_make_f_rwkv_pre(i > 0), tiled, params, [w["bb"]], [(dr, F32)] * 6,
                     axis=0, tile=m.tr, rows=t, name="rwkv_pre")
    sv["logw"], sv["k2"], sv["v2"], sv["kk"], sv["b"], sv["g"] = pre
    heads = [_to_heads(m, a) for a in (sv["zs"][:, :dr], sv["logw"], sv["k2"], sv["v2"], sv["kk"], sv["b"])]
    keys = [feed.keys(carrier, i) if feed is not None else [] for carrier in ("wkv_local", "wkv_state")]
    y_h, sv["states"], got = _wkv_fwd(*heads, name="wkv_fwd", gather=feed.blobs(keys[0]) if keys[0] else (),
                                      gather_state=feed.blobs(keys[1]) if keys[1] else ())
    if keys[0] or keys[1]:
        feed.arrive(keys[0] + keys[1], got)
    sv["y"] = _from_heads(m, y_h)
    tiled, params = _rwkv_post_operands(m, w, i, sv)
    out_b = _stage_fwd(_make_f_rwkv_post(m.n), tiled, params, [w["bb"]], [(dr, BF16)],
                       axis=0, tile=m.tr, rows=t, name="rwkv_post")[0]
    sv["cat"] = jnp.concatenate([out_a, out_b], axis=1)
    h2 = sv["h2"] = _mm(sv["cat"], w["w_o"][i], res=h, name="mm_o")
    sv["u2"] = _norm_fwd(m, h2, w["ln_ffn"][i:i + 1], "norm_ffn")
    sv["gpre"] = _mm_fed(feed, "mm_gate", i, sv["u2"], w["w_gate"][i], name="mm_gate")
    sv["up"] = _mm_fed(feed, "mm_up", i, sv["u2"], w["w_up"][i], name="mm_up")
    tcf = _tile(m.ff, (512, 256, 128))
    sv["act"] = _stage_fwd(_f_ffn_act, [(sv["gpre"], None, 0), (sv["up"], None, 0)],
                           [(w["conv_f_w"][i], 0), (w["conv_f_b"][i:i + 1], 0)], [], [(m.ff, BF16)],
                           axis=1, tile=tcf, rows=t, name="ffn_act")[0]
    h3 = sv["h3"] = _mm_fed(feed, "mm_down", i, sv["act"], w["w_down"][i], res=h2, name="mm_down")
    sv["u3"] = _norm_fwd(m, h3, w["ln_ple"][i:i + 1], "norm_ple")
    sv["eg"] = _mm_fed(feed, "mm_pgate", i, sv["u3"], w["w_ple_gate"][i], name="mm_pgate")
    sv["ep"] = _mm(p_bf, w["w_ple_proj"][i], name="mm_pproj")
    h4 = _stage_fwd(_f_ple, [(h3, m.d, 0), (sv["eg"], m.d, 0), (sv["ep"], m.d, 0)], [(w["ln_ple_post"][i:i + 1], 0)],
                    [], [(m.d, F32)], axis=0, tile=m.tr, rows=t, name="ple")[0]
    return h4, sv


def _layer_bwd(m, w, i, dh4, sv, p_bf, v_first_zs, dvf_in):
    t, d, dl, dr = m.t, m.d, m.dl, m.dr
    g = {}
    deg, dep, g["ln_ple_post"] = _stage_bwd(
        _f_ple, [(sv["h3"], d, 0), (sv["eg"], d, 0), (sv["ep"], d, 0)], [(w["ln_ple_post"][i:i + 1], 0)], [],
        [(dh4, d, 0)], [(1, d, BF16), (2, d, BF16)], axis=0, tile=m.tr, rows=t, name="ple_bwd")
    du3 = _mm(deg, w["w_ple_gate"][i], tb=True, name="mm_pgate_dx")
    g["w_ple_gate"] = _mm(sv["u3"], deg, ta=True, name="mm_pgate_dw")
    g["w_ple_proj"] = _mm(p_bf, dep, ta=True, name="mm_pproj_dw")
    dh3, g["ln_ple"] = _norm_bwd(m, sv["h3"], w["ln_ple"][i:i + 1], du3, dh4, "norm_ple_bwd")
    dh3_bf = dh3.astype(BF16)
    dact = _mm(dh3_bf, w["w_down"][i], tb=True, name="mm_down_dx")
    g["w_down"] = _mm(sv["act"], dh3_bf, ta=True, name="mm_down_dw")
    tcf = _tile(m.ff, (512, 256, 128))
    dgpre, dup, g["conv_f_w"], g["conv_f_b"] = _stage_bwd(
        _f_ffn_act, [(sv["gpre"], None, 0), (sv["up"], None, 0)], [(w["conv_f_w"][i], 0), (w["conv_f_b"][i:i + 1], 0)],
        [], [(dact, 0)], [(0, m.ff, BF16), (1, m.ff, BF16)], axis=1, tile=tcf, rows=t, ncols=m.ff, name="ffn_act_bwd")
    du2 = _mm(dgpre, w["w_gate"][i], tb=True, name="mm_gate_dx")
    du2 = _mm(dup, w["w_up"][i], tb=True, res=du2, name="mm_up_dx")
    g["w_gate"] = _mm(sv["u2"], dgpre, ta=True, name="mm_gate_dw")
    g["w_up"] = _mm(sv["u2"], dup, ta=True, name="mm_up_dw")
    dh2, g["ln_ffn"] = _norm_bwd(m, sv["h2"], w["ln_ffn"][i:i + 1], du2, dh3, "norm_ffn_bwd")
    dh2_bf = dh2.astype(BF16)
    dcat = _mm(dh2_bf, w["w_o"][i], tb=True, name="mm_o_dx")
    g["w_o"] = _mm(sv["cat"], dh2_bf, ta=True, name="mm_o_dw")
    tiled, params = _rwkv_post_operands(m, w, i, sv)
    dy, dr_a, dk2_a, dv2_a, dg, g["rwkv_lnx_w"], g["rwkv_lnx_b"], g["rk"] = _stage_bwd(
        _make_f_rwkv_post(m.n), tiled, params, [w["bb"]], [(dcat, dr, dl // dr)], [(j, dr, F32) for j in range(5)],
        axis=0, tile=m.trb, rows=t, name="rwkv_post_bwd")
    heads = [_to_heads(m, a) for a in (sv["zs"][:, :dr], sv["logw"], sv["k2"], sv["v2"], sv["kk"], sv["b"])]
    dwkv = _wkv_bwd(*heads, sv["states"], _to_heads(m, dy), name="wkv_bwd")
    dr_b, dlw, dk2_b, dv2_b, dkk, db = [_from_heads(m, a) for a in dwkv]
    tiled, params = _rwkv_pre_operands(m, w, i, sv, v_first_zs, True)
    v_cots = [dv2_a, dv2_b] + ([dvf_in] if dvf_in is not None else [])
    cots = [(c, dr, 0) for c in [dr_a, dr_b, dlw, dk2_a, dk2_b] + v_cots + [dkk, db, dg]]
    ntil = len(tiled)
    dtiled = [(0, dr, F32), (1, dr, F32), (2, dr, F32), (3, m.lz, F32)] + ([(4, dr, F32)] if i > 0 else [])
    res = _stage_bwd(_make_f_rwkv_pre(i > 0, len(v_cots)), tiled, params, [w["bb"]], cots, dtiled,
                     axis=0, tile=m.trb, rows=t, name="rwkv_pre_bwd")
    d_r, d_k, d_v, d_lz = res[:4]
    dvf_out = res[4] if i > 0 else None
    pg = res[ntil:]
    g["rwkv_w0"], g["w2p"], g["rwkv_a0"], g["a2p"], g["g2p"], g["rwkv_kk"], g["rwkv_ka"] = pg[:7]
    if i > 0:
        g["rwkv_v0"], g["v2p"] = pg[7:9]
    dzs = jnp.concatenate([d_r, d_k, d_v, d_lz, jnp.zeros((t, m.zs - 3 * dr - m.lz), F32)], axis=1)
    off = 2 * dl // m.tcs
    dzr, g["mu_pad"] = _stage_bwd(_f_shiftmix, [(sv["z"], None, off)], [(w["mu_pad"][i], off)], [], [(dzs, 0)],
                                  [(0, m.zs, BF16)], axis=1, tile=m.tcs, rows=t, ncols=m.zs, name="shiftmix_bwd")
    dhl, dya, g["lru_norm"] = _stage_bwd(
        _f_lru_out, [(sv["hl"], dl, 0), (sv["z"], dl, 1)], [(w["lru_norm"][i:i + 1], 0)], [], [(dcat, dl, 0)],
        [(0, dl, F32), (1, dl, BF16)], axis=0, tile=m.tr, rows=t, name="lru_out_bwd")
    da, db_in = _lru_scan_bwd(sv["a"], sv["hl"], dhl, name="lru_scan_bwd")
    dxb, g["lru_wx"], g["lru_wa"], g["lru_bx"], g["lru_ba"], g["lru_lambda"] = _stage_bwd(
        _make_f_lru_gates(m.hl), [(sv["xb"], dl, 0)], _lru_gate_params(w, i), [], [(da, dl, 0), (db_in, dl, 0)],
        [(0, dl, F32)], axis=0, tile=m.tr, rows=t, name="lru_gates_bwd")
    tca = _tile(dl, (512, 256, 128))
    dxa, g["conv_a_w"], g["conv_a_b"] = _stage_bwd(
        _f_conv, [(sv["z"], None, 0)], [(w["conv_a_w"][i], 0), (w["conv_a_b"][i:i + 1], 0)], [], [(dxb, 0)],
        [(0, dl, BF16)], axis=1, tile=tca, rows=t, ncols=dl, name="conv_a_bwd")
    dz = jnp.concatenate([dxa, dya, dzr], axis=1)
    du1 = _mm(dz, w["wcat"][i], tb=True, name="mm_in_dx")
    g["wcat"] = _mm(sv["u1"], dz, ta=True, name="mm_in_dw")
    dh, g["ln_mix"] = _norm_bwd(m, sv["h"], w["ln_mix"][i:i + 1], du1, dh2, "norm_mix_bwd")
    return dh, g, dvf_out


def _loss_head(m, h, g, tgt):
    tile, d = m.tr, m.d

    def body(h_ref, g_ref, t_ref, loss_ref, dh_ref, dg_ref):
        def f(hv, gv):
            err = _rms(hv, gv) - t_ref[...]
            return 0.5 * jnp.sum(jnp.mean(err * err, axis=-1))

        val, vjp = jax.vjp(f, h_ref[...], g_ref[...])
        dh, dg = vjp(jnp.ones((), F32))
        dh_ref[...] = dh

        @pl.when(pl.program_id(0) == 0)
        def _():
            dg_ref[...] = jnp.zeros_like(dg_ref)
            loss_ref[...] = jnp.zeros_like(loss_ref)

        dg_ref[...] += dg
        loss_ref[...] += jnp.full(loss_ref.shape, val, F32)

    row = pl.BlockSpec((tile, d), lambda i: (i, 0))
    return pl.pallas_call(
        body, name="loss_head", grid=(m.t // tile,),
        in_specs=[row, pl.BlockSpec((1, d), lambda i: (0, 0)), row],
        out_specs=[pl.BlockSpec((1, LANES_V7X), lambda i: (0, 0)), row, pl.BlockSpec((1, d), lambda i: (0, 0))],
        out_shape=[jax.ShapeDtypeStruct((1, LANES_V7X), F32), jax.ShapeDtypeStruct((m.t, d), F32),
                   jax.ShapeDtypeStruct((1, d), F32)],
        compiler_params=_cparams(("arbitrary",)),
    )(h, g, tgt)


def _local_step(m, w, x, p, tgt, feed=None):
    h = x
    saved = []
    p_bf = p.astype(BF16)
    for i in range(m.nl):
        h, sv = _layer_fwd(m, w, i, h, p_bf[i], saved[0]["zs"] if i > 0 else None, feed)
        saved.append(sv)
    loss_row, dh, d_ln_final = _loss_head(m, h, w["ln_final"], tgt)
    grads = [None] * m.nl
    dvf = None
    for i in reversed(range(m.nl)):
        dh, grads[i], dvf_i = _layer_bwd(m, w, i, dh, saved[i], p_bf[i], saved[0]["zs"] if i > 0 else None,
                                         dvf if i == 0 else None)
        if i > 0:
            dvf = dvf_i if dvf is None else dvf + dvf_i
    return loss_row, dh, grads, d_ln_final


_BIG = ("w_o", "w_gate", "w_up", "w_down", "w_ple_gate", "w_ple_proj")


def _lora_rows(m):
    o1 = m.lw
    o2 = o1 + m.la
    o3 = o2 + m.lg
    return {"w2p": (0, o1), "a2p": (o1, o2), "g2p": (o2, o3), "v2p": (o3, o3 + m.lv)}


def _prepare_weights(m, wf):
    w = {k: v for k, v in wf.items() if k not in _BIG and k not in ("w_in", "w_in_vres")}
    nl = m.nl
    for k in _BIG:
        w[k] = [wf[k][i].astype(BF16) for i in range(nl)] if k in wf else [None] * nl
    w["wcat"] = [None] * nl
    if "w_in" in wf:
        vres = jnp.concatenate([jnp.zeros((1, m.d, m.lv), BF16), wf["w_in_vres"].astype(BF16)], axis=0)
        pad = jnp.zeros((m.d, m.zw - m.din - m.lv), BF16)
        w["wcat"] = [jnp.concatenate([wf["w_in"][i].astype(BF16), vres[i], pad], axis=1) for i in range(nl)]
    mu_v = jnp.concatenate([jnp.zeros((1, m.lv), F32), wf["mu_shift_vres"]], axis=0)
    w["mu_pad"] = jnp.concatenate([jnp.zeros((nl, 2 * m.dl), F32), wf["mu_shift"], mu_v,
                                   jnp.zeros((nl, m.zw - m.din - m.lv), F32)], axis=1)[:, None, :]
    rows = _lora_rows(m)
    for name, src in (("w2p", "rwkv_w2"), ("a2p", "rwkv_a2"), ("g2p", "rwkv_g2"), ("v2p", "rwkv_v2")):
        lo, hi = rows[name]
        a = wf[src]
        w[name] = jnp.concatenate([jnp.zeros((a.shape[0], lo, m.dr), F32), a, jnp.zeros((a.shape[0], m.lz - hi, m.dr), F32)],
                                  axis=1)
    w["rk"] = wf["rwkv_rk"].reshape(nl, 1, m.dr)
    w["ln_final"] = wf["ln_final"].reshape(1, m.d)
    head = jnp.arange(m.dr, dtype=jnp.int32) // m.n
    w["bb"] = (head[:, None] == head[None, :]).astype(BF16)
    return w


def _unpack_grads(m, grads, d_ln_final):
    nl = m.nl
    out = {}

    def stack(key):
        return jnp.stack([grads[i][key] for i in range(nl)], axis=0)

    for k in _BIG + ("conv_a_w", "conv_f_w", "lru_wx", "lru_wa"):
        out[k] = stack(k)
    for k in ("ln_mix", "conv_a_b", "lru_bx", "lru_ba", "lru_lambda", "lru_norm", "rwkv_w0", "rwkv_a0", "rwkv_kk",
              "rwkv_ka", "rwkv_lnx_w", "rwkv_lnx_b", "ln_ffn", "conv_f_b", "ln_ple", "ln_ple_post"):
        out[k] = stack(k)[:, 0, :]
    wcat = stack("wcat")
    out["w_in"] = wcat[:, :, :m.din]
    out["w_in_vres"] = wcat[1:, :, m.din:m.din + m.lv]
    mu = stack("mu_pad")[:, 0, :]
    out["mu_shift"] = mu[:, :m.nsh]
    out["mu_shift_vres"] = mu[1:, m.nsh:m.nsh + m.lv]
    rows = _lora_rows(m)
    for name, dst in (("w2p", "rwkv_w2"), ("a2p", "rwkv_a2"), ("g2p", "rwkv_g2")):
        lo, hi = rows[name]
        out[dst] = stack(name)[:, lo:hi, :]
    lo, hi = rows["v2p"]
    out["rwkv_v2"] = jnp.stack([grads[i]["v2p"] for i in range(1, nl)], axis=0)[:, lo:hi, :]
    out["rwkv_v0"] = jnp.stack([grads[i]["rwkv_v0"] for i in range(1, nl)], axis=0)[:, 0, :]
    out["rwkv_rk"] = stack("rk").reshape(nl, m.h, m.n)
    out["ln_final"] = d_ln_final.reshape(m.d)
    return out


_ANY = pl.BlockSpec(memory_space=pl.ANY)


def _position():
    return lax.axis_index("x"), lax.axis_index("y"), lax.axis_index("c")


def _other_chips(x, y):
    return [(1 - x, y), (x, 1 - y), (1 - x, 1 - y)]


def _gather_blob(shard):
    rows, wd = shard.shape
    return shard.reshape(2, rows // 2, wd)


def _gathered_shape(blob):
    return jax.ShapeDtypeStruct((N_XY,) + blob.shape, blob.dtype)


def _gather_scratch(njobs):
    return [pltpu.SemaphoreType.DMA((6,)), pltpu.SemaphoreType.DMA((6,))] * njobs


def _own_slot(gathered, shard, chip):
    return jnp.stack([jnp.where(chip == q, shard, gathered[q]) for q in range(N_XY)], axis=0)


def _gather_copies(in_ref, out_ref, send_sems, recv_sems):
    x, y, c = _position()
    me = 2 * x + y
    sends, hands, ici_in, d2d_in = [], [], [], []
    for k, (px, py) in enumerate(_other_chips(x, y)):
        landed = out_ref.at[2 * px + py, c]
        sends.append(pltpu.make_async_remote_copy(
            src_ref=in_ref.at[c], dst_ref=out_ref.at[me, c], send_sem=send_sems.at[k], recv_sem=recv_sems.at[k],
            device_id=(px, py, c), device_id_type=MESH))
        ici_in.append(pltpu.make_async_remote_copy(
            src_ref=in_ref.at[c], dst_ref=landed, send_sem=send_sems.at[k], recv_sem=recv_sems.at[k],
            device_id=(px, py, c), device_id_type=MESH))
        hands.append(pltpu.make_async_remote_copy(
            src_ref=landed, dst_ref=landed, send_sem=send_sems.at[3 + k], recv_sem=recv_sems.at[3 + k],
            device_id=(x, y, 1 - c), device_id_type=MESH))
        d2d_in.append(pltpu.make_async_remote_copy(
            src_ref=in_ref.at[c], dst_ref=out_ref.at[2 * px + py, 1 - c], send_sem=send_sems.at[3 + k],
            recv_sem=recv_sems.at[3 + k], device_id=(x, y, 1 - c), device_id_type=MESH))
    return sends, hands, ici_in, d2d_in


def _gather_start(in_ref, out_ref, send_sems, recv_sems):
    for cp in _gather_copies(in_ref, out_ref, send_sems, recv_sems)[0]:
        cp.start()


def _gather_finish(in_ref, out_ref, send_sems, recv_sems):
    sends, hands, ici_in, d2d_in = _gather_copies(in_ref, out_ref, send_sems, recv_sems)
    for arrived, hand in zip(ici_in, hands):
        arrived.wait_recv()
        hand.start()
    for arrived in d2d_in:
        arrived.wait_recv()
    for cp in sends + hands:
        cp.wait_send()


def _all_gather_xy(blobs, *, name):
    ng = len(blobs)

    def body(*refs):
        for q in range(ng):
            _gather_start(refs[q], refs[ng + q], *refs[2 * ng + 2 * q:2 * ng + 2 * q + 2])
        for q in range(ng):
            _gather_finish(refs[q], refs[ng + q], *refs[2 * ng + 2 * q:2 * ng + 2 * q + 2])

    return pl.pallas_call(body, name=name, in_specs=[_ANY] * ng, out_specs=[_ANY] * ng,
                          out_shape=[_gathered_shape(b) for b in blobs], scratch_shapes=_gather_scratch(ng))(*blobs)


def _pair_send_half(g, *, name):
    nq, r, wd = g.shape
    half = r // 2

    def body(g_ref, out_ref, send_sem, recv_sem):
        x, y, c = _position()
        cp = pltpu.make_async_remote_copy(src_ref=g_ref.at[:, pl.ds((1 - c) * half, half), :], dst_ref=out_ref,
                                          send_sem=send_sem, recv_sem=recv_sem, device_id=(x, y, 1 - c), device_id_type=MESH)
        cp.start()
        cp.wait()

    return pl.pallas_call(
        body, name=name, in_specs=[_ANY], out_specs=_ANY, out_shape=jax.ShapeDtypeStruct((nq, half, wd), g.dtype),
        scratch_shapes=[pltpu.SemaphoreType.DMA(()), pltpu.SemaphoreType.DMA(())],
    )(g)


def _pair_sum(g, got, pos, *, name):
    nq, r, wd = g.shape
    half = r // 2
    tr = _tile(half, (256, 128, 64, 32, 16, 8))
    nb = half // tr

    def body(c_ref, g_ref, got_ref, o_ref):
        o_ref[...] = (g_ref[...] + got_ref[...]).astype(o_ref.dtype)

    grid_spec = pltpu.PrefetchScalarGridSpec(
        num_scalar_prefetch=1, grid=(nq, nb),
        in_specs=[pl.BlockSpec((1, tr, wd), lambda q, j, c_ref: (q, c_ref[0] * nb + j, 0)),
                  pl.BlockSpec((1, tr, wd), lambda q, j, c_ref: (q, j, 0))],
        out_specs=pl.BlockSpec((1, tr, wd), lambda q, j, c_ref: (q, j, 0)))
    return pl.pallas_call(body, name=name, grid_spec=grid_spec, out_shape=jax.ShapeDtypeStruct((nq, half, wd), BF16),
                          compiler_params=_cparams(("arbitrary", "arbitrary")))(pos[0], g, got)


def _exchange_xy(pb, *, name):
    def body(in_ref, out_ref, send_sems, recv_sems):
        x, y, c = _position()
        me = 2 * x + y
        sends = []
        for k, (px, py) in enumerate(_other_chips(x, y)):
            cp = pltpu.make_async_remote_copy(src_ref=in_ref.at[2 * px + py], dst_ref=out_ref.at[me], send_sem=send_sems.at[k],
                                              recv_sem=recv_sems.at[k], device_id=(px, py, c), device_id_type=MESH)
            cp.start()
            sends.append(cp)
        for k, (px, py) in enumerate(_other_chips(x, y)):
            pltpu.make_async_remote_copy(src_ref=in_ref.at[me], dst_ref=out_ref.at[2 * px + py], send_sem=send_sems.at[k],
                                         recv_sem=recv_sems.at[k], device_id=(px, py, c), device_id_type=MESH).wait_recv()
        for cp in sends:
            cp.wait_send()

    return pl.pallas_call(
        body, name=name, in_specs=[_ANY], out_specs=_ANY, out_shape=jax.ShapeDtypeStruct(pb.shape, pb.dtype),
        scratch_shapes=[pltpu.SemaphoreType.DMA((3,)), pltpu.SemaphoreType.DMA((3,))],
    )(pb)


def _chip_sum(parts, pb, pos, *, name):
    nq, half, wd = parts.shape
    tr = _tile(half, (256, 128, 64, 32, 16, 8))
    nb = half // tr

    def body(c_ref, x_ref, y_ref, p_ref, own_ref, o_ref):
        chip = 2 * x_ref[0] + y_ref[0]
        own = own_ref[0].astype(F32)
        acc = None
        for q in range(nq):
            term = jnp.where(chip == q, own, p_ref[q].astype(F32))
            acc = term if acc is None else acc + term
        o_ref[...] = acc

    grid_spec = pltpu.PrefetchScalarGridSpec(
        num_scalar_prefetch=3, grid=(nb,),
        in_specs=[pl.BlockSpec((nq, tr, wd), lambda j, c_ref, x_ref, y_ref: (0, j, 0)),
                  pl.BlockSpec((1, tr, wd), lambda j, c_ref, x_ref, y_ref: (2 * x_ref[0] + y_ref[0], j, 0))],
        out_specs=pl.BlockSpec((tr, wd), lambda j, c_ref, x_ref, y_ref: (c_ref[0] * nb + j, 0)))
    return pl.pallas_call(body, name=name, grid_spec=grid_spec, out_shape=jax.ShapeDtypeStruct((2 * half, wd), F32),
                          compiler_params=_cparams(("arbitrary",)))(*pos, parts, pb)


def _pair_gather(full, *, name):
    r, wd = full.shape
    half = r // 2

    def body(in_ref, out_ref, send_sem, recv_sem):
        x, y, c = _position()
        mine = out_ref.at[pl.ds(c * half, half), :]
        cp = pltpu.make_async_remote_copy(src_ref=mine, dst_ref=mine, send_sem=send_sem, recv_sem=recv_sem,
                                          device_id=(x, y, 1 - c), device_id_type=MESH)
        cp.start()
        pltpu.make_async_remote_copy(src_ref=mine, dst_ref=out_ref.at[pl.ds((1 - c) * half, half), :], send_sem=send_sem,
                                     recv_sem=recv_sem, device_id=(x, y, 1 - c), device_id_type=MESH).wait_recv()
        cp.wait_send()

    return pl.pallas_call(
        body, name=name, in_specs=[_ANY], out_specs=_ANY, out_shape=jax.ShapeDtypeStruct(full.shape, full.dtype),
        input_output_aliases={0: 0}, scratch_shapes=[pltpu.SemaphoreType.DMA(()), pltpu.SemaphoreType.DMA(())],
    )(full)


def _reduce_to_shard(g, pos, tag):
    got = _pair_send_half(g, name="rs_pair_send_" + tag)
    pb = _pair_sum(g, got, pos, name="rs_pair_sum_" + tag)
    parts = _exchange_xy(pb, name="rs_exchange_" + tag)
    full = _chip_sum(parts, pb, pos, name="rs_chip_sum_" + tag)
    return _pair_gather(full, name="rs_pair_gather_" + tag)


def _all_reduce_small(vec, *, name):
    r, wd = vec.shape

    def body(in_ref, out_ref, slots, send_sems, recv_sems):
        x, y, c = _position()
        me = 4 * x + 2 * y + c
        flips = [(fx, fy, fc) for fx in (0, 1) for fy in (0, 1) for fc in (0, 1) if fx + fy + fc]
        peers = [(1 - x if fx else x, 1 - y if fy else y, 1 - c if fc else c) for fx, fy, fc in flips]
        sends = []
        for k, peer in enumerate(peers):
            cp = pltpu.make_async_remote_copy(src_ref=in_ref, dst_ref=slots.at[me], send_sem=send_sems.at[k],
                                              recv_sem=recv_sems.at[k], device_id=peer, device_id_type=MESH)
            cp.start()
            sends.append(cp)
        slots[me] = in_ref[...]
        for k, (px, py, pc) in enumerate(peers):
            pltpu.make_async_remote_copy(src_ref=in_ref, dst_ref=slots.at[4 * px + 2 * py + pc], send_sem=send_sems.at[k],
                                         recv_sem=recv_sems.at[k], device_id=(px, py, pc), device_id_type=MESH).wait_recv()
        for cp in sends:
            cp.wait_send()
        acc = slots[0]
        for q in range(1, N_DEV):
            acc = acc + slots[q]
        out_ref[...] = acc

    vm = pl.BlockSpec(memory_space=pltpu.VMEM)
    return pl.pallas_call(
        body, name=name, in_specs=[vm], out_specs=vm, out_shape=jax.ShapeDtypeStruct((r, wd), F32),
        scratch_shapes=[pltpu.VMEM((N_DEV, r, wd), F32), pltpu.SemaphoreType.DMA((N_DEV - 1,)),
                        pltpu.SemaphoreType.DMA((N_DEV - 1,))],
        compiler_params=_cparams(),
    )(vec)


def _adamw(w, g, m, v, *, name):
    r, wd = w.shape
    tr = _tile(r, (256, 128, 64, 32, 16, 8))

    def body(w_ref, g_ref, m_ref, v_ref, d_ref, m_out, v_out):
        gv = g_ref[...]
        m_new = ADAM_B1 * m_ref[...] + (1.0 - ADAM_B1) * gv
        v_new = ADAM_B2 * v_ref[...] + (1.0 - ADAM_B2) * (gv * gv)
        m_hat = m_new / (1.0 - ADAM_B1 ** ADAM_STEP)
        v_hat = v_new / (1.0 - ADAM_B2 ** ADAM_STEP)
        d_ref[...] = -ADAM_LR * (m_hat / (jnp.sqrt(v_hat) + ADAM_EPS) + ADAM_WD * w_ref[...])
        m_out[...] = m_new
        v_out[...] = v_new

    spec = pl.BlockSpec((tr, wd), lambda j: (j, 0))
    return pl.pallas_call(body, name=name, grid=(r // tr,), in_specs=[spec] * 4, out_specs=[spec] * 3,
                          out_shape=[jax.ShapeDtypeStruct((r, wd), F32)] * 3, compiler_params=_cparams(("arbitrary",)))(w, g, m, v)


_WEIGHTS = ("ln_mix", "w_in", "w_in_vres", "mu_shift", "mu_shift_vres", "conv_a_w", "conv_a_b", "lru_wx", "lru_bx", "lru_wa",
            "lru_ba", "lru_lambda", "lru_norm", "rwkv_w0", "rwkv_w2", "rwkv_a0", "rwkv_a2", "rwkv_v0", "rwkv_v2", "rwkv_g2",
            "rwkv_kk", "rwkv_ka", "rwkv_rk", "rwkv_lnx_w", "rwkv_lnx_b", "w_o", "ln_ffn", "w_gate", "w_up", "conv_f_w",
            "conv_f_b", "w_down", "ln_ple", "w_ple_gate", "w_ple_proj", "ln_ple_post", "ln_final")
_SHARD_AXIS = {"w_in": 2, "w_in_vres": 1, "conv_a_w": 2, "lru_wx": 2, "lru_wa": 2, "rwkv_w2": 2, "rwkv_a2": 2, "rwkv_v2": 2,
               "rwkv_g2": 2, "w_o": 1, "w_gate": 2, "w_up": 2, "conv_f_w": 2, "w_down": 1, "w_ple_gate": 1, "w_ple_proj": 2}
_BIG_SHARDED = ("w_in",) + _BIG
_SMALL_SHARDED = tuple(k for k in _WEIGHTS if k in _SHARD_AXIS and k not in _BIG_SHARDED)
_REPLICATED = tuple(k for k in _WEIGHTS if k not in _SHARD_AXIS)
PACK_WIDTH = 512


def _to_shards(g, axis):
    n = g.shape[axis] // N_XY
    return jnp.moveaxis(g.reshape(g.shape[:axis] + (N_XY, n) + g.shape[axis + 1:]), axis, 0)


def _from_shards(s, axis):
    s = jnp.moveaxis(s, 0, axis)
    return s.reshape(s.shape[:axis] + (N_XY * s.shape[axis + 1],) + s.shape[axis + 2:])


def _pack(arrs, lead, width, row_mult):
    lead_shape = arrs[0].shape[:lead]
    flat = jnp.concatenate([a.reshape(lead_shape + (-1,)) for a in arrs], axis=-1)
    n = flat.shape[-1]
    total = _round_up(n, width * row_mult)
    flat = jnp.pad(flat, [(0, 0)] * lead + [(0, total - n)])
    return flat.reshape(lead_shape + (total // width, width))


def _unpack(packed, shapes):
    flat = packed.reshape(-1)
    out, o = [], 0
    for s in shapes:
        n = 1
        for dim in s:
            n *= dim
        out.append(flat[o:o + n].reshape(s))
        o += n
    return out


def _as2d(a):
    return a.reshape(-1, a.shape[-1])


def kernel(x, p, ln_mix, w_in, w_in_vres, mu_shift, mu_shift_vres, conv_a_w, conv_a_b, lru_wx, lru_bx, lru_wa, lru_ba, lru_lambda, lru_norm, rwkv_w0, rwkv_w2, rwkv_a0, rwkv_a2, rwkv_v0, rwkv_v2, rwkv_g2, rwkv_kk, rwkv_ka, rwkv_rk, rwkv_lnx_w, rwkv_lnx_b, w_o, ln_ffn, w_gate, w_up, conv_f_w, conv_f_b, w_down, ln_ple, w_ple_gate, w_ple_proj, ln_ple_post, ln_final, loss_target, m_ln_mix, m_w_in, m_w_in_vres, m_mu_shift, m_mu_shift_vres, m_conv_a_w, m_conv_a_b, m_lru_wx, m_lru_bx, m_lru_wa, m_lru_ba, m_lru_lambda, m_lru_norm, m_rwkv_w0, m_rwkv_w2, m_rwkv_a0, m_rwkv_a2, m_rwkv_v0, m_rwkv_v2, m_rwkv_g2, m_rwkv_kk, m_rwkv_ka, m_rwkv_rk, m_rwkv_lnx_w, m_rwkv_lnx_b, m_w_o, m_ln_ffn, m_w_gate, m_w_up, m_conv_f_w, m_conv_f_b, m_w_down, m_ln_ple, m_w_ple_gate, m_w_ple_proj, m_ln_ple_post, m_ln_final, v_ln_mix, v_w_in, v_w_in_vres, v_mu_shift, v_mu_shift_vres, v_conv_a_w, v_conv_a_b, v_lru_wx, v_lru_bx, v_lru_wa, v_lru_ba, v_lru_lambda, v_lru_norm, v_rwkv_w0, v_rwkv_w2, v_rwkv_a0, v_rwkv_a2, v_rwkv_v0, v_rwkv_v2, v_rwkv_g2, v_rwkv_kk, v_rwkv_ka, v_rwkv_rk, v_rwkv_lnx_w, v_rwkv_lnx_b, v_w_o, v_ln_ffn, v_w_gate, v_w_up, v_conv_f_w, v_conv_f_b, v_w_down, v_ln_ple, v_w_ple_gate, v_w_ple_proj, v_ln_ple_post, v_ln_final):
    a = dict(locals())
    x2, p, tgt = a["x"][0], a["p"][:, 0], a["loss_target"][0]
    pos = tuple(lax.axis_index(ax).astype(jnp.int32).reshape(1) for ax in ("c", "x", "y"))

    wf = {k: a[k] for k in _REPLICATED}
    small_shapes = [a[k].shape for k in _SMALL_SHARDED]
    shards = {k: [a[k][i].astype(BF16) for i in range(a[k].shape[0])] for k in _BIG_SHARDED}
    chip = 2 * lax.axis_index("x") + lax.axis_index("y")
    packed = _pack([a[k] for k in _SMALL_SHARDED], 0, PACK_WIDTH, 16)
    got_small, got_w_in = _all_gather_xy([_gather_blob(packed), _gather_blob(shards["w_in"][0])], name="ag_first")
    got_small = _own_slot(got_small.reshape((N_XY,) + packed.shape), packed, chip)
    pieces = [_unpack(got_small[q], small_shapes) for q in range(N_XY)]
    for j, k in enumerate(_SMALL_SHARDED):
        wf[k] = _from_shards(jnp.stack([pieces[q][j] for q in range(N_XY)], axis=0), _SHARD_AXIS[k])

    m = _make_dims(x2, p, wf)
    w = _prepare_weights(m, wf)
    feed = _WeightFeed(m, w, shards, wf["w_in_vres"].astype(BF16), chip)
    feed.arrive([("w_in", 0)], [got_w_in])
    loss_row, dx, grads, d_ln_final = _local_step(m, w, x2, p, tgt, feed)
    gfull = _unpack_grads(m, grads, d_ln_final)
    loss = lax.psum(loss_row[0, 0], ("x", "y", "c"))

    gred = {}
    for k in _BIG_SHARDED:
        gs = _to_shards(gfull[k], _SHARD_AXIS[k])
        gred[k] = _reduce_to_shard(gs.reshape(N_XY, -1, gs.shape[-1]), pos, k).reshape(gs.shape[1:])
    gs = _pack([_to_shards(gfull[k], _SHARD_AXIS[k]) for k in _SMALL_SHARDED], 1, PACK_WIDTH, 32)
    g_small = _reduce_to_shard(gs, pos, "small")
    rep_shapes = [a[k].shape for k in _REPLICATED]
    g_rep = _all_reduce_small(_pack([gfull[k] for k in _REPLICATED], 0, LANES_V7X, 8), name="ar_replicated")

    delta, new_m, new_v = {}, {}, {}
    for k in _BIG_SHARDED:
        res = _adamw(_as2d(a[k]), _as2d(gred[k]), _as2d(a["m_" + k]), _as2d(a["v_" + k]), name="adamw_" + k)
        delta[k], new_m[k], new_v[k] = (r.reshape(a[k].shape) for r in res)
    for names, shapes, g_packed, width, mult, tag in ((_SMALL_SHARDED, small_shapes, g_small, PACK_WIDTH, 32, "small"),
                                                      (_REPLICATED, rep_shapes, g_rep, LANES_V7X, 8, "replicated")):
        packs = [_pack([a[pre + k] for k in names], 0, width, mult) for pre in ("", "m_", "v_")]
        res = _adamw(packs[0], g_packed, packs[1], packs[2], name="adamw_" + tag)
        for dst, r in zip((gred, delta, new_m, new_v), [g_packed] + list(res)):
            dst.update(zip(names, _unpack(r, shapes)))
    return (loss, dx[None], *[gred[k] for k in _WEIGHTS], *[delta[k] for k in _WEIGHTS],
            *[new_m[k] for k in _WEIGHTS], *[new_v[k] for k in _WEIGHTS])
```

```python
import functools

import jax
import jax.numpy as jnp
from jax import lax
from jax.experimental import pallas as pl
from jax.experimental.pallas import tpu as pltpu

F32 = jnp.float32
BF16 = jnp.bfloat16
HIGHEST = lax.Precision.HIGHEST
MESH = pl.DeviceIdType.MESH

RMS_EPS = 1e-6
LNX_EPS = 64e-5
LRU_C = 8.0
ADAM_LR = 0.001
ADAM_B1 = 0.9
ADAM_B2 = 0.999
ADAM_EPS = 1e-08
ADAM_WD = 0.01
ADAM_STEP = 10

LANES_V7X = 128
VMEM_LIMIT_V7X = 60 * 1024 * 1024
WKV_CHUNK = 16
N_XY = 4
N_DEV = 8


def _cparams(sem=None, **kw):
    if sem is not None:
        kw["dimension_semantics"] = sem
    return pltpu.CompilerParams(vmem_limit_bytes=VMEM_LIMIT_V7X, **kw)


def _tile(dim, prefs):
    for t in prefs:
        if dim % t == 0:
            return t
    return dim


def _round_up(n, m):
    return (n + m - 1) // m * m


MM_MAX_TK = 2816


def _tile_k(kdim):
    best = None
    for t in range(LANES_V7X, min(kdim, MM_MAX_TK) + 1, LANES_V7X):
        if kdim % t == 0:
            best = t
    return best or kdim


def _mm(a, b, *, ta=False, tb=False, res=None, out_dtype=F32, name, gather=()):
    if ta:
        kdim, m = a.shape
    else:
        m, kdim = a.shape
    n = b.shape[0] if tb else b.shape[1]
    assert (b.shape[1] if tb else b.shape[0]) == kdim
    tk = _tile_k(kdim)
    tm = _tile(m, (2048, 1024, 512, 256, 128) if tk <= 2048 else (1024, 512, 256, 128))
    tn = _tile(n, (512, 256, 128))
    nk = kdim // tk
    ni, nj = m // tm, n // tn
    a_spec = pl.BlockSpec((tk, tm), lambda i, j, k: (k, i)) if ta else pl.BlockSpec((tm, tk), lambda i, j, k: (i, k))
    b_spec = pl.BlockSpec((tn, tk), lambda i, j, k: (j, k)) if tb else pl.BlockSpec((tk, tn), lambda i, j, k: (k, j))
    o_spec = pl.BlockSpec((tm, tn), lambda i, j, k: (i, j))
    dn = (((0 if ta else 1,), (1 if tb else 0,)), ((), ()))
    has_res = res is not None
    ng = len(gather)
    nin = 2 + has_res

    def body(*refs):
        a_ref, b_ref = refs[:2]
        r_ref = refs[2] if has_res else None
        g_in, o_ref, g_out = refs[nin:nin + ng], refs[nin + ng], refs[nin + ng + 1:nin + 2 * ng + 1]
        scratch = refs[nin + 2 * ng + 1:]
        acc_ref = scratch[0] if nk > 1 else None
        g_sems = scratch[1 if nk > 1 else 0:]
        i, j, k = pl.program_id(0), pl.program_id(1), pl.program_id(2)

        if ng:
            @pl.when((i == 0) & (j == 0) & (k == 0))
            def _():
                for q in range(ng):
                    _gather_start(g_in[q], g_out[q], *g_sems[2 * q:2 * q + 2])

        def finish(acc):
            if has_res:
                acc = acc + r_ref[...].astype(F32)
            o_ref[...] = acc.astype(out_dtype)

        prod = lax.dot_general(a_ref[...], b_ref[...], dn, preferred_element_type=F32)
        if nk == 1:
            finish(prod)
        else:
            @pl.when(k == 0)
            def _():
                acc_ref[...] = prod

            @pl.when(k > 0)
            def _():
                acc_ref[...] += prod

            @pl.when(k == nk - 1)
            def _():
                finish(acc_ref[...])

        if ng:
            @pl.when((i == ni - 1) & (j == nj - 1) & (k == nk - 1))
            def _():
                for q in range(ng):
                    _gather_finish(g_in[q], g_out[q], *g_sems[2 * q:2 * q + 2])

    ins = [a, b] + ([res] if has_res else []) + list(gather)
    in_specs = [a_spec, b_spec] + ([o_spec] if has_res else []) + [_ANY] * ng
    scratch = ([pltpu.VMEM((tm, tn), F32)] if nk > 1 else []) + _gather_scratch(ng)
    sem = ("arbitrary",) * 3 if ng else ("parallel", "parallel", "arbitrary")
    out = pl.pallas_call(
        body, name=name, grid=(ni, nj, nk), in_specs=in_specs, out_specs=[o_spec] + [_ANY] * ng,
        out_shape=[jax.ShapeDtypeStruct((m, n), out_dtype)] + [_gathered_shape(g) for g in gather],
        scratch_shapes=scratch, compiler_params=_cparams(sem),
    )(*ins)
    return (out[0], list(out[1:])) if ng else out[0]


def _stage_specs(axis, tile, tiled, params, consts, rows):
    specs = []
    for arr, width, cblk in tiled:
        if axis == 0:
            specs.append(pl.BlockSpec((tile, width), functools.partial(lambda i, c: (i, c), c=cblk)))
        else:
            specs.append(pl.BlockSpec((rows, tile), functools.partial(lambda i, c: (0, i + c), c=cblk)))
    for arr, cblk in params:
        if axis == 0:
            specs.append(pl.BlockSpec(arr.shape, functools.partial(lambda i, nd: (0,) * nd, nd=arr.ndim)))
        else:
            specs.append(pl.BlockSpec((arr.shape[0], tile), functools.partial(lambda i, c: (0, i + c), c=cblk)))
    for arr in consts:
        specs.append(pl.BlockSpec(arr.shape, functools.partial(lambda i, nd: (0,) * nd, nd=arr.ndim)))
    return specs


def _stage_fwd(fn, tiled, params, consts, outs, *, axis, tile, rows, name):
    nt, npar, nc = len(tiled), len(params), len(consts)
    ntiles = (rows // tile) if axis == 0 else (outs[0][0] // tile)

    def body(*refs):
        ins = refs[: nt + npar + nc]
        orefs = refs[nt + npar + nc:]
        vals = [r[...].astype(F32) for r in ins[: nt + npar]] + [r[...] for r in ins[nt + npar:]]
        ctx = pl.program_id(0) * tile
        res = fn(ctx, *vals)
        for o_ref, o in zip(orefs, res):
            o_ref[...] = o.astype(o_ref.dtype)

    if axis == 0:
        out_specs = [pl.BlockSpec((tile, w), lambda i: (i, 0)) for w, _ in outs]
    else:
        out_specs = [pl.BlockSpec((rows, tile), lambda i: (0, i)) for w, _ in outs]
    res = pl.pallas_call(
        body, name=name, grid=(ntiles,),
        in_specs=_stage_specs(axis, tile, tiled, params, consts, rows), out_specs=out_specs,
        out_shape=[jax.ShapeDtypeStruct((rows, w), dt) for w, dt in outs],
        compiler_params=_cparams(("arbitrary",)),
    )(*[t[0] for t in tiled], *[p[0] for p in params], *consts)
    return res


def _stage_bwd(fn, tiled, params, consts, cots, dtiled, *, axis, tile, rows, name, ncols=None):
    nt, npar, nc, nco = len(tiled), len(params), len(consts), len(cots)
    ntiles = (rows // tile) if axis == 0 else (ncols // tile)
    didx = [d[0] for d in dtiled]

    def body(*refs):
        ins = refs[: nt + npar + nc]
        crefs = refs[nt + npar + nc: nt + npar + nc + nco]
        orefs = refs[nt + npar + nc + nco:]
        vals = [r[...].astype(F32) for r in ins[: nt + npar]] + [r[...] for r in ins[nt + npar:]]
        ctx = pl.program_id(0) * tile

        def g(*dv):
            full = list(vals)
            for j, ix in enumerate(didx):
                full[ix] = dv[j]
            for j in range(npar):
                full[nt + j] = dv[len(didx) + j]
            return tuple(fn(ctx, *full))

        prim = [vals[ix] for ix in didx] + [vals[nt + j] for j in range(npar)]
        _, vjp = jax.vjp(g, *prim)
        grads = vjp(tuple(c[...].astype(F32) for c in crefs))
        for j in range(len(didx)):
            orefs[j][...] = grads[j].astype(orefs[j].dtype)
        for j in range(npar):
            o_ref = orefs[len(didx) + j]
            gp = grads[len(didx) + j]
            if axis == 0:
                @pl.when(pl.program_id(0) == 0)
                def _(o_ref=o_ref):
                    o_ref[...] = jnp.zeros_like(o_ref)

                o_ref[...] += gp
            else:
                o_ref[...] = gp

    if axis == 0:
        cot_specs = [pl.BlockSpec((tile, w), functools.partial(lambda i, c: (i, c), c=cb)) for _, w, cb in cots]
        out_specs = [pl.BlockSpec((tile, w), lambda i: (i, 0)) for _, w, _ in dtiled]
        out_specs += [pl.BlockSpec(p.shape, functools.partial(lambda i, nd: (0,) * nd, nd=p.ndim)) for p, _ in params]
        out_shape = [jax.ShapeDtypeStruct((rows, w), dt) for _, w, dt in dtiled]
        out_shape += [jax.ShapeDtypeStruct(p.shape, F32) for p, _ in params]
    else:
        cot_specs = [pl.BlockSpec((rows, tile), functools.partial(lambda i, c: (0, i + c), c=cb)) for _, cb in cots]
        out_specs = [pl.BlockSpec((rows, tile), lambda i: (0, i)) for _ in dtiled]
        out_specs += [pl.BlockSpec((p.shape[0], tile), lambda i: (0, i)) for p, _ in params]
        out_shape = [jax.ShapeDtypeStruct((rows, w), dt) for _, w, dt in dtiled]
        out_shape += [jax.ShapeDtypeStruct((p.shape[0], ncols), F32) for p, _ in params]
    return pl.pallas_call(
        body, name=name, grid=(ntiles,),
        in_specs=_stage_specs(axis, tile, tiled, params, consts, rows) + cot_specs, out_specs=out_specs,
        out_shape=out_shape, compiler_params=_cparams(("arbitrary",)),
    )(*[t[0] for t in tiled], *[p[0] for p in params], *consts, *[c[0] for c in cots])


def _rms(x, g):
    return x * lax.rsqrt(jnp.mean(x * x, axis=-1, keepdims=True) + RMS_EPS) * g


def _row_mask(x, k, first):
    t = lax.broadcasted_iota(jnp.int32, x.shape, 0)
    keep = (t >= k) if first else (t < x.shape[0] - k)
    return jnp.where(keep, x, 0.0)


@functools.partial(jax.custom_vjp, nondiff_argnums=(1,))
def _shift_down(x, k):
    return _row_mask(pltpu.roll(x, k, 0), k, True)


def _shift_down_fwd(x, k):
    return _shift_down(x, k), None


def _shift_down_bwd(k, _, g):
    return (_row_mask(pltpu.roll(g, g.shape[0] - k, 0), k, False),)


_shift_down.defvjp(_shift_down_fwd, _shift_down_bwd)


def _dwconv(x, w, b):
    kw = w.shape[0]
    out = x * w[kw - 1:kw] + b
    for j in range(kw - 1):
        out = out + _shift_down(x, kw - 1 - j) * w[j:j + 1]
    return out


def _f_norm(ctx, x, g):
    return (_rms(x, g),)


def _f_norm_res(ctx, x, g):
    return (_rms(x, g), x)


def _f_shiftmix(ctx, z, mu):
    return (z + (_shift_down(z, 1) - z) * mu,)


def _f_conv(ctx, x, w, b):
    return (_dwconv(x, w, b),)


def _f_ffn_act(ctx, gpre, up, w, b):
    return (jax.nn.gelu(_dwconv(gpre, w, b)) * up,)


def _make_f_lru_gates(heads):
    def fn(ctx, xb, wx, wa, bx, ba, lam):
        blk = xb.shape[1] // heads
        px, pa = [], []
        for h in range(heads):
            xh = xb[:, h * blk:(h + 1) * blk]
            px.append(jnp.dot(xh, wx[h], preferred_element_type=F32))
            pa.append(jnp.dot(xh, wa[h], preferred_element_type=F32))
        px = px[0] if heads == 1 else jnp.concatenate(px, axis=1)
        pa = pa[0] if heads == 1 else jnp.concatenate(pa, axis=1)
        gate_x = jax.nn.sigmoid(px + bx)
        gate_a = jax.nn.sigmoid(pa + ba)
        log_a = -LRU_C * gate_a * jax.nn.softplus(-lam)
        a = jnp.exp(log_a)
        mult = jnp.sqrt(1.0 - jnp.exp(2.0 * log_a))
        t = ctx + lax.broadcasted_iota(jnp.int32, xb.shape, 0)
        mult = jnp.where(t == 0, 1.0, mult)
        return a, xb * gate_x * mult

    return fn


def _f_lru_out(ctx, hl, ya, g):
    return (_rms(hl * jax.nn.gelu(ya), g),)


def _headsum_3pass(x, bb):
    hi = x.astype(BF16)
    r1 = x - hi.astype(F32)
    mid = r1.astype(BF16)
    lo = (r1 - mid.astype(F32)).astype(BF16)
    return (jnp.dot(hi, bb, preferred_element_type=F32) + jnp.dot(mid, bb, preferred_element_type=F32)
            + jnp.dot(lo, bb, preferred_element_type=F32))


@jax.custom_vjp
def _headsum(x, bb):
    return _headsum_3pass(x, bb)


def _headsum_fwd(x, bb):
    return _headsum_3pass(x, bb), bb


def _headsum_bwd(bb, g):
    return _headsum_3pass(g, bb), None


_headsum.defvjp(_headsum_fwd, _headsum_bwd)


def _make_f_rwkv_pre(has_vres, v_uses=0):
    def fn(ctx, *args):
        if v_uses:
            r, args = args[0], args[1:]
        if has_vres:
            k, v, lz, vf, w0, w2, a0, a2, g2, kkw, ka, v0, v2, bb = args
        else:
            k, v, lz, w0, w2, a0, a2, g2, kkw, ka, bb = args
        w_log = -jax.nn.softplus(-(w0 + jnp.dot(jnp.tanh(lz), w2, preferred_element_type=F32))) - 0.5
        logw = -jnp.exp(w_log)
        a = jax.nn.sigmoid(a0 + jnp.dot(lz, a2, preferred_element_type=F32))
        g = jnp.dot(jax.nn.sigmoid(lz), g2, preferred_element_type=F32)
        if has_vres:
            v = v + (vf - v) * jax.nn.sigmoid(v0 + jnp.dot(lz, v2, preferred_element_type=F32))
        xk = k * kkw
        kk = xk / jnp.maximum(jnp.sqrt(_headsum(xk * xk, bb)), 1e-12)
        k2 = k * (1.0 + (a - 1.0) * ka)
        if v_uses:
            return (r, r, logw, k2, k2) + (v,) * v_uses + (kk, kk * a, g)
        return logw, k2, v, kk, kk * a, g

    return fn


def _make_f_rwkv_post(head_size):
    def fn(ctx, y, r, k2, v2, g, lnw, lnb, rk, bb):
        mean = _headsum(y, bb) / head_size
        d = y - mean
        var = _headsum(d * d, bb) / head_size
        yn = d * lax.rsqrt(var + LNX_EPS) * lnw + lnb
        bonus = _headsum(r * k2 * rk, bb) * v2
        return ((yn + bonus) * g,)

    return fn


def _f_ple(ctx, h, eg, ep, g):
    return (h + _rms(jax.nn.sigmoid(eg) * ep, g),)


def _lru_scan(a, b, *, name):
    rows, cols = a.shape
    tc = _tile(cols, (512, 256, 128))

    def body(a_ref, b_ref, h_ref):
        def step(t, carry):
            h = a_ref[pl.ds(t, 1), :] * carry + b_ref[pl.ds(t, 1), :]
            h_ref[pl.ds(t, 1), :] = h
            return h

        lax.fori_loop(0, rows, step, jnp.zeros((1, tc), F32), unroll=8)

    spec = pl.BlockSpec((rows, tc), lambda j: (0, j))
    return pl.pallas_call(body, name=name, grid=(cols // tc,), in_specs=[spec, spec], out_specs=spec,
                          out_shape=jax.ShapeDtypeStruct((rows, cols), F32), compiler_params=_cparams(("arbitrary",)))(a, b)


def _lru_scan_bwd(a, h, dh, *, name):
    rows, cols = a.shape
    tc = _tile(cols, (512, 256, 128))

    def body(a_ref, h_ref, dh_ref, da_ref, db_ref):
        def step(i, carry):
            t = rows - 1 - i
            g = dh_ref[pl.ds(t, 1), :] + carry
            db_ref[pl.ds(t, 1), :] = g
            hp = h_ref[pl.ds(jnp.maximum(t - 1, 0), 1), :]
            da_ref[pl.ds(t, 1), :] = jnp.where(t > 0, g * hp, 0.0)
            return a_ref[pl.ds(t, 1), :] * g

        lax.fori_loop(0, rows, step, jnp.zeros((1, tc), F32), unroll=8)

    spec = pl.BlockSpec((rows, tc), lambda j: (0, j))
    return pl.pallas_call(body, name=name, grid=(cols // tc,), in_specs=[spec] * 3, out_specs=[spec] * 2,
                          out_shape=[jax.ShapeDtypeStruct((rows, cols), F32)] * 2,
                          compiler_params=_cparams(("arbitrary",)))(a, h, dh)


def _split_bf16(x):
    hi = x.astype(BF16)
    return hi, (x - hi.astype(F32)).astype(BF16)


def _dot3_passes(a, b, ca, cb):
    dn = (((ca,), (cb,)), ((), ()))
    ah, al = _split_bf16(a)
    bh, bl = _split_bf16(b)
    return (lax.dot_general(ah, bh, dn, preferred_element_type=F32) + lax.dot_general(al, bh, dn, preferred_element_type=F32)
            + lax.dot_general(ah, bl, dn, preferred_element_type=F32))


@functools.partial(jax.custom_vjp, nondiff_argnums=(2, 3))
def _dot3(a, b, ca, cb):
    return _dot3_passes(a, b, ca, cb)


def _dot3_fwd(a, b, ca, cb):
    return _dot3_passes(a, b, ca, cb), (a, b)


def _dot3_bwd(ca, cb, res, g):
    a, b = res
    fa, fb = 1 - ca, 1 - cb
    da = _dot3_passes(g, b, 1, fb) if ca == 1 else _dot3_passes(b, g, fb, 1)
    db = _dot3_passes(a, g, fa, 0) if cb == 0 else _dot3_passes(g, a, 0, fa)
    return da, db


_dot3.defvjp(_dot3_fwd, _dot3_bwd)


def _each(f, *lists):
    return [f(*t) for t in zip(*lists)]


def _wkv_local(r, lw, k, v, kk, b):
    c, n = r[0].shape
    row = lax.broadcasted_iota(jnp.int32, (c, c), 0)
    col = lax.broadcasted_iota(jnp.int32, (c, c), 1)
    incl = (row >= col).astype(F32)
    strict = (row > col).astype(F32)
    eye = lax.broadcasted_iota(jnp.int32, (n, n), 0) == lax.broadcasted_iota(jnp.int32, (n, n), 1)
    cl = _each(lambda x: _dot3(incl, x, 1, 0), lw)
    w_t = _each(jnp.exp, cl)
    inv_w = _each(lambda x: jnp.exp(-x), cl)
    kk_s = _each(lambda x, y, z: x * jnp.exp(y - z), kk, cl, lw)
    b_s = _each(jnp.multiply, b, inv_w)
    k_s = _each(jnp.multiply, k, inv_w)
    r_s = _each(jnp.multiply, r, w_t)
    q = _each(lambda x, y: jnp.concatenate([x, y], axis=0), kk_s, r_s)
    qb = _each(lambda x, y: _dot3(x, y, 1, 1), q, b_s)
    qk = _each(lambda x, y: _dot3(x, y, 1, 1), q, k_s)
    m = _each(lambda x: -strict * x[:c], qb)
    pb = _each(lambda x: incl * x[c:], qb)
    lkv = _each(lambda x, y: _dot3(strict * x[:c], y, 1, 0), qk, v)
    pkv = _each(lambda x, y: _dot3(incl * x[c:], y, 1, 0), qk, v)
    a = _each(lambda x, y: jnp.concatenate([x, y], axis=1), kk_s, lkv)
    steps = max(1, (c - 1).bit_length())
    for i in range(steps):
        a = _each(lambda x, y: y + _dot3(x, y, 1, 0), m, a)
        if i + 1 < steps:
            m = _each(lambda x: _dot3(x, x, 1, 0), m)
    ry = _each(lambda x, y, z, w: jnp.concatenate([x, y], axis=1) - _dot3(z, w, 1, 0), r_s, pkv, pb, a)
    w_end = _each(lambda x: x[c - 1:c, :], w_t)
    gu_low = _each(lambda x, y, z: _dot3(x, y * z, 0, 0), a, b_s, w_end)
    g = _each(lambda x, y: jnp.where(eye, jnp.broadcast_to(x, (n, n)), 0.0) - y[:n], w_end, gu_low)
    u = _each(lambda x, y, z, w: _dot3(x, y * z, 0, 0) - w[n:], v, k_s, w_end, gu_low)
    return g, u, _each(lambda x: x[:, :n], ry), _each(lambda x: x[:, n:], ry)


def _wkv_blocks(h, nchunk):
    return (_tile(h, (4, 2, 1)), _tile(nchunk, (4, 2, 1))), (h, _tile(nchunk, (4, 2, 1)))


def _wkv_fwd(r, lw, k, v, kk, b, *, name, gather=(), gather_state=()):
    h, t, n = r.shape
    c = WKV_CHUNK
    nchunk = t // c
    (hb, cb), (hs, cs) = _wkv_blocks(h, nchunk)
    ng = len(gather)
    ni, nj = h // hb, nchunk // cb

    pairs = [(i, j) for i in range(hb) for j in range(cb)]

    def local_body(*refs):
        ins, g_in = refs[:6], refs[6:6 + ng]
        g_ref, u_ref, r2_ref, y0_ref = refs[6 + ng:10 + ng]
        g_out, g_sems = refs[10 + ng:10 + 2 * ng], refs[10 + 2 * ng:]
        if ng:
            @pl.when((pl.program_id(0) == 0) & (pl.program_id(1) == 0))
            def _():
                for q in range(ng):
                    _gather_start(g_in[q], g_out[q], *g_sems[2 * q:2 * q + 2])

        g, u, r2, y0 = _wkv_local(*[[ref[i, pl.ds(j * c, c)] for i, j in pairs] for ref in ins])
        for idx, (i, j) in enumerate(pairs):
            g_ref[i, j] = g[idx]
            u_ref[i, j] = u[idx]
            r2_ref[i, pl.ds(j * c, c)] = r2[idx]
            y0_ref[i, pl.ds(j * c, c)] = y0[idx]
        if ng:
            @pl.when((pl.program_id(0) == ni - 1) & (pl.program_id(1) == nj - 1))
            def _():
                for q in range(ng):
                    _gather_finish(g_in[q], g_out[q], *g_sems[2 * q:2 * q + 2])

    seq = pl.BlockSpec((hb, cb * c, n), lambda i, j: (i, j, 0))
    mat = pl.BlockSpec((hb, cb, n, n), lambda i, j: (i, j, 0, 0))
    res = pl.pallas_call(
        local_body, name=name + "_local", grid=(ni, nj), in_specs=[seq] * 6 + [_ANY] * ng,
        out_specs=[mat, mat, seq, seq] + [_ANY] * ng,
        out_shape=[jax.ShapeDtypeStruct((h, nchunk, n, n), F32)] * 2 + [jax.ShapeDtypeStruct((h, t, n), F32)] * 2
        + [_gathered_shape(g) for g in gather],
        scratch_shapes=_gather_scratch(ng),
        compiler_params=_cparams(("arbitrary", "arbitrary") if ng else ("parallel", "parallel")),
    )(r, lw, k, v, kk, b, *gather)
    gm, um, r2, y0 = res[:4]
    gathered = list(res[4:])

    ng2 = len(gather_state)
    nsteps = nchunk // cs

    def state_body(*refs):
        g_ref, u_ref, r2_ref, y0_ref = refs[:4]
        g_in, (y_ref, st_ref) = refs[4:4 + ng2], refs[4 + ng2:6 + ng2]
        g_out, s_ref, g_sems = refs[6 + ng2:6 + 2 * ng2], refs[6 + 2 * ng2], refs[7 + 2 * ng2:]

        @pl.when(pl.program_id(0) == 0)
        def _():
            s_ref[...] = jnp.zeros_like(s_ref)
            for q in range(ng2):
                _gather_start(g_in[q], g_out[q], *g_sems[2 * q:2 * q + 2])

        s = [s_ref[i] for i in range(hs)]
        for j in range(cs):
            rows = pl.ds(j * c, c)
            for i in range(hs):
                st_ref[i, j] = s[i]
                y_ref[i, rows] = _dot3(r2_ref[i, rows], s[i], 1, 1) + y0_ref[i, rows]
            s = [_dot3(s[i], g_ref[i, j], 1, 0) + u_ref[i, j] for i in range(hs)]
        for i in range(hs):
            s_ref[i] = s[i]
        if ng2:
            @pl.when(pl.program_id(0) == nsteps - 1)
            def _():
                for q in range(ng2):
                    _gather_finish(g_in[q], g_out[q], *g_sems[2 * q:2 * q + 2])

    seq = pl.BlockSpec((hs, cs * c, n), lambda j: (0, j, 0))
    mat = pl.BlockSpec((hs, cs, n, n), lambda j: (0, j, 0, 0))
    res = pl.pallas_call(
        state_body, name=name + "_state", grid=(nsteps,), in_specs=[mat, mat, seq, seq] + [_ANY] * ng2,
        out_specs=[seq, mat] + [_ANY] * ng2,
        out_shape=[jax.ShapeDtypeStruct((h, t, n), F32), jax.ShapeDtypeStruct((h, nchunk, n, n), F32)]
        + [_gathered_shape(g) for g in gather_state],
        scratch_shapes=[pltpu.VMEM((hs, n, n), F32)] + _gather_scratch(ng2), compiler_params=_cparams(("arbitrary",)),
    )(gm, um, r2, y0, *gather_state)
    return res[0], (res[1], gm, r2), gathered + list(res[2:])


def _wkv_bwd(r, lw, k, v, kk, b, saved, dy, *, name):
    states, gm, r2 = saved
    h, t, n = r.shape
    c = WKV_CHUNK
    nchunk = t // c
    (hb, _), (hs, cs) = _wkv_blocks(h, nchunk)
    cb = _tile(nchunk, (8, 4, 2, 1))
    nsteps = nchunk // cs

    def state_body(g_ref, r2_ref, st_ref, dy_ref, dg_ref, du_ref, dr2_ref, ds_ref):
        @pl.when(pl.program_id(0) == 0)
        def _():
            ds_ref[...] = jnp.zeros_like(ds_ref)

        ds = [ds_ref[i] for i in range(hs)]
        for j in reversed(range(cs)):
            rows = pl.ds(j * c, c)
            for i in range(hs):
                s0 = st_ref[i, j]
                du_ref[i, j] = ds[i]
                dg_ref[i, j] = _dot3(s0, ds[i], 0, 0)
                dr2_ref[i, rows] = _dot3(dy_ref[i, rows], s0, 1, 0)
            ds = [_dot3(dy_ref[i, rows], r2_ref[i, rows], 0, 0) + _dot3(ds[i], g_ref[i, j], 1, 1) for i in range(hs)]
        for i in range(hs):
            ds_ref[i] = ds[i]

    seq = pl.BlockSpec((hs, cs * c, n), lambda j: (0, nsteps - 1 - j, 0))
    mat = pl.BlockSpec((hs, cs, n, n), lambda j: (0, nsteps - 1 - j, 0, 0))
    dg, du, dr2 = pl.pallas_call(
        state_body, name=name + "_state", grid=(nsteps,), in_specs=[mat, seq, mat, seq], out_specs=[mat, mat, seq],
        out_shape=[jax.ShapeDtypeStruct((h, nchunk, n, n), F32)] * 2 + [jax.ShapeDtypeStruct((h, t, n), F32)],
        scratch_shapes=[pltpu.VMEM((hs, n, n), F32)], compiler_params=_cparams(("arbitrary",)),
    )(gm, r2, states, dy)

    pairs = [(i, j) for i in range(hb) for j in range(cb)]

    def local_body(*refs):
        ins, (dg_ref, du_ref, dr2_ref, dy_ref), out_refs = refs[:6], refs[6:10], refs[10:]
        _, vjp = jax.vjp(_wkv_local, *[[ref[i, pl.ds(j * c, c)] for i, j in pairs] for ref in ins])
        grads = vjp(([dg_ref[i, j] for i, j in pairs], [du_ref[i, j] for i, j in pairs],
                     [dr2_ref[i, pl.ds(j * c, c)] for i, j in pairs], [dy_ref[i, pl.ds(j * c, c)] for i, j in pairs]))
        for o_ref, gr in zip(out_refs, grads):
            for idx, (i, j) in enumerate(pairs):
                o_ref[i, pl.ds(j * c, c)] = gr[idx]

    seq = pl.BlockSpec((hb, cb * c, n), lambda i, j: (i, j, 0))
    mat = pl.BlockSpec((hb, cb, n, n), lambda i, j: (i, j, 0, 0))
    return pl.pallas_call(
        local_body, name=name + "_local", grid=(h // hb, nchunk // cb), in_specs=[seq] * 6 + [mat, mat, seq, seq],
        out_specs=[seq] * 6, out_shape=[jax.ShapeDtypeStruct((h, t, n), F32)] * 6,
        compiler_params=_cparams(("parallel", "parallel")),
    )(r, lw, k, v, kk, b, dg, du, dr2, dy)


class _Dims:
    pass


def _make_dims(x, p, w):
    m = _Dims()
    m.t, m.d = x.shape[-2], x.shape[-1]
    m.nl = w["ln_mix"].shape[0]
    m.dl = w["conv_a_b"].shape[1]
    m.hl = w["lru_wx"].shape[1]
    m.dr = w["rwkv_w0"].shape[1]
    m.h, m.n = w["rwkv_rk"].shape[1], w["rwkv_rk"].shape[2]
    m.lw, m.la, m.lg, m.lv = (w[k].shape[1] for k in ("rwkv_w2", "rwkv_a2", "rwkv_g2", "rwkv_v2"))
    m.nsh = w["mu_shift"].shape[1]
    m.ff = w["conv_f_b"].shape[1]
    m.ple = p.shape[-1]
    m.din = 2 * m.dl + m.nsh
    m.lz = _round_up(m.lw + m.la + m.lg + m.lv, LANES_V7X)
    m.zw = _round_up(2 * m.dl + 3 * m.dr + m.lz, 512)
    m.zs = m.zw - 2 * m.dl
    m.tr = _tile(m.t, (256, 128, 64, 32, 16, 8))
    m.trb = _tile(m.t, (128, 64, 32, 16, 8))
    m.tcs = _tile(m.zs, (512, 256, 128))
    assert (3 * m.dr) % m.lz == 0 and (2 * m.dl) % m.tcs == 0 and m.t % WKV_CHUNK == 0
    assert m.nsh == 3 * m.dr + m.lw + m.la + m.lg
    return m


def _to_heads(m, a):
    return jnp.transpose(a.reshape(m.t, m.h, m.n), (1, 0, 2))


def _from_heads(m, a):
    return jnp.transpose(a, (1, 0, 2)).reshape(m.t, m.dr)


def _norm_fwd(m, h, g, name):
    return _stage_fwd(_f_norm, [(h, m.d, 0)], [(g, 0)], [], [(m.d, BF16)], axis=0, tile=m.tr, rows=m.t, name=name)[0]


def _norm_bwd(m, h, g, du, dres, name):
    return _stage_bwd(_f_norm_res, [(h, m.d, 0)], [(g, 0)], [], [(du, m.d, 0), (dres, m.d, 0)], [(0, m.d, F32)],
                      axis=0, tile=m.tr, rows=m.t, name=name)


def _rwkv_pre_operands(m, w, i, sv, v_first_zs, with_r):
    zs = sv["zs"]
    tiled = ([(zs, m.dr, 0)] if with_r else []) + [(zs, m.dr, 1), (zs, m.dr, 2), (zs, m.lz, 3 * m.dr // m.lz)]
    params = [(w["rwkv_w0"][i:i + 1], 0), (w["w2p"][i], 0), (w["rwkv_a0"][i:i + 1], 0), (w["a2p"][i], 0),
              (w["g2p"][i], 0), (w["rwkv_kk"][i:i + 1], 0), (w["rwkv_ka"][i:i + 1], 0)]
    if i > 0:
        tiled.append((v_first_zs, m.dr, 2))
        params += [(w["rwkv_v0"][i - 1:i], 0), (w["v2p"][i - 1], 0)]
    return tiled, params


def _rwkv_post_operands(m, w, i, sv):
    tiled = [(sv["y"], m.dr, 0), (sv["zs"], m.dr, 0), (sv["k2"], m.dr, 0), (sv["v2"], m.dr, 0), (sv["g"], m.dr, 0)]
    params = [(w["rwkv_lnx_w"][i:i + 1], 0), (w["rwkv_lnx_b"][i:i + 1], 0), (w["rk"][i], 0)]
    return tiled, params


def _lru_gate_params(w, i):
    return [(w["lru_wx"][i], 0), (w["lru_wa"][i], 0), (w["lru_bx"][i:i + 1], 0), (w["lru_ba"][i:i + 1], 0),
            (w["lru_lambda"][i:i + 1], 0)]


class _WeightFeed:
    def __init__(self, m, w, shards, vres):
        self.m, self.w, self.shards, self.vres = m, w, shards, vres

    def keys(self, carrier, i):
        plan = {"mm_in": [("w_o", i)] if i == 0 else [],
                "wkv_local": [("w_gate", i), ("w_up", i)], "wkv_state": [("w_down", i)],
                "mm_gate": [("w_ple_gate", i)], "mm_up": [("w_ple_proj", i), ("w_o", i + 1)],
                "mm_down": [("w_in", i + 1)], "mm_pgate": []}
        return [key for key in plan[carrier] if key[1] < self.m.nl]

    def blobs(self, keys):
        return [_gather_blob(self.shards[name][layer]) for name, layer in keys]

    def arrive(self, keys, gathered):
        m = self.m
        for (name, layer), got in zip(keys, gathered):
            full = got.reshape((N_XY,) + self.shards[name][layer].shape)
            full = _from_shards(full, _SHARD_AXIS[name] - 1)
            if name == "w_in":
                vres = self.vres[layer - 1] if layer > 0 else jnp.zeros((m.d, m.lv), BF16)
                self.w["wcat"][layer] = jnp.concatenate([full, vres, jnp.zeros((m.d, m.zw - m.din - m.lv), BF16)], axis=1)
            else:
                self.w[name][layer] = full


def _mm_fed(feed, carrier, i, a, b, **kw):
    keys = feed.keys(carrier, i) if feed is not None else []
    if not keys:
        return _mm(a, b, **kw)
    out, got = _mm(a, b, gather=feed.blobs(keys), **kw)
    feed.arrive(keys, got)
    return out


def _layer_fwd(m, w, i, h, p_bf, v_first_zs, feed=None):
    sv = {"h": h}
    t, dl, dr = m.t, m.dl, m.dr
    sv["u1"] = _norm_fwd(m, h, w["ln_mix"][i:i + 1], "norm_mix")
    z = sv["z"] = _mm_fed(feed, "mm_in", i, sv["u1"], w["wcat"][i], name="mm_in")
    off = 2 * dl // m.tcs
    sv["zs"] = _stage_fwd(_f_shiftmix, [(z, None, off)], [(w["mu_pad"][i], off)], [], [(m.zs, F32)],
                          axis=1, tile=m.tcs, rows=t, name="shiftmix")[0]
    tca = _tile(dl, (512, 256, 128))
    sv["xb"] = _stage_fwd(_f_conv, [(z, None, 0)], [(w["conv_a_w"][i], 0), (w["conv_a_b"][i:i + 1], 0)], [],
                          [(dl, F32)], axis=1, tile=tca, rows=t, name="conv_a")[0]
    sv["a"], b_in = _stage_fwd(_make_f_lru_gates(m.hl), [(sv["xb"], dl, 0)], _lru_gate_params(w, i), [],
                               [(dl, F32), (dl, F32)], axis=0, tile=m.tr, rows=t, name="lru_gates")
    sv["hl"] = _lru_scan(sv["a"], b_in, name="lru_scan")
    out_a = _stage_fwd(_f_lru_out, [(sv["hl"], dl, 0), (z, dl, 1)], [(w["lru_norm"][i:i + 1], 0)], [],
                       [(dl, BF16)], axis=0, tile=m.tr, rows=t, name="lru_out")[0]
    tiled, params = _rwkv_pre_operands(m, w, i, sv, v_first_zs, False)
    pre = _stage_fwd(_make_f_rwkv_pre(i > 0), tiled, params, [w["bb"]], [(dr, F32)] * 6,
                     axis=0, tile=m.tr, rows=t, name="rwkv_pre")
    sv["logw"], sv["k2"], sv["v2"], sv["kk"], sv["b"], sv["g"] = pre
    heads = [_to_heads(m, a) for a in (sv["zs"][:, :dr], sv["logw"], sv["k2"], sv["v2"], sv["kk"], sv["b"])]
    keys = [feed.keys(carrier, i) if feed is not None else [] for carrier in ("wkv_local", "wkv_state")]
    y_h, sv["states"], got = _wkv_fwd(*heads, name="wkv_fwd", gather=feed.blobs(keys[0]) if keys[0] else (),
                                      gather_state=feed.blobs(keys[1]) if keys[1] else ())
    if keys[0] or keys[1]:
        feed.arrive(keys[0] + keys[1], got)
    sv["y"] = _from_heads(m, y_h)
    tiled, params = _rwkv_post_operands(m, w, i, sv)
    out_b = _stage_fwd(_make_f_rwkv_post(m.n), tiled, params, [w["bb"]], [(dr, BF16)],
                       axis=0, tile=m.tr, rows=t, name="rwkv_post")[0]
    sv["cat"] = jnp.concatenate([out_a, out_b], axis=1)
    h2 = sv["h2"] = _mm(sv["cat"], w["w_o"][i], res=h, name="mm_o")
    sv["u2"] = _norm_fwd(m, h2, w["ln_ffn"][i:i + 1], "norm_ffn")
    sv["gpre"] = _mm_fed(feed, "mm_gate", i, sv["u2"], w["w_gate"][i], name="mm_gate")
    sv["up"] = _mm_fed(feed, "mm_up", i, sv["u2"], w["w_up"][i], name="mm_up")
    tcf = _tile(m.ff, (512, 256, 128))
    sv["act"] = _stage_fwd(_f_ffn_act, [(sv["gpre"], None, 0), (sv["up"], None, 0)],
                           [(w["conv_f_w"][i], 0), (w["conv_f_b"][i:i + 1], 0)], [], [(m.ff, BF16)],
                           axis=1, tile=tcf, rows=t, name="ffn_act")[0]
    h3 = sv["h3"] = _mm_fed(feed, "mm_down", i, sv["act"], w["w_down"][i], res=h2, name="mm_down")
    sv["u3"] = _norm_fwd(m, h3, w["ln_ple"][i:i + 1], "norm_ple")
    sv["eg"] = _mm_fed(feed, "mm_pgate", i, sv["u3"], w["w_ple_gate"][i], name="mm_pgate")
    sv["ep"] = _mm(p_bf, w["w_ple_proj"][i], name="mm_pproj")
    h4 = _stage_fwd(_f_ple, [(h3, m.d, 0), (sv["eg"], m.d, 0), (sv["ep"], m.d, 0)], [(w["ln_ple_post"][i:i + 1], 0)],
                    [], [(m.d, F32)], axis=0, tile=m.tr, rows=t, name="ple")[0]
    return h4, sv


def _layer_bwd(m, w, i, dh4, sv, p_bf, v_first_zs, dvf_in):
    t, d, dl, dr = m.t, m.d, m.dl, m.dr
    g = {}
    deg, dep, g["ln_ple_post"] = _stage_bwd(
        _f_ple, [(sv["h3"], d, 0), (sv["eg"], d, 0), (sv["ep"], d, 0)], [(w["ln_ple_post"][i:i + 1], 0)], [],
        [(dh4, d, 0)], [(1, d, BF16), (2, d, BF16)], axis=0, tile=m.tr, rows=t, name="ple_bwd")
    du3 = _mm(deg, w["w_ple_gate"][i], tb=True, name="mm_pgate_dx")
    g["w_ple_gate"] = _mm(sv["u3"], deg, ta=True, name="mm_pgate_dw")
    g["w_ple_proj"] = _mm(p_bf, dep, ta=True, name="mm_pproj_dw")
    dh3, g["ln_ple"] = _norm_bwd(m, sv["h3"], w["ln_ple"][i:i + 1], du3, dh4, "norm_ple_bwd")
    dh3_bf = dh3.astype(BF16)
    dact = _mm(dh3_bf, w["w_down"][i], tb=True, name="mm_down_dx")
    g["w_down"] = _mm(sv["act"], dh3_bf, ta=True, name="mm_down_dw")
    tcf = _tile(m.ff, (512, 256, 128))
    dgpre, dup, g["conv_f_w"], g["conv_f_b"] = _stage_bwd(
        _f_ffn_act, [(sv["gpre"], None, 0), (sv["up"], None, 0)], [(w["conv_f_w"][i], 0), (w["conv_f_b"][i:i + 1], 0)],
        [], [(dact, 0)], [(0, m.ff, BF16), (1, m.ff, BF16)], axis=1, tile=tcf, rows=t, ncols=m.ff, name="ffn_act_bwd")
    du2 = _mm(dgpre, w["w_gate"][i], tb=True, name="mm_gate_dx")
    du2 = _mm(dup, w["w_up"][i], tb=True, res=du2, name="mm_up_dx")
    g["w_gate"] = _mm(sv["u2"], dgpre, ta=True, name="mm_gate_dw")
    g["w_up"] = _mm(sv["u2"], dup, ta=True, name="mm_up_dw")
    dh2, g["ln_ffn"] = _norm_bwd(m, sv["h2"], w["ln_ffn"][i:i + 1], du2, dh3, "norm_ffn_bwd")
    dh2_bf = dh2.astype(BF16)
    dcat = _mm(dh2_bf, w["w_o"][i], tb=True, name="mm_o_dx")
    g["w_o"] = _mm(sv["cat"], dh2_bf, ta=True, name="mm_o_dw")
    tiled, params = _rwkv_post_operands(m, w, i, sv)
    dy, dr_a, dk2_a, dv2_a, dg, g["rwkv_lnx_w"], g["rwkv_lnx_b"], g["rk"] = _stage_bwd(
        _make_f_rwkv_post(m.n), tiled, params, [w["bb"]], [(dcat, dr, dl // dr)], [(j, dr, F32) for j in range(5)],
        axis=0, tile=m.trb, rows=t, name="rwkv_post_bwd")
    heads = [_to_heads(m, a) for a in (sv["zs"][:, :dr], sv["logw"], sv["k2"], sv["v2"], sv["kk"], sv["b"])]
    dwkv = _wkv_bwd(*heads, sv["states"], _to_heads(m, dy), name="wkv_bwd")
    dr_b, dlw, dk2_b, dv2_b, dkk, db = [_from_heads(m, a) for a in dwkv]
    tiled, params = _rwkv_pre_operands(m, w, i, sv, v_first_zs, True)
    v_cots = [dv2_a, dv2_b] + ([dvf_in] if dvf_in is not None else [])
    cots = [(c, dr, 0) for c in [dr_a, dr_b, dlw, dk2_a, dk2_b] + v_cots + [dkk, db, dg]]
    ntil = len(tiled)
    dtiled = [(0, dr, F32), (1, dr, F32), (2, dr, F32), (3, m.lz, F32)] + ([(4, dr, F32)] if i > 0 else [])
    res = _stage_bwd(_make_f_rwkv_pre(i > 0, len(v_cots)), tiled, params, [w["bb"]], cots, dtiled,
                     axis=0, tile=m.trb, rows=t, name="rwkv_pre_bwd")
    d_r, d_k, d_v, d_lz = res[:4]
    dvf_out = res[4] if i > 0 else None
    pg = res[ntil:]
    g["rwkv_w0"], g["w2p"], g["rwkv_a0"], g["a2p"], g["g2p"], g["rwkv_kk"], g["rwkv_ka"] = pg[:7]
    if i > 0:
        g["rwkv_v0"], g["v2p"] = pg[7:9]
    dzs = jnp.concatenate([d_r, d_k, d_v, d_lz, jnp.zeros((t, m.zs - 3 * dr - m.lz), F32)], axis=1)
    off = 2 * dl // m.tcs
    dzr, g["mu_pad"] = _stage_bwd(_f_shiftmix, [(sv["z"], None, off)], [(w["mu_pad"][i], off)], [], [(dzs, 0)],
                                  [(0, m.zs, BF16)], axis=1, tile=m.tcs, rows=t, ncols=m.zs, name="shiftmix_bwd")
    dhl, dya, g["lru_norm"] = _stage_bwd(
        _f_lru_out, [(sv["hl"], dl, 0), (sv["z"], dl, 1)], [(w["lru_norm"][i:i + 1], 0)], [], [(dcat, dl, 0)],
        [(0, dl, F32), (1, dl, BF16)], axis=0, tile=m.tr, rows=t, name="lru_out_bwd")
    da, db_in = _lru_scan_bwd(sv["a"], sv["hl"], dhl, name="lru_scan_bwd")
    dxb, g["lru_wx"], g["lru_wa"], g["lru_bx"], g["lru_ba"], g["lru_lambda"] = _stage_bwd(
        _make_f_lru_gates(m.hl), [(sv["xb"], dl, 0)], _lru_gate_params(w, i), [], [(da, dl, 0), (db_in, dl, 0)],
        [(0, dl, F32)], axis=0, tile=m.tr, rows=t, name="lru_gates_bwd")
    tca = _tile(dl, (512, 256, 128))
    dxa, g["conv_a_w"], g["conv_a_b"] = _stage_bwd(
        _f_conv, [(sv["z"], None, 0)], [(w["conv_a_w"][i], 0), (w["conv_a_b"][i:i + 1], 0)], [], [(dxb, 0)],
        [(0, dl, BF16)], axis=1, tile=tca, rows=t, ncols=dl, name="conv_a_bwd")
    dz = jnp.concatenate([dxa, dya, dzr], axis=1)
    du1 = _mm(dz, w["wcat"][i], tb=True, name="mm_in_dx")
    g["wcat"] = _mm(sv["u1"], dz, ta=True, name="mm_in_dw")
    dh, g["ln_mix"] = _norm_bwd(m, sv["h"], w["ln_mix"][i:i + 1], du1, dh2, "norm_mix_bwd")
    return dh, g, dvf_out


def _loss_head(m, h, g, tgt):
    tile, d = m.tr, m.d

    def body(h_ref, g_ref, t_ref, loss_ref, dh_ref, dg_ref):
        def f(hv, gv):
            err = _rms(hv, gv) - t_ref[...]
            return 0.5 * jnp.sum(jnp.mean(err * err, axis=-1))

        val, vjp = jax.vjp(f, h_ref[...], g_ref[...])
        dh, dg = vjp(jnp.ones((), F32))
        dh_ref[...] = dh

        @pl.when(pl.program_id(0) == 0)
        def _():
            dg_ref[...] = jnp.zeros_like(dg_ref)
            loss_ref[...] = jnp.zeros_like(loss_ref)

        dg_ref[...] += dg
        loss_ref[...] += jnp.full(loss_ref.shape, val, F32)

    row = pl.BlockSpec((tile, d), lambda i: (i, 0))
    return pl.pallas_call(
        body, name="loss_head", grid=(m.t // tile,),
        in_specs=[row, pl.BlockSpec((1, d), lambda i: (0, 0)), row],
        out_specs=[pl.BlockSpec((1, LANES_V7X), lambda i: (0, 0)), row, pl.BlockSpec((1, d), lambda i: (0, 0))],
        out_shape=[jax.ShapeDtypeStruct((1, LANES_V7X), F32), jax.ShapeDtypeStruct((m.t, d), F32),
                   jax.ShapeDtypeStruct((1, d), F32)],
        compiler_params=_cparams(("arbitrary",)),
    )(h, g, tgt)


def _local_step(m, w, x, p, tgt, feed=None):
    h = x
    saved = []
    p_bf = p.astype(BF16)
    for i in range(m.nl):
        h, sv = _layer_fwd(m, w, i, h, p_bf[i], saved[0]["zs"] if i > 0 else None, feed)
        saved.append(sv)
    loss_row, dh, d_ln_final = _loss_head(m, h, w["ln_final"], tgt)
    grads = [None] * m.nl
    dvf = None
    for i in reversed(range(m.nl)):
        dh, grads[i], dvf_i = _layer_bwd(m, w, i, dh, saved[i], p_bf[i], saved[0]["zs"] if i > 0 else None,
                                         dvf if i == 0 else None)
        if i > 0:
            dvf = dvf_i if dvf is None else dvf + dvf_i
    return loss_row, dh, grads, d_ln_final


_BIG = ("w_o", "w_gate", "w_up", "w_down", "w_ple_gate", "w_ple_proj")


def _lora_rows(m):
    o1 = m.lw
    o2 = o1 + m.la
    o3 = o2 + m.lg
    return {"w2p": (0, o1), "a2p": (o1, o2), "g2p": (o2, o3), "v2p": (o3, o3 + m.lv)}


def _prepare_weights(m, wf):
    w = {k: v for k, v in wf.items() if k not in _BIG and k not in ("w_in", "w_in_vres")}
    nl = m.nl
    for k in _BIG:
        w[k] = [wf[k][i].astype(BF16) for i in range(nl)] if k in wf else [None] * nl
    w["wcat"] = [None] * nl
    if "w_in" in wf:
        vres = jnp.concatenate([jnp.zeros((1, m.d, m.lv), BF16), wf["w_in_vres"].astype(BF16)], axis=0)
        pad = jnp.zeros((m.d, m.zw - m.din - m.lv), BF16)
        w["wcat"] = [jnp.concatenate([wf["w_in"][i].astype(BF16), vres[i], pad], axis=1) for i in range(nl)]
    mu_v = jnp.concatenate([jnp.zeros((1, m.lv), F32), wf["mu_shift_vres"]], axis=0)
    w["mu_pad"] = jnp.concatenate([jnp.zeros((nl, 2 * m.dl), F32), wf["mu_shift"], mu_v,
                                   jnp.zeros((nl, m.zw - m.din - m.lv), F32)], axis=1)[:, None, :]
    rows = _lora_rows(m)
    for name, src in (("w2p", "rwkv_w2"), ("a2p", "rwkv_a2"), ("g2p", "rwkv_g2"), ("v2p", "rwkv_v2")):
        lo, hi = rows[name]
        a = wf[src]
        w[name] = jnp.concatenate([jnp.zeros((a.shape[0], lo, m.dr), F32), a, jnp.zeros((a.shape[0], m.lz - hi, m.dr), F32)],
                                  axis=1)
    w["rk"] = wf["rwkv_rk"].reshape(nl, 1, m.dr)
    w["ln_final"] = wf["ln_final"].reshape(1, m.d)
    head = jnp.arange(m.dr, dtype=jnp.int32) // m.n
    w["bb"] = (head[:, None] == head[None, :]).astype(BF16)
    return w


def _unpack_grads(m, grads, d_ln_final):
    nl = m.nl
    out = {}

    def stack(key):
        return jnp.stack([grads[i][key] for i in range(nl)], axis=0)

    for k in _BIG + ("conv_a_w", "conv_f_w", "lru_wx", "lru_wa"):
        out[k] = stack(k)
    for k in ("ln_mix", "conv_a_b", "lru_bx", "lru_ba", "lru_lambda", "lru_norm", "rwkv_w0", "rwkv_a0", "rwkv_kk",
              "rwkv_ka", "rwkv_lnx_w", "rwkv_lnx_b", "ln_ffn", "conv_f_b", "ln_ple", "ln_ple_post"):
        out[k] = stack(k)[:, 0, :]
    wcat = stack("wcat")
    out["w_in"] = wcat[:, :, :m.din]
    out["w_in_vres"] = wcat[1:, :, m.din:m.din + m.lv]
    mu = stack("mu_pad")[:, 0, :]
    out["mu_shift"] = mu[:, :m.nsh]
    out["mu_shift_vres"] = mu[1:, m.nsh:m.nsh + m.lv]
    rows = _lora_rows(m)
    for name, dst in (("w2p", "rwkv_w2"), ("a2p", "rwkv_a2"), ("g2p", "rwkv_g2")):
        lo, hi = rows[name]
        out[dst] = stack(name)[:, lo:hi, :]
    lo, hi = rows["v2p"]
    out["rwkv_v2"] = jnp.stack([grads[i]["v2p"] for i in range(1, nl)], axis=0)[:, lo:hi, :]
    out["rwkv_v0"] = jnp.stack([grads[i]["rwkv_v0"] for i in range(1, nl)], axis=0)[:, 0, :]
    out["rwkv_rk"] = stack("rk").reshape(nl, m.h, m.n)
    out["ln_final"] = d_ln_final.reshape(m.d)
    return out


_ANY = pl.BlockSpec(memory_space=pl.ANY)


def _position():
    return lax.axis_index("x"), lax.axis_index("y"), lax.axis_index("c")


def _other_chips(x, y):
    return [(1 - x, y), (x, 1 - y), (1 - x, 1 - y)]


def _gather_blob(shard):
    rows, wd = shard.shape
    return shard.reshape(2, rows // 2, wd)


def _gathered_shape(blob):
    return jax.ShapeDtypeStruct((N_XY,) + blob.shape, blob.dtype)


def _gather_scratch(njobs):
    return [pltpu.SemaphoreType.DMA((7,)), pltpu.SemaphoreType.DMA((7,))] * njobs


def _gather_copies(in_ref, out_ref, send_sems, recv_sems):
    x, y, c = _position()
    me = 2 * x + y
    sends, hands, ici_in, d2d_in = [], [], [], []
    for k, (px, py) in enumerate(_other_chips(x, y)):
        landed = out_ref.at[2 * px + py, c]
        sends.append(pltpu.make_async_remote_copy(
            src_ref=in_ref.at[c], dst_ref=out_ref.at[me, c], send_sem=send_sems.at[k], recv_sem=recv_sems.at[k],
            device_id=(px, py, c), device_id_type=MESH))
        ici_in.append(pltpu.make_async_remote_copy(
            src_ref=in_ref.at[c], dst_ref=landed, send_sem=send_sems.at[k], recv_sem=recv_sems.at[k],
            device_id=(px, py, c), device_id_type=MESH))
        hands.append(pltpu.make_async_remote_copy(
            src_ref=landed, dst_ref=landed, send_sem=send_sems.at[3 + k], recv_sem=recv_sems.at[3 + k],
            device_id=(x, y, 1 - c), device_id_type=MESH))
        d2d_in.append(pltpu.make_async_remote_copy(
            src_ref=in_ref.at[c], dst_ref=out_ref.at[2 * px + py, 1 - c], send_sem=send_sems.at[3 + k],
            recv_sem=recv_sems.at[3 + k], device_id=(x, y, 1 - c), device_id_type=MESH))
    own = pltpu.make_async_remote_copy(src_ref=in_ref, dst_ref=out_ref.at[me], send_sem=send_sems.at[6],
                                       recv_sem=recv_sems.at[6], device_id=(x, y, 1 - c), device_id_type=MESH)
    sends.append(own)
    d2d_in.append(own)
    return sends, hands, ici_in, d2d_in


def _gather_start(in_ref, out_ref, send_sems, recv_sems):
    for cp in _gather_copies(in_ref, out_ref, send_sems, recv_sems)[0]:
        cp.start()


def _gather_finish(in_ref, out_ref, send_sems, recv_sems):
    sends, hands, ici_in, d2d_in = _gather_copies(in_ref, out_ref, send_sems, recv_sems)
    for arrived, hand in zip(ici_in, hands):
        arrived.wait_recv()
        hand.start()
    for arrived in d2d_in:
        arrived.wait_recv()
    for cp in sends + hands:
        cp.wait_send()


def _all_gather_xy(blobs, *, name):
    ng = len(blobs)

    def body(*refs):
        for q in range(ng):
            _gather_start(refs[q], refs[ng + q], *refs[2 * ng + 2 * q:2 * ng + 2 * q + 2])
        for q in range(ng):
            _gather_finish(refs[q], refs[ng + q], *refs[2 * ng + 2 * q:2 * ng + 2 * q + 2])

    return pl.pallas_call(body, name=name, in_specs=[_ANY] * ng, out_specs=[_ANY] * ng,
                          out_shape=[_gathered_shape(b) for b in blobs], scratch_shapes=_gather_scratch(ng))(*blobs)


def _pair_send_half(g, *, name):
    nq, r, wd = g.shape
    half = r // 2

    def body(g_ref, out_ref, send_sem, recv_sem):
        x, y, c = _position()
        cp = pltpu.make_async_remote_copy(src_ref=g_ref.at[:, pl.ds((1 - c) * half, half), :], dst_ref=out_ref,
                                          send_sem=send_sem, recv_sem=recv_sem, device_id=(x, y, 1 - c), device_id_type=MESH)
        cp.start()
        cp.wait()

    return pl.pallas_call(
        body, name=name, in_specs=[_ANY], out_specs=_ANY, out_shape=jax.ShapeDtypeStruct((nq, half, wd), g.dtype),
        scratch_shapes=[pltpu.SemaphoreType.DMA(()), pltpu.SemaphoreType.DMA(())],
    )(g)


def _pair_sum(g, got, pos, *, name):
    nq, r, wd = g.shape
    half = r // 2
    tr = _tile(half, (256, 128, 64, 32, 16, 8))
    nb = half // tr

    def body(c_ref, g_ref, got_ref, o_ref):
        o_ref[...] = (g_ref[...] + got_ref[...]).astype(o_ref.dtype)

    grid_spec = pltpu.PrefetchScalarGridSpec(
        num_scalar_prefetch=1, grid=(nq, nb),
        in_specs=[pl.BlockSpec((1, tr, wd), lambda q, j, c_ref: (q, c_ref[0] * nb + j, 0)),
                  pl.BlockSpec((1, tr, wd), lambda q, j, c_ref: (q, j, 0))],
        out_specs=pl.BlockSpec((1, tr, wd), lambda q, j, c_ref: (q, j, 0)))
    return pl.pallas_call(body, name=name, grid_spec=grid_spec, out_shape=jax.ShapeDtypeStruct((nq, half, wd), BF16),
                          compiler_params=_cparams(("arbitrary", "arbitrary")))(pos[0], g, got)


def _exchange_xy(pb, *, name):
    def body(in_ref, out_ref, send_sems, recv_sems):
        x, y, c = _position()
        me = 2 * x + y
        sends = []
        for k, (px, py) in enumerate(_other_chips(x, y)):
            cp = pltpu.make_async_remote_copy(src_ref=in_ref.at[2 * px + py], dst_ref=out_ref.at[me], send_sem=send_sems.at[k],
                                              recv_sem=recv_sems.at[k], device_id=(px, py, c), device_id_type=MESH)
            cp.start()
            sends.append(cp)
        for k, (px, py) in enumerate(_other_chips(x, y)):
            pltpu.make_async_remote_copy(src_ref=in_ref.at[me], dst_ref=out_ref.at[2 * px + py], send_sem=send_sems.at[k],
                                         recv_sem=recv_sems.at[k], device_id=(px, py, c), device_id_type=MESH).wait_recv()
        for cp in sends:
            cp.wait_send()

    return pl.pallas_call(
        body, name=name, in_specs=[_ANY], out_specs=_ANY, out_shape=jax.ShapeDtypeStruct(pb.shape, pb.dtype),
        scratch_shapes=[pltpu.SemaphoreType.DMA((3,)), pltpu.SemaphoreType.DMA((3,))],
    )(pb)


def _chip_sum(parts, pb, pos, *, name):
    nq, half, wd = parts.shape
    tr = _tile(half, (256, 128, 64, 32, 16, 8))
    nb = half // tr

    def body(c_ref, x_ref, y_ref, p_ref, own_ref, o_ref):
        chip = 2 * x_ref[0] + y_ref[0]
        own = own_ref[0].astype(F32)
        acc = None
        for q in range(nq):
            term = jnp.where(chip == q, own, p_ref[q].astype(F32))
            acc = term if acc is None else acc + term
        o_ref[...] = acc

    grid_spec = pltpu.PrefetchScalarGridSpec(
        num_scalar_prefetch=3, grid=(nb,),
        in_specs=[pl.BlockSpec((nq, tr, wd), lambda j, c_ref, x_ref, y_ref: (0, j, 0)),
                  pl.BlockSpec((1, tr, wd), lambda j, c_ref, x_ref, y_ref: (2 * x_ref[0] + y_ref[0], j, 0))],
        out_specs=pl.BlockSpec((tr, wd), lambda j, c_ref, x_ref, y_ref: (c_ref[0] * nb + j, 0)))
    return pl.pallas_call(body, name=name, grid_spec=grid_spec, out_shape=jax.ShapeDtypeStruct((2 * half, wd), F32),
                          compiler_params=_cparams(("arbitrary",)))(*pos, parts, pb)


def _pair_gather(full, *, name):
    r, wd = full.shape
    half = r // 2

    def body(in_ref, out_ref, send_sem, recv_sem):
        x, y, c = _position()
        mine = out_ref.at[pl.ds(c * half, half), :]
        cp = pltpu.make_async_remote_copy(src_ref=mine, dst_ref=mine, send_sem=send_sem, recv_sem=recv_sem,
                                          device_id=(x, y, 1 - c), device_id_type=MESH)
        cp.start()
        pltpu.make_async_remote_copy(src_ref=mine, dst_ref=out_ref.at[pl.ds((1 - c) * half, half), :], send_sem=send_sem,
                                     recv_sem=recv_sem, device_id=(x, y, 1 - c), device_id_type=MESH).wait_recv()
        cp.wait_send()

    return pl.pallas_call(
        body, name=name, in_specs=[_ANY], out_specs=_ANY, out_shape=jax.ShapeDtypeStruct(full.shape, full.dtype),
        input_output_aliases={0: 0}, scratch_shapes=[pltpu.SemaphoreType.DMA(()), pltpu.SemaphoreType.DMA(())],
    )(full)


def _reduce_to_shard(g, pos, tag):
    got = _pair_send_half(g, name="rs_pair_send_" + tag)
    pb = _pair_sum(g, got, pos, name="rs_pair_sum_" + tag)
    parts = _exchange_xy(pb, name="rs_exchange_" + tag)
    full = _chip_sum(parts, pb, pos, name="rs_chip_sum_" + tag)
    return _pair_gather(full, name="rs_pair_gather_" + tag)


def _all_reduce_small(vec, *, name):
    r, wd = vec.shape

    def body(in_ref, out_ref, slots, send_sems, recv_sems):
        x, y, c = _position()
        me = 4 * x + 2 * y + c
        flips = [(fx, fy, fc) for fx in (0, 1) for fy in (0, 1) for fc in (0, 1) if fx + fy + fc]
        peers = [(1 - x if fx else x, 1 - y if fy else y, 1 - c if fc else c) for fx, fy, fc in flips]
        sends = []
        for k, peer in enumerate(peers):
            cp = pltpu.make_async_remote_copy(src_ref=in_ref, dst_ref=slots.at[me], send_sem=send_sems.at[k],
                                              recv_sem=recv_sems.at[k], device_id=peer, device_id_type=MESH)
            cp.start()
            sends.append(cp)
        slots[me] = in_ref[...]
        for k, (px, py, pc) in enumerate(peers):
            pltpu.make_async_remote_copy(src_ref=in_ref, dst_ref=slots.at[4 * px + 2 * py + pc], send_sem=send_sems.at[k],
                                         recv_sem=recv_sems.at[k], device_id=(px, py, pc), device_id_type=MESH).wait_recv()
        for cp in sends:
            cp.wait_send()
        acc = slots[0]
        for q in range(1, N_DEV):
            acc = acc + slots[q]
        out_ref[...] = acc

    vm = pl.BlockSpec(memory_space=pltpu.VMEM)
    return pl.pallas_call(
        body, name=name, in_specs=[vm], out_specs=vm, out_shape=jax.ShapeDtypeStruct((r, wd), F32),
        scratch_shapes=[pltpu.VMEM((N_DEV, r, wd), F32), pltpu.SemaphoreType.DMA((N_DEV - 1,)),
                        pltpu.SemaphoreType.DMA((N_DEV - 1,))],
        compiler_params=_cparams(),
    )(vec)


def _adamw(w, g, m, v, *, name):
    r, wd = w.shape
    tr = _tile(r, (256, 128, 64, 32, 16, 8))

    def body(w_ref, g_ref, m_ref, v_ref, d_ref, m_out, v_out):
        gv = g_ref[...]
        m_new = ADAM_B1 * m_ref[...] + (1.0 - ADAM_B1) * gv
        v_new = ADAM_B2 * v_ref[...] + (1.0 - ADAM_B2) * (gv * gv)
        m_hat = m_new / (1.0 - ADAM_B1 ** ADAM_STEP)
        v_hat = v_new / (1.0 - ADAM_B2 ** ADAM_STEP)
        d_ref[...] = -ADAM_LR * (m_hat / (jnp.sqrt(v_hat) + ADAM_EPS) + ADAM_WD * w_ref[...])
        m_out[...] = m_new
        v_out[...] = v_new

    spec = pl.BlockSpec((tr, wd), lambda j: (j, 0))
    return pl.pallas_call(body, name=name, grid=(r // tr,), in_specs=[spec] * 4, out_specs=[spec] * 3,
                          out_shape=[jax.ShapeDtypeStruct((r, wd), F32)] * 3, compiler_params=_cparams(("arbitrary",)))(w, g, m, v)


_WEIGHTS = ("ln_mix", "w_in", "w_in_vres", "mu_shift", "mu_shift_vres", "conv_a_w", "conv_a_b", "lru_wx", "lru_bx", "lru_wa",
            "lru_ba", "lru_lambda", "lru_norm", "rwkv_w0", "rwkv_w2", "rwkv_a0", "rwkv_a2", "rwkv_v0", "rwkv_v2", "rwkv_g2",
            "rwkv_kk", "rwkv_ka", "rwkv_rk", "rwkv_lnx_w", "rwkv_lnx_b", "w_o", "ln_ffn", "w_gate", "w_up", "conv_f_w",
            "conv_f_b", "w_down", "ln_ple", "w_ple_gate", "w_ple_proj", "ln_ple_post", "ln_final")
_SHARD_AXIS = {"w_in": 2, "w_in_vres": 1, "conv_a_w": 2, "lru_wx": 2, "lru_wa": 2, "rwkv_w2": 2, "rwkv_a2": 2, "rwkv_v2": 2,
               "rwkv_g2": 2, "w_o": 1, "w_gate": 2, "w_up": 2, "conv_f_w": 2, "w_down": 1, "w_ple_gate": 1, "w_ple_proj": 2}
_BIG_SHARDED = ("w_in",) + _BIG
_SMALL_SHARDED = tuple(k for k in _WEIGHTS if k in _SHARD_AXIS and k not in _BIG_SHARDED)
_REPLICATED = tuple(k for k in _WEIGHTS if k not in _SHARD_AXIS)
PACK_WIDTH = 512


def _to_shards(g, axis):
    n = g.shape[axis] // N_XY
    return jnp.moveaxis(g.reshape(g.shape[:axis] + (N_XY, n) + g.shape[axis + 1:]), axis, 0)


def _from_shards(s, axis):
    s = jnp.moveaxis(s, 0, axis)
    return s.reshape(s.shape[:axis] + (N_XY * s.shape[axis + 1],) + s.shape[axis + 2:])


def _pack(arrs, lead, width, row_mult):
    lead_shape = arrs[0].shape[:lead]
    flat = jnp.concatenate([a.reshape(lead_shape + (-1,)) for a in arrs], axis=-1)
    n = flat.shape[-1]
    total = _round_up(n, width * row_mult)
    flat = jnp.pad(flat, [(0, 0)] * lead + [(0, total - n)])
    return flat.reshape(lead_shape + (total // width, width))


def _unpack(packed, shapes):
    flat = packed.reshape(-1)
    out, o = [], 0
    for s in shapes:
        n = 1
        for dim in s:
            n *= dim
        out.append(flat[o:o + n].reshape(s))
        o += n
    return out


def _as2d(a):
    return a.reshape(-1, a.shape[-1])


def kernel(x, p, ln_mix, w_in, w_in_vres, mu_shift, mu_shift_vres, conv_a_w, conv_a_b, lru_wx, lru_bx, lru_wa, lru_ba, lru_lambda, lru_norm, rwkv_w0, rwkv_w2, rwkv_a0, rwkv_a2, rwkv_v0, rwkv_v2, rwkv_g2, rwkv_kk, rwkv_ka, rwkv_rk, rwkv_lnx_w, rwkv_lnx_b, w_o, ln_ffn, w_gate, w_up, conv_f_w, conv_f_b, w_down, ln_ple, w_ple_gate, w_ple_proj, ln_ple_post, ln_final, loss_target, m_ln_mix, m_w_in, m_w_in_vres, m_mu_shift, m_mu_shift_vres, m_conv_a_w, m_conv_a_b, m_lru_wx, m_lru_bx, m_lru_wa, m_lru_ba, m_lru_lambda, m_lru_norm, m_rwkv_w0, m_rwkv_w2, m_rwkv_a0, m_rwkv_a2, m_rwkv_v0, m_rwkv_v2, m_rwkv_g2, m_rwkv_kk, m_rwkv_ka, m_rwkv_rk, m_rwkv_lnx_w, m_rwkv_lnx_b, m_w_o, m_ln_ffn, m_w_gate, m_w_up, m_conv_f_w, m_conv_f_b, m_w_down, m_ln_ple, m_w_ple_gate, m_w_ple_proj, m_ln_ple_post, m_ln_final, v_ln_mix, v_w_in, v_w_in_vres, v_mu_shift, v_mu_shift_vres, v_conv_a_w, v_conv_a_b, v_lru_wx, v_lru_bx, v_lru_wa, v_lru_ba, v_lru_lambda, v_lru_norm, v_rwkv_w0, v_rwkv_w2, v_rwkv_a0, v_rwkv_a2, v_rwkv_v0, v_rwkv_v2, v_rwkv_g2, v_rwkv_kk, v_rwkv_ka, v_rwkv_rk, v_rwkv_lnx_w, v_rwkv_lnx_b, v_w_o, v_ln_ffn, v_w_gate, v_w_up, v_conv_f_w, v_conv_f_b, v_w_down, v_ln_ple, v_w_ple_gate, v_w_ple_proj, v_ln_ple_post, v_ln_final):
    a = dict(locals())
    x2, p, tgt = a["x"][0], a["p"][:, 0], a["loss_target"][0]
    pos = tuple(lax.axis_index(ax).astype(jnp.int32).reshape(1) for ax in ("c", "x", "y"))

    wf = {k: a[k] for k in _REPLICATED}
    small_shapes = [a[k].shape for k in _SMALL_SHARDED]
    shards = {k: [a[k][i].astype(BF16) for i in range(a[k].shape[0])] for k in _BIG_SHARDED}
    packed = _pack([a[k] for k in _SMALL_SHARDED], 0, PACK_WIDTH, 16)
    got_small, got_w_in = _all_gather_xy([_gather_blob(packed), _gather_blob(shards["w_in"][0])], name="ag_first")
    got_small = got_small.reshape((N_XY,) + packed.shape)
    pieces = [_unpack(got_small[q], small_shapes) for q in range(N_XY)]
    for j, k in enumerate(_SMALL_SHARDED):
        wf[k] = _from_shards(jnp.stack([pieces[q][j] for q in range(N_XY)], axis=0), _SHARD_AXIS[k])

    m = _make_dims(x2, p, wf)
    w = _prepare_weights(m, wf)
    feed = _WeightFeed(m, w, shards, wf["w_in_vres"].astype(BF16))
    feed.arrive([("w_in", 0)], [got_w_in])
    loss_row, dx, grads, d_ln_final = _local_step(m, w, x2, p, tgt, feed)
    gfull = _unpack_grads(m, grads, d_ln_final)
    loss = lax.psum(loss_row[0, 0], ("x", "y", "c"))

    gred = {}
    for k in _BIG_SHARDED:
        gs = _to_shards(gfull[k], _SHARD_AXIS[k])
        gred[k] = _reduce_to_shard(gs.reshape(N_XY, -1, gs.shape[-1]), pos, k).reshape(gs.shape[1:])
    gs = _pack([_to_shards(gfull[k], _SHARD_AXIS[k]) for k in _SMALL_SHARDED], 1, PACK_WIDTH, 32)
    g_small = _reduce_to_shard(gs, pos, "small")
    rep_shapes = [a[k].shape for k in _REPLICATED]
    g_rep = _all_reduce_small(_pack([gfull[k] for k in _REPLICATED], 0, LANES_V7X, 8), name="ar_replicated")

    delta, new_m, new_v = {}, {}, {}
    for k in _BIG_SHARDED:
        res = _adamw(_as2d(a[k]), _as2d(gred[k]), _as2d(a["m_" + k]), _as2d(a["v_" + k]), name="adamw_" + k)
        delta[k], new_m[k], new_v[k] = (r.reshape(a[k].shape) for r in res)
    for names, shapes, g_packed, width, mult, tag in ((_SMALL_SHARDED, small_shapes, g_small, PACK_WIDTH, 32, "small"),
                                                      (_REPLICATED, rep_shapes, g_rep, LANES_V7X, 8, "replicated")):
        packs = [_pack([a[pre + k] for k in names], 0, width, mult) for pre in ("", "m_", "v_")]
        res = _adamw(packs[0], g_packed, packs[1], packs[2], name="adamw_" + tag)
        for dst, r in zip((gred, delta, new_m, new_v), [g_packed] + list(res)):
            dst.update(zip(names, _unpack(r, shapes)))
    return (loss, dx[None], *[gred[k] for k in _WEIGHTS], *[delta[k] for k in _WEIGHTS],
            *[new_m[k] for k in _WEIGHTS], *[new_v[k] for k in _WEIGHTS])
```

```python
import functools

import jax
import jax.numpy as jnp
from jax import lax
from jax.experimental import pallas as pl
from jax.experimental.pallas import tpu as pltpu

F32 = jnp.float32
BF16 = jnp.bfloat16
HIGHEST = lax.Precision.HIGHEST
MESH = pl.DeviceIdType.MESH

RMS_EPS = 1e-6
LNX_EPS = 64e-5
LRU_C = 8.0
ADAM_LR = 0.001
ADAM_B1 = 0.9
ADAM_B2 = 0.999
ADAM_EPS = 1e-08
ADAM_WD = 0.01
ADAM_STEP = 10

LANES_V7X = 128
VMEM_LIMIT_V7X = 60 * 1024 * 1024
WKV_CHUNK = 16
N_XY = 4
N_DEV = 8


def _cparams(sem=None, **kw):
    if sem is not None:
        kw["dimension_semantics"] = sem
    return pltpu.CompilerParams(vmem_limit_bytes=VMEM_LIMIT_V7X, **kw)


def _tile(dim, prefs):
    for t in prefs:
        if dim % t == 0:
            return t
    return dim


def _round_up(n, m):
    return (n + m - 1) // m * m


MM_MAX_TK = 2816


def _tile_k(kdim):
    best = None
    for t in range(LANES_V7X, min(kdim, MM_MAX_TK) + 1, LANES_V7X):
        if kdim % t == 0:
            best = t
    return best or kdim


def _mm(a, b, *, ta=False, tb=False, res=None, out_dtype=F32, name, gather=()):
    if ta:
        kdim, m = a.shape
    else:
        m, kdim = a.shape
    n = b.shape[0] if tb else b.shape[1]
    assert (b.shape[1] if tb else b.shape[0]) == kdim
    tk = _tile_k(kdim)
    tm = _tile(m, (2048, 1024, 512, 256, 128) if tk <= 2048 else (1024, 512, 256, 128))
    tn = _tile(n, (512, 256, 128))
    nk = kdim // tk
    ni, nj = m // tm, n // tn
    a_spec = pl.BlockSpec((tk, tm), lambda i, j, k: (k, i)) if ta else pl.BlockSpec((tm, tk), lambda i, j, k: (i, k))
    b_spec = pl.BlockSpec((tn, tk), lambda i, j, k: (j, k)) if tb else pl.BlockSpec((tk, tn), lambda i, j, k: (k, j))
    o_spec = pl.BlockSpec((tm, tn), lambda i, j, k: (i, j))
    dn = (((0 if ta else 1,), (1 if tb else 0,)), ((), ()))
    has_res = res is not None
    ng = len(gather)
    nin = 2 + has_res

    def body(*refs):
        a_ref, b_ref = refs[:2]
        r_ref = refs[2] if has_res else None
        g_in, o_ref, g_out = refs[nin:nin + ng], refs[nin + ng], refs[nin + ng + 1:nin + 2 * ng + 1]
        scratch = refs[nin + 2 * ng + 1:]
        acc_ref = scratch[0] if nk > 1 else None
        g_sems = scratch[1 if nk > 1 else 0:]
        i, j, k = pl.program_id(0), pl.program_id(1), pl.program_id(2)

        if ng:
            @pl.when((i == 0) & (j == 0) & (k == 0))
            def _():
                _jobs_start(gather, g_in, g_out, g_sems)

        def finish(acc):
            if has_res:
                acc = acc + r_ref[...].astype(F32)
            o_ref[...] = acc.astype(out_dtype)

        prod = lax.dot_general(a_ref[...], b_ref[...], dn, preferred_element_type=F32)
        if nk == 1:
            finish(prod)
        else:
            @pl.when(k == 0)
            def _():
                acc_ref[...] = prod

            @pl.when(k > 0)
            def _():
                acc_ref[...] += prod

            @pl.when(k == nk - 1)
            def _():
                finish(acc_ref[...])

        if ng:
            @pl.when((i == ni - 1) & (j == nj - 1) & (k == nk - 1))
            def _():
                _jobs_finish(gather, g_in, g_out, g_sems)

    ins = [a, b] + ([res] if has_res else []) + _job_arrays(gather)
    in_specs = [a_spec, b_spec] + ([o_spec] if has_res else []) + [_ANY] * ng
    scratch = ([pltpu.VMEM((tm, tn), F32)] if nk > 1 else []) + _jobs_scratch(gather)
    sem = ("arbitrary",) * 3 if ng else ("parallel", "parallel", "arbitrary")
    out = pl.pallas_call(
        body, name=name, grid=(ni, nj, nk), in_specs=in_specs, out_specs=[o_spec] + [_ANY] * ng,
        out_shape=[jax.ShapeDtypeStruct((m, n), out_dtype)] + [_job_out_shape(g) for g in gather],
        scratch_shapes=scratch, compiler_params=_cparams(sem),
    )(*ins)
    return (out[0], list(out[1:])) if ng else out[0]


def _stage_specs(axis, tile, tiled, params, consts, rows):
    specs = []
    for arr, width, cblk in tiled:
        if axis == 0:
            specs.append(pl.BlockSpec((tile, width), functools.partial(lambda i, c: (i, c), c=cblk)))
        else:
            specs.append(pl.BlockSpec((rows, tile), functools.partial(lambda i, c: (0, i + c), c=cblk)))
    for arr, cblk in params:
        if axis == 0:
            specs.append(pl.BlockSpec(arr.shape, functools.partial(lambda i, nd: (0,) * nd, nd=arr.ndim)))
        else:
            specs.append(pl.BlockSpec((arr.shape[0], tile), functools.partial(lambda i, c: (0, i + c), c=cblk)))
    for arr in consts:
        specs.append(pl.BlockSpec(arr.shape, functools.partial(lambda i, nd: (0,) * nd, nd=arr.ndim)))
    return specs


def _stage_fwd(fn, tiled, params, consts, outs, *, axis, tile, rows, name):
    nt, npar, nc = len(tiled), len(params), len(consts)
    ntiles = (rows // tile) if axis == 0 else (outs[0][0] // tile)

    def body(*refs):
        ins = refs[: nt + npar + nc]
        orefs = refs[nt + npar + nc:]
        vals = [r[...].astype(F32) for r in ins[: nt + npar]] + [r[...] for r in ins[nt + npar:]]
        ctx = pl.program_id(0) * tile
        res = fn(ctx, *vals)
        for o_ref, o in zip(orefs, res):
            o_ref[...] = o.astype(o_ref.dtype)

    if axis == 0:
        out_specs = [pl.BlockSpec((tile, w), lambda i: (i, 0)) for w, _ in outs]
    else:
        out_specs = [pl.BlockSpec((rows, tile), lambda i: (0, i)) for w, _ in outs]
    res = pl.pallas_call(
        body, name=name, grid=(ntiles,),
        in_specs=_stage_specs(axis, tile, tiled, params, consts, rows), out_specs=out_specs,
        out_shape=[jax.ShapeDtypeStruct((rows, w), dt) for w, dt in outs],
        compiler_params=_cparams(("arbitrary",)),
    )(*[t[0] for t in tiled], *[p[0] for p in params], *consts)
    return res


def _stage_bwd(fn, tiled, params, consts, cots, dtiled, *, axis, tile, rows, name, ncols=None):
    nt, npar, nc, nco = len(tiled), len(params), len(consts), len(cots)
    ntiles = (rows // tile) if axis == 0 else (ncols // tile)
    didx = [d[0] for d in dtiled]

    def body(*refs):
        ins = refs[: nt + npar + nc]
        crefs = refs[nt + npar + nc: nt + npar + nc + nco]
        orefs = refs[nt + npar + nc + nco:]
        vals = [r[...].astype(F32) for r in ins[: nt + npar]] + [r[...] for r in ins[nt + npar:]]
        ctx = pl.program_id(0) * tile

        def g(*dv):
            full = list(vals)
            for j, ix in enumerate(didx):
                full[ix] = dv[j]
            for j in range(npar):
                full[nt + j] = dv[len(didx) + j]
            return tuple(fn(ctx, *full))

        prim = [vals[ix] for ix in didx] + [vals[nt + j] for j in range(npar)]
        _, vjp = jax.vjp(g, *prim)
        grads = vjp(tuple(c[...].astype(F32) for c in crefs))
        for j in range(len(didx)):
            orefs[j][...] = grads[j].astype(orefs[j].dtype)
        for j in range(npar):
            o_ref = orefs[len(didx) + j]
            gp = grads[len(didx) + j]
            if axis == 0:
                @pl.when(pl.program_id(0) == 0)
                def _(o_ref=o_ref):
                    o_ref[...] = jnp.zeros_like(o_ref)

                o_ref[...] += gp
            else:
                o_ref[...] = gp

    if axis == 0:
        cot_specs = [pl.BlockSpec((tile, w), functools.partial(lambda i, c: (i, c), c=cb)) for _, w, cb in cots]
        out_specs = [pl.BlockSpec((tile, w), lambda i: (i, 0)) for _, w, _ in dtiled]
        out_specs += [pl.BlockSpec(p.shape, functools.partial(lambda i, nd: (0,) * nd, nd=p.ndim)) for p, _ in params]
        out_shape = [jax.ShapeDtypeStruct((rows, w), dt) for _, w, dt in dtiled]
        out_shape += [jax.ShapeDtypeStruct(p.shape, F32) for p, _ in params]
    else:
        cot_specs = [pl.BlockSpec((rows, tile), functools.partial(lambda i, c: (0, i + c), c=cb)) for _, cb in cots]
        out_specs = [pl.BlockSpec((rows, tile), lambda i: (0, i)) for _ in dtiled]
        out_specs += [pl.BlockSpec((p.shape[0], tile), lambda i: (0, i)) for p, _ in params]
        out_shape = [jax.ShapeDtypeStruct((rows, w), dt) for _, w, dt in dtiled]
        out_shape += [jax.ShapeDtypeStruct((p.shape[0], ncols), F32) for p, _ in params]
    return pl.pallas_call(
        body, name=name, grid=(ntiles,),
        in_specs=_stage_specs(axis, tile, tiled, params, consts, rows) + cot_specs, out_specs=out_specs,
        out_shape=out_shape, compiler_params=_cparams(("arbitrary",)),
    )(*[t[0] for t in tiled], *[p[0] for p in params], *consts, *[c[0] for c in cots])


def _rms(x, g):
    return x * lax.rsqrt(jnp.mean(x * x, axis=-1, keepdims=True) + RMS_EPS) * g


def _row_mask(x, k, first):
    t = lax.broadcasted_iota(jnp.int32, x.shape, 0)
    keep = (t >= k) if first else (t < x.shape[0] - k)
    return jnp.where(keep, x, 0.0)


@functools.partial(jax.custom_vjp, nondiff_argnums=(1,))
def _shift_down(x, k):
    return _row_mask(pltpu.roll(x, k, 0), k, True)


def _shift_down_fwd(x, k):
    return _shift_down(x, k), None


def _shift_down_bwd(k, _, g):
    return (_row_mask(pltpu.roll(g, g.shape[0] - k, 0), k, False),)


_shift_down.defvjp(_shift_down_fwd, _shift_down_bwd)


def _dwconv(x, w, b):
    kw = w.shape[0]
    out = x * w[kw - 1:kw] + b
    for j in range(kw - 1):
        out = out + _shift_down(x, kw - 1 - j) * w[j:j + 1]
    return out


def _f_norm(ctx, x, g):
    return (_rms(x, g),)


def _f_norm_res(ctx, x, g):
    return (_rms(x, g), x)


def _f_shiftmix(ctx, z, mu):
    return (z + (_shift_down(z, 1) - z) * mu,)


def _f_conv(ctx, x, w, b):
    return (_dwconv(x, w, b),)


def _f_ffn_act(ctx, gpre, up, w, b):
    return (jax.nn.gelu(_dwconv(gpre, w, b)) * up,)


def _make_f_lru_gates(heads):
    def fn(ctx, xb, wx, wa, bx, ba, lam):
        blk = xb.shape[1] // heads
        px, pa = [], []
        for h in range(heads):
            xh = xb[:, h * blk:(h + 1) * blk]
            px.append(jnp.dot(xh, wx[h], preferred_element_type=F32))
            pa.append(jnp.dot(xh, wa[h], preferred_element_type=F32))
        px = px[0] if heads == 1 else jnp.concatenate(px, axis=1)
        pa = pa[0] if heads == 1 else jnp.concatenate(pa, axis=1)
        gate_x = jax.nn.sigmoid(px + bx)
        gate_a = jax.nn.sigmoid(pa + ba)
        log_a = -LRU_C * gate_a * jax.nn.softplus(-lam)
        a = jnp.exp(log_a)
        mult = jnp.sqrt(1.0 - jnp.exp(2.0 * log_a))
        t = ctx + lax.broadcasted_iota(jnp.int32, xb.shape, 0)
        mult = jnp.where(t == 0, 1.0, mult)
        return a, xb * gate_x * mult

    return fn


def _f_lru_out(ctx, hl, ya, g):
    return (_rms(hl * jax.nn.gelu(ya), g),)


def _headsum_3pass(x, bb):
    hi = x.astype(BF16)
    r1 = x - hi.astype(F32)
    mid = r1.astype(BF16)
    lo = (r1 - mid.astype(F32)).astype(BF16)
    return (jnp.dot(hi, bb, preferred_element_type=F32) + jnp.dot(mid, bb, preferred_element_type=F32)
            + jnp.dot(lo, bb, preferred_element_type=F32))


@jax.custom_vjp
def _headsum(x, bb):
    return _headsum_3pass(x, bb)


def _headsum_fwd(x, bb):
    return _headsum_3pass(x, bb), bb


def _headsum_bwd(bb, g):
    return _headsum_3pass(g, bb), None


_headsum.defvjp(_headsum_fwd, _headsum_bwd)


def _make_f_rwkv_pre(has_vres, v_uses=0):
    def fn(ctx, *args):
        if v_uses:
            r, args = args[0], args[1:]
        if has_vres:
            k, v, lz, vf, w0, w2, a0, a2, g2, kkw, ka, v0, v2, bb = args
        else:
            k, v, lz, w0, w2, a0, a2, g2, kkw, ka, bb = args
        w_log = -jax.nn.softplus(-(w0 + jnp.dot(jnp.tanh(lz), w2, preferred_element_type=F32))) - 0.5
        logw = -jnp.exp(w_log)
        a = jax.nn.sigmoid(a0 + jnp.dot(lz, a2, preferred_element_type=F32))
        g = jnp.dot(jax.nn.sigmoid(lz), g2, preferred_element_type=F32)
        if has_vres:
            v = v + (vf - v) * jax.nn.sigmoid(v0 + jnp.dot(lz, v2, preferred_element_type=F32))
        xk = k * kkw
        kk = xk / jnp.maximum(jnp.sqrt(_headsum(xk * xk, bb)), 1e-12)
        k2 = k * (1.0 + (a - 1.0) * ka)
        if v_uses:
            return (r, r, logw, k2, k2) + (v,) * v_uses + (kk, kk * a, g)
        return logw, k2, v, kk, kk * a, g

    return fn


def _make_f_rwkv_post(head_size):
    def fn(ctx, y, r, k2, v2, g, lnw, lnb, rk, bb):
        mean = _headsum(y, bb) / head_size
        d = y - mean
        var = _headsum(d * d, bb) / head_size
        yn = d * lax.rsqrt(var + LNX_EPS) * lnw + lnb
        bonus = _headsum(r * k2 * rk, bb) * v2
        return ((yn + bonus) * g,)

    return fn


def _f_ple(ctx, h, eg, ep, g):
    return (h + _rms(jax.nn.sigmoid(eg) * ep, g),)


def _lru_scan(a, b, *, name):
    rows, cols = a.shape
    tc = _tile(cols, (512, 256, 128))

    def body(a_ref, b_ref, h_ref):
        def step(t, carry):
            h = a_ref[pl.ds(t, 1), :] * carry + b_ref[pl.ds(t, 1), :]
            h_ref[pl.ds(t, 1), :] = h
            return h

        lax.fori_loop(0, rows, step, jnp.zeros((1, tc), F32), unroll=8)

    spec = pl.BlockSpec((rows, tc), lambda j: (0, j))
    return pl.pallas_call(body, name=name, grid=(cols // tc,), in_specs=[spec, spec], out_specs=spec,
                          out_shape=jax.ShapeDtypeStruct((rows, cols), F32), compiler_params=_cparams(("arbitrary",)))(a, b)


def _lru_scan_bwd(a, h, dh, *, name):
    rows, cols = a.shape
    tc = _tile(cols, (512, 256, 128))

    def body(a_ref, h_ref, dh_ref, da_ref, db_ref):
        def step(i, carry):
            t = rows - 1 - i
            g = dh_ref[pl.ds(t, 1), :] + carry
            db_ref[pl.ds(t, 1), :] = g
            hp = h_ref[pl.ds(jnp.maximum(t - 1, 0), 1), :]
            da_ref[pl.ds(t, 1), :] = jnp.where(t > 0, g * hp, 0.0)
            return a_ref[pl.ds(t, 1), :] * g

        lax.fori_loop(0, rows, step, jnp.zeros((1, tc), F32), unroll=8)

    spec = pl.BlockSpec((rows, tc), lambda j: (0, j))
    return pl.pallas_call(body, name=name, grid=(cols // tc,), in_specs=[spec] * 3, out_specs=[spec] * 2,
                          out_shape=[jax.ShapeDtypeStruct((rows, cols), F32)] * 2,
                          compiler_params=_cparams(("arbitrary",)))(a, h, dh)


def _split_bf16(x):
    hi = x.astype(BF16)
    return hi, (x - hi.astype(F32)).astype(BF16)


def _dot3_passes(a, b, ca, cb):
    dn = (((ca,), (cb,)), ((), ()))
    ah, al = _split_bf16(a)
    bh, bl = _split_bf16(b)
    return (lax.dot_general(ah, bh, dn, preferred_element_type=F32) + lax.dot_general(al, bh, dn, preferred_element_type=F32)
            + lax.dot_general(ah, bl, dn, preferred_element_type=F32))


@functools.partial(jax.custom_vjp, nondiff_argnums=(2, 3))
def _dot3(a, b, ca, cb):
    return _dot3_passes(a, b, ca, cb)


def _dot3_fwd(a, b, ca, cb):
    return _dot3_passes(a, b, ca, cb), (a, b)


def _dot3_bwd(ca, cb, res, g):
    a, b = res
    fa, fb = 1 - ca, 1 - cb
    da = _dot3_passes(g, b, 1, fb) if ca == 1 else _dot3_passes(b, g, fb, 1)
    db = _dot3_passes(a, g, fa, 0) if cb == 0 else _dot3_passes(g, a, 0, fa)
    return da, db


_dot3.defvjp(_dot3_fwd, _dot3_bwd)


def _each(f, *lists):
    return [f(*t) for t in zip(*lists)]


def _wkv_local(r, lw, k, v, kk, b):
    c, n = r[0].shape
    row = lax.broadcasted_iota(jnp.int32, (c, c), 0)
    col = lax.broadcasted_iota(jnp.int32, (c, c), 1)
    incl = (row >= col).astype(F32)
    strict = (row > col).astype(F32)
    eye = lax.broadcasted_iota(jnp.int32, (n, n), 0) == lax.broadcasted_iota(jnp.int32, (n, n), 1)
    cl = _each(lambda x: _dot3(incl, x, 1, 0), lw)
    w_t = _each(jnp.exp, cl)
    inv_w = _each(lambda x: jnp.exp(-x), cl)
    kk_s = _each(lambda x, y, z: x * jnp.exp(y - z), kk, cl, lw)
    b_s = _each(jnp.multiply, b, inv_w)
    k_s = _each(jnp.multiply, k, inv_w)
    r_s = _each(jnp.multiply, r, w_t)
    q = _each(lambda x, y: jnp.concatenate([x, y], axis=0), kk_s, r_s)
    qb = _each(lambda x, y: _dot3(x, y, 1, 1), q, b_s)
    qk = _each(lambda x, y: _dot3(x, y, 1, 1), q, k_s)
    m = _each(lambda x: -strict * x[:c], qb)
    pb = _each(lambda x: incl * x[c:], qb)
    lkv = _each(lambda x, y: _dot3(strict * x[:c], y, 1, 0), qk, v)
    pkv = _each(lambda x, y: _dot3(incl * x[c:], y, 1, 0), qk, v)
    a = _each(lambda x, y: jnp.concatenate([x, y], axis=1), kk_s, lkv)
    steps = max(1, (c - 1).bit_length())
    for i in range(steps):
        a = _each(lambda x, y: y + _dot3(x, y, 1, 0), m, a)
        if i + 1 < steps:
            m = _each(lambda x: _dot3(x, x, 1, 0), m)
    ry = _each(lambda x, y, z, w: jnp.concatenate([x, y], axis=1) - _dot3(z, w, 1, 0), r_s, pkv, pb, a)
    w_end = _each(lambda x: x[c - 1:c, :], w_t)
    gu_low = _each(lambda x, y, z: _dot3(x, y * z, 0, 0), a, b_s, w_end)
    g = _each(lambda x, y: jnp.where(eye, jnp.broadcast_to(x, (n, n)), 0.0) - y[:n], w_end, gu_low)
    u = _each(lambda x, y, z, w: _dot3(x, y * z, 0, 0) - w[n:], v, k_s, w_end, gu_low)
    return g, u, _each(lambda x: x[:, :n], ry), _each(lambda x: x[:, n:], ry)


def _wkv_blocks(h, nchunk):
    return (_tile(h, (4, 2, 1)), _tile(nchunk, (4, 2, 1))), (h, _tile(nchunk, (4, 2, 1)))


def _wkv_fwd(r, lw, k, v, kk, b, *, name, gather=(), gather_state=()):
    h, t, n = r.shape
    c = WKV_CHUNK
    nchunk = t // c
    (hb, cb), (hs, cs) = _wkv_blocks(h, nchunk)
    ng = len(gather)
    ni, nj = h // hb, nchunk // cb

    pairs = [(i, j) for i in range(hb) for j in range(cb)]

    def local_body(*refs):
        ins, g_in = refs[:6], refs[6:6 + ng]
        g_ref, u_ref, r2_ref, y0_ref = refs[6 + ng:10 + ng]
        g_out, g_sems = refs[10 + ng:10 + 2 * ng], refs[10 + 2 * ng:]
        if ng:
            @pl.when((pl.program_id(0) == 0) & (pl.program_id(1) == 0))
            def _():
                _jobs_start(gather, g_in, g_out, g_sems)

        g, u, r2, y0 = _wkv_local(*[[ref[i, pl.ds(j * c, c)] for i, j in pairs] for ref in ins])
        for idx, (i, j) in enumerate(pairs):
            g_ref[i, j] = g[idx]
            u_ref[i, j] = u[idx]
            r2_ref[i, pl.ds(j * c, c)] = r2[idx]
            y0_ref[i, pl.ds(j * c, c)] = y0[idx]
        if ng:
            @pl.when((pl.program_id(0) == ni - 1) & (pl.program_id(1) == nj - 1))
            def _():
                _jobs_finish(gather, g_in, g_out, g_sems)

    seq = pl.BlockSpec((hb, cb * c, n), lambda i, j: (i, j, 0))
    mat = pl.BlockSpec((hb, cb, n, n), lambda i, j: (i, j, 0, 0))
    res = pl.pallas_call(
        local_body, name=name + "_local", grid=(ni, nj), in_specs=[seq] * 6 + [_ANY] * ng,
        out_specs=[mat, mat, seq, seq] + [_ANY] * ng,
        out_shape=[jax.ShapeDtypeStruct((h, nchunk, n, n), F32)] * 2 + [jax.ShapeDtypeStruct((h, t, n), F32)] * 2
        + [_job_out_shape(g) for g in gather],
        scratch_shapes=_jobs_scratch(gather),
        compiler_params=_cparams(("arbitrary", "arbitrary") if ng else ("parallel", "parallel")),
    )(r, lw, k, v, kk, b, *_job_arrays(gather))
    gm, um, r2, y0 = res[:4]
    gathered = list(res[4:])

    ng2 = len(gather_state)
    nsteps = nchunk // cs

    def state_body(*refs):
        g_ref, u_ref, r2_ref, y0_ref = refs[:4]
        g_in, (y_ref, st_ref) = refs[4:4 + ng2], refs[4 + ng2:6 + ng2]
        g_out, s_ref, g_sems = refs[6 + ng2:6 + 2 * ng2], refs[6 + 2 * ng2], refs[7 + 2 * ng2:]

        @pl.when(pl.program_id(0) == 0)
        def _():
            s_ref[...] = jnp.zeros_like(s_ref)
            _jobs_start(gather_state, g_in, g_out, g_sems)

        s = [s_ref[i] for i in range(hs)]
        for j in range(cs):
            rows = pl.ds(j * c, c)
            for i in range(hs):
                st_ref[i, j] = s[i]
                y_ref[i, rows] = _dot3(r2_ref[i, rows], s[i], 1, 1) + y0_ref[i, rows]
            s = [_dot3(s[i], g_ref[i, j], 1, 0) + u_ref[i, j] for i in range(hs)]
        for i in range(hs):
            s_ref[i] = s[i]
        if ng2:
            @pl.when(pl.program_id(0) == nsteps - 1)
            def _():
                _jobs_finish(gather_state, g_in, g_out, g_sems)

    seq = pl.BlockSpec((hs, cs * c, n), lambda j: (0, j, 0))
    mat = pl.BlockSpec((hs, cs, n, n), lambda j: (0, j, 0, 0))
    res = pl.pallas_call(
        state_body, name=name + "_state", grid=(nsteps,), in_specs=[mat, mat, seq, seq] + [_ANY] * ng2,
        out_specs=[seq, mat] + [_ANY] * ng2,
        out_shape=[jax.ShapeDtypeStruct((h, t, n), F32), jax.ShapeDtypeStruct((h, nchunk, n, n), F32)]
        + [_job_out_shape(g) for g in gather_state],
        scratch_shapes=[pltpu.VMEM((hs, n, n), F32)] + _jobs_scratch(gather_state), compiler_params=_cparams(("arbitrary",)),
    )(gm, um, r2, y0, *_job_arrays(gather_state))
    return res[0], (res[1], gm, r2), gathered + list(res[2:])


def _wkv_bwd(r, lw, k, v, kk, b, saved, dy, *, name, jobs=()):
    states, gm, r2 = saved
    h, t, n = r.shape
    c = WKV_CHUNK
    nchunk = t // c
    (hb, _), (hs, cs) = _wkv_blocks(h, nchunk)
    cb = _tile(nchunk, (8, 4, 2, 1))
    nsteps = nchunk // cs

    def state_body(g_ref, r2_ref, st_ref, dy_ref, dg_ref, du_ref, dr2_ref, ds_ref):
        @pl.when(pl.program_id(0) == 0)
        def _():
            ds_ref[...] = jnp.zeros_like(ds_ref)

        ds = [ds_ref[i] for i in range(hs)]
        for j in reversed(range(cs)):
            rows = pl.ds(j * c, c)
            for i in range(hs):
                s0 = st_ref[i, j]
                du_ref[i, j] = ds[i]
                dg_ref[i, j] = _dot3(s0, ds[i], 0, 0)
                dr2_ref[i, rows] = _dot3(dy_ref[i, rows], s0, 1, 0)
            ds = [_dot3(dy_ref[i, rows], r2_ref[i, rows], 0, 0) + _dot3(ds[i], g_ref[i, j], 1, 1) for i in range(hs)]
        for i in range(hs):
            ds_ref[i] = ds[i]

    seq = pl.BlockSpec((hs, cs * c, n), lambda j: (0, nsteps - 1 - j, 0))
    mat = pl.BlockSpec((hs, cs, n, n), lambda j: (0, nsteps - 1 - j, 0, 0))
    dg, du, dr2 = pl.pallas_call(
        state_body, name=name + "_state", grid=(nsteps,), in_specs=[mat, seq, mat, seq], out_specs=[mat, mat, seq],
        out_shape=[jax.ShapeDtypeStruct((h, nchunk, n, n), F32)] * 2 + [jax.ShapeDtypeStruct((h, t, n), F32)],
        scratch_shapes=[pltpu.VMEM((hs, n, n), F32)], compiler_params=_cparams(("arbitrary",)),
    )(gm, r2, states, dy)

    pairs = [(i, j) for i in range(hb) for j in range(cb)]
    nj_ = len(jobs)
    ni, nj = h // hb, nchunk // cb

    def local_body(*refs):
        ins, (dg_ref, du_ref, dr2_ref, dy_ref) = refs[:6], refs[6:10]
        j_in, out_refs, j_out, j_sems = refs[10:10 + nj_], refs[10 + nj_:16 + nj_], refs[16 + nj_:16 + 2 * nj_], refs[16 + 2 * nj_:]
        if nj_:
            @pl.when((pl.program_id(0) == 0) & (pl.program_id(1) == 0))
            def _():
                _jobs_start(jobs, j_in, j_out, j_sems)

        _, vjp = jax.vjp(_wkv_local, *[[ref[i, pl.ds(j * c, c)] for i, j in pairs] for ref in ins])
        grads = vjp(([dg_ref[i, j] for i, j in pairs], [du_ref[i, j] for i, j in pairs],
                     [dr2_ref[i, pl.ds(j * c, c)] for i, j in pairs], [dy_ref[i, pl.ds(j * c, c)] for i, j in pairs]))
        for o_ref, gr in zip(out_refs, grads):
            for idx, (i, j) in enumerate(pairs):
                o_ref[i, pl.ds(j * c, c)] = gr[idx]
        if nj_:
            @pl.when((pl.program_id(0) == ni - 1) & (pl.program_id(1) == nj - 1))
            def _():
                _jobs_finish(jobs, j_in, j_out, j_sems)

    seq = pl.BlockSpec((hb, cb * c, n), lambda i, j: (i, j, 0))
    mat = pl.BlockSpec((hb, cb, n, n), lambda i, j: (i, j, 0, 0))
    res = pl.pallas_call(
        local_body, name=name + "_local", grid=(ni, nj), in_specs=[seq] * 6 + [mat, mat, seq, seq] + [_ANY] * nj_,
        out_specs=[seq] * 6 + [_ANY] * nj_,
        out_shape=[jax.ShapeDtypeStruct((h, t, n), F32)] * 6 + [_job_out_shape(j) for j in jobs],
        scratch_shapes=_jobs_scratch(jobs),
        compiler_params=_cparams(("arbitrary", "arbitrary") if nj_ else ("parallel", "parallel")),
    )(r, lw, k, v, kk, b, dg, du, dr2, dy, *_job_arrays(jobs))
    return list(res[:6]), list(res[6:])


class _Dims:
    pass


def _make_dims(x, p, w):
    m = _Dims()
    m.t, m.d = x.shape[-2], x.shape[-1]
    m.nl = w["ln_mix"].shape[0]
    m.dl = w["conv_a_b"].shape[1]
    m.hl = w["lru_wx"].shape[1]
    m.dr = w["rwkv_w0"].shape[1]
    m.h, m.n = w["rwkv_rk"].shape[1], w["rwkv_rk"].shape[2]
    m.lw, m.la, m.lg, m.lv = (w[k].shape[1] for k in ("rwkv_w2", "rwkv_a2", "rwkv_g2", "rwkv_v2"))
    m.nsh = w["mu_shift"].shape[1]
    m.ff = w["conv_f_b"].shape[1]
    m.ple = p.shape[-1]
    m.din = 2 * m.dl + m.nsh
    m.lz = _round_up(m.lw + m.la + m.lg + m.lv, LANES_V7X)
    m.zw = _round_up(2 * m.dl + 3 * m.dr + m.lz, 512)
    m.zs = m.zw - 2 * m.dl
    m.tr = _tile(m.t, (256, 128, 64, 32, 16, 8))
    m.trb = _tile(m.t, (128, 64, 32, 16, 8))
    m.tcs = _tile(m.zs, (512, 256, 128))
    assert (3 * m.dr) % m.lz == 0 and (2 * m.dl) % m.tcs == 0 and m.t % WKV_CHUNK == 0
    assert m.nsh == 3 * m.dr + m.lw + m.la + m.lg
    return m


def _to_heads(m, a):
    return jnp.transpose(a.reshape(m.t, m.h, m.n), (1, 0, 2))


def _from_heads(m, a):
    return jnp.transpose(a, (1, 0, 2)).reshape(m.t, m.dr)


def _norm_fwd(m, h, g, name):
    return _stage_fwd(_f_norm, [(h, m.d, 0)], [(g, 0)], [], [(m.d, BF16)], axis=0, tile=m.tr, rows=m.t, name=name)[0]


def _norm_bwd(m, h, g, du, dres, name):
    return _stage_bwd(_f_norm_res, [(h, m.d, 0)], [(g, 0)], [], [(du, m.d, 0), (dres, m.d, 0)], [(0, m.d, F32)],
                      axis=0, tile=m.tr, rows=m.t, name=name)


def _rwkv_pre_operands(m, w, i, sv, v_first_zs, with_r):
    zs = sv["zs"]
    tiled = ([(zs, m.dr, 0)] if with_r else []) + [(zs, m.dr, 1), (zs, m.dr, 2), (zs, m.lz, 3 * m.dr // m.lz)]
    params = [(w["rwkv_w0"][i:i + 1], 0), (w["w2p"][i], 0), (w["rwkv_a0"][i:i + 1], 0), (w["a2p"][i], 0),
              (w["g2p"][i], 0), (w["rwkv_kk"][i:i + 1], 0), (w["rwkv_ka"][i:i + 1], 0)]
    if i > 0:
        tiled.append((v_first_zs, m.dr, 2))
        params += [(w["rwkv_v0"][i - 1:i], 0), (w["v2p"][i - 1], 0)]
    return tiled, params


def _rwkv_post_operands(m, w, i, sv):
    tiled = [(sv["y"], m.dr, 0), (sv["zs"], m.dr, 0), (sv["k2"], m.dr, 0), (sv["v2"], m.dr, 0), (sv["g"], m.dr, 0)]
    params = [(w["rwkv_lnx_w"][i:i + 1], 0), (w["rwkv_lnx_b"][i:i + 1], 0), (w["rk"][i], 0)]
    return tiled, params


def _lru_gate_params(w, i):
    return [(w["lru_wx"][i], 0), (w["lru_wa"][i], 0), (w["lru_bx"][i:i + 1], 0), (w["lru_ba"][i:i + 1], 0),
            (w["lru_lambda"][i:i + 1], 0)]


class _WeightFeed:
    def __init__(self, m, w, shards, vres):
        self.m, self.w, self.shards, self.vres = m, w, shards, vres

    def keys(self, carrier, i):
        plan = {"mm_in": [("w_o", i)] if i == 0 else [],
                "wkv_local": [("w_gate", i), ("w_up", i)], "wkv_state": [("w_down", i)],
                "mm_gate": [("w_ple_gate", i)], "mm_up": [("w_ple_proj", i), ("w_o", i + 1)],
                "mm_down": [("w_in", i + 1)], "mm_pgate": []}
        return [key for key in plan[carrier] if key[1] < self.m.nl]

    def blobs(self, keys):
        return [_gather_blob(self.shards[name][layer]) for name, layer in keys]

    def arrive(self, keys, gathered):
        m = self.m
        for (name, layer), got in zip(keys, gathered):
            full = got.reshape((N_XY,) + self.shards[name][layer].shape)
            full = _from_shards(full, _SHARD_AXIS[name] - 1)
            if name == "w_in":
                vres = self.vres[layer - 1] if layer > 0 else jnp.zeros((m.d, m.lv), BF16)
                self.w["wcat"][layer] = jnp.concatenate([full, vres, jnp.zeros((m.d, m.zw - m.din - m.lv), BF16)], axis=1)
            else:
                self.w[name][layer] = full


def _mm_fed(feed, carrier, i, a, b, **kw):
    keys = feed.keys(carrier, i) if feed is not None else []
    if not keys:
        return _mm(a, b, **kw)
    out, got = _mm(a, b, gather=feed.blobs(keys), **kw)
    feed.arrive(keys, got)
    return out


def _layer_fwd(m, w, i, h, p_bf, v_first_zs, feed=None):
    sv = {"h": h}
    t, dl, dr = m.t, m.dl, m.dr
    sv["u1"] = _norm_fwd(m, h, w["ln_mix"][i:i + 1], "norm_mix")
    z = sv["z"] = _mm_fed(feed, "mm_in", i, sv["u1"], w["wcat"][i], name="mm_in")
    off = 2 * dl // m.tcs
    sv["zs"] = _stage_fwd(_f_shiftmix, [(z, None, off)], [(w["mu_pad"][i], off)], [], [(m.zs, F32)],
                          axis=1, tile=m.tcs, rows=t, name="shiftmix")[0]
    tca = _tile(dl, (512, 256, 128))
    sv["xb"] = _stage_fwd(_f_conv, [(z, None, 0)], [(w["conv_a_w"][i], 0), (w["conv_a_b"][i:i + 1], 0)], [],
                          [(dl, F32)], axis=1, tile=tca, rows=t, name="conv_a")[0]
    sv["a"], b_in = _stage_fwd(_make_f_lru_gates(m.hl), [(sv["xb"], dl, 0)], _lru_gate_params(w, i), [],
                               [(dl, F32), (dl, F32)], axis=0, tile=m.tr, rows=t, name="lru_gates")
    sv["hl"] = _lru_scan(sv["a"], b_in, name="lru_scan")
    out_a = _stage_fwd(_f_lru_out, [(sv["hl"], dl, 0), (z, dl, 1)], [(w["lru_norm"][i:i + 1], 0)], [],
                       [(dl, BF16)], axis=0, tile=m.tr, rows=t, name="lru_out")[0]
    tiled, params = _rwkv_pre_operands(m, w, i, sv, v_first_zs, False)
    pre = _stage_fwd(_make_f_rwkv_pre(i > 0), tiled, params, [w["bb"]], [(dr, F32)] * 6,
                     axis=0, tile=m.tr, rows=t, name="rwkv_pre")
    sv["logw"], sv["k2"], sv["v2"], sv["kk"], sv["b"], sv["g"] = pre
    heads = [_to_heads(m, a) for a in (sv["zs"][:, :dr], sv["logw"], sv["k2"], sv["v2"], sv["kk"], sv["b"])]
    keys = [feed.keys(carrier, i) if feed is not None else [] for carrier in ("wkv_local", "wkv_state")]
    y_h, sv["states"], got = _wkv_fwd(*heads, name="wkv_fwd", gather=feed.blobs(keys[0]) if keys[0] else (),
                                      gather_state=feed.blobs(keys[1]) if keys[1] else ())
    if keys[0] or keys[1]:
        feed.arrive(keys[0] + keys[1], got)
    sv["y"] = _from_heads(m, y_h)
    tiled, params = _rwkv_post_operands(m, w, i, sv)
    out_b = _stage_fwd(_make_f_rwkv_post(m.n), tiled, params, [w["bb"]], [(dr, BF16)],
                       axis=0, tile=m.tr, rows=t, name="rwkv_post")[0]
    sv["cat"] = jnp.concatenate([out_a, out_b], axis=1)
    h2 = sv["h2"] = _mm(sv["cat"], w["w_o"][i], res=h, name="mm_o")
    sv["u2"] = _norm_fwd(m, h2, w["ln_ffn"][i:i + 1], "norm_ffn")
    sv["gpre"] = _mm_fed(feed, "mm_gate", i, sv["u2"], w["w_gate"][i], name="mm_gate")
    sv["up"] = _mm_fed(feed, "mm_up", i, sv["u2"], w["w_up"][i], name="mm_up")
    tcf = _tile(m.ff, (512, 256, 128))
    sv["act"] = _stage_fwd(_f_ffn_act, [(sv["gpre"], None, 0), (sv["up"], None, 0)],
                           [(w["conv_f_w"][i], 0), (w["conv_f_b"][i:i + 1], 0)], [], [(m.ff, BF16)],
                           axis=1, tile=tcf, rows=t, name="ffn_act")[0]
    h3 = sv["h3"] = _mm_fed(feed, "mm_down", i, sv["act"], w["w_down"][i], res=h2, name="mm_down")
    sv["u3"] = _norm_fwd(m, h3, w["ln_ple"][i:i + 1], "norm_ple")
    sv["eg"] = _mm_fed(feed, "mm_pgate", i, sv["u3"], w["w_ple_gate"][i], name="mm_pgate")
    sv["ep"] = _mm(p_bf, w["w_ple_proj"][i], name="mm_pproj")
    h4 = _stage_fwd(_f_ple, [(h3, m.d, 0), (sv["eg"], m.d, 0), (sv["ep"], m.d, 0)], [(w["ln_ple_post"][i:i + 1], 0)],
                    [], [(m.d, F32)], axis=0, tile=m.tr, rows=t, name="ple")[0]
    return h4, sv


def _layer_bwd(m, w, i, dh4, sv, p_bf, v_first_zs, dvf_in, pending=None):
    t, d, dl, dr = m.t, m.d, m.dl, m.dr
    g = {}
    deg, dep, g["ln_ple_post"] = _stage_bwd(
        _f_ple, [(sv["h3"], d, 0), (sv["eg"], d, 0), (sv["ep"], d, 0)], [(w["ln_ple_post"][i:i + 1], 0)], [],
        [(dh4, d, 0)], [(1, d, BF16), (2, d, BF16)], axis=0, tile=m.tr, rows=t, name="ple_bwd")
    du3 = _mm(deg, w["w_ple_gate"][i], tb=True, name="mm_pgate_dx")
    g["w_ple_gate"] = _mm(sv["u3"], deg, ta=True, name="mm_pgate_dw")
    g["w_ple_proj"] = _mm(p_bf, dep, ta=True, name="mm_pproj_dw")
    dh3, g["ln_ple"] = _norm_bwd(m, sv["h3"], w["ln_ple"][i:i + 1], du3, dh4, "norm_ple_bwd")
    dh3_bf = dh3.astype(BF16)
    dact = _mm(dh3_bf, w["w_down"][i], tb=True, name="mm_down_dx")
    if pending is None:
        g["w_down"] = _mm(sv["act"], dh3_bf, ta=True, name="mm_down_dw")
    else:
        g["w_down"], gots = _mm(sv["act"], dh3_bf, ta=True, name="mm_down_dw", gather=pending.send_jobs())
        pending.after_send(gots)
    tcf = _tile(m.ff, (512, 256, 128))
    dgpre, dup, g["conv_f_w"], g["conv_f_b"] = _stage_bwd(
        _f_ffn_act, [(sv["gpre"], None, 0), (sv["up"], None, 0)], [(w["conv_f_w"][i], 0), (w["conv_f_b"][i:i + 1], 0)],
        [], [(dact, 0)], [(0, m.ff, BF16), (1, m.ff, BF16)], axis=1, tile=tcf, rows=t, ncols=m.ff, name="ffn_act_bwd")
    du2 = _mm(dgpre, w["w_gate"][i], tb=True, name="mm_gate_dx")
    du2 = _mm(dup, w["w_up"][i], tb=True, res=du2, name="mm_up_dx")
    g["w_gate"] = _mm(sv["u2"], dgpre, ta=True, name="mm_gate_dw")
    g["w_up"] = _mm(sv["u2"], dup, ta=True, name="mm_up_dw")
    dh2, g["ln_ffn"] = _norm_bwd(m, sv["h2"], w["ln_ffn"][i:i + 1], du2, dh3, "norm_ffn_bwd")
    dh2_bf = dh2.astype(BF16)
    dcat = _mm(dh2_bf, w["w_o"][i], tb=True, name="mm_o_dx")
    g["w_o"] = _mm(sv["cat"], dh2_bf, ta=True, name="mm_o_dw")
    tiled, params = _rwkv_post_operands(m, w, i, sv)
    dy, dr_a, dk2_a, dv2_a, dg, g["rwkv_lnx_w"], g["rwkv_lnx_b"], g["rk"] = _stage_bwd(
        _make_f_rwkv_post(m.n), tiled, params, [w["bb"]], [(dcat, dr, dl // dr)], [(j, dr, F32) for j in range(5)],
        axis=0, tile=m.trb, rows=t, name="rwkv_post_bwd")
    heads = [_to_heads(m, a) for a in (sv["zs"][:, :dr], sv["logw"], sv["k2"], sv["v2"], sv["kk"], sv["b"])]
    dwkv, parts = _wkv_bwd(*heads, sv["states"], _to_heads(m, dy), name="wkv_bwd",
                           jobs=pending.exchange_jobs() if pending is not None else ())
    if pending is not None:
        pending.after_exchange(parts)
    dr_b, dlw, dk2_b, dv2_b, dkk, db = [_from_heads(m, a) for a in dwkv]
    tiled, params = _rwkv_pre_operands(m, w, i, sv, v_first_zs, True)
    v_cots = [dv2_a, dv2_b] + ([dvf_in] if dvf_in is not None else [])
    cots = [(c, dr, 0) for c in [dr_a, dr_b, dlw, dk2_a, dk2_b] + v_cots + [dkk, db, dg]]
    ntil = len(tiled)
    dtiled = [(0, dr, F32), (1, dr, F32), (2, dr, F32), (3, m.lz, F32)] + ([(4, dr, F32)] if i > 0 else [])
    res = _stage_bwd(_make_f_rwkv_pre(i > 0, len(v_cots)), tiled, params, [w["bb"]], cots, dtiled,
                     axis=0, tile=m.trb, rows=t, name="rwkv_pre_bwd")
    d_r, d_k, d_v, d_lz = res[:4]
    dvf_out = res[4] if i > 0 else None
    pg = res[ntil:]
    g["rwkv_w0"], g["w2p"], g["rwkv_a0"], g["a2p"], g["g2p"], g["rwkv_kk"], g["rwkv_ka"] = pg[:7]
    if i > 0:
        g["rwkv_v0"], g["v2p"] = pg[7:9]
    dzs = jnp.concatenate([d_r, d_k, d_v, d_lz, jnp.zeros((t, m.zs - 3 * dr - m.lz), F32)], axis=1)
    off = 2 * dl // m.tcs
    dzr, g["mu_pad"] = _stage_bwd(_f_shiftmix, [(sv["z"], None, off)], [(w["mu_pad"][i], off)], [], [(dzs, 0)],
                                  [(0, m.zs, BF16)], axis=1, tile=m.tcs, rows=t, ncols=m.zs, name="shiftmix_bwd")
    dhl, dya, g["lru_norm"] = _stage_bwd(
        _f_lru_out, [(sv["hl"], dl, 0), (sv["z"], dl, 1)], [(w["lru_norm"][i:i + 1], 0)], [], [(dcat, dl, 0)],
        [(0, dl, F32), (1, dl, BF16)], axis=0, tile=m.tr, rows=t, name="lru_out_bwd")
    da, db_in = _lru_scan_bwd(sv["a"], sv["hl"], dhl, name="lru_scan_bwd")
    dxb, g["lru_wx"], g["lru_wa"], g["lru_bx"], g["lru_ba"], g["lru_lambda"] = _stage_bwd(
        _make_f_lru_gates(m.hl), [(sv["xb"], dl, 0)], _lru_gate_params(w, i), [], [(da, dl, 0), (db_in, dl, 0)],
        [(0, dl, F32)], axis=0, tile=m.tr, rows=t, name="lru_gates_bwd")
    tca = _tile(dl, (512, 256, 128))
    dxa, g["conv_a_w"], g["conv_a_b"] = _stage_bwd(
        _f_conv, [(sv["z"], None, 0)], [(w["conv_a_w"][i], 0), (w["conv_a_b"][i:i + 1], 0)], [], [(dxb, 0)],
        [(0, dl, BF16)], axis=1, tile=tca, rows=t, ncols=dl, name="conv_a_bwd")
    dz = jnp.concatenate([dxa, dya, dzr], axis=1)
    du1 = _mm(dz, w["wcat"][i], tb=True, name="mm_in_dx")
    g["wcat"] = _mm(sv["u1"], dz, ta=True, name="mm_in_dw")
    dh, g["ln_mix"] = _norm_bwd(m, sv["h"], w["ln_mix"][i:i + 1], du1, dh2, "norm_mix_bwd")
    return dh, g, dvf_out


def _loss_head(m, h, g, tgt):
    tile, d = m.tr, m.d

    def body(h_ref, g_ref, t_ref, loss_ref, dh_ref, dg_ref):
        def f(hv, gv):
            err = _rms(hv, gv) - t_ref[...]
            return 0.5 * jnp.sum(jnp.mean(err * err, axis=-1))

        val, vjp = jax.vjp(f, h_ref[...], g_ref[...])
        dh, dg = vjp(jnp.ones((), F32))
        dh_ref[...] = dh

        @pl.when(pl.program_id(0) == 0)
        def _():
            dg_ref[...] = jnp.zeros_like(dg_ref)
            loss_ref[...] = jnp.zeros_like(loss_ref)

        dg_ref[...] += dg
        loss_ref[...] += jnp.full(loss_ref.shape, val, F32)

    row = pl.BlockSpec((tile, d), lambda i: (i, 0))
    return pl.pallas_call(
        body, name="loss_head", grid=(m.t // tile,),
        in_specs=[row, pl.BlockSpec((1, d), lambda i: (0, 0)), row],
        out_specs=[pl.BlockSpec((1, LANES_V7X), lambda i: (0, 0)), row, pl.BlockSpec((1, d), lambda i: (0, 0))],
        out_shape=[jax.ShapeDtypeStruct((1, LANES_V7X), F32), jax.ShapeDtypeStruct((m.t, d), F32),
                   jax.ShapeDtypeStruct((1, d), F32)],
        compiler_params=_cparams(("arbitrary",)),
    )(h, g, tgt)


def _local_step(m, w, x, p, tgt, feed=None, reducer=None):
    h = x
    saved = []
    p_bf = p.astype(BF16)
    for i in range(m.nl):
        h, sv = _layer_fwd(m, w, i, h, p_bf[i], saved[0]["zs"] if i > 0 else None, feed)
        saved.append(sv)
    loss_row, dh, d_ln_final = _loss_head(m, h, w["ln_final"], tgt)
    grads = [None] * m.nl
    reductions = [None] * m.nl
    dvf = None
    for i in reversed(range(m.nl)):
        pending = reductions[i + 1] if i + 1 < m.nl else None
        dh, grads[i], dvf_i = _layer_bwd(m, w, i, dh, saved[i], p_bf[i], saved[0]["zs"] if i > 0 else None,
                                         dvf if i == 0 else None, pending)
        if reducer is not None:
            reductions[i] = reducer(i, grads[i])
        if i > 0:
            dvf = dvf_i if dvf is None else dvf + dvf_i
    return loss_row, dh, grads, d_ln_final, reductions


_BIG = ("w_o", "w_gate", "w_up", "w_down", "w_ple_gate", "w_ple_proj")


def _lora_rows(m):
    o1 = m.lw
    o2 = o1 + m.la
    o3 = o2 + m.lg
    return {"w2p": (0, o1), "a2p": (o1, o2), "g2p": (o2, o3), "v2p": (o3, o3 + m.lv)}


def _prepare_weights(m, wf):
    w = {k: v for k, v in wf.items() if k not in _BIG and k not in ("w_in", "w_in_vres")}
    nl = m.nl
    for k in _BIG:
        w[k] = [wf[k][i].astype(BF16) for i in range(nl)] if k in wf else [None] * nl
    w["wcat"] = [None] * nl
    if "w_in" in wf:
        vres = jnp.concatenate([jnp.zeros((1, m.d, m.lv), BF16), wf["w_in_vres"].astype(BF16)], axis=0)
        pad = jnp.zeros((m.d, m.zw - m.din - m.lv), BF16)
        w["wcat"] = [jnp.concatenate([wf["w_in"][i].astype(BF16), vres[i], pad], axis=1) for i in range(nl)]
    mu_v = jnp.concatenate([jnp.zeros((1, m.lv), F32), wf["mu_shift_vres"]], axis=0)
    w["mu_pad"] = jnp.concatenate([jnp.zeros((nl, 2 * m.dl), F32), wf["mu_shift"], mu_v,
                                   jnp.zeros((nl, m.zw - m.din - m.lv), F32)], axis=1)[:, None, :]
    rows = _lora_rows(m)
    for name, src in (("w2p", "rwkv_w2"), ("a2p", "rwkv_a2"), ("g2p", "rwkv_g2"), ("v2p", "rwkv_v2")):
        lo, hi = rows[name]
        a = wf[src]
        w[name] = jnp.concatenate([jnp.zeros((a.shape[0], lo, m.dr), F32), a, jnp.zeros((a.shape[0], m.lz - hi, m.dr), F32)],
                                  axis=1)
    w["rk"] = wf["rwkv_rk"].reshape(nl, 1, m.dr)
    w["ln_final"] = wf["ln_final"].reshape(1, m.d)
    head = jnp.arange(m.dr, dtype=jnp.int32) // m.n
    w["bb"] = (head[:, None] == head[None, :]).astype(BF16)
    return w


def _unpack_grads(m, grads, d_ln_final, with_big=True):
    nl = m.nl
    out = {}

    def stack(key):
        return jnp.stack([grads[i][key] for i in range(nl)], axis=0)

    for k in (_BIG if with_big else ()) + ("conv_a_w", "conv_f_w", "lru_wx", "lru_wa"):
        out[k] = stack(k)
    for k in ("ln_mix", "conv_a_b", "lru_bx", "lru_ba", "lru_lambda", "lru_norm", "rwkv_w0", "rwkv_a0", "rwkv_kk",
              "rwkv_ka", "rwkv_lnx_w", "rwkv_lnx_b", "ln_ffn", "conv_f_b", "ln_ple", "ln_ple_post"):
        out[k] = stack(k)[:, 0, :]
    if with_big:
        out["w_in"] = stack("wcat")[:, :, :m.din]
    out["w_in_vres"] = jnp.stack([grads[i]["wcat"][:, m.din:m.din + m.lv] for i in range(1, nl)], axis=0)
    mu = stack("mu_pad")[:, 0, :]
    out["mu_shift"] = mu[:, :m.nsh]
    out["mu_shift_vres"] = mu[1:, m.nsh:m.nsh + m.lv]
    rows = _lora_rows(m)
    for name, dst in (("w2p", "rwkv_w2"), ("a2p", "rwkv_a2"), ("g2p", "rwkv_g2")):
        lo, hi = rows[name]
        out[dst] = stack(name)[:, lo:hi, :]
    lo, hi = rows["v2p"]
    out["rwkv_v2"] = jnp.stack([grads[i]["v2p"] for i in range(1, nl)], axis=0)[:, lo:hi, :]
    out["rwkv_v0"] = jnp.stack([grads[i]["rwkv_v0"] for i in range(1, nl)], axis=0)[:, 0, :]
    out["rwkv_rk"] = stack("rk").reshape(nl, m.h, m.n)
    out["ln_final"] = d_ln_final.reshape(m.d)
    return out


_ANY = pl.BlockSpec(memory_space=pl.ANY)


def _position():
    return lax.axis_index("x"), lax.axis_index("y"), lax.axis_index("c")


def _other_chips(x, y):
    return [(1 - x, y), (x, 1 - y), (1 - x, 1 - y)]


def _gather_blob(shard):
    rows, wd = shard.shape
    return shard.reshape(2, rows // 2, wd)


_JOB_SEMS = {"gather": 7, "pair_send": 1, "exchange": 3}


def _job_parts(job):
    return job if isinstance(job, tuple) else ("gather", job)


def _job_arrays(jobs):
    return [_job_parts(j)[1] for j in jobs]


def _job_out_shape(job):
    kind, arr = _job_parts(job)
    if kind == "gather":
        return jax.ShapeDtypeStruct((N_XY,) + arr.shape, arr.dtype)
    if kind == "pair_send":
        return jax.ShapeDtypeStruct((arr.shape[0], arr.shape[1] // 2, arr.shape[2]), arr.dtype)
    return jax.ShapeDtypeStruct(arr.shape, arr.dtype)


def _jobs_scratch(jobs):
    out = []
    for j in jobs:
        n = _JOB_SEMS[_job_parts(j)[0]]
        out += [pltpu.SemaphoreType.DMA((n,)), pltpu.SemaphoreType.DMA((n,))]
    return out


def _jobs_start(jobs, in_refs, out_refs, sems):
    for q, j in enumerate(jobs):
        _JOB_START[_job_parts(j)[0]](in_refs[q], out_refs[q], sems[2 * q], sems[2 * q + 1])


def _jobs_finish(jobs, in_refs, out_refs, sems):
    for q, j in enumerate(jobs):
        _JOB_FINISH[_job_parts(j)[0]](in_refs[q], out_refs[q], sems[2 * q], sems[2 * q + 1])


def _run_jobs(jobs, *, name):
    n = len(jobs)

    def body(*refs):
        _jobs_start(jobs, refs[:n], refs[n:2 * n], refs[2 * n:])
        _jobs_finish(jobs, refs[:n], refs[n:2 * n], refs[2 * n:])

    return pl.pallas_call(body, name=name, in_specs=[_ANY] * n, out_specs=[_ANY] * n,
                          out_shape=[_job_out_shape(j) for j in jobs], scratch_shapes=_jobs_scratch(jobs))(*_job_arrays(jobs))


def _gather_copies(in_ref, out_ref, send_sems, recv_sems):
    x, y, c = _position()
    me = 2 * x + y
    sends, hands, ici_in, d2d_in = [], [], [], []
    for k, (px, py) in enumerate(_other_chips(x, y)):
        landed = out_ref.at[2 * px + py, c]
        sends.append(pltpu.make_async_remote_copy(
            src_ref=in_ref.at[c], dst_ref=out_ref.at[me, c], send_sem=send_sems.at[k], recv_sem=recv_sems.at[k],
            device_id=(px, py, c), device_id_type=MESH))
        ici_in.append(pltpu.make_async_remote_copy(
            src_ref=in_ref.at[c], dst_ref=landed, send_sem=send_sems.at[k], recv_sem=recv_sems.at[k],
            device_id=(px, py, c), device_id_type=MESH))
        hands.append(pltpu.make_async_remote_copy(
            src_ref=landed, dst_ref=landed, send_sem=send_sems.at[3 + k], recv_sem=recv_sems.at[3 + k],
            device_id=(x, y, 1 - c), device_id_type=MESH))
        d2d_in.append(pltpu.make_async_remote_copy(
            src_ref=in_ref.at[c], dst_ref=out_ref.at[2 * px + py, 1 - c], send_sem=send_sems.at[3 + k],
            recv_sem=recv_sems.at[3 + k], device_id=(x, y, 1 - c), device_id_type=MESH))
    own = pltpu.make_async_remote_copy(src_ref=in_ref, dst_ref=out_ref.at[me], send_sem=send_sems.at[6],
                                       recv_sem=recv_sems.at[6], device_id=(x, y, 1 - c), device_id_type=MESH)
    sends.append(own)
    d2d_in.append(own)
    return sends, hands, ici_in, d2d_in


def _gather_start(in_ref, out_ref, send_sems, recv_sems):
    for cp in _gather_copies(in_ref, out_ref, send_sems, recv_sems)[0]:
        cp.start()


def _gather_finish(in_ref, out_ref, send_sems, recv_sems):
    sends, hands, ici_in, d2d_in = _gather_copies(in_ref, out_ref, send_sems, recv_sems)
    for arrived, hand in zip(ici_in, hands):
        arrived.wait_recv()
        hand.start()
    for arrived in d2d_in:
        arrived.wait_recv()
    for cp in sends + hands:
        cp.wait_send()


def _pair_send_copy(g_ref, out_ref, send_sems, recv_sems):
    x, y, c = _position()
    half = out_ref.shape[1]
    return pltpu.make_async_remote_copy(src_ref=g_ref.at[:, pl.ds((1 - c) * half, half), :], dst_ref=out_ref,
                                        send_sem=send_sems.at[0], recv_sem=recv_sems.at[0], device_id=(x, y, 1 - c),
                                        device_id_type=MESH)


def _exchange_copies(in_ref, out_ref, send_sems, recv_sems):
    x, y, c = _position()
    me = 2 * x + y
    sends, arrivals = [], []
    for k, (px, py) in enumerate(_other_chips(x, y)):
        sends.append(pltpu.make_async_remote_copy(
            src_ref=in_ref.at[2 * px + py], dst_ref=out_ref.at[me], send_sem=send_sems.at[k], recv_sem=recv_sems.at[k],
            device_id=(px, py, c), device_id_type=MESH))
        arrivals.append(pltpu.make_async_remote_copy(
            src_ref=in_ref.at[me], dst_ref=out_ref.at[2 * px + py], send_sem=send_sems.at[k], recv_sem=recv_sems.at[k],
            device_id=(px, py, c), device_id_type=MESH))
    return sends, arrivals


def _exchange_start(*refs):
    for cp in _exchange_copies(*refs)[0]:
        cp.start()


def _exchange_finish(*refs):
    sends, arrivals = _exchange_copies(*refs)
    for cp in arrivals:
        cp.wait_recv()
    for cp in sends:
        cp.wait_send()


_JOB_START = {"gather": _gather_start, "pair_send": lambda *refs: _pair_send_copy(*refs).start(), "exchange": _exchange_start}
_JOB_FINISH = {"gather": _gather_finish, "pair_send": lambda *refs: _pair_send_copy(*refs).wait(),
               "exchange": _exchange_finish}


def _pair_sum(g, got, pos, *, name):
    nq, r, wd = g.shape
    half = r // 2
    tr = _tile(half, (256, 128, 64, 32, 16, 8))
    nb = half // tr

    def body(c_ref, g_ref, got_ref, o_ref):
        o_ref[...] = (g_ref[...] + got_ref[...]).astype(o_ref.dtype)

    grid_spec = pltpu.PrefetchScalarGridSpec(
        num_scalar_prefetch=1, grid=(nq, nb),
        in_specs=[pl.BlockSpec((1, tr, wd), lambda q, j, c_ref: (q, c_ref[0] * nb + j, 0)),
                  pl.BlockSpec((1, tr, wd), lambda q, j, c_ref: (q, j, 0))],
        out_specs=pl.BlockSpec((1, tr, wd), lambda q, j, c_ref: (q, j, 0)))
    return pl.pallas_call(body, name=name, grid_spec=grid_spec, out_shape=jax.ShapeDtypeStruct((nq, half, wd), BF16),
                          compiler_params=_cparams(("arbitrary", "arbitrary")))(pos[0], g, got)


def _chip_sum(parts, pb, pos, *, name):
    nq, half, wd = parts.shape
    tr = _tile(half, (256, 128, 64, 32, 16, 8))
    nb = half // tr

    def body(c_ref, x_ref, y_ref, p_ref, own_ref, o_ref):
        chip = 2 * x_ref[0] + y_ref[0]
        own = own_ref[0].astype(F32)
        acc = None
        for q in range(nq):
            term = jnp.where(chip == q, own, p_ref[q].astype(F32))
            acc = term if acc is None else acc + term
        o_ref[...] = acc

    grid_spec = pltpu.PrefetchScalarGridSpec(
        num_scalar_prefetch=3, grid=(nb,),
        in_specs=[pl.BlockSpec((nq, tr, wd), lambda j, c_ref, x_ref, y_ref: (0, j, 0)),
                  pl.BlockSpec((1, tr, wd), lambda j, c_ref, x_ref, y_ref: (2 * x_ref[0] + y_ref[0], j, 0))],
        out_specs=pl.BlockSpec((tr, wd), lambda j, c_ref, x_ref, y_ref: (c_ref[0] * nb + j, 0)))
    return pl.pallas_call(body, name=name, grid_spec=grid_spec, out_shape=jax.ShapeDtypeStruct((2 * half, wd), F32),
                          compiler_params=_cparams(("arbitrary",)))(*pos, parts, pb)


def _pair_gather(full, *, name):
    r, wd = full.shape
    half = r // 2

    def body(in_ref, out_ref, send_sem, recv_sem):
        x, y, c = _position()
        mine = out_ref.at[pl.ds(c * half, half), :]
        cp = pltpu.make_async_remote_copy(src_ref=mine, dst_ref=mine, send_sem=send_sem, recv_sem=recv_sem,
                                          device_id=(x, y, 1 - c), device_id_type=MESH)
        cp.start()
        pltpu.make_async_remote_copy(src_ref=mine, dst_ref=out_ref.at[pl.ds((1 - c) * half, half), :], send_sem=send_sem,
                                     recv_sem=recv_sem, device_id=(x, y, 1 - c), device_id_type=MESH).wait_recv()
        cp.wait_send()

    return pl.pallas_call(
        body, name=name, in_specs=[_ANY], out_specs=_ANY, out_shape=jax.ShapeDtypeStruct(full.shape, full.dtype),
        input_output_aliases={0: 0}, scratch_shapes=[pltpu.SemaphoreType.DMA(()), pltpu.SemaphoreType.DMA(())],
    )(full)


class _GradReduce:
    def __init__(self, slabs, pos, tag):
        self.names, self.slabs, self.pos, self.tag = list(slabs), [slabs[k] for k in slabs], pos, tag
        self.pb = self.out = None

    def send_jobs(self):
        return [("pair_send", g) for g in self.slabs]

    def after_send(self, gots):
        self.pb = [_pair_sum(g, got, self.pos, name=f"rs_pair_sum_{k}_{self.tag}")
                   for k, g, got in zip(self.names, self.slabs, gots)]

    def exchange_jobs(self):
        return [("exchange", pb) for pb in self.pb]

    def after_exchange(self, parts):
        full = [_chip_sum(pt, pb, self.pos, name=f"rs_chip_sum_{k}_{self.tag}") for k, pt, pb in zip(self.names, parts, self.pb)]
        self.out = {k: _pair_gather(f, name=f"rs_pair_gather_{k}_{self.tag}") for k, f in zip(self.names, full)}

    def run(self):
        if self.pb is None:
            self.after_send(_run_jobs(self.send_jobs(), name="rs_pair_send_" + self.tag))
        if self.out is None:
            self.after_exchange(_run_jobs(self.exchange_jobs(), name="rs_exchange_" + self.tag))
        return self.out


def _all_reduce_small(vec, *, name):
    r, wd = vec.shape

    def body(in_ref, out_ref, slots, send_sems, recv_sems):
        x, y, c = _position()
        me = 4 * x + 2 * y + c
        flips = [(fx, fy, fc) for fx in (0, 1) for fy in (0, 1) for fc in (0, 1) if fx + fy + fc]
        peers = [(1 - x if fx else x, 1 - y if fy else y, 1 - c if fc else c) for fx, fy, fc in flips]
        sends = []
        for k, peer in enumerate(peers):
            cp = pltpu.make_async_remote_copy(src_ref=in_ref, dst_ref=slots.at[me], send_sem=send_sems.at[k],
                                              recv_sem=recv_sems.at[k], device_id=peer, device_id_type=MESH)
            cp.start()
            sends.append(cp)
        slots[me] = in_ref[...]
        for k, (px, py, pc) in enumerate(peers):
            pltpu.make_async_remote_copy(src_ref=in_ref, dst_ref=slots.at[4 * px + 2 * py + pc], send_sem=send_sems.at[k],
                                         recv_sem=recv_sems.at[k], device_id=(px, py, pc), device_id_type=MESH).wait_recv()
        for cp in sends:
            cp.wait_send()
        acc = slots[0]
        for q in range(1, N_DEV):
            acc = acc + slots[q]
        out_ref[...] = acc

    vm = pl.BlockSpec(memory_space=pltpu.VMEM)
    return pl.pallas_call(
        body, name=name, in_specs=[vm], out_specs=vm, out_shape=jax.ShapeDtypeStruct((r, wd), F32),
        scratch_shapes=[pltpu.VMEM((N_DEV, r, wd), F32), pltpu.SemaphoreType.DMA((N_DEV - 1,)),
                        pltpu.SemaphoreType.DMA((N_DEV - 1,))],
        compiler_params=_cparams(),
    )(vec)


def _adamw(w, g, m, v, *, name):
    r, wd = w.shape
    tr = _tile(r, (256, 128, 64, 32, 16, 8))

    def body(w_ref, g_ref, m_ref, v_ref, d_ref, m_out, v_out):
        gv = g_ref[...]
        m_new = ADAM_B1 * m_ref[...] + (1.0 - ADAM_B1) * gv
        v_new = ADAM_B2 * v_ref[...] + (1.0 - ADAM_B2) * (gv * gv)
        m_hat = m_new / (1.0 - ADAM_B1 ** ADAM_STEP)
        v_hat = v_new / (1.0 - ADAM_B2 ** ADAM_STEP)
        d_ref[...] = -ADAM_LR * (m_hat / (jnp.sqrt(v_hat) + ADAM_EPS) + ADAM_WD * w_ref[...])
        m_out[...] = m_new
        v_out[...] = v_new

    spec = pl.BlockSpec((tr, wd), lambda j: (j, 0))
    return pl.pallas_call(body, name=name, grid=(r // tr,), in_specs=[spec] * 4, out_specs=[spec] * 3,
                          out_shape=[jax.ShapeDtypeStruct((r, wd), F32)] * 3, compiler_params=_cparams(("arbitrary",)))(w, g, m, v)


_WEIGHTS = ("ln_mix", "w_in", "w_in_vres", "mu_shift", "mu_shift_vres", "conv_a_w", "conv_a_b", "lru_wx", "lru_bx", "lru_wa",
            "lru_ba", "lru_lambda", "lru_norm", "rwkv_w0", "rwkv_w2", "rwkv_a0", "rwkv_a2", "rwkv_v0", "rwkv_v2", "rwkv_g2",
            "rwkv_kk", "rwkv_ka", "rwkv_rk", "rwkv_lnx_w", "rwkv_lnx_b", "w_o", "ln_ffn", "w_gate", "w_up", "conv_f_w",
            "conv_f_b", "w_down", "ln_ple", "w_ple_gate", "w_ple_proj", "ln_ple_post", "ln_final")
_SHARD_AXIS = {"w_in": 2, "w_in_vres": 1, "conv_a_w": 2, "lru_wx": 2, "lru_wa": 2, "rwkv_w2": 2, "rwkv_a2": 2, "rwkv_v2": 2,
               "rwkv_g2": 2, "w_o": 1, "w_gate": 2, "w_up": 2, "conv_f_w": 2, "w_down": 1, "w_ple_gate": 1, "w_ple_proj": 2}
_BIG_SHARDED = ("w_in",) + _BIG
_SMALL_SHARDED = tuple(k for k in _WEIGHTS if k in _SHARD_AXIS and k not in _BIG_SHARDED)
_REPLICATED = tuple(k for k in _WEIGHTS if k not in _SHARD_AXIS)
PACK_WIDTH = 512


def _to_shards(g, axis):
    n = g.shape[axis] // N_XY
    return jnp.moveaxis(g.reshape(g.shape[:axis] + (N_XY, n) + g.shape[axis + 1:]), axis, 0)


def _from_shards(s, axis):
    s = jnp.moveaxis(s, 0, axis)
    return s.reshape(s.shape[:axis] + (N_XY * s.shape[axis + 1],) + s.shape[axis + 2:])


def _pack(arrs, lead, width, row_mult):
    lead_shape = arrs[0].shape[:lead]
    flat = jnp.concatenate([a.reshape(lead_shape + (-1,)) for a in arrs], axis=-1)
    n = flat.shape[-1]
    total = _round_up(n, width * row_mult)
    flat = jnp.pad(flat, [(0, 0)] * lead + [(0, total - n)])
    return flat.reshape(lead_shape + (total // width, width))


def _unpack(packed, shapes):
    flat = packed.reshape(-1)
    out, o = [], 0
    for s in shapes:
        n = 1
        for dim in s:
            n *= dim
        out.append(flat[o:o + n].reshape(s))
        o += n
    return out


def _as2d(a):
    return a.reshape(-1, a.shape[-1])


def kernel(x, p, ln_mix, w_in, w_in_vres, mu_shift, mu_shift_vres, conv_a_w, conv_a_b, lru_wx, lru_bx, lru_wa, lru_ba, lru_lambda, lru_norm, rwkv_w0, rwkv_w2, rwkv_a0, rwkv_a2, rwkv_v0, rwkv_v2, rwkv_g2, rwkv_kk, rwkv_ka, rwkv_rk, rwkv_lnx_w, rwkv_lnx_b, w_o, ln_ffn, w_gate, w_up, conv_f_w, conv_f_b, w_down, ln_ple, w_ple_gate, w_ple_proj, ln_ple_post, ln_final, loss_target, m_ln_mix, m_w_in, m_w_in_vres, m_mu_shift, m_mu_shift_vres, m_conv_a_w, m_conv_a_b, m_lru_wx, m_lru_bx, m_lru_wa, m_lru_ba, m_lru_lambda, m_lru_norm, m_rwkv_w0, m_rwkv_w2, m_rwkv_a0, m_rwkv_a2, m_rwkv_v0, m_rwkv_v2, m_rwkv_g2, m_rwkv_kk, m_rwkv_ka, m_rwkv_rk, m_rwkv_lnx_w, m_rwkv_lnx_b, m_w_o, m_ln_ffn, m_w_gate, m_w_up, m_conv_f_w, m_conv_f_b, m_w_down, m_ln_ple, m_w_ple_gate, m_w_ple_proj, m_ln_ple_post, m_ln_final, v_ln_mix, v_w_in, v_w_in_vres, v_mu_shift, v_mu_shift_vres, v_conv_a_w, v_conv_a_b, v_lru_wx, v_lru_bx, v_lru_wa, v_lru_ba, v_lru_lambda, v_lru_norm, v_rwkv_w0, v_rwkv_w2, v_rwkv_a0, v_rwkv_a2, v_rwkv_v0, v_rwkv_v2, v_rwkv_g2, v_rwkv_kk, v_rwkv_ka, v_rwkv_rk, v_rwkv_lnx_w, v_rwkv_lnx_b, v_w_o, v_ln_ffn, v_w_gate, v_w_up, v_conv_f_w, v_conv_f_b, v_w_down, v_ln_ple, v_w_ple_gate, v_w_ple_proj, v_ln_ple_post, v_ln_final):
    a = dict(locals())
    x2, p, tgt = a["x"][0], a["p"][:, 0], a["loss_target"][0]
    pos = tuple(lax.axis_index(ax).astype(jnp.int32).reshape(1) for ax in ("c", "x", "y"))

    wf = {k: a[k] for k in _REPLICATED}
    small_shapes = [a[k].shape for k in _SMALL_SHARDED]
    shards = {k: [a[k][i].astype(BF16) for i in range(a[k].shape[0])] for k in _BIG_SHARDED}
    packed = _pack([a[k] for k in _SMALL_SHARDED], 0, PACK_WIDTH, 16)
    got_small, got_w_in = _run_jobs([_gather_blob(packed), _gather_blob(shards["w_in"][0])], name="ag_first")
    got_small = got_small.reshape((N_XY,) + packed.shape)
    pieces = [_unpack(got_small[q], small_shapes) for q in range(N_XY)]
    for j, k in enumerate(_SMALL_SHARDED):
        wf[k] = _from_shards(jnp.stack([pieces[q][j] for q in range(N_XY)], axis=0), _SHARD_AXIS[k])

    m = _make_dims(x2, p, wf)
    w = _prepare_weights(m, wf)
    feed = _WeightFeed(m, w, shards, wf["w_in_vres"].astype(BF16))
    feed.arrive([("w_in", 0)], [got_w_in])
    def reducer(i, g):
        full = {k: (g["wcat"][:, :m.din] if k == "w_in" else g[k]) for k in _BIG_SHARDED}
        return _GradReduce({k: _to_shards(full[k], _SHARD_AXIS[k] - 1) for k in _BIG_SHARDED}, pos, str(i))

    loss_row, dx, grads, d_ln_final, reductions = _local_step(m, w, x2, p, tgt, feed, reducer)
    gfull = _unpack_grads(m, grads, d_ln_final, with_big=False)
    loss = lax.psum(loss_row[0, 0], ("x", "y", "c"))

    reduced = [r.run() for r in reductions]
    gred = {k: jnp.stack([reduced[i][k] for i in range(m.nl)], axis=0).reshape(a[k].shape) for k in _BIG_SHARDED}
    gs = _pack([_to_shards(gfull[k], _SHARD_AXIS[k]) for k in _SMALL_SHARDED], 1, PACK_WIDTH, 32)
    g_small = _GradReduce({"small": gs}, pos, "small").run()["small"]
    rep_shapes = [a[k].shape for k in _REPLICATED]
    g_rep = _all_reduce_small(_pack([gfull[k] for k in _REPLICATED], 0, LANES_V7X, 8), name="ar_replicated")

    delta, new_m, new_v = {}, {}, {}
    for k in _BIG_SHARDED:
        res = _adamw(_as2d(a[k]), _as2d(gred[k]), _as2d(a["m_" + k]), _as2d(a["v_" + k]), name="adamw_" + k)
        delta[k], new_m[k], new_v[k] = (r.reshape(a[k].shape) for r in res)
    for names, shapes, g_packed, width, mult, tag in ((_SMALL_SHARDED, small_shapes, g_small, PACK_WIDTH, 32, "small"),
                                                      (_REPLICATED, rep_shapes, g_rep, LANES_V7X, 8, "replicated")):
        packs = [_pack([a[pre + k] for k in names], 0, width, mult) for pre in ("", "m_", "v_")]
        res = _adamw(packs[0], g_packed, packs[1], packs[2], name="adamw_" + tag)
        for dst, r in zip((gred, delta, new_m, new_v), [g_packed] + list(res)):
            dst.update(zip(names, _unpack(r, shapes)))
    return (loss, dx[None], *[gred[k] for k in _WEIGHTS], *[delta[k] for k in _WEIGHTS],
            *[new_m[k] for k in _WEIGHTS], *[new_v[k] for k in _WEIGHTS])
```

```python
import functools

import jax
import jax.numpy as jnp
from jax import lax
from jax.experimental import pallas as pl
from jax.experimental.pallas import tpu as pltpu

F32 = jnp.float32
BF16 = jnp.bfloat16
HIGHEST = lax.Precision.HIGHEST
MESH = pl.DeviceIdType.MESH

RMS_EPS = 1e-6
LNX_EPS = 64e-5
LRU_C = 8.0
ADAM_LR = 0.001
ADAM_B1 = 0.9
ADAM_B2 = 0.999
ADAM_EPS = 1e-08
ADAM_WD = 0.01
ADAM_STEP = 10

LANES_V7X = 128
VMEM_LIMIT_V7X = 60 * 1024 * 1024
WKV_CHUNK = 16
N_XY = 4
N_DEV = 8


def _cparams(sem=None, **kw):
    if sem is not None:
        kw["dimension_semantics"] = sem
    return pltpu.CompilerParams(vmem_limit_bytes=VMEM_LIMIT_V7X, **kw)


def _tile(dim, prefs):
    for t in prefs:
        if dim % t == 0:
            return t
    return dim


def _round_up(n, m):
    return (n + m - 1) // m * m


MM_MAX_TK = 2816


def _tile_k(kdim):
    best = None
    for t in range(LANES_V7X, min(kdim, MM_MAX_TK) + 1, LANES_V7X):
        if kdim % t == 0:
            best = t
    return best or kdim


def _mm(a, b, *, ta=False, tb=False, res=None, out_dtype=F32, name, gather=()):
    if ta:
        kdim, m = a.shape
    else:
        m, kdim = a.shape
    n = b.shape[0] if tb else b.shape[1]
    assert (b.shape[1] if tb else b.shape[0]) == kdim
    tk = _tile_k(kdim)
    tm = _tile(m, (2048, 1024, 512, 256, 128) if tk <= 2048 else (1024, 512, 256, 128))
    tn = _tile(n, (512, 256, 128))
    nk = kdim // tk
    ni, nj = m // tm, n // tn
    a_spec = pl.BlockSpec((tk, tm), lambda i, j, k: (k, i)) if ta else pl.BlockSpec((tm, tk), lambda i, j, k: (i, k))
    b_spec = pl.BlockSpec((tn, tk), lambda i, j, k: (j, k)) if tb else pl.BlockSpec((tk, tn), lambda i, j, k: (k, j))
    o_spec = pl.BlockSpec((tm, tn), lambda i, j, k: (i, j))
    dn = (((0 if ta else 1,), (1 if tb else 0,)), ((), ()))
    has_res = res is not None
    ng = len(gather)
    nin = 2 + has_res

    def body(*refs):
        a_ref, b_ref = refs[:2]
        r_ref = refs[2] if has_res else None
        g_in, o_ref, g_out = refs[nin:nin + ng], refs[nin + ng], refs[nin + ng + 1:nin + 2 * ng + 1]
        scratch = refs[nin + 2 * ng + 1:]
        acc_ref = scratch[0] if nk > 1 else None
        g_sems = scratch[1 if nk > 1 else 0:]
        i, j, k = pl.program_id(0), pl.program_id(1), pl.program_id(2)

        if ng:
            @pl.when((i == 0) & (j == 0) & (k == 0))
            def _():
                _jobs_start(gather, g_in, g_out, g_sems)

        def finish(acc):
            if has_res:
                acc = acc + r_ref[...].astype(F32)
            o_ref[...] = acc.astype(out_dtype)

        prod = lax.dot_general(a_ref[...], b_ref[...], dn, preferred_element_type=F32)
        if nk == 1:
            finish(prod)
        else:
            @pl.when(k == 0)
            def _():
                acc_ref[...] = prod

            @pl.when(k > 0)
            def _():
                acc_ref[...] += prod

            @pl.when(k == nk - 1)
            def _():
                finish(acc_ref[...])

        if ng:
            @pl.when((i == ni - 1) & (j == nj - 1) & (k == nk - 1))
            def _():
                _jobs_finish(gather, g_in, g_out, g_sems)

    ins = [a, b] + ([res] if has_res else []) + _job_arrays(gather)
    in_specs = [a_spec, b_spec] + ([o_spec] if has_res else []) + [_ANY] * ng
    scratch = ([pltpu.VMEM((tm, tn), F32)] if nk > 1 else []) + _jobs_scratch(gather)
    sem = ("arbitrary",) * 3 if ng else ("parallel", "parallel", "arbitrary")
    out = pl.pallas_call(
        body, name=name, grid=(ni, nj, nk), in_specs=in_specs, out_specs=[o_spec] + [_ANY] * ng,
        out_shape=[jax.ShapeDtypeStruct((m, n), out_dtype)] + [_job_out_shape(g) for g in gather],
        scratch_shapes=scratch, compiler_params=_cparams(sem),
    )(*ins)
    return (out[0], list(out[1:])) if ng else out[0]


def _stage_specs(axis, tile, tiled, params, consts, rows):
    specs = []
    for arr, width, cblk in tiled:
        if axis == 0:
            specs.append(pl.BlockSpec((tile, width), functools.partial(lambda i, c: (i, c), c=cblk)))
        else:
            specs.append(pl.BlockSpec((rows, tile), functools.partial(lambda i, c: (0, i + c), c=cblk)))
    for arr, cblk in params:
        if axis == 0:
            specs.append(pl.BlockSpec(arr.shape, functools.partial(lambda i, nd: (0,) * nd, nd=arr.ndim)))
        else:
            specs.append(pl.BlockSpec((arr.shape[0], tile), functools.partial(lambda i, c: (0, i + c), c=cblk)))
    for arr in consts:
        specs.append(pl.BlockSpec(arr.shape, functools.partial(lambda i, nd: (0,) * nd, nd=arr.ndim)))
    return specs


def _stage_fwd(fn, tiled, params, consts, outs, *, axis, tile, rows, name):
    nt, npar, nc = len(tiled), len(params), len(consts)
    ntiles = (rows // tile) if axis == 0 else (outs[0][0] // tile)

    def body(*refs):
        ins = refs[: nt + npar + nc]
        orefs = refs[nt + npar + nc:]
        vals = [r[...].astype(F32) for r in ins[: nt + npar]] + [r[...] for r in ins[nt + npar:]]
        ctx = pl.program_id(0) * tile
        res = fn(ctx, *vals)
        for o_ref, o in zip(orefs, res):
            o_ref[...] = o.astype(o_ref.dtype)

    if axis == 0:
        out_specs = [pl.BlockSpec((tile, w), lambda i: (i, 0)) for w, _ in outs]
    else:
        out_specs = [pl.BlockSpec((rows, tile), lambda i: (0, i)) for w, _ in outs]
    res = pl.pallas_call(
        body, name=name, grid=(ntiles,),
        in_specs=_stage_specs(axis, tile, tiled, params, consts, rows), out_specs=out_specs,
        out_shape=[jax.ShapeDtypeStruct((rows, w), dt) for w, dt in outs],
        compiler_params=_cparams(("arbitrary",)),
    )(*[t[0] for t in tiled], *[p[0] for p in params], *consts)
    return res


def _stage_bwd(fn, tiled, params, consts, cots, dtiled, *, axis, tile, rows, name, ncols=None):
    nt, npar, nc, nco = len(tiled), len(params), len(consts), len(cots)
    ntiles = (rows // tile) if axis == 0 else (ncols // tile)
    didx = [d[0] for d in dtiled]

    def body(*refs):
        ins = refs[: nt + npar + nc]
        crefs = refs[nt + npar + nc: nt + npar + nc + nco]
        orefs = refs[nt + npar + nc + nco:]
        vals = [r[...].astype(F32) for r in ins[: nt + npar]] + [r[...] for r in ins[nt + npar:]]
        ctx = pl.program_id(0) * tile

        def g(*dv):
            full = list(vals)
            for j, ix in enumerate(didx):
                full[ix] = dv[j]
            for j in range(npar):
                full[nt + j] = dv[len(didx) + j]
            return tuple(fn(ctx, *full))

        prim = [vals[ix] for ix in didx] + [vals[nt + j] for j in range(npar)]
        _, vjp = jax.vjp(g, *prim)
        grads = vjp(tuple(c[...].astype(F32) for c in crefs))
        for j in range(len(didx)):
            orefs[j][...] = grads[j].astype(orefs[j].dtype)
        for j in range(npar):
            o_ref = orefs[len(didx) + j]
            gp = grads[len(didx) + j]
            if axis == 0:
                @pl.when(pl.program_id(0) == 0)
                def _(o_ref=o_ref):
                    o_ref[...] = jnp.zeros_like(o_ref)

                o_ref[...] += gp
            else:
                o_ref[...] = gp

    if axis == 0:
        cot_specs = [pl.BlockSpec((tile, w), functools.partial(lambda i, c: (i, c), c=cb)) for _, w, cb in cots]
        out_specs = [pl.BlockSpec((tile, w), lambda i: (i, 0)) for _, w, _ in dtiled]
        out_specs += [pl.BlockSpec(p.shape, functools.partial(lambda i, nd: (0,) * nd, nd=p.ndim)) for p, _ in params]
        out_shape = [jax.ShapeDtypeStruct((rows, w), dt) for _, w, dt in dtiled]
        out_shape += [jax.ShapeDtypeStruct(p.shape, F32) for p, _ in params]
    else:
        cot_specs = [pl.BlockSpec((rows, tile), functools.partial(lambda i, c: (0, i + c), c=cb)) for _, cb in cots]
        out_specs = [pl.BlockSpec((rows, tile), lambda i: (0, i)) for _ in dtiled]
        out_specs += [pl.BlockSpec((p.shape[0], tile), lambda i: (0, i)) for p, _ in params]
        out_shape = [jax.ShapeDtypeStruct((rows, w), dt) for _, w, dt in dtiled]
        out_shape += [jax.ShapeDtypeStruct((p.shape[0], ncols), F32) for p, _ in params]
    return pl.pallas_call(
        body, name=name, grid=(ntiles,),
        in_specs=_stage_specs(axis, tile, tiled, params, consts, rows) + cot_specs, out_specs=out_specs,
        out_shape=out_shape, compiler_params=_cparams(("arbitrary",)),
    )(*[t[0] for t in tiled], *[p[0] for p in params], *consts, *[c[0] for c in cots])


def _rms(x, g):
    return x * lax.rsqrt(jnp.mean(x * x, axis=-1, keepdims=True) + RMS_EPS) * g


def _row_mask(x, k, first):
    t = lax.broadcasted_iota(jnp.int32, x.shape, 0)
    keep = (t >= k) if first else (t < x.shape[0] - k)
    return jnp.where(keep, x, 0.0)


@functools.partial(jax.custom_vjp, nondiff_argnums=(1,))
def _shift_down(x, k):
    return _row_mask(pltpu.roll(x, k, 0), k, True)


def _shift_down_fwd(x, k):
    return _shift_down(x, k), None


def _shift_down_bwd(k, _, g):
    return (_row_mask(pltpu.roll(g, g.shape[0] - k, 0), k, False),)


_shift_down.defvjp(_shift_down_fwd, _shift_down_bwd)


def _dwconv(x, w, b):
    kw = w.shape[0]
    out = x * w[kw - 1:kw] + b
    for j in range(kw - 1):
        out = out + _shift_down(x, kw - 1 - j) * w[j:j + 1]
    return out


def _f_norm(ctx, x, g):
    return (_rms(x, g),)


def _f_norm_res(ctx, x, g):
    return (_rms(x, g), x)


def _f_shiftmix(ctx, z, mu):
    return (z + (_shift_down(z, 1) - z) * mu,)


def _f_conv(ctx, x, w, b):
    return (_dwconv(x, w, b),)


def _f_ffn_act(ctx, gpre, up, w, b):
    return (jax.nn.gelu(_dwconv(gpre, w, b)) * up,)


def _make_f_lru_gates(heads):
    def fn(ctx, xb, wx, wa, bx, ba, lam):
        blk = xb.shape[1] // heads
        px, pa = [], []
        for h in range(heads):
            xh = xb[:, h * blk:(h + 1) * blk]
            px.append(jnp.dot(xh, wx[h], preferred_element_type=F32))
            pa.append(jnp.dot(xh, wa[h], preferred_element_type=F32))
        px = px[0] if heads == 1 else jnp.concatenate(px, axis=1)
        pa = pa[0] if heads == 1 else jnp.concatenate(pa, axis=1)
        gate_x = jax.nn.sigmoid(px + bx)
        gate_a = jax.nn.sigmoid(pa + ba)
        log_a = -LRU_C * gate_a * jax.nn.softplus(-lam)
        a = jnp.exp(log_a)
        mult = jnp.sqrt(1.0 - jnp.exp(2.0 * log_a))
        t = ctx + lax.broadcasted_iota(jnp.int32, xb.shape, 0)
        mult = jnp.where(t == 0, 1.0, mult)
        return a, xb * gate_x * mult

    return fn


def _f_lru_out(ctx, hl, ya, g):
    return (_rms(hl * jax.nn.gelu(ya), g),)


def _headsum_3pass(x, bb):
    hi = x.astype(BF16)
    r1 = x - hi.astype(F32)
    mid = r1.astype(BF16)
    lo = (r1 - mid.astype(F32)).astype(BF16)
    return (jnp.dot(hi, bb, preferred_element_type=F32) + jnp.dot(mid, bb, preferred_element_type=F32)
            + jnp.dot(lo, bb, preferred_element_type=F32))


@jax.custom_vjp
def _headsum(x, bb):
    return _headsum_3pass(x, bb)


def _headsum_fwd(x, bb):
    return _headsum_3pass(x, bb), bb


def _headsum_bwd(bb, g):
    return _headsum_3pass(g, bb), None


_headsum.defvjp(_headsum_fwd, _headsum_bwd)


def _make_f_rwkv_pre(has_vres, v_uses=0):
    def fn(ctx, *args):
        if v_uses:
            r, args = args[0], args[1:]
        if has_vres:
            k, v, lz, vf, w0, w2, a0, a2, g2, kkw, ka, v0, v2, bb = args
        else:
            k, v, lz, w0, w2, a0, a2, g2, kkw, ka, bb = args
        w_log = -jax.nn.softplus(-(w0 + jnp.dot(jnp.tanh(lz), w2, preferred_element_type=F32))) - 0.5
        logw = -jnp.exp(w_log)
        a = jax.nn.sigmoid(a0 + jnp.dot(lz, a2, preferred_element_type=F32))
        g = jnp.dot(jax.nn.sigmoid(lz), g2, preferred_element_type=F32)
        if has_vres:
            v = v + (vf - v) * jax.nn.sigmoid(v0 + jnp.dot(lz, v2, preferred_element_type=F32))
        xk = k * kkw
        kk = xk / jnp.maximum(jnp.sqrt(_headsum(xk * xk, bb)), 1e-12)
        k2 = k * (1.0 + (a - 1.0) * ka)
        if v_uses:
            return (r, r, logw, k2, k2) + (v,) * v_uses + (kk, kk * a, g)
        return logw, k2, v, kk, kk * a, g

    return fn


def _make_f_rwkv_post(head_size):
    def fn(ctx, y, r, k2, v2, g, lnw, lnb, rk, bb):
        mean = _headsum(y, bb) / head_size
        d = y - mean
        var = _headsum(d * d, bb) / head_size
        yn = d * lax.rsqrt(var + LNX_EPS) * lnw + lnb
        bonus = _headsum(r * k2 * rk, bb) * v2
        return ((yn + bonus) * g,)

    return fn


def _f_ple(ctx, h, eg, ep, g):
    return (h + _rms(jax.nn.sigmoid(eg) * ep, g),)


def _lru_scan(a, b, *, name):
    rows, cols = a.shape
    tc = _tile(cols, (512, 256, 128))

    def body(a_ref, b_ref, h_ref):
        def step(t, carry):
            h = a_ref[pl.ds(t, 1), :] * carry + b_ref[pl.ds(t, 1), :]
            h_ref[pl.ds(t, 1), :] = h
            return h

        lax.fori_loop(0, rows, step, jnp.zeros((1, tc), F32), unroll=8)

    spec = pl.BlockSpec((rows, tc), lambda j: (0, j))
    return pl.pallas_call(body, name=name, grid=(cols // tc,), in_specs=[spec, spec], out_specs=spec,
                          out_shape=jax.ShapeDtypeStruct((rows, cols), F32), compiler_params=_cparams(("arbitrary",)))(a, b)


def _lru_scan_bwd(a, h, dh, *, name):
    rows, cols = a.shape
    tc = _tile(cols, (512, 256, 128))

    def body(a_ref, h_ref, dh_ref, da_ref, db_ref):
        def step(i, carry):
            t = rows - 1 - i
            g = dh_ref[pl.ds(t, 1), :] + carry
            db_ref[pl.ds(t, 1), :] = g
            hp = h_ref[pl.ds(jnp.maximum(t - 1, 0), 1), :]
            da_ref[pl.ds(t, 1), :] = jnp.where(t > 0, g * hp, 0.0)
            return a_ref[pl.ds(t, 1), :] * g

        lax.fori_loop(0, rows, step, jnp.zeros((1, tc), F32), unroll=8)

    spec = pl.BlockSpec((rows, tc), lambda j: (0, j))
    return pl.pallas_call(body, name=name, grid=(cols // tc,), in_specs=[spec] * 3, out_specs=[spec] * 2,
                          out_shape=[jax.ShapeDtypeStruct((rows, cols), F32)] * 2,
                          compiler_params=_cparams(("arbitrary",)))(a, h, dh)


def _split_bf16(x):
    hi = x.astype(BF16)
    return hi, (x - hi.astype(F32)).astype(BF16)


def _dot3_passes(a, b, ca, cb):
    dn = (((ca,), (cb,)), ((), ()))
    ah, al = _split_bf16(a)
    bh, bl = _split_bf16(b)
    return (lax.dot_general(ah, bh, dn, preferred_element_type=F32) + lax.dot_general(al, bh, dn, preferred_element_type=F32)
            + lax.dot_general(ah, bl, dn, preferred_element_type=F32))


@functools.partial(jax.custom_vjp, nondiff_argnums=(2, 3))
def _dot3(a, b, ca, cb):
    return _dot3_passes(a, b, ca, cb)


def _dot3_fwd(a, b, ca, cb):
    return _dot3_passes(a, b, ca, cb), (a, b)


def _dot3_bwd(ca, cb, res, g):
    a, b = res
    fa, fb = 1 - ca, 1 - cb
    da = _dot3_passes(g, b, 1, fb) if ca == 1 else _dot3_passes(b, g, fb, 1)
    db = _dot3_passes(a, g, fa, 0) if cb == 0 else _dot3_passes(g, a, 0, fa)
    return da, db


_dot3.defvjp(_dot3_fwd, _dot3_bwd)


def _each(f, *lists):
    return [f(*t) for t in zip(*lists)]


def _wkv_local(r, lw, k, v, kk, b):
    c, n = r[0].shape
    row = lax.broadcasted_iota(jnp.int32, (c, c), 0)
    col = lax.broadcasted_iota(jnp.int32, (c, c), 1)
    incl = (row >= col).astype(F32)
    strict = (row > col).astype(F32)
    eye = lax.broadcasted_iota(jnp.int32, (n, n), 0) == lax.broadcasted_iota(jnp.int32, (n, n), 1)
    cl = _each(lambda x: _dot3(incl, x, 1, 0), lw)
    w_t = _each(jnp.exp, cl)
    inv_w = _each(lambda x: jnp.exp(-x), cl)
    kk_s = _each(lambda x, y, z: x * jnp.exp(y - z), kk, cl, lw)
    b_s = _each(jnp.multiply, b, inv_w)
    k_s = _each(jnp.multiply, k, inv_w)
    r_s = _each(jnp.multiply, r, w_t)
    q = _each(lambda x, y: jnp.concatenate([x, y], axis=0), kk_s, r_s)
    qb = _each(lambda x, y: _dot3(x, y, 1, 1), q, b_s)
    qk = _each(lambda x, y: _dot3(x, y, 1, 1), q, k_s)
    m = _each(lambda x: -strict * x[:c], qb)
    pb = _each(lambda x: incl * x[c:], qb)
    lkv = _each(lambda x, y: _dot3(strict * x[:c], y, 1, 0), qk, v)
    pkv = _each(lambda x, y: _dot3(incl * x[c:], y, 1, 0), qk, v)
    a = _each(lambda x, y: jnp.concatenate([x, y], axis=1), kk_s, lkv)
    steps = max(1, (c - 1).bit_length())
    for i in range(steps):
        a = _each(lambda x, y: y + _dot3(x, y, 1, 0), m, a)
        if i + 1 < steps:
            m = _each(lambda x: _dot3(x, x, 1, 0), m)
    ry = _each(lambda x, y, z, w: jnp.concatenate([x, y], axis=1) - _dot3(z, w, 1, 0), r_s, pkv, pb, a)
    w_end = _each(lambda x: x[c - 1:c, :], w_t)
    gu_low = _each(lambda x, y, z: _dot3(x, y * z, 0, 0), a, b_s, w_end)
    g = _each(lambda x, y: jnp.where(eye, jnp.broadcast_to(x, (n, n)), 0.0) - y[:n], w_end, gu_low)
    u = _each(lambda x, y, z, w: _dot3(x, y * z, 0, 0) - w[n:], v, k_s, w_end, gu_low)
    return g, u, _each(lambda x: x[:, :n], ry), _each(lambda x: x[:, n:], ry)


def _wkv_blocks(h, nchunk):
    return (_tile(h, (4, 2, 1)), _tile(nchunk, (4, 2, 1))), (h, _tile(nchunk, (4, 2, 1)))


def _wkv_fwd(r, lw, k, v, kk, b, *, name, gather=(), gather_state=()):
    h, t, n = r.shape
    c = WKV_CHUNK
    nchunk = t // c
    (hb, cb), (hs, cs) = _wkv_blocks(h, nchunk)
    ng = len(gather)
    ni, nj = h // hb, nchunk // cb

    pairs = [(i, j) for i in range(hb) for j in range(cb)]

    def local_body(*refs):
        ins, g_in = refs[:6], refs[6:6 + ng]
        g_ref, u_ref, r2_ref, y0_ref = refs[6 + ng:10 + ng]
        g_out, g_sems = refs[10 + ng:10 + 2 * ng], refs[10 + 2 * ng:]
        if ng:
            @pl.when((pl.program_id(0) == 0) & (pl.program_id(1) == 0))
            def _():
                _jobs_start(gather, g_in, g_out, g_sems)

        g, u, r2, y0 = _wkv_local(*[[ref[i, pl.ds(j * c, c)] for i, j in pairs] for ref in ins])
        for idx, (i, j) in enumerate(pairs):
            g_ref[i, j] = g[idx]
            u_ref[i, j] = u[idx]
            r2_ref[i, pl.ds(j * c, c)] = r2[idx]
            y0_ref[i, pl.ds(j * c, c)] = y0[idx]
        if ng:
            @pl.when((pl.program_id(0) == ni - 1) & (pl.program_id(1) == nj - 1))
            def _():
                _jobs_finish(gather, g_in, g_out, g_sems)

    seq = pl.BlockSpec((hb, cb * c, n), lambda i, j: (i, j, 0))
    mat = pl.BlockSpec((hb, cb, n, n), lambda i, j: (i, j, 0, 0))
    res = pl.pallas_call(
        local_body, name=name + "_local", grid=(ni, nj), in_specs=[seq] * 6 + [_ANY] * ng,
        out_specs=[mat, mat, seq, seq] + [_ANY] * ng,
        out_shape=[jax.ShapeDtypeStruct((h, nchunk, n, n), F32)] * 2 + [jax.ShapeDtypeStruct((h, t, n), F32)] * 2
        + [_job_out_shape(g) for g in gather],
        scratch_shapes=_jobs_scratch(gather),
        compiler_params=_cparams(("arbitrary", "arbitrary") if ng else ("parallel", "parallel")),
    )(r, lw, k, v, kk, b, *_job_arrays(gather))
    gm, um, r2, y0 = res[:4]
    gathered = list(res[4:])

    ng2 = len(gather_state)
    nsteps = nchunk // cs

    def state_body(*refs):
        g_ref, u_ref, r2_ref, y0_ref = refs[:4]
        g_in, (y_ref, st_ref) = refs[4:4 + ng2], refs[4 + ng2:6 + ng2]
        g_out, s_ref, g_sems = refs[6 + ng2:6 + 2 * ng2], refs[6 + 2 * ng2], refs[7 + 2 * ng2:]

        @pl.when(pl.program_id(0) == 0)
        def _():
            s_ref[...] = jnp.zeros_like(s_ref)
            _jobs_start(gather_state, g_in, g_out, g_sems)

        s = [s_ref[i] for i in range(hs)]
        for j in range(cs):
            rows = pl.ds(j * c, c)
            for i in range(hs):
                st_ref[i, j] = s[i]
                y_ref[i, rows] = _dot3(r2_ref[i, rows], s[i], 1, 1) + y0_ref[i, rows]
            s = [_dot3(s[i], g_ref[i, j], 1, 0) + u_ref[i, j] for i in range(hs)]
        for i in range(hs):
            s_ref[i] = s[i]
        if ng2:
            @pl.when(pl.program_id(0) == nsteps - 1)
            def _():
                _jobs_finish(gather_state, g_in, g_out, g_sems)

    seq = pl.BlockSpec((hs, cs * c, n), lambda j: (0, j, 0))
    mat = pl.BlockSpec((hs, cs, n, n), lambda j: (0, j, 0, 0))
    res = pl.pallas_call(
        state_body, name=name + "_state", grid=(nsteps,), in_specs=[mat, mat, seq, seq] + [_ANY] * ng2,
        out_specs=[seq, mat] + [_ANY] * ng2,
        out_shape=[jax.ShapeDtypeStruct((h, t, n), F32), jax.ShapeDtypeStruct((h, nchunk, n, n), F32)]
        + [_job_out_shape(g) for g in gather_state],
        scratch_shapes=[pltpu.VMEM((hs, n, n), F32)] + _jobs_scratch(gather_state), compiler_params=_cparams(("arbitrary",)),
    )(gm, um, r2, y0, *_job_arrays(gather_state))
    return res[0], (res[1], gm, r2), gathered + list(res[2:])


def _wkv_bwd(r, lw, k, v, kk, b, saved, dy, *, name, jobs_state=(), jobs_local=lambda state_results: ()):
    states, gm, r2 = saved
    h, t, n = r.shape
    c = WKV_CHUNK
    nchunk = t // c
    (hb, _), (hs, cs) = _wkv_blocks(h, nchunk)
    cb = _tile(nchunk, (8, 4, 2, 1))
    nsteps = nchunk // cs

    ns_ = len(jobs_state)

    def state_body(*refs):
        g_ref, r2_ref, st_ref, dy_ref = refs[:4]
        s_in, (dg_ref, du_ref, dr2_ref) = refs[4:4 + ns_], refs[4 + ns_:7 + ns_]
        s_out, ds_ref, s_sems = refs[7 + ns_:7 + 2 * ns_], refs[7 + 2 * ns_], refs[8 + 2 * ns_:]

        @pl.when(pl.program_id(0) == 0)
        def _():
            ds_ref[...] = jnp.zeros_like(ds_ref)
            _jobs_start(jobs_state, s_in, s_out, s_sems)

        ds = [ds_ref[i] for i in range(hs)]
        for j in reversed(range(cs)):
            rows = pl.ds(j * c, c)
            for i in range(hs):
                s0 = st_ref[i, j]
                du_ref[i, j] = ds[i]
                dg_ref[i, j] = _dot3(s0, ds[i], 0, 0)
                dr2_ref[i, rows] = _dot3(dy_ref[i, rows], s0, 1, 0)
            ds = [_dot3(dy_ref[i, rows], r2_ref[i, rows], 0, 0) + _dot3(ds[i], g_ref[i, j], 1, 1) for i in range(hs)]
        for i in range(hs):
            ds_ref[i] = ds[i]
        if ns_:
            @pl.when(pl.program_id(0) == nsteps - 1)
            def _():
                _jobs_finish(jobs_state, s_in, s_out, s_sems)

    seq = pl.BlockSpec((hs, cs * c, n), lambda j: (0, nsteps - 1 - j, 0))
    mat = pl.BlockSpec((hs, cs, n, n), lambda j: (0, nsteps - 1 - j, 0, 0))
    res = pl.pallas_call(
        state_body, name=name + "_state", grid=(nsteps,), in_specs=[mat, seq, mat, seq] + [_ANY] * ns_,
        out_specs=[mat, mat, seq] + [_ANY] * ns_,
        out_shape=[jax.ShapeDtypeStruct((h, nchunk, n, n), F32)] * 2 + [jax.ShapeDtypeStruct((h, t, n), F32)]
        + [_job_out_shape(j) for j in jobs_state],
        scratch_shapes=[pltpu.VMEM((hs, n, n), F32)] + _jobs_scratch(jobs_state), compiler_params=_cparams(("arbitrary",)),
    )(gm, r2, states, dy, *_job_arrays(jobs_state))
    dg, du, dr2 = res[:3]
    jobs = list(jobs_local(list(res[3:])))

    pairs = [(i, j) for i in range(hb) for j in range(cb)]
    nj_ = len(jobs)
    ni, nj = h // hb, nchunk // cb

    def local_body(*refs):
        ins, (dg_ref, du_ref, dr2_ref, dy_ref) = refs[:6], refs[6:10]
        j_in, out_refs, j_out, j_sems = refs[10:10 + nj_], refs[10 + nj_:16 + nj_], refs[16 + nj_:16 + 2 * nj_], refs[16 + 2 * nj_:]
        if nj_:
            @pl.when((pl.program_id(0) == 0) & (pl.program_id(1) == 0))
            def _():
                _jobs_start(jobs, j_in, j_out, j_sems)

        _, vjp = jax.vjp(_wkv_local, *[[ref[i, pl.ds(j * c, c)] for i, j in pairs] for ref in ins])
        grads = vjp(([dg_ref[i, j] for i, j in pairs], [du_ref[i, j] for i, j in pairs],
                     [dr2_ref[i, pl.ds(j * c, c)] for i, j in pairs], [dy_ref[i, pl.ds(j * c, c)] for i, j in pairs]))
        for o_ref, gr in zip(out_refs, grads):
            for idx, (i, j) in enumerate(pairs):
                o_ref[i, pl.ds(j * c, c)] = gr[idx]
        if nj_:
            @pl.when((pl.program_id(0) == ni - 1) & (pl.program_id(1) == nj - 1))
            def _():
                _jobs_finish(jobs, j_in, j_out, j_sems)

    seq = pl.BlockSpec((hb, cb * c, n), lambda i, j: (i, j, 0))
    mat = pl.BlockSpec((hb, cb, n, n), lambda i, j: (i, j, 0, 0))
    res = pl.pallas_call(
        local_body, name=name + "_local", grid=(ni, nj), in_specs=[seq] * 6 + [mat, mat, seq, seq] + [_ANY] * nj_,
        out_specs=[seq] * 6 + [_ANY] * nj_,
        out_shape=[jax.ShapeDtypeStruct((h, t, n), F32)] * 6 + [_job_out_shape(j) for j in jobs],
        scratch_shapes=_jobs_scratch(jobs),
        compiler_params=_cparams(("arbitrary", "arbitrary") if nj_ else ("parallel", "parallel")),
    )(r, lw, k, v, kk, b, dg, du, dr2, dy, *_job_arrays(jobs))
    return list(res[:6]), list(res[6:])


class _Dims:
    pass


def _make_dims(x, p, w):
    m = _Dims()
    m.t, m.d = x.shape[-2], x.shape[-1]
    m.nl = w["ln_mix"].shape[0]
    m.dl = w["conv_a_b"].shape[1]
    m.hl = w["lru_wx"].shape[1]
    m.dr = w["rwkv_w0"].shape[1]
    m.h, m.n = w["rwkv_rk"].shape[1], w["rwkv_rk"].shape[2]
    m.lw, m.la, m.lg, m.lv = (w[k].shape[1] for k in ("rwkv_w2", "rwkv_a2", "rwkv_g2", "rwkv_v2"))
    m.nsh = w["mu_shift"].shape[1]
    m.ff = w["conv_f_b"].shape[1]
    m.ple = p.shape[-1]
    m.din = 2 * m.dl + m.nsh
    m.lz = _round_up(m.lw + m.la + m.lg + m.lv, LANES_V7X)
    m.zw = _round_up(2 * m.dl + 3 * m.dr + m.lz, 512)
    m.zs = m.zw - 2 * m.dl
    m.tr = _tile(m.t, (256, 128, 64, 32, 16, 8))
    m.trb = _tile(m.t, (128, 64, 32, 16, 8))
    m.tcs = _tile(m.zs, (512, 256, 128))
    assert (3 * m.dr) % m.lz == 0 and (2 * m.dl) % m.tcs == 0 and m.t % WKV_CHUNK == 0
    assert m.nsh == 3 * m.dr + m.lw + m.la + m.lg
    return m


def _to_heads(m, a):
    return jnp.transpose(a.reshape(m.t, m.h, m.n), (1, 0, 2))


def _from_heads(m, a):
    return jnp.transpose(a, (1, 0, 2)).reshape(m.t, m.dr)


def _norm_fwd(m, h, g, name):
    return _stage_fwd(_f_norm, [(h, m.d, 0)], [(g, 0)], [], [(m.d, BF16)], axis=0, tile=m.tr, rows=m.t, name=name)[0]


def _norm_bwd(m, h, g, du, dres, name):
    return _stage_bwd(_f_norm_res, [(h, m.d, 0)], [(g, 0)], [], [(du, m.d, 0), (dres, m.d, 0)], [(0, m.d, F32)],
                      axis=0, tile=m.tr, rows=m.t, name=name)


def _rwkv_pre_operands(m, w, i, sv, v_first_zs, with_r):
    zs = sv["zs"]
    tiled = ([(zs, m.dr, 0)] if with_r else []) + [(zs, m.dr, 1), (zs, m.dr, 2), (zs, m.lz, 3 * m.dr // m.lz)]
    params = [(w["rwkv_w0"][i:i + 1], 0), (w["w2p"][i], 0), (w["rwkv_a0"][i:i + 1], 0), (w["a2p"][i], 0),
              (w["g2p"][i], 0), (w["rwkv_kk"][i:i + 1], 0), (w["rwkv_ka"][i:i + 1], 0)]
    if i > 0:
        tiled.append((v_first_zs, m.dr, 2))
        params += [(w["rwkv_v0"][i - 1:i], 0), (w["v2p"][i - 1], 0)]
    return tiled, params


def _rwkv_post_operands(m, w, i, sv):
    tiled = [(sv["y"], m.dr, 0), (sv["zs"], m.dr, 0), (sv["k2"], m.dr, 0), (sv["v2"], m.dr, 0), (sv["g"], m.dr, 0)]
    params = [(w["rwkv_lnx_w"][i:i + 1], 0), (w["rwkv_lnx_b"][i:i + 1], 0), (w["rk"][i], 0)]
    return tiled, params


def _lru_gate_params(w, i):
    return [(w["lru_wx"][i], 0), (w["lru_wa"][i], 0), (w["lru_bx"][i:i + 1], 0), (w["lru_ba"][i:i + 1], 0),
            (w["lru_lambda"][i:i + 1], 0)]


class _WeightFeed:
    def __init__(self, m, w, shards, vres):
        self.m, self.w, self.shards, self.vres = m, w, shards, vres

    def keys(self, carrier, i):
        plan = {"mm_in": [("w_o", i)] if i == 0 else [],
                "wkv_local": [("w_gate", i), ("w_up", i)], "wkv_state": [("w_down", i)],
                "mm_gate": [("w_ple_gate", i)], "mm_up": [("w_ple_proj", i), ("w_o", i + 1)],
                "mm_down": [("w_in", i + 1)], "mm_pgate": []}
        return [key for key in plan[carrier] if key[1] < self.m.nl]

    def blobs(self, keys):
        return [_gather_blob(self.shards[name][layer]) for name, layer in keys]

    def arrive(self, keys, gathered):
        m = self.m
        for (name, layer), got in zip(keys, gathered):
            full = got.reshape((N_XY,) + self.shards[name][layer].shape)
            full = _from_shards(full, _SHARD_AXIS[name] - 1)
            if name == "w_in":
                vres = self.vres[layer - 1] if layer > 0 else jnp.zeros((m.d, m.lv), BF16)
                self.w["wcat"][layer] = jnp.concatenate([full, vres, jnp.zeros((m.d, m.zw - m.din - m.lv), BF16)], axis=1)
            else:
                self.w[name][layer] = full


def _mm_fed(feed, carrier, i, a, b, **kw):
    keys = feed.keys(carrier, i) if feed is not None else []
    if not keys:
        return _mm(a, b, **kw)
    out, got = _mm(a, b, gather=feed.blobs(keys), **kw)
    feed.arrive(keys, got)
    return out


def _layer_fwd(m, w, i, h, p_bf, v_first_zs, feed=None):
    sv = {"h": h}
    t, dl, dr = m.t, m.dl, m.dr
    sv["u1"] = _norm_fwd(m, h, w["ln_mix"][i:i + 1], "norm_mix")
    z = sv["z"] = _mm_fed(feed, "mm_in", i, sv["u1"], w["wcat"][i], name="mm_in")
    off = 2 * dl // m.tcs
    sv["zs"] = _stage_fwd(_f_shiftmix, [(z, None, off)], [(w["mu_pad"][i], off)], [], [(m.zs, F32)],
                          axis=1, tile=m.tcs, rows=t, name="shiftmix")[0]
    tca = _tile(dl, (512, 256, 128))
    sv["xb"] = _stage_fwd(_f_conv, [(z, None, 0)], [(w["conv_a_w"][i], 0), (w["conv_a_b"][i:i + 1], 0)], [],
                          [(dl, F32)], axis=1, tile=tca, rows=t, name="conv_a")[0]
    sv["a"], b_in = _stage_fwd(_make_f_lru_gates(m.hl), [(sv["xb"], dl, 0)], _lru_gate_params(w, i), [],
                               [(dl, F32), (dl, F32)], axis=0, tile=m.tr, rows=t, name="lru_gates")
    sv["hl"] = _lru_scan(sv["a"], b_in, name="lru_scan")
    out_a = _stage_fwd(_f_lru_out, [(sv["hl"], dl, 0), (z, dl, 1)], [(w["lru_norm"][i:i + 1], 0)], [],
                       [(dl, BF16)], axis=0, tile=m.tr, rows=t, name="lru_out")[0]
    tiled, params = _rwkv_pre_operands(m, w, i, sv, v_first_zs, False)
    pre = _stage_fwd(_make_f_rwkv_pre(i > 0), tiled, params, [w["bb"]], [(dr, F32)] * 6,
                     axis=0, tile=m.tr, rows=t, name="rwkv_pre")
    sv["logw"], sv["k2"], sv["v2"], sv["kk"], sv["b"], sv["g"] = pre
    heads = [_to_heads(m, a) for a in (sv["zs"][:, :dr], sv["logw"], sv["k2"], sv["v2"], sv["kk"], sv["b"])]
    keys = [feed.keys(carrier, i) if feed is not None else [] for carrier in ("wkv_local", "wkv_state")]
    y_h, sv["states"], got = _wkv_fwd(*heads, name="wkv_fwd", gather=feed.blobs(keys[0]) if keys[0] else (),
                                      gather_state=feed.blobs(keys[1]) if keys[1] else ())
    if keys[0] or keys[1]:
        feed.arrive(keys[0] + keys[1], got)
    sv["y"] = _from_heads(m, y_h)
    tiled, params = _rwkv_post_operands(m, w, i, sv)
    out_b = _stage_fwd(_make_f_rwkv_post(m.n), tiled, params, [w["bb"]], [(dr, BF16)],
                       axis=0, tile=m.tr, rows=t, name="rwkv_post")[0]
    sv["cat"] = jnp.concatenate([out_a, out_b], axis=1)
    h2 = sv["h2"] = _mm(sv["cat"], w["w_o"][i], res=h, name="mm_o")
    sv["u2"] = _norm_fwd(m, h2, w["ln_ffn"][i:i + 1], "norm_ffn")
    sv["gpre"] = _mm_fed(feed, "mm_gate", i, sv["u2"], w["w_gate"][i], name="mm_gate")
    sv["up"] = _mm_fed(feed, "mm_up", i, sv["u2"], w["w_up"][i], name="mm_up")
    tcf = _tile(m.ff, (512, 256, 128))
    sv["act"] = _stage_fwd(_f_ffn_act, [(sv["gpre"], None, 0), (sv["up"], None, 0)],
                           [(w["conv_f_w"][i], 0), (w["conv_f_b"][i:i + 1], 0)], [], [(m.ff, BF16)],
                           axis=1, tile=tcf, rows=t, name="ffn_act")[0]
    h3 = sv["h3"] = _mm_fed(feed, "mm_down", i, sv["act"], w["w_down"][i], res=h2, name="mm_down")
    sv["u3"] = _norm_fwd(m, h3, w["ln_ple"][i:i + 1], "norm_ple")
    sv["eg"] = _mm_fed(feed, "mm_pgate", i, sv["u3"], w["w_ple_gate"][i], name="mm_pgate")
    sv["ep"] = _mm(p_bf, w["w_ple_proj"][i], name="mm_pproj")
    h4 = _stage_fwd(_f_ple, [(h3, m.d, 0), (sv["eg"], m.d, 0), (sv["ep"], m.d, 0)], [(w["ln_ple_post"][i:i + 1], 0)],
                    [], [(m.d, F32)], axis=0, tile=m.tr, rows=t, name="ple")[0]
    return h4, sv


def _layer_bwd(m, w, i, dh4, sv, p_bf, v_first_zs, dvf_in, pending=None, early=None):
    t, d, dl, dr = m.t, m.d, m.dl, m.dr
    g = {}
    deg, dep, g["ln_ple_post"] = _stage_bwd(
        _f_ple, [(sv["h3"], d, 0), (sv["eg"], d, 0), (sv["ep"], d, 0)], [(w["ln_ple_post"][i:i + 1], 0)], [],
        [(dh4, d, 0)], [(1, d, BF16), (2, d, BF16)], axis=0, tile=m.tr, rows=t, name="ple_bwd")
    du3 = _mm(deg, w["w_ple_gate"][i], tb=True, name="mm_pgate_dx")
    g["w_ple_gate"] = _mm(sv["u3"], deg, ta=True, name="mm_pgate_dw")
    g["w_ple_proj"] = _mm(p_bf, dep, ta=True, name="mm_pproj_dw")
    dh3, g["ln_ple"] = _norm_bwd(m, sv["h3"], w["ln_ple"][i:i + 1], du3, dh4, "norm_ple_bwd")
    dh3_bf = dh3.astype(BF16)
    dact = _mm(dh3_bf, w["w_down"][i], tb=True, name="mm_down_dx")
    if pending is None:
        g["w_down"] = _mm(sv["act"], dh3_bf, ta=True, name="mm_down_dw")
    else:
        g["w_down"], gots = _mm(sv["act"], dh3_bf, ta=True, name="mm_down_dw", gather=pending.send_jobs())
        pending.after_send(gots)
    tcf = _tile(m.ff, (512, 256, 128))
    dgpre, dup, g["conv_f_w"], g["conv_f_b"] = _stage_bwd(
        _f_ffn_act, [(sv["gpre"], None, 0), (sv["up"], None, 0)], [(w["conv_f_w"][i], 0), (w["conv_f_b"][i:i + 1], 0)],
        [], [(dact, 0)], [(0, m.ff, BF16), (1, m.ff, BF16)], axis=1, tile=tcf, rows=t, ncols=m.ff, name="ffn_act_bwd")
    du2 = _mm(dgpre, w["w_gate"][i], tb=True, name="mm_gate_dx")
    du2 = _mm(dup, w["w_up"][i], tb=True, res=du2, name="mm_up_dx")
    g["w_gate"] = _mm(sv["u2"], dgpre, ta=True, name="mm_gate_dw")
    g["w_up"] = _mm(sv["u2"], dup, ta=True, name="mm_up_dw")
    dh2, g["ln_ffn"] = _norm_bwd(m, sv["h2"], w["ln_ffn"][i:i + 1], du2, dh3, "norm_ffn_bwd")
    dh2_bf = dh2.astype(BF16)
    dcat = _mm(dh2_bf, w["w_o"][i], tb=True, name="mm_o_dx")
    g["w_o"] = _mm(sv["cat"], dh2_bf, ta=True, name="mm_o_dw")
    tiled, params = _rwkv_post_operands(m, w, i, sv)
    dy, dr_a, dk2_a, dv2_a, dg, g["rwkv_lnx_w"], g["rwkv_lnx_b"], g["rk"] = _stage_bwd(
        _make_f_rwkv_post(m.n), tiled, params, [w["bb"]], [(dcat, dr, dl // dr)], [(j, dr, F32) for j in range(5)],
        axis=0, tile=m.trb, rows=t, name="rwkv_post_bwd")
    heads = [_to_heads(m, a) for a in (sv["zs"][:, :dr], sv["logw"], sv["k2"], sv["v2"], sv["kk"], sv["b"])]
    own = early(g) if early is not None else None

    def local_jobs(state_results):
        jobs = []
        if own is not None:
            own.after_send(state_results)
            jobs += own.exchange_jobs()
        if pending is not None:
            jobs += pending.exchange_jobs()
        return jobs

    dwkv, parts = _wkv_bwd(*heads, sv["states"], _to_heads(m, dy), name="wkv_bwd",
                           jobs_state=own.send_jobs() if own is not None else (), jobs_local=local_jobs)
    n_own = len(own.names) if own is not None else 0
    if own is not None:
        own.after_exchange(parts[:n_own])
    if pending is not None:
        pending.after_exchange(parts[n_own:])
    dr_b, dlw, dk2_b, dv2_b, dkk, db = [_from_heads(m, a) for a in dwkv]
    tiled, params = _rwkv_pre_operands(m, w, i, sv, v_first_zs, True)
    v_cots = [dv2_a, dv2_b] + ([dvf_in] if dvf_in is not None else [])
    cots = [(c, dr, 0) for c in [dr_a, dr_b, dlw, dk2_a, dk2_b] + v_cots + [dkk, db, dg]]
    ntil = len(tiled)
    dtiled = [(0, dr, F32), (1, dr, F32), (2, dr, F32), (3, m.lz, F32)] + ([(4, dr, F32)] if i > 0 else [])
    res = _stage_bwd(_make_f_rwkv_pre(i > 0, len(v_cots)), tiled, params, [w["bb"]], cots, dtiled,
                     axis=0, tile=m.trb, rows=t, name="rwkv_pre_bwd")
    d_r, d_k, d_v, d_lz = res[:4]
    dvf_out = res[4] if i > 0 else None
    pg = res[ntil:]
    g["rwkv_w0"], g["w2p"], g["rwkv_a0"], g["a2p"], g["g2p"], g["rwkv_kk"], g["rwkv_ka"] = pg[:7]
    if i > 0:
        g["rwkv_v0"], g["v2p"] = pg[7:9]
    dzs = jnp.concatenate([d_r, d_k, d_v, d_lz, jnp.zeros((t, m.zs - 3 * dr - m.lz), F32)], axis=1)
    off = 2 * dl // m.tcs
    dzr, g["mu_pad"] = _stage_bwd(_f_shiftmix, [(sv["z"], None, off)], [(w["mu_pad"][i], off)], [], [(dzs, 0)],
                                  [(0, m.zs, BF16)], axis=1, tile=m.tcs, rows=t, ncols=m.zs, name="shiftmix_bwd")
    dhl, dya, g["lru_norm"] = _stage_bwd(
        _f_lru_out, [(sv["hl"], dl, 0), (sv["z"], dl, 1)], [(w["lru_norm"][i:i + 1], 0)], [], [(dcat, dl, 0)],
        [(0, dl, F32), (1, dl, BF16)], axis=0, tile=m.tr, rows=t, name="lru_out_bwd")
    da, db_in = _lru_scan_bwd(sv["a"], sv["hl"], dhl, name="lru_scan_bwd")
    dxb, g["lru_wx"], g["lru_wa"], g["lru_bx"], g["lru_ba"], g["lru_lambda"] = _stage_bwd(
        _make_f_lru_gates(m.hl), [(sv["xb"], dl, 0)], _lru_gate_params(w, i), [], [(da, dl, 0), (db_in, dl, 0)],
        [(0, dl, F32)], axis=0, tile=m.tr, rows=t, name="lru_gates_bwd")
    tca = _tile(dl, (512, 256, 128))
    dxa, g["conv_a_w"], g["conv_a_b"] = _stage_bwd(
        _f_conv, [(sv["z"], None, 0)], [(w["conv_a_w"][i], 0), (w["conv_a_b"][i:i + 1], 0)], [], [(dxb, 0)],
        [(0, dl, BF16)], axis=1, tile=tca, rows=t, ncols=dl, name="conv_a_bwd")
    dz = jnp.concatenate([dxa, dya, dzr], axis=1)
    du1 = _mm(dz, w["wcat"][i], tb=True, name="mm_in_dx")
    g["wcat"] = _mm(sv["u1"], dz, ta=True, name="mm_in_dw")
    dh, g["ln_mix"] = _norm_bwd(m, sv["h"], w["ln_mix"][i:i + 1], du1, dh2, "norm_mix_bwd")
    return dh, g, dvf_out, own


def _loss_head(m, h, g, tgt):
    tile, d = m.tr, m.d

    def body(h_ref, g_ref, t_ref, loss_ref, dh_ref, dg_ref):
        def f(hv, gv):
            err = _rms(hv, gv) - t_ref[...]
            return 0.5 * jnp.sum(jnp.mean(err * err, axis=-1))

        val, vjp = jax.vjp(f, h_ref[...], g_ref[...])
        dh, dg = vjp(jnp.ones((), F32))
        dh_ref[...] = dh

        @pl.when(pl.program_id(0) == 0)
        def _():
            dg_ref[...] = jnp.zeros_like(dg_ref)
            loss_ref[...] = jnp.zeros_like(loss_ref)

        dg_ref[...] += dg
        loss_ref[...] += jnp.full(loss_ref.shape, val, F32)

    row = pl.BlockSpec((tile, d), lambda i: (i, 0))
    return pl.pallas_call(
        body, name="loss_head", grid=(m.t // tile,),
        in_specs=[row, pl.BlockSpec((1, d), lambda i: (0, 0)), row],
        out_specs=[pl.BlockSpec((1, LANES_V7X), lambda i: (0, 0)), row, pl.BlockSpec((1, d), lambda i: (0, 0))],
        out_shape=[jax.ShapeDtypeStruct((1, LANES_V7X), F32), jax.ShapeDtypeStruct((m.t, d), F32),
                   jax.ShapeDtypeStruct((1, d), F32)],
        compiler_params=_cparams(("arbitrary",)),
    )(h, g, tgt)


def _local_step(m, w, x, p, tgt, feed=None, reducer=None):
    h = x
    saved = []
    p_bf = p.astype(BF16)
    for i in range(m.nl):
        h, sv = _layer_fwd(m, w, i, h, p_bf[i], saved[0]["zs"] if i > 0 else None, feed)
        saved.append(sv)
    loss_row, dh, d_ln_final = _loss_head(m, h, w["ln_final"], tgt)
    grads = [None] * m.nl
    reductions = [None] * m.nl
    dvf = None
    for i in reversed(range(m.nl)):
        pending = reductions[i + 1][1] if reducer is not None and i + 1 < m.nl else None
        early = functools.partial(reducer, i, which="early") if reducer is not None else None
        dh, grads[i], dvf_i, own = _layer_bwd(m, w, i, dh, saved[i], p_bf[i], saved[0]["zs"] if i > 0 else None,
                                              dvf if i == 0 else None, pending, early)
        if reducer is not None:
            reductions[i] = (own, reducer(i, grads[i], which="late"))
        if i > 0:
            dvf = dvf_i if dvf is None else dvf + dvf_i
    return loss_row, dh, grads, d_ln_final, reductions


_BIG = ("w_o", "w_gate", "w_up", "w_down", "w_ple_gate", "w_ple_proj")


def _lora_rows(m):
    o1 = m.lw
    o2 = o1 + m.la
    o3 = o2 + m.lg
    return {"w2p": (0, o1), "a2p": (o1, o2), "g2p": (o2, o3), "v2p": (o3, o3 + m.lv)}


def _prepare_weights(m, wf):
    w = {k: v for k, v in wf.items() if k not in _BIG and k not in ("w_in", "w_in_vres")}
    nl = m.nl
    for k in _BIG:
        w[k] = [wf[k][i].astype(BF16) for i in range(nl)] if k in wf else [None] * nl
    w["wcat"] = [None] * nl
    if "w_in" in wf:
        vres = jnp.concatenate([jnp.zeros((1, m.d, m.lv), BF16), wf["w_in_vres"].astype(BF16)], axis=0)
        pad = jnp.zeros((m.d, m.zw - m.din - m.lv), BF16)
        w["wcat"] = [jnp.concatenate([wf["w_in"][i].astype(BF16), vres[i], pad], axis=1) for i in range(nl)]
    mu_v = jnp.concatenate([jnp.zeros((1, m.lv), F32), wf["mu_shift_vres"]], axis=0)
    w["mu_pad"] = jnp.concatenate([jnp.zeros((nl, 2 * m.dl), F32), wf["mu_shift"], mu_v,
                                   jnp.zeros((nl, m.zw - m.din - m.lv), F32)], axis=1)[:, None, :]
    rows = _lora_rows(m)
    for name, src in (("w2p", "rwkv_w2"), ("a2p", "rwkv_a2"), ("g2p", "rwkv_g2"), ("v2p", "rwkv_v2")):
        lo, hi = rows[name]
        a = wf[src]
        w[name] = jnp.concatenate([jnp.zeros((a.shape[0], lo, m.dr), F32), a, jnp.zeros((a.shape[0], m.lz - hi, m.dr), F32)],
                                  axis=1)
    w["rk"] = wf["rwkv_rk"].reshape(nl, 1, m.dr)
    w["ln_final"] = wf["ln_final"].reshape(1, m.d)
    head = jnp.arange(m.dr, dtype=jnp.int32) // m.n
    w["bb"] = (head[:, None] == head[None, :]).astype(BF16)
    return w


def _unpack_grads(m, grads, d_ln_final, with_big=True):
    nl = m.nl
    out = {}

    def stack(key):
        return jnp.stack([grads[i][key] for i in range(nl)], axis=0)

    for k in (_BIG if with_big else ()) + ("conv_a_w", "conv_f_w", "lru_wx", "lru_wa"):
        out[k] = stack(k)
    for k in ("ln_mix", "conv_a_b", "lru_bx", "lru_ba", "lru_lambda", "lru_norm", "rwkv_w0", "rwkv_a0", "rwkv_kk",
              "rwkv_ka", "rwkv_lnx_w", "rwkv_lnx_b", "ln_ffn", "conv_f_b", "ln_ple", "ln_ple_post"):
        out[k] = stack(k)[:, 0, :]
    if with_big:
        out["w_in"] = stack("wcat")[:, :, :m.din]
    out["w_in_vres"] = jnp.stack([grads[i]["wcat"][:, m.din:m.din + m.lv] for i in range(1, nl)], axis=0)
    mu = stack("mu_pad")[:, 0, :]
    out["mu_shift"] = mu[:, :m.nsh]
    out["mu_shift_vres"] = mu[1:, m.nsh:m.nsh + m.lv]
    rows = _lora_rows(m)
    for name, dst in (("w2p", "rwkv_w2"), ("a2p", "rwkv_a2"), ("g2p", "rwkv_g2")):
        lo, hi = rows[name]
        out[dst] = stack(name)[:, lo:hi, :]
    lo, hi = rows["v2p"]
    out["rwkv_v2"] = jnp.stack([grads[i]["v2p"] for i in range(1, nl)], axis=0)[:, lo:hi, :]
    out["rwkv_v0"] = jnp.stack([grads[i]["rwkv_v0"] for i in range(1, nl)], axis=0)[:, 0, :]
    out["rwkv_rk"] = stack("rk").reshape(nl, m.h, m.n)
    out["ln_final"] = d_ln_final.reshape(m.d)
    return out


_ANY = pl.BlockSpec(memory_space=pl.ANY)


def _position():
    return lax.axis_index("x"), lax.axis_index("y"), lax.axis_index("c")


def _other_chips(x, y):
    return [(1 - x, y), (x, 1 - y), (1 - x, 1 - y)]


def _gather_blob(shard):
    rows, wd = shard.shape
    return shard.reshape(2, rows // 2, wd)


_JOB_SEMS = {"gather": 7, "pair_send": 1, "exchange": 3}


def _job_parts(job):
    return job if isinstance(job, tuple) else ("gather", job)


def _job_arrays(jobs):
    return [_job_parts(j)[1] for j in jobs]


def _job_out_shape(job):
    kind, arr = _job_parts(job)
    if kind == "gather":
        return jax.ShapeDtypeStruct((N_XY,) + arr.shape, arr.dtype)
    if kind == "pair_send":
        return jax.ShapeDtypeStruct((arr.shape[0], arr.shape[1] // 2, arr.shape[2]), arr.dtype)
    return jax.ShapeDtypeStruct(arr.shape, arr.dtype)


def _jobs_scratch(jobs):
    out = []
    for j in jobs:
        n = _JOB_SEMS[_job_parts(j)[0]]
        out += [pltpu.SemaphoreType.DMA((n,)), pltpu.SemaphoreType.DMA((n,))]
    return out


def _jobs_start(jobs, in_refs, out_refs, sems):
    for q, j in enumerate(jobs):
        _JOB_START[_job_parts(j)[0]](in_refs[q], out_refs[q], sems[2 * q], sems[2 * q + 1])


def _jobs_finish(jobs, in_refs, out_refs, sems):
    for q, j in enumerate(jobs):
        _JOB_FINISH[_job_parts(j)[0]](in_refs[q], out_refs[q], sems[2 * q], sems[2 * q + 1])


def _run_jobs(jobs, *, name):
    n = len(jobs)

    def body(*refs):
        _jobs_start(jobs, refs[:n], refs[n:2 * n], refs[2 * n:])
        _jobs_finish(jobs, refs[:n], refs[n:2 * n], refs[2 * n:])

    return pl.pallas_call(body, name=name, in_specs=[_ANY] * n, out_specs=[_ANY] * n,
                          out_shape=[_job_out_shape(j) for j in jobs], scratch_shapes=_jobs_scratch(jobs))(*_job_arrays(jobs))


def _gather_copies(in_ref, out_ref, send_sems, recv_sems):
    x, y, c = _position()
    me = 2 * x + y
    sends, hands, ici_in, d2d_in = [], [], [], []
    for k, (px, py) in enumerate(_other_chips(x, y)):
        landed = out_ref.at[2 * px + py, c]
        sends.append(pltpu.make_async_remote_copy(
            src_ref=in_ref.at[c], dst_ref=out_ref.at[me, c], send_sem=send_sems.at[k], recv_sem=recv_sems.at[k],
            device_id=(px, py, c), device_id_type=MESH))
        ici_in.append(pltpu.make_async_remote_copy(
            src_ref=in_ref.at[c], dst_ref=landed, send_sem=send_sems.at[k], recv_sem=recv_sems.at[k],
            device_id=(px, py, c), device_id_type=MESH))
        hands.append(pltpu.make_async_remote_copy(
            src_ref=landed, dst_ref=landed, send_sem=send_sems.at[3 + k], recv_sem=recv_sems.at[3 + k],
            device_id=(x, y, 1 - c), device_id_type=MESH))
        d2d_in.append(pltpu.make_async_remote_copy(
            src_ref=in_ref.at[c], dst_ref=out_ref.at[2 * px + py, 1 - c], send_sem=send_sems.at[3 + k],
            recv_sem=recv_sems.at[3 + k], device_id=(x, y, 1 - c), device_id_type=MESH))
    own = pltpu.make_async_remote_copy(src_ref=in_ref, dst_ref=out_ref.at[me], send_sem=send_sems.at[6],
                                       recv_sem=recv_sems.at[6], device_id=(x, y, 1 - c), device_id_type=MESH)
    sends.append(own)
    d2d_in.append(own)
    return sends, hands, ici_in, d2d_in


def _gather_start(in_ref, out_ref, send_sems, recv_sems):
    for cp in _gather_copies(in_ref, out_ref, send_sems, recv_sems)[0]:
        cp.start()


def _gather_finish(in_ref, out_ref, send_sems, recv_sems):
    sends, hands, ici_in, d2d_in = _gather_copies(in_ref, out_ref, send_sems, recv_sems)
    for arrived, hand in zip(ici_in, hands):
        arrived.wait_recv()
        hand.start()
    for arrived in d2d_in:
        arrived.wait_recv()
    for cp in sends + hands:
        cp.wait_send()


def _pair_send_copy(g_ref, out_ref, send_sems, recv_sems):
    x, y, c = _position()
    half = out_ref.shape[1]
    return pltpu.make_async_remote_copy(src_ref=g_ref.at[:, pl.ds((1 - c) * half, half), :], dst_ref=out_ref,
                                        send_sem=send_sems.at[0], recv_sem=recv_sems.at[0], device_id=(x, y, 1 - c),
                                        device_id_type=MESH)


def _exchange_copies(in_ref, out_ref, send_sems, recv_sems):
    x, y, c = _position()
    me = 2 * x + y
    sends, arrivals = [], []
    for k, (px, py) in enumerate(_other_chips(x, y)):
        sends.append(pltpu.make_async_remote_copy(
            src_ref=in_ref.at[2 * px + py], dst_ref=out_ref.at[me], send_sem=send_sems.at[k], recv_sem=recv_sems.at[k],
            device_id=(px, py, c), device_id_type=MESH))
        arrivals.append(pltpu.make_async_remote_copy(
            src_ref=in_ref.at[me], dst_ref=out_ref.at[2 * px + py], send_sem=send_sems.at[k], recv_sem=recv_sems.at[k],
            device_id=(px, py, c), device_id_type=MESH))
    return sends, arrivals


def _exchange_start(*refs):
    for cp in _exchange_copies(*refs)[0]:
        cp.start()


def _exchange_finish(*refs):
    sends, arrivals = _exchange_copies(*refs)
    for cp in arrivals:
        cp.wait_recv()
    for cp in sends:
        cp.wait_send()


_JOB_START = {"gather": _gather_start, "pair_send": lambda *refs: _pair_send_copy(*refs).start(), "exchange": _exchange_start}
_JOB_FINISH = {"gather": _gather_finish, "pair_send": lambda *refs: _pair_send_copy(*refs).wait(),
               "exchange": _exchange_finish}


def _pair_sum(g, got, pos, *, name):
    nq, r, wd = g.shape
    half = r // 2
    tr = _tile(half, (256, 128, 64, 32, 16, 8))
    nb = half // tr

    def body(c_ref, g_ref, got_ref, o_ref):
        o_ref[...] = (g_ref[...] + got_ref[...]).astype(o_ref.dtype)

    grid_spec = pltpu.PrefetchScalarGridSpec(
        num_scalar_prefetch=1, grid=(nq, nb),
        in_specs=[pl.BlockSpec((1, tr, wd), lambda q, j, c_ref: (q, c_ref[0] * nb + j, 0)),
                  pl.BlockSpec((1, tr, wd), lambda q, j, c_ref: (q, j, 0))],
        out_specs=pl.BlockSpec((1, tr, wd), lambda q, j, c_ref: (q, j, 0)))
    return pl.pallas_call(body, name=name, grid_spec=grid_spec, out_shape=jax.ShapeDtypeStruct((nq, half, wd), BF16),
                          compiler_params=_cparams(("arbitrary", "arbitrary")))(pos[0], g, got)


def _chip_sum(parts, pb, pos, *, name):
    nq, half, wd = parts.shape
    tr = _tile(half, (256, 128, 64, 32, 16, 8))
    nb = half // tr

    def body(c_ref, x_ref, y_ref, p_ref, own_ref, o_ref):
        chip = 2 * x_ref[0] + y_ref[0]
        own = own_ref[0].astype(F32)
        acc = None
        for q in range(nq):
            term = jnp.where(chip == q, own, p_ref[q].astype(F32))
            acc = term if acc is None else acc + term
        o_ref[...] = acc

    grid_spec = pltpu.PrefetchScalarGridSpec(
        num_scalar_prefetch=3, grid=(nb,),
        in_specs=[pl.BlockSpec((nq, tr, wd), lambda j, c_ref, x_ref, y_ref: (0, j, 0)),
                  pl.BlockSpec((1, tr, wd), lambda j, c_ref, x_ref, y_ref: (2 * x_ref[0] + y_ref[0], j, 0))],
        out_specs=pl.BlockSpec((tr, wd), lambda j, c_ref, x_ref, y_ref: (c_ref[0] * nb + j, 0)))
    return pl.pallas_call(body, name=name, grid_spec=grid_spec, out_shape=jax.ShapeDtypeStruct((2 * half, wd), F32),
                          compiler_params=_cparams(("arbitrary",)))(*pos, parts, pb)


def _pair_gather(full, *, name):
    r, wd = full.shape
    half = r // 2

    def body(in_ref, out_ref, send_sem, recv_sem):
        x, y, c = _position()
        mine = out_ref.at[pl.ds(c * half, half), :]
        cp = pltpu.make_async_remote_copy(src_ref=mine, dst_ref=mine, send_sem=send_sem, recv_sem=recv_sem,
                                          device_id=(x, y, 1 - c), device_id_type=MESH)
        cp.start()
        pltpu.make_async_remote_copy(src_ref=mine, dst_ref=out_ref.at[pl.ds((1 - c) * half, half), :], send_sem=send_sem,
                                     recv_sem=recv_sem, device_id=(x, y, 1 - c), device_id_type=MESH).wait_recv()
        cp.wait_send()

    return pl.pallas_call(
        body, name=name, in_specs=[_ANY], out_specs=_ANY, out_shape=jax.ShapeDtypeStruct(full.shape, full.dtype),
        input_output_aliases={0: 0}, scratch_shapes=[pltpu.SemaphoreType.DMA(()), pltpu.SemaphoreType.DMA(())],
    )(full)


class _GradReduce:
    def __init__(self, slabs, pos, tag):
        self.names, self.slabs, self.pos, self.tag = list(slabs), [slabs[k] for k in slabs], pos, tag
        self.pb = self.out = None

    def send_jobs(self):
        return [("pair_send", g) for g in self.slabs]

    def after_send(self, gots):
        self.pb = [_pair_sum(g, got, self.pos, name=f"rs_pair_sum_{k}_{self.tag}")
                   for k, g, got in zip(self.names, self.slabs, gots)]

    def exchange_jobs(self):
        return [("exchange", pb) for pb in self.pb]

    def after_exchange(self, parts):
        full = [_chip_sum(pt, pb, self.pos, name=f"rs_chip_sum_{k}_{self.tag}") for k, pt, pb in zip(self.names, parts, self.pb)]
        self.out = {k: _pair_gather(f, name=f"rs_pair_gather_{k}_{self.tag}") for k, f in zip(self.names, full)}

    def run(self):
        if self.pb is None:
            self.after_send(_run_jobs(self.send_jobs(), name="rs_pair_send_" + self.tag))
        if self.out is None:
            self.after_exchange(_run_jobs(self.exchange_jobs(), name="rs_exchange_" + self.tag))
        return self.out


def _all_reduce_small(vec, *, name):
    r, wd = vec.shape

    def body(in_ref, out_ref, slots, send_sems, recv_sems):
        x, y, c = _position()
        me = 4 * x + 2 * y + c
        flips = [(fx, fy, fc) for fx in (0, 1) for fy in (0, 1) for fc in (0, 1) if fx + fy + fc]
        peers = [(1 - x if fx else x, 1 - y if fy else y, 1 - c if fc else c) for fx, fy, fc in flips]
        sends = []
        for k, peer in enumerate(peers):
            cp = pltpu.make_async_remote_copy(src_ref=in_ref, dst_ref=slots.at[me], send_sem=send_sems.at[k],
                                              recv_sem=recv_sems.at[k], device_id=peer, device_id_type=MESH)
            cp.start()
            sends.append(cp)
        slots[me] = in_ref[...]
        for k, (px, py, pc) in enumerate(peers):
            pltpu.make_async_remote_copy(src_ref=in_ref, dst_ref=slots.at[4 * px + 2 * py + pc], send_sem=send_sems.at[k],
                                         recv_sem=recv_sems.at[k], device_id=(px, py, pc), device_id_type=MESH).wait_recv()
        for cp in sends:
            cp.wait_send()
        acc = slots[0]
        for q in range(1, N_DEV):
            acc = acc + slots[q]
        out_ref[...] = acc

    vm = pl.BlockSpec(memory_space=pltpu.VMEM)
    return pl.pallas_call(
        body, name=name, in_specs=[vm], out_specs=vm, out_shape=jax.ShapeDtypeStruct((r, wd), F32),
        scratch_shapes=[pltpu.VMEM((N_DEV, r, wd), F32), pltpu.SemaphoreType.DMA((N_DEV - 1,)),
                        pltpu.SemaphoreType.DMA((N_DEV - 1,))],
        compiler_params=_cparams(),
    )(vec)


def _adamw(w, g, m, v, *, name):
    r, wd = w.shape
    tr = _tile(r, (256, 128, 64, 32, 16, 8))

    def body(w_ref, g_ref, m_ref, v_ref, d_ref, m_out, v_out):
        gv = g_ref[...]
        m_new = ADAM_B1 * m_ref[...] + (1.0 - ADAM_B1) * gv
        v_new = ADAM_B2 * v_ref[...] + (1.0 - ADAM_B2) * (gv * gv)
        m_hat = m_new / (1.0 - ADAM_B1 ** ADAM_STEP)
        v_hat = v_new / (1.0 - ADAM_B2 ** ADAM_STEP)
        d_ref[...] = -ADAM_LR * (m_hat / (jnp.sqrt(v_hat) + ADAM_EPS) + ADAM_WD * w_ref[...])
        m_out[...] = m_new
        v_out[...] = v_new

    spec = pl.BlockSpec((tr, wd), lambda j: (j, 0))
    return pl.pallas_call(body, name=name, grid=(r // tr,), in_specs=[spec] * 4, out_specs=[spec] * 3,
                          out_shape=[jax.ShapeDtypeStruct((r, wd), F32)] * 3, compiler_params=_cparams(("arbitrary",)))(w, g, m, v)


_WEIGHTS = ("ln_mix", "w_in", "w_in_vres", "mu_shift", "mu_shift_vres", "conv_a_w", "conv_a_b", "lru_wx", "lru_bx", "lru_wa",
            "lru_ba", "lru_lambda", "lru_norm", "rwkv_w0", "rwkv_w2", "rwkv_a0", "rwkv_a2", "rwkv_v0", "rwkv_v2", "rwkv_g2",
            "rwkv_kk", "rwkv_ka", "rwkv_rk", "rwkv_lnx_w", "rwkv_lnx_b", "w_o", "ln_ffn", "w_gate", "w_up", "conv_f_w",
            "conv_f_b", "w_down", "ln_ple", "w_ple_gate", "w_ple_proj", "ln_ple_post", "ln_final")
_SHARD_AXIS = {"w_in": 2, "w_in_vres": 1, "conv_a_w": 2, "lru_wx": 2, "lru_wa": 2, "rwkv_w2": 2, "rwkv_a2": 2, "rwkv_v2": 2,
               "rwkv_g2": 2, "w_o": 1, "w_gate": 2, "w_up": 2, "conv_f_w": 2, "w_down": 1, "w_ple_gate": 1, "w_ple_proj": 2}
_BIG_SHARDED = ("w_in",) + _BIG
_SMALL_SHARDED = tuple(k for k in _WEIGHTS if k in _SHARD_AXIS and k not in _BIG_SHARDED)
_REPLICATED = tuple(k for k in _WEIGHTS if k not in _SHARD_AXIS)
PACK_WIDTH = 512


def _to_shards(g, axis):
    n = g.shape[axis] // N_XY
    return jnp.moveaxis(g.reshape(g.shape[:axis] + (N_XY, n) + g.shape[axis + 1:]), axis, 0)


def _from_shards(s, axis):
    s = jnp.moveaxis(s, 0, axis)
    return s.reshape(s.shape[:axis] + (N_XY * s.shape[axis + 1],) + s.shape[axis + 2:])


def _pack(arrs, lead, width, row_mult):
    lead_shape = arrs[0].shape[:lead]
    flat = jnp.concatenate([a.reshape(lead_shape + (-1,)) for a in arrs], axis=-1)
    n = flat.shape[-1]
    total = _round_up(n, width * row_mult)
    flat = jnp.pad(flat, [(0, 0)] * lead + [(0, total - n)])
    return flat.reshape(lead_shape + (total // width, width))


def _unpack(packed, shapes):
    flat = packed.reshape(-1)
    out, o = [], 0
    for s in shapes:
        n = 1
        for dim in s:
            n *= dim
        out.append(flat[o:o + n].reshape(s))
        o += n
    return out


def _as2d(a):
    return a.reshape(-1, a.shape[-1])


def kernel(x, p, ln_mix, w_in, w_in_vres, mu_shift, mu_shift_vres, conv_a_w, conv_a_b, lru_wx, lru_bx, lru_wa, lru_ba, lru_lambda, lru_norm, rwkv_w0, rwkv_w2, rwkv_a0, rwkv_a2, rwkv_v0, rwkv_v2, rwkv_g2, rwkv_kk, rwkv_ka, rwkv_rk, rwkv_lnx_w, rwkv_lnx_b, w_o, ln_ffn, w_gate, w_up, conv_f_w, conv_f_b, w_down, ln_ple, w_ple_gate, w_ple_proj, ln_ple_post, ln_final, loss_target, m_ln_mix, m_w_in, m_w_in_vres, m_mu_shift, m_mu_shift_vres, m_conv_a_w, m_conv_a_b, m_lru_wx, m_lru_bx, m_lru_wa, m_lru_ba, m_lru_lambda, m_lru_norm, m_rwkv_w0, m_rwkv_w2, m_rwkv_a0, m_rwkv_a2, m_rwkv_v0, m_rwkv_v2, m_rwkv_g2, m_rwkv_kk, m_rwkv_ka, m_rwkv_rk, m_rwkv_lnx_w, m_rwkv_lnx_b, m_w_o, m_ln_ffn, m_w_gate, m_w_up, m_conv_f_w, m_conv_f_b, m_w_down, m_ln_ple, m_w_ple_gate, m_w_ple_proj, m_ln_ple_post, m_ln_final, v_ln_mix, v_w_in, v_w_in_vres, v_mu_shift, v_mu_shift_vres, v_conv_a_w, v_conv_a_b, v_lru_wx, v_lru_bx, v_lru_wa, v_lru_ba, v_lru_lambda, v_lru_norm, v_rwkv_w0, v_rwkv_w2, v_rwkv_a0, v_rwkv_a2, v_rwkv_v0, v_rwkv_v2, v_rwkv_g2, v_rwkv_kk, v_rwkv_ka, v_rwkv_rk, v_rwkv_lnx_w, v_rwkv_lnx_b, v_w_o, v_ln_ffn, v_w_gate, v_w_up, v_conv_f_w, v_conv_f_b, v_w_down, v_ln_ple, v_w_ple_gate, v_w_ple_proj, v_ln_ple_post, v_ln_final):
    a = dict(locals())
    x2, p, tgt = a["x"][0], a["p"][:, 0], a["loss_target"][0]
    pos = tuple(lax.axis_index(ax).astype(jnp.int32).reshape(1) for ax in ("c", "x", "y"))

    wf = {k: a[k] for k in _REPLICATED}
    small_shapes = [a[k].shape for k in _SMALL_SHARDED]
    shards = {k: [a[k][i].astype(BF16) for i in range(a[k].shape[0])] for k in _BIG_SHARDED}
    packed = _pack([a[k] for k in _SMALL_SHARDED], 0, PACK_WIDTH, 16)
    got_small, got_w_in = _run_jobs([_gather_blob(packed), _gather_blob(shards["w_in"][0])], name="ag_first")
    got_small = got_small.reshape((N_XY,) + packed.shape)
    pieces = [_unpack(got_small[q], small_shapes) for q in range(N_XY)]
    for j, k in enumerate(_SMALL_SHARDED):
        wf[k] = _from_shards(jnp.stack([pieces[q][j] for q in range(N_XY)], axis=0), _SHARD_AXIS[k])

    m = _make_dims(x2, p, wf)
    w = _prepare_weights(m, wf)
    feed = _WeightFeed(m, w, shards, wf["w_in_vres"].astype(BF16))
    feed.arrive([("w_in", 0)], [got_w_in])
    def reducer(i, g, which):
        names = ("w_in",) if which == "late" else _BIG
        full = {k: (g["wcat"][:, :m.din] if k == "w_in" else g[k]) for k in names}
        return _GradReduce({k: _to_shards(full[k], _SHARD_AXIS[k] - 1) for k in names}, pos, f"{which}_{i}")

    loss_row, dx, grads, d_ln_final, reductions = _local_step(m, w, x2, p, tgt, feed, reducer)
    gfull = _unpack_grads(m, grads, d_ln_final, with_big=False)
    loss = lax.psum(loss_row[0, 0], ("x", "y", "c"))

    reduced = [{**early.run(), **late.run()} for early, late in reductions]
    gred = {k: jnp.stack([reduced[i][k] for i in range(m.nl)], axis=0).reshape(a[k].shape) for k in _BIG_SHARDED}
    gs = _pack([_to_shards(gfull[k], _SHARD_AXIS[k]) for k in _SMALL_SHARDED], 1, PACK_WIDTH, 32)
    g_small = _GradReduce({"small": gs}, pos, "small").run()["small"]
    rep_shapes = [a[k].shape for k in _REPLICATED]
    g_rep = _all_reduce_small(_pack([gfull[k] for k in _REPLICATED], 0, LANES_V7X, 8), name="ar_replicated")

    delta, new_m, new_v = {}, {}, {}
    for k in _BIG_SHARDED:
        res = _adamw(_as2d(a[k]), _as2d(gred[k]), _as2d(a["m_" + k]), _as2d(a["v_" + k]), name="adamw_" + k)
        delta[k], new_m[k], new_v[k] = (r.reshape(a[k].shape) for r in res)
    for names, shapes, g_packed, width, mult, tag in ((_SMALL_SHARDED, small_shapes, g_small, PACK_WIDTH, 32, "small"),
                                                      (_REPLICATED, rep_shapes, g_rep, LANES_V7X, 8, "replicated")):
        packs = [_pack([a[pre + k] for k in names], 0, width, mult) for pre in ("", "m_", "v_")]
        res = _adamw(packs[0], g_packed, packs[1], packs[2], name="adamw_" + tag)
        for dst, r in zip((gred, delta, new_m, new_v), [g_packed] + list(res)):
            dst.update(zip(names, _unpack(r, shapes)))
    return (loss, dx[None], *[gred[k] for k in _WEIGHTS], *[delta[k] for k in _WEIGHTS],
            *[new_m[k] for k in _WEIGHTS], *[new_v[k] for k in _WEIGHTS])
```

```python
import functools

import jax
import jax.numpy as jnp
from jax import lax
from jax.experimental import pallas as pl
from jax.experimental.pallas import tpu as pltpu

F32 = jnp.float32
BF16 = jnp.bfloat16
HIGHEST = lax.Precision.HIGHEST
MESH = pl.DeviceIdType.MESH

RMS_EPS = 1e-6
LNX_EPS = 64e-5
LRU_C = 8.0
ADAM_LR = 0.001
ADAM_B1 = 0.9
ADAM_B2 = 0.999
ADAM_EPS = 1e-08
ADAM_WD = 0.01
ADAM_STEP = 10

LANES_V7X = 128
VMEM_LIMIT_V7X = 60 * 1024 * 1024
WKV_CHUNK = 16
N_XY = 4
N_DEV = 8


def _cparams(sem=None, **kw):
    if sem is not None:
        kw["dimension_semantics"] = sem
    return pltpu.CompilerParams(vmem_limit_bytes=VMEM_LIMIT_V7X, **kw)


def _tile(dim, prefs):
    for t in prefs:
        if dim % t == 0:
            return t
    return dim


def _round_up(n, m):
    return (n + m - 1) // m * m


MM_MAX_TK = 2816


def _tile_k(kdim):
    best = None
    for t in range(LANES_V7X, min(kdim, MM_MAX_TK) + 1, LANES_V7X):
        if kdim % t == 0:
            best = t
    return best or kdim


def _mm(a, b, *, ta=False, tb=False, res=None, out_dtype=F32, name, gather=(), out_shards=0):
    if ta:
        kdim, m = a.shape
    else:
        m, kdim = a.shape
    bs = b.shape[0] if b.ndim == 3 else 0
    if bs:
        b_rows, b_cols = b.shape[1], bs * b.shape[2]
    else:
        b_rows, b_cols = b.shape
    n = b_rows if tb else b_cols
    assert (b_cols if tb else b_rows) == kdim
    per_shard = (lambda total, s: total // s if s else total)
    tk = _tile_k(per_shard(kdim, bs) if tb else kdim)
    tm = _tile(m, (2048, 1024, 512, 256, 128) if tk <= 2048 else (1024, 512, 256, 128))
    tn = _tile(per_shard(per_shard(n, out_shards), 0 if tb else bs), (512, 256, 128))
    nk = kdim // tk
    ni, nj = m // tm, n // tn
    a_spec = pl.BlockSpec((tk, tm), lambda i, j, k: (k, i)) if ta else pl.BlockSpec((tm, tk), lambda i, j, k: (i, k))
    if bs and tb:
        kps = b.shape[2] // tk
        b_spec = pl.BlockSpec((None, tn, tk), lambda i, j, k: (k // kps, j, k % kps))
    elif bs:
        nps = b.shape[2] // tn
        b_spec = pl.BlockSpec((None, tk, tn), lambda i, j, k: (j // nps, k, j % nps))
    else:
        b_spec = pl.BlockSpec((tn, tk), lambda i, j, k: (j, k)) if tb else pl.BlockSpec((tk, tn), lambda i, j, k: (k, j))
    if out_shards:
        assert res is None
        ops = n // out_shards // tn
        o_spec = pl.BlockSpec((None, tm, tn), lambda i, j, k: (j // ops, i, j % ops))
        o_shape = jax.ShapeDtypeStruct((out_shards, m, n // out_shards), out_dtype)
    else:
        o_spec = pl.BlockSpec((tm, tn), lambda i, j, k: (i, j))
        o_shape = jax.ShapeDtypeStruct((m, n), out_dtype)
    dn = (((0 if ta else 1,), (1 if tb else 0,)), ((), ()))
    has_res = res is not None
    ng = len(gather)
    nin = 2 + has_res

    def body(*refs):
        a_ref, b_ref = refs[:2]
        r_ref = refs[2] if has_res else None
        g_in, o_ref, g_out = refs[nin:nin + ng], refs[nin + ng], refs[nin + ng + 1:nin + 2 * ng + 1]
        scratch = refs[nin + 2 * ng + 1:]
        acc_ref = scratch[0] if nk > 1 else None
        g_sems = scratch[1 if nk > 1 else 0:]
        i, j, k = pl.program_id(0), pl.program_id(1), pl.program_id(2)

        if ng:
            @pl.when((i == 0) & (j == 0) & (k == 0))
            def _():
                _jobs_start(gather, g_in, g_out, g_sems)

        def finish(acc):
            if has_res:
                acc = acc + r_ref[...].astype(F32)
            o_ref[...] = acc.astype(out_dtype)

        prod = lax.dot_general(a_ref[...], b_ref[...], dn, preferred_element_type=F32)
        if nk == 1:
            finish(prod)
        else:
            @pl.when(k == 0)
            def _():
                acc_ref[...] = prod

            @pl.when(k > 0)
            def _():
                acc_ref[...] += prod

            @pl.when(k == nk - 1)
            def _():
                finish(acc_ref[...])

        if ng:
            @pl.when((i == ni - 1) & (j == nj - 1) & (k == nk - 1))
            def _():
                _jobs_finish(gather, g_in, g_out, g_sems)

    ins = [a, b] + ([res] if has_res else []) + _job_arrays(gather)
    in_specs = [a_spec, b_spec] + ([o_spec] if has_res else []) + [_ANY] * ng
    scratch = ([pltpu.VMEM((tm, tn), F32)] if nk > 1 else []) + _jobs_scratch(gather)
    sem = ("arbitrary",) * 3 if ng else ("parallel", "parallel", "arbitrary")
    out = pl.pallas_call(
        body, name=name, grid=(ni, nj, nk), in_specs=in_specs, out_specs=[o_spec] + [_ANY] * ng,
        out_shape=[o_shape] + [_job_out_shape(g) for g in gather],
        scratch_shapes=scratch, compiler_params=_cparams(sem),
    )(*ins)
    return (out[0], list(out[1:])) if ng else out[0]


def _stage_specs(axis, tile, tiled, params, consts, rows):
    specs = []
    for arr, width, cblk in tiled:
        if axis == 0:
            specs.append(pl.BlockSpec((tile, width), functools.partial(lambda i, c: (i, c), c=cblk)))
        else:
            specs.append(pl.BlockSpec((rows, tile), functools.partial(lambda i, c: (0, i + c), c=cblk)))
    for arr, cblk in params:
        if axis == 0:
            specs.append(pl.BlockSpec(arr.shape, functools.partial(lambda i, nd: (0,) * nd, nd=arr.ndim)))
        else:
            specs.append(pl.BlockSpec((arr.shape[0], tile), functools.partial(lambda i, c: (0, i + c), c=cblk)))
    for arr in consts:
        specs.append(pl.BlockSpec(arr.shape, functools.partial(lambda i, nd: (0,) * nd, nd=arr.ndim)))
    return specs


def _stage_fwd(fn, tiled, params, consts, outs, *, axis, tile, rows, name):
    nt, npar, nc = len(tiled), len(params), len(consts)
    ntiles = (rows // tile) if axis == 0 else (outs[0][0] // tile)

    def body(*refs):
        ins = refs[: nt + npar + nc]
        orefs = refs[nt + npar + nc:]
        vals = [r[...].astype(F32) for r in ins[: nt + npar]] + [r[...] for r in ins[nt + npar:]]
        ctx = pl.program_id(0) * tile
        res = fn(ctx, *vals)
        for o_ref, o in zip(orefs, res):
            o_ref[...] = o.astype(o_ref.dtype)

    if axis == 0:
        out_specs = [pl.BlockSpec((tile, w), lambda i: (i, 0)) for w, _ in outs]
    else:
        out_specs = [pl.BlockSpec((rows, tile), lambda i: (0, i)) for w, _ in outs]
    res = pl.pallas_call(
        body, name=name, grid=(ntiles,),
        in_specs=_stage_specs(axis, tile, tiled, params, consts, rows), out_specs=out_specs,
        out_shape=[jax.ShapeDtypeStruct((rows, w), dt) for w, dt in outs],
        compiler_params=_cparams(("arbitrary",)),
    )(*[t[0] for t in tiled], *[p[0] for p in params], *consts)
    return res


def _stage_bwd(fn, tiled, params, consts, cots, dtiled, *, axis, tile, rows, name, ncols=None):
    nt, npar, nc, nco = len(tiled), len(params), len(consts), len(cots)
    ntiles = (rows // tile) if axis == 0 else (ncols // tile)
    didx = [d[0] for d in dtiled]

    def body(*refs):
        ins = refs[: nt + npar + nc]
        crefs = refs[nt + npar + nc: nt + npar + nc + nco]
        orefs = refs[nt + npar + nc + nco:]
        vals = [r[...].astype(F32) for r in ins[: nt + npar]] + [r[...] for r in ins[nt + npar:]]
        ctx = pl.program_id(0) * tile

        def g(*dv):
            full = list(vals)
            for j, ix in enumerate(didx):
                full[ix] = dv[j]
            for j in range(npar):
                full[nt + j] = dv[len(didx) + j]
            return tuple(fn(ctx, *full))

        prim = [vals[ix] for ix in didx] + [vals[nt + j] for j in range(npar)]
        _, vjp = jax.vjp(g, *prim)
        grads = vjp(tuple(c[...].astype(F32) for c in crefs))
        for j in range(len(didx)):
            orefs[j][...] = grads[j].astype(orefs[j].dtype)
        for j in range(npar):
            o_ref = orefs[len(didx) + j]
            gp = grads[len(didx) + j]
            if axis == 0:
                @pl.when(pl.program_id(0) == 0)
                def _(o_ref=o_ref):
                    o_ref[...] = jnp.zeros_like(o_ref)

                o_ref[...] += gp
            else:
                o_ref[...] = gp

    if axis == 0:
        cot_specs = [pl.BlockSpec((tile, w), functools.partial(lambda i, c: (i, c), c=cb)) for _, w, cb in cots]
        out_specs = [pl.BlockSpec((tile, w), lambda i: (i, 0)) for _, w, _ in dtiled]
        out_specs += [pl.BlockSpec(p.shape, functools.partial(lambda i, nd: (0,) * nd, nd=p.ndim)) for p, _ in params]
        out_shape = [jax.ShapeDtypeStruct((rows, w), dt) for _, w, dt in dtiled]
        out_shape += [jax.ShapeDtypeStruct(p.shape, F32) for p, _ in params]
    else:
        cot_specs = [pl.BlockSpec((rows, tile), functools.partial(lambda i, c: (0, i + c), c=cb)) for _, cb in cots]
        out_specs = [pl.BlockSpec((rows, tile), lambda i: (0, i)) for _ in dtiled]
        out_specs += [pl.BlockSpec((p.shape[0], tile), lambda i: (0, i)) for p, _ in params]
        out_shape = [jax.ShapeDtypeStruct((rows, w), dt) for _, w, dt in dtiled]
        out_shape += [jax.ShapeDtypeStruct((p.shape[0], ncols), F32) for p, _ in params]
    return pl.pallas_call(
        body, name=name, grid=(ntiles,),
        in_specs=_stage_specs(axis, tile, tiled, params, consts, rows) + cot_specs, out_specs=out_specs,
        out_shape=out_shape, compiler_params=_cparams(("arbitrary",)),
    )(*[t[0] for t in tiled], *[p[0] for p in params], *consts, *[c[0] for c in cots])


def _rms(x, g):
    return x * lax.rsqrt(jnp.mean(x * x, axis=-1, keepdims=True) + RMS_EPS) * g


def _row_mask(x, k, first):
    t = lax.broadcasted_iota(jnp.int32, x.shape, 0)
    keep = (t >= k) if first else (t < x.shape[0] - k)
    return jnp.where(keep, x, 0.0)


@functools.partial(jax.custom_vjp, nondiff_argnums=(1,))
def _shift_down(x, k):
    return _row_mask(pltpu.roll(x, k, 0), k, True)


def _shift_down_fwd(x, k):
    return _shift_down(x, k), None


def _shift_down_bwd(k, _, g):
    return (_row_mask(pltpu.roll(g, g.shape[0] - k, 0), k, False),)


_shift_down.defvjp(_shift_down_fwd, _shift_down_bwd)


def _dwconv(x, w, b):
    kw = w.shape[0]
    out = x * w[kw - 1:kw] + b
    for j in range(kw - 1):
        out = out + _shift_down(x, kw - 1 - j) * w[j:j + 1]
    return out


def _f_norm(ctx, x, g):
    return (_rms(x, g),)


def _f_norm_res(ctx, x, g):
    return (_rms(x, g), x)


def _f_shiftmix(ctx, z, mu):
    return (z + (_shift_down(z, 1) - z) * mu,)


def _f_conv(ctx, x, w, b):
    return (_dwconv(x, w, b),)


def _f_ffn_act(ctx, gpre, up, w, b):
    return (jax.nn.gelu(_dwconv(gpre, w, b)) * up,)


def _make_f_lru_gates(heads):
    def fn(ctx, xb, wx, wa, bx, ba, lam):
        blk = xb.shape[1] // heads
        px, pa = [], []
        for h in range(heads):
            xh = xb[:, h * blk:(h + 1) * blk]
            px.append(jnp.dot(xh, wx[h], preferred_element_type=F32))
            pa.append(jnp.dot(xh, wa[h], preferred_element_type=F32))
        px = px[0] if heads == 1 else jnp.concatenate(px, axis=1)
        pa = pa[0] if heads == 1 else jnp.concatenate(pa, axis=1)
        gate_x = jax.nn.sigmoid(px + bx)
        gate_a = jax.nn.sigmoid(pa + ba)
        log_a = -LRU_C * gate_a * jax.nn.softplus(-lam)
        a = jnp.exp(log_a)
        mult = jnp.sqrt(1.0 - jnp.exp(2.0 * log_a))
        t = ctx + lax.broadcasted_iota(jnp.int32, xb.shape, 0)
        mult = jnp.where(t == 0, 1.0, mult)
        return a, xb * gate_x * mult

    return fn


def _f_lru_out(ctx, hl, ya, g):
    return (_rms(hl * jax.nn.gelu(ya), g),)


def _headsum_3pass(x, bb):
    hi = x.astype(BF16)
    r1 = x - hi.astype(F32)
    mid = r1.astype(BF16)
    lo = (r1 - mid.astype(F32)).astype(BF16)
    return (jnp.dot(hi, bb, preferred_element_type=F32) + jnp.dot(mid, bb, preferred_element_type=F32)
            + jnp.dot(lo, bb, preferred_element_type=F32))


@jax.custom_vjp
def _headsum(x, bb):
    return _headsum_3pass(x, bb)


def _headsum_fwd(x, bb):
    return _headsum_3pass(x, bb), bb


def _headsum_bwd(bb, g):
    return _headsum_3pass(g, bb), None


_headsum.defvjp(_headsum_fwd, _headsum_bwd)


def _make_f_rwkv_pre(has_vres, v_uses=0):
    def fn(ctx, *args):
        if v_uses:
            r, args = args[0], args[1:]
        if has_vres:
            k, v, lz, vf, w0, w2, a0, a2, g2, kkw, ka, v0, v2, bb = args
        else:
            k, v, lz, w0, w2, a0, a2, g2, kkw, ka, bb = args
        w_log = -jax.nn.softplus(-(w0 + jnp.dot(jnp.tanh(lz), w2, preferred_element_type=F32))) - 0.5
        logw = -jnp.exp(w_log)
        a = jax.nn.sigmoid(a0 + jnp.dot(lz, a2, preferred_element_type=F32))
        g = jnp.dot(jax.nn.sigmoid(lz), g2, preferred_element_type=F32)
        if has_vres:
            v = v + (vf - v) * jax.nn.sigmoid(v0 + jnp.dot(lz, v2, preferred_element_type=F32))
        xk = k * kkw
        kk = xk / jnp.maximum(jnp.sqrt(_headsum(xk * xk, bb)), 1e-12)
        k2 = k * (1.0 + (a - 1.0) * ka)
        if v_uses:
            return (r, r, logw, k2, k2) + (v,) * v_uses + (kk, kk * a, g)
        return logw, k2, v, kk, kk * a, g

    return fn


def _make_f_rwkv_post(head_size):
    def fn(ctx, y, r, k2, v2, g, lnw, lnb, rk, bb):
        mean = _headsum(y, bb) / head_size
        d = y - mean
        var = _headsum(d * d, bb) / head_size
        yn = d * lax.rsqrt(var + LNX_EPS) * lnw + lnb
        bonus = _headsum(r * k2 * rk, bb) * v2
        return ((yn + bonus) * g,)

    return fn


def _f_ple(ctx, h, eg, ep, g):
    return (h + _rms(jax.nn.sigmoid(eg) * ep, g),)


def _lru_scan(a, b, *, name):
    rows, cols = a.shape
    tc = _tile(cols, (512, 256, 128))

    def body(a_ref, b_ref, h_ref):
        def step(t, carry):
            h = a_ref[pl.ds(t, 1), :] * carry + b_ref[pl.ds(t, 1), :]
            h_ref[pl.ds(t, 1), :] = h
            return h

        lax.fori_loop(0, rows, step, jnp.zeros((1, tc), F32), unroll=8)

    spec = pl.BlockSpec((rows, tc), lambda j: (0, j))
    return pl.pallas_call(body, name=name, grid=(cols // tc,), in_specs=[spec, spec], out_specs=spec,
                          out_shape=jax.ShapeDtypeStruct((rows, cols), F32), compiler_params=_cparams(("arbitrary",)))(a, b)


def _lru_scan_bwd(a, h, dh, *, name):
    rows, cols = a.shape
    tc = _tile(cols, (512, 256, 128))

    def body(a_ref, h_ref, dh_ref, da_ref, db_ref):
        def step(i, carry):
            t = rows - 1 - i
            g = dh_ref[pl.ds(t, 1), :] + carry
            db_ref[pl.ds(t, 1), :] = g
            hp = h_ref[pl.ds(jnp.maximum(t - 1, 0), 1), :]
            da_ref[pl.ds(t, 1), :] = jnp.where(t > 0, g * hp, 0.0)
            return a_ref[pl.ds(t, 1), :] * g

        lax.fori_loop(0, rows, step, jnp.zeros((1, tc), F32), unroll=8)

    spec = pl.BlockSpec((rows, tc), lambda j: (0, j))
    return pl.pallas_call(body, name=name, grid=(cols // tc,), in_specs=[spec] * 3, out_specs=[spec] * 2,
                          out_shape=[jax.ShapeDtypeStruct((rows, cols), F32)] * 2,
                          compiler_params=_cparams(("arbitrary",)))(a, h, dh)


def _split_bf16(x):
    hi = x.astype(BF16)
    return hi, (x - hi.astype(F32)).astype(BF16)


def _dot3_passes(a, b, ca, cb):
    dn = (((ca,), (cb,)), ((), ()))
    ah, al = _split_bf16(a)
    bh, bl = _split_bf16(b)
    return (lax.dot_general(ah, bh, dn, preferred_element_type=F32) + lax.dot_general(al, bh, dn, preferred_element_type=F32)
            + lax.dot_general(ah, bl, dn, preferred_element_type=F32))


@functools.partial(jax.custom_vjp, nondiff_argnums=(2, 3))
def _dot3(a, b, ca, cb):
    return _dot3_passes(a, b, ca, cb)


def _dot3_fwd(a, b, ca, cb):
    return _dot3_passes(a, b, ca, cb), (a, b)


def _dot3_bwd(ca, cb, res, g):
    a, b = res
    fa, fb = 1 - ca, 1 - cb
    da = _dot3_passes(g, b, 1, fb) if ca == 1 else _dot3_passes(b, g, fb, 1)
    db = _dot3_passes(a, g, fa, 0) if cb == 0 else _dot3_passes(g, a, 0, fa)
    return da, db


_dot3.defvjp(_dot3_fwd, _dot3_bwd)


def _each(f, *lists):
    return [f(*t) for t in zip(*lists)]


def _wkv_local(r, lw, k, v, kk, b):
    c, n = r[0].shape
    row = lax.broadcasted_iota(jnp.int32, (c, c), 0)
    col = lax.broadcasted_iota(jnp.int32, (c, c), 1)
    incl = (row >= col).astype(F32)
    strict = (row > col).astype(F32)
    eye = lax.broadcasted_iota(jnp.int32, (n, n), 0) == lax.broadcasted_iota(jnp.int32, (n, n), 1)
    cl = _each(lambda x: _dot3(incl, x, 1, 0), lw)
    w_t = _each(jnp.exp, cl)
    inv_w = _each(lambda x: jnp.exp(-x), cl)
    kk_s = _each(lambda x, y, z: x * jnp.exp(y - z), kk, cl, lw)
    b_s = _each(jnp.multiply, b, inv_w)
    k_s = _each(jnp.multiply, k, inv_w)
    r_s = _each(jnp.multiply, r, w_t)
    q = _each(lambda x, y: jnp.concatenate([x, y], axis=0), kk_s, r_s)
    qb = _each(lambda x, y: _dot3(x, y, 1, 1), q, b_s)
    qk = _each(lambda x, y: _dot3(x, y, 1, 1), q, k_s)
    m = _each(lambda x: -strict * x[:c], qb)
    pb = _each(lambda x: incl * x[c:], qb)
    lkv = _each(lambda x, y: _dot3(strict * x[:c], y, 1, 0), qk, v)
    pkv = _each(lambda x, y: _dot3(incl * x[c:], y, 1, 0), qk, v)
    a = _each(lambda x, y: jnp.concatenate([x, y], axis=1), kk_s, lkv)
    steps = max(1, (c - 1).bit_length())
    for i in range(steps):
        a = _each(lambda x, y: y + _dot3(x, y, 1, 0), m, a)
        if i + 1 < steps:
            m = _each(lambda x: _dot3(x, x, 1, 0), m)
    ry = _each(lambda x, y, z, w: jnp.concatenate([x, y], axis=1) - _dot3(z, w, 1, 0), r_s, pkv, pb, a)
    w_end = _each(lambda x: x[c - 1:c, :], w_t)
    gu_low = _each(lambda x, y, z: _dot3(x, y * z, 0, 0), a, b_s, w_end)
    g = _each(lambda x, y: jnp.where(eye, jnp.broadcast_to(x, (n, n)), 0.0) - y[:n], w_end, gu_low)
    u = _each(lambda x, y, z, w: _dot3(x, y * z, 0, 0) - w[n:], v, k_s, w_end, gu_low)
    return g, u, _each(lambda x: x[:, :n], ry), _each(lambda x: x[:, n:], ry)


def _wkv_blocks(h, nchunk):
    return (_tile(h, (4, 2, 1)), _tile(nchunk, (4, 2, 1))), (h, _tile(nchunk, (4, 2, 1)))


def _wkv_fwd(r, lw, k, v, kk, b, *, name, gather=(), gather_state=()):
    h, t, n = r.shape
    c = WKV_CHUNK
    nchunk = t // c
    (hb, cb), (hs, cs) = _wkv_blocks(h, nchunk)
    ng = len(gather)
    ni, nj = h // hb, nchunk // cb

    pairs = [(i, j) for i in range(hb) for j in range(cb)]

    def local_body(*refs):
        ins, g_in = refs[:6], refs[6:6 + ng]
        g_ref, u_ref, r2_ref, y0_ref = refs[6 + ng:10 + ng]
        g_out, g_sems = refs[10 + ng:10 + 2 * ng], refs[10 + 2 * ng:]
        if ng:
            @pl.when((pl.program_id(0) == 0) & (pl.program_id(1) == 0))
            def _():
                _jobs_start(gather, g_in, g_out, g_sems)

        g, u, r2, y0 = _wkv_local(*[[ref[i, pl.ds(j * c, c)] for i, j in pairs] for ref in ins])
        for idx, (i, j) in enumerate(pairs):
            g_ref[i, j] = g[idx]
            u_ref[i, j] = u[idx]
            r2_ref[i, pl.ds(j * c, c)] = r2[idx]
            y0_ref[i, pl.ds(j * c, c)] = y0[idx]
        if ng:
            @pl.when((pl.program_id(0) == ni - 1) & (pl.program_id(1) == nj - 1))
            def _():
                _jobs_finish(gather, g_in, g_out, g_sems)

    seq = pl.BlockSpec((hb, cb * c, n), lambda i, j: (i, j, 0))
    mat = pl.BlockSpec((hb, cb, n, n), lambda i, j: (i, j, 0, 0))
    res = pl.pallas_call(
        local_body, name=name + "_local", grid=(ni, nj), in_specs=[seq] * 6 + [_ANY] * ng,
        out_specs=[mat, mat, seq, seq] + [_ANY] * ng,
        out_shape=[jax.ShapeDtypeStruct((h, nchunk, n, n), F32)] * 2 + [jax.ShapeDtypeStruct((h, t, n), F32)] * 2
        + [_job_out_shape(g) for g in gather],
        scratch_shapes=_jobs_scratch(gather),
        compiler_params=_cparams(("arbitrary", "arbitrary") if ng else ("parallel", "parallel")),
    )(r, lw, k, v, kk, b, *_job_arrays(gather))
    gm, um, r2, y0 = res[:4]
    gathered = list(res[4:])

    ng2 = len(gather_state)
    nsteps = nchunk // cs

    def state_body(*refs):
        g_ref, u_ref, r2_ref, y0_ref = refs[:4]
        g_in, (y_ref, st_ref) = refs[4:4 + ng2], refs[4 + ng2:6 + ng2]
        g_out, s_ref, g_sems = refs[6 + ng2:6 + 2 * ng2], refs[6 + 2 * ng2], refs[7 + 2 * ng2:]

        @pl.when(pl.program_id(0) == 0)
        def _():
            s_ref[...] = jnp.zeros_like(s_ref)
            _jobs_start(gather_state, g_in, g_out, g_sems)

        s = [s_ref[i] for i in range(hs)]
        for j in range(cs):
            rows = pl.ds(j * c, c)
            for i in range(hs):
                st_ref[i, j] = s[i]
                y_ref[i, rows] = _dot3(r2_ref[i, rows], s[i], 1, 1) + y0_ref[i, rows]
            s = [_dot3(s[i], g_ref[i, j], 1, 0) + u_ref[i, j] for i in range(hs)]
        for i in range(hs):
            s_ref[i] = s[i]
        if ng2:
            @pl.when(pl.program_id(0) == nsteps - 1)
            def _():
                _jobs_finish(gather_state, g_in, g_out, g_sems)

    seq = pl.BlockSpec((hs, cs * c, n), lambda j: (0, j, 0))
    mat = pl.BlockSpec((hs, cs, n, n), lambda j: (0, j, 0, 0))
    res = pl.pallas_call(
        state_body, name=name + "_state", grid=(nsteps,), in_specs=[mat, mat, seq, seq] + [_ANY] * ng2,
        out_specs=[seq, mat] + [_ANY] * ng2,
        out_shape=[jax.ShapeDtypeStruct((h, t, n), F32), jax.ShapeDtypeStruct((h, nchunk, n, n), F32)]
        + [_job_out_shape(g) for g in gather_state],
        scratch_shapes=[pltpu.VMEM((hs, n, n), F32)] + _jobs_scratch(gather_state), compiler_params=_cparams(("arbitrary",)),
    )(gm, um, r2, y0, *_job_arrays(gather_state))
    return res[0], (res[1], gm, r2), gathered + list(res[2:])


def _wkv_bwd(r, lw, k, v, kk, b, saved, dy, *, name, jobs_state=(), jobs_local=lambda state_results: ()):
    states, gm, r2 = saved
    h, t, n = r.shape
    c = WKV_CHUNK
    nchunk = t // c
    (hb, _), (hs, cs) = _wkv_blocks(h, nchunk)
    cb = _tile(nchunk, (8, 4, 2, 1))
    nsteps = nchunk // cs

    ns_ = len(jobs_state)

    def state_body(*refs):
        g_ref, r2_ref, st_ref, dy_ref = refs[:4]
        s_in, (dg_ref, du_ref, dr2_ref) = refs[4:4 + ns_], refs[4 + ns_:7 + ns_]
        s_out, ds_ref, s_sems = refs[7 + ns_:7 + 2 * ns_], refs[7 + 2 * ns_], refs[8 + 2 * ns_:]

        @pl.when(pl.program_id(0) == 0)
        def _():
            ds_ref[...] = jnp.zeros_like(ds_ref)
            _jobs_start(jobs_state, s_in, s_out, s_sems)

        ds = [ds_ref[i] for i in range(hs)]
        for j in reversed(range(cs)):
            rows = pl.ds(j * c, c)
            for i in range(hs):
                s0 = st_ref[i, j]
                du_ref[i, j] = ds[i]
                dg_ref[i, j] = _dot3(s0, ds[i], 0, 0)
                dr2_ref[i, rows] = _dot3(dy_ref[i, rows], s0, 1, 0)
            ds = [_dot3(dy_ref[i, rows], r2_ref[i, rows], 0, 0) + _dot3(ds[i], g_ref[i, j], 1, 1) for i in range(hs)]
        for i in range(hs):
            ds_ref[i] = ds[i]
        if ns_:
            @pl.when(pl.program_id(0) == nsteps - 1)
            def _():
                _jobs_finish(jobs_state, s_in, s_out, s_sems)

    seq = pl.BlockSpec((hs, cs * c, n), lambda j: (0, nsteps - 1 - j, 0))
    mat = pl.BlockSpec((hs, cs, n, n), lambda j: (0, nsteps - 1 - j, 0, 0))
    res = pl.pallas_call(
        state_body, name=name + "_state", grid=(nsteps,), in_specs=[mat, seq, mat, seq] + [_ANY] * ns_,
        out_specs=[mat, mat, seq] + [_ANY] * ns_,
        out_shape=[jax.ShapeDtypeStruct((h, nchunk, n, n), F32)] * 2 + [jax.ShapeDtypeStruct((h, t, n), F32)]
        + [_job_out_shape(j) for j in jobs_state],
        scratch_shapes=[pltpu.VMEM((hs, n, n), F32)] + _jobs_scratch(jobs_state), compiler_params=_cparams(("arbitrary",)),
    )(gm, r2, states, dy, *_job_arrays(jobs_state))
    dg, du, dr2 = res[:3]
    jobs = list(jobs_local(list(res[3:])))

    pairs = [(i, j) for i in range(hb) for j in range(cb)]
    nj_ = len(jobs)
    ni, nj = h // hb, nchunk // cb

    def local_body(*refs):
        ins, (dg_ref, du_ref, dr2_ref, dy_ref) = refs[:6], refs[6:10]
        j_in, out_refs, j_out, j_sems = refs[10:10 + nj_], refs[10 + nj_:16 + nj_], refs[16 + nj_:16 + 2 * nj_], refs[16 + 2 * nj_:]
        if nj_:
            @pl.when((pl.program_id(0) == 0) & (pl.program_id(1) == 0))
            def _():
                _jobs_start(jobs, j_in, j_out, j_sems)

        _, vjp = jax.vjp(_wkv_local, *[[ref[i, pl.ds(j * c, c)] for i, j in pairs] for ref in ins])
        grads = vjp(([dg_ref[i, j] for i, j in pairs], [du_ref[i, j] for i, j in pairs],
                     [dr2_ref[i, pl.ds(j * c, c)] for i, j in pairs], [dy_ref[i, pl.ds(j * c, c)] for i, j in pairs]))
        for o_ref, gr in zip(out_refs, grads):
            for idx, (i, j) in enumerate(pairs):
                o_ref[i, pl.ds(j * c, c)] = gr[idx]
        if nj_:
            @pl.when((pl.program_id(0) == ni - 1) & (pl.program_id(1) == nj - 1))
            def _():
                _jobs_finish(jobs, j_in, j_out, j_sems)

    seq = pl.BlockSpec((hb, cb * c, n), lambda i, j: (i, j, 0))
    mat = pl.BlockSpec((hb, cb, n, n), lambda i, j: (i, j, 0, 0))
    res = pl.pallas_call(
        local_body, name=name + "_local", grid=(ni, nj), in_specs=[seq] * 6 + [mat, mat, seq, seq] + [_ANY] * nj_,
        out_specs=[seq] * 6 + [_ANY] * nj_,
        out_shape=[jax.ShapeDtypeStruct((h, t, n), F32)] * 6 + [_job_out_shape(j) for j in jobs],
        scratch_shapes=_jobs_scratch(jobs),
        compiler_params=_cparams(("arbitrary", "arbitrary") if nj_ else ("parallel", "parallel")),
    )(r, lw, k, v, kk, b, dg, du, dr2, dy, *_job_arrays(jobs))
    return list(res[:6]), list(res[6:])


class _Dims:
    pass


def _make_dims(x, p, w):
    m = _Dims()
    m.t, m.d = x.shape[-2], x.shape[-1]
    m.nl = w["ln_mix"].shape[0]
    m.dl = w["conv_a_b"].shape[1]
    m.hl = w["lru_wx"].shape[1]
    m.dr = w["rwkv_w0"].shape[1]
    m.h, m.n = w["rwkv_rk"].shape[1], w["rwkv_rk"].shape[2]
    m.lw, m.la, m.lg, m.lv = (w[k].shape[1] for k in ("rwkv_w2", "rwkv_a2", "rwkv_g2", "rwkv_v2"))
    m.nsh = w["mu_shift"].shape[1]
    m.ff = w["conv_f_b"].shape[1]
    m.ple = p.shape[-1]
    m.din = 2 * m.dl + m.nsh
    m.lz = _round_up(m.lw + m.la + m.lg + m.lv, LANES_V7X)
    m.zw = _round_up(2 * m.dl + 3 * m.dr + m.lz, 512)
    m.zs = m.zw - 2 * m.dl
    m.tr = _tile(m.t, (256, 128, 64, 32, 16, 8))
    m.trb = _tile(m.t, (128, 64, 32, 16, 8))
    m.tcs = _tile(m.zs, (512, 256, 128))
    m.dw_shards = 0
    assert (3 * m.dr) % m.lz == 0 and (2 * m.dl) % m.tcs == 0 and m.t % WKV_CHUNK == 0
    assert m.nsh == 3 * m.dr + m.lw + m.la + m.lg
    return m


def _to_heads(m, a):
    return jnp.transpose(a.reshape(m.t, m.h, m.n), (1, 0, 2))


def _from_heads(m, a):
    return jnp.transpose(a, (1, 0, 2)).reshape(m.t, m.dr)


def _norm_fwd(m, h, g, name):
    return _stage_fwd(_f_norm, [(h, m.d, 0)], [(g, 0)], [], [(m.d, BF16)], axis=0, tile=m.tr, rows=m.t, name=name)[0]


def _norm_bwd(m, h, g, du, dres, name):
    return _stage_bwd(_f_norm_res, [(h, m.d, 0)], [(g, 0)], [], [(du, m.d, 0), (dres, m.d, 0)], [(0, m.d, F32)],
                      axis=0, tile=m.tr, rows=m.t, name=name)


def _rwkv_pre_operands(m, w, i, sv, v_first_zs, with_r):
    zs = sv["zs"]
    tiled = ([(zs, m.dr, 0)] if with_r else []) + [(zs, m.dr, 1), (zs, m.dr, 2), (zs, m.lz, 3 * m.dr // m.lz)]
    params = [(w["rwkv_w0"][i:i + 1], 0), (w["w2p"][i], 0), (w["rwkv_a0"][i:i + 1], 0), (w["a2p"][i], 0),
              (w["g2p"][i], 0), (w["rwkv_kk"][i:i + 1], 0), (w["rwkv_ka"][i:i + 1], 0)]
    if i > 0:
        tiled.append((v_first_zs, m.dr, 2))
        params += [(w["rwkv_v0"][i - 1:i], 0), (w["v2p"][i - 1], 0)]
    return tiled, params


def _rwkv_post_operands(m, w, i, sv):
    tiled = [(sv["y"], m.dr, 0), (sv["zs"], m.dr, 0), (sv["k2"], m.dr, 0), (sv["v2"], m.dr, 0), (sv["g"], m.dr, 0)]
    params = [(w["rwkv_lnx_w"][i:i + 1], 0), (w["rwkv_lnx_b"][i:i + 1], 0), (w["rk"][i], 0)]
    return tiled, params


def _lru_gate_params(w, i):
    return [(w["lru_wx"][i], 0), (w["lru_wa"][i], 0), (w["lru_bx"][i:i + 1], 0), (w["lru_ba"][i:i + 1], 0),
            (w["lru_lambda"][i:i + 1], 0)]


_COLUMN_SHARDED_OPERANDS = ("w_gate", "w_up", "w_ple_proj")


class _WeightFeed:
    def __init__(self, m, w, shards, vres):
        self.m, self.w, self.shards, self.vres = m, w, shards, vres

    def keys(self, carrier, i):
        plan = {"mm_in": [("w_o", i)] if i == 0 else [],
                "wkv_local": [("w_gate", i), ("w_up", i)], "wkv_state": [("w_down", i)],
                "mm_gate": [("w_ple_gate", i)], "mm_up": [("w_ple_proj", i), ("w_o", i + 1)],
                "mm_down": [("w_in", i + 1)], "mm_pgate": []}
        return [key for key in plan[carrier] if key[1] < self.m.nl]

    def blobs(self, keys):
        return [_gather_blob(self.shards[name][layer]) for name, layer in keys]

    def arrive(self, keys, gathered):
        m = self.m
        for (name, layer), got in zip(keys, gathered):
            full = got.reshape((N_XY,) + self.shards[name][layer].shape)
            if name in _COLUMN_SHARDED_OPERANDS:
                self.w[name][layer] = full
                continue
            full = _from_shards(full, _SHARD_AXIS[name] - 1)
            if name == "w_in":
                vres = self.vres[layer - 1] if layer > 0 else jnp.zeros((m.d, m.lv), BF16)
                self.w["wcat"][layer] = jnp.concatenate([full, vres, jnp.zeros((m.d, m.zw - m.din - m.lv), BF16)], axis=1)
            else:
                self.w[name][layer] = full


def _mm_fed(feed, carrier, i, a, b, **kw):
    keys = feed.keys(carrier, i) if feed is not None else []
    if not keys:
        return _mm(a, b, **kw)
    out, got = _mm(a, b, gather=feed.blobs(keys), **kw)
    feed.arrive(keys, got)
    return out


def _layer_fwd(m, w, i, h, p_bf, v_first_zs, feed=None):
    sv = {"h": h}
    t, dl, dr = m.t, m.dl, m.dr
    sv["u1"] = _norm_fwd(m, h, w["ln_mix"][i:i + 1], "norm_mix")
    z = sv["z"] = _mm_fed(feed, "mm_in", i, sv["u1"], w["wcat"][i], name="mm_in")
    off = 2 * dl // m.tcs
    sv["zs"] = _stage_fwd(_f_shiftmix, [(z, None, off)], [(w["mu_pad"][i], off)], [], [(m.zs, F32)],
                          axis=1, tile=m.tcs, rows=t, name="shiftmix")[0]
    tca = _tile(dl, (512, 256, 128))
    sv["xb"] = _stage_fwd(_f_conv, [(z, None, 0)], [(w["conv_a_w"][i], 0), (w["conv_a_b"][i:i + 1], 0)], [],
                          [(dl, F32)], axis=1, tile=tca, rows=t, name="conv_a")[0]
    sv["a"], b_in = _stage_fwd(_make_f_lru_gates(m.hl), [(sv["xb"], dl, 0)], _lru_gate_params(w, i), [],
                               [(dl, F32), (dl, F32)], axis=0, tile=m.tr, rows=t, name="lru_gates")
    sv["hl"] = _lru_scan(sv["a"], b_in, name="lru_scan")
    out_a = _stage_fwd(_f_lru_out, [(sv["hl"], dl, 0), (z, dl, 1)], [(w["lru_norm"][i:i + 1], 0)], [],
                       [(dl, BF16)], axis=0, tile=m.tr, rows=t, name="lru_out")[0]
    tiled, params = _rwkv_pre_operands(m, w, i, sv, v_first_zs, False)
    pre = _stage_fwd(_make_f_rwkv_pre(i > 0), tiled, params, [w["bb"]], [(dr, F32)] * 6,
                     axis=0, tile=m.tr, rows=t, name="rwkv_pre")
    sv["logw"], sv["k2"], sv["v2"], sv["kk"], sv["b"], sv["g"] = pre
    heads = [_to_heads(m, a) for a in (sv["zs"][:, :dr], sv["logw"], sv["k2"], sv["v2"], sv["kk"], sv["b"])]
    keys = [feed.keys(carrier, i) if feed is not None else [] for carrier in ("wkv_local", "wkv_state")]
    y_h, sv["states"], got = _wkv_fwd(*heads, name="wkv_fwd", gather=feed.blobs(keys[0]) if keys[0] else (),
                                      gather_state=feed.blobs(keys[1]) if keys[1] else ())
    if keys[0] or keys[1]:
        feed.arrive(keys[0] + keys[1], got)
    sv["y"] = _from_heads(m, y_h)
    tiled, params = _rwkv_post_operands(m, w, i, sv)
    out_b = _stage_fwd(_make_f_rwkv_post(m.n), tiled, params, [w["bb"]], [(dr, BF16)],
                       axis=0, tile=m.tr, rows=t, name="rwkv_post")[0]
    sv["cat"] = jnp.concatenate([out_a, out_b], axis=1)
    h2 = sv["h2"] = _mm(sv["cat"], w["w_o"][i], res=h, name="mm_o")
    sv["u2"] = _norm_fwd(m, h2, w["ln_ffn"][i:i + 1], "norm_ffn")
    sv["gpre"] = _mm_fed(feed, "mm_gate", i, sv["u2"], w["w_gate"][i], name="mm_gate")
    sv["up"] = _mm_fed(feed, "mm_up", i, sv["u2"], w["w_up"][i], name="mm_up")
    tcf = _tile(m.ff, (512, 256, 128))
    sv["act"] = _stage_fwd(_f_ffn_act, [(sv["gpre"], None, 0), (sv["up"], None, 0)],
                           [(w["conv_f_w"][i], 0), (w["conv_f_b"][i:i + 1], 0)], [], [(m.ff, BF16)],
                           axis=1, tile=tcf, rows=t, name="ffn_act")[0]
    h3 = sv["h3"] = _mm_fed(feed, "mm_down", i, sv["act"], w["w_down"][i], res=h2, name="mm_down")
    sv["u3"] = _norm_fwd(m, h3, w["ln_ple"][i:i + 1], "norm_ple")
    sv["eg"] = _mm_fed(feed, "mm_pgate", i, sv["u3"], w["w_ple_gate"][i], name="mm_pgate")
    sv["ep"] = _mm(p_bf, w["w_ple_proj"][i], name="mm_pproj")
    h4 = _stage_fwd(_f_ple, [(h3, m.d, 0), (sv["eg"], m.d, 0), (sv["ep"], m.d, 0)], [(w["ln_ple_post"][i:i + 1], 0)],
                    [], [(m.d, F32)], axis=0, tile=m.tr, rows=t, name="ple")[0]
    return h4, sv


def _layer_bwd(m, w, i, dh4, sv, p_bf, v_first_zs, dvf_in, pending=None, early=None):
    t, d, dl, dr = m.t, m.d, m.dl, m.dr
    g = {}
    deg, dep, g["ln_ple_post"] = _stage_bwd(
        _f_ple, [(sv["h3"], d, 0), (sv["eg"], d, 0), (sv["ep"], d, 0)], [(w["ln_ple_post"][i:i + 1], 0)], [],
        [(dh4, d, 0)], [(1, d, BF16), (2, d, BF16)], axis=0, tile=m.tr, rows=t, name="ple_bwd")
    du3 = _mm(deg, w["w_ple_gate"][i], tb=True, name="mm_pgate_dx")
    g["w_ple_gate"] = _mm(sv["u3"], deg, ta=True, name="mm_pgate_dw")
    g["w_ple_proj"] = _mm(p_bf, dep, ta=True, name="mm_pproj_dw", out_shards=m.dw_shards)
    dh3, g["ln_ple"] = _norm_bwd(m, sv["h3"], w["ln_ple"][i:i + 1], du3, dh4, "norm_ple_bwd")
    dh3_bf = dh3.astype(BF16)
    dact = _mm(dh3_bf, w["w_down"][i], tb=True, name="mm_down_dx")
    if pending is None:
        g["w_down"] = _mm(sv["act"], dh3_bf, ta=True, name="mm_down_dw")
    else:
        g["w_down"], gots = _mm(sv["act"], dh3_bf, ta=True, name="mm_down_dw", gather=pending.send_jobs())
        pending.after_send(gots)
    tcf = _tile(m.ff, (512, 256, 128))
    dgpre, dup, g["conv_f_w"], g["conv_f_b"] = _stage_bwd(
        _f_ffn_act, [(sv["gpre"], None, 0), (sv["up"], None, 0)], [(w["conv_f_w"][i], 0), (w["conv_f_b"][i:i + 1], 0)],
        [], [(dact, 0)], [(0, m.ff, BF16), (1, m.ff, BF16)], axis=1, tile=tcf, rows=t, ncols=m.ff, name="ffn_act_bwd")
    du2 = _mm(dgpre, w["w_gate"][i], tb=True, name="mm_gate_dx")
    du2 = _mm(dup, w["w_up"][i], tb=True, res=du2, name="mm_up_dx")
    g["w_gate"] = _mm(sv["u2"], dgpre, ta=True, name="mm_gate_dw", out_shards=m.dw_shards)
    g["w_up"] = _mm(sv["u2"], dup, ta=True, name="mm_up_dw", out_shards=m.dw_shards)
    dh2, g["ln_ffn"] = _norm_bwd(m, sv["h2"], w["ln_ffn"][i:i + 1], du2, dh3, "norm_ffn_bwd")
    dh2_bf = dh2.astype(BF16)
    dcat = _mm(dh2_bf, w["w_o"][i], tb=True, name="mm_o_dx")
    g["w_o"] = _mm(sv["cat"], dh2_bf, ta=True, name="mm_o_dw")
    tiled, params = _rwkv_post_operands(m, w, i, sv)
    dy, dr_a, dk2_a, dv2_a, dg, g["rwkv_lnx_w"], g["rwkv_lnx_b"], g["rk"] = _stage_bwd(
        _make_f_rwkv_post(m.n), tiled, params, [w["bb"]], [(dcat, dr, dl // dr)], [(j, dr, F32) for j in range(5)],
        axis=0, tile=m.trb, rows=t, name="rwkv_post_bwd")
    heads = [_to_heads(m, a) for a in (sv["zs"][:, :dr], sv["logw"], sv["k2"], sv["v2"], sv["kk"], sv["b"])]
    own = early(g) if early is not None else None

    def local_jobs(state_results):
        jobs = []
        if own is not None:
            own.after_send(state_results)
            jobs += own.exchange_jobs()
        if pending is not None:
            jobs += pending.exchange_jobs()
        return jobs

    dwkv, parts = _wkv_bwd(*heads, sv["states"], _to_heads(m, dy), name="wkv_bwd",
                           jobs_state=own.send_jobs() if own is not None else (), jobs_local=local_jobs)
    n_own = len(own.names) if own is not None else 0
    if own is not None:
        own.after_exchange(parts[:n_own])
    if pending is not None:
        pending.after_exchange(parts[n_own:])
    dr_b, dlw, dk2_b, dv2_b, dkk, db = [_from_heads(m, a) for a in dwkv]
    tiled, params = _rwkv_pre_operands(m, w, i, sv, v_first_zs, True)
    v_cots = [dv2_a, dv2_b] + ([dvf_in] if dvf_in is not None else [])
    cots = [(c, dr, 0) for c in [dr_a, dr_b, dlw, dk2_a, dk2_b] + v_cots + [dkk, db, dg]]
    ntil = len(tiled)
    dtiled = [(0, dr, F32), (1, dr, F32), (2, dr, F32), (3, m.lz, F32)] + ([(4, dr, F32)] if i > 0 else [])
    res = _stage_bwd(_make_f_rwkv_pre(i > 0, len(v_cots)), tiled, params, [w["bb"]], cots, dtiled,
                     axis=0, tile=m.trb, rows=t, name="rwkv_pre_bwd")
    d_r, d_k, d_v, d_lz = res[:4]
    dvf_out = res[4] if i > 0 else None
    pg = res[ntil:]
    g["rwkv_w0"], g["w2p"], g["rwkv_a0"], g["a2p"], g["g2p"], g["rwkv_kk"], g["rwkv_ka"] = pg[:7]
    if i > 0:
        g["rwkv_v0"], g["v2p"] = pg[7:9]
    dzs = jnp.concatenate([d_r, d_k, d_v, d_lz, jnp.zeros((t, m.zs - 3 * dr - m.lz), F32)], axis=1)
    off = 2 * dl // m.tcs
    dzr, g["mu_pad"] = _stage_bwd(_f_shiftmix, [(sv["z"], None, off)], [(w["mu_pad"][i], off)], [], [(dzs, 0)],
                                  [(0, m.zs, BF16)], axis=1, tile=m.tcs, rows=t, ncols=m.zs, name="shiftmix_bwd")
    dhl, dya, g["lru_norm"] = _stage_bwd(
        _f_lru_out, [(sv["hl"], dl, 0), (sv["z"], dl, 1)], [(w["lru_norm"][i:i + 1], 0)], [], [(dcat, dl, 0)],
        [(0, dl, F32), (1, dl, BF16)], axis=0, tile=m.tr, rows=t, name="lru_out_bwd")
    da, db_in = _lru_scan_bwd(sv["a"], sv["hl"], dhl, name="lru_scan_bwd")
    dxb, g["lru_wx"], g["lru_wa"], g["lru_bx"], g["lru_ba"], g["lru_lambda"] = _stage_bwd(
        _make_f_lru_gates(m.hl), [(sv["xb"], dl, 0)], _lru_gate_params(w, i), [], [(da, dl, 0), (db_in, dl, 0)],
        [(0, dl, F32)], axis=0, tile=m.tr, rows=t, name="lru_gates_bwd")
    tca = _tile(dl, (512, 256, 128))
    dxa, g["conv_a_w"], g["conv_a_b"] = _stage_bwd(
        _f_conv, [(sv["z"], None, 0)], [(w["conv_a_w"][i], 0), (w["conv_a_b"][i:i + 1], 0)], [], [(dxb, 0)],
        [(0, dl, BF16)], axis=1, tile=tca, rows=t, ncols=dl, name="conv_a_bwd")
    dz = jnp.concatenate([dxa, dya, dzr], axis=1)
    du1 = _mm(dz, w["wcat"][i], tb=True, name="mm_in_dx")
    g["wcat"] = _mm(sv["u1"], dz, ta=True, name="mm_in_dw")
    dh, g["ln_mix"] = _norm_bwd(m, sv["h"], w["ln_mix"][i:i + 1], du1, dh2, "norm_mix_bwd")
    return dh, g, dvf_out, own


def _loss_head(m, h, g, tgt):
    tile, d = m.tr, m.d

    def body(h_ref, g_ref, t_ref, loss_ref, dh_ref, dg_ref):
        def f(hv, gv):
            err = _rms(hv, gv) - t_ref[...]
            return 0.5 * jnp.sum(jnp.mean(err * err, axis=-1))

        val, vjp = jax.vjp(f, h_ref[...], g_ref[...])
        dh, dg = vjp(jnp.ones((), F32))
        dh_ref[...] = dh

        @pl.when(pl.program_id(0) == 0)
        def _():
            dg_ref[...] = jnp.zeros_like(dg_ref)
            loss_ref[...] = jnp.zeros_like(loss_ref)

        dg_ref[...] += dg
        loss_ref[...] += jnp.full(loss_ref.shape, val, F32)

    row = pl.BlockSpec((tile, d), lambda i: (i, 0))
    return pl.pallas_call(
        body, name="loss_head", grid=(m.t // tile,),
        in_specs=[row, pl.BlockSpec((1, d), lambda i: (0, 0)), row],
        out_specs=[pl.BlockSpec((1, LANES_V7X), lambda i: (0, 0)), row, pl.BlockSpec((1, d), lambda i: (0, 0))],
        out_shape=[jax.ShapeDtypeStruct((1, LANES_V7X), F32), jax.ShapeDtypeStruct((m.t, d), F32),
                   jax.ShapeDtypeStruct((1, d), F32)],
        compiler_params=_cparams(("arbitrary",)),
    )(h, g, tgt)


def _local_step(m, w, x, p, tgt, feed=None, reducer=None):
    h = x
    saved = []
    p_bf = p.astype(BF16)
    for i in range(m.nl):
        h, sv = _layer_fwd(m, w, i, h, p_bf[i], saved[0]["zs"] if i > 0 else None, feed)
        saved.append(sv)
    loss_row, dh, d_ln_final = _loss_head(m, h, w["ln_final"], tgt)
    grads = [None] * m.nl
    reductions = [None] * m.nl
    dvf = None
    for i in reversed(range(m.nl)):
        pending = reductions[i + 1][1] if reducer is not None and i + 1 < m.nl else None
        early = functools.partial(reducer, i, which="early") if reducer is not None else None
        dh, grads[i], dvf_i, own = _layer_bwd(m, w, i, dh, saved[i], p_bf[i], saved[0]["zs"] if i > 0 else None,
                                              dvf if i == 0 else None, pending, early)
        if reducer is not None:
            reductions[i] = (own, reducer(i, grads[i], which="late"))
        if i > 0:
            dvf = dvf_i if dvf is None else dvf + dvf_i
    return loss_row, dh, grads, d_ln_final, reductions


_BIG = ("w_o", "w_gate", "w_up", "w_down", "w_ple_gate", "w_ple_proj")


def _lora_rows(m):
    o1 = m.lw
    o2 = o1 + m.la
    o3 = o2 + m.lg
    return {"w2p": (0, o1), "a2p": (o1, o2), "g2p": (o2, o3), "v2p": (o3, o3 + m.lv)}


def _prepare_weights(m, wf):
    w = {k: v for k, v in wf.items() if k not in _BIG and k not in ("w_in", "w_in_vres")}
    nl = m.nl
    for k in _BIG:
        w[k] = [wf[k][i].astype(BF16) for i in range(nl)] if k in wf else [None] * nl
    w["wcat"] = [None] * nl
    if "w_in" in wf:
        vres = jnp.concatenate([jnp.zeros((1, m.d, m.lv), BF16), wf["w_in_vres"].astype(BF16)], axis=0)
        pad = jnp.zeros((m.d, m.zw - m.din - m.lv), BF16)
        w["wcat"] = [jnp.concatenate([wf["w_in"][i].astype(BF16), vres[i], pad], axis=1) for i in range(nl)]
    mu_v = jnp.concatenate([jnp.zeros((1, m.lv), F32), wf["mu_shift_vres"]], axis=0)
    w["mu_pad"] = jnp.concatenate([jnp.zeros((nl, 2 * m.dl), F32), wf["mu_shift"], mu_v,
                                   jnp.zeros((nl, m.zw - m.din - m.lv), F32)], axis=1)[:, None, :]
    rows = _lora_rows(m)
    for name, src in (("w2p", "rwkv_w2"), ("a2p", "rwkv_a2"), ("g2p", "rwkv_g2"), ("v2p", "rwkv_v2")):
        lo, hi = rows[name]
        a = wf[src]
        w[name] = jnp.concatenate([jnp.zeros((a.shape[0], lo, m.dr), F32), a, jnp.zeros((a.shape[0], m.lz - hi, m.dr), F32)],
                                  axis=1)
    w["rk"] = wf["rwkv_rk"].reshape(nl, 1, m.dr)
    w["ln_final"] = wf["ln_final"].reshape(1, m.d)
    head = jnp.arange(m.dr, dtype=jnp.int32) // m.n
    w["bb"] = (head[:, None] == head[None, :]).astype(BF16)
    return w


def _unpack_grads(m, grads, d_ln_final, with_big=True):
    nl = m.nl
    out = {}

    def stack(key):
        return jnp.stack([grads[i][key] for i in range(nl)], axis=0)

    for k in (_BIG if with_big else ()) + ("conv_a_w", "conv_f_w", "lru_wx", "lru_wa"):
        out[k] = stack(k)
    for k in ("ln_mix", "conv_a_b", "lru_bx", "lru_ba", "lru_lambda", "lru_norm", "rwkv_w0", "rwkv_a0", "rwkv_kk",
              "rwkv_ka", "rwkv_lnx_w", "rwkv_lnx_b", "ln_ffn", "conv_f_b", "ln_ple", "ln_ple_post"):
        out[k] = stack(k)[:, 0, :]
    if with_big:
        out["w_in"] = stack("wcat")[:, :, :m.din]
    out["w_in_vres"] = jnp.stack([grads[i]["wcat"][:, m.din:m.din + m.lv] for i in range(1, nl)], axis=0)
    mu = stack("mu_pad")[:, 0, :]
    out["mu_shift"] = mu[:, :m.nsh]
    out["mu_shift_vres"] = mu[1:, m.nsh:m.nsh + m.lv]
    rows = _lora_rows(m)
    for name, dst in (("w2p", "rwkv_w2"), ("a2p", "rwkv_a2"), ("g2p", "rwkv_g2")):
        lo, hi = rows[name]
        out[dst] = stack(name)[:, lo:hi, :]
    lo, hi = rows["v2p"]
    out["rwkv_v2"] = jnp.stack([grads[i]["v2p"] for i in range(1, nl)], axis=0)[:, lo:hi, :]
    out["rwkv_v0"] = jnp.stack([grads[i]["rwkv_v0"] for i in range(1, nl)], axis=0)[:, 0, :]
    out["rwkv_rk"] = stack("rk").reshape(nl, m.h, m.n)
    out["ln_final"] = d_ln_final.reshape(m.d)
    return out


_ANY = pl.BlockSpec(memory_space=pl.ANY)


def _position():
    return lax.axis_index("x"), lax.axis_index("y"), lax.axis_index("c")


def _other_chips(x, y):
    return [(1 - x, y), (x, 1 - y), (1 - x, 1 - y)]


def _gather_blob(shard):
    rows, wd = shard.shape
    return shard.reshape(2, rows // 2, wd)


_JOB_SEMS = {"gather": 7, "pair_send": 1, "exchange": 3}


def _job_parts(job):
    return job if isinstance(job, tuple) else ("gather", job)


def _job_arrays(jobs):
    return [_job_parts(j)[1] for j in jobs]


def _job_out_shape(job):
    kind, arr = _job_parts(job)
    if kind == "gather":
        return jax.ShapeDtypeStruct((N_XY,) + arr.shape, arr.dtype)
    if kind == "pair_send":
        return jax.ShapeDtypeStruct((arr.shape[0], arr.shape[1] // 2, arr.shape[2]), arr.dtype)
    return jax.ShapeDtypeStruct(arr.shape, arr.dtype)


def _jobs_scratch(jobs):
    out = []
    for j in jobs:
        n = _JOB_SEMS[_job_parts(j)[0]]
        out += [pltpu.SemaphoreType.DMA((n,)), pltpu.SemaphoreType.DMA((n,))]
    return out


def _jobs_start(jobs, in_refs, out_refs, sems):
    for q, j in enumerate(jobs):
        _JOB_START[_job_parts(j)[0]](in_refs[q], out_refs[q], sems[2 * q], sems[2 * q + 1])


def _jobs_finish(jobs, in_refs, out_refs, sems):
    for q, j in enumerate(jobs):
        _JOB_FINISH[_job_parts(j)[0]](in_refs[q], out_refs[q], sems[2 * q], sems[2 * q + 1])


def _run_jobs(jobs, *, name):
    n = len(jobs)

    def body(*refs):
        _jobs_start(jobs, refs[:n], refs[n:2 * n], refs[2 * n:])
        _jobs_finish(jobs, refs[:n], refs[n:2 * n], refs[2 * n:])

    return pl.pallas_call(body, name=name, in_specs=[_ANY] * n, out_specs=[_ANY] * n,
                          out_shape=[_job_out_shape(j) for j in jobs], scratch_shapes=_jobs_scratch(jobs))(*_job_arrays(jobs))


def _gather_copies(in_ref, out_ref, send_sems, recv_sems):
    x, y, c = _position()
    me = 2 * x + y
    sends, hands, ici_in, d2d_in = [], [], [], []
    for k, (px, py) in enumerate(_other_chips(x, y)):
        landed = out_ref.at[2 * px + py, c]
        sends.append(pltpu.make_async_remote_copy(
            src_ref=in_ref.at[c], dst_ref=out_ref.at[me, c], send_sem=send_sems.at[k], recv_sem=recv_sems.at[k],
            device_id=(px, py, c), device_id_type=MESH))
        ici_in.append(pltpu.make_async_remote_copy(
            src_ref=in_ref.at[c], dst_ref=landed, send_sem=send_sems.at[k], recv_sem=recv_sems.at[k],
            device_id=(px, py, c), device_id_type=MESH))
        hands.append(pltpu.make_async_remote_copy(
            src_ref=landed, dst_ref=landed, send_sem=send_sems.at[3 + k], recv_sem=recv_sems.at[3 + k],
            device_id=(x, y, 1 - c), device_id_type=MESH))
        d2d_in.append(pltpu.make_async_remote_copy(
            src_ref=in_ref.at[c], dst_ref=out_ref.at[2 * px + py, 1 - c], send_sem=send_sems.at[3 + k],
            recv_sem=recv_sems.at[3 + k], device_id=(x, y, 1 - c), device_id_type=MESH))
    own = pltpu.make_async_remote_copy(src_ref=in_ref, dst_ref=out_ref.at[me], send_sem=send_sems.at[6],
                                       recv_sem=recv_sems.at[6], device_id=(x, y, 1 - c), device_id_type=MESH)
    sends.append(own)
    d2d_in.append(own)
    return sends, hands, ici_in, d2d_in


def _gather_start(in_ref, out_ref, send_sems, recv_sems):
    for cp in _gather_copies(in_ref, out_ref, send_sems, recv_sems)[0]:
        cp.start()


def _gather_finish(in_ref, out_ref, send_sems, recv_sems):
    sends, hands, ici_in, d2d_in = _gather_copies(in_ref, out_ref, send_sems, recv_sems)
    for arrived, hand in zip(ici_in, hands):
        arrived.wait_recv()
        hand.start()
    for arrived in d2d_in:
        arrived.wait_recv()
    for cp in sends + hands:
        cp.wait_send()


def _pair_send_copy(g_ref, out_ref, send_sems, recv_sems):
    x, y, c = _position()
    half = out_ref.shape[1]
    return pltpu.make_async_remote_copy(src_ref=g_ref.at[:, pl.ds((1 - c) * half, half), :], dst_ref=out_ref,
                                        send_sem=send_sems.at[0], recv_sem=recv_sems.at[0], device_id=(x, y, 1 - c),
                                        device_id_type=MESH)


def _exchange_copies(in_ref, out_ref, send_sems, recv_sems):
    x, y, c = _position()
    me = 2 * x + y
    sends, arrivals = [], []
    for k, (px, py) in enumerate(_other_chips(x, y)):
        sends.append(pltpu.make_async_remote_copy(
            src_ref=in_ref.at[2 * px + py], dst_ref=out_ref.at[me], send_sem=send_sems.at[k], recv_sem=recv_sems.at[k],
            device_id=(px, py, c), device_id_type=MESH))
        arrivals.append(pltpu.make_async_remote_copy(
            src_ref=in_ref.at[me], dst_ref=out_ref.at[2 * px + py], send_sem=send_sems.at[k], recv_sem=recv_sems.at[k],
            device_id=(px, py, c), device_id_type=MESH))
    return sends, arrivals


def _exchange_start(*refs):
    for cp in _exchange_copies(*refs)[0]:
        cp.start()


def _exchange_finish(*refs):
    sends, arrivals = _exchange_copies(*refs)
    for cp in arrivals:
        cp.wait_recv()
    for cp in sends:
        cp.wait_send()


_JOB_START = {"gather": _gather_start, "pair_send": lambda *refs: _pair_send_copy(*refs).start(), "exchange": _exchange_start}
_JOB_FINISH = {"gather": _gather_finish, "pair_send": lambda *refs: _pair_send_copy(*refs).wait(),
               "exchange": _exchange_finish}


def _pair_sum(g, got, pos, *, name):
    nq, r, wd = g.shape
    half = r // 2
    tr = _tile(half, (256, 128, 64, 32, 16, 8))
    nb = half // tr

    def body(c_ref, g_ref, got_ref, o_ref):
        o_ref[...] = (g_ref[...] + got_ref[...]).astype(o_ref.dtype)

    grid_spec = pltpu.PrefetchScalarGridSpec(
        num_scalar_prefetch=1, grid=(nq, nb),
        in_specs=[pl.BlockSpec((1, tr, wd), lambda q, j, c_ref: (q, c_ref[0] * nb + j, 0)),
                  pl.BlockSpec((1, tr, wd), lambda q, j, c_ref: (q, j, 0))],
        out_specs=pl.BlockSpec((1, tr, wd), lambda q, j, c_ref: (q, j, 0)))
    return pl.pallas_call(body, name=name, grid_spec=grid_spec, out_shape=jax.ShapeDtypeStruct((nq, half, wd), BF16),
                          compiler_params=_cparams(("arbitrary", "arbitrary")))(pos[0], g, got)


def _chip_sum(parts, pb, pos, *, name):
    nq, half, wd = parts.shape
    tr = _tile(half, (256, 128, 64, 32, 16, 8))
    nb = half // tr

    def body(c_ref, x_ref, y_ref, p_ref, own_ref, o_ref):
        chip = 2 * x_ref[0] + y_ref[0]
        own = own_ref[0].astype(F32)
        acc = None
        for q in range(nq):
            term = jnp.where(chip == q, own, p_ref[q].astype(F32))
            acc = term if acc is None else acc + term
        o_ref[...] = acc

    grid_spec = pltpu.PrefetchScalarGridSpec(
        num_scalar_prefetch=3, grid=(nb,),
        in_specs=[pl.BlockSpec((nq, tr, wd), lambda j, c_ref, x_ref, y_ref: (0, j, 0)),
                  pl.BlockSpec((1, tr, wd), lambda j, c_ref, x_ref, y_ref: (2 * x_ref[0] + y_ref[0], j, 0))],
        out_specs=pl.BlockSpec((tr, wd), lambda j, c_ref, x_ref, y_ref: (c_ref[0] * nb + j, 0)))
    return pl.pallas_call(body, name=name, grid_spec=grid_spec, out_shape=jax.ShapeDtypeStruct((2 * half, wd), F32),
                          compiler_params=_cparams(("arbitrary",)))(*pos, parts, pb)


def _pair_gather(full, *, name):
    r, wd = full.shape
    half = r // 2

    def body(in_ref, out_ref, send_sem, recv_sem):
        x, y, c = _position()
        mine = out_ref.at[pl.ds(c * half, half), :]
        cp = pltpu.make_async_remote_copy(src_ref=mine, dst_ref=mine, send_sem=send_sem, recv_sem=recv_sem,
                                          device_id=(x, y, 1 - c), device_id_type=MESH)
        cp.start()
        pltpu.make_async_remote_copy(src_ref=mine, dst_ref=out_ref.at[pl.ds((1 - c) * half, half), :], send_sem=send_sem,
                                     recv_sem=recv_sem, device_id=(x, y, 1 - c), device_id_type=MESH).wait_recv()
        cp.wait_send()

    return pl.pallas_call(
        body, name=name, in_specs=[_ANY], out_specs=_ANY, out_shape=jax.ShapeDtypeStruct(full.shape, full.dtype),
        input_output_aliases={0: 0}, scratch_shapes=[pltpu.SemaphoreType.DMA(()), pltpu.SemaphoreType.DMA(())],
    )(full)


class _GradReduce:
    def __init__(self, slabs, pos, tag):
        self.names, self.slabs, self.pos, self.tag = list(slabs), [slabs[k] for k in slabs], pos, tag
        self.pb = self.out = None

    def send_jobs(self):
        return [("pair_send", g) for g in self.slabs]

    def after_send(self, gots):
        self.pb = [_pair_sum(g, got, self.pos, name=f"rs_pair_sum_{k}_{self.tag}")
                   for k, g, got in zip(self.names, self.slabs, gots)]

    def exchange_jobs(self):
        return [("exchange", pb) for pb in self.pb]

    def after_exchange(self, parts):
        full = [_chip_sum(pt, pb, self.pos, name=f"rs_chip_sum_{k}_{self.tag}") for k, pt, pb in zip(self.names, parts, self.pb)]
        self.out = {k: _pair_gather(f, name=f"rs_pair_gather_{k}_{self.tag}") for k, f in zip(self.names, full)}

    def run(self):
        if self.pb is None:
            self.after_send(_run_jobs(self.send_jobs(), name="rs_pair_send_" + self.tag))
        if self.out is None:
            self.after_exchange(_run_jobs(self.exchange_jobs(), name="rs_exchange_" + self.tag))
        return self.out


def _all_reduce_small(vec, *, name):
    r, wd = vec.shape

    def body(in_ref, out_ref, slots, send_sems, recv_sems):
        x, y, c = _position()
        me = 4 * x + 2 * y + c
        flips = [(fx, fy, fc) for fx in (0, 1) for fy in (0, 1) for fc in (0, 1) if fx + fy + fc]
        peers = [(1 - x if fx else x, 1 - y if fy else y, 1 - c if fc else c) for fx, fy, fc in flips]
        sends = []
        for k, peer in enumerate(peers):
            cp = pltpu.make_async_remote_copy(src_ref=in_ref, dst_ref=slots.at[me], send_sem=send_sems.at[k],
                                              recv_sem=recv_sems.at[k], device_id=peer, device_id_type=MESH)
            cp.start()
            sends.append(cp)
        slots[me] = in_ref[...]
        for k, (px, py, pc) in enumerate(peers):
            pltpu.make_async_remote_copy(src_ref=in_ref, dst_ref=slots.at[4 * px + 2 * py + pc], send_sem=send_sems.at[k],
                                         recv_sem=recv_sems.at[k], device_id=(px, py, pc), device_id_type=MESH).wait_recv()
        for cp in sends:
            cp.wait_send()
        acc = slots[0]
        for q in range(1, N_DEV):
            acc = acc + slots[q]
        out_ref[...] = acc

    vm = pl.BlockSpec(memory_space=pltpu.VMEM)
    return pl.pallas_call(
        body, name=name, in_specs=[vm], out_specs=vm, out_shape=jax.ShapeDtypeStruct((r, wd), F32),
        scratch_shapes=[pltpu.VMEM((N_DEV, r, wd), F32), pltpu.SemaphoreType.DMA((N_DEV - 1,)),
                        pltpu.SemaphoreType.DMA((N_DEV - 1,))],
        compiler_params=_cparams(),
    )(vec)


def _adamw(w, g, m, v, *, name):
    r, wd = w.shape
    tr = _tile(r, (256, 128, 64, 32, 16, 8))

    def body(w_ref, g_ref, m_ref, v_ref, d_ref, m_out, v_out):
        gv = g_ref[...]
        m_new = ADAM_B1 * m_ref[...] + (1.0 - ADAM_B1) * gv
        v_new = ADAM_B2 * v_ref[...] + (1.0 - ADAM_B2) * (gv * gv)
        m_hat = m_new / (1.0 - ADAM_B1 ** ADAM_STEP)
        v_hat = v_new / (1.0 - ADAM_B2 ** ADAM_STEP)
        d_ref[...] = -ADAM_LR * (m_hat / (jnp.sqrt(v_hat) + ADAM_EPS) + ADAM_WD * w_ref[...])
        m_out[...] = m_new
        v_out[...] = v_new

    spec = pl.BlockSpec((tr, wd), lambda j: (j, 0))
    return pl.pallas_call(body, name=name, grid=(r // tr,), in_specs=[spec] * 4, out_specs=[spec] * 3,
                          out_shape=[jax.ShapeDtypeStruct((r, wd), F32)] * 3, compiler_params=_cparams(("arbitrary",)))(w, g, m, v)


_WEIGHTS = ("ln_mix", "w_in", "w_in_vres", "mu_shift", "mu_shift_vres", "conv_a_w", "conv_a_b", "lru_wx", "lru_bx", "lru_wa",
            "lru_ba", "lru_lambda", "lru_norm", "rwkv_w0", "rwkv_w2", "rwkv_a0", "rwkv_a2", "rwkv_v0", "rwkv_v2", "rwkv_g2",
            "rwkv_kk", "rwkv_ka", "rwkv_rk", "rwkv_lnx_w", "rwkv_lnx_b", "w_o", "ln_ffn", "w_gate", "w_up", "conv_f_w",
            "conv_f_b", "w_down", "ln_ple", "w_ple_gate", "w_ple_proj", "ln_ple_post", "ln_final")
_SHARD_AXIS = {"w_in": 2, "w_in_vres": 1, "conv_a_w": 2, "lru_wx": 2, "lru_wa": 2, "rwkv_w2": 2, "rwkv_a2": 2, "rwkv_v2": 2,
               "rwkv_g2": 2, "w_o": 1, "w_gate": 2, "w_up": 2, "conv_f_w": 2, "w_down": 1, "w_ple_gate": 1, "w_ple_proj": 2}
_BIG_SHARDED = ("w_in",) + _BIG
_SMALL_SHARDED = tuple(k for k in _WEIGHTS if k in _SHARD_AXIS and k not in _BIG_SHARDED)
_REPLICATED = tuple(k for k in _WEIGHTS if k not in _SHARD_AXIS)
PACK_WIDTH = 512


def _to_shards(g, axis):
    n = g.shape[axis] // N_XY
    return jnp.moveaxis(g.reshape(g.shape[:axis] + (N_XY, n) + g.shape[axis + 1:]), axis, 0)


def _from_shards(s, axis):
    s = jnp.moveaxis(s, 0, axis)
    return s.reshape(s.shape[:axis] + (N_XY * s.shape[axis + 1],) + s.shape[axis + 2:])


def _pack(arrs, lead, width, row_mult):
    lead_shape = arrs[0].shape[:lead]
    flat = jnp.concatenate([a.reshape(lead_shape + (-1,)) for a in arrs], axis=-1)
    n = flat.shape[-1]
    total = _round_up(n, width * row_mult)
    flat = jnp.pad(flat, [(0, 0)] * lead + [(0, total - n)])
    return flat.reshape(lead_shape + (total // width, width))


def _unpack(packed, shapes):
    flat = packed.reshape(-1)
    out, o = [], 0
    for s in shapes:
        n = 1
        for dim in s:
            n *= dim
        out.append(flat[o:o + n].reshape(s))
        o += n
    return out


def _as2d(a):
    return a.reshape(-1, a.shape[-1])


def kernel(x, p, ln_mix, w_in, w_in_vres, mu_shift, mu_shift_vres, conv_a_w, conv_a_b, lru_wx, lru_bx, lru_wa, lru_ba, lru_lambda, lru_norm, rwkv_w0, rwkv_w2, rwkv_a0, rwkv_a2, rwkv_v0, rwkv_v2, rwkv_g2, rwkv_kk, rwkv_ka, rwkv_rk, rwkv_lnx_w, rwkv_lnx_b, w_o, ln_ffn, w_gate, w_up, conv_f_w, conv_f_b, w_down, ln_ple, w_ple_gate, w_ple_proj, ln_ple_post, ln_final, loss_target, m_ln_mix, m_w_in, m_w_in_vres, m_mu_shift, m_mu_shift_vres, m_conv_a_w, m_conv_a_b, m_lru_wx, m_lru_bx, m_lru_wa, m_lru_ba, m_lru_lambda, m_lru_norm, m_rwkv_w0, m_rwkv_w2, m_rwkv_a0, m_rwkv_a2, m_rwkv_v0, m_rwkv_v2, m_rwkv_g2, m_rwkv_kk, m_rwkv_ka, m_rwkv_rk, m_rwkv_lnx_w, m_rwkv_lnx_b, m_w_o, m_ln_ffn, m_w_gate, m_w_up, m_conv_f_w, m_conv_f_b, m_w_down, m_ln_ple, m_w_ple_gate, m_w_ple_proj, m_ln_ple_post, m_ln_final, v_ln_mix, v_w_in, v_w_in_vres, v_mu_shift, v_mu_shift_vres, v_conv_a_w, v_conv_a_b, v_lru_wx, v_lru_bx, v_lru_wa, v_lru_ba, v_lru_lambda, v_lru_norm, v_rwkv_w0, v_rwkv_w2, v_rwkv_a0, v_rwkv_a2, v_rwkv_v0, v_rwkv_v2, v_rwkv_g2, v_rwkv_kk, v_rwkv_ka, v_rwkv_rk, v_rwkv_lnx_w, v_rwkv_lnx_b, v_w_o, v_ln_ffn, v_w_gate, v_w_up, v_conv_f_w, v_conv_f_b, v_w_down, v_ln_ple, v_w_ple_gate, v_w_ple_proj, v_ln_ple_post, v_ln_final):
    a = dict(locals())
    x2, p, tgt = a["x"][0], a["p"][:, 0], a["loss_target"][0]
    pos = tuple(lax.axis_index(ax).astype(jnp.int32).reshape(1) for ax in ("c", "x", "y"))

    wf = {k: a[k] for k in _REPLICATED}
    small_shapes = [a[k].shape for k in _SMALL_SHARDED]
    shards = {k: [a[k][i].astype(BF16) for i in range(a[k].shape[0])] for k in _BIG_SHARDED}
    packed = _pack([a[k] for k in _SMALL_SHARDED], 0, PACK_WIDTH, 16)
    got_small, got_w_in = _run_jobs([_gather_blob(packed), _gather_blob(shards["w_in"][0])], name="ag_first")
    got_small = got_small.reshape((N_XY,) + packed.shape)
    pieces = [_unpack(got_small[q], small_shapes) for q in range(N_XY)]
    for j, k in enumerate(_SMALL_SHARDED):
        wf[k] = _from_shards(jnp.stack([pieces[q][j] for q in range(N_XY)], axis=0), _SHARD_AXIS[k])

    m = _make_dims(x2, p, wf)
    m.dw_shards = N_XY
    w = _prepare_weights(m, wf)
    feed = _WeightFeed(m, w, shards, wf["w_in_vres"].astype(BF16))
    feed.arrive([("w_in", 0)], [got_w_in])
    def reducer(i, g, which):
        names = ("w_in",) if which == "late" else _BIG
        full = {k: (g["wcat"][:, :m.din] if k == "w_in" else g[k]) for k in names}
        slabs = {k: full[k] if full[k].ndim == 3 else _to_shards(full[k], _SHARD_AXIS[k] - 1) for k in names}
        return _GradReduce(slabs, pos, f"{which}_{i}")

    loss_row, dx, grads, d_ln_final, reductions = _local_step(m, w, x2, p, tgt, feed, reducer)
    gfull = _unpack_grads(m, grads, d_ln_final, with_big=False)
    loss = lax.psum(loss_row[0, 0], ("x", "y", "c"))

    reduced = [{**early.run(), **late.run()} for early, late in reductions]
    gred = {k: jnp.stack([reduced[i][k] for i in range(m.nl)], axis=0).reshape(a[k].shape) for k in _BIG_SHARDED}
    gs = _pack([_to_shards(gfull[k], _SHARD_AXIS[k]) for k in _SMALL_SHARDED], 1, PACK_WIDTH, 32)
    g_small = _GradReduce({"small": gs}, pos, "small").run()["small"]
    rep_shapes = [a[k].shape for k in _REPLICATED]
    g_rep = _all_reduce_small(_pack([gfull[k] for k in _REPLICATED], 0, LANES_V7X, 8), name="ar_replicated")

    delta, new_m, new_v = {}, {}, {}
    for k in _BIG_SHARDED:
        res = _adamw(_as2d(a[k]), _as2d(gred[k]), _as2d(a["m_" + k]), _as2d(a["v_" + k]), name="adamw_" + k)
        delta[k], new_m[k], new_v[k] = (r.reshape(a[k].shape) for r in res)
    for names, shapes, g_packed, width, mult, tag in ((_SMALL_SHARDED, small_shapes, g_small, PACK_WIDTH, 32, "small"),
                                                      (_REPLICATED, rep_shapes, g_rep, LANES_V7X, 8, "replicated")):
        packs = [_pack([a[pre + k] for k in names], 0, width, mult) for pre in ("", "m_", "v_")]
        res = _adamw(packs[0], g_packed, packs[1], packs[2], name="adamw_" + tag)
        for dst, r in zip((gred, delta, new_m, new_v), [g_packed] + list(res)):
            dst.update(zip(names, _unpack(r, shapes)))
    return (loss, dx[None], *[gred[k] for k in _WEIGHTS], *[delta[k] for k in _WEIGHTS],
            *[new_m[k] for k in _WEIGHTS], *[new_v[k] for k in _WEIGHTS])
```

```python
import functools

import jax
import jax.numpy as jnp
from jax import lax
from jax.experimental import pallas as pl
from jax.experimental.pallas import tpu as pltpu

F32 = jnp.float32
BF16 = jnp.bfloat16
HIGHEST = lax.Precision.HIGHEST
MESH = pl.DeviceIdType.MESH

RMS_EPS = 1e-6
LNX_EPS = 64e-5
LRU_C = 8.0
ADAM_LR = 0.001
ADAM_B1 = 0.9
ADAM_B2 = 0.999
ADAM_EPS = 1e-08
ADAM_WD = 0.01
ADAM_STEP = 10

LANES_V7X = 128
VMEM_LIMIT_V7X = 60 * 1024 * 1024
WKV_CHUNK = 16
N_XY = 4
N_DEV = 8


def _cparams(sem=None, **kw):
    if sem is not None:
        kw["dimension_semantics"] = sem
    return pltpu.CompilerParams(vmem_limit_bytes=VMEM_LIMIT_V7X, **kw)


def _tile(dim, prefs):
    for t in prefs:
        if dim % t == 0:
            return t
    return dim


def _round_up(n, m):
    return (n + m - 1) // m * m


MM_MAX_TK = 2816


def _tile_k(kdim):
    best = None
    for t in range(LANES_V7X, min(kdim, MM_MAX_TK) + 1, LANES_V7X):
        if kdim % t == 0:
            best = t
    return best or kdim


def _mm(a, b, *, ta=False, tb=False, res=None, out_dtype=F32, name, gather=(), out_shards=0):
    if ta:
        kdim, m = a.shape
    else:
        m, kdim = a.shape
    bs = b.shape[0] if b.ndim == 3 else 0
    if bs:
        b_rows, b_cols = b.shape[1], bs * b.shape[2]
    else:
        b_rows, b_cols = b.shape
    n = b_rows if tb else b_cols
    assert (b_cols if tb else b_rows) == kdim
    per_shard = (lambda total, s: total // s if s else total)
    tk = _tile_k(per_shard(kdim, bs) if tb else kdim)
    tm = _tile(m, (2048, 1024, 512, 256, 128) if tk <= 2048 else (1024, 512, 256, 128))
    tn = _tile(per_shard(per_shard(n, out_shards), 0 if tb else bs), (512, 256, 128))
    nk = kdim // tk
    ni, nj = m // tm, n // tn
    a_spec = pl.BlockSpec((tk, tm), lambda i, j, k: (k, i)) if ta else pl.BlockSpec((tm, tk), lambda i, j, k: (i, k))
    if bs and tb:
        kps = b.shape[2] // tk
        b_spec = pl.BlockSpec((None, tn, tk), lambda i, j, k: (k // kps, j, k % kps))
    elif bs:
        nps = b.shape[2] // tn
        b_spec = pl.BlockSpec((None, tk, tn), lambda i, j, k: (j // nps, k, j % nps))
    else:
        b_spec = pl.BlockSpec((tn, tk), lambda i, j, k: (j, k)) if tb else pl.BlockSpec((tk, tn), lambda i, j, k: (k, j))
    if out_shards:
        assert res is None
        ops = n // out_shards // tn
        o_spec = pl.BlockSpec((None, tm, tn), lambda i, j, k: (j // ops, i, j % ops))
        o_shape = jax.ShapeDtypeStruct((out_shards, m, n // out_shards), out_dtype)
    else:
        o_spec = pl.BlockSpec((tm, tn), lambda i, j, k: (i, j))
        o_shape = jax.ShapeDtypeStruct((m, n), out_dtype)
    dn = (((0 if ta else 1,), (1 if tb else 0,)), ((), ()))
    has_res = res is not None
    ng = len(gather)
    nin = 2 + has_res

    def body(*refs):
        a_ref, b_ref = refs[:2]
        r_ref = refs[2] if has_res else None
        g_in, o_ref, g_out = refs[nin:nin + ng], refs[nin + ng], refs[nin + ng + 1:nin + 2 * ng + 1]
        scratch = refs[nin + 2 * ng + 1:]
        acc_ref = scratch[0] if nk > 1 else None
        g_sems = scratch[1 if nk > 1 else 0:]
        i, j, k = pl.program_id(0), pl.program_id(1), pl.program_id(2)

        if ng:
            @pl.when((i == 0) & (j == 0) & (k == 0))
            def _():
                _jobs_start(gather, g_in, g_out, g_sems)

        def finish(acc):
            if has_res:
                acc = acc + r_ref[...].astype(F32)
            o_ref[...] = acc.astype(out_dtype)

        prod = lax.dot_general(a_ref[...], b_ref[...], dn, preferred_element_type=F32)
        if nk == 1:
            finish(prod)
        else:
            @pl.when(k == 0)
            def _():
                acc_ref[...] = prod

            @pl.when(k > 0)
            def _():
                acc_ref[...] += prod

            @pl.when(k == nk - 1)
            def _():
                finish(acc_ref[...])

        if ng:
            @pl.when((i == ni - 1) & (j == nj - 1) & (k == nk - 1))
            def _():
                _jobs_finish(gather, g_in, g_out, g_sems)

    ins = [a, b] + ([res] if has_res else []) + _job_arrays(gather)
    in_specs = [a_spec, b_spec] + ([o_spec] if has_res else []) + [_ANY] * ng
    scratch = ([pltpu.VMEM((tm, tn), F32)] if nk > 1 else []) + _jobs_scratch(gather)
    sem = ("arbitrary",) * 3 if ng else ("parallel", "parallel", "arbitrary")
    out = pl.pallas_call(
        body, name=name, grid=(ni, nj, nk), in_specs=in_specs, out_specs=[o_spec] + [_ANY] * ng,
        out_shape=[o_shape] + [_job_out_shape(g) for g in gather],
        scratch_shapes=scratch, compiler_params=_cparams(sem),
    )(*ins)
    return (out[0], list(out[1:])) if ng else out[0]


def _stage_specs(axis, tile, tiled, params, consts, rows):
    specs = []
    for arr, width, cblk in tiled:
        if axis == 0:
            specs.append(pl.BlockSpec((tile, width), functools.partial(lambda i, c: (i, c), c=cblk)))
        else:
            specs.append(pl.BlockSpec((rows, tile), functools.partial(lambda i, c: (0, i + c), c=cblk)))
    for arr, cblk in params:
        if axis == 0:
            specs.append(pl.BlockSpec(arr.shape, functools.partial(lambda i, nd: (0,) * nd, nd=arr.ndim)))
        else:
            specs.append(pl.BlockSpec((arr.shape[0], tile), functools.partial(lambda i, c: (0, i + c), c=cblk)))
    for arr in consts:
        specs.append(pl.BlockSpec(arr.shape, functools.partial(lambda i, nd: (0,) * nd, nd=arr.ndim)))
    return specs


def _stage_fwd(fn, tiled, params, consts, outs, *, axis, tile, rows, name):
    nt, npar, nc = len(tiled), len(params), len(consts)
    ntiles = (rows // tile) if axis == 0 else (outs[0][0] // tile)

    def body(*refs):
        ins = refs[: nt + npar + nc]
        orefs = refs[nt + npar + nc:]
        vals = [r[...].astype(F32) for r in ins[: nt + npar]] + [r[...] for r in ins[nt + npar:]]
        ctx = pl.program_id(0) * tile
        res = fn(ctx, *vals)
        for o_ref, o in zip(orefs, res):
            o_ref[...] = o.astype(o_ref.dtype)

    if axis == 0:
        out_specs = [pl.BlockSpec((tile, w), lambda i: (i, 0)) for w, _ in outs]
    else:
        out_specs = [pl.BlockSpec((rows, tile), lambda i: (0, i)) for w, _ in outs]
    res = pl.pallas_call(
        body, name=name, grid=(ntiles,),
        in_specs=_stage_specs(axis, tile, tiled, params, consts, rows), out_specs=out_specs,
        out_shape=[jax.ShapeDtypeStruct((rows, w), dt) for w, dt in outs],
        compiler_params=_cparams(("arbitrary",)),
    )(*[t[0] for t in tiled], *[p[0] for p in params], *consts)
    return res


def _stage_bwd(fn, tiled, params, consts, cots, dtiled, *, axis, tile, rows, name, ncols=None):
    nt, npar, nc, nco = len(tiled), len(params), len(consts), len(cots)
    ntiles = (rows // tile) if axis == 0 else (ncols // tile)
    didx = [d[0] for d in dtiled]

    def body(*refs):
        ins = refs[: nt + npar + nc]
        crefs = refs[nt + npar + nc: nt + npar + nc + nco]
        orefs = refs[nt + npar + nc + nco:]
        vals = [r[...].astype(F32) for r in ins[: nt + npar]] + [r[...] for r in ins[nt + npar:]]
        ctx = pl.program_id(0) * tile

        def g(*dv):
            full = list(vals)
            for j, ix in enumerate(didx):
                full[ix] = dv[j]
            for j in range(npar):
                full[nt + j] = dv[len(didx) + j]
            return tuple(fn(ctx, *full))

        prim = [vals[ix] for ix in didx] + [vals[nt + j] for j in range(npar)]
        _, vjp = jax.vjp(g, *prim)
        grads = vjp(tuple(c[...].astype(F32) for c in crefs))
        for j in range(len(didx)):
            orefs[j][...] = grads[j].astype(orefs[j].dtype)
        for j in range(npar):
            o_ref = orefs[len(didx) + j]
            gp = grads[len(didx) + j]
            if axis == 0:
                @pl.when(pl.program_id(0) == 0)
                def _(o_ref=o_ref):
                    o_ref[...] = jnp.zeros_like(o_ref)

                o_ref[...] += gp
            else:
                o_ref[...] = gp

    if axis == 0:
        cot_specs = [pl.BlockSpec((tile, w), functools.partial(lambda i, c: (i, c), c=cb)) for _, w, cb in cots]
        out_specs = [pl.BlockSpec((tile, w), lambda i: (i, 0)) for _, w, _ in dtiled]
        out_specs += [pl.BlockSpec(p.shape, functools.partial(lambda i, nd: (0,) * nd, nd=p.ndim)) for p, _ in params]
        out_shape = [jax.ShapeDtypeStruct((rows, w), dt) for _, w, dt in dtiled]
        out_shape += [jax.ShapeDtypeStruct(p.shape, F32) for p, _ in params]
    else:
        cot_specs = [pl.BlockSpec((rows, tile), functools.partial(lambda i, c: (0, i + c), c=cb)) for _, cb in cots]
        out_specs = [pl.BlockSpec((rows, tile), lambda i: (0, i)) for _ in dtiled]
        out_specs += [pl.BlockSpec((p.shape[0], tile), lambda i: (0, i)) for p, _ in params]
        out_shape = [jax.ShapeDtypeStruct((rows, w), dt) for _, w, dt in dtiled]
        out_shape += [jax.ShapeDtypeStruct((p.shape[0], ncols), F32) for p, _ in params]
    return pl.pallas_call(
        body, name=name, grid=(ntiles,),
        in_specs=_stage_specs(axis, tile, tiled, params, consts, rows) + cot_specs, out_specs=out_specs,
        out_shape=out_shape, compiler_params=_cparams(("arbitrary",)),
    )(*[t[0] for t in tiled], *[p[0] for p in params], *consts, *[c[0] for c in cots])


def _rms(x, g):
    return x * lax.rsqrt(jnp.mean(x * x, axis=-1, keepdims=True) + RMS_EPS) * g


def _row_mask(x, k, first):
    t = lax.broadcasted_iota(jnp.int32, x.shape, 0)
    keep = (t >= k) if first else (t < x.shape[0] - k)
    return jnp.where(keep, x, 0.0)


@functools.partial(jax.custom_vjp, nondiff_argnums=(1,))
def _shift_down(x, k):
    return _row_mask(pltpu.roll(x, k, 0), k, True)


def _shift_down_fwd(x, k):
    return _shift_down(x, k), None


def _shift_down_bwd(k, _, g):
    return (_row_mask(pltpu.roll(g, g.shape[0] - k, 0), k, False),)


_shift_down.defvjp(_shift_down_fwd, _shift_down_bwd)


def _dwconv(x, w, b):
    kw = w.shape[0]
    out = x * w[kw - 1:kw] + b
    for j in range(kw - 1):
        out = out + _shift_down(x, kw - 1 - j) * w[j:j + 1]
    return out


def _f_norm(ctx, x, g):
    return (_rms(x, g),)


def _f_norm_res(ctx, x, g):
    return (_rms(x, g), x)


def _f_shiftmix(ctx, z, mu):
    return (z + (_shift_down(z, 1) - z) * mu,)


def _f_conv(ctx, x, w, b):
    return (_dwconv(x, w, b),)


def _f_ffn_act(ctx, gpre, up, w, b):
    return (jax.nn.gelu(_dwconv(gpre, w, b)) * up,)


def _make_f_lru_gates(heads):
    def fn(ctx, xb, wx, wa, bx, ba, lam):
        blk = xb.shape[1] // heads
        px, pa = [], []
        for h in range(heads):
            xh = xb[:, h * blk:(h + 1) * blk]
            px.append(jnp.dot(xh, wx[h], preferred_element_type=F32))
            pa.append(jnp.dot(xh, wa[h], preferred_element_type=F32))
        px = px[0] if heads == 1 else jnp.concatenate(px, axis=1)
        pa = pa[0] if heads == 1 else jnp.concatenate(pa, axis=1)
        gate_x = jax.nn.sigmoid(px + bx)
        gate_a = jax.nn.sigmoid(pa + ba)
        log_a = -LRU_C * gate_a * jax.nn.softplus(-lam)
        a = jnp.exp(log_a)
        mult = jnp.sqrt(1.0 - jnp.exp(2.0 * log_a))
        t = ctx + lax.broadcasted_iota(jnp.int32, xb.shape, 0)
        mult = jnp.where(t == 0, 1.0, mult)
        return a, xb * gate_x * mult

    return fn


def _f_lru_out(ctx, hl, ya, g):
    return (_rms(hl * jax.nn.gelu(ya), g),)


def _headsum_3pass(x, bb):
    hi = x.astype(BF16)
    r1 = x - hi.astype(F32)
    mid = r1.astype(BF16)
    lo = (r1 - mid.astype(F32)).astype(BF16)
    width, group = x.shape[1], bb.shape[0]
    out = []
    for g0 in range(0, width, group):
        cols = slice(g0, g0 + group)
        out.append(jnp.dot(hi[:, cols], bb, preferred_element_type=F32) + jnp.dot(mid[:, cols], bb, preferred_element_type=F32)
                   + jnp.dot(lo[:, cols], bb, preferred_element_type=F32))
    return out[0] if len(out) == 1 else jnp.concatenate(out, axis=1)


@jax.custom_vjp
def _headsum(x, bb):
    return _headsum_3pass(x, bb)


def _headsum_fwd(x, bb):
    return _headsum_3pass(x, bb), bb


def _headsum_bwd(bb, g):
    return _headsum_3pass(g, bb), None


_headsum.defvjp(_headsum_fwd, _headsum_bwd)


def _make_f_rwkv_pre(has_vres, v_uses=0):
    def fn(ctx, *args):
        if v_uses:
            r, args = args[0], args[1:]
        if has_vres:
            k, v, lz, vf, w0, w2, a0, a2, g2, kkw, ka, v0, v2, bb = args
        else:
            k, v, lz, w0, w2, a0, a2, g2, kkw, ka, bb = args
        w_log = -jax.nn.softplus(-(w0 + jnp.dot(jnp.tanh(lz), w2, preferred_element_type=F32))) - 0.5
        logw = -jnp.exp(w_log)
        a = jax.nn.sigmoid(a0 + jnp.dot(lz, a2, preferred_element_type=F32))
        g = jnp.dot(jax.nn.sigmoid(lz), g2, preferred_element_type=F32)
        if has_vres:
            v = v + (vf - v) * jax.nn.sigmoid(v0 + jnp.dot(lz, v2, preferred_element_type=F32))
        xk = k * kkw
        kk = xk / jnp.maximum(jnp.sqrt(_headsum(xk * xk, bb)), 1e-12)
        k2 = k * (1.0 + (a - 1.0) * ka)
        if v_uses:
            return (r, r, logw, k2, k2) + (v,) * v_uses + (kk, kk * a, g)
        return logw, k2, v, kk, kk * a, g

    return fn


def _make_f_rwkv_post(head_size):
    def fn(ctx, y, r, k2, v2, g, lnw, lnb, rk, bb):
        mean = _headsum(y, bb) / head_size
        d = y - mean
        var = _headsum(d * d, bb) / head_size
        yn = d * lax.rsqrt(var + LNX_EPS) * lnw + lnb
        bonus = _headsum(r * k2 * rk, bb) * v2
        return ((yn + bonus) * g,)

    return fn


def _f_ple(ctx, h, eg, ep, g):
    return (h + _rms(jax.nn.sigmoid(eg) * ep, g),)


def _lru_scan(a, b, *, name):
    rows, cols = a.shape
    tc = _tile(cols, (512, 256, 128))

    def body(a_ref, b_ref, h_ref):
        def step(t, carry):
            h = a_ref[pl.ds(t, 1), :] * carry + b_ref[pl.ds(t, 1), :]
            h_ref[pl.ds(t, 1), :] = h
            return h

        lax.fori_loop(0, rows, step, jnp.zeros((1, tc), F32), unroll=8)

    spec = pl.BlockSpec((rows, tc), lambda j: (0, j))
    return pl.pallas_call(body, name=name, grid=(cols // tc,), in_specs=[spec, spec], out_specs=spec,
                          out_shape=jax.ShapeDtypeStruct((rows, cols), F32), compiler_params=_cparams(("arbitrary",)))(a, b)


def _lru_scan_bwd(a, h, dh, *, name):
    rows, cols = a.shape
    tc = _tile(cols, (512, 256, 128))

    def body(a_ref, h_ref, dh_ref, da_ref, db_ref):
        def step(i, carry):
            t = rows - 1 - i
            g = dh_ref[pl.ds(t, 1), :] + carry
            db_ref[pl.ds(t, 1), :] = g
            hp = h_ref[pl.ds(jnp.maximum(t - 1, 0), 1), :]
            da_ref[pl.ds(t, 1), :] = jnp.where(t > 0, g * hp, 0.0)
            return a_ref[pl.ds(t, 1), :] * g

        lax.fori_loop(0, rows, step, jnp.zeros((1, tc), F32), unroll=8)

    spec = pl.BlockSpec((rows, tc), lambda j: (0, j))
    return pl.pallas_call(body, name=name, grid=(cols // tc,), in_specs=[spec] * 3, out_specs=[spec] * 2,
                          out_shape=[jax.ShapeDtypeStruct((rows, cols), F32)] * 2,
                          compiler_params=_cparams(("arbitrary",)))(a, h, dh)


def _split_bf16(x):
    hi = x.astype(BF16)
    return hi, (x - hi.astype(F32)).astype(BF16)


def _dot3_passes(a, b, ca, cb):
    dn = (((ca,), (cb,)), ((), ()))
    ah, al = _split_bf16(a)
    bh, bl = _split_bf16(b)
    return (lax.dot_general(ah, bh, dn, preferred_element_type=F32) + lax.dot_general(al, bh, dn, preferred_element_type=F32)
            + lax.dot_general(ah, bl, dn, preferred_element_type=F32))


@functools.partial(jax.custom_vjp, nondiff_argnums=(2, 3))
def _dot3(a, b, ca, cb):
    return _dot3_passes(a, b, ca, cb)


def _dot3_fwd(a, b, ca, cb):
    return _dot3_passes(a, b, ca, cb), (a, b)


def _dot3_bwd(ca, cb, res, g):
    a, b = res
    fa, fb = 1 - ca, 1 - cb
    da = _dot3_passes(g, b, 1, fb) if ca == 1 else _dot3_passes(b, g, fb, 1)
    db = _dot3_passes(a, g, fa, 0) if cb == 0 else _dot3_passes(g, a, 0, fa)
    return da, db


_dot3.defvjp(_dot3_fwd, _dot3_bwd)


def _each(f, *lists):
    return [f(*t) for t in zip(*lists)]


def _wkv_local(r, lw, k, v, kk, b):
    c, n = r[0].shape
    row = lax.broadcasted_iota(jnp.int32, (c, c), 0)
    col = lax.broadcasted_iota(jnp.int32, (c, c), 1)
    incl = (row >= col).astype(F32)
    strict = (row > col).astype(F32)
    eye = lax.broadcasted_iota(jnp.int32, (n, n), 0) == lax.broadcasted_iota(jnp.int32, (n, n), 1)
    cl = _each(lambda x: _dot3(incl, x, 1, 0), lw)
    w_t = _each(jnp.exp, cl)
    inv_w = _each(lambda x: jnp.exp(-x), cl)
    kk_s = _each(lambda x, y, z: x * jnp.exp(y - z), kk, cl, lw)
    b_s = _each(jnp.multiply, b, inv_w)
    k_s = _each(jnp.multiply, k, inv_w)
    r_s = _each(jnp.multiply, r, w_t)
    q = _each(lambda x, y: jnp.concatenate([x, y], axis=0), kk_s, r_s)
    qb = _each(lambda x, y: _dot3(x, y, 1, 1), q, b_s)
    qk = _each(lambda x, y: _dot3(x, y, 1, 1), q, k_s)
    m = _each(lambda x: -strict * x[:c], qb)
    pb = _each(lambda x: incl * x[c:], qb)
    lkv = _each(lambda x, y: _dot3(strict * x[:c], y, 1, 0), qk, v)
    pkv = _each(lambda x, y: _dot3(incl * x[c:], y, 1, 0), qk, v)
    a = _each(lambda x, y: jnp.concatenate([x, y], axis=1), kk_s, lkv)
    steps = max(1, (c - 1).bit_length())
    for i in range(steps):
        a = _each(lambda x, y: y + _dot3(x, y, 1, 0), m, a)
        if i + 1 < steps:
            m = _each(lambda x: _dot3(x, x, 1, 0), m)
    ry = _each(lambda x, y, z, w: jnp.concatenate([x, y], axis=1) - _dot3(z, w, 1, 0), r_s, pkv, pb, a)
    w_end = _each(lambda x: x[c - 1:c, :], w_t)
    gu_low = _each(lambda x, y, z: _dot3(x, y * z, 0, 0), a, b_s, w_end)
    g = _each(lambda x, y: jnp.where(eye, jnp.broadcast_to(x, (n, n)), 0.0) - y[:n], w_end, gu_low)
    u = _each(lambda x, y, z, w: _dot3(x, y * z, 0, 0) - w[n:], v, k_s, w_end, gu_low)
    return g, u, _each(lambda x: x[:, :n], ry), _each(lambda x: x[:, n:], ry)


def _wkv_blocks(h, nchunk):
    return (_tile(h, (4, 2, 1)), _tile(nchunk, (4, 2, 1))), (h, _tile(nchunk, (4, 2, 1)))


def _wkv_fwd(r, lw, k, v, kk, b, *, name, gather=(), gather_state=()):
    h, t, n = r.shape
    c = WKV_CHUNK
    nchunk = t // c
    (hb, cb), (hs, cs) = _wkv_blocks(h, nchunk)
    ng = len(gather)
    ni, nj = h // hb, nchunk // cb

    pairs = [(i, j) for i in range(hb) for j in range(cb)]

    def local_body(*refs):
        ins, g_in = refs[:6], refs[6:6 + ng]
        g_ref, u_ref, r2_ref, y0_ref = refs[6 + ng:10 + ng]
        g_out, g_sems = refs[10 + ng:10 + 2 * ng], refs[10 + 2 * ng:]
        if ng:
            @pl.when((pl.program_id(0) == 0) & (pl.program_id(1) == 0))
            def _():
                _jobs_start(gather, g_in, g_out, g_sems)

        g, u, r2, y0 = _wkv_local(*[[ref[i, pl.ds(j * c, c)] for i, j in pairs] for ref in ins])
        for idx, (i, j) in enumerate(pairs):
            g_ref[i, j] = g[idx]
            u_ref[i, j] = u[idx]
            r2_ref[i, pl.ds(j * c, c)] = r2[idx]
            y0_ref[i, pl.ds(j * c, c)] = y0[idx]
        if ng:
            @pl.when((pl.program_id(0) == ni - 1) & (pl.program_id(1) == nj - 1))
            def _():
                _jobs_finish(gather, g_in, g_out, g_sems)

    seq = pl.BlockSpec((hb, cb * c, n), lambda i, j: (i, j, 0))
    mat = pl.BlockSpec((hb, cb, n, n), lambda i, j: (i, j, 0, 0))
    res = pl.pallas_call(
        local_body, name=name + "_local", grid=(ni, nj), in_specs=[seq] * 6 + [_ANY] * ng,
        out_specs=[mat, mat, seq, seq] + [_ANY] * ng,
        out_shape=[jax.ShapeDtypeStruct((h, nchunk, n, n), F32)] * 2 + [jax.ShapeDtypeStruct((h, t, n), F32)] * 2
        + [_job_out_shape(g) for g in gather],
        scratch_shapes=_jobs_scratch(gather),
        compiler_params=_cparams(("arbitrary", "arbitrary") if ng else ("parallel", "parallel")),
    )(r, lw, k, v, kk, b, *_job_arrays(gather))
    gm, um, r2, y0 = res[:4]
    gathered = list(res[4:])

    ng2 = len(gather_state)
    nsteps = nchunk // cs

    def state_body(*refs):
        g_ref, u_ref, r2_ref, y0_ref = refs[:4]
        g_in, (y_ref, st_ref) = refs[4:4 + ng2], refs[4 + ng2:6 + ng2]
        g_out, s_ref, g_sems = refs[6 + ng2:6 + 2 * ng2], refs[6 + 2 * ng2], refs[7 + 2 * ng2:]

        @pl.when(pl.program_id(0) == 0)
        def _():
            s_ref[...] = jnp.zeros_like(s_ref)
            _jobs_start(gather_state, g_in, g_out, g_sems)

        s = [s_ref[i] for i in range(hs)]
        for j in range(cs):
            rows = pl.ds(j * c, c)
            for i in range(hs):
                st_ref[i, j] = s[i]
                y_ref[i, rows] = _dot3(r2_ref[i, rows], s[i], 1, 1) + y0_ref[i, rows]
            s = [_dot3(s[i], g_ref[i, j], 1, 0) + u_ref[i, j] for i in range(hs)]
        for i in range(hs):
            s_ref[i] = s[i]
        if ng2:
            @pl.when(pl.program_id(0) == nsteps - 1)
            def _():
                _jobs_finish(gather_state, g_in, g_out, g_sems)

    seq = pl.BlockSpec((hs, cs * c, n), lambda j: (0, j, 0))
    mat = pl.BlockSpec((hs, cs, n, n), lambda j: (0, j, 0, 0))
    res = pl.pallas_call(
        state_body, name=name + "_state", grid=(nsteps,), in_specs=[mat, mat, seq, seq] + [_ANY] * ng2,
        out_specs=[seq, mat] + [_ANY] * ng2,
        out_shape=[jax.ShapeDtypeStruct((h, t, n), F32), jax.ShapeDtypeStruct((h, nchunk, n, n), F32)]
        + [_job_out_shape(g) for g in gather_state],
        scratch_shapes=[pltpu.VMEM((hs, n, n), F32)] + _jobs_scratch(gather_state), compiler_params=_cparams(("arbitrary",)),
    )(gm, um, r2, y0, *_job_arrays(gather_state))
    return res[0], (res[1], gm, r2), gathered + list(res[2:])


def _wkv_bwd(r, lw, k, v, kk, b, saved, dy, *, name, jobs_state=(), jobs_local=lambda state_results: ()):
    states, gm, r2 = saved
    h, t, n = r.shape
    c = WKV_CHUNK
    nchunk = t // c
    (hb, _), (hs, cs) = _wkv_blocks(h, nchunk)
    cb = _tile(nchunk, (8, 4, 2, 1))
    nsteps = nchunk // cs

    ns_ = len(jobs_state)

    def state_body(*refs):
        g_ref, r2_ref, st_ref, dy_ref = refs[:4]
        s_in, (dg_ref, du_ref, dr2_ref) = refs[4:4 + ns_], refs[4 + ns_:7 + ns_]
        s_out, ds_ref, s_sems = refs[7 + ns_:7 + 2 * ns_], refs[7 + 2 * ns_], refs[8 + 2 * ns_:]

        @pl.when(pl.program_id(0) == 0)
        def _():
            ds_ref[...] = jnp.zeros_like(ds_ref)
            _jobs_start(jobs_state, s_in, s_out, s_sems)

        ds = [ds_ref[i] for i in range(hs)]
        for j in reversed(range(cs)):
            rows = pl.ds(j * c, c)
            for i in range(hs):
                s0 = st_ref[i, j]
                du_ref[i, j] = ds[i]
                dg_ref[i, j] = _dot3(s0, ds[i], 0, 0)
                dr2_ref[i, rows] = _dot3(dy_ref[i, rows], s0, 1, 0)
            ds = [_dot3(dy_ref[i, rows], r2_ref[i, rows], 0, 0) + _dot3(ds[i], g_ref[i, j], 1, 1) for i in range(hs)]
        for i in range(hs):
            ds_ref[i] = ds[i]
        if ns_:
            @pl.when(pl.program_id(0) == nsteps - 1)
            def _():
                _jobs_finish(jobs_state, s_in, s_out, s_sems)

    seq = pl.BlockSpec((hs, cs * c, n), lambda j: (0, nsteps - 1 - j, 0))
    mat = pl.BlockSpec((hs, cs, n, n), lambda j: (0, nsteps - 1 - j, 0, 0))
    res = pl.pallas_call(
        state_body, name=name + "_state", grid=(nsteps,), in_specs=[mat, seq, mat, seq] + [_ANY] * ns_,
        out_specs=[mat, mat, seq] + [_ANY] * ns_,
        out_shape=[jax.ShapeDtypeStruct((h, nchunk, n, n), F32)] * 2 + [jax.ShapeDtypeStruct((h, t, n), F32)]
        + [_job_out_shape(j) for j in jobs_state],
        scratch_shapes=[pltpu.VMEM((hs, n, n), F32)] + _jobs_scratch(jobs_state), compiler_params=_cparams(("arbitrary",)),
    )(gm, r2, states, dy, *_job_arrays(jobs_state))
    dg, du, dr2 = res[:3]
    jobs = list(jobs_local(list(res[3:])))

    pairs = [(i, j) for i in range(hb) for j in range(cb)]
    nj_ = len(jobs)
    ni, nj = h // hb, nchunk // cb

    def local_body(*refs):
        ins, (dg_ref, du_ref, dr2_ref, dy_ref) = refs[:6], refs[6:10]
        j_in, out_refs, j_out, j_sems = refs[10:10 + nj_], refs[10 + nj_:16 + nj_], refs[16 + nj_:16 + 2 * nj_], refs[16 + 2 * nj_:]
        if nj_:
            @pl.when((pl.program_id(0) == 0) & (pl.program_id(1) == 0))
            def _():
                _jobs_start(jobs, j_in, j_out, j_sems)

        _, vjp = jax.vjp(_wkv_local, *[[ref[i, pl.ds(j * c, c)] for i, j in pairs] for ref in ins])
        grads = vjp(([dg_ref[i, j] for i, j in pairs], [du_ref[i, j] for i, j in pairs],
                     [dr2_ref[i, pl.ds(j * c, c)] for i, j in pairs], [dy_ref[i, pl.ds(j * c, c)] for i, j in pairs]))
        for o_ref, gr in zip(out_refs, grads):
            for idx, (i, j) in enumerate(pairs):
                o_ref[i, pl.ds(j * c, c)] = gr[idx]
        if nj_:
            @pl.when((pl.program_id(0) == ni - 1) & (pl.program_id(1) == nj - 1))
            def _():
                _jobs_finish(jobs, j_in, j_out, j_sems)

    seq = pl.BlockSpec((hb, cb * c, n), lambda i, j: (i, j, 0))
    mat = pl.BlockSpec((hb, cb, n, n), lambda i, j: (i, j, 0, 0))
    res = pl.pallas_call(
        local_body, name=name + "_local", grid=(ni, nj), in_specs=[seq] * 6 + [mat, mat, seq, seq] + [_ANY] * nj_,
        out_specs=[seq] * 6 + [_ANY] * nj_,
        out_shape=[jax.ShapeDtypeStruct((h, t, n), F32)] * 6 + [_job_out_shape(j) for j in jobs],
        scratch_shapes=_jobs_scratch(jobs),
        compiler_params=_cparams(("arbitrary", "arbitrary") if nj_ else ("parallel", "parallel")),
    )(r, lw, k, v, kk, b, dg, du, dr2, dy, *_job_arrays(jobs))
    return list(res[:6]), list(res[6:])


class _Dims:
    pass


def _make_dims(x, p, w):
    m = _Dims()
    m.t, m.d = x.shape[-2], x.shape[-1]
    m.nl = w["ln_mix"].shape[0]
    m.dl = w["conv_a_b"].shape[1]
    m.hl = w["lru_wx"].shape[1]
    m.dr = w["rwkv_w0"].shape[1]
    m.h, m.n = w["rwkv_rk"].shape[1], w["rwkv_rk"].shape[2]
    m.lw, m.la, m.lg, m.lv = (w[k].shape[1] for k in ("rwkv_w2", "rwkv_a2", "rwkv_g2", "rwkv_v2"))
    m.nsh = w["mu_shift"].shape[1]
    m.ff = w["conv_f_b"].shape[1]
    m.ple = p.shape[-1]
    m.din = 2 * m.dl + m.nsh
    m.lz = _round_up(m.lw + m.la + m.lg + m.lv, LANES_V7X)
    m.zw = _round_up(2 * m.dl + 3 * m.dr + m.lz, 512)
    m.zs = m.zw - 2 * m.dl
    m.tr = _tile(m.t, (256, 128, 64, 32, 16, 8))
    m.trb = _tile(m.t, (128, 64, 32, 16, 8))
    m.tcs = _tile(m.zs, (512, 256, 128))
    m.dw_shards = 0
    assert (3 * m.dr) % m.lz == 0 and (2 * m.dl) % m.tcs == 0 and m.t % WKV_CHUNK == 0
    assert m.nsh == 3 * m.dr + m.lw + m.la + m.lg
    return m


def _to_heads(m, a):
    return jnp.transpose(a.reshape(m.t, m.h, m.n), (1, 0, 2))


def _from_heads(m, a):
    return jnp.transpose(a, (1, 0, 2)).reshape(m.t, m.dr)


def _norm_fwd(m, h, g, name):
    return _stage_fwd(_f_norm, [(h, m.d, 0)], [(g, 0)], [], [(m.d, BF16)], axis=0, tile=m.tr, rows=m.t, name=name)[0]


def _norm_bwd(m, h, g, du, dres, name):
    return _stage_bwd(_f_norm_res, [(h, m.d, 0)], [(g, 0)], [], [(du, m.d, 0), (dres, m.d, 0)], [(0, m.d, F32)],
                      axis=0, tile=m.tr, rows=m.t, name=name)


def _rwkv_pre_operands(m, w, i, sv, v_first_zs, with_r):
    zs = sv["zs"]
    tiled = ([(zs, m.dr, 0)] if with_r else []) + [(zs, m.dr, 1), (zs, m.dr, 2), (zs, m.lz, 3 * m.dr // m.lz)]
    params = [(w["rwkv_w0"][i:i + 1], 0), (w["w2p"][i], 0), (w["rwkv_a0"][i:i + 1], 0), (w["a2p"][i], 0),
              (w["g2p"][i], 0), (w["rwkv_kk"][i:i + 1], 0), (w["rwkv_ka"][i:i + 1], 0)]
    if i > 0:
        tiled.append((v_first_zs, m.dr, 2))
        params += [(w["rwkv_v0"][i - 1:i], 0), (w["v2p"][i - 1], 0)]
    return tiled, params


def _rwkv_post_operands(m, w, i, sv):
    tiled = [(sv["y"], m.dr, 0), (sv["zs"], m.dr, 0), (sv["k2"], m.dr, 0), (sv["v2"], m.dr, 0), (sv["g"], m.dr, 0)]
    params = [(w["rwkv_lnx_w"][i:i + 1], 0), (w["rwkv_lnx_b"][i:i + 1], 0), (w["rk"][i], 0)]
    return tiled, params


def _lru_gate_params(w, i):
    return [(w["lru_wx"][i], 0), (w["lru_wa"][i], 0), (w["lru_bx"][i:i + 1], 0), (w["lru_ba"][i:i + 1], 0),
            (w["lru_lambda"][i:i + 1], 0)]


_COLUMN_SHARDED_OPERANDS = ("w_gate", "w_up", "w_ple_proj")


class _WeightFeed:
    def __init__(self, m, w, shards, vres):
        self.m, self.w, self.shards, self.vres = m, w, shards, vres

    def keys(self, carrier, i):
        plan = {"mm_in": [("w_o", i)] if i == 0 else [],
                "wkv_local": [("w_gate", i), ("w_up", i)], "wkv_state": [("w_down", i)],
                "mm_gate": [("w_ple_gate", i)], "mm_up": [("w_ple_proj", i), ("w_o", i + 1)],
                "mm_down": [("w_in", i + 1)], "mm_pgate": []}
        return [key for key in plan[carrier] if key[1] < self.m.nl]

    def blobs(self, keys):
        return [_gather_blob(self.shards[name][layer]) for name, layer in keys]

    def arrive(self, keys, gathered):
        m = self.m
        for (name, layer), got in zip(keys, gathered):
            full = got.reshape((N_XY,) + self.shards[name][layer].shape)
            if name in _COLUMN_SHARDED_OPERANDS:
                self.w[name][layer] = full
                continue
            full = _from_shards(full, _SHARD_AXIS[name] - 1)
            if name == "w_in":
                vres = self.vres[layer - 1] if layer > 0 else jnp.zeros((m.d, m.lv), BF16)
                self.w["wcat"][layer] = jnp.concatenate([full, vres, jnp.zeros((m.d, m.zw - m.din - m.lv), BF16)], axis=1)
            else:
                self.w[name][layer] = full


def _mm_fed(feed, carrier, i, a, b, **kw):
    keys = feed.keys(carrier, i) if feed is not None else []
    if not keys:
        return _mm(a, b, **kw)
    out, got = _mm(a, b, gather=feed.blobs(keys), **kw)
    feed.arrive(keys, got)
    return out


def _layer_fwd(m, w, i, h, p_bf, v_first_zs, feed=None):
    sv = {"h": h}
    t, dl, dr = m.t, m.dl, m.dr
    sv["u1"] = _norm_fwd(m, h, w["ln_mix"][i:i + 1], "norm_mix")
    z = sv["z"] = _mm_fed(feed, "mm_in", i, sv["u1"], w["wcat"][i], name="mm_in")
    off = 2 * dl // m.tcs
    sv["zs"] = _stage_fwd(_f_shiftmix, [(z, None, off)], [(w["mu_pad"][i], off)], [], [(m.zs, F32)],
                          axis=1, tile=m.tcs, rows=t, name="shiftmix")[0]
    tca = _tile(dl, (512, 256, 128))
    sv["xb"] = _stage_fwd(_f_conv, [(z, None, 0)], [(w["conv_a_w"][i], 0), (w["conv_a_b"][i:i + 1], 0)], [],
                          [(dl, F32)], axis=1, tile=tca, rows=t, name="conv_a")[0]
    sv["a"], b_in = _stage_fwd(_make_f_lru_gates(m.hl), [(sv["xb"], dl, 0)], _lru_gate_params(w, i), [],
                               [(dl, F32), (dl, F32)], axis=0, tile=m.tr, rows=t, name="lru_gates")
    sv["hl"] = _lru_scan(sv["a"], b_in, name="lru_scan")
    out_a = _stage_fwd(_f_lru_out, [(sv["hl"], dl, 0), (z, dl, 1)], [(w["lru_norm"][i:i + 1], 0)], [],
                       [(dl, BF16)], axis=0, tile=m.tr, rows=t, name="lru_out")[0]
    tiled, params = _rwkv_pre_operands(m, w, i, sv, v_first_zs, False)
    pre = _stage_fwd(_make_f_rwkv_pre(i > 0), tiled, params, [w["bb"]], [(dr, F32)] * 6,
                     axis=0, tile=m.tr, rows=t, name="rwkv_pre")
    sv["logw"], sv["k2"], sv["v2"], sv["kk"], sv["b"], sv["g"] = pre
    heads = [_to_heads(m, a) for a in (sv["zs"][:, :dr], sv["logw"], sv["k2"], sv["v2"], sv["kk"], sv["b"])]
    keys = [feed.keys(carrier, i) if feed is not None else [] for carrier in ("wkv_local", "wkv_state")]
    y_h, sv["states"], got = _wkv_fwd(*heads, name="wkv_fwd", gather=feed.blobs(keys[0]) if keys[0] else (),
                                      gather_state=feed.blobs(keys[1]) if keys[1] else ())
    if keys[0] or keys[1]:
        feed.arrive(keys[0] + keys[1], got)
    sv["y"] = _from_heads(m, y_h)
    tiled, params = _rwkv_post_operands(m, w, i, sv)
    out_b = _stage_fwd(_make_f_rwkv_post(m.n), tiled, params, [w["bb"]], [(dr, BF16)],
                       axis=0, tile=m.tr, rows=t, name="rwkv_post")[0]
    sv["cat"] = jnp.concatenate([out_a, out_b], axis=1)
    h2 = sv["h2"] = _mm(sv["cat"], w["w_o"][i], res=h, name="mm_o")
    sv["u2"] = _norm_fwd(m, h2, w["ln_ffn"][i:i + 1], "norm_ffn")
    sv["gpre"] = _mm_fed(feed, "mm_gate", i, sv["u2"], w["w_gate"][i], name="mm_gate")
    sv["up"] = _mm_fed(feed, "mm_up", i, sv["u2"], w["w_up"][i], name="mm_up")
    tcf = _tile(m.ff, (512, 256, 128))
    sv["act"] = _stage_fwd(_f_ffn_act, [(sv["gpre"], None, 0), (sv["up"], None, 0)],
                           [(w["conv_f_w"][i], 0), (w["conv_f_b"][i:i + 1], 0)], [], [(m.ff, BF16)],
                           axis=1, tile=tcf, rows=t, name="ffn_act")[0]
    h3 = sv["h3"] = _mm_fed(feed, "mm_down", i, sv["act"], w["w_down"][i], res=h2, name="mm_down")
    sv["u3"] = _norm_fwd(m, h3, w["ln_ple"][i:i + 1], "norm_ple")
    sv["eg"] = _mm_fed(feed, "mm_pgate", i, sv["u3"], w["w_ple_gate"][i], name="mm_pgate")
    sv["ep"] = _mm(p_bf, w["w_ple_proj"][i], name="mm_pproj")
    h4 = _stage_fwd(_f_ple, [(h3, m.d, 0), (sv["eg"], m.d, 0), (sv["ep"], m.d, 0)], [(w["ln_ple_post"][i:i + 1], 0)],
                    [], [(m.d, F32)], axis=0, tile=m.tr, rows=t, name="ple")[0]
    return h4, sv


def _layer_bwd(m, w, i, dh4, sv, p_bf, v_first_zs, dvf_in, pending=None, early=None):
    t, d, dl, dr = m.t, m.d, m.dl, m.dr
    g = {}
    deg, dep, g["ln_ple_post"] = _stage_bwd(
        _f_ple, [(sv["h3"], d, 0), (sv["eg"], d, 0), (sv["ep"], d, 0)], [(w["ln_ple_post"][i:i + 1], 0)], [],
        [(dh4, d, 0)], [(1, d, BF16), (2, d, BF16)], axis=0, tile=m.tr, rows=t, name="ple_bwd")
    du3 = _mm(deg, w["w_ple_gate"][i], tb=True, name="mm_pgate_dx")
    g["w_ple_gate"] = _mm(sv["u3"], deg, ta=True, name="mm_pgate_dw")
    g["w_ple_proj"] = _mm(p_bf, dep, ta=True, name="mm_pproj_dw", out_shards=m.dw_shards)
    dh3, g["ln_ple"] = _norm_bwd(m, sv["h3"], w["ln_ple"][i:i + 1], du3, dh4, "norm_ple_bwd")
    dh3_bf = dh3.astype(BF16)
    dact = _mm(dh3_bf, w["w_down"][i], tb=True, name="mm_down_dx")
    if pending is None:
        g["w_down"] = _mm(sv["act"], dh3_bf, ta=True, name="mm_down_dw")
    else:
        g["w_down"], gots = _mm(sv["act"], dh3_bf, ta=True, name="mm_down_dw", gather=pending.send_jobs())
        pending.after_send(gots)
    tcf = _tile(m.ff, (512, 256, 128))
    dgpre, dup, g["conv_f_w"], g["conv_f_b"] = _stage_bwd(
        _f_ffn_act, [(sv["gpre"], None, 0), (sv["up"], None, 0)], [(w["conv_f_w"][i], 0), (w["conv_f_b"][i:i + 1], 0)],
        [], [(dact, 0)], [(0, m.ff, BF16), (1, m.ff, BF16)], axis=1, tile=tcf, rows=t, ncols=m.ff, name="ffn_act_bwd")
    du2 = _mm(dgpre, w["w_gate"][i], tb=True, name="mm_gate_dx")
    du2 = _mm(dup, w["w_up"][i], tb=True, res=du2, name="mm_up_dx")
    g["w_gate"] = _mm(sv["u2"], dgpre, ta=True, name="mm_gate_dw", out_shards=m.dw_shards)
    g["w_up"] = _mm(sv["u2"], dup, ta=True, name="mm_up_dw", out_shards=m.dw_shards)
    dh2, g["ln_ffn"] = _norm_bwd(m, sv["h2"], w["ln_ffn"][i:i + 1], du2, dh3, "norm_ffn_bwd")
    dh2_bf = dh2.astype(BF16)
    dcat = _mm(dh2_bf, w["w_o"][i], tb=True, name="mm_o_dx")
    g["w_o"] = _mm(sv["cat"], dh2_bf, ta=True, name="mm_o_dw")
    tiled, params = _rwkv_post_operands(m, w, i, sv)
    dy, dr_a, dk2_a, dv2_a, dg, g["rwkv_lnx_w"], g["rwkv_lnx_b"], g["rk"] = _stage_bwd(
        _make_f_rwkv_post(m.n), tiled, params, [w["bb"]], [(dcat, dr, dl // dr)], [(j, dr, F32) for j in range(5)],
        axis=0, tile=m.trb, rows=t, name="rwkv_post_bwd")
    heads = [_to_heads(m, a) for a in (sv["zs"][:, :dr], sv["logw"], sv["k2"], sv["v2"], sv["kk"], sv["b"])]
    own = early(g) if early is not None else None

    def local_jobs(state_results):
        jobs = []
        if own is not None:
            own.after_send(state_results)
            jobs += own.exchange_jobs()
        if pending is not None:
            jobs += pending.exchange_jobs()
        return jobs

    dwkv, parts = _wkv_bwd(*heads, sv["states"], _to_heads(m, dy), name="wkv_bwd",
                           jobs_state=own.send_jobs() if own is not None else (), jobs_local=local_jobs)
    n_own = len(own.names) if own is not None else 0
    if own is not None:
        own.after_exchange(parts[:n_own])
    if pending is not None:
        pending.after_exchange(parts[n_own:])
    dr_b, dlw, dk2_b, dv2_b, dkk, db = [_from_heads(m, a) for a in dwkv]
    tiled, params = _rwkv_pre_operands(m, w, i, sv, v_first_zs, True)
    v_cots = [dv2_a, dv2_b] + ([dvf_in] if dvf_in is not None else [])
    cots = [(c, dr, 0) for c in [dr_a, dr_b, dlw, dk2_a, dk2_b] + v_cots + [dkk, db, dg]]
    ntil = len(tiled)
    dtiled = [(0, dr, F32), (1, dr, F32), (2, dr, F32), (3, m.lz, F32)] + ([(4, dr, F32)] if i > 0 else [])
    res = _stage_bwd(_make_f_rwkv_pre(i > 0, len(v_cots)), tiled, params, [w["bb"]], cots, dtiled,
                     axis=0, tile=m.trb, rows=t, name="rwkv_pre_bwd")
    d_r, d_k, d_v, d_lz = res[:4]
    dvf_out = res[4] if i > 0 else None
    pg = res[ntil:]
    g["rwkv_w0"], g["w2p"], g["rwkv_a0"], g["a2p"], g["g2p"], g["rwkv_kk"], g["rwkv_ka"] = pg[:7]
    if i > 0:
        g["rwkv_v0"], g["v2p"] = pg[7:9]
    dzs = jnp.concatenate([d_r, d_k, d_v, d_lz, jnp.zeros((t, m.zs - 3 * dr - m.lz), F32)], axis=1)
    off = 2 * dl // m.tcs
    dzr, g["mu_pad"] = _stage_bwd(_f_shiftmix, [(sv["z"], None, off)], [(w["mu_pad"][i], off)], [], [(dzs, 0)],
                                  [(0, m.zs, BF16)], axis=1, tile=m.tcs, rows=t, ncols=m.zs, name="shiftmix_bwd")
    dhl, dya, g["lru_norm"] = _stage_bwd(
        _f_lru_out, [(sv["hl"], dl, 0), (sv["z"], dl, 1)], [(w["lru_norm"][i:i + 1], 0)], [], [(dcat, dl, 0)],
        [(0, dl, F32), (1, dl, BF16)], axis=0, tile=m.tr, rows=t, name="lru_out_bwd")
    da, db_in = _lru_scan_bwd(sv["a"], sv["hl"], dhl, name="lru_scan_bwd")
    dxb, g["lru_wx"], g["lru_wa"], g["lru_bx"], g["lru_ba"], g["lru_lambda"] = _stage_bwd(
        _make_f_lru_gates(m.hl), [(sv["xb"], dl, 0)], _lru_gate_params(w, i), [], [(da, dl, 0), (db_in, dl, 0)],
        [(0, dl, F32)], axis=0, tile=m.tr, rows=t, name="lru_gates_bwd")
    tca = _tile(dl, (512, 256, 128))
    dxa, g["conv_a_w"], g["conv_a_b"] = _stage_bwd(
        _f_conv, [(sv["z"], None, 0)], [(w["conv_a_w"][i], 0), (w["conv_a_b"][i:i + 1], 0)], [], [(dxb, 0)],
        [(0, dl, BF16)], axis=1, tile=tca, rows=t, ncols=dl, name="conv_a_bwd")
    dz = jnp.concatenate([dxa, dya, dzr], axis=1)
    du1 = _mm(dz, w["wcat"][i], tb=True, name="mm_in_dx")
    g["wcat"] = _mm(sv["u1"], dz, ta=True, name="mm_in_dw")
    dh, g["ln_mix"] = _norm_bwd(m, sv["h"], w["ln_mix"][i:i + 1], du1, dh2, "norm_mix_bwd")
    return dh, g, dvf_out, own


def _loss_head(m, h, g, tgt):
    tile, d = m.tr, m.d

    def body(h_ref, g_ref, t_ref, loss_ref, dh_ref, dg_ref):
        def f(hv, gv):
            err = _rms(hv, gv) - t_ref[...]
            return 0.5 * jnp.sum(jnp.mean(err * err, axis=-1))

        val, vjp = jax.vjp(f, h_ref[...], g_ref[...])
        dh, dg = vjp(jnp.ones((), F32))
        dh_ref[...] = dh

        @pl.when(pl.program_id(0) == 0)
        def _():
            dg_ref[...] = jnp.zeros_like(dg_ref)
            loss_ref[...] = jnp.zeros_like(loss_ref)

        dg_ref[...] += dg
        loss_ref[...] += jnp.full(loss_ref.shape, val, F32)

    row = pl.BlockSpec((tile, d), lambda i: (i, 0))
    return pl.pallas_call(
        body, name="loss_head", grid=(m.t // tile,),
        in_specs=[row, pl.BlockSpec((1, d), lambda i: (0, 0)), row],
        out_specs=[pl.BlockSpec((1, LANES_V7X), lambda i: (0, 0)), row, pl.BlockSpec((1, d), lambda i: (0, 0))],
        out_shape=[jax.ShapeDtypeStruct((1, LANES_V7X), F32), jax.ShapeDtypeStruct((m.t, d), F32),
                   jax.ShapeDtypeStruct((1, d), F32)],
        compiler_params=_cparams(("arbitrary",)),
    )(h, g, tgt)


def _local_step(m, w, x, p, tgt, feed=None, reducer=None):
    h = x
    saved = []
    p_bf = p.astype(BF16)
    for i in range(m.nl):
        h, sv = _layer_fwd(m, w, i, h, p_bf[i], saved[0]["zs"] if i > 0 else None, feed)
        saved.append(sv)
    loss_row, dh, d_ln_final = _loss_head(m, h, w["ln_final"], tgt)
    grads = [None] * m.nl
    reductions = [None] * m.nl
    dvf = None
    for i in reversed(range(m.nl)):
        pending = reductions[i + 1][1] if reducer is not None and i + 1 < m.nl else None
        early = functools.partial(reducer, i, which="early") if reducer is not None else None
        dh, grads[i], dvf_i, own = _layer_bwd(m, w, i, dh, saved[i], p_bf[i], saved[0]["zs"] if i > 0 else None,
                                              dvf if i == 0 else None, pending, early)
        if reducer is not None:
            reductions[i] = (own, reducer(i, grads[i], which="late"))
        if i > 0:
            dvf = dvf_i if dvf is None else dvf + dvf_i
    return loss_row, dh, grads, d_ln_final, reductions


_BIG = ("w_o", "w_gate", "w_up", "w_down", "w_ple_gate", "w_ple_proj")


def _lora_rows(m):
    o1 = m.lw
    o2 = o1 + m.la
    o3 = o2 + m.lg
    return {"w2p": (0, o1), "a2p": (o1, o2), "g2p": (o2, o3), "v2p": (o3, o3 + m.lv)}


def _prepare_weights(m, wf):
    w = {k: v for k, v in wf.items() if k not in _BIG and k not in ("w_in", "w_in_vres")}
    nl = m.nl
    for k in _BIG:
        w[k] = [wf[k][i].astype(BF16) for i in range(nl)] if k in wf else [None] * nl
    w["wcat"] = [None] * nl
    if "w_in" in wf:
        vres = jnp.concatenate([jnp.zeros((1, m.d, m.lv), BF16), wf["w_in_vres"].astype(BF16)], axis=0)
        pad = jnp.zeros((m.d, m.zw - m.din - m.lv), BF16)
        w["wcat"] = [jnp.concatenate([wf["w_in"][i].astype(BF16), vres[i], pad], axis=1) for i in range(nl)]
    mu_v = jnp.concatenate([jnp.zeros((1, m.lv), F32), wf["mu_shift_vres"]], axis=0)
    w["mu_pad"] = jnp.concatenate([jnp.zeros((nl, 2 * m.dl), F32), wf["mu_shift"], mu_v,
                                   jnp.zeros((nl, m.zw - m.din - m.lv), F32)], axis=1)[:, None, :]
    rows = _lora_rows(m)
    for name, src in (("w2p", "rwkv_w2"), ("a2p", "rwkv_a2"), ("g2p", "rwkv_g2"), ("v2p", "rwkv_v2")):
        lo, hi = rows[name]
        a = wf[src]
        w[name] = jnp.concatenate([jnp.zeros((a.shape[0], lo, m.dr), F32), a, jnp.zeros((a.shape[0], m.lz - hi, m.dr), F32)],
                                  axis=1)
    w["rk"] = wf["rwkv_rk"].reshape(nl, 1, m.dr)
    w["ln_final"] = wf["ln_final"].reshape(1, m.d)
    group = max(m.n, LANES_V7X)
    assert group % m.n == 0 and m.dr % group == 0
    head = jnp.arange(group, dtype=jnp.int32) // m.n
    w["bb"] = (head[:, None] == head[None, :]).astype(BF16)
    return w


def _unpack_grads(m, grads, d_ln_final, with_big=True):
    nl = m.nl
    out = {}

    def stack(key):
        return jnp.stack([grads[i][key] for i in range(nl)], axis=0)

    for k in (_BIG if with_big else ()) + ("conv_a_w", "conv_f_w", "lru_wx", "lru_wa"):
        out[k] = stack(k)
    for k in ("ln_mix", "conv_a_b", "lru_bx", "lru_ba", "lru_lambda", "lru_norm", "rwkv_w0", "rwkv_a0", "rwkv_kk",
              "rwkv_ka", "rwkv_lnx_w", "rwkv_lnx_b", "ln_ffn", "conv_f_b", "ln_ple", "ln_ple_post"):
        out[k] = stack(k)[:, 0, :]
    if with_big:
        out["w_in"] = stack("wcat")[:, :, :m.din]
    out["w_in_vres"] = jnp.stack([grads[i]["wcat"][:, m.din:m.din + m.lv] for i in range(1, nl)], axis=0)
    mu = stack("mu_pad")[:, 0, :]
    out["mu_shift"] = mu[:, :m.nsh]
    out["mu_shift_vres"] = mu[1:, m.nsh:m.nsh + m.lv]
    rows = _lora_rows(m)
    for name, dst in (("w2p", "rwkv_w2"), ("a2p", "rwkv_a2"), ("g2p", "rwkv_g2")):
        lo, hi = rows[name]
        out[dst] = stack(name)[:, lo:hi, :]
    lo, hi = rows["v2p"]
    out["rwkv_v2"] = jnp.stack([grads[i]["v2p"] for i in range(1, nl)], axis=0)[:, lo:hi, :]
    out["rwkv_v0"] = jnp.stack([grads[i]["rwkv_v0"] for i in range(1, nl)], axis=0)[:, 0, :]
    out["rwkv_rk"] = stack("rk").reshape(nl, m.h, m.n)
    out["ln_final"] = d_ln_final.reshape(m.d)
    return out


_ANY = pl.BlockSpec(memory_space=pl.ANY)


def _position():
    return lax.axis_index("x"), lax.axis_index("y"), lax.axis_index("c")


def _other_chips(x, y):
    return [(1 - x, y), (x, 1 - y), (1 - x, 1 - y)]


def _gather_blob(shard):
    rows, wd = shard.shape
    return shard.reshape(2, rows // 2, wd)


_JOB_SEMS = {"gather": 7, "pair_send": 1, "exchange": 3}


def _job_parts(job):
    return job if isinstance(job, tuple) else ("gather", job)


def _job_arrays(jobs):
    return [_job_parts(j)[1] for j in jobs]


def _job_out_shape(job):
    kind, arr = _job_parts(job)
    if kind == "gather":
        return jax.ShapeDtypeStruct((N_XY,) + arr.shape, arr.dtype)
    if kind == "pair_send":
        return jax.ShapeDtypeStruct((arr.shape[0], arr.shape[1] // 2, arr.shape[2]), arr.dtype)
    return jax.ShapeDtypeStruct(arr.shape, arr.dtype)


def _jobs_scratch(jobs):
    out = []
    for j in jobs:
        n = _JOB_SEMS[_job_parts(j)[0]]
        out += [pltpu.SemaphoreType.DMA((n,)), pltpu.SemaphoreType.DMA((n,))]
    return out


def _jobs_start(jobs, in_refs, out_refs, sems):
    for q, j in enumerate(jobs):
        _JOB_START[_job_parts(j)[0]](in_refs[q], out_refs[q], sems[2 * q], sems[2 * q + 1])


def _jobs_finish(jobs, in_refs, out_refs, sems):
    for q, j in enumerate(jobs):
        _JOB_FINISH[_job_parts(j)[0]](in_refs[q], out_refs[q], sems[2 * q], sems[2 * q + 1])


def _run_jobs(jobs, *, name):
    n = len(jobs)

    def body(*refs):
        _jobs_start(jobs, refs[:n], refs[n:2 * n], refs[2 * n:])
        _jobs_finish(jobs, refs[:n], refs[n:2 * n], refs[2 * n:])

    return pl.pallas_call(body, name=name, in_specs=[_ANY] * n, out_specs=[_ANY] * n,
                          out_shape=[_job_out_shape(j) for j in jobs], scratch_shapes=_jobs_scratch(jobs))(*_job_arrays(jobs))


def _gather_copies(in_ref, out_ref, send_sems, recv_sems):
    x, y, c = _position()
    me = 2 * x + y
    sends, hands, ici_in, d2d_in = [], [], [], []
    for k, (px, py) in enumerate(_other_chips(x, y)):
        landed = out_ref.at[2 * px + py, c]
        sends.append(pltpu.make_async_remote_copy(
            src_ref=in_ref.at[c], dst_ref=out_ref.at[me, c], send_sem=send_sems.at[k], recv_sem=recv_sems.at[k],
            device_id=(px, py, c), device_id_type=MESH))
        ici_in.append(pltpu.make_async_remote_copy(
            src_ref=in_ref.at[c], dst_ref=landed, send_sem=send_sems.at[k], recv_sem=recv_sems.at[k],
            device_id=(px, py, c), device_id_type=MESH))
        hands.append(pltpu.make_async_remote_copy(
            src_ref=landed, dst_ref=landed, send_sem=send_sems.at[3 + k], recv_sem=recv_sems.at[3 + k],
            device_id=(x, y, 1 - c), device_id_type=MESH))
        d2d_in.append(pltpu.make_async_remote_copy(
            src_ref=in_ref.at[c], dst_ref=out_ref.at[2 * px + py, 1 - c], send_sem=send_sems.at[3 + k],
            recv_sem=recv_sems.at[3 + k], device_id=(x, y, 1 - c), device_id_type=MESH))
    own = pltpu.make_async_remote_copy(src_ref=in_ref, dst_ref=out_ref.at[me], send_sem=send_sems.at[6],
                                       recv_sem=recv_sems.at[6], device_id=(x, y, 1 - c), device_id_type=MESH)
    sends.append(own)
    d2d_in.append(own)
    return sends, hands, ici_in, d2d_in


def _gather_start(in_ref, out_ref, send_sems, recv_sems):
    for cp in _gather_copies(in_ref, out_ref, send_sems, recv_sems)[0]:
        cp.start()


def _gather_finish(in_ref, out_ref, send_sems, recv_sems):
    sends, hands, ici_in, d2d_in = _gather_copies(in_ref, out_ref, send_sems, recv_sems)
    for arrived, hand in zip(ici_in, hands):
        arrived.wait_recv()
        hand.start()
    for arrived in d2d_in:
        arrived.wait_recv()
    for cp in sends + hands:
        cp.wait_send()


def _pair_send_copy(g_ref, out_ref, send_sems, recv_sems):
    x, y, c = _position()
    half = out_ref.shape[1]
    return pltpu.make_async_remote_copy(src_ref=g_ref.at[:, pl.ds((1 - c) * half, half), :], dst_ref=out_ref,
                                        send_sem=send_sems.at[0], recv_sem=recv_sems.at[0], device_id=(x, y, 1 - c),
                                        device_id_type=MESH)


def _exchange_copies(in_ref, out_ref, send_sems, recv_sems):
    x, y, c = _position()
    me = 2 * x + y
    sends, arrivals = [], []
    for k, (px, py) in enumerate(_other_chips(x, y)):
        sends.append(pltpu.make_async_remote_copy(
            src_ref=in_ref.at[2 * px + py], dst_ref=out_ref.at[me], send_sem=send_sems.at[k], recv_sem=recv_sems.at[k],
            device_id=(px, py, c), device_id_type=MESH))
        arrivals.append(pltpu.make_async_remote_copy(
            src_ref=in_ref.at[me], dst_ref=out_ref.at[2 * px + py], send_sem=send_sems.at[k], recv_sem=recv_sems.at[k],
            device_id=(px, py, c), device_id_type=MESH))
    return sends, arrivals


def _exchange_start(*refs):
    for cp in _exchange_copies(*refs)[0]:
        cp.start()


def _exchange_finish(*refs):
    sends, arrivals = _exchange_copies(*refs)
    for cp in arrivals:
        cp.wait_recv()
    for cp in sends:
        cp.wait_send()


_JOB_START = {"gather": _gather_start, "pair_send": lambda *refs: _pair_send_copy(*refs).start(), "exchange": _exchange_start}
_JOB_FINISH = {"gather": _gather_finish, "pair_send": lambda *refs: _pair_send_copy(*refs).wait(),
               "exchange": _exchange_finish}


def _pair_sum(g, got, pos, *, name):
    nq, r, wd = g.shape
    half = r // 2
    tr = _tile(half, (256, 128, 64, 32, 16, 8))
    nb = half // tr

    def body(c_ref, g_ref, got_ref, o_ref):
        o_ref[...] = (g_ref[...] + got_ref[...]).astype(o_ref.dtype)

    grid_spec = pltpu.PrefetchScalarGridSpec(
        num_scalar_prefetch=1, grid=(nq, nb),
        in_specs=[pl.BlockSpec((1, tr, wd), lambda q, j, c_ref: (q, c_ref[0] * nb + j, 0)),
                  pl.BlockSpec((1, tr, wd), lambda q, j, c_ref: (q, j, 0))],
        out_specs=pl.BlockSpec((1, tr, wd), lambda q, j, c_ref: (q, j, 0)))
    return pl.pallas_call(body, name=name, grid_spec=grid_spec, out_shape=jax.ShapeDtypeStruct((nq, half, wd), BF16),
                          compiler_params=_cparams(("arbitrary", "arbitrary")))(pos[0], g, got)


def _chip_sum(parts, pb, pos, *, name):
    nq, half, wd = parts.shape
    tr = _tile(half, (256, 128, 64, 32, 16, 8))
    nb = half // tr

    def body(c_ref, x_ref, y_ref, p_ref, own_ref, o_ref):
        chip = 2 * x_ref[0] + y_ref[0]
        own = own_ref[0].astype(F32)
        acc = None
        for q in range(nq):
            term = jnp.where(chip == q, own, p_ref[q].astype(F32))
            acc = term if acc is None else acc + term
        o_ref[...] = acc

    grid_spec = pltpu.PrefetchScalarGridSpec(
        num_scalar_prefetch=3, grid=(nb,),
        in_specs=[pl.BlockSpec((nq, tr, wd), lambda j, c_ref, x_ref, y_ref: (0, j, 0)),
                  pl.BlockSpec((1, tr, wd), lambda j, c_ref, x_ref, y_ref: (2 * x_ref[0] + y_ref[0], j, 0))],
        out_specs=pl.BlockSpec((tr, wd), lambda j, c_ref, x_ref, y_ref: (c_ref[0] * nb + j, 0)))
    return pl.pallas_call(body, name=name, grid_spec=grid_spec, out_shape=jax.ShapeDtypeStruct((2 * half, wd), F32),
                          compiler_params=_cparams(("arbitrary",)))(*pos, parts, pb)


def _pair_gather(full, *, name):
    r, wd = full.shape
    half = r // 2

    def body(in_ref, out_ref, send_sem, recv_sem):
        x, y, c = _position()
        mine = out_ref.at[pl.ds(c * half, half), :]
        cp = pltpu.make_async_remote_copy(src_ref=mine, dst_ref=mine, send_sem=send_sem, recv_sem=recv_sem,
                                          device_id=(x, y, 1 - c), device_id_type=MESH)
        cp.start()
        pltpu.make_async_remote_copy(src_ref=mine, dst_ref=out_ref.at[pl.ds((1 - c) * half, half), :], send_sem=send_sem,
                                     recv_sem=recv_sem, device_id=(x, y, 1 - c), device_id_type=MESH).wait_recv()
        cp.wait_send()

    return pl.pallas_call(
        body, name=name, in_specs=[_ANY], out_specs=_ANY, out_shape=jax.ShapeDtypeStruct(full.shape, full.dtype),
        input_output_aliases={0: 0}, scratch_shapes=[pltpu.SemaphoreType.DMA(()), pltpu.SemaphoreType.DMA(())],
    )(full)


class _GradReduce:
    def __init__(self, slabs, pos, tag):
        self.names, self.slabs, self.pos, self.tag = list(slabs), [slabs[k] for k in slabs], pos, tag
        self.pb = self.out = None

    def send_jobs(self):
        return [("pair_send", g) for g in self.slabs]

    def after_send(self, gots):
        self.pb = [_pair_sum(g, got, self.pos, name=f"rs_pair_sum_{k}_{self.tag}")
                   for k, g, got in zip(self.names, self.slabs, gots)]

    def exchange_jobs(self):
        return [("exchange", pb) for pb in self.pb]

    def after_exchange(self, parts):
        full = [_chip_sum(pt, pb, self.pos, name=f"rs_chip_sum_{k}_{self.tag}") for k, pt, pb in zip(self.names, parts, self.pb)]
        self.out = {k: _pair_gather(f, name=f"rs_pair_gather_{k}_{self.tag}") for k, f in zip(self.names, full)}

    def run(self):
        if self.pb is None:
            self.after_send(_run_jobs(self.send_jobs(), name="rs_pair_send_" + self.tag))
        if self.out is None:
            self.after_exchange(_run_jobs(self.exchange_jobs(), name="rs_exchange_" + self.tag))
        return self.out


def _all_reduce_small(vec, *, name):
    r, wd = vec.shape

    def body(in_ref, out_ref, slots, send_sems, recv_sems):
        x, y, c = _position()
        me = 4 * x + 2 * y + c
        flips = [(fx, fy, fc) for fx in (0, 1) for fy in (0, 1) for fc in (0, 1) if fx + fy + fc]
        peers = [(1 - x if fx else x, 1 - y if fy else y, 1 - c if fc else c) for fx, fy, fc in flips]
        sends = []
        for k, peer in enumerate(peers):
            cp = pltpu.make_async_remote_copy(src_ref=in_ref, dst_ref=slots.at[me], send_sem=send_sems.at[k],
                                              recv_sem=recv_sems.at[k], device_id=peer, device_id_type=MESH)
            cp.start()
            sends.append(cp)
        slots[me] = in_ref[...]
        for k, (px, py, pc) in enumerate(peers):
            pltpu.make_async_remote_copy(src_ref=in_ref, dst_ref=slots.at[4 * px + 2 * py + pc], send_sem=send_sems.at[k],
                                         recv_sem=recv_sems.at[k], device_id=(px, py, pc), device_id_type=MESH).wait_recv()
        for cp in sends:
            cp.wait_send()
        acc = slots[0]
        for q in range(1, N_DEV):
            acc = acc + slots[q]
        out_ref[...] = acc

    vm = pl.BlockSpec(memory_space=pltpu.VMEM)
    return pl.pallas_call(
        body, name=name, in_specs=[vm], out_specs=vm, out_shape=jax.ShapeDtypeStruct((r, wd), F32),
        scratch_shapes=[pltpu.VMEM((N_DEV, r, wd), F32), pltpu.SemaphoreType.DMA((N_DEV - 1,)),
                        pltpu.SemaphoreType.DMA((N_DEV - 1,))],
        compiler_params=_cparams(),
    )(vec)


def _adamw(w, g, m, v, *, name, jobs=()):
    r, wd = w.shape
    tr = _tile(r, (256, 128, 64, 32, 16, 8))
    nj, steps = len(jobs), r // tr

    def body(*refs):
        w_ref, g_ref, m_ref, v_ref = refs[:4]
        j_in, (d_ref, m_out, v_out) = refs[4:4 + nj], refs[4 + nj:7 + nj]
        j_out, j_sems = refs[7 + nj:7 + 2 * nj], refs[7 + 2 * nj:]
        if nj:
            @pl.when(pl.program_id(0) == 0)
            def _():
                _jobs_start(jobs, j_in, j_out, j_sems)

        gv = g_ref[...]
        m_new = ADAM_B1 * m_ref[...] + (1.0 - ADAM_B1) * gv
        v_new = ADAM_B2 * v_ref[...] + (1.0 - ADAM_B2) * (gv * gv)
        m_hat = m_new / (1.0 - ADAM_B1 ** ADAM_STEP)
        v_hat = v_new / (1.0 - ADAM_B2 ** ADAM_STEP)
        d_ref[...] = -ADAM_LR * (m_hat / (jnp.sqrt(v_hat) + ADAM_EPS) + ADAM_WD * w_ref[...])
        m_out[...] = m_new
        v_out[...] = v_new
        if nj:
            @pl.when(pl.program_id(0) == steps - 1)
            def _():
                _jobs_finish(jobs, j_in, j_out, j_sems)

    spec = pl.BlockSpec((tr, wd), lambda j: (j, 0))
    res = pl.pallas_call(body, name=name, grid=(steps,), in_specs=[spec] * 4 + [_ANY] * nj, out_specs=[spec] * 3 + [_ANY] * nj,
                         out_shape=[jax.ShapeDtypeStruct((r, wd), F32)] * 3 + [_job_out_shape(j) for j in jobs],
                         scratch_shapes=_jobs_scratch(jobs), compiler_params=_cparams(("arbitrary",)),
                         )(w, g, m, v, *_job_arrays(jobs))
    return list(res[:3]), list(res[3:])


_WEIGHTS = ("ln_mix", "w_in", "w_in_vres", "mu_shift", "mu_shift_vres", "conv_a_w", "conv_a_b", "lru_wx", "lru_bx", "lru_wa",
            "lru_ba", "lru_lambda", "lru_norm", "rwkv_w0", "rwkv_w2", "rwkv_a0", "rwkv_a2", "rwkv_v0", "rwkv_v2", "rwkv_g2",
            "rwkv_kk", "rwkv_ka", "rwkv_rk", "rwkv_lnx_w", "rwkv_lnx_b", "w_o", "ln_ffn", "w_gate", "w_up", "conv_f_w",
            "conv_f_b", "w_down", "ln_ple", "w_ple_gate", "w_ple_proj", "ln_ple_post", "ln_final")
_SHARD_AXIS = {"w_in": 2, "w_in_vres": 1, "conv_a_w": 2, "lru_wx": 2, "lru_wa": 2, "rwkv_w2": 2, "rwkv_a2": 2, "rwkv_v2": 2,
               "rwkv_g2": 2, "w_o": 1, "w_gate": 2, "w_up": 2, "conv_f_w": 2, "w_down": 1, "w_ple_gate": 1, "w_ple_proj": 2}
_BIG_SHARDED = ("w_in",) + _BIG
_SMALL_SHARDED = tuple(k for k in _WEIGHTS if k in _SHARD_AXIS and k not in _BIG_SHARDED)
_REPLICATED = tuple(k for k in _WEIGHTS if k not in _SHARD_AXIS)
PACK_WIDTH = 512


def _to_shards(g, axis):
    n = g.shape[axis] // N_XY
    return jnp.moveaxis(g.reshape(g.shape[:axis] + (N_XY, n) + g.shape[axis + 1:]), axis, 0)


def _from_shards(s, axis):
    s = jnp.moveaxis(s, 0, axis)
    return s.reshape(s.shape[:axis] + (N_XY * s.shape[axis + 1],) + s.shape[axis + 2:])


def _pack(arrs, lead, width, row_mult):
    lead_shape = arrs[0].shape[:lead]
    flat = jnp.concatenate([a.reshape(lead_shape + (-1,)) for a in arrs], axis=-1)
    n = flat.shape[-1]
    total = _round_up(n, width * row_mult)
    flat = jnp.pad(flat, [(0, 0)] * lead + [(0, total - n)])
    return flat.reshape(lead_shape + (total // width, width))


def _unpack(packed, shapes):
    flat = packed.reshape(-1)
    out, o = [], 0
    for s in shapes:
        n = 1
        for dim in s:
            n *= dim
        out.append(flat[o:o + n].reshape(s))
        o += n
    return out


def _as2d(a):
    return a.reshape(-1, a.shape[-1])


def kernel(x, p, ln_mix, w_in, w_in_vres, mu_shift, mu_shift_vres, conv_a_w, conv_a_b, lru_wx, lru_bx, lru_wa, lru_ba, lru_lambda, lru_norm, rwkv_w0, rwkv_w2, rwkv_a0, rwkv_a2, rwkv_v0, rwkv_v2, rwkv_g2, rwkv_kk, rwkv_ka, rwkv_rk, rwkv_lnx_w, rwkv_lnx_b, w_o, ln_ffn, w_gate, w_up, conv_f_w, conv_f_b, w_down, ln_ple, w_ple_gate, w_ple_proj, ln_ple_post, ln_final, loss_target, m_ln_mix, m_w_in, m_w_in_vres, m_mu_shift, m_mu_shift_vres, m_conv_a_w, m_conv_a_b, m_lru_wx, m_lru_bx, m_lru_wa, m_lru_ba, m_lru_lambda, m_lru_norm, m_rwkv_w0, m_rwkv_w2, m_rwkv_a0, m_rwkv_a2, m_rwkv_v0, m_rwkv_v2, m_rwkv_g2, m_rwkv_kk, m_rwkv_ka, m_rwkv_rk, m_rwkv_lnx_w, m_rwkv_lnx_b, m_w_o, m_ln_ffn, m_w_gate, m_w_up, m_conv_f_w, m_conv_f_b, m_w_down, m_ln_ple, m_w_ple_gate, m_w_ple_proj, m_ln_ple_post, m_ln_final, v_ln_mix, v_w_in, v_w_in_vres, v_mu_shift, v_mu_shift_vres, v_conv_a_w, v_conv_a_b, v_lru_wx, v_lru_bx, v_lru_wa, v_lru_ba, v_lru_lambda, v_lru_norm, v_rwkv_w0, v_rwkv_w2, v_rwkv_a0, v_rwkv_a2, v_rwkv_v0, v_rwkv_v2, v_rwkv_g2, v_rwkv_kk, v_rwkv_ka, v_rwkv_rk, v_rwkv_lnx_w, v_rwkv_lnx_b, v_w_o, v_ln_ffn, v_w_gate, v_w_up, v_conv_f_w, v_conv_f_b, v_w_down, v_ln_ple, v_w_ple_gate, v_w_ple_proj, v_ln_ple_post, v_ln_final):
    a = dict(locals())
    x2, p, tgt = a["x"][0], a["p"][:, 0], a["loss_target"][0]
    pos = tuple(lax.axis_index(ax).astype(jnp.int32).reshape(1) for ax in ("c", "x", "y"))

    wf = {k: a[k] for k in _REPLICATED}
    small_shapes = [a[k].shape for k in _SMALL_SHARDED]
    shards = {k: [a[k][i].astype(BF16) for i in range(a[k].shape[0])] for k in _BIG_SHARDED}
    packed = _pack([a[k] for k in _SMALL_SHARDED], 0, PACK_WIDTH, 16)
    got_small, got_w_in = _run_jobs([_gather_blob(packed), _gather_blob(shards["w_in"][0])], name="ag_first")
    got_small = got_small.reshape((N_XY,) + packed.shape)
    pieces = [_unpack(got_small[q], small_shapes) for q in range(N_XY)]
    for j, k in enumerate(_SMALL_SHARDED):
        wf[k] = _from_shards(jnp.stack([pieces[q][j] for q in range(N_XY)], axis=0), _SHARD_AXIS[k])

    m = _make_dims(x2, p, wf)
    m.dw_shards = N_XY
    w = _prepare_weights(m, wf)
    feed = _WeightFeed(m, w, shards, wf["w_in_vres"].astype(BF16))
    feed.arrive([("w_in", 0)], [got_w_in])
    def reducer(i, g, which):
        names = ("w_in",) if which == "late" else _BIG
        full = {k: (g["wcat"][:, :m.din] if k == "w_in" else g[k]) for k in names}
        slabs = {k: full[k] if full[k].ndim == 3 else _to_shards(full[k], _SHARD_AXIS[k] - 1) for k in names}
        return _GradReduce(slabs, pos, f"{which}_{i}")

    loss_row, dx, grads, d_ln_final, reductions = _local_step(m, w, x2, p, tgt, feed, reducer)
    gfull = _unpack_grads(m, grads, d_ln_final, with_big=False)
    loss = lax.psum(loss_row[0, 0], ("x", "y", "c"))

    gs = _pack([_to_shards(gfull[k], _SHARD_AXIS[k]) for k in _SMALL_SHARDED], 1, PACK_WIDTH, 32)
    g_small = _GradReduce({"small": gs}, pos, "small").run()["small"]
    rep_shapes = [a[k].shape for k in _REPLICATED]
    g_rep = _all_reduce_small(_pack([gfull[k] for k in _REPLICATED], 0, LANES_V7X, 8), name="ar_replicated")

    last = reductions[0][1]
    gred, delta, new_m, new_v = {}, {}, {}, {}
    carried = {"w_gate": (last.send_jobs, last.after_send), "w_up": (last.exchange_jobs, last.after_exchange)}
    for k in _BIG + ("w_in",):
        which = 1 if k == "w_in" else 0
        per_layer = [reductions[i][which].run()[k] for i in range(m.nl)]
        gred[k] = jnp.stack(per_layer, axis=0).reshape(a[k].shape)
        make_jobs, after = carried.get(k, (lambda: (), None))
        res, job_results = _adamw(_as2d(a[k]), _as2d(gred[k]), _as2d(a["m_" + k]), _as2d(a["v_" + k]), name="adamw_" + k,
                                  jobs=make_jobs())
        if after is not None:
            after(job_results)
        delta[k], new_m[k], new_v[k] = (r.reshape(a[k].shape) for r in res)
    for names, shapes, g_packed, width, mult, tag in ((_SMALL_SHARDED, small_shapes, g_small, PACK_WIDTH, 32, "small"),
                                                      (_REPLICATED, rep_shapes, g_rep, LANES_V7X, 8, "replicated")):
        packs = [_pack([a[pre + k] for k in names], 0, width, mult) for pre in ("", "m_", "v_")]
        res, _ = _adamw(packs[0], g_packed, packs[1], packs[2], name="adamw_" + tag)
        for dst, r in zip((gred, delta, new_m, new_v), [g_packed] + list(res)):
            dst.update(zip(names, _unpack(r, shapes)))
    return (loss, dx[None], *[gred[k] for k in _WEIGHTS], *[delta[k] for k in _WEIGHTS],
            *[new_m[k] for k in _WEIGHTS], *[new_v[k] for k in _WEIGHTS])
```

```python
import functools

import jax
import jax.numpy as jnp
from jax import lax
from jax.experimental import pallas as pl
from jax.experimental.pallas import tpu as pltpu

F32 = jnp.float32
BF16 = jnp.bfloat16
HIGHEST = lax.Precision.HIGHEST
MESH = pl.DeviceIdType.MESH

RMS_EPS = 1e-6
LNX_EPS = 64e-5
LRU_C = 8.0
ADAM_LR = 0.001
ADAM_B1 = 0.9
ADAM_B2 = 0.999
ADAM_EPS = 1e-08
ADAM_WD = 0.01
ADAM_STEP = 10

LANES_V7X = 128
VMEM_LIMIT_V7X = 60 * 1024 * 1024
WKV_CHUNK = 16
N_XY = 4
N_DEV = 8


def _cparams(sem=None, **kw):
    if sem is not None:
        kw["dimension_semantics"] = sem
    return pltpu.CompilerParams(vmem_limit_bytes=VMEM_LIMIT_V7X, **kw)


def _tile(dim, prefs):
    for t in prefs:
        if dim % t == 0:
            return t
    return dim


def _round_up(n, m):
    return (n + m - 1) // m * m


MM_MAX_TK = 2816


def _tile_k(kdim):
    best = None
    for t in range(LANES_V7X, min(kdim, MM_MAX_TK) + 1, LANES_V7X):
        if kdim % t == 0:
            best = t
    return best or kdim


def _mm(a, b, *, ta=False, tb=False, res=None, out_dtype=F32, name, gather=(), out_shards=0):
    if ta:
        kdim, m = a.shape
    else:
        m, kdim = a.shape
    bs = b.shape[0] if b.ndim == 3 else 0
    if bs:
        b_rows, b_cols = b.shape[1], bs * b.shape[2]
    else:
        b_rows, b_cols = b.shape
    n = b_rows if tb else b_cols
    assert (b_cols if tb else b_rows) == kdim
    per_shard = (lambda total, s: total // s if s else total)
    tk = _tile_k(per_shard(kdim, bs) if tb else kdim)
    tm = _tile(m, (2048, 1024, 512, 256, 128) if tk <= 2048 else (1024, 512, 256, 128))
    tn = _tile(per_shard(per_shard(n, out_shards), 0 if tb else bs), (512, 256, 128))
    nk = kdim // tk
    ni, nj = m // tm, n // tn
    a_spec = pl.BlockSpec((tk, tm), lambda i, j, k: (k, i)) if ta else pl.BlockSpec((tm, tk), lambda i, j, k: (i, k))
    if bs and tb:
        kps = b.shape[2] // tk
        b_spec = pl.BlockSpec((None, tn, tk), lambda i, j, k: (k // kps, j, k % kps))
    elif bs:
        nps = b.shape[2] // tn
        b_spec = pl.BlockSpec((None, tk, tn), lambda i, j, k: (j // nps, k, j % nps))
    else:
        b_spec = pl.BlockSpec((tn, tk), lambda i, j, k: (j, k)) if tb else pl.BlockSpec((tk, tn), lambda i, j, k: (k, j))
    if out_shards:
        assert res is None
        ops = n // out_shards // tn
        o_spec = pl.BlockSpec((None, tm, tn), lambda i, j, k: (j // ops, i, j % ops))
        o_shape = jax.ShapeDtypeStruct((out_shards, m, n // out_shards), out_dtype)
    else:
        o_spec = pl.BlockSpec((tm, tn), lambda i, j, k: (i, j))
        o_shape = jax.ShapeDtypeStruct((m, n), out_dtype)
    dn = (((0 if ta else 1,), (1 if tb else 0,)), ((), ()))
    has_res = res is not None
    ng = len(gather)
    nin = 2 + has_res

    def body(*refs):
        a_ref, b_ref = refs[:2]
        r_ref = refs[2] if has_res else None
        g_in, o_ref, g_out = refs[nin:nin + ng], refs[nin + ng], refs[nin + ng + 1:nin + 2 * ng + 1]
        scratch = refs[nin + 2 * ng + 1:]
        acc_ref = scratch[0] if nk > 1 else None
        g_sems = scratch[1 if nk > 1 else 0:]
        i, j, k = pl.program_id(0), pl.program_id(1), pl.program_id(2)

        if ng:
            @pl.when((i == 0) & (j == 0) & (k == 0))
            def _():
                _jobs_start(gather, g_in, g_out, g_sems)

        def finish(acc):
            if has_res:
                acc = acc + r_ref[...].astype(F32)
            o_ref[...] = acc.astype(out_dtype)

        prod = lax.dot_general(a_ref[...], b_ref[...], dn, preferred_element_type=F32)
        if nk == 1:
            finish(prod)
        else:
            @pl.when(k == 0)
            def _():
                acc_ref[...] = prod

            @pl.when(k > 0)
            def _():
                acc_ref[...] += prod

            @pl.when(k == nk - 1)
            def _():
                finish(acc_ref[...])

        if ng:
            @pl.when((i == ni - 1) & (j == nj - 1) & (k == nk - 1))
            def _():
                _jobs_finish(gather, g_in, g_out, g_sems)

    ins = [a, b] + ([res] if has_res else []) + _job_arrays(gather)
    in_specs = [a_spec, b_spec] + ([o_spec] if has_res else []) + [_ANY] * ng
    scratch = ([pltpu.VMEM((tm, tn), F32)] if nk > 1 else []) + _jobs_scratch(gather)
    sem = ("arbitrary",) * 3 if ng else ("parallel", "parallel", "arbitrary")
    out = pl.pallas_call(
        body, name=name, grid=(ni, nj, nk), in_specs=in_specs, out_specs=[o_spec] + [_ANY] * ng,
        out_shape=[o_shape] + [_job_out_shape(g) for g in gather],
        scratch_shapes=scratch, compiler_params=_cparams(sem),
    )(*ins)
    return (out[0], list(out[1:])) if ng else out[0]


def _stage_specs(axis, tile, tiled, params, consts, rows):
    specs = []
    for arr, width, cblk in tiled:
        if axis == 0:
            specs.append(pl.BlockSpec((tile, width), functools.partial(lambda i, c: (i, c), c=cblk)))
        else:
            specs.append(pl.BlockSpec((rows, tile), functools.partial(lambda i, c: (0, i + c), c=cblk)))
    for arr, cblk in params:
        if axis == 0:
            specs.append(pl.BlockSpec(arr.shape, functools.partial(lambda i, nd: (0,) * nd, nd=arr.ndim)))
        else:
            specs.append(pl.BlockSpec((arr.shape[0], tile), functools.partial(lambda i, c: (0, i + c), c=cblk)))
    for arr in consts:
        specs.append(pl.BlockSpec(arr.shape, functools.partial(lambda i, nd: (0,) * nd, nd=arr.ndim)))
    return specs


def _stage_fwd(fn, tiled, params, consts, outs, *, axis, tile, rows, name):
    nt, npar, nc = len(tiled), len(params), len(consts)
    ntiles = (rows // tile) if axis == 0 else (outs[0][0] // tile)

    def body(*refs):
        ins = refs[: nt + npar + nc]
        orefs = refs[nt + npar + nc:]
        vals = [r[...].astype(F32) for r in ins[: nt + npar]] + [r[...] for r in ins[nt + npar:]]
        ctx = pl.program_id(0) * tile
        res = fn(ctx, *vals)
        for o_ref, o in zip(orefs, res):
            o_ref[...] = o.astype(o_ref.dtype)

    if axis == 0:
        out_specs = [pl.BlockSpec((tile, w), lambda i: (i, 0)) for w, _ in outs]
    else:
        out_specs = [pl.BlockSpec((rows, tile), lambda i: (0, i)) for w, _ in outs]
    res = pl.pallas_call(
        body, name=name, grid=(ntiles,),
        in_specs=_stage_specs(axis, tile, tiled, params, consts, rows), out_specs=out_specs,
        out_shape=[jax.ShapeDtypeStruct((rows, w), dt) for w, dt in outs],
        compiler_params=_cparams(("arbitrary",)),
    )(*[t[0] for t in tiled], *[p[0] for p in params], *consts)
    return res


def _stage_bwd(fn, tiled, params, consts, cots, dtiled, *, axis, tile, rows, name, ncols=None):
    nt, npar, nc, nco = len(tiled), len(params), len(consts), len(cots)
    ntiles = (rows // tile) if axis == 0 else (ncols // tile)
    didx = [d[0] for d in dtiled]

    def body(*refs):
        ins = refs[: nt + npar + nc]
        crefs = refs[nt + npar + nc: nt + npar + nc + nco]
        orefs = refs[nt + npar + nc + nco:]
        vals = [r[...].astype(F32) for r in ins[: nt + npar]] + [r[...] for r in ins[nt + npar:]]
        ctx = pl.program_id(0) * tile

        def g(*dv):
            full = list(vals)
            for j, ix in enumerate(didx):
                full[ix] = dv[j]
            for j in range(npar):
                full[nt + j] = dv[len(didx) + j]
            return tuple(fn(ctx, *full))

        prim = [vals[ix] for ix in didx] + [vals[nt + j] for j in range(npar)]
        _, vjp = jax.vjp(g, *prim)
        grads = vjp(tuple(c[...].astype(F32) for c in crefs))
        for j in range(len(didx)):
            orefs[j][...] = grads[j].astype(orefs[j].dtype)
        for j in range(npar):
            o_ref = orefs[len(didx) + j]
            gp = grads[len(didx) + j]
            if axis == 0:
                @pl.when(pl.program_id(0) == 0)
                def _(o_ref=o_ref):
                    o_ref[...] = jnp.zeros_like(o_ref)

                o_ref[...] += gp
            else:
                o_ref[...] = gp

    if axis == 0:
        cot_specs = [pl.BlockSpec((tile, w), functools.partial(lambda i, c: (i, c), c=cb)) for _, w, cb in cots]
        out_specs = [pl.BlockSpec((tile, w), lambda i: (i, 0)) for _, w, _ in dtiled]
        out_specs += [pl.BlockSpec(p.shape, functools.partial(lambda i, nd: (0,) * nd, nd=p.ndim)) for p, _ in params]
        out_shape = [jax.ShapeDtypeStruct((rows, w), dt) for _, w, dt in dtiled]
        out_shape += [jax.ShapeDtypeStruct(p.shape, F32) for p, _ in params]
    else:
        cot_specs = [pl.BlockSpec((rows, tile), functools.partial(lambda i, c: (0, i + c), c=cb)) for _, cb in cots]
        out_specs = [pl.BlockSpec((rows, tile), lambda i: (0, i)) for _ in dtiled]
        out_specs += [pl.BlockSpec((p.shape[0], tile), lambda i: (0, i)) for p, _ in params]
        out_shape = [jax.ShapeDtypeStruct((rows, w), dt) for _, w, dt in dtiled]
        out_shape += [jax.ShapeDtypeStruct((p.shape[0], ncols), F32) for p, _ in params]
    return pl.pallas_call(
        body, name=name, grid=(ntiles,),
        in_specs=_stage_specs(axis, tile, tiled, params, consts, rows) + cot_specs, out_specs=out_specs,
        out_shape=out_shape, compiler_params=_cparams(("arbitrary",)),
    )(*[t[0] for t in tiled], *[p[0] for p in params], *consts, *[c[0] for c in cots])


def _rms(x, g):
    return x * lax.rsqrt(jnp.mean(x * x, axis=-1, keepdims=True) + RMS_EPS) * g


def _row_mask(x, k, first):
    t = lax.broadcasted_iota(jnp.int32, x.shape, 0)
    keep = (t >= k) if first else (t < x.shape[0] - k)
    return jnp.where(keep, x, 0.0)


@functools.partial(jax.custom_vjp, nondiff_argnums=(1,))
def _shift_down(x, k):
    return _row_mask(pltpu.roll(x, k, 0), k, True)


def _shift_down_fwd(x, k):
    return _shift_down(x, k), None


def _shift_down_bwd(k, _, g):
    return (_row_mask(pltpu.roll(g, g.shape[0] - k, 0), k, False),)


_shift_down.defvjp(_shift_down_fwd, _shift_down_bwd)


def _dwconv(x, w, b):
    kw = w.shape[0]
    out = x * w[kw - 1:kw] + b
    for j in range(kw - 1):
        out = out + _shift_down(x, kw - 1 - j) * w[j:j + 1]
    return out


def _f_norm(ctx, x, g):
    return (_rms(x, g),)


def _f_norm_res(ctx, x, g):
    return (_rms(x, g), x)


def _f_shiftmix(ctx, z, mu):
    return (z + (_shift_down(z, 1) - z) * mu,)


def _f_conv(ctx, x, w, b):
    return (_dwconv(x, w, b),)


def _f_ffn_act(ctx, gpre, up, w, b):
    return (jax.nn.gelu(_dwconv(gpre, w, b)) * up,)


def _make_f_lru_gates(heads):
    def fn(ctx, xb, wx, wa, bx, ba, lam):
        blk = xb.shape[1] // heads
        px, pa = [], []
        for h in range(heads):
            xh = xb[:, h * blk:(h + 1) * blk]
            px.append(jnp.dot(xh, wx[h], preferred_element_type=F32))
            pa.append(jnp.dot(xh, wa[h], preferred_element_type=F32))
        px = px[0] if heads == 1 else jnp.concatenate(px, axis=1)
        pa = pa[0] if heads == 1 else jnp.concatenate(pa, axis=1)
        gate_x = jax.nn.sigmoid(px + bx)
        gate_a = jax.nn.sigmoid(pa + ba)
        log_a = -LRU_C * gate_a * jax.nn.softplus(-lam)
        a = jnp.exp(log_a)
        mult = jnp.sqrt(1.0 - jnp.exp(2.0 * log_a))
        t = ctx + lax.broadcasted_iota(jnp.int32, xb.shape, 0)
        mult = jnp.where(t == 0, 1.0, mult)
        return a, xb * gate_x * mult

    return fn


def _f_lru_out(ctx, hl, ya, g):
    return (_rms(hl * jax.nn.gelu(ya), g),)


def _headsum_3pass(x, bb):
    hi = x.astype(BF16)
    r1 = x - hi.astype(F32)
    mid = r1.astype(BF16)
    lo = (r1 - mid.astype(F32)).astype(BF16)
    width, group = x.shape[1], bb.shape[0]
    out = []
    for g0 in range(0, width, group):
        cols = slice(g0, g0 + group)
        out.append(jnp.dot(hi[:, cols], bb, preferred_element_type=F32) + jnp.dot(mid[:, cols], bb, preferred_element_type=F32)
                   + jnp.dot(lo[:, cols], bb, preferred_element_type=F32))
    return out[0] if len(out) == 1 else jnp.concatenate(out, axis=1)


@jax.custom_vjp
def _headsum(x, bb):
    return _headsum_3pass(x, bb)


def _headsum_fwd(x, bb):
    return _headsum_3pass(x, bb), bb


def _headsum_bwd(bb, g):
    return _headsum_3pass(g, bb), None


_headsum.defvjp(_headsum_fwd, _headsum_bwd)


def _make_f_rwkv_pre(has_vres, v_uses=0):
    def fn(ctx, *args):
        if v_uses:
            r, args = args[0], args[1:]
        if has_vres:
            k, v, lz, vf, w0, w2, a0, a2, g2, kkw, ka, v0, v2, bb = args
        else:
            k, v, lz, w0, w2, a0, a2, g2, kkw, ka, bb = args
        w_log = -jax.nn.softplus(-(w0 + jnp.dot(jnp.tanh(lz), w2, preferred_element_type=F32))) - 0.5
        logw = -jnp.exp(w_log)
        a = jax.nn.sigmoid(a0 + jnp.dot(lz, a2, preferred_element_type=F32))
        g = jnp.dot(jax.nn.sigmoid(lz), g2, preferred_element_type=F32)
        if has_vres:
            v = v + (vf - v) * jax.nn.sigmoid(v0 + jnp.dot(lz, v2, preferred_element_type=F32))
        xk = k * kkw
        kk = xk / jnp.maximum(jnp.sqrt(_headsum(xk * xk, bb)), 1e-12)
        k2 = k * (1.0 + (a - 1.0) * ka)
        if v_uses:
            return (r, r, logw, k2, k2) + (v,) * v_uses + (kk, kk * a, g)
        return logw, k2, v, kk, kk * a, g

    return fn


def _make_f_rwkv_post(head_size):
    def fn(ctx, y, r, k2, v2, g, lnw, lnb, rk, bb):
        mean = _headsum(y, bb) / head_size
        d = y - mean
        var = _headsum(d * d, bb) / head_size
        yn = d * lax.rsqrt(var + LNX_EPS) * lnw + lnb
        bonus = _headsum(r * k2 * rk, bb) * v2
        return ((yn + bonus) * g,)

    return fn


def _f_ple(ctx, h, eg, ep, g):
    return (h + _rms(jax.nn.sigmoid(eg) * ep, g),)


def _lru_scan(a, b, *, name):
    rows, cols = a.shape
    tc = _tile(cols, (512, 256, 128))

    def body(a_ref, b_ref, h_ref):
        def step(t, carry):
            h = a_ref[pl.ds(t, 1), :] * carry + b_ref[pl.ds(t, 1), :]
            h_ref[pl.ds(t, 1), :] = h
            return h

        lax.fori_loop(0, rows, step, jnp.zeros((1, tc), F32), unroll=8)

    spec = pl.BlockSpec((rows, tc), lambda j: (0, j))
    return pl.pallas_call(body, name=name, grid=(cols // tc,), in_specs=[spec, spec], out_specs=spec,
                          out_shape=jax.ShapeDtypeStruct((rows, cols), F32), compiler_params=_cparams(("arbitrary",)))(a, b)


def _lru_scan_bwd(a, h, dh, *, name):
    rows, cols = a.shape
    tc = _tile(cols, (512, 256, 128))

    def body(a_ref, h_ref, dh_ref, da_ref, db_ref):
        def step(i, carry):
            t = rows - 1 - i
            g = dh_ref[pl.ds(t, 1), :] + carry
            db_ref[pl.ds(t, 1), :] = g
            hp = h_ref[pl.ds(jnp.maximum(t - 1, 0), 1), :]
            da_ref[pl.ds(t, 1), :] = jnp.where(t > 0, g * hp, 0.0)
            return a_ref[pl.ds(t, 1), :] * g

        lax.fori_loop(0, rows, step, jnp.zeros((1, tc), F32), unroll=8)

    spec = pl.BlockSpec((rows, tc), lambda j: (0, j))
    return pl.pallas_call(body, name=name, grid=(cols // tc,), in_specs=[spec] * 3, out_specs=[spec] * 2,
                          out_shape=[jax.ShapeDtypeStruct((rows, cols), F32)] * 2,
                          compiler_params=_cparams(("arbitrary",)))(a, h, dh)


def _split_bf16(x):
    hi = x.astype(BF16)
    return hi, (x - hi.astype(F32)).astype(BF16)


def _dot3_passes(a, b, ca, cb):
    dn = (((ca,), (cb,)), ((), ()))
    ah, al = _split_bf16(a)
    bh, bl = _split_bf16(b)
    return (lax.dot_general(ah, bh, dn, preferred_element_type=F32) + lax.dot_general(al, bh, dn, preferred_element_type=F32)
            + lax.dot_general(ah, bl, dn, preferred_element_type=F32))


@functools.partial(jax.custom_vjp, nondiff_argnums=(2, 3))
def _dot3(a, b, ca, cb):
    return _dot3_passes(a, b, ca, cb)


def _dot3_fwd(a, b, ca, cb):
    return _dot3_passes(a, b, ca, cb), (a, b)


def _dot3_bwd(ca, cb, res, g):
    a, b = res
    fa, fb = 1 - ca, 1 - cb
    da = _dot3_passes(g, b, 1, fb) if ca == 1 else _dot3_passes(b, g, fb, 1)
    db = _dot3_passes(a, g, fa, 0) if cb == 0 else _dot3_passes(g, a, 0, fa)
    return da, db


_dot3.defvjp(_dot3_fwd, _dot3_bwd)


def _each(f, *lists):
    return [f(*t) for t in zip(*lists)]


def _wkv_local(r, lw, k, v, kk, b):
    c, n = r[0].shape
    row = lax.broadcasted_iota(jnp.int32, (c, c), 0)
    col = lax.broadcasted_iota(jnp.int32, (c, c), 1)
    incl = (row >= col).astype(F32)
    strict = (row > col).astype(F32)
    eye = lax.broadcasted_iota(jnp.int32, (n, n), 0) == lax.broadcasted_iota(jnp.int32, (n, n), 1)
    cl = _each(lambda x: _dot3(incl, x, 1, 0), lw)
    w_t = _each(jnp.exp, cl)
    inv_w = _each(lambda x: jnp.exp(-x), cl)
    kk_s = _each(lambda x, y, z: x * jnp.exp(y - z), kk, cl, lw)
    b_s = _each(jnp.multiply, b, inv_w)
    k_s = _each(jnp.multiply, k, inv_w)
    r_s = _each(jnp.multiply, r, w_t)
    q = _each(lambda x, y: jnp.concatenate([x, y], axis=0), kk_s, r_s)
    qb = _each(lambda x, y: _dot3(x, y, 1, 1), q, b_s)
    qk = _each(lambda x, y: _dot3(x, y, 1, 1), q, k_s)
    m = _each(lambda x: -strict * x[:c], qb)
    pb = _each(lambda x: incl * x[c:], qb)
    lkv = _each(lambda x, y: _dot3(strict * x[:c], y, 1, 0), qk, v)
    pkv = _each(lambda x, y: _dot3(incl * x[c:], y, 1, 0), qk, v)
    a = _each(lambda x, y: jnp.concatenate([x, y], axis=1), kk_s, lkv)
    steps = max(1, (c - 1).bit_length())
    for i in range(steps):
        a = _each(lambda x, y: y + _dot3(x, y, 1, 0), m, a)
        if i + 1 < steps:
            m = _each(lambda x: _dot3(x, x, 1, 0), m)
    ry = _each(lambda x, y, z, w: jnp.concatenate([x, y], axis=1) - _dot3(z, w, 1, 0), r_s, pkv, pb, a)
    w_end = _each(lambda x: x[c - 1:c, :], w_t)
    gu_low = _each(lambda x, y, z: _dot3(x, y * z, 0, 0), a, b_s, w_end)
    g = _each(lambda x, y: jnp.where(eye, jnp.broadcast_to(x, (n, n)), 0.0) - y[:n], w_end, gu_low)
    u = _each(lambda x, y, z, w: _dot3(x, y * z, 0, 0) - w[n:], v, k_s, w_end, gu_low)
    return g, u, _each(lambda x: x[:, :n], ry), _each(lambda x: x[:, n:], ry)


def _wkv_blocks(h, nchunk):
    return (_tile(h, (4, 2, 1)), _tile(nchunk, (4, 2, 1))), (h, _tile(nchunk, (4, 2, 1)))


def _wkv_fwd(r, lw, k, v, kk, b, *, name, gather=(), gather_state=()):
    h, t, n = r.shape
    c = WKV_CHUNK
    nchunk = t // c
    (hb, _), (hs, cs) = _wkv_blocks(h, nchunk)
    cb = _tile(nchunk, (8, 4, 2, 1))
    ng = len(gather)
    ni, nj = h // hb, nchunk // cb

    pairs = [(i, j) for i in range(hb) for j in range(cb)]

    def local_body(*refs):
        ins, g_in = refs[:6], refs[6:6 + ng]
        g_ref, u_ref, r2_ref, y0_ref = refs[6 + ng:10 + ng]
        g_out, g_sems = refs[10 + ng:10 + 2 * ng], refs[10 + 2 * ng:]
        if ng:
            @pl.when((pl.program_id(0) == 0) & (pl.program_id(1) == 0))
            def _():
                _jobs_start(gather, g_in, g_out, g_sems)

        g, u, r2, y0 = _wkv_local(*[[ref[i, pl.ds(j * c, c)] for i, j in pairs] for ref in ins])
        for idx, (i, j) in enumerate(pairs):
            g_ref[i, j] = g[idx]
            u_ref[i, j] = u[idx]
            r2_ref[i, pl.ds(j * c, c)] = r2[idx]
            y0_ref[i, pl.ds(j * c, c)] = y0[idx]
        if ng:
            @pl.when((pl.program_id(0) == ni - 1) & (pl.program_id(1) == nj - 1))
            def _():
                _jobs_finish(gather, g_in, g_out, g_sems)

    seq = pl.BlockSpec((hb, cb * c, n), lambda i, j: (i, j, 0))
    mat = pl.BlockSpec((hb, cb, n, n), lambda i, j: (i, j, 0, 0))
    res = pl.pallas_call(
        local_body, name=name + "_local", grid=(ni, nj), in_specs=[seq] * 6 + [_ANY] * ng,
        out_specs=[mat, mat, seq, seq] + [_ANY] * ng,
        out_shape=[jax.ShapeDtypeStruct((h, nchunk, n, n), F32)] * 2 + [jax.ShapeDtypeStruct((h, t, n), F32)] * 2
        + [_job_out_shape(g) for g in gather],
        scratch_shapes=_jobs_scratch(gather),
        compiler_params=_cparams(("arbitrary", "arbitrary") if ng else ("parallel", "parallel")),
    )(r, lw, k, v, kk, b, *_job_arrays(gather))
    gm, um, r2, y0 = res[:4]
    gathered = list(res[4:])

    ng2 = len(gather_state)
    nsteps = nchunk // cs

    def state_body(*refs):
        g_ref, u_ref, r2_ref, y0_ref = refs[:4]
        g_in, (y_ref, st_ref) = refs[4:4 + ng2], refs[4 + ng2:6 + ng2]
        g_out, s_ref, g_sems = refs[6 + ng2:6 + 2 * ng2], refs[6 + 2 * ng2], refs[7 + 2 * ng2:]

        @pl.when(pl.program_id(0) == 0)
        def _():
            s_ref[...] = jnp.zeros_like(s_ref)
            _jobs_start(gather_state, g_in, g_out, g_sems)

        s = [s_ref[i] for i in range(hs)]
        for j in range(cs):
            rows = pl.ds(j * c, c)
            for i in range(hs):
                st_ref[i, j] = s[i]
                y_ref[i, rows] = _dot3(r2_ref[i, rows], s[i], 1, 1) + y0_ref[i, rows]
            s = [_dot3(s[i], g_ref[i, j], 1, 0) + u_ref[i, j] for i in range(hs)]
        for i in range(hs):
            s_ref[i] = s[i]
        if ng2:
            @pl.when(pl.program_id(0) == nsteps - 1)
            def _():
                _jobs_finish(gather_state, g_in, g_out, g_sems)

    seq = pl.BlockSpec((hs, cs * c, n), lambda j: (0, j, 0))
    mat = pl.BlockSpec((hs, cs, n, n), lambda j: (0, j, 0, 0))
    res = pl.pallas_call(
        state_body, name=name + "_state", grid=(nsteps,), in_specs=[mat, mat, seq, seq] + [_ANY] * ng2,
        out_specs=[seq, mat] + [_ANY] * ng2,
        out_shape=[jax.ShapeDtypeStruct((h, t, n), F32), jax.ShapeDtypeStruct((h, nchunk, n, n), F32)]
        + [_job_out_shape(g) for g in gather_state],
        scratch_shapes=[pltpu.VMEM((hs, n, n), F32)] + _jobs_scratch(gather_state), compiler_params=_cparams(("arbitrary",)),
    )(gm, um, r2, y0, *_job_arrays(gather_state))
    return res[0], (res[1], gm, r2), gathered + list(res[2:])


def _wkv_bwd(r, lw, k, v, kk, b, saved, dy, *, name, jobs_state=(), jobs_local=lambda state_results: ()):
    states, gm, r2 = saved
    h, t, n = r.shape
    c = WKV_CHUNK
    nchunk = t // c
    (hb, _), (hs, cs) = _wkv_blocks(h, nchunk)
    cb = _tile(nchunk, (8, 4, 2, 1))
    nsteps = nchunk // cs

    ns_ = len(jobs_state)

    def state_body(*refs):
        g_ref, r2_ref, st_ref, dy_ref = refs[:4]
        s_in, (dg_ref, du_ref, dr2_ref) = refs[4:4 + ns_], refs[4 + ns_:7 + ns_]
        s_out, ds_ref, s_sems = refs[7 + ns_:7 + 2 * ns_], refs[7 + 2 * ns_], refs[8 + 2 * ns_:]

        @pl.when(pl.program_id(0) == 0)
        def _():
            ds_ref[...] = jnp.zeros_like(ds_ref)
            _jobs_start(jobs_state, s_in, s_out, s_sems)

        ds = [ds_ref[i] for i in range(hs)]
        for j in reversed(range(cs)):
            rows = pl.ds(j * c, c)
            for i in range(hs):
                s0 = st_ref[i, j]
                du_ref[i, j] = ds[i]
                dg_ref[i, j] = _dot3(s0, ds[i], 0, 0)
                dr2_ref[i, rows] = _dot3(dy_ref[i, rows], s0, 1, 0)
            ds = [_dot3(dy_ref[i, rows], r2_ref[i, rows], 0, 0) + _dot3(ds[i], g_ref[i, j], 1, 1) for i in range(hs)]
        for i in range(hs):
            ds_ref[i] = ds[i]
        if ns_:
            @pl.when(pl.program_id(0) == nsteps - 1)
            def _():
                _jobs_finish(jobs_state, s_in, s_out, s_sems)

    seq = pl.BlockSpec((hs, cs * c, n), lambda j: (0, nsteps - 1 - j, 0))
    mat = pl.BlockSpec((hs, cs, n, n), lambda j: (0, nsteps - 1 - j, 0, 0))
    res = pl.pallas_call(
        state_body, name=name + "_state", grid=(nsteps,), in_specs=[mat, seq, mat, seq] + [_ANY] * ns_,
        out_specs=[mat, mat, seq] + [_ANY] * ns_,
        out_shape=[jax.ShapeDtypeStruct((h, nchunk, n, n), F32)] * 2 + [jax.ShapeDtypeStruct((h, t, n), F32)]
        + [_job_out_shape(j) for j in jobs_state],
        scratch_shapes=[pltpu.VMEM((hs, n, n), F32)] + _jobs_scratch(jobs_state), compiler_params=_cparams(("arbitrary",)),
    )(gm, r2, states, dy, *_job_arrays(jobs_state))
    dg, du, dr2 = res[:3]
    jobs = list(jobs_local(list(res[3:])))

    pairs = [(i, j) for i in range(hb) for j in range(cb)]
    nj_ = len(jobs)
    ni, nj = h // hb, nchunk // cb

    def local_body(*refs):
        ins, (dg_ref, du_ref, dr2_ref, dy_ref) = refs[:6], refs[6:10]
        j_in, out_refs, j_out, j_sems = refs[10:10 + nj_], refs[10 + nj_:16 + nj_], refs[16 + nj_:16 + 2 * nj_], refs[16 + 2 * nj_:]
        if nj_:
            @pl.when((pl.program_id(0) == 0) & (pl.program_id(1) == 0))
            def _():
                _jobs_start(jobs, j_in, j_out, j_sems)

        _, vjp = jax.vjp(_wkv_local, *[[ref[i, pl.ds(j * c, c)] for i, j in pairs] for ref in ins])
        grads = vjp(([dg_ref[i, j] for i, j in pairs], [du_ref[i, j] for i, j in pairs],
                     [dr2_ref[i, pl.ds(j * c, c)] for i, j in pairs], [dy_ref[i, pl.ds(j * c, c)] for i, j in pairs]))
        for o_ref, gr in zip(out_refs, grads):
            for idx, (i, j) in enumerate(pairs):
                o_ref[i, pl.ds(j * c, c)] = gr[idx]
        if nj_:
            @pl.when((pl.program_id(0) == ni - 1) & (pl.program_id(1) == nj - 1))
            def _():
                _jobs_finish(jobs, j_in, j_out, j_sems)

    seq = pl.BlockSpec((hb, cb * c, n), lambda i, j: (i, j, 0))
    mat = pl.BlockSpec((hb, cb, n, n), lambda i, j: (i, j, 0, 0))
    res = pl.pallas_call(
        local_body, name=name + "_local", grid=(ni, nj), in_specs=[seq] * 6 + [mat, mat, seq, seq] + [_ANY] * nj_,
        out_specs=[seq] * 6 + [_ANY] * nj_,
        out_shape=[jax.ShapeDtypeStruct((h, t, n), F32)] * 6 + [_job_out_shape(j) for j in jobs],
        scratch_shapes=_jobs_scratch(jobs),
        compiler_params=_cparams(("arbitrary", "arbitrary") if nj_ else ("parallel", "parallel")),
    )(r, lw, k, v, kk, b, dg, du, dr2, dy, *_job_arrays(jobs))
    return list(res[:6]), list(res[6:])


class _Dims:
    pass


def _make_dims(x, p, w):
    m = _Dims()
    m.t, m.d = x.shape[-2], x.shape[-1]
    m.nl = w["ln_mix"].shape[0]
    m.dl = w["conv_a_b"].shape[1]
    m.hl = w["lru_wx"].shape[1]
    m.dr = w["rwkv_w0"].shape[1]
    m.h, m.n = w["rwkv_rk"].shape[1], w["rwkv_rk"].shape[2]
    m.lw, m.la, m.lg, m.lv = (w[k].shape[1] for k in ("rwkv_w2", "rwkv_a2", "rwkv_g2", "rwkv_v2"))
    m.nsh = w["mu_shift"].shape[1]
    m.ff = w["conv_f_b"].shape[1]
    m.ple = p.shape[-1]
    m.din = 2 * m.dl + m.nsh
    m.lz = _round_up(m.lw + m.la + m.lg + m.lv, LANES_V7X)
    m.zw = _round_up(2 * m.dl + 3 * m.dr + m.lz, 512)
    m.zs = m.zw - 2 * m.dl
    m.tr = _tile(m.t, (256, 128, 64, 32, 16, 8))
    m.trb = _tile(m.t, (128, 64, 32, 16, 8))
    m.tcs = _tile(m.zs, (512, 256, 128))
    m.dw_shards = 0
    assert (3 * m.dr) % m.lz == 0 and (2 * m.dl) % m.tcs == 0 and m.t % WKV_CHUNK == 0
    assert m.nsh == 3 * m.dr + m.lw + m.la + m.lg
    return m


def _to_heads(m, a):
    return jnp.transpose(a.reshape(m.t, m.h, m.n), (1, 0, 2))


def _from_heads(m, a):
    return jnp.transpose(a, (1, 0, 2)).reshape(m.t, m.dr)


def _norm_fwd(m, h, g, name):
    return _stage_fwd(_f_norm, [(h, m.d, 0)], [(g, 0)], [], [(m.d, BF16)], axis=0, tile=m.tr, rows=m.t, name=name)[0]


def _norm_bwd(m, h, g, du, dres, name):
    return _stage_bwd(_f_norm_res, [(h, m.d, 0)], [(g, 0)], [], [(du, m.d, 0), (dres, m.d, 0)], [(0, m.d, F32)],
                      axis=0, tile=m.tr, rows=m.t, name=name)


def _rwkv_pre_operands(m, w, i, sv, v_first_zs, with_r):
    zs = sv["zs"]
    tiled = ([(zs, m.dr, 0)] if with_r else []) + [(zs, m.dr, 1), (zs, m.dr, 2), (zs, m.lz, 3 * m.dr // m.lz)]
    params = [(w["rwkv_w0"][i:i + 1], 0), (w["w2p"][i], 0), (w["rwkv_a0"][i:i + 1], 0), (w["a2p"][i], 0),
              (w["g2p"][i], 0), (w["rwkv_kk"][i:i + 1], 0), (w["rwkv_ka"][i:i + 1], 0)]
    if i > 0:
        tiled.append((v_first_zs, m.dr, 2))
        params += [(w["rwkv_v0"][i - 1:i], 0), (w["v2p"][i - 1], 0)]
    return tiled, params


def _rwkv_post_operands(m, w, i, sv):
    tiled = [(sv["y"], m.dr, 0), (sv["zs"], m.dr, 0), (sv["k2"], m.dr, 0), (sv["v2"], m.dr, 0), (sv["g"], m.dr, 0)]
    params = [(w["rwkv_lnx_w"][i:i + 1], 0), (w["rwkv_lnx_b"][i:i + 1], 0), (w["rk"][i], 0)]
    return tiled, params


def _lru_gate_params(w, i):
    return [(w["lru_wx"][i], 0), (w["lru_wa"][i], 0), (w["lru_bx"][i:i + 1], 0), (w["lru_ba"][i:i + 1], 0),
            (w["lru_lambda"][i:i + 1], 0)]


_COLUMN_SHARDED_OPERANDS = ("w_gate", "w_up", "w_ple_proj")


class _WeightFeed:
    def __init__(self, m, w, shards, vres):
        self.m, self.w, self.shards, self.vres = m, w, shards, vres

    def keys(self, carrier, i):
        plan = {"mm_in": [("w_o", i)] if i == 0 else [],
                "wkv_local": [("w_gate", i), ("w_up", i)], "wkv_state": [("w_down", i)],
                "mm_gate": [("w_ple_gate", i)], "mm_up": [("w_ple_proj", i), ("w_o", i + 1)],
                "mm_down": [("w_in", i + 1)], "mm_pgate": []}
        return [key for key in plan[carrier] if key[1] < self.m.nl]

    def blobs(self, keys):
        return [_gather_blob(self.shards[name][layer]) for name, layer in keys]

    def arrive(self, keys, gathered):
        m = self.m
        for (name, layer), got in zip(keys, gathered):
            full = got.reshape((N_XY,) + self.shards[name][layer].shape)
            if name in _COLUMN_SHARDED_OPERANDS:
                self.w[name][layer] = full
                continue
            full = _from_shards(full, _SHARD_AXIS[name] - 1)
            if name == "w_in":
                vres = self.vres[layer - 1] if layer > 0 else jnp.zeros((m.d, m.lv), BF16)
                self.w["wcat"][layer] = jnp.concatenate([full, vres, jnp.zeros((m.d, m.zw - m.din - m.lv), BF16)], axis=1)
            else:
                self.w[name][layer] = full


def _mm_fed(feed, carrier, i, a, b, **kw):
    keys = feed.keys(carrier, i) if feed is not None else []
    if not keys:
        return _mm(a, b, **kw)
    out, got = _mm(a, b, gather=feed.blobs(keys), **kw)
    feed.arrive(keys, got)
    return out


def _layer_fwd(m, w, i, h, p_bf, v_first_zs, feed=None):
    sv = {"h": h}
    t, dl, dr = m.t, m.dl, m.dr
    sv["u1"] = _norm_fwd(m, h, w["ln_mix"][i:i + 1], "norm_mix")
    z = sv["z"] = _mm_fed(feed, "mm_in", i, sv["u1"], w["wcat"][i], name="mm_in")
    off = 2 * dl // m.tcs
    sv["zs"] = _stage_fwd(_f_shiftmix, [(z, None, off)], [(w["mu_pad"][i], off)], [], [(m.zs, F32)],
                          axis=1, tile=m.tcs, rows=t, name="shiftmix")[0]
    tca = _tile(dl, (512, 256, 128))
    sv["xb"] = _stage_fwd(_f_conv, [(z, None, 0)], [(w["conv_a_w"][i], 0), (w["conv_a_b"][i:i + 1], 0)], [],
                          [(dl, F32)], axis=1, tile=tca, rows=t, name="conv_a")[0]
    sv["a"], b_in = _stage_fwd(_make_f_lru_gates(m.hl), [(sv["xb"], dl, 0)], _lru_gate_params(w, i), [],
                               [(dl, F32), (dl, F32)], axis=0, tile=m.tr, rows=t, name="lru_gates")
    sv["hl"] = _lru_scan(sv["a"], b_in, name="lru_scan")
    out_a = _stage_fwd(_f_lru_out, [(sv["hl"], dl, 0), (z, dl, 1)], [(w["lru_norm"][i:i + 1], 0)], [],
                       [(dl, BF16)], axis=0, tile=m.tr, rows=t, name="lru_out")[0]
    tiled, params = _rwkv_pre_operands(m, w, i, sv, v_first_zs, False)
    pre = _stage_fwd(_make_f_rwkv_pre(i > 0), tiled, params, [w["bb"]], [(dr, F32)] * 6,
                     axis=0, tile=m.tr, rows=t, name="rwkv_pre")
    sv["logw"], sv["k2"], sv["v2"], sv["kk"], sv["b"], sv["g"] = pre
    heads = [_to_heads(m, a) for a in (sv["zs"][:, :dr], sv["logw"], sv["k2"], sv["v2"], sv["kk"], sv["b"])]
    keys = [feed.keys(carrier, i) if feed is not None else [] for carrier in ("wkv_local", "wkv_state")]
    y_h, sv["states"], got = _wkv_fwd(*heads, name="wkv_fwd", gather=feed.blobs(keys[0]) if keys[0] else (),
                                      gather_state=feed.blobs(keys[1]) if keys[1] else ())
    if keys[0] or keys[1]:
        feed.arrive(keys[0] + keys[1], got)
    sv["y"] = _from_heads(m, y_h)
    tiled, params = _rwkv_post_operands(m, w, i, sv)
    out_b = _stage_fwd(_make_f_rwkv_post(m.n), tiled, params, [w["bb"]], [(dr, BF16)],
                       axis=0, tile=m.tr, rows=t, name="rwkv_post")[0]
    sv["cat"] = jnp.concatenate([out_a, out_b], axis=1)
    h2 = sv["h2"] = _mm(sv["cat"], w["w_o"][i], res=h, name="mm_o")
    sv["u2"] = _norm_fwd(m, h2, w["ln_ffn"][i:i + 1], "norm_ffn")
    sv["gpre"] = _mm_fed(feed, "mm_gate", i, sv["u2"], w["w_gate"][i], name="mm_gate")
    sv["up"] = _mm_fed(feed, "mm_up", i, sv["u2"], w["w_up"][i], name="mm_up")
    tcf = _tile(m.ff, (512, 256, 128))
    sv["act"] = _stage_fwd(_f_ffn_act, [(sv["gpre"], None, 0), (sv["up"], None, 0)],
                           [(w["conv_f_w"][i], 0), (w["conv_f_b"][i:i + 1], 0)], [], [(m.ff, BF16)],
                           axis=1, tile=tcf, rows=t, name="ffn_act")[0]
    h3 = sv["h3"] = _mm_fed(feed, "mm_down", i, sv["act"], w["w_down"][i], res=h2, name="mm_down")
    sv["u3"] = _norm_fwd(m, h3, w["ln_ple"][i:i + 1], "norm_ple")
    sv["eg"] = _mm_fed(feed, "mm_pgate", i, sv["u3"], w["w_ple_gate"][i], name="mm_pgate")
    sv["ep"] = _mm(p_bf, w["w_ple_proj"][i], name="mm_pproj")
    h4 = _stage_fwd(_f_ple, [(h3, m.d, 0), (sv["eg"], m.d, 0), (sv["ep"], m.d, 0)], [(w["ln_ple_post"][i:i + 1], 0)],
                    [], [(m.d, F32)], axis=0, tile=m.tr, rows=t, name="ple")[0]
    return h4, sv


def _layer_bwd(m, w, i, dh4, sv, p_bf, v_first_zs, dvf_in, pending=None, early=None):
    t, d, dl, dr = m.t, m.d, m.dl, m.dr
    g = {}
    deg, dep, g["ln_ple_post"] = _stage_bwd(
        _f_ple, [(sv["h3"], d, 0), (sv["eg"], d, 0), (sv["ep"], d, 0)], [(w["ln_ple_post"][i:i + 1], 0)], [],
        [(dh4, d, 0)], [(1, d, BF16), (2, d, BF16)], axis=0, tile=m.tr, rows=t, name="ple_bwd")
    du3 = _mm(deg, w["w_ple_gate"][i], tb=True, name="mm_pgate_dx")
    g["w_ple_gate"] = _mm(sv["u3"], deg, ta=True, name="mm_pgate_dw")
    g["w_ple_proj"] = _mm(p_bf, dep, ta=True, name="mm_pproj_dw", out_shards=m.dw_shards)
    dh3, g["ln_ple"] = _norm_bwd(m, sv["h3"], w["ln_ple"][i:i + 1], du3, dh4, "norm_ple_bwd")
    dh3_bf = dh3.astype(BF16)
    dact = _mm(dh3_bf, w["w_down"][i], tb=True, name="mm_down_dx")
    if pending is None:
        g["w_down"] = _mm(sv["act"], dh3_bf, ta=True, name="mm_down_dw")
    else:
        g["w_down"], gots = _mm(sv["act"], dh3_bf, ta=True, name="mm_down_dw", gather=pending.send_jobs())
        pending.after_send(gots)
    tcf = _tile(m.ff, (512, 256, 128))
    dgpre, dup, g["conv_f_w"], g["conv_f_b"] = _stage_bwd(
        _f_ffn_act, [(sv["gpre"], None, 0), (sv["up"], None, 0)], [(w["conv_f_w"][i], 0), (w["conv_f_b"][i:i + 1], 0)],
        [], [(dact, 0)], [(0, m.ff, BF16), (1, m.ff, BF16)], axis=1, tile=tcf, rows=t, ncols=m.ff, name="ffn_act_bwd")
    du2 = _mm(dgpre, w["w_gate"][i], tb=True, name="mm_gate_dx")
    du2 = _mm(dup, w["w_up"][i], tb=True, res=du2, name="mm_up_dx")
    g["w_gate"] = _mm(sv["u2"], dgpre, ta=True, name="mm_gate_dw", out_shards=m.dw_shards)
    g["w_up"] = _mm(sv["u2"], dup, ta=True, name="mm_up_dw", out_shards=m.dw_shards)
    dh2, g["ln_ffn"] = _norm_bwd(m, sv["h2"], w["ln_ffn"][i:i + 1], du2, dh3, "norm_ffn_bwd")
    dh2_bf = dh2.astype(BF16)
    dcat = _mm(dh2_bf, w["w_o"][i], tb=True, name="mm_o_dx")
    g["w_o"] = _mm(sv["cat"], dh2_bf, ta=True, name="mm_o_dw")
    tiled, params = _rwkv_post_operands(m, w, i, sv)
    dy, dr_a, dk2_a, dv2_a, dg, g["rwkv_lnx_w"], g["rwkv_lnx_b"], g["rk"] = _stage_bwd(
        _make_f_rwkv_post(m.n), tiled, params, [w["bb"]], [(dcat, dr, dl // dr)], [(j, dr, F32) for j in range(5)],
        axis=0, tile=m.trb, rows=t, name="rwkv_post_bwd")
    heads = [_to_heads(m, a) for a in (sv["zs"][:, :dr], sv["logw"], sv["k2"], sv["v2"], sv["kk"], sv["b"])]
    own = early(g) if early is not None else None

    def local_jobs(state_results):
        jobs = []
        if own is not None:
            own.after_send(state_results)
            jobs += own.exchange_jobs()
        if pending is not None:
            jobs += pending.exchange_jobs()
        return jobs

    dwkv, parts = _wkv_bwd(*heads, sv["states"], _to_heads(m, dy), name="wkv_bwd",
                           jobs_state=own.send_jobs() if own is not None else (), jobs_local=local_jobs)
    n_own = len(own.names) if own is not None else 0
    if own is not None:
        own.after_exchange(parts[:n_own])
    if pending is not None:
        pending.after_exchange(parts[n_own:])
    dr_b, dlw, dk2_b, dv2_b, dkk, db = [_from_heads(m, a) for a in dwkv]
    tiled, params = _rwkv_pre_operands(m, w, i, sv, v_first_zs, True)
    v_cots = [dv2_a, dv2_b] + ([dvf_in] if dvf_in is not None else [])
    cots = [(c, dr, 0) for c in [dr_a, dr_b, dlw, dk2_a, dk2_b] + v_cots + [dkk, db, dg]]
    ntil = len(tiled)
    dtiled = [(0, dr, F32), (1, dr, F32), (2, dr, F32), (3, m.lz, F32)] + ([(4, dr, F32)] if i > 0 else [])
    res = _stage_bwd(_make_f_rwkv_pre(i > 0, len(v_cots)), tiled, params, [w["bb"]], cots, dtiled,
                     axis=0, tile=m.trb, rows=t, name="rwkv_pre_bwd")
    d_r, d_k, d_v, d_lz = res[:4]
    dvf_out = res[4] if i > 0 else None
    pg = res[ntil:]
    g["rwkv_w0"], g["w2p"], g["rwkv_a0"], g["a2p"], g["g2p"], g["rwkv_kk"], g["rwkv_ka"] = pg[:7]
    if i > 0:
        g["rwkv_v0"], g["v2p"] = pg[7:9]
    dzs = jnp.concatenate([d_r, d_k, d_v, d_lz, jnp.zeros((t, m.zs - 3 * dr - m.lz), F32)], axis=1)
    off = 2 * dl // m.tcs
    dzr, g["mu_pad"] = _stage_bwd(_f_shiftmix, [(sv["z"], None, off)], [(w["mu_pad"][i], off)], [], [(dzs, 0)],
                                  [(0, m.zs, BF16)], axis=1, tile=m.tcs, rows=t, ncols=m.zs, name="shiftmix_bwd")
    dhl, dya, g["lru_norm"] = _stage_bwd(
        _f_lru_out, [(sv["hl"], dl, 0), (sv["z"], dl, 1)], [(w["lru_norm"][i:i + 1], 0)], [], [(dcat, dl, 0)],
        [(0, dl, F32), (1, dl, BF16)], axis=0, tile=m.tr, rows=t, name="lru_out_bwd")
    da, db_in = _lru_scan_bwd(sv["a"], sv["hl"], dhl, name="lru_scan_bwd")
    dxb, g["lru_wx"], g["lru_wa"], g["lru_bx"], g["lru_ba"], g["lru_lambda"] = _stage_bwd(
        _make_f_lru_gates(m.hl), [(sv["xb"], dl, 0)], _lru_gate_params(w, i), [], [(da, dl, 0), (db_in, dl, 0)],
        [(0, dl, F32)], axis=0, tile=m.tr, rows=t, name="lru_gates_bwd")
    tca = _tile(dl, (512, 256, 128))
    dxa, g["conv_a_w"], g["conv_a_b"] = _stage_bwd(
        _f_conv, [(sv["z"], None, 0)], [(w["conv_a_w"][i], 0), (w["conv_a_b"][i:i + 1], 0)], [], [(dxb, 0)],
        [(0, dl, BF16)], axis=1, tile=tca, rows=t, ncols=dl, name="conv_a_bwd")
    dz = jnp.concatenate([dxa, dya, dzr], axis=1)
    du1 = _mm(dz, w["wcat"][i], tb=True, name="mm_in_dx")
    g["wcat"] = _mm(sv["u1"], dz, ta=True, name="mm_in_dw")
    dh, g["ln_mix"] = _norm_bwd(m, sv["h"], w["ln_mix"][i:i + 1], du1, dh2, "norm_mix_bwd")
    return dh, g, dvf_out, own


def _loss_head(m, h, g, tgt):
    tile, d = m.tr, m.d

    def body(h_ref, g_ref, t_ref, loss_ref, dh_ref, dg_ref):
        def f(hv, gv):
            err = _rms(hv, gv) - t_ref[...]
            return 0.5 * jnp.sum(jnp.mean(err * err, axis=-1))

        val, vjp = jax.vjp(f, h_ref[...], g_ref[...])
        dh, dg = vjp(jnp.ones((), F32))
        dh_ref[...] = dh

        @pl.when(pl.program_id(0) == 0)
        def _():
            dg_ref[...] = jnp.zeros_like(dg_ref)
            loss_ref[...] = jnp.zeros_like(loss_ref)

        dg_ref[...] += dg
        loss_ref[...] += jnp.full(loss_ref.shape, val, F32)

    row = pl.BlockSpec((tile, d), lambda i: (i, 0))
    return pl.pallas_call(
        body, name="loss_head", grid=(m.t // tile,),
        in_specs=[row, pl.BlockSpec((1, d), lambda i: (0, 0)), row],
        out_specs=[pl.BlockSpec((1, LANES_V7X), lambda i: (0, 0)), row, pl.BlockSpec((1, d), lambda i: (0, 0))],
        out_shape=[jax.ShapeDtypeStruct((1, LANES_V7X), F32), jax.ShapeDtypeStruct((m.t, d), F32),
                   jax.ShapeDtypeStruct((1, d), F32)],
        compiler_params=_cparams(("arbitrary",)),
    )(h, g, tgt)


def _local_step(m, w, x, p, tgt, feed=None, reducer=None):
    h = x
    saved = []
    p_bf = p.astype(BF16)
    for i in range(m.nl):
        h, sv = _layer_fwd(m, w, i, h, p_bf[i], saved[0]["zs"] if i > 0 else None, feed)
        saved.append(sv)
    loss_row, dh, d_ln_final = _loss_head(m, h, w["ln_final"], tgt)
    grads = [None] * m.nl
    reductions = [None] * m.nl
    dvf = None
    for i in reversed(range(m.nl)):
        pending = reductions[i + 1][1] if reducer is not None and i + 1 < m.nl else None
        early = functools.partial(reducer, i, which="early") if reducer is not None else None
        dh, grads[i], dvf_i, own = _layer_bwd(m, w, i, dh, saved[i], p_bf[i], saved[0]["zs"] if i > 0 else None,
                                              dvf if i == 0 else None, pending, early)
        if reducer is not None:
            reductions[i] = (own, reducer(i, grads[i], which="late"))
        if i > 0:
            dvf = dvf_i if dvf is None else dvf + dvf_i
    return loss_row, dh, grads, d_ln_final, reductions


_BIG = ("w_o", "w_gate", "w_up", "w_down", "w_ple_gate", "w_ple_proj")


def _lora_rows(m):
    o1 = m.lw
    o2 = o1 + m.la
    o3 = o2 + m.lg
    return {"w2p": (0, o1), "a2p": (o1, o2), "g2p": (o2, o3), "v2p": (o3, o3 + m.lv)}


def _prepare_weights(m, wf):
    w = {k: v for k, v in wf.items() if k not in _BIG and k not in ("w_in", "w_in_vres")}
    nl = m.nl
    for k in _BIG:
        w[k] = [wf[k][i].astype(BF16) for i in range(nl)] if k in wf else [None] * nl
    w["wcat"] = [None] * nl
    if "w_in" in wf:
        vres = jnp.concatenate([jnp.zeros((1, m.d, m.lv), BF16), wf["w_in_vres"].astype(BF16)], axis=0)
        pad = jnp.zeros((m.d, m.zw - m.din - m.lv), BF16)
        w["wcat"] = [jnp.concatenate([wf["w_in"][i].astype(BF16), vres[i], pad], axis=1) for i in range(nl)]
    mu_v = jnp.concatenate([jnp.zeros((1, m.lv), F32), wf["mu_shift_vres"]], axis=0)
    w["mu_pad"] = jnp.concatenate([jnp.zeros((nl, 2 * m.dl), F32), wf["mu_shift"], mu_v,
                                   jnp.zeros((nl, m.zw - m.din - m.lv), F32)], axis=1)[:, None, :]
    rows = _lora_rows(m)
    for name, src in (("w2p", "rwkv_w2"), ("a2p", "rwkv_a2"), ("g2p", "rwkv_g2"), ("v2p", "rwkv_v2")):
        lo, hi = rows[name]
        a = wf[src]
        w[name] = jnp.concatenate([jnp.zeros((a.shape[0], lo, m.dr), F32), a, jnp.zeros((a.shape[0], m.lz - hi, m.dr), F32)],
                                  axis=1)
    w["rk"] = wf["rwkv_rk"].reshape(nl, 1, m.dr)
    w["ln_final"] = wf["ln_final"].reshape(1, m.d)
    group = max(m.n, LANES_V7X)
    assert group % m.n == 0 and m.dr % group == 0
    head = jnp.arange(group, dtype=jnp.int32) // m.n
    w["bb"] = (head[:, None] == head[None, :]).astype(BF16)
    return w


def _unpack_grads(m, grads, d_ln_final, with_big=True):
    nl = m.nl
    out = {}

    def stack(key):
        return jnp.stack([grads[i][key] for i in range(nl)], axis=0)

    for k in (_BIG if with_big else ()) + ("conv_a_w", "conv_f_w", "lru_wx", "lru_wa"):
        out[k] = stack(k)
    for k in ("ln_mix", "conv_a_b", "lru_bx", "lru_ba", "lru_lambda", "lru_norm", "rwkv_w0", "rwkv_a0", "rwkv_kk",
              "rwkv_ka", "rwkv_lnx_w", "rwkv_lnx_b", "ln_ffn", "conv_f_b", "ln_ple", "ln_ple_post"):
        out[k] = stack(k)[:, 0, :]
    if with_big:
        out["w_in"] = stack("wcat")[:, :, :m.din]
    out["w_in_vres"] = jnp.stack([grads[i]["wcat"][:, m.din:m.din + m.lv] for i in range(1, nl)], axis=0)
    mu = stack("mu_pad")[:, 0, :]
    out["mu_shift"] = mu[:, :m.nsh]
    out["mu_shift_vres"] = mu[1:, m.nsh:m.nsh + m.lv]
    rows = _lora_rows(m)
    for name, dst in (("w2p", "rwkv_w2"), ("a2p", "rwkv_a2"), ("g2p", "rwkv_g2")):
        lo, hi = rows[name]
        out[dst] = stack(name)[:, lo:hi, :]
    lo, hi = rows["v2p"]
    out["rwkv_v2"] = jnp.stack([grads[i]["v2p"] for i in range(1, nl)], axis=0)[:, lo:hi, :]
    out["rwkv_v0"] = jnp.stack([grads[i]["rwkv_v0"] for i in range(1, nl)], axis=0)[:, 0, :]
    out["rwkv_rk"] = stack("rk").reshape(nl, m.h, m.n)
    out["ln_final"] = d_ln_final.reshape(m.d)
    return out


_ANY = pl.BlockSpec(memory_space=pl.ANY)


def _position():
    return lax.axis_index("x"), lax.axis_index("y"), lax.axis_index("c")


def _other_chips(x, y):
    return [(1 - x, y), (x, 1 - y), (1 - x, 1 - y)]


def _gather_blob(shard):
    rows, wd = shard.shape
    return shard.reshape(2, rows // 2, wd)


_JOB_SEMS = {"gather": 7, "pair_send": 1, "exchange": 3}


def _job_parts(job):
    return job if isinstance(job, tuple) else ("gather", job)


def _job_arrays(jobs):
    return [_job_parts(j)[1] for j in jobs]


def _job_out_shape(job):
    kind, arr = _job_parts(job)
    if kind == "gather":
        return jax.ShapeDtypeStruct((N_XY,) + arr.shape, arr.dtype)
    if kind == "pair_send":
        return jax.ShapeDtypeStruct((arr.shape[0], arr.shape[1] // 2, arr.shape[2]), arr.dtype)
    return jax.ShapeDtypeStruct(arr.shape, arr.dtype)


def _jobs_scratch(jobs):
    out = []
    for j in jobs:
        n = _JOB_SEMS[_job_parts(j)[0]]
        out += [pltpu.SemaphoreType.DMA((n,)), pltpu.SemaphoreType.DMA((n,))]
    return out


def _jobs_start(jobs, in_refs, out_refs, sems):
    for q, j in enumerate(jobs):
        _JOB_START[_job_parts(j)[0]](in_refs[q], out_refs[q], sems[2 * q], sems[2 * q + 1])


def _jobs_finish(jobs, in_refs, out_refs, sems):
    for q, j in enumerate(jobs):
        _JOB_FINISH[_job_parts(j)[0]](in_refs[q], out_refs[q], sems[2 * q], sems[2 * q + 1])


def _run_jobs(jobs, *, name):
    n = len(jobs)

    def body(*refs):
        _jobs_start(jobs, refs[:n], refs[n:2 * n], refs[2 * n:])
        _jobs_finish(jobs, refs[:n], refs[n:2 * n], refs[2 * n:])

    return pl.pallas_call(body, name=name, in_specs=[_ANY] * n, out_specs=[_ANY] * n,
                          out_shape=[_job_out_shape(j) for j in jobs], scratch_shapes=_jobs_scratch(jobs))(*_job_arrays(jobs))


def _gather_copies(in_ref, out_ref, send_sems, recv_sems):
    x, y, c = _position()
    me = 2 * x + y
    sends, hands, ici_in, d2d_in = [], [], [], []
    for k, (px, py) in enumerate(_other_chips(x, y)):
        landed = out_ref.at[2 * px + py, c]
        sends.append(pltpu.make_async_remote_copy(
            src_ref=in_ref.at[c], dst_ref=out_ref.at[me, c], send_sem=send_sems.at[k], recv_sem=recv_sems.at[k],
            device_id=(px, py, c), device_id_type=MESH))
        ici_in.append(pltpu.make_async_remote_copy(
            src_ref=in_ref.at[c], dst_ref=landed, send_sem=send_sems.at[k], recv_sem=recv_sems.at[k],
            device_id=(px, py, c), device_id_type=MESH))
        hands.append(pltpu.make_async_remote_copy(
            src_ref=landed, dst_ref=landed, send_sem=send_sems.at[3 + k], recv_sem=recv_sems.at[3 + k],
            device_id=(x, y, 1 - c), device_id_type=MESH))
        d2d_in.append(pltpu.make_async_remote_copy(
            src_ref=in_ref.at[c], dst_ref=out_ref.at[2 * px + py, 1 - c], send_sem=send_sems.at[3 + k],
            recv_sem=recv_sems.at[3 + k], device_id=(x, y, 1 - c), device_id_type=MESH))
    own = pltpu.make_async_remote_copy(src_ref=in_ref, dst_ref=out_ref.at[me], send_sem=send_sems.at[6],
                                       recv_sem=recv_sems.at[6], device_id=(x, y, 1 - c), device_id_type=MESH)
    sends.append(own)
    d2d_in.append(own)
    return sends, hands, ici_in, d2d_in


def _gather_start(in_ref, out_ref, send_sems, recv_sems):
    for cp in _gather_copies(in_ref, out_ref, send_sems, recv_sems)[0]:
        cp.start()


def _gather_finish(in_ref, out_ref, send_sems, recv_sems):
    sends, hands, ici_in, d2d_in = _gather_copies(in_ref, out_ref, send_sems, recv_sems)
    for arrived, hand in zip(ici_in, hands):
        arrived.wait_recv()
        hand.start()
    for arrived in d2d_in:
        arrived.wait_recv()
    for cp in sends + hands:
        cp.wait_send()


def _pair_send_copy(g_ref, out_ref, send_sems, recv_sems):
    x, y, c = _position()
    half = out_ref.shape[1]
    return pltpu.make_async_remote_copy(src_ref=g_ref.at[:, pl.ds((1 - c) * half, half), :], dst_ref=out_ref,
                                        send_sem=send_sems.at[0], recv_sem=recv_sems.at[0], device_id=(x, y, 1 - c),
                                        device_id_type=MESH)


def _exchange_copies(in_ref, out_ref, send_sems, recv_sems):
    x, y, c = _position()
    me = 2 * x + y
    sends, arrivals = [], []
    for k, (px, py) in enumerate(_other_chips(x, y)):
        sends.append(pltpu.make_async_remote_copy(
            src_ref=in_ref.at[2 * px + py], dst_ref=out_ref.at[me], send_sem=send_sems.at[k], recv_sem=recv_sems.at[k],
            device_id=(px, py, c), device_id_type=MESH))
        arrivals.append(pltpu.make_async_remote_copy(
            src_ref=in_ref.at[me], dst_ref=out_ref.at[2 * px + py], send_sem=send_sems.at[k], recv_sem=recv_sems.at[k],
            device_id=(px, py, c), device_id_type=MESH))
    return sends, arrivals


def _exchange_start(*refs):
    for cp in _exchange_copies(*refs)[0]:
        cp.start()


def _exchange_finish(*refs):
    sends, arrivals = _exchange_copies(*refs)
    for cp in arrivals:
        cp.wait_recv()
    for cp in sends:
        cp.wait_send()


_JOB_START = {"gather": _gather_start, "pair_send": lambda *refs: _pair_send_copy(*refs).start(), "exchange": _exchange_start}
_JOB_FINISH = {"gather": _gather_finish, "pair_send": lambda *refs: _pair_send_copy(*refs).wait(),
               "exchange": _exchange_finish}


def _pair_sum(g, got, pos, *, name):
    nq, r, wd = g.shape
    half = r // 2
    tr = _tile(half, (256, 128, 64, 32, 16, 8))
    nb = half // tr

    def body(c_ref, g_ref, got_ref, o_ref):
        o_ref[...] = (g_ref[...] + got_ref[...]).astype(o_ref.dtype)

    grid_spec = pltpu.PrefetchScalarGridSpec(
        num_scalar_prefetch=1, grid=(nq, nb),
        in_specs=[pl.BlockSpec((1, tr, wd), lambda q, j, c_ref: (q, c_ref[0] * nb + j, 0)),
                  pl.BlockSpec((1, tr, wd), lambda q, j, c_ref: (q, j, 0))],
        out_specs=pl.BlockSpec((1, tr, wd), lambda q, j, c_ref: (q, j, 0)))
    return pl.pallas_call(body, name=name, grid_spec=grid_spec, out_shape=jax.ShapeDtypeStruct((nq, half, wd), BF16),
                          compiler_params=_cparams(("arbitrary", "arbitrary")))(pos[0], g, got)


def _chip_sum(parts, pb, pos, *, name):
    nq, half, wd = parts.shape
    tr = _tile(half, (256, 128, 64, 32, 16, 8))
    nb = half // tr

    def body(c_ref, x_ref, y_ref, p_ref, own_ref, o_ref):
        chip = 2 * x_ref[0] + y_ref[0]
        own = own_ref[0].astype(F32)
        acc = None
        for q in range(nq):
            term = jnp.where(chip == q, own, p_ref[q].astype(F32))
            acc = term if acc is None else acc + term
        o_ref[...] = acc

    grid_spec = pltpu.PrefetchScalarGridSpec(
        num_scalar_prefetch=3, grid=(nb,),
        in_specs=[pl.BlockSpec((nq, tr, wd), lambda j, c_ref, x_ref, y_ref: (0, j, 0)),
                  pl.BlockSpec((1, tr, wd), lambda j, c_ref, x_ref, y_ref: (2 * x_ref[0] + y_ref[0], j, 0))],
        out_specs=pl.BlockSpec((tr, wd), lambda j, c_ref, x_ref, y_ref: (c_ref[0] * nb + j, 0)))
    return pl.pallas_call(body, name=name, grid_spec=grid_spec, out_shape=jax.ShapeDtypeStruct((2 * half, wd), F32),
                          compiler_params=_cparams(("arbitrary",)))(*pos, parts, pb)


def _pair_gather(full, *, name):
    r, wd = full.shape
    half = r // 2

    def body(in_ref, out_ref, send_sem, recv_sem):
        x, y, c = _position()
        mine = out_ref.at[pl.ds(c * half, half), :]
        cp = pltpu.make_async_remote_copy(src_ref=mine, dst_ref=mine, send_sem=send_sem, recv_sem=recv_sem,
                                          device_id=(x, y, 1 - c), device_id_type=MESH)
        cp.start()
        pltpu.make_async_remote_copy(src_ref=mine, dst_ref=out_ref.at[pl.ds((1 - c) * half, half), :], send_sem=send_sem,
                                     recv_sem=recv_sem, device_id=(x, y, 1 - c), device_id_type=MESH).wait_recv()
        cp.wait_send()

    return pl.pallas_call(
        body, name=name, in_specs=[_ANY], out_specs=_ANY, out_shape=jax.ShapeDtypeStruct(full.shape, full.dtype),
        input_output_aliases={0: 0}, scratch_shapes=[pltpu.SemaphoreType.DMA(()), pltpu.SemaphoreType.DMA(())],
    )(full)


class _GradReduce:
    def __init__(self, slabs, pos, tag):
        self.names, self.slabs, self.pos, self.tag = list(slabs), [slabs[k] for k in slabs], pos, tag
        self.pb = self.out = None

    def send_jobs(self):
        return [("pair_send", g) for g in self.slabs]

    def after_send(self, gots):
        self.pb = [_pair_sum(g, got, self.pos, name=f"rs_pair_sum_{k}_{self.tag}")
                   for k, g, got in zip(self.names, self.slabs, gots)]

    def exchange_jobs(self):
        return [("exchange", pb) for pb in self.pb]

    def after_exchange(self, parts):
        full = [_chip_sum(pt, pb, self.pos, name=f"rs_chip_sum_{k}_{self.tag}") for k, pt, pb in zip(self.names, parts, self.pb)]
        self.out = {k: _pair_gather(f, name=f"rs_pair_gather_{k}_{self.tag}") for k, f in zip(self.names, full)}

    def run(self):
        if self.pb is None:
            self.after_send(_run_jobs(self.send_jobs(), name="rs_pair_send_" + self.tag))
        if self.out is None:
            self.after_exchange(_run_jobs(self.exchange_jobs(), name="rs_exchange_" + self.tag))
        return self.out


def _all_reduce_small(vec, *, name):
    r, wd = vec.shape

    def body(in_ref, out_ref, slots, send_sems, recv_sems):
        x, y, c = _position()
        me = 4 * x + 2 * y + c
        flips = [(fx, fy, fc) for fx in (0, 1) for fy in (0, 1) for fc in (0, 1) if fx + fy + fc]
        peers = [(1 - x if fx else x, 1 - y if fy else y, 1 - c if fc else c) for fx, fy, fc in flips]
        sends = []
        for k, peer in enumerate(peers):
            cp = pltpu.make_async_remote_copy(src_ref=in_ref, dst_ref=slots.at[me], send_sem=send_sems.at[k],
                                              recv_sem=recv_sems.at[k], device_id=peer, device_id_type=MESH)
            cp.start()
            sends.append(cp)
        slots[me] = in_ref[...]
        for k, (px, py, pc) in enumerate(peers):
            pltpu.make_async_remote_copy(src_ref=in_ref, dst_ref=slots.at[4 * px + 2 * py + pc], send_sem=send_sems.at[k],
                                         recv_sem=recv_sems.at[k], device_id=(px, py, pc), device_id_type=MESH).wait_recv()
        for cp in sends:
            cp.wait_send()
        acc = slots[0]
        for q in range(1, N_DEV):
            acc = acc + slots[q]
        out_ref[...] = acc

    vm = pl.BlockSpec(memory_space=pltpu.VMEM)
    return pl.pallas_call(
        body, name=name, in_specs=[vm], out_specs=vm, out_shape=jax.ShapeDtypeStruct((r, wd), F32),
        scratch_shapes=[pltpu.VMEM((N_DEV, r, wd), F32), pltpu.SemaphoreType.DMA((N_DEV - 1,)),
                        pltpu.SemaphoreType.DMA((N_DEV - 1,))],
        compiler_params=_cparams(),
    )(vec)


def _adamw(w, g, m, v, *, name, jobs=()):
    r, wd = w.shape
    tr = _tile(r, (256, 128, 64, 32, 16, 8))
    nj, steps = len(jobs), r // tr

    def body(*refs):
        w_ref, g_ref, m_ref, v_ref = refs[:4]
        j_in, (d_ref, m_out, v_out) = refs[4:4 + nj], refs[4 + nj:7 + nj]
        j_out, j_sems = refs[7 + nj:7 + 2 * nj], refs[7 + 2 * nj:]
        if nj:
            @pl.when(pl.program_id(0) == 0)
            def _():
                _jobs_start(jobs, j_in, j_out, j_sems)

        gv = g_ref[...]
        m_new = ADAM_B1 * m_ref[...] + (1.0 - ADAM_B1) * gv
        v_new = ADAM_B2 * v_ref[...] + (1.0 - ADAM_B2) * (gv * gv)
        m_hat = m_new / (1.0 - ADAM_B1 ** ADAM_STEP)
        v_hat = v_new / (1.0 - ADAM_B2 ** ADAM_STEP)
        d_ref[...] = -ADAM_LR * (m_hat / (jnp.sqrt(v_hat) + ADAM_EPS) + ADAM_WD * w_ref[...])
        m_out[...] = m_new
        v_out[...] = v_new
        if nj:
            @pl.when(pl.program_id(0) == steps - 1)
            def _():
                _jobs_finish(jobs, j_in, j_out, j_sems)

    spec = pl.BlockSpec((tr, wd), lambda j: (j, 0))
    res = pl.pallas_call(body, name=name, grid=(steps,), in_specs=[spec] * 4 + [_ANY] * nj, out_specs=[spec] * 3 + [_ANY] * nj,
                         out_shape=[jax.ShapeDtypeStruct((r, wd), F32)] * 3 + [_job_out_shape(j) for j in jobs],
                         scratch_shapes=_jobs_scratch(jobs), compiler_params=_cparams(("arbitrary",)),
                         )(w, g, m, v, *_job_arrays(jobs))
    return list(res[:3]), list(res[3:])


_WEIGHTS = ("ln_mix", "w_in", "w_in_vres", "mu_shift", "mu_shift_vres", "conv_a_w", "conv_a_b", "lru_wx", "lru_bx", "lru_wa",
            "lru_ba", "lru_lambda", "lru_norm", "rwkv_w0", "rwkv_w2", "rwkv_a0", "rwkv_a2", "rwkv_v0", "rwkv_v2", "rwkv_g2",
            "rwkv_kk", "rwkv_ka", "rwkv_rk", "rwkv_lnx_w", "rwkv_lnx_b", "w_o", "ln_ffn", "w_gate", "w_up", "conv_f_w",
            "conv_f_b", "w_down", "ln_ple", "w_ple_gate", "w_ple_proj", "ln_ple_post", "ln_final")
_SHARD_AXIS = {"w_in": 2, "w_in_vres": 1, "conv_a_w": 2, "lru_wx": 2, "lru_wa": 2, "rwkv_w2": 2, "rwkv_a2": 2, "rwkv_v2": 2,
               "rwkv_g2": 2, "w_o": 1, "w_gate": 2, "w_up": 2, "conv_f_w": 2, "w_down": 1, "w_ple_gate": 1, "w_ple_proj": 2}
_BIG_SHARDED = ("w_in",) + _BIG
_SMALL_SHARDED = tuple(k for k in _WEIGHTS if k in _SHARD_AXIS and k not in _BIG_SHARDED)
_REPLICATED = tuple(k for k in _WEIGHTS if k not in _SHARD_AXIS)
PACK_WIDTH = 512


def _to_shards(g, axis):
    n = g.shape[axis] // N_XY
    return jnp.moveaxis(g.reshape(g.shape[:axis] + (N_XY, n) + g.shape[axis + 1:]), axis, 0)


def _from_shards(s, axis):
    s = jnp.moveaxis(s, 0, axis)
    return s.reshape(s.shape[:axis] + (N_XY * s.shape[axis + 1],) + s.shape[axis + 2:])


def _pack(arrs, lead, width, row_mult):
    lead_shape = arrs[0].shape[:lead]
    flat = jnp.concatenate([a.reshape(lead_shape + (-1,)) for a in arrs], axis=-1)
    n = flat.shape[-1]
    total = _round_up(n, width * row_mult)
    flat = jnp.pad(flat, [(0, 0)] * lead + [(0, total - n)])
    return flat.reshape(lead_shape + (total // width, width))


def _unpack(packed, shapes):
    flat = packed.reshape(-1)
    out, o = [], 0
    for s in shapes:
        n = 1
        for dim in s:
            n *= dim
        out.append(flat[o:o + n].reshape(s))
        o += n
    return out


def _as2d(a):
    return a.reshape(-1, a.shape[-1])


def kernel(x, p, ln_mix, w_in, w_in_vres, mu_shift, mu_shift_vres, conv_a_w, conv_a_b, lru_wx, lru_bx, lru_wa, lru_ba, lru_lambda, lru_norm, rwkv_w0, rwkv_w2, rwkv_a0, rwkv_a2, rwkv_v0, rwkv_v2, rwkv_g2, rwkv_kk, rwkv_ka, rwkv_rk, rwkv_lnx_w, rwkv_lnx_b, w_o, ln_ffn, w_gate, w_up, conv_f_w, conv_f_b, w_down, ln_ple, w_ple_gate, w_ple_proj, ln_ple_post, ln_final, loss_target, m_ln_mix, m_w_in, m_w_in_vres, m_mu_shift, m_mu_shift_vres, m_conv_a_w, m_conv_a_b, m_lru_wx, m_lru_bx, m_lru_wa, m_lru_ba, m_lru_lambda, m_lru_norm, m_rwkv_w0, m_rwkv_w2, m_rwkv_a0, m_rwkv_a2, m_rwkv_v0, m_rwkv_v2, m_rwkv_g2, m_rwkv_kk, m_rwkv_ka, m_rwkv_rk, m_rwkv_lnx_w, m_rwkv_lnx_b, m_w_o, m_ln_ffn, m_w_gate, m_w_up, m_conv_f_w, m_conv_f_b, m_w_down, m_ln_ple, m_w_ple_gate, m_w_ple_proj, m_ln_ple_post, m_ln_final, v_ln_mix, v_w_in, v_w_in_vres, v_mu_shift, v_mu_shift_vres, v_conv_a_w, v_conv_a_b, v_lru_wx, v_lru_bx, v_lru_wa, v_lru_ba, v_lru_lambda, v_lru_norm, v_rwkv_w0, v_rwkv_w2, v_rwkv_a0, v_rwkv_a2, v_rwkv_v0, v_rwkv_v2, v_rwkv_g2, v_rwkv_kk, v_rwkv_ka, v_rwkv_rk, v_rwkv_lnx_w, v_rwkv_lnx_b, v_w_o, v_ln_ffn, v_w_gate, v_w_up, v_conv_f_w, v_conv_f_b, v_w_down, v_ln_ple, v_w_ple_gate, v_w_ple_proj, v_ln_ple_post, v_ln_final):
    a = dict(locals())
    x2, p, tgt = a["x"][0], a["p"][:, 0], a["loss_target"][0]
    pos = tuple(lax.axis_index(ax).astype(jnp.int32).reshape(1) for ax in ("c", "x", "y"))

    wf = {k: a[k] for k in _REPLICATED}
    small_shapes = [a[k].shape for k in _SMALL_SHARDED]
    shards = {k: [a[k][i].astype(BF16) for i in range(a[k].shape[0])] for k in _BIG_SHARDED}
    packed = _pack([a[k] for k in _SMALL_SHARDED], 0, PACK_WIDTH, 16)
    got_small, got_w_in = _run_jobs([_gather_blob(packed), _gather_blob(shards["w_in"][0])], name="ag_first")
    got_small = got_small.reshape((N_XY,) + packed.shape)
    pieces = [_unpack(got_small[q], small_shapes) for q in range(N_XY)]
    for j, k in enumerate(_SMALL_SHARDED):
        wf[k] = _from_shards(jnp.stack([pieces[q][j] for q in range(N_XY)], axis=0), _SHARD_AXIS[k])

    m = _make_dims(x2, p, wf)
    m.dw_shards = N_XY
    w = _prepare_weights(m, wf)
    feed = _WeightFeed(m, w, shards, wf["w_in_vres"].astype(BF16))
    feed.arrive([("w_in", 0)], [got_w_in])
    def reducer(i, g, which):
        names = ("w_in",) if which == "late" else _BIG
        full = {k: (g["wcat"][:, :m.din] if k == "w_in" else g[k]) for k in names}
        slabs = {k: full[k] if full[k].ndim == 3 else _to_shards(full[k], _SHARD_AXIS[k] - 1) for k in names}
        return _GradReduce(slabs, pos, f"{which}_{i}")

    loss_row, dx, grads, d_ln_final, reductions = _local_step(m, w, x2, p, tgt, feed, reducer)
    gfull = _unpack_grads(m, grads, d_ln_final, with_big=False)
    loss = lax.psum(loss_row[0, 0], ("x", "y", "c"))

    gs = _pack([_to_shards(gfull[k], _SHARD_AXIS[k]) for k in _SMALL_SHARDED], 1, PACK_WIDTH, 32)
    g_small = _GradReduce({"small": gs}, pos, "small").run()["small"]
    rep_shapes = [a[k].shape for k in _REPLICATED]
    g_rep = _all_reduce_small(_pack([gfull[k] for k in _REPLICATED], 0, LANES_V7X, 8), name="ar_replicated")

    last = reductions[0][1]
    gred, delta, new_m, new_v = {}, {}, {}, {}
    carried = {"w_gate": (last.send_jobs, last.after_send), "w_up": (last.exchange_jobs, last.after_exchange)}
    for k in _BIG + ("w_in",):
        which = 1 if k == "w_in" else 0
        per_layer = [reductions[i][which].run()[k] for i in range(m.nl)]
        gred[k] = jnp.stack(per_layer, axis=0).reshape(a[k].shape)
        make_jobs, after = carried.get(k, (lambda: (), None))
        res, job_results = _adamw(_as2d(a[k]), _as2d(gred[k]), _as2d(a["m_" + k]), _as2d(a["v_" + k]), name="adamw_" + k,
                                  jobs=make_jobs())
        if after is not None:
            after(job_results)
        delta[k], new_m[k], new_v[k] = (r.reshape(a[k].shape) for r in res)
    for names, shapes, g_packed, width, mult, tag in ((_SMALL_SHARDED, small_shapes, g_small, PACK_WIDTH, 32, "small"),
                                                      (_REPLICATED, rep_shapes, g_rep, LANES_V7X, 8, "replicated")):
        packs = [_pack([a[pre + k] for k in names], 0, width, mult) for pre in ("", "m_", "v_")]
        res, _ = _adamw(packs[0], g_packed, packs[1], packs[2], name="adamw_" + tag)
        for dst, r in zip((gred, delta, new_m, new_v), [g_packed] + list(res)):
            dst.update(zip(names, _unpack(r, shapes)))
    return (loss, dx[None], *[gred[k] for k in _WEIGHTS], *[delta[k] for k in _WEIGHTS],
            *[new_m[k] for k in _WEIGHTS], *[new_v[k] for k in _WEIGHTS])
```

```python
import functools

import jax
import jax.numpy as jnp
from jax import lax
from jax.experimental import pallas as pl
from jax.experimental.pallas import tpu as pltpu

F32 = jnp.float32
BF16 = jnp.bfloat16
HIGHEST = lax.Precision.HIGHEST
MESH = pl.DeviceIdType.MESH

RMS_EPS = 1e-6
LNX_EPS = 64e-5
LRU_C = 8.0
ADAM_LR = 0.001
ADAM_B1 = 0.9
ADAM_B2 = 0.999
ADAM_EPS = 1e-08
ADAM_WD = 0.01
ADAM_STEP = 10

LANES_V7X = 128
VMEM_LIMIT_V7X = 60 * 1024 * 1024
WKV_CHUNK = 16
N_XY = 4
N_DEV = 8


def _cparams(sem=None, **kw):
    if sem is not None:
        kw["dimension_semantics"] = sem
    return pltpu.CompilerParams(vmem_limit_bytes=VMEM_LIMIT_V7X, **kw)


def _tile(dim, prefs):
    for t in prefs:
        if dim % t == 0:
            return t
    return dim


def _round_up(n, m):
    return (n + m - 1) // m * m


MM_MAX_TK = 2816


def _tile_k(kdim):
    best = None
    for t in range(LANES_V7X, min(kdim, MM_MAX_TK) + 1, LANES_V7X):
        if kdim % t == 0:
            best = t
    return best or kdim


def _mm(a, b, *, ta=False, tb=False, res=None, out_dtype=F32, name, gather=(), out_shards=0):
    if ta:
        kdim, m = a.shape
    else:
        m, kdim = a.shape
    bs = b.shape[0] if b.ndim == 3 else 0
    if bs:
        b_rows, b_cols = b.shape[1], bs * b.shape[2]
    else:
        b_rows, b_cols = b.shape
    n = b_rows if tb else b_cols
    assert (b_cols if tb else b_rows) == kdim
    per_shard = (lambda total, s: total // s if s else total)
    tk = _tile_k(per_shard(kdim, bs) if tb else kdim)
    tm = _tile(m, (2048, 1024, 512, 256, 128) if tk <= 2048 else (1024, 512, 256, 128))
    tn = _tile(per_shard(per_shard(n, out_shards), 0 if tb else bs), (512, 256, 128))
    nk = kdim // tk
    ni, nj = m // tm, n // tn
    a_spec = pl.BlockSpec((tk, tm), lambda i, j, k: (k, i)) if ta else pl.BlockSpec((tm, tk), lambda i, j, k: (i, k))
    if bs and tb:
        kps = b.shape[2] // tk
        b_spec = pl.BlockSpec((None, tn, tk), lambda i, j, k: (k // kps, j, k % kps))
    elif bs:
        nps = b.shape[2] // tn
        b_spec = pl.BlockSpec((None, tk, tn), lambda i, j, k: (j // nps, k, j % nps))
    else:
        b_spec = pl.BlockSpec((tn, tk), lambda i, j, k: (j, k)) if tb else pl.BlockSpec((tk, tn), lambda i, j, k: (k, j))
    if out_shards:
        assert res is None
        ops = n // out_shards // tn
        o_spec = pl.BlockSpec((None, tm, tn), lambda i, j, k: (j // ops, i, j % ops))
        o_shape = jax.ShapeDtypeStruct((out_shards, m, n // out_shards), out_dtype)
    else:
        o_spec = pl.BlockSpec((tm, tn), lambda i, j, k: (i, j))
        o_shape = jax.ShapeDtypeStruct((m, n), out_dtype)
    dn = (((0 if ta else 1,), (1 if tb else 0,)), ((), ()))
    has_res = res is not None
    ng = len(gather)
    nin = 2 + has_res

    def body(*refs):
        a_ref, b_ref = refs[:2]
        r_ref = refs[2] if has_res else None
        g_in, o_ref, g_out = refs[nin:nin + ng], refs[nin + ng], refs[nin + ng + 1:nin + 2 * ng + 1]
        scratch = refs[nin + 2 * ng + 1:]
        acc_ref = scratch[0] if nk > 1 else None
        g_sems = scratch[1 if nk > 1 else 0:]
        i, j, k = pl.program_id(0), pl.program_id(1), pl.program_id(2)

        if ng:
            @pl.when((i == 0) & (j == 0) & (k == 0))
            def _():
                _jobs_start(gather, g_in, g_out, g_sems)

        def finish(acc):
            if has_res:
                acc = acc + r_ref[...].astype(F32)
            o_ref[...] = acc.astype(out_dtype)

        prod = lax.dot_general(a_ref[...], b_ref[...], dn, preferred_element_type=F32)
        if nk == 1:
            finish(prod)
        else:
            @pl.when(k == 0)
            def _():
                acc_ref[...] = prod

            @pl.when(k > 0)
            def _():
                acc_ref[...] += prod

            @pl.when(k == nk - 1)
            def _():
                finish(acc_ref[...])

        if ng:
            @pl.when((i == ni - 1) & (j == nj - 1) & (k == nk - 1))
            def _():
                _jobs_finish(gather, g_in, g_out, g_sems)

    ins = [a, b] + ([res] if has_res else []) + _job_arrays(gather)
    in_specs = [a_spec, b_spec] + ([o_spec] if has_res else []) + [_ANY] * ng
    scratch = ([pltpu.VMEM((tm, tn), F32)] if nk > 1 else []) + _jobs_scratch(gather)
    sem = ("arbitrary",) * 3 if ng else ("parallel", "parallel", "arbitrary")
    out = pl.pallas_call(
        body, name=name, grid=(ni, nj, nk), in_specs=in_specs, out_specs=[o_spec] + [_ANY] * ng,
        out_shape=[o_shape] + [_job_out_shape(g) for g in gather],
        scratch_shapes=scratch, compiler_params=_cparams(sem),
    )(*ins)
    return (out[0], list(out[1:])) if ng else out[0]


def _stage_specs(axis, tile, tiled, params, consts, rows):
    specs = []
    for arr, width, cblk in tiled:
        if axis == 0:
            specs.append(pl.BlockSpec((tile, width), functools.partial(lambda i, c: (i, c), c=cblk)))
        else:
            specs.append(pl.BlockSpec((rows, tile), functools.partial(lambda i, c: (0, i + c), c=cblk)))
    for arr, cblk in params:
        if axis == 0:
            specs.append(pl.BlockSpec(arr.shape, functools.partial(lambda i, nd: (0,) * nd, nd=arr.ndim)))
        else:
            specs.append(pl.BlockSpec((arr.shape[0], tile), functools.partial(lambda i, c: (0, i + c), c=cblk)))
    for arr in consts:
        specs.append(pl.BlockSpec(arr.shape, functools.partial(lambda i, nd: (0,) * nd, nd=arr.ndim)))
    return specs


def _stage_fwd(fn, tiled, params, consts, outs, *, axis, tile, rows, name):
    nt, npar, nc = len(tiled), len(params), len(consts)
    ntiles = (rows // tile) if axis == 0 else (outs[0][0] // tile)

    def body(*refs):
        ins = refs[: nt + npar + nc]
        orefs = refs[nt + npar + nc:]
        vals = [r[...].astype(F32) for r in ins[: nt + npar]] + [r[...] for r in ins[nt + npar:]]
        ctx = pl.program_id(0) * tile
        res = fn(ctx, *vals)
        for o_ref, o in zip(orefs, res):
            o_ref[...] = o.astype(o_ref.dtype)

    if axis == 0:
        out_specs = [pl.BlockSpec((tile, w), lambda i: (i, 0)) for w, _ in outs]
    else:
        out_specs = [pl.BlockSpec((rows, tile), lambda i: (0, i)) for w, _ in outs]
    res = pl.pallas_call(
        body, name=name, grid=(ntiles,),
        in_specs=_stage_specs(axis, tile, tiled, params, consts, rows), out_specs=out_specs,
        out_shape=[jax.ShapeDtypeStruct((rows, w), dt) for w, dt in outs],
        compiler_params=_cparams(("arbitrary",)),
    )(*[t[0] for t in tiled], *[p[0] for p in params], *consts)
    return res


def _stage_bwd(fn, tiled, params, consts, cots, dtiled, *, axis, tile, rows, name, ncols=None):
    nt, npar, nc, nco = len(tiled), len(params), len(consts), len(cots)
    ntiles = (rows // tile) if axis == 0 else (ncols // tile)
    didx = [d[0] for d in dtiled]

    def body(*refs):
        ins = refs[: nt + npar + nc]
        crefs = refs[nt + npar + nc: nt + npar + nc + nco]
        orefs = refs[nt + npar + nc + nco:]
        vals = [r[...].astype(F32) for r in ins[: nt + npar]] + [r[...] for r in ins[nt + npar:]]
        ctx = pl.program_id(0) * tile

        def g(*dv):
            full = list(vals)
            for j, ix in enumerate(didx):
                full[ix] = dv[j]
            for j in range(npar):
                full[nt + j] = dv[len(didx) + j]
            return tuple(fn(ctx, *full))

        prim = [vals[ix] for ix in didx] + [vals[nt + j] for j in range(npar)]
        _, vjp = jax.vjp(g, *prim)
        grads = vjp(tuple(c[...].astype(F32) for c in crefs))
        for j in range(len(didx)):
            orefs[j][...] = grads[j].astype(orefs[j].dtype)
        for j in range(npar):
            o_ref = orefs[len(didx) + j]
            gp = grads[len(didx) + j]
            if axis == 0:
                @pl.when(pl.program_id(0) == 0)
                def _(o_ref=o_ref):
                    o_ref[...] = jnp.zeros_like(o_ref)

                o_ref[...] += gp
            else:
                o_ref[...] = gp

    if axis == 0:
        cot_specs = [pl.BlockSpec((tile, w), functools.partial(lambda i, c: (i, c), c=cb)) for _, w, cb in cots]
        out_specs = [pl.BlockSpec((tile, w), lambda i: (i, 0)) for _, w, _ in dtiled]
        out_specs += [pl.BlockSpec(p.shape, functools.partial(lambda i, nd: (0,) * nd, nd=p.ndim)) for p, _ in params]
        out_shape = [jax.ShapeDtypeStruct((rows, w), dt) for _, w, dt in dtiled]
        out_shape += [jax.ShapeDtypeStruct(p.shape, F32) for p, _ in params]
    else:
        cot_specs = [pl.BlockSpec((rows, tile), functools.partial(lambda i, c: (0, i + c), c=cb)) for _, cb in cots]
        out_specs = [pl.BlockSpec((rows, tile), lambda i: (0, i)) for _ in dtiled]
        out_specs += [pl.BlockSpec((p.shape[0], tile), lambda i: (0, i)) for p, _ in params]
        out_shape = [jax.ShapeDtypeStruct((rows, w), dt) for _, w, dt in dtiled]
        out_shape += [jax.ShapeDtypeStruct((p.shape[0], ncols), F32) for p, _ in params]
    return pl.pallas_call(
        body, name=name, grid=(ntiles,),
        in_specs=_stage_specs(axis, tile, tiled, params, consts, rows) + cot_specs, out_specs=out_specs,
        out_shape=out_shape, compiler_params=_cparams(("arbitrary",)),
    )(*[t[0] for t in tiled], *[p[0] for p in params], *consts, *[c[0] for c in cots])


def _rms(x, g):
    return x * lax.rsqrt(jnp.mean(x * x, axis=-1, keepdims=True) + RMS_EPS) * g


def _row_mask(x, k, first):
    t = lax.broadcasted_iota(jnp.int32, x.shape, 0)
    keep = (t >= k) if first else (t < x.shape[0] - k)
    return jnp.where(keep, x, 0.0)


@functools.partial(jax.custom_vjp, nondiff_argnums=(1,))
def _shift_down(x, k):
    return _row_mask(pltpu.roll(x, k, 0), k, True)


def _shift_down_fwd(x, k):
    return _shift_down(x, k), None


def _shift_down_bwd(k, _, g):
    return (_row_mask(pltpu.roll(g, g.shape[0] - k, 0), k, False),)


_shift_down.defvjp(_shift_down_fwd, _shift_down_bwd)


def _dwconv(x, w, b):
    kw = w.shape[0]
    out = x * w[kw - 1:kw] + b
    for j in range(kw - 1):
        out = out + _shift_down(x, kw - 1 - j) * w[j:j + 1]
    return out


def _f_norm(ctx, x, g):
    return (_rms(x, g),)


def _f_norm_res(ctx, x, g):
    return (_rms(x, g), x)


def _f_shiftmix(ctx, z, mu):
    return (z + (_shift_down(z, 1) - z) * mu,)


def _f_conv(ctx, x, w, b):
    return (_dwconv(x, w, b),)


def _f_ffn_act(ctx, gpre, up, w, b):
    return (jax.nn.gelu(_dwconv(gpre, w, b)) * up,)


def _make_f_lru_gates(heads):
    def fn(ctx, xb, wx, wa, bx, ba, lam):
        blk = xb.shape[1] // heads
        px, pa = [], []
        for h in range(heads):
            xh = xb[:, h * blk:(h + 1) * blk]
            px.append(jnp.dot(xh, wx[h], preferred_element_type=F32))
            pa.append(jnp.dot(xh, wa[h], preferred_element_type=F32))
        px = px[0] if heads == 1 else jnp.concatenate(px, axis=1)
        pa = pa[0] if heads == 1 else jnp.concatenate(pa, axis=1)
        gate_x = jax.nn.sigmoid(px + bx)
        gate_a = jax.nn.sigmoid(pa + ba)
        log_a = -LRU_C * gate_a * jax.nn.softplus(-lam)
        a = jnp.exp(log_a)
        mult = jnp.sqrt(1.0 - jnp.exp(2.0 * log_a))
        t = ctx + lax.broadcasted_iota(jnp.int32, xb.shape, 0)
        mult = jnp.where(t == 0, 1.0, mult)
        return a, xb * gate_x * mult

    return fn


def _f_lru_out(ctx, hl, ya, g):
    return (_rms(hl * jax.nn.gelu(ya), g),)


def _headsum_3pass(x, bb):
    hi = x.astype(BF16)
    r1 = x - hi.astype(F32)
    mid = r1.astype(BF16)
    lo = (r1 - mid.astype(F32)).astype(BF16)
    width, group = x.shape[1], bb.shape[0]
    out = []
    for g0 in range(0, width, group):
        cols = slice(g0, g0 + group)
        out.append(jnp.dot(hi[:, cols], bb, preferred_element_type=F32) + jnp.dot(mid[:, cols], bb, preferred_element_type=F32)
                   + jnp.dot(lo[:, cols], bb, preferred_element_type=F32))
    return out[0] if len(out) == 1 else jnp.concatenate(out, axis=1)


@jax.custom_vjp
def _headsum(x, bb):
    return _headsum_3pass(x, bb)


def _headsum_fwd(x, bb):
    return _headsum_3pass(x, bb), bb


def _headsum_bwd(bb, g):
    return _headsum_3pass(g, bb), None


_headsum.defvjp(_headsum_fwd, _headsum_bwd)


def _make_f_rwkv_pre(has_vres, v_uses=0):
    def fn(ctx, *args):
        if v_uses:
            r, args = args[0], args[1:]
        if has_vres:
            k, v, lz, vf, w0, w2, a0, a2, g2, kkw, ka, v0, v2, bb = args
        else:
            k, v, lz, w0, w2, a0, a2, g2, kkw, ka, bb = args
        def lora(x, wt):
            return jnp.dot(x.astype(BF16), wt.astype(BF16), preferred_element_type=F32)

        w_log = -jax.nn.softplus(-(w0 + lora(jnp.tanh(lz), w2))) - 0.5
        logw = -jnp.exp(w_log)
        a = jax.nn.sigmoid(a0 + lora(lz, a2))
        g = lora(jax.nn.sigmoid(lz), g2)
        if has_vres:
            v = v + (vf - v) * jax.nn.sigmoid(v0 + lora(lz, v2))
        xk = k * kkw
        kk = xk / jnp.maximum(jnp.sqrt(_headsum(xk * xk, bb)), 1e-12)
        k2 = k * (1.0 + (a - 1.0) * ka)
        if v_uses:
            return (r, r, logw, k2, k2) + (v,) * v_uses + (kk, kk * a, g)
        return logw, k2, v, kk, kk * a, g

    return fn


def _make_f_rwkv_post(head_size):
    def fn(ctx, y, r, k2, v2, g, lnw, lnb, rk, bb):
        mean = _headsum(y, bb) / head_size
        d = y - mean
        var = _headsum(d * d, bb) / head_size
        yn = d * lax.rsqrt(var + LNX_EPS) * lnw + lnb
        bonus = _headsum(r * k2 * rk, bb) * v2
        return ((yn + bonus) * g,)

    return fn


def _f_ple(ctx, h, eg, ep, g):
    return (h + _rms(jax.nn.sigmoid(eg) * ep, g),)


def _lru_scan(a, b, *, name):
    rows, cols = a.shape
    tc = _tile(cols, (512, 256, 128))

    def body(a_ref, b_ref, h_ref):
        def step(t, carry):
            h = a_ref[pl.ds(t, 1), :] * carry + b_ref[pl.ds(t, 1), :]
            h_ref[pl.ds(t, 1), :] = h
            return h

        lax.fori_loop(0, rows, step, jnp.zeros((1, tc), F32), unroll=8)

    spec = pl.BlockSpec((rows, tc), lambda j: (0, j))
    return pl.pallas_call(body, name=name, grid=(cols // tc,), in_specs=[spec, spec], out_specs=spec,
                          out_shape=jax.ShapeDtypeStruct((rows, cols), F32), compiler_params=_cparams(("arbitrary",)))(a, b)


def _lru_scan_bwd(a, h, dh, *, name):
    rows, cols = a.shape
    tc = _tile(cols, (512, 256, 128))

    def body(a_ref, h_ref, dh_ref, da_ref, db_ref):
        def step(i, carry):
            t = rows - 1 - i
            g = dh_ref[pl.ds(t, 1), :] + carry
            db_ref[pl.ds(t, 1), :] = g
            hp = h_ref[pl.ds(jnp.maximum(t - 1, 0), 1), :]
            da_ref[pl.ds(t, 1), :] = jnp.where(t > 0, g * hp, 0.0)
            return a_ref[pl.ds(t, 1), :] * g

        lax.fori_loop(0, rows, step, jnp.zeros((1, tc), F32), unroll=8)

    spec = pl.BlockSpec((rows, tc), lambda j: (0, j))
    return pl.pallas_call(body, name=name, grid=(cols // tc,), in_specs=[spec] * 3, out_specs=[spec] * 2,
                          out_shape=[jax.ShapeDtypeStruct((rows, cols), F32)] * 2,
                          compiler_params=_cparams(("arbitrary",)))(a, h, dh)


def _split_bf16(x):
    hi = x.astype(BF16)
    return hi, (x - hi.astype(F32)).astype(BF16)


def _dot3_passes(a, b, ca, cb):
    dn = (((ca,), (cb,)), ((), ()))
    ah, al = _split_bf16(a)
    bh, bl = _split_bf16(b)
    return (lax.dot_general(ah, bh, dn, preferred_element_type=F32) + lax.dot_general(al, bh, dn, preferred_element_type=F32)
            + lax.dot_general(ah, bl, dn, preferred_element_type=F32))


@functools.partial(jax.custom_vjp, nondiff_argnums=(2, 3))
def _dot3(a, b, ca, cb):
    return _dot3_passes(a, b, ca, cb)


def _dot3_fwd(a, b, ca, cb):
    return _dot3_passes(a, b, ca, cb), (a, b)


def _dot3_bwd(ca, cb, res, g):
    a, b = res
    fa, fb = 1 - ca, 1 - cb
    da = _dot3_passes(g, b, 1, fb) if ca == 1 else _dot3_passes(b, g, fb, 1)
    db = _dot3_passes(a, g, fa, 0) if cb == 0 else _dot3_passes(g, a, 0, fa)
    return da, db


_dot3.defvjp(_dot3_fwd, _dot3_bwd)


def _each(f, *lists):
    return [f(*t) for t in zip(*lists)]


def _wkv_local(r, lw, k, v, kk, b):
    c, n = r[0].shape
    row = lax.broadcasted_iota(jnp.int32, (c, c), 0)
    col = lax.broadcasted_iota(jnp.int32, (c, c), 1)
    incl = (row >= col).astype(F32)
    strict = (row > col).astype(F32)
    eye = lax.broadcasted_iota(jnp.int32, (n, n), 0) == lax.broadcasted_iota(jnp.int32, (n, n), 1)
    cl = _each(lambda x: _dot3(incl, x, 1, 0), lw)
    w_t = _each(jnp.exp, cl)
    inv_w = _each(lambda x: jnp.exp(-x), cl)
    kk_s = _each(lambda x, y, z: x * jnp.exp(y - z), kk, cl, lw)
    b_s = _each(jnp.multiply, b, inv_w)
    k_s = _each(jnp.multiply, k, inv_w)
    r_s = _each(jnp.multiply, r, w_t)
    q = _each(lambda x, y: jnp.concatenate([x, y], axis=0), kk_s, r_s)
    qb = _each(lambda x, y: _dot3(x, y, 1, 1), q, b_s)
    qk = _each(lambda x, y: _dot3(x, y, 1, 1), q, k_s)
    m = _each(lambda x: -strict * x[:c], qb)
    pb = _each(lambda x: incl * x[c:], qb)
    lkv = _each(lambda x, y: _dot3(strict * x[:c], y, 1, 0), qk, v)
    pkv = _each(lambda x, y: _dot3(incl * x[c:], y, 1, 0), qk, v)
    a = _each(lambda x, y: jnp.concatenate([x, y], axis=1), kk_s, lkv)
    steps = max(1, (c - 1).bit_length())
    for i in range(steps):
        a = _each(lambda x, y: y + _dot3(x, y, 1, 0), m, a)
        if i + 1 < steps:
            m = _each(lambda x: _dot3(x, x, 1, 0), m)
    ry = _each(lambda x, y, z, w: jnp.concatenate([x, y], axis=1) - _dot3(z, w, 1, 0), r_s, pkv, pb, a)
    w_end = _each(lambda x: x[c - 1:c, :], w_t)
    gu_low = _each(lambda x, y, z: _dot3(x, y * z, 0, 0), a, b_s, w_end)
    g = _each(lambda x, y: jnp.where(eye, jnp.broadcast_to(x, (n, n)), 0.0) - y[:n], w_end, gu_low)
    u = _each(lambda x, y, z, w: _dot3(x, y * z, 0, 0) - w[n:], v, k_s, w_end, gu_low)
    return g, u, _each(lambda x: x[:, :n], ry), _each(lambda x: x[:, n:], ry)


def _wkv_blocks(h, nchunk):
    return (_tile(h, (4, 2, 1)), _tile(nchunk, (4, 2, 1))), (h, _tile(nchunk, (4, 2, 1)))


def _wkv_fwd(r, lw, k, v, kk, b, *, name, gather=(), gather_state=()):
    h, t, n = r.shape
    c = WKV_CHUNK
    nchunk = t // c
    (hb, _), (hs, cs) = _wkv_blocks(h, nchunk)
    cb = _tile(nchunk, (8, 4, 2, 1))
    ng = len(gather)
    ni, nj = h // hb, nchunk // cb

    pairs = [(i, j) for i in range(hb) for j in range(cb)]

    def local_body(*refs):
        ins, g_in = refs[:6], refs[6:6 + ng]
        g_ref, u_ref, r2_ref, y0_ref = refs[6 + ng:10 + ng]
        g_out, g_sems = refs[10 + ng:10 + 2 * ng], refs[10 + 2 * ng:]
        if ng:
            @pl.when((pl.program_id(0) == 0) & (pl.program_id(1) == 0))
            def _():
                _jobs_start(gather, g_in, g_out, g_sems)

        g, u, r2, y0 = _wkv_local(*[[ref[i, pl.ds(j * c, c)] for i, j in pairs] for ref in ins])
        for idx, (i, j) in enumerate(pairs):
            g_ref[i, j] = g[idx]
            u_ref[i, j] = u[idx]
            r2_ref[i, pl.ds(j * c, c)] = r2[idx]
            y0_ref[i, pl.ds(j * c, c)] = y0[idx]
        if ng:
            @pl.when((pl.program_id(0) == ni - 1) & (pl.program_id(1) == nj - 1))
            def _():
                _jobs_finish(gather, g_in, g_out, g_sems)

    seq = pl.BlockSpec((hb, cb * c, n), lambda i, j: (i, j, 0))
    mat = pl.BlockSpec((hb, cb, n, n), lambda i, j: (i, j, 0, 0))
    res = pl.pallas_call(
        local_body, name=name + "_local", grid=(ni, nj), in_specs=[seq] * 6 + [_ANY] * ng,
        out_specs=[mat, mat, seq, seq] + [_ANY] * ng,
        out_shape=[jax.ShapeDtypeStruct((h, nchunk, n, n), F32)] * 2 + [jax.ShapeDtypeStruct((h, t, n), F32)] * 2
        + [_job_out_shape(g) for g in gather],
        scratch_shapes=_jobs_scratch(gather),
        compiler_params=_cparams(("arbitrary", "arbitrary") if ng else ("parallel", "parallel")),
    )(r, lw, k, v, kk, b, *_job_arrays(gather))
    gm, um, r2, y0 = res[:4]
    gathered = list(res[4:])

    ng2 = len(gather_state)
    nsteps = nchunk // cs

    def state_body(*refs):
        g_ref, u_ref, r2_ref, y0_ref = refs[:4]
        g_in, (y_ref, st_ref) = refs[4:4 + ng2], refs[4 + ng2:6 + ng2]
        g_out, s_ref, g_sems = refs[6 + ng2:6 + 2 * ng2], refs[6 + 2 * ng2], refs[7 + 2 * ng2:]

        @pl.when(pl.program_id(0) == 0)
        def _():
            s_ref[...] = jnp.zeros_like(s_ref)
            _jobs_start(gather_state, g_in, g_out, g_sems)

        s = [s_ref[i] for i in range(hs)]
        for j in range(cs):
            rows = pl.ds(j * c, c)
            for i in range(hs):
                st_ref[i, j] = s[i]
                y_ref[i, rows] = _dot3(r2_ref[i, rows], s[i], 1, 1) + y0_ref[i, rows]
            s = [_dot3(s[i], g_ref[i, j], 1, 0) + u_ref[i, j] for i in range(hs)]
        for i in range(hs):
            s_ref[i] = s[i]
        if ng2:
            @pl.when(pl.program_id(0) == nsteps - 1)
            def _():
                _jobs_finish(gather_state, g_in, g_out, g_sems)

    seq = pl.BlockSpec((hs, cs * c, n), lambda j: (0, j, 0))
    mat = pl.BlockSpec((hs, cs, n, n), lambda j: (0, j, 0, 0))
    res = pl.pallas_call(
        state_body, name=name + "_state", grid=(nsteps,), in_specs=[mat, mat, seq, seq] + [_ANY] * ng2,
        out_specs=[seq, mat] + [_ANY] * ng2,
        out_shape=[jax.ShapeDtypeStruct((h, t, n), F32), jax.ShapeDtypeStruct((h, nchunk, n, n), F32)]
        + [_job_out_shape(g) for g in gather_state],
        scratch_shapes=[pltpu.VMEM((hs, n, n), F32)] + _jobs_scratch(gather_state), compiler_params=_cparams(("arbitrary",)),
    )(gm, um, r2, y0, *_job_arrays(gather_state))
    return res[0], (res[1], gm, r2), gathered + list(res[2:])


def _wkv_bwd(r, lw, k, v, kk, b, saved, dy, *, name, jobs_state=(), jobs_local=lambda state_results: ()):
    states, gm, r2 = saved
    h, t, n = r.shape
    c = WKV_CHUNK
    nchunk = t // c
    (hb, _), (hs, cs) = _wkv_blocks(h, nchunk)
    cb = _tile(nchunk, (8, 4, 2, 1))
    nsteps = nchunk // cs

    ns_ = len(jobs_state)

    def state_body(*refs):
        g_ref, r2_ref, st_ref, dy_ref = refs[:4]
        s_in, (dg_ref, du_ref, dr2_ref) = refs[4:4 + ns_], refs[4 + ns_:7 + ns_]
        s_out, ds_ref, s_sems = refs[7 + ns_:7 + 2 * ns_], refs[7 + 2 * ns_], refs[8 + 2 * ns_:]

        @pl.when(pl.program_id(0) == 0)
        def _():
            ds_ref[...] = jnp.zeros_like(ds_ref)
            _jobs_start(jobs_state, s_in, s_out, s_sems)

        ds = [ds_ref[i] for i in range(hs)]
        for j in reversed(range(cs)):
            rows = pl.ds(j * c, c)
            for i in range(hs):
                s0 = st_ref[i, j]
                du_ref[i, j] = ds[i]
                dg_ref[i, j] = _dot3(s0, ds[i], 0, 0)
                dr2_ref[i, rows] = _dot3(dy_ref[i, rows], s0, 1, 0)
            ds = [_dot3(dy_ref[i, rows], r2_ref[i, rows], 0, 0) + _dot3(ds[i], g_ref[i, j], 1, 1) for i in range(hs)]
        for i in range(hs):
            ds_ref[i] = ds[i]
        if ns_:
            @pl.when(pl.program_id(0) == nsteps - 1)
            def _():
                _jobs_finish(jobs_state, s_in, s_out, s_sems)

    seq = pl.BlockSpec((hs, cs * c, n), lambda j: (0, nsteps - 1 - j, 0))
    mat = pl.BlockSpec((hs, cs, n, n), lambda j: (0, nsteps - 1 - j, 0, 0))
    res = pl.pallas_call(
        state_body, name=name + "_state", grid=(nsteps,), in_specs=[mat, seq, mat, seq] + [_ANY] * ns_,
        out_specs=[mat, mat, seq] + [_ANY] * ns_,
        out_shape=[jax.ShapeDtypeStruct((h, nchunk, n, n), F32)] * 2 + [jax.ShapeDtypeStruct((h, t, n), F32)]
        + [_job_out_shape(j) for j in jobs_state],
        scratch_shapes=[pltpu.VMEM((hs, n, n), F32)] + _jobs_scratch(jobs_state), compiler_params=_cparams(("arbitrary",)),
    )(gm, r2, states, dy, *_job_arrays(jobs_state))
    dg, du, dr2 = res[:3]
    jobs = list(jobs_local(list(res[3:])))

    pairs = [(i, j) for i in range(hb) for j in range(cb)]
    nj_ = len(jobs)
    ni, nj = h // hb, nchunk // cb

    def local_body(*refs):
        ins, (dg_ref, du_ref, dr2_ref, dy_ref) = refs[:6], refs[6:10]
        j_in, out_refs, j_out, j_sems = refs[10:10 + nj_], refs[10 + nj_:16 + nj_], refs[16 + nj_:16 + 2 * nj_], refs[16 + 2 * nj_:]
        if nj_:
            @pl.when((pl.program_id(0) == 0) & (pl.program_id(1) == 0))
            def _():
                _jobs_start(jobs, j_in, j_out, j_sems)

        _, vjp = jax.vjp(_wkv_local, *[[ref[i, pl.ds(j * c, c)] for i, j in pairs] for ref in ins])
        grads = vjp(([dg_ref[i, j] for i, j in pairs], [du_ref[i, j] for i, j in pairs],
                     [dr2_ref[i, pl.ds(j * c, c)] for i, j in pairs], [dy_ref[i, pl.ds(j * c, c)] for i, j in pairs]))
        for o_ref, gr in zip(out_refs, grads):
            for idx, (i, j) in enumerate(pairs):
                o_ref[i, pl.ds(j * c, c)] = gr[idx]
        if nj_:
            @pl.when((pl.program_id(0) == ni - 1) & (pl.program_id(1) == nj - 1))
            def _():
                _jobs_finish(jobs, j_in, j_out, j_sems)

    seq = pl.BlockSpec((hb, cb * c, n), lambda i, j: (i, j, 0))
    mat = pl.BlockSpec((hb, cb, n, n), lambda i, j: (i, j, 0, 0))
    res = pl.pallas_call(
        local_body, name=name + "_local", grid=(ni, nj), in_specs=[seq] * 6 + [mat, mat, seq, seq] + [_ANY] * nj_,
        out_specs=[seq] * 6 + [_ANY] * nj_,
        out_shape=[jax.ShapeDtypeStruct((h, t, n), F32)] * 6 + [_job_out_shape(j) for j in jobs],
        scratch_shapes=_jobs_scratch(jobs),
        compiler_params=_cparams(("arbitrary", "arbitrary") if nj_ else ("parallel", "parallel")),
    )(r, lw, k, v, kk, b, dg, du, dr2, dy, *_job_arrays(jobs))
    return list(res[:6]), list(res[6:])


class _Dims:
    pass


def _make_dims(x, p, w):
    m = _Dims()
    m.t, m.d = x.shape[-2], x.shape[-1]
    m.nl = w["ln_mix"].shape[0]
    m.dl = w["conv_a_b"].shape[1]
    m.hl = w["lru_wx"].shape[1]
    m.dr = w["rwkv_w0"].shape[1]
    m.h, m.n = w["rwkv_rk"].shape[1], w["rwkv_rk"].shape[2]
    m.lw, m.la, m.lg, m.lv = (w[k].shape[1] for k in ("rwkv_w2", "rwkv_a2", "rwkv_g2", "rwkv_v2"))
    m.nsh = w["mu_shift"].shape[1]
    m.ff = w["conv_f_b"].shape[1]
    m.ple = p.shape[-1]
    m.din = 2 * m.dl + m.nsh
    m.lz = _round_up(m.lw + m.la + m.lg + m.lv, LANES_V7X)
    m.zw = _round_up(2 * m.dl + 3 * m.dr + m.lz, 512)
    m.zs = m.zw - 2 * m.dl
    m.tr = _tile(m.t, (256, 128, 64, 32, 16, 8))
    m.trb = _tile(m.t, (128, 64, 32, 16, 8))
    m.tcs = _tile(m.zs, (512, 256, 128))
    m.dw_shards = 0
    assert (3 * m.dr) % m.lz == 0 and (2 * m.dl) % m.tcs == 0 and m.t % WKV_CHUNK == 0
    assert m.nsh == 3 * m.dr + m.lw + m.la + m.lg
    return m


def _to_heads(m, a):
    return jnp.transpose(a.reshape(m.t, m.h, m.n), (1, 0, 2))


def _from_heads(m, a):
    return jnp.transpose(a, (1, 0, 2)).reshape(m.t, m.dr)


def _norm_fwd(m, h, g, name):
    return _stage_fwd(_f_norm, [(h, m.d, 0)], [(g, 0)], [], [(m.d, BF16)], axis=0, tile=m.tr, rows=m.t, name=name)[0]


def _norm_bwd(m, h, g, du, dres, name):
    return _stage_bwd(_f_norm_res, [(h, m.d, 0)], [(g, 0)], [], [(du, m.d, 0), (dres, m.d, 0)], [(0, m.d, F32)],
                      axis=0, tile=m.tr, rows=m.t, name=name)


def _rwkv_pre_operands(m, w, i, sv, v_first_zs, with_r):
    zs = sv["zs"]
    tiled = ([(zs, m.dr, 0)] if with_r else []) + [(zs, m.dr, 1), (zs, m.dr, 2), (zs, m.lz, 3 * m.dr // m.lz)]
    params = [(w["rwkv_w0"][i:i + 1], 0), (w["w2p"][i], 0), (w["rwkv_a0"][i:i + 1], 0), (w["a2p"][i], 0),
              (w["g2p"][i], 0), (w["rwkv_kk"][i:i + 1], 0), (w["rwkv_ka"][i:i + 1], 0)]
    if i > 0:
        tiled.append((v_first_zs, m.dr, 2))
        params += [(w["rwkv_v0"][i - 1:i], 0), (w["v2p"][i - 1], 0)]
    return tiled, params


def _rwkv_post_operands(m, w, i, sv):
    tiled = [(sv["y"], m.dr, 0), (sv["zs"], m.dr, 0), (sv["k2"], m.dr, 0), (sv["v2"], m.dr, 0), (sv["g"], m.dr, 0)]
    params = [(w["rwkv_lnx_w"][i:i + 1], 0), (w["rwkv_lnx_b"][i:i + 1], 0), (w["rk"][i], 0)]
    return tiled, params


def _lru_gate_params(w, i):
    return [(w["lru_wx"][i], 0), (w["lru_wa"][i], 0), (w["lru_bx"][i:i + 1], 0), (w["lru_ba"][i:i + 1], 0),
            (w["lru_lambda"][i:i + 1], 0)]


_COLUMN_SHARDED_OPERANDS = ("w_gate", "w_up", "w_ple_proj")


class _WeightFeed:
    def __init__(self, m, w, shards, vres):
        self.m, self.w, self.shards, self.vres = m, w, shards, vres

    def keys(self, carrier, i):
        plan = {"mm_in": [("w_o", i)] if i == 0 else [],
                "wkv_local": [("w_gate", i), ("w_up", i)], "wkv_state": [("w_down", i)],
                "mm_gate": [("w_ple_gate", i)], "mm_up": [("w_ple_proj", i), ("w_o", i + 1)],
                "mm_down": [("w_in", i + 1)], "mm_pgate": []}
        return [key for key in plan[carrier] if key[1] < self.m.nl]

    def blobs(self, keys):
        return [_gather_blob(self.shards[name][layer]) for name, layer in keys]

    def arrive(self, keys, gathered):
        m = self.m
        for (name, layer), got in zip(keys, gathered):
            full = got.reshape((N_XY,) + self.shards[name][layer].shape)
            if name in _COLUMN_SHARDED_OPERANDS:
                self.w[name][layer] = full
                continue
            full = _from_shards(full, _SHARD_AXIS[name] - 1)
            if name == "w_in":
                vres = self.vres[layer - 1] if layer > 0 else jnp.zeros((m.d, m.lv), BF16)
                self.w["wcat"][layer] = jnp.concatenate([full, vres, jnp.zeros((m.d, m.zw - m.din - m.lv), BF16)], axis=1)
            else:
                self.w[name][layer] = full


def _mm_fed(feed, carrier, i, a, b, **kw):
    keys = feed.keys(carrier, i) if feed is not None else []
    if not keys:
        return _mm(a, b, **kw)
    out, got = _mm(a, b, gather=feed.blobs(keys), **kw)
    feed.arrive(keys, got)
    return out


def _layer_fwd(m, w, i, h, p_bf, v_first_zs, feed=None):
    sv = {"h": h}
    t, dl, dr = m.t, m.dl, m.dr
    sv["u1"] = _norm_fwd(m, h, w["ln_mix"][i:i + 1], "norm_mix")
    z = sv["z"] = _mm_fed(feed, "mm_in", i, sv["u1"], w["wcat"][i], name="mm_in")
    off = 2 * dl // m.tcs
    sv["zs"] = _stage_fwd(_f_shiftmix, [(z, None, off)], [(w["mu_pad"][i], off)], [], [(m.zs, F32)],
                          axis=1, tile=m.tcs, rows=t, name="shiftmix")[0]
    tca = _tile(dl, (512, 256, 128))
    sv["xb"] = _stage_fwd(_f_conv, [(z, None, 0)], [(w["conv_a_w"][i], 0), (w["conv_a_b"][i:i + 1], 0)], [],
                          [(dl, F32)], axis=1, tile=tca, rows=t, name="conv_a")[0]
    sv["a"], b_in = _stage_fwd(_make_f_lru_gates(m.hl), [(sv["xb"], dl, 0)], _lru_gate_params(w, i), [],
                               [(dl, F32), (dl, F32)], axis=0, tile=m.tr, rows=t, name="lru_gates")
    sv["hl"] = _lru_scan(sv["a"], b_in, name="lru_scan")
    out_a = _stage_fwd(_f_lru_out, [(sv["hl"], dl, 0), (z, dl, 1)], [(w["lru_norm"][i:i + 1], 0)], [],
                       [(dl, BF16)], axis=0, tile=m.tr, rows=t, name="lru_out")[0]
    tiled, params = _rwkv_pre_operands(m, w, i, sv, v_first_zs, False)
    pre = _stage_fwd(_make_f_rwkv_pre(i > 0), tiled, params, [w["bb"]], [(dr, F32)] * 6,
                     axis=0, tile=m.tr, rows=t, name="rwkv_pre")
    sv["logw"], sv["k2"], sv["v2"], sv["kk"], sv["b"], sv["g"] = pre
    heads = [_to_heads(m, a) for a in (sv["zs"][:, :dr], sv["logw"], sv["k2"], sv["v2"], sv["kk"], sv["b"])]
    keys = [feed.keys(carrier, i) if feed is not None else [] for carrier in ("wkv_local", "wkv_state")]
    y_h, sv["states"], got = _wkv_fwd(*heads, name="wkv_fwd", gather=feed.blobs(keys[0]) if keys[0] else (),
                                      gather_state=feed.blobs(keys[1]) if keys[1] else ())
    if keys[0] or keys[1]:
        feed.arrive(keys[0] + keys[1], got)
    sv["y"] = _from_heads(m, y_h)
    tiled, params = _rwkv_post_operands(m, w, i, sv)
    out_b = _stage_fwd(_make_f_rwkv_post(m.n), tiled, params, [w["bb"]], [(dr, BF16)],
                       axis=0, tile=m.tr, rows=t, name="rwkv_post")[0]
    sv["cat"] = jnp.concatenate([out_a, out_b], axis=1)
    h2 = sv["h2"] = _mm(sv["cat"], w["w_o"][i], res=h, name="mm_o")
    sv["u2"] = _norm_fwd(m, h2, w["ln_ffn"][i:i + 1], "norm_ffn")
    sv["gpre"] = _mm_fed(feed, "mm_gate", i, sv["u2"], w["w_gate"][i], name="mm_gate")
    sv["up"] = _mm_fed(feed, "mm_up", i, sv["u2"], w["w_up"][i], name="mm_up")
    tcf = _tile(m.ff, (512, 256, 128))
    sv["act"] = _stage_fwd(_f_ffn_act, [(sv["gpre"], None, 0), (sv["up"], None, 0)],
                           [(w["conv_f_w"][i], 0), (w["conv_f_b"][i:i + 1], 0)], [], [(m.ff, BF16)],
                           axis=1, tile=tcf, rows=t, name="ffn_act")[0]
    h3 = sv["h3"] = _mm_fed(feed, "mm_down", i, sv["act"], w["w_down"][i], res=h2, name="mm_down")
    sv["u3"] = _norm_fwd(m, h3, w["ln_ple"][i:i + 1], "norm_ple")
    sv["eg"] = _mm_fed(feed, "mm_pgate", i, sv["u3"], w["w_ple_gate"][i], name="mm_pgate")
    sv["ep"] = _mm(p_bf, w["w_ple_proj"][i], name="mm_pproj")
    h4 = _stage_fwd(_f_ple, [(h3, m.d, 0), (sv["eg"], m.d, 0), (sv["ep"], m.d, 0)], [(w["ln_ple_post"][i:i + 1], 0)],
                    [], [(m.d, F32)], axis=0, tile=m.tr, rows=t, name="ple")[0]
    return h4, sv


def _layer_bwd(m, w, i, dh4, sv, p_bf, v_first_zs, dvf_in, pending=None, early=None):
    t, d, dl, dr = m.t, m.d, m.dl, m.dr
    g = {}
    deg, dep, g["ln_ple_post"] = _stage_bwd(
        _f_ple, [(sv["h3"], d, 0), (sv["eg"], d, 0), (sv["ep"], d, 0)], [(w["ln_ple_post"][i:i + 1], 0)], [],
        [(dh4, d, 0)], [(1, d, BF16), (2, d, BF16)], axis=0, tile=m.tr, rows=t, name="ple_bwd")
    du3 = _mm(deg, w["w_ple_gate"][i], tb=True, name="mm_pgate_dx")
    g["w_ple_gate"] = _mm(sv["u3"], deg, ta=True, name="mm_pgate_dw")
    g["w_ple_proj"] = _mm(p_bf, dep, ta=True, name="mm_pproj_dw", out_shards=m.dw_shards)
    dh3, g["ln_ple"] = _norm_bwd(m, sv["h3"], w["ln_ple"][i:i + 1], du3, dh4, "norm_ple_bwd")
    dh3_bf = dh3.astype(BF16)
    dact = _mm(dh3_bf, w["w_down"][i], tb=True, name="mm_down_dx")
    if pending is None:
        g["w_down"] = _mm(sv["act"], dh3_bf, ta=True, name="mm_down_dw")
    else:
        g["w_down"], gots = _mm(sv["act"], dh3_bf, ta=True, name="mm_down_dw", gather=pending.send_jobs())
        pending.after_send(gots)
    tcf = _tile(m.ff, (512, 256, 128))
    dgpre, dup, g["conv_f_w"], g["conv_f_b"] = _stage_bwd(
        _f_ffn_act, [(sv["gpre"], None, 0), (sv["up"], None, 0)], [(w["conv_f_w"][i], 0), (w["conv_f_b"][i:i + 1], 0)],
        [], [(dact, 0)], [(0, m.ff, BF16), (1, m.ff, BF16)], axis=1, tile=tcf, rows=t, ncols=m.ff, name="ffn_act_bwd")
    du2 = _mm(dgpre, w["w_gate"][i], tb=True, name="mm_gate_dx")
    du2 = _mm(dup, w["w_up"][i], tb=True, res=du2, name="mm_up_dx")
    g["w_gate"] = _mm(sv["u2"], dgpre, ta=True, name="mm_gate_dw", out_shards=m.dw_shards)
    g["w_up"] = _mm(sv["u2"], dup, ta=True, name="mm_up_dw", out_shards=m.dw_shards)
    dh2, g["ln_ffn"] = _norm_bwd(m, sv["h2"], w["ln_ffn"][i:i + 1], du2, dh3, "norm_ffn_bwd")
    dh2_bf = dh2.astype(BF16)
    dcat = _mm(dh2_bf, w["w_o"][i], tb=True, name="mm_o_dx")
    g["w_o"] = _mm(sv["cat"], dh2_bf, ta=True, name="mm_o_dw")
    tiled, params = _rwkv_post_operands(m, w, i, sv)
    dy, dr_a, dk2_a, dv2_a, dg, g["rwkv_lnx_w"], g["rwkv_lnx_b"], g["rk"] = _stage_bwd(
        _make_f_rwkv_post(m.n), tiled, params, [w["bb"]], [(dcat, dr, dl // dr)], [(j, dr, F32) for j in range(5)],
        axis=0, tile=m.trb, rows=t, name="rwkv_post_bwd")
    heads = [_to_heads(m, a) for a in (sv["zs"][:, :dr], sv["logw"], sv["k2"], sv["v2"], sv["kk"], sv["b"])]
    own = early(g) if early is not None else None

    def local_jobs(state_results):
        jobs = []
        if own is not None:
            own.after_send(state_results)
            jobs += own.exchange_jobs()
        if pending is not None:
            jobs += pending.exchange_jobs()
        return jobs

    dwkv, parts = _wkv_bwd(*heads, sv["states"], _to_heads(m, dy), name="wkv_bwd",
                           jobs_state=own.send_jobs() if own is not None else (), jobs_local=local_jobs)
    n_own = len(own.names) if own is not None else 0
    if own is not None:
        own.after_exchange(parts[:n_own])
    if pending is not None:
        pending.after_exchange(parts[n_own:])
    dr_b, dlw, dk2_b, dv2_b, dkk, db = [_from_heads(m, a) for a in dwkv]
    tiled, params = _rwkv_pre_operands(m, w, i, sv, v_first_zs, True)
    v_cots = [dv2_a, dv2_b] + ([dvf_in] if dvf_in is not None else [])
    cots = [(c, dr, 0) for c in [dr_a, dr_b, dlw, dk2_a, dk2_b] + v_cots + [dkk, db, dg]]
    ntil = len(tiled)
    dtiled = [(0, dr, F32), (1, dr, F32), (2, dr, F32), (3, m.lz, F32)] + ([(4, dr, F32)] if i > 0 else [])
    res = _stage_bwd(_make_f_rwkv_pre(i > 0, len(v_cots)), tiled, params, [w["bb"]], cots, dtiled,
                     axis=0, tile=m.trb, rows=t, name="rwkv_pre_bwd")
    d_r, d_k, d_v, d_lz = res[:4]
    dvf_out = res[4] if i > 0 else None
    pg = res[ntil:]
    g["rwkv_w0"], g["w2p"], g["rwkv_a0"], g["a2p"], g["g2p"], g["rwkv_kk"], g["rwkv_ka"] = pg[:7]
    if i > 0:
        g["rwkv_v0"], g["v2p"] = pg[7:9]
    dzs = jnp.concatenate([d_r, d_k, d_v, d_lz, jnp.zeros((t, m.zs - 3 * dr - m.lz), F32)], axis=1)
    off = 2 * dl // m.tcs
    dzr, g["mu_pad"] = _stage_bwd(_f_shiftmix, [(sv["z"], None, off)], [(w["mu_pad"][i], off)], [], [(dzs, 0)],
                                  [(0, m.zs, BF16)], axis=1, tile=m.tcs, rows=t, ncols=m.zs, name="shiftmix_bwd")
    dhl, dya, g["lru_norm"] = _stage_bwd(
        _f_lru_out, [(sv["hl"], dl, 0), (sv["z"], dl, 1)], [(w["lru_norm"][i:i + 1], 0)], [], [(dcat, dl, 0)],
        [(0, dl, F32), (1, dl, BF16)], axis=0, tile=m.tr, rows=t, name="lru_out_bwd")
    da, db_in = _lru_scan_bwd(sv["a"], sv["hl"], dhl, name="lru_scan_bwd")
    dxb, g["lru_wx"], g["lru_wa"], g["lru_bx"], g["lru_ba"], g["lru_lambda"] = _stage_bwd(
        _make_f_lru_gates(m.hl), [(sv["xb"], dl, 0)], _lru_gate_params(w, i), [], [(da, dl, 0), (db_in, dl, 0)],
        [(0, dl, F32)], axis=0, tile=m.tr, rows=t, name="lru_gates_bwd")
    tca = _tile(dl, (512, 256, 128))
    dxa, g["conv_a_w"], g["conv_a_b"] = _stage_bwd(
        _f_conv, [(sv["z"], None, 0)], [(w["conv_a_w"][i], 0), (w["conv_a_b"][i:i + 1], 0)], [], [(dxb, 0)],
        [(0, dl, BF16)], axis=1, tile=tca, rows=t, ncols=dl, name="conv_a_bwd")
    dz = jnp.concatenate([dxa, dya, dzr], axis=1)
    du1 = _mm(dz, w["wcat"][i], tb=True, name="mm_in_dx")
    g["wcat"] = _mm(sv["u1"], dz, ta=True, name="mm_in_dw")
    dh, g["ln_mix"] = _norm_bwd(m, sv["h"], w["ln_mix"][i:i + 1], du1, dh2, "norm_mix_bwd")
    return dh, g, dvf_out, own


def _loss_head(m, h, g, tgt):
    tile, d = m.tr, m.d

    def body(h_ref, g_ref, t_ref, loss_ref, dh_ref, dg_ref):
        def f(hv, gv):
            err = _rms(hv, gv) - t_ref[...]
            return 0.5 * jnp.sum(jnp.mean(err * err, axis=-1))

        val, vjp = jax.vjp(f, h_ref[...], g_ref[...])
        dh, dg = vjp(jnp.ones((), F32))
        dh_ref[...] = dh

        @pl.when(pl.program_id(0) == 0)
        def _():
            dg_ref[...] = jnp.zeros_like(dg_ref)
            loss_ref[...] = jnp.zeros_like(loss_ref)

        dg_ref[...] += dg
        loss_ref[...] += jnp.full(loss_ref.shape, val, F32)

    row = pl.BlockSpec((tile, d), lambda i: (i, 0))
    return pl.pallas_call(
        body, name="loss_head", grid=(m.t // tile,),
        in_specs=[row, pl.BlockSpec((1, d), lambda i: (0, 0)), row],
        out_specs=[pl.BlockSpec((1, LANES_V7X), lambda i: (0, 0)), row, pl.BlockSpec((1, d), lambda i: (0, 0))],
        out_shape=[jax.ShapeDtypeStruct((1, LANES_V7X), F32), jax.ShapeDtypeStruct((m.t, d), F32),
                   jax.ShapeDtypeStruct((1, d), F32)],
        compiler_params=_cparams(("arbitrary",)),
    )(h, g, tgt)


def _local_step(m, w, x, p, tgt, feed=None, reducer=None):
    h = x
    saved = []
    p_bf = p.astype(BF16)
    for i in range(m.nl):
        h, sv = _layer_fwd(m, w, i, h, p_bf[i], saved[0]["zs"] if i > 0 else None, feed)
        saved.append(sv)
    loss_row, dh, d_ln_final = _loss_head(m, h, w["ln_final"], tgt)
    grads = [None] * m.nl
    reductions = [None] * m.nl
    dvf = None
    for i in reversed(range(m.nl)):
        pending = reductions[i + 1][1] if reducer is not None and i + 1 < m.nl else None
        early = functools.partial(reducer, i, which="early") if reducer is not None else None
        dh, grads[i], dvf_i, own = _layer_bwd(m, w, i, dh, saved[i], p_bf[i], saved[0]["zs"] if i > 0 else None,
                                              dvf if i == 0 else None, pending, early)
        if reducer is not None:
            reductions[i] = (own, reducer(i, grads[i], which="late"))
        if i > 0:
            dvf = dvf_i if dvf is None else dvf + dvf_i
    return loss_row, dh, grads, d_ln_final, reductions


_BIG = ("w_o", "w_gate", "w_up", "w_down", "w_ple_gate", "w_ple_proj")


def _lora_rows(m):
    o1 = m.lw
    o2 = o1 + m.la
    o3 = o2 + m.lg
    return {"w2p": (0, o1), "a2p": (o1, o2), "g2p": (o2, o3), "v2p": (o3, o3 + m.lv)}


def _prepare_weights(m, wf):
    w = {k: v for k, v in wf.items() if k not in _BIG and k not in ("w_in", "w_in_vres")}
    nl = m.nl
    for k in _BIG:
        w[k] = [wf[k][i].astype(BF16) for i in range(nl)] if k in wf else [None] * nl
    w["wcat"] = [None] * nl
    if "w_in" in wf:
        vres = jnp.concatenate([jnp.zeros((1, m.d, m.lv), BF16), wf["w_in_vres"].astype(BF16)], axis=0)
        pad = jnp.zeros((m.d, m.zw - m.din - m.lv), BF16)
        w["wcat"] = [jnp.concatenate([wf["w_in"][i].astype(BF16), vres[i], pad], axis=1) for i in range(nl)]
    mu_v = jnp.concatenate([jnp.zeros((1, m.lv), F32), wf["mu_shift_vres"]], axis=0)
    w["mu_pad"] = jnp.concatenate([jnp.zeros((nl, 2 * m.dl), F32), wf["mu_shift"], mu_v,
                                   jnp.zeros((nl, m.zw - m.din - m.lv), F32)], axis=1)[:, None, :]
    rows = _lora_rows(m)
    for name, src in (("w2p", "rwkv_w2"), ("a2p", "rwkv_a2"), ("g2p", "rwkv_g2"), ("v2p", "rwkv_v2")):
        lo, hi = rows[name]
        a = wf[src]
        w[name] = jnp.concatenate([jnp.zeros((a.shape[0], lo, m.dr), F32), a, jnp.zeros((a.shape[0], m.lz - hi, m.dr), F32)],
                                  axis=1)
    w["rk"] = wf["rwkv_rk"].reshape(nl, 1, m.dr)
    w["ln_final"] = wf["ln_final"].reshape(1, m.d)
    group = max(m.n, LANES_V7X)
    assert group % m.n == 0 and m.dr % group == 0
    head = jnp.arange(group, dtype=jnp.int32) // m.n
    w["bb"] = (head[:, None] == head[None, :]).astype(BF16)
    return w


def _unpack_grads(m, grads, d_ln_final, with_big=True):
    nl = m.nl
    out = {}

    def stack(key):
        return jnp.stack([grads[i][key] for i in range(nl)], axis=0)

    for k in (_BIG if with_big else ()) + ("conv_a_w", "conv_f_w", "lru_wx", "lru_wa"):
        out[k] = stack(k)
    for k in ("ln_mix", "conv_a_b", "lru_bx", "lru_ba", "lru_lambda", "lru_norm", "rwkv_w0", "rwkv_a0", "rwkv_kk",
              "rwkv_ka", "rwkv_lnx_w", "rwkv_lnx_b", "ln_ffn", "conv_f_b", "ln_ple", "ln_ple_post"):
        out[k] = stack(k)[:, 0, :]
    if with_big:
        out["w_in"] = stack("wcat")[:, :, :m.din]
    out["w_in_vres"] = jnp.stack([grads[i]["wcat"][:, m.din:m.din + m.lv] for i in range(1, nl)], axis=0)
    mu = stack("mu_pad")[:, 0, :]
    out["mu_shift"] = mu[:, :m.nsh]
    out["mu_shift_vres"] = mu[1:, m.nsh:m.nsh + m.lv]
    rows = _lora_rows(m)
    for name, dst in (("w2p", "rwkv_w2"), ("a2p", "rwkv_a2"), ("g2p", "rwkv_g2")):
        lo, hi = rows[name]
        out[dst] = stack(name)[:, lo:hi, :]
    lo, hi = rows["v2p"]
    out["rwkv_v2"] = jnp.stack([grads[i]["v2p"] for i in range(1, nl)], axis=0)[:, lo:hi, :]
    out["rwkv_v0"] = jnp.stack([grads[i]["rwkv_v0"] for i in range(1, nl)], axis=0)[:, 0, :]
    out["rwkv_rk"] = stack("rk").reshape(nl, m.h, m.n)
    out["ln_final"] = d_ln_final.reshape(m.d)
    return out


_ANY = pl.BlockSpec(memory_space=pl.ANY)


def _position():
    return lax.axis_index("x"), lax.axis_index("y"), lax.axis_index("c")


def _other_chips(x, y):
    return [(1 - x, y), (x, 1 - y), (1 - x, 1 - y)]


def _gather_blob(shard):
    rows, wd = shard.shape
    return shard.reshape(2, rows // 2, wd)


_JOB_SEMS = {"gather": 7, "pair_send": 1, "exchange": 3}


def _job_parts(job):
    return job if isinstance(job, tuple) else ("gather", job)


def _job_arrays(jobs):
    return [_job_parts(j)[1] for j in jobs]


def _job_out_shape(job):
    kind, arr = _job_parts(job)
    if kind == "gather":
        return jax.ShapeDtypeStruct((N_XY,) + arr.shape, arr.dtype)
    if kind == "pair_send":
        return jax.ShapeDtypeStruct((arr.shape[0], arr.shape[1] // 2, arr.shape[2]), arr.dtype)
    return jax.ShapeDtypeStruct(arr.shape, arr.dtype)


def _jobs_scratch(jobs):
    out = []
    for j in jobs:
        n = _JOB_SEMS[_job_parts(j)[0]]
        out += [pltpu.SemaphoreType.DMA((n,)), pltpu.SemaphoreType.DMA((n,))]
    return out


def _jobs_start(jobs, in_refs, out_refs, sems):
    for q, j in enumerate(jobs):
        _JOB_START[_job_parts(j)[0]](in_refs[q], out_refs[q], sems[2 * q], sems[2 * q + 1])


def _jobs_finish(jobs, in_refs, out_refs, sems):
    for q, j in enumerate(jobs):
        _JOB_FINISH[_job_parts(j)[0]](in_refs[q], out_refs[q], sems[2 * q], sems[2 * q + 1])


def _run_jobs(jobs, *, name):
    n = len(jobs)

    def body(*refs):
        _jobs_start(jobs, refs[:n], refs[n:2 * n], refs[2 * n:])
        _jobs_finish(jobs, refs[:n], refs[n:2 * n], refs[2 * n:])

    return pl.pallas_call(body, name=name, in_specs=[_ANY] * n, out_specs=[_ANY] * n,
                          out_shape=[_job_out_shape(j) for j in jobs], scratch_shapes=_jobs_scratch(jobs))(*_job_arrays(jobs))


def _gather_copies(in_ref, out_ref, send_sems, recv_sems):
    x, y, c = _position()
    me = 2 * x + y
    sends, hands, ici_in, d2d_in = [], [], [], []
    for k, (px, py) in enumerate(_other_chips(x, y)):
        landed = out_ref.at[2 * px + py, c]
        sends.append(pltpu.make_async_remote_copy(
            src_ref=in_ref.at[c], dst_ref=out_ref.at[me, c], send_sem=send_sems.at[k], recv_sem=recv_sems.at[k],
            device_id=(px, py, c), device_id_type=MESH))
        ici_in.append(pltpu.make_async_remote_copy(
            src_ref=in_ref.at[c], dst_ref=landed, send_sem=send_sems.at[k], recv_sem=recv_sems.at[k],
            device_id=(px, py, c), device_id_type=MESH))
        hands.append(pltpu.make_async_remote_copy(
            src_ref=landed, dst_ref=landed, send_sem=send_sems.at[3 + k], recv_sem=recv_sems.at[3 + k],
            device_id=(x, y, 1 - c), device_id_type=MESH))
        d2d_in.append(pltpu.make_async_remote_copy(
            src_ref=in_ref.at[c], dst_ref=out_ref.at[2 * px + py, 1 - c], send_sem=send_sems.at[3 + k],
            recv_sem=recv_sems.at[3 + k], device_id=(x, y, 1 - c), device_id_type=MESH))
    own = pltpu.make_async_remote_copy(src_ref=in_ref, dst_ref=out_ref.at[me], send_sem=send_sems.at[6],
                                       recv_sem=recv_sems.at[6], device_id=(x, y, 1 - c), device_id_type=MESH)
    sends.append(own)
    d2d_in.append(own)
    return sends, hands, ici_in, d2d_in


def _gather_start(in_ref, out_ref, send_sems, recv_sems):
    for cp in _gather_copies(in_ref, out_ref, send_sems, recv_sems)[0]:
        cp.start()


def _gather_finish(in_ref, out_ref, send_sems, recv_sems):
    sends, hands, ici_in, d2d_in = _gather_copies(in_ref, out_ref, send_sems, recv_sems)
    for arrived, hand in zip(ici_in, hands):
        arrived.wait_recv()
        hand.start()
    for arrived in d2d_in:
        arrived.wait_recv()
    for cp in sends + hands:
        cp.wait_send()


def _pair_send_copy(g_ref, out_ref, send_sems, recv_sems):
    x, y, c = _position()
    half = out_ref.shape[1]
    return pltpu.make_async_remote_copy(src_ref=g_ref.at[:, pl.ds((1 - c) * half, half), :], dst_ref=out_ref,
                                        send_sem=send_sems.at[0], recv_sem=recv_sems.at[0], device_id=(x, y, 1 - c),
                                        device_id_type=MESH)


def _exchange_copies(in_ref, out_ref, send_sems, recv_sems):
    x, y, c = _position()
    me = 2 * x + y
    sends, arrivals = [], []
    for k, (px, py) in enumerate(_other_chips(x, y)):
        sends.append(pltpu.make_async_remote_copy(
            src_ref=in_ref.at[2 * px + py], dst_ref=out_ref.at[me], send_sem=send_sems.at[k], recv_sem=recv_sems.at[k],
            device_id=(px, py, c), device_id_type=MESH))
        arrivals.append(pltpu.make_async_remote_copy(
            src_ref=in_ref.at[me], dst_ref=out_ref.at[2 * px + py], send_sem=send_sems.at[k], recv_sem=recv_sems.at[k],
            device_id=(px, py, c), device_id_type=MESH))
    return sends, arrivals


def _exchange_start(*refs):
    for cp in _exchange_copies(*refs)[0]:
        cp.start()


def _exchange_finish(*refs):
    sends, arrivals = _exchange_copies(*refs)
    for cp in arrivals:
        cp.wait_recv()
    for cp in sends:
        cp.wait_send()


_JOB_START = {"gather": _gather_start, "pair_send": lambda *refs: _pair_send_copy(*refs).start(), "exchange": _exchange_start}
_JOB_FINISH = {"gather": _gather_finish, "pair_send": lambda *refs: _pair_send_copy(*refs).wait(),
               "exchange": _exchange_finish}


def _pair_sum(g, got, pos, *, name):
    nq, r, wd = g.shape
    half = r // 2
    tr = _tile(half, (256, 128, 64, 32, 16, 8))
    nb = half // tr

    def body(c_ref, g_ref, got_ref, o_ref):
        o_ref[...] = (g_ref[...] + got_ref[...]).astype(o_ref.dtype)

    grid_spec = pltpu.PrefetchScalarGridSpec(
        num_scalar_prefetch=1, grid=(nq, nb),
        in_specs=[pl.BlockSpec((1, tr, wd), lambda q, j, c_ref: (q, c_ref[0] * nb + j, 0)),
                  pl.BlockSpec((1, tr, wd), lambda q, j, c_ref: (q, j, 0))],
        out_specs=pl.BlockSpec((1, tr, wd), lambda q, j, c_ref: (q, j, 0)))
    return pl.pallas_call(body, name=name, grid_spec=grid_spec, out_shape=jax.ShapeDtypeStruct((nq, half, wd), BF16),
                          compiler_params=_cparams(("arbitrary", "arbitrary")))(pos[0], g, got)


def _chip_sum(parts, pb, pos, *, name):
    nq, half, wd = parts.shape
    tr = _tile(half, (256, 128, 64, 32, 16, 8))
    nb = half // tr

    def body(c_ref, x_ref, y_ref, p_ref, own_ref, o_ref):
        chip = 2 * x_ref[0] + y_ref[0]
        own = own_ref[0].astype(F32)
        acc = None
        for q in range(nq):
            term = jnp.where(chip == q, own, p_ref[q].astype(F32))
            acc = term if acc is None else acc + term
        o_ref[...] = acc

    grid_spec = pltpu.PrefetchScalarGridSpec(
        num_scalar_prefetch=3, grid=(nb,),
        in_specs=[pl.BlockSpec((nq, tr, wd), lambda j, c_ref, x_ref, y_ref: (0, j, 0)),
                  pl.BlockSpec((1, tr, wd), lambda j, c_ref, x_ref, y_ref: (2 * x_ref[0] + y_ref[0], j, 0))],
        out_specs=pl.BlockSpec((tr, wd), lambda j, c_ref, x_ref, y_ref: (c_ref[0] * nb + j, 0)))
    return pl.pallas_call(body, name=name, grid_spec=grid_spec, out_shape=jax.ShapeDtypeStruct((2 * half, wd), F32),
                          compiler_params=_cparams(("arbitrary",)))(*pos, parts, pb)


def _pair_gather(full, *, name):
    r, wd = full.shape
    half = r // 2

    def body(in_ref, out_ref, send_sem, recv_sem):
        x, y, c = _position()
        mine = out_ref.at[pl.ds(c * half, half), :]
        cp = pltpu.make_async_remote_copy(src_ref=mine, dst_ref=mine, send_sem=send_sem, recv_sem=recv_sem,
                                          device_id=(x, y, 1 - c), device_id_type=MESH)
        cp.start()
        pltpu.make_async_remote_copy(src_ref=mine, dst_ref=out_ref.at[pl.ds((1 - c) * half, half), :], send_sem=send_sem,
                                     recv_sem=recv_sem, device_id=(x, y, 1 - c), device_id_type=MESH).wait_recv()
        cp.wait_send()

    return pl.pallas_call(
        body, name=name, in_specs=[_ANY], out_specs=_ANY, out_shape=jax.ShapeDtypeStruct(full.shape, full.dtype),
        input_output_aliases={0: 0}, scratch_shapes=[pltpu.SemaphoreType.DMA(()), pltpu.SemaphoreType.DMA(())],
    )(full)


class _GradReduce:
    def __init__(self, slabs, pos, tag):
        self.names, self.slabs, self.pos, self.tag = list(slabs), [slabs[k] for k in slabs], pos, tag
        self.pb = self.out = None

    def send_jobs(self):
        return [("pair_send", g) for g in self.slabs]

    def after_send(self, gots):
        self.pb = [_pair_sum(g, got, self.pos, name=f"rs_pair_sum_{k}_{self.tag}")
                   for k, g, got in zip(self.names, self.slabs, gots)]

    def exchange_jobs(self):
        return [("exchange", pb) for pb in self.pb]

    def after_exchange(self, parts):
        full = [_chip_sum(pt, pb, self.pos, name=f"rs_chip_sum_{k}_{self.tag}") for k, pt, pb in zip(self.names, parts, self.pb)]
        self.out = {k: _pair_gather(f, name=f"rs_pair_gather_{k}_{self.tag}") for k, f in zip(self.names, full)}

    def run(self):
        if self.pb is None:
            self.after_send(_run_jobs(self.send_jobs(), name="rs_pair_send_" + self.tag))
        if self.out is None:
            self.after_exchange(_run_jobs(self.exchange_jobs(), name="rs_exchange_" + self.tag))
        return self.out


def _all_reduce_small(vec, *, name):
    r, wd = vec.shape

    def body(in_ref, out_ref, slots, send_sems, recv_sems):
        x, y, c = _position()
        me = 4 * x + 2 * y + c
        flips = [(fx, fy, fc) for fx in (0, 1) for fy in (0, 1) for fc in (0, 1) if fx + fy + fc]
        peers = [(1 - x if fx else x, 1 - y if fy else y, 1 - c if fc else c) for fx, fy, fc in flips]
        sends = []
        for k, peer in enumerate(peers):
            cp = pltpu.make_async_remote_copy(src_ref=in_ref, dst_ref=slots.at[me], send_sem=send_sems.at[k],
                                              recv_sem=recv_sems.at[k], device_id=peer, device_id_type=MESH)
            cp.start()
            sends.append(cp)
        slots[me] = in_ref[...]
        for k, (px, py, pc) in enumerate(peers):
            pltpu.make_async_remote_copy(src_ref=in_ref, dst_ref=slots.at[4 * px + 2 * py + pc], send_sem=send_sems.at[k],
                                         recv_sem=recv_sems.at[k], device_id=(px, py, pc), device_id_type=MESH).wait_recv()
        for cp in sends:
            cp.wait_send()
        acc = slots[0]
        for q in range(1, N_DEV):
            acc = acc + slots[q]
        out_ref[...] = acc

    vm = pl.BlockSpec(memory_space=pltpu.VMEM)
    return pl.pallas_call(
        body, name=name, in_specs=[vm], out_specs=vm, out_shape=jax.ShapeDtypeStruct((r, wd), F32),
        scratch_shapes=[pltpu.VMEM((N_DEV, r, wd), F32), pltpu.SemaphoreType.DMA((N_DEV - 1,)),
                        pltpu.SemaphoreType.DMA((N_DEV - 1,))],
        compiler_params=_cparams(),
    )(vec)


def _adamw(w, g, m, v, *, name, jobs=()):
    r, wd = w.shape
    tr = _tile(r, (256, 128, 64, 32, 16, 8))
    nj, steps = len(jobs), r // tr

    def body(*refs):
        w_ref, g_ref, m_ref, v_ref = refs[:4]
        j_in, (d_ref, m_out, v_out) = refs[4:4 + nj], refs[4 + nj:7 + nj]
        j_out, j_sems = refs[7 + nj:7 + 2 * nj], refs[7 + 2 * nj:]
        if nj:
            @pl.when(pl.program_id(0) == 0)
            def _():
                _jobs_start(jobs, j_in, j_out, j_sems)

        gv = g_ref[...]
        m_new = ADAM_B1 * m_ref[...] + (1.0 - ADAM_B1) * gv
        v_new = ADAM_B2 * v_ref[...] + (1.0 - ADAM_B2) * (gv * gv)
        m_hat = m_new / (1.0 - ADAM_B1 ** ADAM_STEP)
        v_hat = v_new / (1.0 - ADAM_B2 ** ADAM_STEP)
        d_ref[...] = -ADAM_LR * (m_hat / (jnp.sqrt(v_hat) + ADAM_EPS) + ADAM_WD * w_ref[...])
        m_out[...] = m_new
        v_out[...] = v_new
        if nj:
            @pl.when(pl.program_id(0) == steps - 1)
            def _():
                _jobs_finish(jobs, j_in, j_out, j_sems)

    spec = pl.BlockSpec((tr, wd), lambda j: (j, 0))
    res = pl.pallas_call(body, name=name, grid=(steps,), in_specs=[spec] * 4 + [_ANY] * nj, out_specs=[spec] * 3 + [_ANY] * nj,
                         out_shape=[jax.ShapeDtypeStruct((r, wd), F32)] * 3 + [_job_out_shape(j) for j in jobs],
                         scratch_shapes=_jobs_scratch(jobs), compiler_params=_cparams(("arbitrary",)),
                         )(w, g, m, v, *_job_arrays(jobs))
    return list(res[:3]), list(res[3:])


_WEIGHTS = ("ln_mix", "w_in", "w_in_vres", "mu_shift", "mu_shift_vres", "conv_a_w", "conv_a_b", "lru_wx", "lru_bx", "lru_wa",
            "lru_ba", "lru_lambda", "lru_norm", "rwkv_w0", "rwkv_w2", "rwkv_a0", "rwkv_a2", "rwkv_v0", "rwkv_v2", "rwkv_g2",
            "rwkv_kk", "rwkv_ka", "rwkv_rk", "rwkv_lnx_w", "rwkv_lnx_b", "w_o", "ln_ffn", "w_gate", "w_up", "conv_f_w",
            "conv_f_b", "w_down", "ln_ple", "w_ple_gate", "w_ple_proj", "ln_ple_post", "ln_final")
_SHARD_AXIS = {"w_in": 2, "w_in_vres": 1, "conv_a_w": 2, "lru_wx": 2, "lru_wa": 2, "rwkv_w2": 2, "rwkv_a2": 2, "rwkv_v2": 2,
               "rwkv_g2": 2, "w_o": 1, "w_gate": 2, "w_up": 2, "conv_f_w": 2, "w_down": 1, "w_ple_gate": 1, "w_ple_proj": 2}
_BIG_SHARDED = ("w_in",) + _BIG
_SMALL_SHARDED = tuple(k for k in _WEIGHTS if k in _SHARD_AXIS and k not in _BIG_SHARDED)
_REPLICATED = tuple(k for k in _WEIGHTS if k not in _SHARD_AXIS)
PACK_WIDTH = 512


def _to_shards(g, axis):
    n = g.shape[axis] // N_XY
    return jnp.moveaxis(g.reshape(g.shape[:axis] + (N_XY, n) + g.shape[axis + 1:]), axis, 0)


def _from_shards(s, axis):
    s = jnp.moveaxis(s, 0, axis)
    return s.reshape(s.shape[:axis] + (N_XY * s.shape[axis + 1],) + s.shape[axis + 2:])


def _pack(arrs, lead, width, row_mult):
    lead_shape = arrs[0].shape[:lead]
    flat = jnp.concatenate([a.reshape(lead_shape + (-1,)) for a in arrs], axis=-1)
    n = flat.shape[-1]
    total = _round_up(n, width * row_mult)
    flat = jnp.pad(flat, [(0, 0)] * lead + [(0, total - n)])
    return flat.reshape(lead_shape + (total // width, width))


def _unpack(packed, shapes):
    flat = packed.reshape(-1)
    out, o = [], 0
    for s in shapes:
        n = 1
        for dim in s:
            n *= dim
        out.append(flat[o:o + n].reshape(s))
        o += n
    return out


def _as2d(a):
    return a.reshape(-1, a.shape[-1])


def kernel(x, p, ln_mix, w_in, w_in_vres, mu_shift, mu_shift_vres, conv_a_w, conv_a_b, lru_wx, lru_bx, lru_wa, lru_ba, lru_lambda, lru_norm, rwkv_w0, rwkv_w2, rwkv_a0, rwkv_a2, rwkv_v0, rwkv_v2, rwkv_g2, rwkv_kk, rwkv_ka, rwkv_rk, rwkv_lnx_w, rwkv_lnx_b, w_o, ln_ffn, w_gate, w_up, conv_f_w, conv_f_b, w_down, ln_ple, w_ple_gate, w_ple_proj, ln_ple_post, ln_final, loss_target, m_ln_mix, m_w_in, m_w_in_vres, m_mu_shift, m_mu_shift_vres, m_conv_a_w, m_conv_a_b, m_lru_wx, m_lru_bx, m_lru_wa, m_lru_ba, m_lru_lambda, m_lru_norm, m_rwkv_w0, m_rwkv_w2, m_rwkv_a0, m_rwkv_a2, m_rwkv_v0, m_rwkv_v2, m_rwkv_g2, m_rwkv_kk, m_rwkv_ka, m_rwkv_rk, m_rwkv_lnx_w, m_rwkv_lnx_b, m_w_o, m_ln_ffn, m_w_gate, m_w_up, m_conv_f_w, m_conv_f_b, m_w_down, m_ln_ple, m_w_ple_gate, m_w_ple_proj, m_ln_ple_post, m_ln_final, v_ln_mix, v_w_in, v_w_in_vres, v_mu_shift, v_mu_shift_vres, v_conv_a_w, v_conv_a_b, v_lru_wx, v_lru_bx, v_lru_wa, v_lru_ba, v_lru_lambda, v_lru_norm, v_rwkv_w0, v_rwkv_w2, v_rwkv_a0, v_rwkv_a2, v_rwkv_v0, v_rwkv_v2, v_rwkv_g2, v_rwkv_kk, v_rwkv_ka, v_rwkv_rk, v_rwkv_lnx_w, v_rwkv_lnx_b, v_w_o, v_ln_ffn, v_w_gate, v_w_up, v_conv_f_w, v_conv_f_b, v_w_down, v_ln_ple, v_w_ple_gate, v_w_ple_proj, v_ln_ple_post, v_ln_final):
    a = dict(locals())
    x2, p, tgt = a["x"][0], a["p"][:, 0], a["loss_target"][0]
    pos = tuple(lax.axis_index(ax).astype(jnp.int32).reshape(1) for ax in ("c", "x", "y"))

    wf = {k: a[k] for k in _REPLICATED}
    small_shapes = [a[k].shape for k in _SMALL_SHARDED]
    shards = {k: [a[k][i].astype(BF16) for i in range(a[k].shape[0])] for k in _BIG_SHARDED}
    packed = _pack([a[k] for k in _SMALL_SHARDED], 0, PACK_WIDTH, 16)
    got_small, got_w_in = _run_jobs([_gather_blob(packed), _gather_blob(shards["w_in"][0])], name="ag_first")
    got_small = got_small.reshape((N_XY,) + packed.shape)
    pieces = [_unpack(got_small[q], small_shapes) for q in range(N_XY)]
    for j, k in enumerate(_SMALL_SHARDED):
        wf[k] = _from_shards(jnp.stack([pieces[q][j] for q in range(N_XY)], axis=0), _SHARD_AXIS[k])

    m = _make_dims(x2, p, wf)
    m.dw_shards = N_XY
    w = _prepare_weights(m, wf)
    feed = _WeightFeed(m, w, shards, wf["w_in_vres"].astype(BF16))
    feed.arrive([("w_in", 0)], [got_w_in])
    def reducer(i, g, which):
        names = ("w_in",) if which == "late" else _BIG
        full = {k: (g["wcat"][:, :m.din] if k == "w_in" else g[k]) for k in names}
        slabs = {k: full[k] if full[k].ndim == 3 else _to_shards(full[k], _SHARD_AXIS[k] - 1) for k in names}
        return _GradReduce(slabs, pos, f"{which}_{i}")

    loss_row, dx, grads, d_ln_final, reductions = _local_step(m, w, x2, p, tgt, feed, reducer)
    gfull = _unpack_grads(m, grads, d_ln_final, with_big=False)
    loss = lax.psum(loss_row[0, 0], ("x", "y", "c"))

    gs = _pack([_to_shards(gfull[k], _SHARD_AXIS[k]) for k in _SMALL_SHARDED], 1, PACK_WIDTH, 32)
    g_small = _GradReduce({"small": gs}, pos, "small").run()["small"]
    rep_shapes = [a[k].shape for k in _REPLICATED]
    g_rep = _all_reduce_small(_pack([gfull[k] for k in _REPLICATED], 0, LANES_V7X, 8), name="ar_replicated")

    last = reductions[0][1]
    gred, delta, new_m, new_v = {}, {}, {}, {}
    carried = {"w_gate": (last.send_jobs, last.after_send), "w_up": (last.exchange_jobs, last.after_exchange)}
    for k in _BIG + ("w_in",):
        which = 1 if k == "w_in" else 0
        per_layer = [reductions[i][which].run()[k] for i in range(m.nl)]
        gred[k] = jnp.stack(per_layer, axis=0).reshape(a[k].shape)
        make_jobs, after = carried.get(k, (lambda: (), None))
        res, job_results = _adamw(_as2d(a[k]), _as2d(gred[k]), _as2d(a["m_" + k]), _as2d(a["v_" + k]), name="adamw_" + k,
                                  jobs=make_jobs())
        if after is not None:
            after(job_results)
        delta[k], new_m[k], new_v[k] = (r.reshape(a[k].shape) for r in res)
    for names, shapes, g_packed, width, mult, tag in ((_SMALL_SHARDED, small_shapes, g_small, PACK_WIDTH, 32, "small"),
                                                      (_REPLICATED, rep_shapes, g_rep, LANES_V7X, 8, "replicated")):
        packs = [_pack([a[pre + k] for k in names], 0, width, mult) for pre in ("", "m_", "v_")]
        res, _ = _adamw(packs[0], g_packed, packs[1], packs[2], name="adamw_" + tag)
        for dst, r in zip((gred, delta, new_m, new_v), [g_packed] + list(res)):
            dst.update(zip(names, _unpack(r, shapes)))
    return (loss, dx[None], *[gred[k] for k in _WEIGHTS], *[delta[k] for k in _WEIGHTS],
            *[new_m[k] for k in _WEIGHTS], *[new_v[k] for k in _WEIGHTS])
```
